```python
import math
import jax
import jax.numpy as jnp
from jax import lax
import numpy as np

D_MODEL = 1024
BATCH = 8
SEQ = 2048
DEPTH = 1
DEC_BATCH = 128
DEC_SEQ = 4
PAST_LEN = 16384
PAGE_SIZE = 128

DN_HEADS = 8
DN_HEAD_DIM = 64
DN_WIDTH = DN_HEADS * DN_HEAD_DIM
QKV_DIM = 3 * DN_WIDTH
CONV_W = 4
DN_CHUNK = 64
S5_GROUP_CH = 16
S5_WIDTH = D_MODEL // 2
S5_GROUPS = S5_WIDTH // S5_GROUP_CH
S5_STATE = 64
IN_COLS = QKV_DIM + DN_WIDTH + 2 * DN_HEADS + S5_WIDTH + 2 * D_MODEL
MOE_GROUPS = 4
EXPERTS_PER_GROUP = 8
N_EXPERTS = MOE_GROUPS * EXPERTS_PER_GROUP
TOP_K = 2
EXPERT_FF = 256
RMS_EPS = 1e-6
L2_EPS = 1e-6

kernel_name = 'hybrid_gdn_s5_hmoe_step'


def _rmsnorm(x, w):
    xf = x.astype(jnp.float32)
    y = xf * lax.rsqrt(jnp.mean(xf * xf, axis=-1, keepdims=True) + RMS_EPS)
    return (y * w.astype(jnp.float32)).astype(x.dtype)


def _l2norm(x):
    xf = x.astype(jnp.float32)
    return xf * lax.rsqrt(jnp.sum(xf * xf, axis=-1, keepdims=True) + L2_EPS)


def _causal_conv(x, buf, w):
    t = x.shape[1]
    xp = jnp.concatenate([buf.astype(x.dtype), x], axis=1)
    out = sum(xp[:, i:i + t] * w[i] for i in range(CONV_W))
    return jax.nn.silu(out), xp[:, t:]


def _gated_delta(q, k, v, g, beta, s0):
    bsz, t, nh, dk = q.shape
    dv = v.shape[-1]
    c = min(DN_CHUNK, t)
    pad = (-t) % c
    if pad:
        pw = ((0, 0), (0, pad), (0, 0))
        q = jnp.pad(q, pw + ((0, 0),))
        k = jnp.pad(k, pw + ((0, 0),))
        v = jnp.pad(v, pw + ((0, 0),))
        g = jnp.pad(g, pw)
        beta = jnp.pad(beta, pw)
    n = (t + pad) // c

    def chunks(a):
        a = a.reshape((bsz, n, c, nh) + a.shape[3:])
        return jnp.moveaxis(a, 3, 1)

    q, k, v, g, beta = map(chunks, (q, k, v, g, beta))
    q = q * (dk ** -0.5)
    gc = jnp.cumsum(g, axis=-1)
    causal = jnp.tril(jnp.ones((c, c), dtype=bool))
    strict = jnp.tril(jnp.ones((c, c), dtype=bool), k=-1)
    diff = gc[..., :, None] - gc[..., None, :]
    decay = jnp.where(causal, jnp.exp(jnp.where(causal, diff, 0.0)), 0.0)
    kb = k * beta[..., None]
    a_mat = jnp.where(strict, jnp.einsum('bhncd,bhnsd->bhncs', kb, k) * decay, 0.0)
    eye = jnp.eye(c, dtype=jnp.float32)
    rhs = jnp.concatenate([v * beta[..., None], kb * jnp.exp(gc)[..., None]], axis=-1)
    sol = lax.linalg.triangular_solve(a_mat + eye, rhs, left_side=True, lower=True, unit_diagonal=True)
    u_val, w_k = sol[..., :dv], sol[..., dv:]
    qk = jnp.einsum('bhncd,bhnsd->bhncs', q, k) * decay
    q_dec = q * jnp.exp(gc)[..., None]
    k_dec = k * jnp.exp(gc[..., -1:] - gc)[..., None]
    d_last = jnp.exp(gc[..., -1])

    def step(s, inp):
        u_c, w_c, qk_c, qd_c, kd_c, dl_c = inp
        v_new = u_c - jnp.einsum('bhcd,bhde->bhce', w_c, s)
        o_c = jnp.einsum('bhcd,bhde->bhce', qd_c, s) + jnp.einsum('bhcs,bhse->bhce', qk_c, v_new)
        s = s * dl_c[..., None, None] + jnp.einsum('bhcd,bhce->bhde', kd_c, v_new)
        return s, o_c

    xs = tuple(jnp.moveaxis(a, 2, 0) for a in (u_val, w_k, qk, q_dec, k_dec, d_last))
    s_fin, o = lax.scan(step, s0, xs)
    o = jnp.transpose(o, (1, 0, 3, 2, 4)).reshape(bsz, n * c, nh, dv)[:, :t]
    return o, s_fin


def _s5(u, h0_re, h0_im, lam_re, lam_im, log_step, b_re, b_im, c_re, c_im, s5_d):
    bsz, t, _ = u.shape
    uf = u.astype(jnp.float32)
    ug = uf.reshape(bsz, t, S5_GROUPS, S5_GROUP_CH)
    lr, li = lam_re.astype(jnp.float32), lam_im.astype(jnp.float32)
    dt = jnp.exp(log_step.astype(jnp.float32))[:, None]
    mag = jnp.exp(lr * dt)
    ab_re, ab_im = mag * jnp.cos(li * dt), mag * jnp.sin(li * dt)
    den = lr * lr + li * li
    nr, ni = ab_re - 1.0, ab_im
    f_re = (nr * lr + ni * li) / den
    f_im = (ni * lr - nr * li) / den
    bu_re = jnp.einsum('btgc,gpc->btgp', ug, b_re.astype(jnp.float32))
    bu_im = jnp.einsum('btgc,gpc->btgp', ug, b_im.astype(jnp.float32))
    x_re = f_re * bu_re - f_im * bu_im
    x_im = f_re * bu_im + f_im * bu_re
    hr0, hi0 = h0_re.astype(jnp.float32), h0_im.astype(jnp.float32)
    x_re = x_re.at[:, 0].add(ab_re * hr0 - ab_im * hi0)
    x_im = x_im.at[:, 0].add(ab_re * hi0 + ab_im * hr0)
    a_re = jnp.broadcast_to(ab_re, x_re.shape)
    a_im = jnp.broadcast_to(ab_im, x_im.shape)

    def combine(e1, e2):
        a1r, a1i, b1r, b1i = e1
        a2r, a2i, b2r, b2i = e2
        return (a2r * a1r - a2i * a1i, a2r * a1i + a2i * a1r,
                a2r * b1r - a2i * b1i + b2r, a2r * b1i + a2i * b1r + b2i)

    _, _, h_re, h_im = lax.associative_scan(combine, (a_re, a_im, x_re, x_im), axis=1)
    y = (jnp.einsum('btgp,gcp->btgc', h_re, c_re.astype(jnp.float32))
         - jnp.einsum('btgp,gcp->btgc', h_im, c_im.astype(jnp.float32)))
    y = y.reshape(bsz, t, S5_WIDTH) + s5_d.astype(jnp.float32) * uf
    return y, h_re[:, -1], h_im[:, -1]


def _hier_moe(h, w_rc, w_rf, w_eu, w_ed):
    bsz, t, d = h.shape
    tok = h.reshape(bsz * t, d)
    pc = jax.nn.softmax((tok @ w_rc).astype(jnp.float32), axis=-1)
    g_sel = jnp.argmax(pc, axis=-1)
    p_sel = jnp.max(pc, axis=-1)
    fine = (tok @ w_rf).astype(jnp.float32).reshape(-1, MOE_GROUPS, EXPERTS_PER_GROUP)
    fine_sel = jnp.sum(fine * jax.nn.one_hot(g_sel, MOE_GROUPS, dtype=jnp.float32)[:, :, None], axis=1)
    pf = jax.nn.softmax(fine_sel, axis=-1)
    top_w, top_i = lax.top_k(pf, TOP_K)
    top_w = top_w / jnp.sum(top_w, axis=-1, keepdims=True) * p_sel[:, None]
    expert_idx = g_sel[:, None] * EXPERTS_PER_GROUP + top_i
    combine_w = jnp.sum(jax.nn.one_hot(expert_idx, N_EXPERTS, dtype=jnp.float32) * top_w[..., None], axis=1)
    combine_w = combine_w.astype(h.dtype)
    out = jnp.zeros_like(tok)
    for gi in range(MOE_GROUPS):
        sl = slice(gi * EXPERTS_PER_GROUP, (gi + 1) * EXPERTS_PER_GROUP)
        hu = jnp.einsum('nd,edf->nef', tok, w_eu[sl])
        gate, up = hu[..., :EXPERT_FF], hu[..., EXPERT_FF:]
        act = jax.nn.silu(gate) * up * combine_w[:, sl, None]
        out = out + jnp.einsum('nef,efd->nd', act, w_ed[sl])
    return out.reshape(bsz, t, d)


def _block(x, state, p):
    (norm_mix_w, w_in, conv_w, a_log, dt_bias, head_norm_w, w_a_up,
     lam_re, lam_im, log_step, b_re, b_im, c_re, c_im, s5_d, w_glu, w_b_up, w_o,
     norm_ffn_w, w_rc, w_rf, w_eu, w_ed) = p
    conv_buf, s_delta, h_re, h_im = state
    bsz, t, _ = x.shape
    h = _rmsnorm(x, norm_mix_w)
    proj = h @ w_in
    cuts = [int(i) for i in np.cumsum([QKV_DIM, DN_WIDTH, DN_HEADS, DN_HEADS, S5_WIDTH, D_MODEL])]
    qkv, z, a, b, u, gate_a, gate_b = jnp.split(proj, cuts, axis=-1)
    qkv, conv_new = _causal_conv(qkv, conv_buf, conv_w)
    q, k, v = jnp.split(qkv, 3, axis=-1)
    shp = (bsz, t, DN_HEADS, DN_HEAD_DIM)
    q = _l2norm(q.reshape(shp))
    k = _l2norm(k.reshape(shp))
    v = v.reshape(shp).astype(jnp.float32)
    g = -jnp.exp(a_log.astype(jnp.float32)) * jax.nn.softplus(a.astype(jnp.float32) + dt_bias.astype(jnp.float32))
    beta = jax.nn.sigmoid(b.astype(jnp.float32))
    o, s_new = _gated_delta(q, k, v, g, beta, s_delta.astype(jnp.float32))
    o = _rmsnorm(o, head_norm_w) * jax.nn.silu(z.reshape(shp).astype(jnp.float32))
    y_a = o.reshape(bsz, t, DN_WIDTH).astype(x.dtype) @ w_a_up
    ys, hr_new, hi_new = _s5(u, h_re, h_im, lam_re, lam_im, log_step, b_re, b_im, c_re, c_im, s5_d)
    ys = jax.nn.gelu(ys).astype(x.dtype)
    ys = ys * jax.nn.sigmoid(ys @ w_glu)
    y_b = ys @ w_b_up
    mixed = (jax.nn.sigmoid(gate_a) * y_a + jax.nn.sigmoid(gate_b) * y_b) @ w_o
    x = x + mixed
    x = x + _hier_moe(_rmsnorm(x, norm_ffn_w), w_rc, w_rf, w_eu, w_ed)
    return x, (conv_new, s_new, hr_new, hi_new)


def setup_inputs(seed: int = 0) -> dict:
    key = jax.random.key(seed)
    ks = jax.random.split(key, 48)
    cnt = [0]
    f32 = jnp.float32

    def nk():
        cnt[0] += 1
        return ks[cnt[0] - 1]

    def nrm(shape, scale):
        return jax.random.normal(nk(), shape, f32) * scale

    def unif(shape, lo, hi):
        return jax.random.uniform(nk(), shape, f32, lo, hi)

    L = DEPTH
    x_prompt = nrm((BATCH, SEQ, D_MODEL), 1.0)
    x_sample = nrm((DEC_BATCH, DEC_SEQ, D_MODEL), 1.0)
    state_conv = nrm((L, DEC_BATCH, CONV_W - 1, QKV_DIM), 1.0)
    state_delta = nrm((L, DEC_BATCH, DN_HEADS, DN_HEAD_DIM, DN_HEAD_DIM), 0.1)
    state_ssm_re = nrm((L, DEC_BATCH, S5_GROUPS, S5_STATE), 0.1)
    state_ssm_im = nrm((L, DEC_BATCH, S5_GROUPS, S5_STATE), 0.1)
    norm_mix_w = 1.0 + nrm((L, D_MODEL), 0.02)
    w_in = nrm((L, D_MODEL, IN_COLS), D_MODEL ** -0.5)
    conv_w = nrm((L, CONV_W, QKV_DIM), CONV_W ** -0.5)
    a_log = jnp.log(unif((L, DN_HEADS), 1.0, 16.0))
    dt0 = jnp.exp(unif((L, DN_HEADS), math.log(1e-3), math.log(1e-1)))
    dt_bias = dt0 + jnp.log(-jnp.expm1(-dt0))
    head_norm_w = 1.0 + nrm((L, DN_HEAD_DIM), 0.02)
    w_a_up = nrm((L, DN_WIDTH, D_MODEL), DN_WIDTH ** -0.5)
    s5_lambda_re = -0.5 + nrm((L, S5_GROUPS, S5_STATE), 0.01)
    s5_lambda_im = jnp.pi * jnp.arange(S5_STATE, dtype=f32) + nrm((L, S5_GROUPS, S5_STATE), 0.01)
    s5_log_step = unif((L, S5_GROUPS), math.log(1e-3), math.log(1e-1))
    s5_b_re = nrm((L, S5_GROUPS, S5_STATE, S5_GROUP_CH), (2 * S5_GROUP_CH) ** -0.5)
    s5_b_im = nrm((L, S5_GROUPS, S5_STATE, S5_GROUP_CH), (2 * S5_GROUP_CH) ** -0.5)
    s5_c_re = nrm((L, S5_GROUPS, S5_GROUP_CH, S5_STATE), S5_STATE ** -0.5)
    s5_c_im = nrm((L, S5_GROUPS, S5_GROUP_CH, S5_STATE), S5_STATE ** -0.5)
    s5_d = nrm((L, S5_WIDTH), 1.0)
    w_glu = nrm((L, S5_WIDTH, S5_WIDTH), S5_WIDTH ** -0.5)
    w_b_up = nrm((L, S5_WIDTH, D_MODEL), S5_WIDTH ** -0.5)
    w_o = nrm((L, D_MODEL, D_MODEL), D_MODEL ** -0.5)
    norm_ffn_w = 1.0 + nrm((L, D_MODEL), 0.02)
    w_router_coarse = nrm((L, D_MODEL, MOE_GROUPS), D_MODEL ** -0.5)
    w_router_fine = nrm((L, D_MODEL, MOE_GROUPS * EXPERTS_PER_GROUP), D_MODEL ** -0.5)
    w_expert_up = nrm((L, N_EXPERTS, D_MODEL, 2 * EXPERT_FF), D_MODEL ** -0.5)
    w_expert_down = nrm((L, N_EXPERTS, EXPERT_FF, D_MODEL), EXPERT_FF ** -0.5)
    norm_final_w = 1.0 + nrm((D_MODEL,), 0.02)
    return {'x_prompt': x_prompt, 'x_sample': x_sample, 'state_conv': state_conv,
            'state_delta': state_delta, 'state_ssm_re': state_ssm_re, 'state_ssm_im': state_ssm_im,
            'norm_mix_w': norm_mix_w, 'w_in': w_in, 'conv_w': conv_w, 'a_log': a_log,
            'dt_bias': dt_bias, 'head_norm_w': head_norm_w, 'w_a_up': w_a_up,
            's5_lambda_re': s5_lambda_re, 's5_lambda_im': s5_lambda_im, 's5_log_step': s5_log_step,
            's5_b_re': s5_b_re, 's5_b_im': s5_b_im, 's5_c_re': s5_c_re, 's5_c_im': s5_c_im,
            's5_d': s5_d, 'w_glu': w_glu, 'w_b_up': w_b_up, 'w_o': w_o, 'norm_ffn_w': norm_ffn_w,
            'w_router_coarse': w_router_coarse, 'w_router_fine': w_router_fine,
            'w_expert_up': w_expert_up, 'w_expert_down': w_expert_down, 'norm_final_w': norm_final_w}


def reference(x_prompt, x_sample, state_conv, state_delta, state_ssm_re, state_ssm_im,
              norm_mix_w, w_in, conv_w, a_log, dt_bias, head_norm_w, w_a_up,
              s5_lambda_re, s5_lambda_im, s5_log_step, s5_b_re, s5_b_im, s5_c_re, s5_c_im,
              s5_d, w_glu, w_b_up, w_o, norm_ffn_w, w_router_coarse, w_router_fine,
              w_expert_up, w_expert_down, norm_final_w):
    bp = x_prompt.shape[0]
    xp, xs = x_prompt, x_sample
    conv_p, delta_p, ssr_p, ssi_p = [], [], [], []
    conv_s, delta_s, ssr_s, ssi_s = [], [], [], []
    for l in range(DEPTH):
        params = (norm_mix_w[l], w_in[l], conv_w[l], a_log[l], dt_bias[l], head_norm_w[l], w_a_up[l],
                  s5_lambda_re[l], s5_lambda_im[l], s5_log_step[l], s5_b_re[l], s5_b_im[l],
                  s5_c_re[l], s5_c_im[l], s5_d[l], w_glu[l], w_b_up[l], w_o[l], norm_ffn_w[l],
                  w_router_coarse[l], w_router_fine[l], w_expert_up[l], w_expert_down[l])
        zero_state = (jnp.zeros((bp, CONV_W - 1, QKV_DIM), x_prompt.dtype),
                      jnp.zeros((bp, DN_HEADS, DN_HEAD_DIM, DN_HEAD_DIM), jnp.float32),
                      jnp.zeros((bp, S5_GROUPS, S5_STATE), jnp.float32),
                      jnp.zeros((bp, S5_GROUPS, S5_STATE), jnp.float32))
        xp, st_p = _block(xp, zero_state, params)
        xs, st_s = _block(xs, (state_conv[l], state_delta[l], state_ssm_re[l], state_ssm_im[l]), params)
        conv_p.append(st_p[0]); delta_p.append(st_p[1]); ssr_p.append(st_p[2]); ssi_p.append(st_p[3])
        conv_s.append(st_s[0]); delta_s.append(st_s[1]); ssr_s.append(st_s[2]); ssi_s.append(st_s[3])
    y_prompt = _rmsnorm(xp, norm_final_w)
    y_sample = _rmsnorm(xs, norm_final_w)
    return (y_prompt, y_sample,
            jnp.stack(conv_p), jnp.stack(delta_p), jnp.stack(ssr_p), jnp.stack(ssi_p),
            jnp.stack(conv_s), jnp.stack(delta_s), jnp.stack(ssr_s), jnp.stack(ssi_s))
```

```python
import functools
import math

import jax
import jax.numpy as jnp
import numpy as np
from jax import lax
from jax.experimental import pallas as pl
from jax.experimental.pallas import tpu as pltpu

F32 = jnp.float32
BF16 = jnp.bfloat16
I32 = jnp.int32

D_MODEL = 1024
DN_HEADS = 8
DN_HEAD_DIM = 64
DN_WIDTH = DN_HEADS * DN_HEAD_DIM
QKV_DIM = 3 * DN_WIDTH
CONV_W = 4
DN_CHUNK = 64
S5_GROUP_CH = 16
S5_WIDTH = D_MODEL // 2
S5_GROUPS = S5_WIDTH // S5_GROUP_CH
S5_STATE = 64
S5_FLAT = S5_GROUPS * S5_STATE
MOE_GROUPS = 4
EXPERTS_PER_GROUP = 8
N_EXPERTS = MOE_GROUPS * EXPERTS_PER_GROUP
TOP_K = 2
EXPERT_FF = 256
RMS_EPS = 1e-6
L2_EPS = 1e-6

LANES = 128
SUBLANES = 8
VMEM_LIMIT = 56 * 1024 * 1024

C_QKV, C_Z, C_U, C_GA, C_GB, C_AB = 0, 1536, 2048, 2560, 3584, 4608
IN_PACKED = C_AB + LANES

ROW_TILE = 512
MOE_TILE = 256
ROUTER_ROWS = 40


def _mm(a, b):
    return jnp.dot(a.astype(BF16), b.astype(BF16), preferred_element_type=F32)


def _mm_nt(a, b):
    return lax.dot_general(a.astype(BF16), b.astype(BF16), (((1,), (1,)), ((), ())),
                           preferred_element_type=F32)


def _split3_dot(a, b01):
    a1 = a.astype(BF16)
    r1 = a - a1.astype(F32)
    a2 = r1.astype(BF16)
    a3 = (r1 - a2.astype(F32)).astype(BF16)
    out = jnp.dot(a3, b01, preferred_element_type=F32)
    out = out + jnp.dot(a2, b01, preferred_element_type=F32)
    return out + jnp.dot(a1, b01, preferred_element_type=F32)


def _cparams(sem):
    return pltpu.CompilerParams(dimension_semantics=sem, vmem_limit_bytes=VMEM_LIMIT)


def _inproj_kernel(x_ref, nw_ref, w_ref, qkv_ref, z_ref, u_ref, ga_ref, gb_ref, ab_ref):
    x = x_ref[...]
    h = x * lax.rsqrt(jnp.mean(x * x, axis=-1, keepdims=True) + RMS_EPS) * nw_ref[...]
    hb = h.astype(BF16)

    def proj(lo, hi):
        return jnp.dot(hb, w_ref[:, lo:hi], preferred_element_type=F32)

    qkv_ref[...] = proj(C_QKV, C_Z)
    z_ref[...] = proj(C_Z, C_U).astype(z_ref.dtype)
    u_ref[...] = proj(C_U, C_GA).astype(u_ref.dtype)
    ga_ref[...] = proj(C_GA, C_GB).astype(ga_ref.dtype)
    gb_ref[...] = proj(C_GB, C_AB).astype(gb_ref.dtype)
    ab_ref[...] = proj(C_AB, IN_PACKED)


def _inproj(x2d, nw, wcat, nb):
    n = x2d.shape[0]
    t = n // nb
    tt = min(ROW_TILE, t)
    nt = t // tt
    row = lambda b, i: (b * nt + i, 0)
    const = lambda b, i: (0, 0)
    outs = pl.pallas_call(
        _inproj_kernel,
        grid=(nb, nt),
        in_specs=[pl.BlockSpec((tt, D_MODEL), row),
                  pl.BlockSpec((1, D_MODEL), const),
                  pl.BlockSpec((D_MODEL, IN_PACKED), const)],
        out_specs=[pl.BlockSpec((tt, QKV_DIM), row),
                   pl.BlockSpec((tt, DN_WIDTH), row),
                   pl.BlockSpec((tt, S5_WIDTH), lambda b, i: (i, b)),
                   pl.BlockSpec((tt, D_MODEL), row),
                   pl.BlockSpec((tt, D_MODEL), row),
                   pl.BlockSpec((tt, LANES), row)],
        out_shape=[jax.ShapeDtypeStruct((n, QKV_DIM), F32),
                   jax.ShapeDtypeStruct((n, DN_WIDTH), BF16),
                   jax.ShapeDtypeStruct((t, nb * S5_WIDTH), BF16),
                   jax.ShapeDtypeStruct((n, D_MODEL), BF16),
                   jax.ShapeDtypeStruct((n, D_MODEL), BF16),
                   jax.ShapeDtypeStruct((n, LANES), F32)],
        compiler_params=_cparams(("arbitrary", "arbitrary")),
        name="inproj",
    )(x2d, nw, wcat)
    qkv, z, u, ga, gb, ab = outs
    return qkv, z, u.reshape(t * nb, S5_WIDTH), ga, gb, ab


def _softplus(x):
    return jnp.maximum(x, 0.0) + jnp.log1p(jnp.exp(-jnp.abs(x)))


def _prep_kernel(qkv_ref, cinit_ref, cw_ref, ab_ref, gp_ref, seg_ref,
                 q_ref, k_ref, v_ref, gate_ref, cnew_ref, xp_ref, *, shift, rc, rows):
    @pl.when(pl.program_id(1) == 0)
    def _():
        xp_ref[0:rc, :] = cinit_ref[0]

    xp_ref[rc:rc + rows, :] = qkv_ref[...]
    acc = None
    for i in range(CONV_W):
        lo = rc + (i - (CONV_W - 1)) * shift
        term = xp_ref[lo:lo + rows, :] * cw_ref[i:i + 1, :]
        acc = term if acc is None else acc + term
    y = acc * jax.nn.sigmoid(acc)
    keep = (CONV_W - 1) * shift
    cnew_ref[0] = xp_ref[rc + rows - keep:rc + rows, :]
    xp_ref[0:rc, :] = xp_ref[rows:rows + rc, :]

    seg = seg_ref[...]
    q = y[:, 0:DN_WIDTH]
    k = y[:, DN_WIDTH:2 * DN_WIDTH]
    q_ref[...] = q * lax.rsqrt(jnp.dot((q * q).astype(BF16), seg, preferred_element_type=F32) + L2_EPS)
    k_ref[...] = k * lax.rsqrt(jnp.dot((k * k).astype(BF16), seg, preferred_element_type=F32) + L2_EPS)
    v_ref[...] = y[:, 2 * DN_WIDTH:]

    ab = ab_ref[...]
    g = -jnp.exp(gp_ref[0:1, :]) * _softplus(ab + gp_ref[1:2, :])
    beta = jax.nn.sigmoid(ab)
    lane = lax.broadcasted_iota(I32, ab.shape, 1)
    gate_ref[...] = jnp.where(lane < DN_HEADS, g, beta)


def _prep(qkv, cinit, conv_w, ab, gate_p, seg, nb, shift):
    n = qkv.shape[0]
    r = n // nb
    rows = min(ROW_TILE, r)
    nt = r // rows
    rc = cinit.shape[1]
    keep = (CONV_W - 1) * shift
    row = lambda b, i: (b * nt + i, 0)
    const = lambda b, i: (0, 0)
    kern = functools.partial(_prep_kernel, shift=shift, rc=rc, rows=rows)
    return pl.pallas_call(
        kern,
        grid=(nb, nt),
        in_specs=[pl.BlockSpec((rows, QKV_DIM), row),
                  pl.BlockSpec((1, rc, QKV_DIM), lambda b, i: (b, 0, 0)),
                  pl.BlockSpec((CONV_W, QKV_DIM), const),
                  pl.BlockSpec((rows, LANES), row),
                  pl.BlockSpec((2, LANES), const),
                  pl.BlockSpec((DN_WIDTH, DN_WIDTH), const)],
        out_specs=[pl.BlockSpec((rows, DN_WIDTH), row),
                   pl.BlockSpec((rows, DN_WIDTH), row),
                   pl.BlockSpec((rows, DN_WIDTH), row),
                   pl.BlockSpec((rows, LANES), row),
                   pl.BlockSpec((1, keep, QKV_DIM), lambda b, i: (b, 0, 0))],
        out_shape=[jax.ShapeDtypeStruct((n, DN_WIDTH), F32),
                   jax.ShapeDtypeStruct((n, DN_WIDTH), F32),
                   jax.ShapeDtypeStruct((n, DN_WIDTH), F32),
                   jax.ShapeDtypeStruct((n, LANES), F32),
                   jax.ShapeDtypeStruct((nb, keep, QKV_DIM), F32)],
        scratch_shapes=[pltpu.VMEM((rc + rows, QKV_DIM), F32)],
        compiler_params=_cparams(("arbitrary", "arbitrary")),
        name="prep",
    )(qkv, cinit, conv_w, ab, gate_p, seg)


def _delta_chunk_kernel(q_ref, k_ref, v_ref, gate_ref, tril_ref, o_ref, sfin_ref, s_ref):
    c = DN_CHUNK
    dk = DN_HEAD_DIM

    @pl.when(pl.program_id(1) == 0)
    def _():
        s_ref[...] = jnp.zeros_like(s_ref)

    gate = gate_ref[...]
    gc_all = _split3_dot_left(tril_ref[...], gate)
    gc_t = gc_all.T
    rowi = lax.broadcasted_iota(I32, (c, c), 0)
    coli = lax.broadcasted_iota(I32, (c, c), 1)
    causal = rowi >= coli
    strict = rowi > coli
    q_all = q_ref[...]
    k_all = k_ref[...]
    v_all = v_ref[...]
    outs = []
    for h in range(DN_HEADS):
        sl = slice(h * dk, (h + 1) * dk)
        qh = q_all[:, sl] * (dk ** -0.5)
        kh = k_all[:, sl]
        vh = v_all[:, sl]
        gcol = gc_all[:, h:h + 1]
        grow = gc_t[h:h + 1, :]
        beta = gate[:, DN_HEADS + h:DN_HEADS + h + 1]
        decay = jnp.where(causal, jnp.exp(jnp.where(causal, gcol - grow, 0.0)), 0.0)
        kb = kh * beta
        a_mat = jnp.where(strict, _mm_nt(kb, kh) * decay, 0.0)
        egc = jnp.exp(gcol)
        rhs = jnp.concatenate([vh * beta, kb * egc], axis=1)
        sol = rhs - _mm(a_mat, rhs)
        pw = a_mat
        for _ in range(int(math.log2(c)) - 1):
            pw = _mm(pw, pw)
            sol = sol + _mm(pw, sol)
        u_val = sol[:, :dk]
        w_k = sol[:, dk:]
        qk = jnp.where(causal, _mm_nt(qh, kh) * decay, 0.0)
        q_dec = qh * egc
        g_last = gc_all[c - 1:c, h:h + 1]
        k_dec = kh * jnp.exp(g_last - gcol)
        d_last = jnp.exp(g_last)
        s = s_ref[h]
        v_new = u_val - _mm(w_k, s)
        outs.append(_mm(q_dec, s) + _mm(qk, v_new))
        s_ref[h] = s * d_last + _mm(k_dec.T, v_new)
    o_ref[...] = jnp.concatenate(outs, axis=1)
    sfin_ref[0] = s_ref[...]


def _split3_dot_left(b01, a):
    a1 = a.astype(BF16)
    r1 = a - a1.astype(F32)
    a2 = r1.astype(BF16)
    a3 = (r1 - a2.astype(F32)).astype(BF16)
    out = jnp.dot(b01, a3, preferred_element_type=F32)
    out = out + jnp.dot(b01, a2, preferred_element_type=F32)
    return out + jnp.dot(b01, a1, preferred_element_type=F32)


def _delta_prompt(q, k, v, gate, nb):
    n = q.shape[0]
    t = n // nb
    c = DN_CHUNK
    nc = t // c
    row = lambda b, i: (b * nc + i, 0)
    tril = jnp.tril(jnp.ones((c, c), F32)).astype(BF16)
    return pl.pallas_call(
        _delta_chunk_kernel,
        grid=(nb, nc),
        in_specs=[pl.BlockSpec((c, DN_WIDTH), row),
                  pl.BlockSpec((c, DN_WIDTH), row),
                  pl.BlockSpec((c, DN_WIDTH), row),
                  pl.BlockSpec((c, LANES), row),
                  pl.BlockSpec((c, c), lambda b, i: (0, 0))],
        out_specs=[pl.BlockSpec((c, DN_WIDTH), row),
                   pl.BlockSpec((1, DN_HEADS, DN_HEAD_DIM, DN_HEAD_DIM), lambda b, i: (b, 0, 0, 0))],
        out_shape=[jax.ShapeDtypeStruct((n, DN_WIDTH), F32),
                   jax.ShapeDtypeStruct((nb, DN_HEADS, DN_HEAD_DIM, DN_HEAD_DIM), F32)],
        scratch_shapes=[pltpu.VMEM((DN_HEADS, DN_HEAD_DIM, DN_HEAD_DIM), F32)],
        compiler_params=_cparams(("arbitrary", "arbitrary")),
        name="delta_prompt",
    )(q, k, v, gate, tril)


def _delta_step_kernel(q_ref, k_ref, v_ref, gate_ref, s0_ref, ex_ref, o_ref, s_ref, kx_ref, qx_ref, *, nt, nb):
    dk = DN_HEAD_DIM
    flat = dk * dk
    nv = flat // LANES
    p = pl.program_id(0)
    lane = lax.broadcasted_iota(I32, (SUBLANES, LANES), 1)
    low = lane < dk
    ex = ex_ref[...]

    for t in range(nt):
        rs = slice(t * nb, (t + 1) * nb)
        kx_ref[...] = jnp.dot(k_ref[rs, :].astype(BF16), ex, preferred_element_type=F32)
        qx_ref[...] = jnp.dot((q_ref[rs, :] * (dk ** -0.5)).astype(BF16), ex, preferred_element_type=F32)
        src_ref = s0_ref if t == 0 else s_ref

        def tile_body(bt, carry, t=t, src_ref=src_ref):
            b0 = pl.multiple_of(bt * SUBLANES, SUBLANES)
            r0 = pl.multiple_of(t * nb + b0, SUBLANES)
            gate = gate_ref[pl.ds(r0, SUBLANES), :]
            vv = v_ref[pl.ds(r0, SUBLANES), :]
            vsw = pltpu.roll(vv, dk, axis=1)
            o_pair = None
            for j in range(2):
                c0 = j * flat
                g = jnp.sum(jnp.where(lane == 2 * p + j, gate, 0.0), axis=1, keepdims=True)
                beta = jnp.sum(jnp.where(lane == 2 * p + j + DN_HEADS, gate, 0.0), axis=1, keepdims=True)
                a = jnp.exp(g)
                vdup = jnp.where(low, vv, vsw) if j == 0 else jnp.where(low, vsw, vv)
                s = [src_ref[pl.ds(b0, SUBLANES), c0 + i * LANES:c0 + (i + 1) * LANES] for i in range(nv)]
                kx = [kx_ref[pl.ds(b0, SUBLANES), c0 + i * LANES:c0 + (i + 1) * LANES] for i in range(nv)]
                ks = kx[0] * s[0]
                for i in range(1, nv):
                    ks = ks + kx[i] * s[i]
                ks = ks + pltpu.roll(ks, dk, axis=1)
                delta = beta * (vdup - a * ks)
                oh = None
                for i in range(nv):
                    si = a * s[i] + kx[i] * delta
                    s_ref[pl.ds(b0, SUBLANES), c0 + i * LANES:c0 + (i + 1) * LANES] = si
                    term = qx_ref[pl.ds(b0, SUBLANES), c0 + i * LANES:c0 + (i + 1) * LANES] * si
                    oh = term if oh is None else oh + term
                oh = oh + pltpu.roll(oh, dk, axis=1)
                o_pair = oh if j == 0 else jnp.where(low, o_pair, oh)
            o_ref[pl.ds(r0, SUBLANES), :] = o_pair
            return carry

        lax.fori_loop(0, nb // SUBLANES, tile_body, 0)


def _delta_sample(q, k, v, gate, s0, nb, nt):
    dk = DN_HEAD_DIM
    flat = dk * dk
    n = nt * nb
    col = np.arange(2 * flat)
    ex = np.arange(LANES)[:, None] == ((col // flat) * dk + (col % flat) // dk)[None, :]
    ex = jnp.asarray(ex, BF16)
    kern = functools.partial(_delta_step_kernel, nt=nt, nb=nb)
    pair = lambda p: (0, p)
    return pl.pallas_call(
        kern,
        grid=(DN_HEADS // 2,),
        in_specs=[pl.BlockSpec((n, LANES), pair),
                  pl.BlockSpec((n, LANES), pair),
                  pl.BlockSpec((n, LANES), pair),
                  pl.BlockSpec((n, LANES), lambda p: (0, 0)),
                  pl.BlockSpec((nb, 2 * flat), pair),
                  pl.BlockSpec((LANES, 2 * flat), lambda p: (0, 0))],
        out_specs=[pl.BlockSpec((n, LANES), pair),
                   pl.BlockSpec((nb, 2 * flat), pair)],
        out_shape=[jax.ShapeDtypeStruct((n, DN_WIDTH), F32),
                   jax.ShapeDtypeStruct((nb, DN_HEADS * flat), F32)],
        scratch_shapes=[pltpu.VMEM((nb, 2 * flat), F32),
                        pltpu.VMEM((nb, 2 * flat), F32)],
        compiler_params=_cparams(("arbitrary",)),
        name="delta_sample",
    )(q, k, v, gate, s0, ex)


def _s5_kernel(u_ref, bre_ref, bim_ref, lam_ref, c_ref, d_ref, h0_ref, y_ref, hfin_ref,
               bw_ref, ab_ref, x_ref, h_ref, *, nb, tt):
    p2 = S5_FLAT

    @pl.when(pl.program_id(0) == 0)
    def _():
        lr = lam_ref[0:1, :]
        li = lam_ref[1:2, :]
        dt = jnp.exp(lam_ref[2:3, :])
        mag = jnp.exp(lr * dt)
        ab_re = mag * jnp.cos(li * dt)
        ab_im = mag * jnp.sin(li * dt)
        den = lr * lr + li * li
        nr = ab_re - 1.0
        ni = ab_im
        f_re = (nr * lr + ni * li) / den
        f_im = (ni * lr - nr * li) / den
        ab_ref[0:1, :] = ab_re
        ab_ref[1:2, :] = ab_im
        bre = bre_ref[...]
        bim = bim_ref[...]
        bw_ref[:, 0:p2] = (bre * f_re - bim * f_im).astype(BF16)
        bw_ref[:, p2:2 * p2] = (bim * f_re + bre * f_im).astype(BF16)
        h_ref[...] = h0_ref[...]

    u = u_ref[...]
    x_ref[...] = jnp.dot(u, bw_ref[...], preferred_element_type=F32)
    a_re = ab_ref[0:1, :]
    a_im = ab_ref[1:2, :]

    if nb == SUBLANES:
        are = jnp.broadcast_to(a_re, (nb, p2))
        aim = jnp.broadcast_to(a_im, (nb, p2))

        def step(t, carry):
            hr, hi = carry
            r0 = pl.multiple_of(t * nb, nb)
            xr = x_ref[pl.ds(r0, nb), 0:p2]
            xi = x_ref[pl.ds(r0, nb), p2:2 * p2]
            nr = are * hr - aim * hi + xr
            ni = are * hi + aim * hr + xi
            x_ref[pl.ds(r0, nb), 0:p2] = nr
            x_ref[pl.ds(r0, nb), p2:2 * p2] = ni
            return nr, ni

        hr, hi = lax.fori_loop(0, tt, step, (h_ref[:, 0:p2], h_ref[:, p2:2 * p2]))
        h_ref[:, 0:p2] = hr
        h_ref[:, p2:2 * p2] = hi
    else:
        for t in range(tt):
            rs = slice(t * nb, (t + 1) * nb)
            hr = h_ref[:, 0:p2]
            hi = h_ref[:, p2:2 * p2]
            nr = a_re * hr - a_im * hi + x_ref[rs, 0:p2]
            ni = a_re * hi + a_im * hr + x_ref[rs, p2:2 * p2]
            h_ref[:, 0:p2] = nr
            h_ref[:, p2:2 * p2] = ni
            x_ref[rs, 0:p2] = nr
            x_ref[rs, p2:2 * p2] = ni

    y = jnp.dot(x_ref[...].astype(BF16), c_ref[...], preferred_element_type=F32)
    y_ref[...] = y + d_ref[...] * u.astype(F32)
    hfin_ref[...] = h_ref[...]


def _s5(u, bre, bim, lam, cmat, dvec, h0, nb):
    n = u.shape[0]
    t = n // nb
    tt = min(ROW_TILE // nb, t)
    rows = tt * nb
    const = lambda i: (0, 0)
    kern = functools.partial(_s5_kernel, nb=nb, tt=tt)
    return pl.pallas_call(
        kern,
        grid=(t // tt,),
        in_specs=[pl.BlockSpec((rows, S5_WIDTH), lambda i: (i, 0)),
                  pl.BlockSpec((S5_WIDTH, S5_FLAT), const),
                  pl.BlockSpec((S5_WIDTH, S5_FLAT), const),
                  pl.BlockSpec((SUBLANES, S5_FLAT), const),
                  pl.BlockSpec((2 * S5_FLAT, S5_WIDTH), const),
                  pl.BlockSpec((1, S5_WIDTH), const),
                  pl.BlockSpec((nb, 2 * S5_FLAT), const)],
        out_specs=[pl.BlockSpec((rows, S5_WIDTH), lambda i: (i, 0)),
                   pl.BlockSpec((nb, 2 * S5_FLAT), const)],
        out_shape=[jax.ShapeDtypeStruct((n, S5_WIDTH), F32),
                   jax.ShapeDtypeStruct((nb, 2 * S5_FLAT), F32)],
        scratch_shapes=[pltpu.VMEM((S5_WIDTH, 2 * S5_FLAT), BF16),
                        pltpu.VMEM((SUBLANES, S5_FLAT), F32),
                        pltpu.VMEM((rows, 2 * S5_FLAT), F32),
                        pltpu.VMEM((nb, 2 * S5_FLAT), F32)],
        compiler_params=_cparams(("arbitrary",)),
        name="s5",
    )(u, bre, bim, lam, cmat, dvec, h0)


def _postmix_kernel(xp_ref, op_ref, zp_ref, ysp_ref, gap_ref, gbp_ref,
                    xs_ref, os_ref, zs_ref, yss_ref, gas_ref, gbs_ref, *rest, nblk_p):
    @pl.when(pl.program_id(0) < nblk_p)
    def _():
        _postmix_body(xp_ref, op_ref, zp_ref, ysp_ref, gap_ref, gbp_ref, *rest)

    @pl.when(pl.program_id(0) >= nblk_p)
    def _():
        _postmix_body(xs_ref, os_ref, zs_ref, yss_ref, gas_ref, gbs_ref, *rest)


def _postmix_body(x_ref, o_ref, z_ref, ys_ref, ga_ref, gb_ref, hw_ref, seg_ref, wa_ref, wglu_ref, wb_ref,
                  wo_ref, nf_ref, wr_ref, x1_ref, hn_ref, ridx_ref, rw_ref):
    o = o_ref[...]
    ms = jnp.dot((o * o).astype(BF16), seg_ref[...], preferred_element_type=F32) * (1.0 / DN_HEAD_DIM)
    on = o * lax.rsqrt(ms + RMS_EPS) * hw_ref[...]
    z = z_ref[...].astype(F32)
    oa = on * (z * jax.nn.sigmoid(z))
    y_a = _mm(oa, wa_ref[...])
    ys = jax.nn.gelu(ys_ref[...])
    ys = ys * jax.nn.sigmoid(_mm(ys, wglu_ref[...]))
    y_b = _mm(ys, wb_ref[...])
    mixed = jax.nn.sigmoid(ga_ref[...].astype(F32)) * y_a + jax.nn.sigmoid(gb_ref[...].astype(F32)) * y_b
    x1 = x_ref[...] + _mm(mixed, wo_ref[...])
    x1_ref[...] = x1
    hn = x1 * lax.rsqrt(jnp.mean(x1 * x1, axis=-1, keepdims=True) + RMS_EPS) * nf_ref[...]
    hn_ref[...] = hn

    logits = lax.dot_general(wr_ref[...], hn, (((1,), (1,)), ((), ())),
                             precision=lax.Precision.HIGHEST, preferred_element_type=F32)
    coarse = logits[N_EXPERTS:N_EXPERTS + MOE_GROUPS, :]
    cm = jnp.max(coarse, axis=0, keepdims=True)
    ce = jnp.exp(coarse - cm)
    pc = ce / jnp.sum(ce, axis=0, keepdims=True)
    p_sel = jnp.max(pc, axis=0, keepdims=True)
    gi = lax.broadcasted_iota(I32, pc.shape, 0)
    g_sel = jnp.min(jnp.where(pc == p_sel, gi, MOE_GROUPS), axis=0, keepdims=True)
    fine = jnp.zeros((EXPERTS_PER_GROUP, logits.shape[1]), F32)
    for g in range(MOE_GROUPS):
        fine = fine + jnp.where(g_sel == g, logits[g * EXPERTS_PER_GROUP:(g + 1) * EXPERTS_PER_GROUP, :], 0.0)
    fm = jnp.max(fine, axis=0, keepdims=True)
    fe = jnp.exp(fine - fm)
    pf = fe / jnp.sum(fe, axis=0, keepdims=True)
    ei = lax.broadcasted_iota(I32, pf.shape, 0)
    v1 = jnp.max(pf, axis=0, keepdims=True)
    i1 = jnp.min(jnp.where(pf == v1, ei, EXPERTS_PER_GROUP), axis=0, keepdims=True)
    rest = jnp.where(ei == i1, -1.0, pf)
    v2 = jnp.max(rest, axis=0, keepdims=True)
    i2 = jnp.min(jnp.where(rest == v2, ei, EXPERTS_PER_GROUP), axis=0, keepdims=True)
    tot = v1 + v2
    ridx_ref[0:1, :] = g_sel * EXPERTS_PER_GROUP + i1
    ridx_ref[1:2, :] = g_sel * EXPERTS_PER_GROUP + i2
    rw_ref[0:1, :] = v1 / tot * p_sel
    rw_ref[1:2, :] = v2 / tot * p_sel


def _postmix(prompt, sample, weights, nb):
    n_p = prompt[0].shape[0]
    n_s = sample[0].shape[0]
    t = n_p // nb
    tt = min(ROW_TILE, t, n_s)
    nt = t // tt
    nblk_p = n_p // tt
    nblk = nblk_p + n_s // tt
    n_total = n_p + n_s
    prow = lambda i: (jnp.minimum(i, nblk_p - 1), 0)
    pys = lambda i: (jnp.minimum(i, nblk_p - 1) % nt, jnp.minimum(i, nblk_p - 1) // nt)
    srow = lambda i: (jnp.maximum(i - nblk_p, 0), 0)
    const = lambda i: (0, 0)

    def stream_specs(row, ysmap):
        return [pl.BlockSpec((tt, D_MODEL), row),
                pl.BlockSpec((tt, DN_WIDTH), row),
                pl.BlockSpec((tt, DN_WIDTH), row),
                pl.BlockSpec((tt, S5_WIDTH), ysmap),
                pl.BlockSpec((tt, D_MODEL), row),
                pl.BlockSpec((tt, D_MODEL), row)]

    weight_specs = [pl.BlockSpec((1, DN_WIDTH), const),
                    pl.BlockSpec((DN_WIDTH, DN_WIDTH), const),
                    pl.BlockSpec((DN_WIDTH, D_MODEL), const),
                    pl.BlockSpec((S5_WIDTH, S5_WIDTH), const),
                    pl.BlockSpec((S5_WIDTH, D_MODEL), const),
                    pl.BlockSpec((D_MODEL, D_MODEL), const),
                    pl.BlockSpec((1, D_MODEL), const),
                    pl.BlockSpec((ROUTER_ROWS, D_MODEL), const)]
    xp, op, zp, ysp, gap, gbp = prompt
    return pl.pallas_call(
        functools.partial(_postmix_kernel, nblk_p=nblk_p),
        grid=(nblk,),
        in_specs=stream_specs(prow, pys) + stream_specs(srow, srow) + weight_specs,
        out_specs=[pl.BlockSpec((tt, D_MODEL), lambda i: (i, 0)),
                   pl.BlockSpec((tt, D_MODEL), lambda i: (i, 0)),
                   pl.BlockSpec((TOP_K, tt), lambda i: (0, i)),
                   pl.BlockSpec((TOP_K, tt), lambda i: (0, i))],
        out_shape=[jax.ShapeDtypeStruct((n_total, D_MODEL), F32),
                   jax.ShapeDtypeStruct((n_total, D_MODEL), F32),
                   jax.ShapeDtypeStruct((TOP_K, n_total), I32),
                   jax.ShapeDtypeStruct((TOP_K, n_total), F32)],
        compiler_params=_cparams(("arbitrary",)),
        name="postmix",
    )(xp, op, zp, ysp.reshape(t, nb * S5_WIDTH), gap, gbp, *sample, *weights)


def _moe_kernel(texp_ref, tvalid_ref, rtok_ref, rdst_ref, hn_hbm, wu_ref, wd_ref, ytok_hbm,
                xbuf, ybuf, gsem, ssem, *, n_assign):
    i = pl.program_id(0)
    nsteps = pl.num_programs(0)
    tm = MOE_TILE
    slot = lax.rem(i, 2)

    def gather_row(step, sl, r):
        tok = rtok_ref[step * tm + r]
        return pltpu.make_async_copy(hn_hbm.at[pl.ds(tok, 1), :], xbuf.at[sl, pl.ds(r, 1), :], gsem.at[sl])

    def scatter_row(step, sl, r):
        dst = rdst_ref[step * tm + r]
        return pltpu.make_async_copy(ybuf.at[sl, pl.ds(r, 1), :], ytok_hbm.at[pl.ds(dst, 1), :], ssem.at[sl])

    def start_gather(step, sl):
        def body(r, c):
            gather_row(step, sl, r).start()
            return c
        lax.fori_loop(0, tm, body, 0)

    def wait_gather(step, sl):
        def body(r, c):
            gather_row(step, sl, r).wait()
            return c
        lax.fori_loop(0, tm, body, 0)

    def start_scatter(step, sl):
        def body(r, c):
            @pl.when(rdst_ref[step * tm + r] < n_assign)
            def _():
                scatter_row(step, sl, r).start()
            return c
        lax.fori_loop(0, tm, body, 0)

    def wait_scatter(step, sl):
        def body(r, c):
            @pl.when(rdst_ref[step * tm + r] < n_assign)
            def _():
                scatter_row(step, sl, r).wait()
            return c
        lax.fori_loop(0, tm, body, 0)

    @pl.when(i == 0)
    def _():
        @pl.when(tvalid_ref[0] == 1)
        def _():
            start_gather(0, 0)

    @pl.when(jnp.logical_and(i + 1 < nsteps, tvalid_ref[jnp.minimum(i + 1, nsteps - 1)] == 1))
    def _():
        start_gather(i + 1, 1 - slot)

    @pl.when(jnp.logical_and(i >= 2, tvalid_ref[jnp.maximum(i - 2, 0)] == 1))
    def _():
        wait_scatter(i - 2, slot)

    @pl.when(tvalid_ref[i] == 1)
    def _():
        wait_gather(i, slot)
        x = xbuf[slot].astype(BF16)
        hu = jnp.dot(x, wu_ref[0].astype(BF16), preferred_element_type=F32)
        gate = hu[:, :EXPERT_FF]
        up = hu[:, EXPERT_FF:]
        act = gate * jax.nn.sigmoid(gate) * up
        ybuf[slot] = jnp.dot(act.astype(BF16), wd_ref[0].astype(BF16), preferred_element_type=F32)
        start_scatter(i, slot)

    @pl.when(i == nsteps - 1)
    def _():
        @pl.when(jnp.logical_and(i >= 1, tvalid_ref[jnp.maximum(i - 1, 0)] == 1))
        def _():
            wait_scatter(i - 1, 1 - slot)

        @pl.when(tvalid_ref[i] == 1)
        def _():
            wait_scatter(i, slot)


def _moe(hn, w_up, w_down, texp, tvalid, rtok, rdst, n_assign):
    ntiles = texp.shape[0]
    grid_spec = pltpu.PrefetchScalarGridSpec(
        num_scalar_prefetch=4,
        grid=(ntiles,),
        in_specs=[pl.BlockSpec(memory_space=pl.ANY),
                  pl.BlockSpec((1, D_MODEL, 2 * EXPERT_FF), lambda i, te, tv, rt, rd: (te[i], 0, 0)),
                  pl.BlockSpec((1, EXPERT_FF, D_MODEL), lambda i, te, tv, rt, rd: (te[i], 0, 0))],
        out_specs=pl.BlockSpec(memory_space=pl.ANY),
        scratch_shapes=[pltpu.VMEM((2, MOE_TILE, D_MODEL), F32),
                        pltpu.VMEM((2, MOE_TILE, D_MODEL), F32),
                        pltpu.SemaphoreType.DMA((2,)),
                        pltpu.SemaphoreType.DMA((2,))])
    return pl.pallas_call(
        functools.partial(_moe_kernel, n_assign=n_assign),
        grid_spec=grid_spec,
        out_shape=jax.ShapeDtypeStruct((n_assign, D_MODEL), F32),
        compiler_params=pltpu.CompilerParams(dimension_semantics=("arbitrary",), vmem_limit_bytes=VMEM_LIMIT,
                                             has_side_effects=True),
        name="moe",
    )(texp, tvalid, rtok, rdst, hn, w_up, w_down)


def _combine_kernel(x1_ref, y0_ref, y1_ref, w_ref, nw_ref, out_ref):
    w = w_ref[...]
    x = x1_ref[...] + w[:, 0:1] * y0_ref[...] + w[:, 1:2] * y1_ref[...]
    out_ref[...] = x * lax.rsqrt(jnp.mean(x * x, axis=-1, keepdims=True) + RMS_EPS) * nw_ref[...]


def _combine(x1, ytok, wtok, nw):
    n = x1.shape[0]
    tt = math.gcd(n, ROW_TILE)
    return pl.pallas_call(
        _combine_kernel,
        grid=(n // tt,),
        in_specs=[pl.BlockSpec((tt, D_MODEL), lambda i: (i, 0)),
                  pl.BlockSpec((tt, D_MODEL), lambda i: (i, 0)),
                  pl.BlockSpec((tt, D_MODEL), lambda i: (i, 1)),
                  pl.BlockSpec((tt, TOP_K), lambda i: (i, 0)),
                  pl.BlockSpec((1, D_MODEL), lambda i: (0, 0))],
        out_specs=pl.BlockSpec((tt, D_MODEL), lambda i: (i, 0)),
        out_shape=jax.ShapeDtypeStruct((n, D_MODEL), F32),
        compiler_params=_cparams(("arbitrary",)),
        name="combine",
    )(x1, ytok, ytok, wtok, nw)


def _route_plan(ridx, n_tok):
    tm = MOE_TILE
    n_assign = n_tok * TOP_K
    ntiles = n_assign // tm + N_EXPERTS
    e_flat = ridx.T.reshape(n_assign)
    order = jnp.argsort(e_flat, stable=True).astype(I32)
    counts = jnp.sum((e_flat[:, None] == jnp.arange(N_EXPERTS, dtype=I32)[None, :]).astype(I32), axis=0)
    cstart = jnp.cumsum(counts) - counts
    tiles_e = (counts + tm - 1) // tm
    tend = jnp.cumsum(tiles_e)
    tstart = tend - tiles_e
    tile_id = jnp.arange(ntiles, dtype=I32)
    texp = jnp.minimum(jnp.sum((tile_id[:, None] >= tend[None, :]).astype(I32), axis=1), N_EXPERTS - 1)
    tvalid = (tile_id < tend[-1]).astype(I32)
    rows = jnp.arange(ntiles * tm, dtype=I32)
    rexp = texp[rows // tm]
    rank = rows - tstart[rexp] * tm
    ok = jnp.logical_and(rank < counts[rexp], tvalid[rows // tm] == 1)
    src = jnp.clip(cstart[rexp] + rank, 0, n_assign - 1)
    assign = order[src]
    rtok = jnp.where(ok, assign // TOP_K, 0).astype(I32)
    rdst = jnp.where(ok, assign, n_assign).astype(I32)
    return texp.astype(I32), tvalid, rtok, rdst


def _block_diag(m):
    g, a, b = m.shape
    eye = jnp.eye(g, dtype=m.dtype)
    return (eye[:, None, :, None] * m[:, :, None, :]).reshape(g * a, g * b)


def kernel(x_prompt, x_sample, state_conv, state_delta, state_ssm_re, state_ssm_im, norm_mix_w, w_in, conv_w, a_log, dt_bias, head_norm_w, w_a_up, s5_lambda_re, s5_lambda_im, s5_log_step, s5_b_re, s5_b_im, s5_c_re, s5_c_im, s5_d, w_glu, w_b_up, w_o, norm_ffn_w, w_router_coarse, w_router_fine, w_expert_up, w_expert_down, norm_final_w):
    bp, tp, _ = x_prompt.shape
    bs, ts, _ = x_sample.shape
    n_p = bp * tp
    n_s = bs * ts
    n_tok = n_p + n_s
    l = 0

    w = w_in[l]
    cuts = np.cumsum([0, QKV_DIM, DN_WIDTH, DN_HEADS, DN_HEADS, S5_WIDTH, D_MODEL, D_MODEL])
    w_qkv, w_z, w_a, w_b, w_u, w_ga, w_gb = [w[:, cuts[i]:cuts[i + 1]] for i in range(7)]
    w_ab = jnp.concatenate([w_a, w_b, jnp.zeros((D_MODEL, LANES - 2 * DN_HEADS), F32)], axis=1)
    wcat = jnp.concatenate([w_qkv, w_z, w_u, w_ga, w_gb, w_ab], axis=1).astype(BF16)
    nw_mix = norm_mix_w[l].reshape(1, D_MODEL)
    pad8 = lambda v: jnp.concatenate([v, jnp.zeros((LANES - DN_HEADS,), F32)]).reshape(1, LANES)
    gate_p = jnp.concatenate([pad8(a_log[l]), pad8(dt_bias[l])], axis=0)
    seg = _block_diag(jnp.ones((DN_HEADS, DN_HEAD_DIM, DN_HEAD_DIM), BF16))
    bre = _block_diag(jnp.swapaxes(s5_b_re[l], 1, 2))
    bim = _block_diag(jnp.swapaxes(s5_b_im[l], 1, 2))
    lam = jnp.concatenate([s5_lambda_re[l].reshape(1, S5_FLAT), s5_lambda_im[l].reshape(1, S5_FLAT),
                           jnp.repeat(s5_log_step[l], S5_STATE).reshape(1, S5_FLAT),
                           jnp.zeros((SUBLANES - 3, S5_FLAT), F32)], axis=0)
    cmat = jnp.concatenate([_block_diag(jnp.swapaxes(s5_c_re[l], 1, 2)),
                            -_block_diag(jnp.swapaxes(s5_c_im[l], 1, 2))], axis=0).astype(BF16)
    dvec = s5_d[l].reshape(1, S5_WIDTH)
    hw = jnp.tile(head_norm_w[l], DN_HEADS).reshape(1, DN_WIDTH)
    wr = jnp.concatenate([w_router_fine[l].T, w_router_coarse[l].T,
                          jnp.zeros((ROUTER_ROWS - N_EXPERTS - MOE_GROUPS, D_MODEL), F32)], axis=0)
    pm_weights = (hw, seg, w_a_up[l].astype(BF16), w_glu[l].astype(BF16), w_b_up[l].astype(BF16),
                  w_o[l].astype(BF16), norm_ffn_w[l].reshape(1, D_MODEL), wr)

    xp2 = x_prompt.reshape(n_p, D_MODEL)
    qkv_p, z_p, u_p, ga_p, gb_p, ab_p = _inproj(xp2, nw_mix, wcat, bp)
    q_p, k_p, v_p, gates_p, conv_p = _prep(qkv_p, jnp.zeros((bp, SUBLANES, QKV_DIM), F32), conv_w[l], ab_p,
                                              gate_p, seg, bp, 1)
    o_p, delta_p = _delta_prompt(q_p, k_p, v_p, gates_p, bp)
    ys_p, h_p = _s5(u_p, bre, bim, lam, cmat, dvec, jnp.zeros((bp, 2 * S5_FLAT), F32), bp)

    xs2 = jnp.swapaxes(x_sample, 0, 1).reshape(n_s, D_MODEL)
    qkv_s, z_s, u_s, ga_s, gb_s, ab_s = _inproj(xs2, nw_mix, wcat, 1)
    cinit_s = jnp.swapaxes(state_conv[l], 0, 1).reshape(1, (CONV_W - 1) * bs, QKV_DIM)
    q_s, k_s, v_s, gate_s, conv_s = _prep(qkv_s, cinit_s, conv_w[l], ab_s, gate_p, seg, 1, bs)
    s0 = state_delta[l].reshape(bs, DN_HEADS * DN_HEAD_DIM * DN_HEAD_DIM)
    o_s, delta_s = _delta_sample(q_s, k_s, v_s, gate_s, s0, bs, ts)
    h0_s = jnp.concatenate([state_ssm_re[l].reshape(bs, S5_FLAT), state_ssm_im[l].reshape(bs, S5_FLAT)], axis=1)
    ys_s, h_s = _s5(u_s, bre, bim, lam, cmat, dvec, h0_s, bs)
    x1, hn, ridx, rw = _postmix((xp2, o_p, z_p, ys_p, ga_p, gb_p), (xs2, o_s, z_s, ys_s, ga_s, gb_s),
                                pm_weights, bp)

    texp, tvalid, rtok, rdst = _route_plan(ridx, n_tok)
    ytok = _moe(hn, w_expert_up[l], w_expert_down[l], texp, tvalid, rtok, rdst, n_tok * TOP_K)
    y_all = _combine(x1, ytok.reshape(n_tok, TOP_K * D_MODEL), rw.T, norm_final_w.reshape(1, D_MODEL))

    y_prompt = y_all[:n_p].reshape(bp, tp, D_MODEL)
    y_sample = jnp.swapaxes(y_all[n_p:].reshape(ts, bs, D_MODEL), 0, 1)
    conv_sample = jnp.swapaxes(conv_s.reshape(CONV_W - 1, bs, QKV_DIM), 0, 1)
    return (y_prompt, y_sample,
            conv_p[None], delta_p[None],
            h_p[:, :S5_FLAT].reshape(1, bp, S5_GROUPS, S5_STATE), h_p[:, S5_FLAT:].reshape(1, bp, S5_GROUPS, S5_STATE),
            conv_sample[None], delta_s.reshape(1, bs, DN_HEADS, DN_HEAD_DIM, DN_HEAD_DIM),
            h_s[:, :S5_FLAT].reshape(1, bs, S5_GROUPS, S5_STATE), h_s[:, S5_FLAT:].reshape(1, bs, S5_GROUPS, S5_STATE))
```

```python
import functools
import math

import jax
import jax.numpy as jnp
import numpy as np
from jax import lax
from jax.experimental import pallas as pl
from jax.experimental.pallas import tpu as pltpu

F32 = jnp.float32
BF16 = jnp.bfloat16
I32 = jnp.int32

D_MODEL = 1024
DN_HEADS = 8
DN_HEAD_DIM = 64
DN_WIDTH = DN_HEADS * DN_HEAD_DIM
QKV_DIM = 3 * DN_WIDTH
CONV_W = 4
DN_CHUNK = 64
S5_GROUP_CH = 16
S5_WIDTH = D_MODEL // 2
S5_GROUPS = S5_WIDTH // S5_GROUP_CH
S5_STATE = 64
S5_FLAT = S5_GROUPS * S5_STATE
MOE_GROUPS = 4
EXPERTS_PER_GROUP = 8
N_EXPERTS = MOE_GROUPS * EXPERTS_PER_GROUP
TOP_K = 2
EXPERT_FF = 256
RMS_EPS = 1e-6
L2_EPS = 1e-6

LANES = 128
SUBLANES = 8
VMEM_LIMIT = 56 * 1024 * 1024

C_QKV, C_Z, C_U, C_GA, C_GB, C_AB = 0, 1536, 2048, 2560, 3584, 4608
IN_PACKED = C_AB + LANES

ROW_TILE = 512
MOE_TILE = 256
ROUTER_ROWS = 40


def _mm(a, b):
    return jnp.dot(a.astype(BF16), b.astype(BF16), preferred_element_type=F32)


def _mm_nt(a, b):
    return lax.dot_general(a.astype(BF16), b.astype(BF16), (((1,), (1,)), ((), ())),
                           preferred_element_type=F32)


def _split3_dot(a, b01):
    a1 = a.astype(BF16)
    r1 = a - a1.astype(F32)
    a2 = r1.astype(BF16)
    a3 = (r1 - a2.astype(F32)).astype(BF16)
    out = jnp.dot(a3, b01, preferred_element_type=F32)
    out = out + jnp.dot(a2, b01, preferred_element_type=F32)
    return out + jnp.dot(a1, b01, preferred_element_type=F32)


def _cparams(sem):
    return pltpu.CompilerParams(dimension_semantics=sem, vmem_limit_bytes=VMEM_LIMIT)


ROW_SLAB = D_MODEL // LANES


def _slab_load(ref, rows, first=0, pitch=ROW_SLAB):
    return jnp.concatenate([ref[pl.ds(first + j, rows, stride=pitch), :] for j in range(ROW_SLAB)], axis=1)


def _slab_store(ref, x):
    for j in range(ROW_SLAB):
        ref[pl.ds(j, x.shape[0], stride=ROW_SLAB), :] = x[:, j * LANES:(j + 1) * LANES]


def _inproj_kernel(x_ref, nw_ref, w_ref, qkv_ref, z_ref, u_ref, ga_ref, gb_ref, ab_ref):
    x = x_ref[...]
    h = x * lax.rsqrt(jnp.mean(x * x, axis=-1, keepdims=True) + RMS_EPS) * nw_ref[...]
    hb = h.astype(BF16)

    def proj(lo, hi):
        return jnp.dot(hb, w_ref[:, lo:hi], preferred_element_type=F32)

    qkv_ref[...] = proj(C_QKV, C_Z)
    z_ref[...] = proj(C_Z, C_U).astype(z_ref.dtype)
    u_ref[...] = proj(C_U, C_GA).astype(u_ref.dtype)
    ga_ref[...] = proj(C_GA, C_GB).astype(ga_ref.dtype)
    gb_ref[...] = proj(C_GB, C_AB).astype(gb_ref.dtype)
    ab_ref[...] = proj(C_AB, IN_PACKED)


def _inproj(x2d, nw, wcat, nb):
    n = x2d.shape[0]
    t = n // nb
    tt = min(ROW_TILE, t)
    nt = t // tt
    row = lambda b, i: (b * nt + i, 0)
    const = lambda b, i: (0, 0)
    outs = pl.pallas_call(
        _inproj_kernel,
        grid=(nb, nt),
        in_specs=[pl.BlockSpec((tt, D_MODEL), row),
                  pl.BlockSpec((1, D_MODEL), const),
                  pl.BlockSpec((D_MODEL, IN_PACKED), const)],
        out_specs=[pl.BlockSpec((tt, QKV_DIM), row),
                   pl.BlockSpec((tt, DN_WIDTH), row),
                   pl.BlockSpec((tt, S5_WIDTH), lambda b, i: (i, b)),
                   pl.BlockSpec((tt, D_MODEL), row),
                   pl.BlockSpec((tt, D_MODEL), row),
                   pl.BlockSpec((tt, LANES), row)],
        out_shape=[jax.ShapeDtypeStruct((n, QKV_DIM), F32),
                   jax.ShapeDtypeStruct((n, DN_WIDTH), BF16),
                   jax.ShapeDtypeStruct((t, nb * S5_WIDTH), BF16),
                   jax.ShapeDtypeStruct((n, D_MODEL), BF16),
                   jax.ShapeDtypeStruct((n, D_MODEL), BF16),
                   jax.ShapeDtypeStruct((n, LANES), F32)],
        compiler_params=_cparams(("arbitrary", "arbitrary")),
        name="inproj",
    )(x2d, nw, wcat)
    qkv, z, u, ga, gb, ab = outs
    return qkv, z, u.reshape(t * nb, S5_WIDTH), ga, gb, ab


def _softplus(x):
    return jnp.maximum(x, 0.0) + jnp.log1p(jnp.exp(-jnp.abs(x)))


def _prep_kernel(qkv_ref, cinit_ref, cw_ref, ab_ref, gp_ref, seg_ref,
                 q_ref, k_ref, v_ref, gate_ref, cnew_ref, xp_ref, *, shift, rc, rows):
    @pl.when(pl.program_id(1) == 0)
    def _():
        xp_ref[0:rc, :] = cinit_ref[0]

    xp_ref[rc:rc + rows, :] = qkv_ref[...]
    acc = None
    for i in range(CONV_W):
        lo = rc + (i - (CONV_W - 1)) * shift
        term = xp_ref[lo:lo + rows, :] * cw_ref[i:i + 1, :]
        acc = term if acc is None else acc + term
    y = acc * jax.nn.sigmoid(acc)
    keep = (CONV_W - 1) * shift
    cnew_ref[0] = xp_ref[rc + rows - keep:rc + rows, :]
    xp_ref[0:rc, :] = xp_ref[rows:rows + rc, :]

    seg = seg_ref[...]
    q = y[:, 0:DN_WIDTH]
    k = y[:, DN_WIDTH:2 * DN_WIDTH]
    q_ref[...] = q * lax.rsqrt(jnp.dot((q * q).astype(BF16), seg, preferred_element_type=F32) + L2_EPS)
    k_ref[...] = k * lax.rsqrt(jnp.dot((k * k).astype(BF16), seg, preferred_element_type=F32) + L2_EPS)
    v_ref[...] = y[:, 2 * DN_WIDTH:]

    ab = ab_ref[...]
    g = -jnp.exp(gp_ref[0:1, :]) * _softplus(ab + gp_ref[1:2, :])
    beta = jax.nn.sigmoid(ab)
    lane = lax.broadcasted_iota(I32, ab.shape, 1)
    gate_ref[...] = jnp.where(lane < DN_HEADS, g, beta)


def _prep(qkv, cinit, conv_w, ab, gate_p, seg, nb, shift):
    n = qkv.shape[0]
    r = n // nb
    rows = min(ROW_TILE, r)
    nt = r // rows
    rc = cinit.shape[1]
    keep = (CONV_W - 1) * shift
    row = lambda b, i: (b * nt + i, 0)
    const = lambda b, i: (0, 0)
    kern = functools.partial(_prep_kernel, shift=shift, rc=rc, rows=rows)
    return pl.pallas_call(
        kern,
        grid=(nb, nt),
        in_specs=[pl.BlockSpec((rows, QKV_DIM), row),
                  pl.BlockSpec((1, rc, QKV_DIM), lambda b, i: (b, 0, 0)),
                  pl.BlockSpec((CONV_W, QKV_DIM), const),
                  pl.BlockSpec((rows, LANES), row),
                  pl.BlockSpec((2, LANES), const),
                  pl.BlockSpec((DN_WIDTH, DN_WIDTH), const)],
        out_specs=[pl.BlockSpec((rows, DN_WIDTH), row),
                   pl.BlockSpec((rows, DN_WIDTH), row),
                   pl.BlockSpec((rows, DN_WIDTH), row),
                   pl.BlockSpec((rows, LANES), row),
                   pl.BlockSpec((1, keep, QKV_DIM), lambda b, i: (b, 0, 0))],
        out_shape=[jax.ShapeDtypeStruct((n, DN_WIDTH), F32),
                   jax.ShapeDtypeStruct((n, DN_WIDTH), F32),
                   jax.ShapeDtypeStruct((n, DN_WIDTH), F32),
                   jax.ShapeDtypeStruct((n, LANES), F32),
                   jax.ShapeDtypeStruct((nb, keep, QKV_DIM), F32)],
        scratch_shapes=[pltpu.VMEM((rc + rows, QKV_DIM), F32)],
        compiler_params=_cparams(("arbitrary", "arbitrary")),
        name="prep",
    )(qkv, cinit, conv_w, ab, gate_p, seg)


def _delta_chunk_kernel(q_ref, k_ref, v_ref, gate_ref, tril_ref, o_ref, sfin_ref, s_ref):
    c = DN_CHUNK
    dk = DN_HEAD_DIM

    @pl.when(pl.program_id(1) == 0)
    def _():
        s_ref[...] = jnp.zeros_like(s_ref)

    gate = gate_ref[...]
    gc_all = _split3_dot_left(tril_ref[...], gate)
    gc_t = gc_all.T
    rowi = lax.broadcasted_iota(I32, (c, c), 0)
    coli = lax.broadcasted_iota(I32, (c, c), 1)
    causal = rowi >= coli
    strict = rowi > coli
    q_all = q_ref[...]
    k_all = k_ref[...]
    v_all = v_ref[...]
    heads = range(DN_HEADS)
    sl = [slice(h * dk, (h + 1) * dk) for h in heads]
    qh = [q_all[:, sl[h]] * (dk ** -0.5) for h in heads]
    kh = [k_all[:, sl[h]] for h in heads]
    gcol = [gc_all[:, h:h + 1] for h in heads]
    beta = [gate[:, DN_HEADS + h:DN_HEADS + h + 1] for h in heads]
    decay = [jnp.where(causal, jnp.exp(jnp.where(causal, gcol[h] - gc_t[h:h + 1, :], 0.0)), 0.0) for h in heads]
    kb = [kh[h] * beta[h] for h in heads]
    egc = [jnp.exp(gcol[h]) for h in heads]
    a_mat = [jnp.where(strict, _mm_nt(kb[h], kh[h]) * decay[h], 0.0) for h in heads]
    qk = [jnp.where(causal, _mm_nt(qh[h], kh[h]) * decay[h], 0.0) for h in heads]
    rhs = [jnp.concatenate([v_all[:, sl[h]] * beta[h], kb[h] * egc[h]], axis=1) for h in heads]
    sol = [rhs[h] - _mm(a_mat[h], rhs[h]) for h in heads]
    pw = a_mat
    for _ in range(int(math.log2(c)) - 1):
        pw = [_mm(pw[h], pw[h]) for h in heads]
        sol = [sol[h] + _mm(pw[h], sol[h]) for h in heads]
    g_last = [gc_all[c - 1:c, h:h + 1] for h in heads]
    q_dec = [qh[h] * egc[h] for h in heads]
    k_dec_t = [(kh[h] * jnp.exp(g_last[h] - gcol[h])).T for h in heads]
    s = [s_ref[h] for h in heads]
    v_new = [sol[h][:, :dk] - _mm(sol[h][:, dk:], s[h]) for h in heads]
    outs = [_mm(q_dec[h], s[h]) + _mm(qk[h], v_new[h]) for h in heads]
    for h in heads:
        s_ref[h] = s[h] * jnp.exp(g_last[h]) + _mm(k_dec_t[h], v_new[h])
    o_ref[...] = jnp.concatenate(outs, axis=1)
    sfin_ref[0] = s_ref[...]


def _split3_dot_left(b01, a):
    a1 = a.astype(BF16)
    r1 = a - a1.astype(F32)
    a2 = r1.astype(BF16)
    a3 = (r1 - a2.astype(F32)).astype(BF16)
    out = jnp.dot(b01, a3, preferred_element_type=F32)
    out = out + jnp.dot(b01, a2, preferred_element_type=F32)
    return out + jnp.dot(b01, a1, preferred_element_type=F32)


def _delta_prompt(q, k, v, gate, nb):
    n = q.shape[0]
    t = n // nb
    c = DN_CHUNK
    nc = t // c
    row = lambda b, i: (b * nc + i, 0)
    tril = jnp.tril(jnp.ones((c, c), F32)).astype(BF16)
    return pl.pallas_call(
        _delta_chunk_kernel,
        grid=(nb, nc),
        in_specs=[pl.BlockSpec((c, DN_WIDTH), row),
                  pl.BlockSpec((c, DN_WIDTH), row),
                  pl.BlockSpec((c, DN_WIDTH), row),
                  pl.BlockSpec((c, LANES), row),
                  pl.BlockSpec((c, c), lambda b, i: (0, 0))],
        out_specs=[pl.BlockSpec((c, DN_WIDTH), row),
                   pl.BlockSpec((1, DN_HEADS, DN_HEAD_DIM, DN_HEAD_DIM), lambda b, i: (b, 0, 0, 0))],
        out_shape=[jax.ShapeDtypeStruct((n, DN_WIDTH), F32),
                   jax.ShapeDtypeStruct((nb, DN_HEADS, DN_HEAD_DIM, DN_HEAD_DIM), F32)],
        scratch_shapes=[pltpu.VMEM((DN_HEADS, DN_HEAD_DIM, DN_HEAD_DIM), F32)],
        compiler_params=_cparams(("arbitrary", "arbitrary")),
        name="delta_prompt",
    )(q, k, v, gate, tril)


def _delta_step_kernel(q_ref, k_ref, v_ref, gate_ref, s0_ref, ex_ref, o_ref, s_ref, kx_ref, qx_ref, *, nt, nb):
    dk = DN_HEAD_DIM
    flat = dk * dk
    nv = flat // LANES
    p = pl.program_id(0)
    lane = lax.broadcasted_iota(I32, (SUBLANES, LANES), 1)
    low = lane < dk
    ex = ex_ref[...]

    for t in range(nt):
        rs = slice(t * nb, (t + 1) * nb)
        kx_ref[...] = jnp.dot(k_ref[rs, :].astype(BF16), ex, preferred_element_type=F32)
        qx_ref[...] = jnp.dot((q_ref[rs, :] * (dk ** -0.5)).astype(BF16), ex, preferred_element_type=F32)
        src_ref = s0_ref if t == 0 else s_ref

        def tile_body(bt, carry, t=t, src_ref=src_ref):
            b0 = pl.multiple_of(bt * SUBLANES, SUBLANES)
            r0 = pl.multiple_of(t * nb + b0, SUBLANES)
            gate = gate_ref[pl.ds(r0, SUBLANES), :]
            vv = v_ref[pl.ds(r0, SUBLANES), :]
            vsw = pltpu.roll(vv, dk, axis=1)
            o_pair = None
            for j in range(2):
                c0 = j * flat
                g = jnp.sum(jnp.where(lane == 2 * p + j, gate, 0.0), axis=1, keepdims=True)
                beta = jnp.sum(jnp.where(lane == 2 * p + j + DN_HEADS, gate, 0.0), axis=1, keepdims=True)
                a = jnp.exp(g)
                vdup = jnp.where(low, vv, vsw) if j == 0 else jnp.where(low, vsw, vv)
                s = [src_ref[pl.ds(b0, SUBLANES), c0 + i * LANES:c0 + (i + 1) * LANES] for i in range(nv)]
                kx = [kx_ref[pl.ds(b0, SUBLANES), c0 + i * LANES:c0 + (i + 1) * LANES] for i in range(nv)]
                ks = kx[0] * s[0]
                for i in range(1, nv):
                    ks = ks + kx[i] * s[i]
                ks = ks + pltpu.roll(ks, dk, axis=1)
                delta = beta * (vdup - a * ks)
                oh = None
                for i in range(nv):
                    si = a * s[i] + kx[i] * delta
                    s_ref[pl.ds(b0, SUBLANES), c0 + i * LANES:c0 + (i + 1) * LANES] = si
                    term = qx_ref[pl.ds(b0, SUBLANES), c0 + i * LANES:c0 + (i + 1) * LANES] * si
                    oh = term if oh is None else oh + term
                oh = oh + pltpu.roll(oh, dk, axis=1)
                o_pair = oh if j == 0 else jnp.where(low, o_pair, oh)
            o_ref[pl.ds(r0, SUBLANES), :] = o_pair
            return carry

        lax.fori_loop(0, nb // SUBLANES, tile_body, 0)


def _delta_sample(q, k, v, gate, s0, nb, nt):
    dk = DN_HEAD_DIM
    flat = dk * dk
    n = nt * nb
    col = np.arange(2 * flat)
    ex = np.arange(LANES)[:, None] == ((col // flat) * dk + (col % flat) // dk)[None, :]
    ex = jnp.asarray(ex, BF16)
    kern = functools.partial(_delta_step_kernel, nt=nt, nb=nb)
    pair = lambda p: (0, p)
    return pl.pallas_call(
        kern,
        grid=(DN_HEADS // 2,),
        in_specs=[pl.BlockSpec((n, LANES), pair),
                  pl.BlockSpec((n, LANES), pair),
                  pl.BlockSpec((n, LANES), pair),
                  pl.BlockSpec((n, LANES), lambda p: (0, 0)),
                  pl.BlockSpec((nb, 2 * flat), pair),
                  pl.BlockSpec((LANES, 2 * flat), lambda p: (0, 0))],
        out_specs=[pl.BlockSpec((n, LANES), pair),
                   pl.BlockSpec((nb, 2 * flat), pair)],
        out_shape=[jax.ShapeDtypeStruct((n, DN_WIDTH), F32),
                   jax.ShapeDtypeStruct((nb, DN_HEADS * flat), F32)],
        scratch_shapes=[pltpu.VMEM((nb, 2 * flat), F32),
                        pltpu.VMEM((nb, 2 * flat), F32)],
        compiler_params=_cparams(("arbitrary",)),
        name="delta_sample",
    )(q, k, v, gate, s0, ex)


def _s5_kernel(u_ref, bre_ref, bim_ref, lam_ref, c_ref, d_ref, h0_ref, y_ref, hfin_ref,
               bw_ref, ab_ref, x_ref, h_ref, *, nb, tt):
    p2 = S5_FLAT

    @pl.when(pl.program_id(0) == 0)
    def _():
        lr = lam_ref[0:1, :]
        li = lam_ref[1:2, :]
        dt = jnp.exp(lam_ref[2:3, :])
        mag = jnp.exp(lr * dt)
        ab_re = mag * jnp.cos(li * dt)
        ab_im = mag * jnp.sin(li * dt)
        den = lr * lr + li * li
        nr = ab_re - 1.0
        ni = ab_im
        f_re = (nr * lr + ni * li) / den
        f_im = (ni * lr - nr * li) / den
        ab_ref[0:1, :] = ab_re
        ab_ref[1:2, :] = ab_im
        bre = bre_ref[...]
        bim = bim_ref[...]
        bw_ref[:, 0:p2] = (bre * f_re - bim * f_im).astype(BF16)
        bw_ref[:, p2:2 * p2] = (bim * f_re + bre * f_im).astype(BF16)
        h_ref[...] = h0_ref[...]

    u = u_ref[...]
    x_ref[...] = jnp.dot(u, bw_ref[...], preferred_element_type=F32)
    a_re = ab_ref[0:1, :]
    a_im = ab_ref[1:2, :]

    if nb == SUBLANES:
        are = jnp.broadcast_to(a_re, (nb, p2))
        aim = jnp.broadcast_to(a_im, (nb, p2))

        def step(t, carry):
            hr, hi = carry
            r0 = pl.multiple_of(t * nb, nb)
            xr = x_ref[pl.ds(r0, nb), 0:p2]
            xi = x_ref[pl.ds(r0, nb), p2:2 * p2]
            nr = are * hr - aim * hi + xr
            ni = are * hi + aim * hr + xi
            x_ref[pl.ds(r0, nb), 0:p2] = nr
            x_ref[pl.ds(r0, nb), p2:2 * p2] = ni
            return nr, ni

        hr, hi = lax.fori_loop(0, tt, step, (h_ref[:, 0:p2], h_ref[:, p2:2 * p2]))
        h_ref[:, 0:p2] = hr
        h_ref[:, p2:2 * p2] = hi
    else:
        for t in range(tt):
            rs = slice(t * nb, (t + 1) * nb)
            hr = h_ref[:, 0:p2]
            hi = h_ref[:, p2:2 * p2]
            nr = a_re * hr - a_im * hi + x_ref[rs, 0:p2]
            ni = a_re * hi + a_im * hr + x_ref[rs, p2:2 * p2]
            h_ref[:, 0:p2] = nr
            h_ref[:, p2:2 * p2] = ni
            x_ref[rs, 0:p2] = nr
            x_ref[rs, p2:2 * p2] = ni

    y = jnp.dot(x_ref[...].astype(BF16), c_ref[...], preferred_element_type=F32)
    y_ref[...] = y + d_ref[...] * u.astype(F32)
    hfin_ref[...] = h_ref[...]


def _s5(u, bre, bim, lam, cmat, dvec, h0, nb):
    n = u.shape[0]
    t = n // nb
    tt = min(ROW_TILE // nb, t)
    rows = tt * nb
    const = lambda i: (0, 0)
    kern = functools.partial(_s5_kernel, nb=nb, tt=tt)
    return pl.pallas_call(
        kern,
        grid=(t // tt,),
        in_specs=[pl.BlockSpec((rows, S5_WIDTH), lambda i: (i, 0)),
                  pl.BlockSpec((S5_WIDTH, S5_FLAT), const),
                  pl.BlockSpec((S5_WIDTH, S5_FLAT), const),
                  pl.BlockSpec((SUBLANES, S5_FLAT), const),
                  pl.BlockSpec((2 * S5_FLAT, S5_WIDTH), const),
                  pl.BlockSpec((1, S5_WIDTH), const),
                  pl.BlockSpec((nb, 2 * S5_FLAT), const)],
        out_specs=[pl.BlockSpec((rows, S5_WIDTH), lambda i: (i, 0)),
                   pl.BlockSpec((nb, 2 * S5_FLAT), const)],
        out_shape=[jax.ShapeDtypeStruct((n, S5_WIDTH), F32),
                   jax.ShapeDtypeStruct((nb, 2 * S5_FLAT), F32)],
        scratch_shapes=[pltpu.VMEM((S5_WIDTH, 2 * S5_FLAT), BF16),
                        pltpu.VMEM((SUBLANES, S5_FLAT), F32),
                        pltpu.VMEM((rows, 2 * S5_FLAT), F32),
                        pltpu.VMEM((nb, 2 * S5_FLAT), F32)],
        compiler_params=_cparams(("arbitrary",)),
        name="s5",
    )(u, bre, bim, lam, cmat, dvec, h0)


def _postmix_kernel(xp_ref, op_ref, zp_ref, ysp_ref, gap_ref, gbp_ref,
                    xs_ref, os_ref, zs_ref, yss_ref, gas_ref, gbs_ref, *rest, nblk_p):
    @pl.when(pl.program_id(0) < nblk_p)
    def _():
        _postmix_body(xp_ref, op_ref, zp_ref, ysp_ref, gap_ref, gbp_ref, *rest)

    @pl.when(pl.program_id(0) >= nblk_p)
    def _():
        _postmix_body(xs_ref, os_ref, zs_ref, yss_ref, gas_ref, gbs_ref, *rest)


def _postmix_body(x_ref, o_ref, z_ref, ys_ref, ga_ref, gb_ref, hw_ref, seg_ref, wa_ref, wglu_ref, wb_ref,
                  wo_ref, nf_ref, wr_ref, x1_ref, hn_ref, ridx_ref, rw_ref):
    o = o_ref[...]
    ms = jnp.dot((o * o).astype(BF16), seg_ref[...], preferred_element_type=F32) * (1.0 / DN_HEAD_DIM)
    on = o * lax.rsqrt(ms + RMS_EPS) * hw_ref[...]
    z = z_ref[...].astype(F32)
    oa = on * (z * jax.nn.sigmoid(z))
    y_a = _mm(oa, wa_ref[...])
    ys = jax.nn.gelu(ys_ref[...])
    ys = ys * jax.nn.sigmoid(_mm(ys, wglu_ref[...]))
    y_b = _mm(ys, wb_ref[...])
    mixed = jax.nn.sigmoid(ga_ref[...].astype(F32)) * y_a + jax.nn.sigmoid(gb_ref[...].astype(F32)) * y_b
    x1 = x_ref[...] + _mm(mixed, wo_ref[...])
    x1_ref[...] = x1
    hn = x1 * lax.rsqrt(jnp.mean(x1 * x1, axis=-1, keepdims=True) + RMS_EPS) * nf_ref[...]
    _slab_store(hn_ref, hn)

    logits = lax.dot_general(wr_ref[...], hn, (((1,), (1,)), ((), ())),
                             precision=lax.Precision.HIGHEST, preferred_element_type=F32)
    coarse = logits[N_EXPERTS:N_EXPERTS + MOE_GROUPS, :]
    cm = jnp.max(coarse, axis=0, keepdims=True)
    ce = jnp.exp(coarse - cm)
    pc = ce / jnp.sum(ce, axis=0, keepdims=True)
    p_sel = jnp.max(pc, axis=0, keepdims=True)
    gi = lax.broadcasted_iota(I32, pc.shape, 0)
    g_sel = jnp.min(jnp.where(pc == p_sel, gi, MOE_GROUPS), axis=0, keepdims=True)
    fine = jnp.zeros((EXPERTS_PER_GROUP, logits.shape[1]), F32)
    for g in range(MOE_GROUPS):
        fine = fine + jnp.where(g_sel == g, logits[g * EXPERTS_PER_GROUP:(g + 1) * EXPERTS_PER_GROUP, :], 0.0)
    fm = jnp.max(fine, axis=0, keepdims=True)
    fe = jnp.exp(fine - fm)
    pf = fe / jnp.sum(fe, axis=0, keepdims=True)
    ei = lax.broadcasted_iota(I32, pf.shape, 0)
    v1 = jnp.max(pf, axis=0, keepdims=True)
    i1 = jnp.min(jnp.where(pf == v1, ei, EXPERTS_PER_GROUP), axis=0, keepdims=True)
    rest = jnp.where(ei == i1, -1.0, pf)
    v2 = jnp.max(rest, axis=0, keepdims=True)
    i2 = jnp.min(jnp.where(rest == v2, ei, EXPERTS_PER_GROUP), axis=0, keepdims=True)
    tot = v1 + v2
    ridx_ref[0:1, :] = g_sel * EXPERTS_PER_GROUP + i1
    ridx_ref[1:2, :] = g_sel * EXPERTS_PER_GROUP + i2
    rw_ref[0:1, :] = v1 / tot * p_sel
    rw_ref[1:2, :] = v2 / tot * p_sel


def _postmix(prompt, sample, weights, nb):
    n_p = prompt[0].shape[0]
    n_s = sample[0].shape[0]
    t = n_p // nb
    tt = min(ROW_TILE, t, n_s)
    nt = t // tt
    nblk_p = n_p // tt
    nblk = nblk_p + n_s // tt
    n_total = n_p + n_s
    prow = lambda i: (jnp.minimum(i, nblk_p - 1), 0)
    pys = lambda i: (jnp.minimum(i, nblk_p - 1) % nt, jnp.minimum(i, nblk_p - 1) // nt)
    srow = lambda i: (jnp.maximum(i - nblk_p, 0), 0)
    const = lambda i: (0, 0)

    def stream_specs(row, ysmap):
        return [pl.BlockSpec((tt, D_MODEL), row),
                pl.BlockSpec((tt, DN_WIDTH), row),
                pl.BlockSpec((tt, DN_WIDTH), row),
                pl.BlockSpec((tt, S5_WIDTH), ysmap),
                pl.BlockSpec((tt, D_MODEL), row),
                pl.BlockSpec((tt, D_MODEL), row)]

    weight_specs = [pl.BlockSpec((1, DN_WIDTH), const),
                    pl.BlockSpec((DN_WIDTH, DN_WIDTH), const),
                    pl.BlockSpec((DN_WIDTH, D_MODEL), const),
                    pl.BlockSpec((S5_WIDTH, S5_WIDTH), const),
                    pl.BlockSpec((S5_WIDTH, D_MODEL), const),
                    pl.BlockSpec((D_MODEL, D_MODEL), const),
                    pl.BlockSpec((1, D_MODEL), const),
                    pl.BlockSpec((ROUTER_ROWS, D_MODEL), const)]
    xp, op, zp, ysp, gap, gbp = prompt
    return pl.pallas_call(
        functools.partial(_postmix_kernel, nblk_p=nblk_p),
        grid=(nblk,),
        in_specs=stream_specs(prow, pys) + stream_specs(srow, srow) + weight_specs,
        out_specs=[pl.BlockSpec((tt, D_MODEL), lambda i: (i, 0)),
                   pl.BlockSpec((tt * ROW_SLAB, LANES), lambda i: (i, 0)),
                   pl.BlockSpec((TOP_K, tt), lambda i: (0, i)),
                   pl.BlockSpec((TOP_K, tt), lambda i: (0, i))],
        out_shape=[jax.ShapeDtypeStruct((n_total, D_MODEL), F32),
                   jax.ShapeDtypeStruct((n_total * ROW_SLAB, LANES), F32),
                   jax.ShapeDtypeStruct((TOP_K, n_total), I32),
                   jax.ShapeDtypeStruct((TOP_K, n_total), F32)],
        compiler_params=_cparams(("arbitrary",)),
        name="postmix",
    )(xp, op, zp, ysp.reshape(t, nb * S5_WIDTH), gap, gbp, *sample, *weights)


def _moe_kernel(texp_ref, tsrc_ref, tnv_ref, order_ref, otok_ref, hn_hbm, wu_ref, wd_ref, ytok_hbm,
                xbuf, ybuf, gsem, ssem, *, n_assign):
    del texp_ref
    i = pl.program_id(0)
    nsteps = pl.num_programs(0)
    tm = MOE_TILE
    slot = lax.rem(i, 2)
    nv = tnv_ref[i]
    prev_valid = jnp.logical_and(i >= 1, tnv_ref[jnp.maximum(i - 1, 0)] > 0)

    rs = ROW_SLAB

    def start_gather(step, sl):
        src0 = tsrc_ref[step]
        for r in range(tm):
            tok8 = pl.multiple_of(otok_ref[src0 + r], rs)
            pltpu.make_async_copy(hn_hbm.at[pl.ds(tok8, rs), :], xbuf.at[sl, pl.ds(r * rs, rs), :],
                                  gsem.at[sl]).start()

    def start_scatter(step, sl):
        src0 = tsrc_ref[step]
        nvs = tnv_ref[step]
        for r in range(tm):
            dst8 = jnp.where(r < nvs, order_ref[src0 + r], (n_assign + sl * tm + r) * rs)
            pltpu.make_async_copy(ybuf.at[sl, pl.ds(r * rs, rs), :],
                                  ytok_hbm.at[pl.ds(pl.multiple_of(dst8, rs), rs), :], ssem.at[sl]).start()

    def wait_slot(buf, sem, sl):
        pltpu.make_async_copy(buf.at[sl], buf.at[sl], sem.at[sl]).wait()

    @pl.when(i == 0)
    def _():
        ybuf[...] = jnp.zeros_like(ybuf)
        for sl in range(2):
            spare = pltpu.make_async_copy(ybuf.at[sl], ytok_hbm.at[pl.ds((n_assign + sl * tm) * rs, tm * rs), :],
                                          ssem.at[sl])
            spare.start()
            spare.wait()
        start_gather(0, 0)

    for sl in range(2):
        @pl.when(jnp.logical_and(slot == sl, jnp.logical_and(i >= 2, tnv_ref[jnp.maximum(i - 2, 0)] > 0)))
        def _():
            wait_slot(ybuf, ssem, sl)

        @pl.when(jnp.logical_and(slot == sl, nv > 0))
        def _():
            wait_slot(xbuf, gsem, sl)
            start_gather(jnp.minimum(i + 1, nsteps - 1), 1 - sl)
            x = _slab_load(xbuf.at[sl], tm).astype(BF16)
            hu = jnp.dot(x, wu_ref[0].astype(BF16), preferred_element_type=F32)
            gate = hu[:, :EXPERT_FF]
            up = hu[:, EXPERT_FF:]
            act = gate * jax.nn.sigmoid(gate) * up
            _slab_store(ybuf.at[sl], jnp.dot(act.astype(BF16), wd_ref[0].astype(BF16), preferred_element_type=F32))
            start_scatter(i, sl)

        @pl.when(jnp.logical_and(slot == sl, jnp.logical_and(nv == 0, prev_valid)))
        def _():
            wait_slot(xbuf, gsem, sl)

        @pl.when(jnp.logical_and(slot == sl, i == nsteps - 1))
        def _():
            @pl.when(nv > 0)
            def _():
                wait_slot(xbuf, gsem, 1 - sl)
                wait_slot(ybuf, ssem, sl)

            @pl.when(prev_valid)
            def _():
                wait_slot(ybuf, ssem, 1 - sl)


def _moe(hn, w_up, w_down, texp, tsrc, tnv, order, otok, n_assign):
    ntiles = texp.shape[0]
    wmap = lambda i, te, ts, tn, od, ot: (te[i], 0, 0)
    grid_spec = pltpu.PrefetchScalarGridSpec(
        num_scalar_prefetch=5,
        grid=(ntiles,),
        in_specs=[pl.BlockSpec(memory_space=pl.ANY),
                  pl.BlockSpec((1, D_MODEL, 2 * EXPERT_FF), wmap),
                  pl.BlockSpec((1, EXPERT_FF, D_MODEL), wmap)],
        out_specs=pl.BlockSpec(memory_space=pl.ANY),
        scratch_shapes=[pltpu.VMEM((2, MOE_TILE * ROW_SLAB, LANES), F32),
                        pltpu.VMEM((2, MOE_TILE * ROW_SLAB, LANES), F32),
                        pltpu.SemaphoreType.DMA((2,)),
                        pltpu.SemaphoreType.DMA((2,))])
    return pl.pallas_call(
        functools.partial(_moe_kernel, n_assign=n_assign),
        grid_spec=grid_spec,
        out_shape=jax.ShapeDtypeStruct(((n_assign + 2 * MOE_TILE) * ROW_SLAB, LANES), F32),
        compiler_params=pltpu.CompilerParams(dimension_semantics=("arbitrary",), vmem_limit_bytes=VMEM_LIMIT,
                                             has_side_effects=True),
        name="moe",
    )(texp, tsrc, tnv, order, otok, hn, w_up, w_down)


def _combine_kernel(x1_ref, y_ref, w_ref, nw_ref, out_ref):
    tt = x1_ref.shape[0]
    w = w_ref[...]
    y0 = _slab_load(y_ref, tt, 0, TOP_K * ROW_SLAB)
    y1 = _slab_load(y_ref, tt, ROW_SLAB, TOP_K * ROW_SLAB)
    x = x1_ref[...] + w[:, 0:1] * y0 + w[:, 1:2] * y1
    out_ref[...] = x * lax.rsqrt(jnp.mean(x * x, axis=-1, keepdims=True) + RMS_EPS) * nw_ref[...]


def _combine(x1, ytok, wtok, nw):
    n = x1.shape[0]
    tt = math.gcd(n, ROW_TILE)
    return pl.pallas_call(
        _combine_kernel,
        grid=(n // tt,),
        in_specs=[pl.BlockSpec((tt, D_MODEL), lambda i: (i, 0)),
                  pl.BlockSpec((tt * TOP_K * ROW_SLAB, LANES), lambda i: (i, 0)),
                  pl.BlockSpec((tt, TOP_K), lambda i: (i, 0)),
                  pl.BlockSpec((1, D_MODEL), lambda i: (0, 0))],
        out_specs=pl.BlockSpec((tt, D_MODEL), lambda i: (i, 0)),
        out_shape=jax.ShapeDtypeStruct((n, D_MODEL), F32),
        compiler_params=_cparams(("arbitrary",)),
        name="combine",
    )(x1, ytok, wtok, nw)


def _route_plan(ridx, n_tok):
    tm = MOE_TILE
    n_assign = n_tok * TOP_K
    ntiles = n_assign // tm + N_EXPERTS
    e_flat = ridx.T.reshape(n_assign)
    order = jnp.argsort(e_flat, stable=True).astype(I32)
    counts = jnp.sum((e_flat[:, None] == jnp.arange(N_EXPERTS, dtype=I32)[None, :]).astype(I32), axis=0)
    cstart = jnp.cumsum(counts) - counts
    tiles_e = (counts + tm - 1) // tm
    tend = jnp.cumsum(tiles_e)
    tstart = tend - tiles_e
    tile_id = jnp.arange(ntiles, dtype=I32)
    texp = jnp.minimum(jnp.sum((tile_id[:, None] >= tend[None, :]).astype(I32), axis=1), N_EXPERTS - 1)
    onehot = (texp[:, None] == jnp.arange(N_EXPERTS, dtype=I32)[None, :]).astype(I32)
    pick = lambda v: jnp.sum(onehot * v[None, :], axis=1)
    done = (tile_id - pick(tstart)) * tm
    tnv = jnp.where(tile_id < tend[-1], jnp.clip(pick(counts) - done, 0, tm), 0)
    tsrc = jnp.where(tnv > 0, pick(cstart) + done, 0)
    order = jnp.concatenate([order, jnp.zeros((tm,), I32)])
    return texp.astype(I32), tsrc.astype(I32), tnv.astype(I32), order * ROW_SLAB, (order // TOP_K) * ROW_SLAB


def _block_diag(m):
    g, a, b = m.shape
    eye = jnp.eye(g, dtype=m.dtype)
    return (eye[:, None, :, None] * m[:, :, None, :]).reshape(g * a, g * b)


def kernel(x_prompt, x_sample, state_conv, state_delta, state_ssm_re, state_ssm_im, norm_mix_w, w_in, conv_w, a_log, dt_bias, head_norm_w, w_a_up, s5_lambda_re, s5_lambda_im, s5_log_step, s5_b_re, s5_b_im, s5_c_re, s5_c_im, s5_d, w_glu, w_b_up, w_o, norm_ffn_w, w_router_coarse, w_router_fine, w_expert_up, w_expert_down, norm_final_w):
    bp, tp, _ = x_prompt.shape
    bs, ts, _ = x_sample.shape
    n_p = bp * tp
    n_s = bs * ts
    n_tok = n_p + n_s
    l = 0

    w = w_in[l]
    cuts = np.cumsum([0, QKV_DIM, DN_WIDTH, DN_HEADS, DN_HEADS, S5_WIDTH, D_MODEL, D_MODEL])
    w_qkv, w_z, w_a, w_b, w_u, w_ga, w_gb = [w[:, cuts[i]:cuts[i + 1]] for i in range(7)]
    w_ab = jnp.concatenate([w_a, w_b, jnp.zeros((D_MODEL, LANES - 2 * DN_HEADS), F32)], axis=1)
    wcat = jnp.concatenate([w_qkv, w_z, w_u, w_ga, w_gb, w_ab], axis=1).astype(BF16)
    nw_mix = norm_mix_w[l].reshape(1, D_MODEL)
    pad8 = lambda v: jnp.concatenate([v, jnp.zeros((LANES - DN_HEADS,), F32)]).reshape(1, LANES)
    gate_p = jnp.concatenate([pad8(a_log[l]), pad8(dt_bias[l])], axis=0)
    seg = _block_diag(jnp.ones((DN_HEADS, DN_HEAD_DIM, DN_HEAD_DIM), BF16))
    bre = _block_diag(jnp.swapaxes(s5_b_re[l], 1, 2))
    bim = _block_diag(jnp.swapaxes(s5_b_im[l], 1, 2))
    lam = jnp.concatenate([s5_lambda_re[l].reshape(1, S5_FLAT), s5_lambda_im[l].reshape(1, S5_FLAT),
                           jnp.repeat(s5_log_step[l], S5_STATE).reshape(1, S5_FLAT),
                           jnp.zeros((SUBLANES - 3, S5_FLAT), F32)], axis=0)
    cmat = jnp.concatenate([_block_diag(jnp.swapaxes(s5_c_re[l], 1, 2)),
                            -_block_diag(jnp.swapaxes(s5_c_im[l], 1, 2))], axis=0).astype(BF16)
    dvec = s5_d[l].reshape(1, S5_WIDTH)
    hw = jnp.tile(head_norm_w[l], DN_HEADS).reshape(1, DN_WIDTH)
    wr = jnp.concatenate([w_router_fine[l].T, w_router_coarse[l].T,
                          jnp.zeros((ROUTER_ROWS - N_EXPERTS - MOE_GROUPS, D_MODEL), F32)], axis=0)
    pm_weights = (hw, seg, w_a_up[l].astype(BF16), w_glu[l].astype(BF16), w_b_up[l].astype(BF16),
                  w_o[l].astype(BF16), norm_ffn_w[l].reshape(1, D_MODEL), wr)

    xp2 = x_prompt.reshape(n_p, D_MODEL)
    qkv_p, z_p, u_p, ga_p, gb_p, ab_p = _inproj(xp2, nw_mix, wcat, bp)
    q_p, k_p, v_p, gates_p, conv_p = _prep(qkv_p, jnp.zeros((bp, SUBLANES, QKV_DIM), F32), conv_w[l], ab_p,
                                              gate_p, seg, bp, 1)
    o_p, delta_p = _delta_prompt(q_p, k_p, v_p, gates_p, bp)
    ys_p, h_p = _s5(u_p, bre, bim, lam, cmat, dvec, jnp.zeros((bp, 2 * S5_FLAT), F32), bp)

    xs2 = jnp.swapaxes(x_sample, 0, 1).reshape(n_s, D_MODEL)
    qkv_s, z_s, u_s, ga_s, gb_s, ab_s = _inproj(xs2, nw_mix, wcat, 1)
    cinit_s = jnp.swapaxes(state_conv[l], 0, 1).reshape(1, (CONV_W - 1) * bs, QKV_DIM)
    q_s, k_s, v_s, gate_s, conv_s = _prep(qkv_s, cinit_s, conv_w[l], ab_s, gate_p, seg, 1, bs)
    s0 = state_delta[l].reshape(bs, DN_HEADS * DN_HEAD_DIM * DN_HEAD_DIM)
    o_s, delta_s = _delta_sample(q_s, k_s, v_s, gate_s, s0, bs, ts)
    h0_s = jnp.concatenate([state_ssm_re[l].reshape(bs, S5_FLAT), state_ssm_im[l].reshape(bs, S5_FLAT)], axis=1)
    ys_s, h_s = _s5(u_s, bre, bim, lam, cmat, dvec, h0_s, bs)
    x1, hn, ridx, rw = _postmix((xp2, o_p, z_p, ys_p, ga_p, gb_p), (xs2, o_s, z_s, ys_s, ga_s, gb_s),
                                pm_weights, bp)

    texp, tsrc, tnv, order, otok = _route_plan(ridx, n_tok)
    ytok = _moe(hn, w_expert_up[l], w_expert_down[l], texp, tsrc, tnv, order, otok, n_tok * TOP_K)
    y_all = _combine(x1, ytok, rw.T, norm_final_w.reshape(1, D_MODEL))

    y_prompt = y_all[:n_p].reshape(bp, tp, D_MODEL)
    y_sample = jnp.swapaxes(y_all[n_p:].reshape(ts, bs, D_MODEL), 0, 1)
    conv_sample = jnp.swapaxes(conv_s.reshape(CONV_W - 1, bs, QKV_DIM), 0, 1)
    return (y_prompt, y_sample,
            conv_p[None], delta_p[None],
            h_p[:, :S5_FLAT].reshape(1, bp, S5_GROUPS, S5_STATE), h_p[:, S5_FLAT:].reshape(1, bp, S5_GROUPS, S5_STATE),
            conv_sample[None], delta_s.reshape(1, bs, DN_HEADS, DN_HEAD_DIM, DN_HEAD_DIM),
            h_s[:, :S5_FLAT].reshape(1, bs, S5_GROUPS, S5_STATE), h_s[:, S5_FLAT:].reshape(1, bs, S5_GROUPS, S5_STATE))
```

```python
import functools
import math

import jax
import jax.numpy as jnp
import numpy as np
from jax import lax
from jax.experimental import pallas as pl
from jax.experimental.pallas import tpu as pltpu

F32 = jnp.float32
BF16 = jnp.bfloat16
I32 = jnp.int32

D_MODEL = 1024
DN_HEADS = 8
DN_HEAD_DIM = 64
DN_WIDTH = DN_HEADS * DN_HEAD_DIM
QKV_DIM = 3 * DN_WIDTH
CONV_W = 4
DN_CHUNK = 64
S5_GROUP_CH = 16
S5_WIDTH = D_MODEL // 2
S5_GROUPS = S5_WIDTH // S5_GROUP_CH
S5_STATE = 64
S5_FLAT = S5_GROUPS * S5_STATE
MOE_GROUPS = 4
EXPERTS_PER_GROUP = 8
N_EXPERTS = MOE_GROUPS * EXPERTS_PER_GROUP
TOP_K = 2
EXPERT_FF = 256
RMS_EPS = 1e-6
L2_EPS = 1e-6

LANES = 128
SUBLANES = 8
VMEM_LIMIT = 56 * 1024 * 1024

C_QKV, C_Z, C_U, C_GA, C_GB, C_AB = 0, 1536, 2048, 2560, 3584, 4608
IN_PACKED = C_AB + LANES

ROW_TILE = 512
MOE_TILE = 256
COMBINE_TILE = 256
DELTA_SUBCHUNKS = 4
S5_SUPER = 2
S5_SCAN_SPLIT = 2
ROUTER_ROWS = 40


def _mm(a, b):
    return jnp.dot(a.astype(BF16), b.astype(BF16), preferred_element_type=F32)


def _mm_nt(a, b):
    return lax.dot_general(a.astype(BF16), b.astype(BF16), (((1,), (1,)), ((), ())),
                           preferred_element_type=F32)


def _split3_dot(a, b01):
    a1 = a.astype(BF16)
    r1 = a - a1.astype(F32)
    a2 = r1.astype(BF16)
    a3 = (r1 - a2.astype(F32)).astype(BF16)
    out = jnp.dot(a3, b01, preferred_element_type=F32)
    out = out + jnp.dot(a2, b01, preferred_element_type=F32)
    return out + jnp.dot(a1, b01, preferred_element_type=F32)


def _cparams(sem):
    return pltpu.CompilerParams(dimension_semantics=sem, vmem_limit_bytes=VMEM_LIMIT)


ROW_SLAB = D_MODEL // LANES


def _slab_load(ref, rows, first=0, pitch=ROW_SLAB):
    return jnp.concatenate([ref[pl.ds(first + j, rows, stride=pitch), :] for j in range(ROW_SLAB)], axis=1)


def _slab_store(ref, x):
    for j in range(ROW_SLAB):
        ref[pl.ds(j, x.shape[0], stride=ROW_SLAB), :] = x[:, j * LANES:(j + 1) * LANES]


def _inproj_kernel(x_ref, nw_ref, w_ref, qkv_ref, z_ref, u_ref, ga_ref, gb_ref, ab_ref):
    x = x_ref[...]
    h = x * lax.rsqrt(jnp.mean(x * x, axis=-1, keepdims=True) + RMS_EPS) * nw_ref[...]
    hb = h.astype(BF16)

    def proj(lo, hi):
        return jnp.dot(hb, w_ref[:, lo:hi], preferred_element_type=F32)

    qkv_ref[...] = proj(C_QKV, C_Z)
    z_ref[...] = proj(C_Z, C_U).astype(z_ref.dtype)
    u_ref[...] = proj(C_U, C_GA).astype(u_ref.dtype)
    ga_ref[...] = proj(C_GA, C_GB).astype(ga_ref.dtype)
    gb_ref[...] = proj(C_GB, C_AB).astype(gb_ref.dtype)
    ab_ref[...] = proj(C_AB, IN_PACKED)


def _inproj(x2d, nw, wcat, nb):
    n = x2d.shape[0]
    t = n // nb
    tt = min(ROW_TILE, t)
    nt = t // tt
    row = lambda b, i: (b * nt + i, 0)
    const = lambda b, i: (0, 0)
    outs = pl.pallas_call(
        _inproj_kernel,
        grid=(nb, nt),
        in_specs=[pl.BlockSpec((tt, D_MODEL), row),
                  pl.BlockSpec((1, D_MODEL), const),
                  pl.BlockSpec((D_MODEL, IN_PACKED), const)],
        out_specs=[pl.BlockSpec((tt, QKV_DIM), row),
                   pl.BlockSpec((tt, DN_WIDTH), row),
                   pl.BlockSpec((tt, S5_WIDTH), lambda b, i: (i, b)),
                   pl.BlockSpec((tt, D_MODEL), row),
                   pl.BlockSpec((tt, D_MODEL), row),
                   pl.BlockSpec((tt, LANES), row)],
        out_shape=[jax.ShapeDtypeStruct((n, QKV_DIM), F32),
                   jax.ShapeDtypeStruct((n, DN_WIDTH), BF16),
                   jax.ShapeDtypeStruct((t, nb * S5_WIDTH), BF16),
                   jax.ShapeDtypeStruct((n, D_MODEL), BF16),
                   jax.ShapeDtypeStruct((n, D_MODEL), BF16),
                   jax.ShapeDtypeStruct((n, LANES), F32)],
        compiler_params=_cparams(("arbitrary", "arbitrary")),
        name="inproj",
    )(x2d, nw, wcat)
    qkv, z, u, ga, gb, ab = outs
    return qkv, z, u.reshape(t * nb, S5_WIDTH), ga, gb, ab


def _softplus(x):
    return jnp.maximum(x, 0.0) + jnp.log1p(jnp.exp(-jnp.abs(x)))


def _prep_kernel(qkv_ref, cinit_ref, cw_ref, ab_ref, gp_ref, seg_ref,
                 q_ref, k_ref, v_ref, gate_ref, cnew_ref, xp_ref, *, shift, rc, rows):
    @pl.when(pl.program_id(1) == 0)
    def _():
        xp_ref[0:rc, :] = cinit_ref[0]

    xp_ref[rc:rc + rows, :] = qkv_ref[...]
    acc = None
    for i in range(CONV_W):
        lo = rc + (i - (CONV_W - 1)) * shift
        term = xp_ref[lo:lo + rows, :] * cw_ref[i:i + 1, :]
        acc = term if acc is None else acc + term
    y = acc * jax.nn.sigmoid(acc)
    keep = (CONV_W - 1) * shift
    cnew_ref[0] = xp_ref[rc + rows - keep:rc + rows, :]
    xp_ref[0:rc, :] = xp_ref[rows:rows + rc, :]

    seg = seg_ref[...]
    q = y[:, 0:DN_WIDTH]
    k = y[:, DN_WIDTH:2 * DN_WIDTH]
    q_ref[...] = q * lax.rsqrt(jnp.dot((q * q).astype(BF16), seg, preferred_element_type=F32) + L2_EPS)
    k_ref[...] = k * lax.rsqrt(jnp.dot((k * k).astype(BF16), seg, preferred_element_type=F32) + L2_EPS)
    v_ref[...] = y[:, 2 * DN_WIDTH:]

    ab = ab_ref[...]
    g = -jnp.exp(gp_ref[0:1, :]) * _softplus(ab + gp_ref[1:2, :])
    beta = jax.nn.sigmoid(ab)
    lane = lax.broadcasted_iota(I32, ab.shape, 1)
    gate_ref[...] = jnp.where(lane < DN_HEADS, g, beta)


def _prep(qkv, cinit, conv_w, ab, gate_p, seg, nb, shift):
    n = qkv.shape[0]
    r = n // nb
    rows = min(ROW_TILE, r)
    nt = r // rows
    rc = cinit.shape[1]
    keep = (CONV_W - 1) * shift
    row = lambda b, i: (b * nt + i, 0)
    const = lambda b, i: (0, 0)
    kern = functools.partial(_prep_kernel, shift=shift, rc=rc, rows=rows)
    return pl.pallas_call(
        kern,
        grid=(nb, nt),
        in_specs=[pl.BlockSpec((rows, QKV_DIM), row),
                  pl.BlockSpec((1, rc, QKV_DIM), lambda b, i: (b, 0, 0)),
                  pl.BlockSpec((CONV_W, QKV_DIM), const),
                  pl.BlockSpec((rows, LANES), row),
                  pl.BlockSpec((2, LANES), const),
                  pl.BlockSpec((DN_WIDTH, DN_WIDTH), const)],
        out_specs=[pl.BlockSpec((rows, DN_WIDTH), row),
                   pl.BlockSpec((rows, DN_WIDTH), row),
                   pl.BlockSpec((rows, DN_WIDTH), row),
                   pl.BlockSpec((rows, LANES), row),
                   pl.BlockSpec((1, keep, QKV_DIM), lambda b, i: (b, 0, 0))],
        out_shape=[jax.ShapeDtypeStruct((n, DN_WIDTH), F32),
                   jax.ShapeDtypeStruct((n, DN_WIDTH), F32),
                   jax.ShapeDtypeStruct((n, DN_WIDTH), F32),
                   jax.ShapeDtypeStruct((n, LANES), F32),
                   jax.ShapeDtypeStruct((nb, keep, QKV_DIM), F32)],
        scratch_shapes=[pltpu.VMEM((rc + rows, QKV_DIM), F32)],
        compiler_params=_cparams(("arbitrary", "arbitrary")),
        name="prep",
    )(qkv, cinit, conv_w, ab, gate_p, seg)


def _delta_chunk_kernel(q_ref, k_ref, v_ref, gate_ref, tril_ref, o_ref, sfin_ref, s_ref, *, nsub):
    c = DN_CHUNK
    dk = DN_HEAD_DIM

    @pl.when(pl.program_id(1) == 0)
    def _():
        s_ref[...] = jnp.zeros_like(s_ref)

    rowi = lax.broadcasted_iota(I32, (c, c), 0)
    coli = lax.broadcasted_iota(I32, (c, c), 1)
    causal = rowi >= coli
    strict = rowi > coli
    tril = tril_ref[...]
    pairs = [(j, h) for j in range(nsub) for h in range(DN_HEADS)]
    rows = [slice(j * c, (j + 1) * c) for j in range(nsub)]
    gate = [gate_ref[rows[j], :] for j in range(nsub)]
    gc_all = [_split3_dot_left(tril, gate[j]) for j in range(nsub)]
    gc_t = [gc_all[j].T for j in range(nsub)]

    def head(ref, j, h):
        return ref[rows[j], h * dk:(h + 1) * dk]

    qh = {p: head(q_ref, *p) * (dk ** -0.5) for p in pairs}
    kh = {p: head(k_ref, *p) for p in pairs}
    gcol = {(j, h): gc_all[j][:, h:h + 1] for j, h in pairs}
    beta = {(j, h): gate[j][:, DN_HEADS + h:DN_HEADS + h + 1] for j, h in pairs}
    decay = {(j, h): jnp.where(causal, jnp.exp(jnp.where(causal, gcol[j, h] - gc_t[j][h:h + 1, :], 0.0)), 0.0)
             for j, h in pairs}
    kb = {p: kh[p] * beta[p] for p in pairs}
    egc = {p: jnp.exp(gcol[p]) for p in pairs}
    gram = {p: _mm_nt(jnp.concatenate([kb[p], qh[p]], axis=0), kh[p]) for p in pairs}
    mat = {p: jnp.where(strict, gram[p][:c] * decay[p], 0.0) for p in pairs}
    qk = {p: jnp.where(causal, gram[p][c:] * decay[p], 0.0) for p in pairs}
    sol = {p: jnp.concatenate([head(v_ref, *p) * beta[p], kb[p] * egc[p]], axis=1) for p in pairs}
    levels = int(math.log2(c))
    for lvl in range(levels):
        if lvl < levels - 1:
            y = {p: _mm(mat[p], jnp.concatenate([sol[p], mat[p]], axis=1)) for p in pairs}
            mat = {p: y[p][:, 2 * dk:] for p in pairs}
            upd = {p: y[p][:, :2 * dk] for p in pairs}
        else:
            upd = {p: _mm(mat[p], sol[p]) for p in pairs}
        sol = {p: (sol[p] - upd[p]) if lvl == 0 else (sol[p] + upd[p]) for p in pairs}
    g_last = {(j, h): gc_all[j][c - 1:c, h:h + 1] for j, h in pairs}
    wq = {p: jnp.concatenate([sol[p][:, dk:], qh[p] * egc[p]], axis=0) for p in pairs}
    k_dec_t = {p: (kh[p] * jnp.exp(g_last[p] - gcol[p])).T for p in pairs}
    d_last = {p: jnp.exp(g_last[p]) for p in pairs}

    s = [s_ref[h] for h in range(DN_HEADS)]
    for j in range(nsub):
        ws = [_mm(wq[j, h], s[h]) for h in range(DN_HEADS)]
        v_new = [sol[j, h][:, :dk] - ws[h][:c] for h in range(DN_HEADS)]
        o_ref[rows[j], :] = jnp.concatenate(
            [ws[h][c:] + _mm(qk[j, h], v_new[h]) for h in range(DN_HEADS)], axis=1)
        s = [s[h] * d_last[j, h] + _mm(k_dec_t[j, h], v_new[h]) for h in range(DN_HEADS)]
    for h in range(DN_HEADS):
        s_ref[h] = s[h]
    sfin_ref[0] = s_ref[...]


def _split3_dot_left(b01, a):
    a1 = a.astype(BF16)
    r1 = a - a1.astype(F32)
    a2 = r1.astype(BF16)
    a3 = (r1 - a2.astype(F32)).astype(BF16)
    out = jnp.dot(b01, a3, preferred_element_type=F32)
    out = out + jnp.dot(b01, a2, preferred_element_type=F32)
    return out + jnp.dot(b01, a1, preferred_element_type=F32)


def _delta_prompt(q, k, v, gate, nb):
    n = q.shape[0]
    t = n // nb
    c = DN_CHUNK
    nsub = DELTA_SUBCHUNKS
    rows = nsub * c
    nc = t // rows
    row = lambda b, i: (b * nc + i, 0)
    tril = jnp.tril(jnp.ones((c, c), F32)).astype(BF16)
    return pl.pallas_call(
        functools.partial(_delta_chunk_kernel, nsub=nsub),
        grid=(nb, nc),
        in_specs=[pl.BlockSpec((rows, DN_WIDTH), row),
                  pl.BlockSpec((rows, DN_WIDTH), row),
                  pl.BlockSpec((rows, DN_WIDTH), row),
                  pl.BlockSpec((rows, LANES), row),
                  pl.BlockSpec((c, c), lambda b, i: (0, 0))],
        out_specs=[pl.BlockSpec((rows, DN_WIDTH), row),
                   pl.BlockSpec((1, DN_HEADS, DN_HEAD_DIM, DN_HEAD_DIM), lambda b, i: (b, 0, 0, 0))],
        out_shape=[jax.ShapeDtypeStruct((n, DN_WIDTH), F32),
                   jax.ShapeDtypeStruct((nb, DN_HEADS, DN_HEAD_DIM, DN_HEAD_DIM), F32)],
        scratch_shapes=[pltpu.VMEM((DN_HEADS, DN_HEAD_DIM, DN_HEAD_DIM), F32)],
        compiler_params=_cparams(("arbitrary", "arbitrary")),
        name="delta_prompt",
    )(q, k, v, gate, tril)


def _delta_step_kernel(q_ref, k_ref, v_ref, gate_ref, s0_ref, ex_ref, o_ref, s_ref, kx_ref, qx_ref, *, nt, nb):
    dk = DN_HEAD_DIM
    flat = dk * dk
    nv = flat // LANES
    p = pl.program_id(0)
    lane = lax.broadcasted_iota(I32, (SUBLANES, LANES), 1)
    low = lane < dk
    ex = ex_ref[...]

    for t in range(nt):
        rs = slice(t * nb, (t + 1) * nb)
        kx_ref[...] = jnp.dot(k_ref[rs, :].astype(BF16), ex, preferred_element_type=F32)
        qx_ref[...] = jnp.dot((q_ref[rs, :] * (dk ** -0.5)).astype(BF16), ex, preferred_element_type=F32)
        src_ref = s0_ref if t == 0 else s_ref

        def tile_body(bt, carry, t=t, src_ref=src_ref):
            b0 = pl.multiple_of(bt * SUBLANES, SUBLANES)
            r0 = pl.multiple_of(t * nb + b0, SUBLANES)
            gate = gate_ref[pl.ds(r0, SUBLANES), :]
            vv = v_ref[pl.ds(r0, SUBLANES), :]
            vsw = pltpu.roll(vv, dk, axis=1)
            o_pair = None
            for j in range(2):
                c0 = j * flat
                g = jnp.sum(jnp.where(lane == 2 * p + j, gate, 0.0), axis=1, keepdims=True)
                beta = jnp.sum(jnp.where(lane == 2 * p + j + DN_HEADS, gate, 0.0), axis=1, keepdims=True)
                a = jnp.exp(g)
                vdup = jnp.where(low, vv, vsw) if j == 0 else jnp.where(low, vsw, vv)
                s = [src_ref[pl.ds(b0, SUBLANES), c0 + i * LANES:c0 + (i + 1) * LANES] for i in range(nv)]
                kx = [kx_ref[pl.ds(b0, SUBLANES), c0 + i * LANES:c0 + (i + 1) * LANES] for i in range(nv)]
                ks = kx[0] * s[0]
                for i in range(1, nv):
                    ks = ks + kx[i] * s[i]
                ks = ks + pltpu.roll(ks, dk, axis=1)
                delta = beta * (vdup - a * ks)
                oh = None
                for i in range(nv):
                    si = a * s[i] + kx[i] * delta
                    s_ref[pl.ds(b0, SUBLANES), c0 + i * LANES:c0 + (i + 1) * LANES] = si
                    term = qx_ref[pl.ds(b0, SUBLANES), c0 + i * LANES:c0 + (i + 1) * LANES] * si
                    oh = term if oh is None else oh + term
                oh = oh + pltpu.roll(oh, dk, axis=1)
                o_pair = oh if j == 0 else jnp.where(low, o_pair, oh)
            o_ref[pl.ds(r0, SUBLANES), :] = o_pair
            return carry

        lax.fori_loop(0, nb // SUBLANES, tile_body, 0)


def _delta_sample(q, k, v, gate, s0, nb, nt):
    dk = DN_HEAD_DIM
    flat = dk * dk
    n = nt * nb
    col = np.arange(2 * flat)
    ex = np.arange(LANES)[:, None] == ((col // flat) * dk + (col % flat) // dk)[None, :]
    ex = jnp.asarray(ex, BF16)
    kern = functools.partial(_delta_step_kernel, nt=nt, nb=nb)
    pair = lambda p: (0, p)
    return pl.pallas_call(
        kern,
        grid=(DN_HEADS // 2,),
        in_specs=[pl.BlockSpec((n, LANES), pair),
                  pl.BlockSpec((n, LANES), pair),
                  pl.BlockSpec((n, LANES), pair),
                  pl.BlockSpec((n, LANES), lambda p: (0, 0)),
                  pl.BlockSpec((nb, 2 * flat), pair),
                  pl.BlockSpec((LANES, 2 * flat), lambda p: (0, 0))],
        out_specs=[pl.BlockSpec((n, LANES), pair),
                   pl.BlockSpec((nb, 2 * flat), pair)],
        out_shape=[jax.ShapeDtypeStruct((n, DN_WIDTH), F32),
                   jax.ShapeDtypeStruct((nb, DN_HEADS * flat), F32)],
        scratch_shapes=[pltpu.VMEM((nb, 2 * flat), F32),
                        pltpu.VMEM((nb, 2 * flat), F32)],
        compiler_params=_cparams(("arbitrary",)),
        name="delta_sample",
    )(q, k, v, gate, s0, ex)


def _s5_kernel(u_ref, bre_ref, bim_ref, lam_ref, c_ref, d_ref, h0_ref, y_ref, hfin_ref,
               bw_ref, ab_ref, x_ref, h_ref, *, nb, tt):
    p2 = S5_FLAT

    @pl.when(pl.program_id(0) == 0)
    def _():
        lr = lam_ref[0:1, :]
        li = lam_ref[1:2, :]
        dt = jnp.exp(lam_ref[2:3, :])
        mag = jnp.exp(lr * dt)
        ab_re = mag * jnp.cos(li * dt)
        ab_im = mag * jnp.sin(li * dt)
        den = lr * lr + li * li
        nr = ab_re - 1.0
        ni = ab_im
        f_re = (nr * lr + ni * li) / den
        f_im = (ni * lr - nr * li) / den
        ab_ref[0:1, :] = ab_re
        ab_ref[1:2, :] = ab_im
        bre = bre_ref[...]
        bim = bim_ref[...]
        bw_ref[:, 0:p2] = (bre * f_re - bim * f_im).astype(BF16)
        bw_ref[:, p2:2 * p2] = (bim * f_re + bre * f_im).astype(BF16)
        h_ref[...] = h0_ref[...]

    u = u_ref[...]
    cw = S5_WIDTH // S5_SUPER
    sw = S5_FLAT // S5_SUPER
    for part in (0, p2):
        for b in range(S5_SUPER):
            x_ref[:, part + b * sw:part + (b + 1) * sw] = jnp.dot(
                u[:, b * cw:(b + 1) * cw], bw_ref[b * cw:(b + 1) * cw, part + b * sw:part + (b + 1) * sw],
                preferred_element_type=F32)
    a_re = ab_ref[0:1, :]
    a_im = ab_ref[1:2, :]

    if nb == SUBLANES:
        wsl = p2 // S5_SCAN_SPLIT
        for sp in range(S5_SCAN_SPLIT):
            c0 = sp * wsl
            are = jnp.broadcast_to(a_re[:, c0:c0 + wsl], (nb, wsl))
            aim = jnp.broadcast_to(a_im[:, c0:c0 + wsl], (nb, wsl))

            def step(t, carry, c0=c0, are=are, aim=aim):
                hr, hi = carry
                r0 = pl.multiple_of(t * nb, nb)
                nr = are * hr - aim * hi + x_ref[pl.ds(r0, nb), c0:c0 + wsl]
                ni = are * hi + aim * hr + x_ref[pl.ds(r0, nb), p2 + c0:p2 + c0 + wsl]
                x_ref[pl.ds(r0, nb), c0:c0 + wsl] = nr
                x_ref[pl.ds(r0, nb), p2 + c0:p2 + c0 + wsl] = ni
                return nr, ni

            hr, hi = lax.fori_loop(0, tt, step, (h_ref[:, c0:c0 + wsl], h_ref[:, p2 + c0:p2 + c0 + wsl]),
                                   unroll=2)
            h_ref[:, c0:c0 + wsl] = hr
            h_ref[:, p2 + c0:p2 + c0 + wsl] = hi
    else:
        for t in range(tt):
            rs = slice(t * nb, (t + 1) * nb)
            hr = h_ref[:, 0:p2]
            hi = h_ref[:, p2:2 * p2]
            nr = a_re * hr - a_im * hi + x_ref[rs, 0:p2]
            ni = a_re * hi + a_im * hr + x_ref[rs, p2:2 * p2]
            h_ref[:, 0:p2] = nr
            h_ref[:, p2:2 * p2] = ni
            x_ref[rs, 0:p2] = nr
            x_ref[rs, p2:2 * p2] = ni

    for b in range(S5_SUPER):
        cols = slice(b * cw, (b + 1) * cw)
        y = None
        for part in (0, p2):
            rws = slice(part + b * sw, part + (b + 1) * sw)
            term = jnp.dot(x_ref[:, rws].astype(BF16), c_ref[rws, cols], preferred_element_type=F32)
            y = term if y is None else y + term
        y_ref[:, cols] = y + d_ref[:, cols] * u[:, cols].astype(F32)
    hfin_ref[...] = h_ref[...]


def _s5(u, bre, bim, lam, cmat, dvec, h0, nb):
    n = u.shape[0]
    t = n // nb
    tt = min(ROW_TILE // nb, t)
    rows = tt * nb
    const = lambda i: (0, 0)
    kern = functools.partial(_s5_kernel, nb=nb, tt=tt)
    return pl.pallas_call(
        kern,
        grid=(t // tt,),
        in_specs=[pl.BlockSpec((rows, S5_WIDTH), lambda i: (i, 0)),
                  pl.BlockSpec((S5_WIDTH, S5_FLAT), const),
                  pl.BlockSpec((S5_WIDTH, S5_FLAT), const),
                  pl.BlockSpec((SUBLANES, S5_FLAT), const),
                  pl.BlockSpec((2 * S5_FLAT, S5_WIDTH), const),
                  pl.BlockSpec((1, S5_WIDTH), const),
                  pl.BlockSpec((nb, 2 * S5_FLAT), const)],
        out_specs=[pl.BlockSpec((rows, S5_WIDTH), lambda i: (i, 0)),
                   pl.BlockSpec((nb, 2 * S5_FLAT), const)],
        out_shape=[jax.ShapeDtypeStruct((n, S5_WIDTH), F32),
                   jax.ShapeDtypeStruct((nb, 2 * S5_FLAT), F32)],
        scratch_shapes=[pltpu.VMEM((S5_WIDTH, 2 * S5_FLAT), BF16),
                        pltpu.VMEM((SUBLANES, S5_FLAT), F32),
                        pltpu.VMEM((rows, 2 * S5_FLAT), F32),
                        pltpu.VMEM((nb, 2 * S5_FLAT), F32)],
        compiler_params=_cparams(("arbitrary",)),
        name="s5",
    )(u, bre, bim, lam, cmat, dvec, h0)


def _postmix_kernel(xp_ref, op_ref, zp_ref, ysp_ref, gap_ref, gbp_ref,
                    xs_ref, os_ref, zs_ref, yss_ref, gas_ref, gbs_ref, *rest, nblk_p):
    @pl.when(pl.program_id(0) < nblk_p)
    def _():
        _postmix_body(xp_ref, op_ref, zp_ref, ysp_ref, gap_ref, gbp_ref, *rest)

    @pl.when(pl.program_id(0) >= nblk_p)
    def _():
        _postmix_body(xs_ref, os_ref, zs_ref, yss_ref, gas_ref, gbs_ref, *rest)


def _postmix_body(x_ref, o_ref, z_ref, ys_ref, ga_ref, gb_ref, hw_ref, seg_ref, wa_ref, wglu_ref, wb_ref,
                  wo_ref, nf_ref, wr_ref, x1_ref, hn_ref, ridx_ref, rw_ref):
    o = o_ref[...]
    ms = jnp.dot((o * o).astype(BF16), seg_ref[...], preferred_element_type=F32) * (1.0 / DN_HEAD_DIM)
    on = o * lax.rsqrt(ms + RMS_EPS) * hw_ref[...]
    z = z_ref[...].astype(F32)
    oa = on * (z * jax.nn.sigmoid(z))
    y_a = _mm(oa, wa_ref[...])
    ys = jax.nn.gelu(ys_ref[...])
    ys = ys * jax.nn.sigmoid(_mm(ys, wglu_ref[...]))
    y_b = _mm(ys, wb_ref[...])
    mixed = jax.nn.sigmoid(ga_ref[...].astype(F32)) * y_a + jax.nn.sigmoid(gb_ref[...].astype(F32)) * y_b
    x1 = x_ref[...] + _mm(mixed, wo_ref[...])
    x1_ref[...] = x1
    hn = x1 * lax.rsqrt(jnp.mean(x1 * x1, axis=-1, keepdims=True) + RMS_EPS) * nf_ref[...]
    _slab_store(hn_ref, hn)

    logits = lax.dot_general(wr_ref[...], hn, (((1,), (1,)), ((), ())),
                             precision=lax.Precision.HIGHEST, preferred_element_type=F32)
    coarse = logits[N_EXPERTS:N_EXPERTS + MOE_GROUPS, :]
    cm = jnp.max(coarse, axis=0, keepdims=True)
    ce = jnp.exp(coarse - cm)
    pc = ce / jnp.sum(ce, axis=0, keepdims=True)
    p_sel = jnp.max(pc, axis=0, keepdims=True)
    gi = lax.broadcasted_iota(I32, pc.shape, 0)
    g_sel = jnp.min(jnp.where(pc == p_sel, gi, MOE_GROUPS), axis=0, keepdims=True)
    fine = jnp.zeros((EXPERTS_PER_GROUP, logits.shape[1]), F32)
    for g in range(MOE_GROUPS):
        fine = fine + jnp.where(g_sel == g, logits[g * EXPERTS_PER_GROUP:(g + 1) * EXPERTS_PER_GROUP, :], 0.0)
    fm = jnp.max(fine, axis=0, keepdims=True)
    fe = jnp.exp(fine - fm)
    pf = fe / jnp.sum(fe, axis=0, keepdims=True)
    ei = lax.broadcasted_iota(I32, pf.shape, 0)
    v1 = jnp.max(pf, axis=0, keepdims=True)
    i1 = jnp.min(jnp.where(pf == v1, ei, EXPERTS_PER_GROUP), axis=0, keepdims=True)
    rest = jnp.where(ei == i1, -1.0, pf)
    v2 = jnp.max(rest, axis=0, keepdims=True)
    i2 = jnp.min(jnp.where(rest == v2, ei, EXPERTS_PER_GROUP), axis=0, keepdims=True)
    tot = v1 + v2
    ridx_ref[0:1, :] = g_sel * EXPERTS_PER_GROUP + i1
    ridx_ref[1:2, :] = g_sel * EXPERTS_PER_GROUP + i2
    rw_ref[0:1, :] = v1 / tot * p_sel
    rw_ref[1:2, :] = v2 / tot * p_sel


def _postmix(prompt, sample, weights, nb):
    n_p = prompt[0].shape[0]
    n_s = sample[0].shape[0]
    t = n_p // nb
    tt = min(ROW_TILE, t, n_s)
    nt = t // tt
    nblk_p = n_p // tt
    nblk = nblk_p + n_s // tt
    n_total = n_p + n_s
    prow = lambda i: (jnp.minimum(i, nblk_p - 1), 0)
    pys = lambda i: (jnp.minimum(i, nblk_p - 1) % nt, jnp.minimum(i, nblk_p - 1) // nt)
    srow = lambda i: (jnp.maximum(i - nblk_p, 0), 0)
    const = lambda i: (0, 0)

    def stream_specs(row, ysmap):
        return [pl.BlockSpec((tt, D_MODEL), row),
                pl.BlockSpec((tt, DN_WIDTH), row),
                pl.BlockSpec((tt, DN_WIDTH), row),
                pl.BlockSpec((tt, S5_WIDTH), ysmap),
                pl.BlockSpec((tt, D_MODEL), row),
                pl.BlockSpec((tt, D_MODEL), row)]

    weight_specs = [pl.BlockSpec((1, DN_WIDTH), const),
                    pl.BlockSpec((DN_WIDTH, DN_WIDTH), const),
                    pl.BlockSpec((DN_WIDTH, D_MODEL), const),
                    pl.BlockSpec((S5_WIDTH, S5_WIDTH), const),
                    pl.BlockSpec((S5_WIDTH, D_MODEL), const),
                    pl.BlockSpec((D_MODEL, D_MODEL), const),
                    pl.BlockSpec((1, D_MODEL), const),
                    pl.BlockSpec((ROUTER_ROWS, D_MODEL), const)]
    xp, op, zp, ysp, gap, gbp = prompt
    return pl.pallas_call(
        functools.partial(_postmix_kernel, nblk_p=nblk_p),
        grid=(nblk,),
        in_specs=stream_specs(prow, pys) + stream_specs(srow, srow) + weight_specs,
        out_specs=[pl.BlockSpec((tt, D_MODEL), lambda i: (i, 0)),
                   pl.BlockSpec((tt * ROW_SLAB, LANES), lambda i: (i, 0)),
                   pl.BlockSpec((TOP_K, tt), lambda i: (0, i)),
                   pl.BlockSpec((TOP_K, tt), lambda i: (0, i))],
        out_shape=[jax.ShapeDtypeStruct((n_total, D_MODEL), F32),
                   jax.ShapeDtypeStruct((n_total * ROW_SLAB, LANES), F32),
                   jax.ShapeDtypeStruct((TOP_K, n_total), I32),
                   jax.ShapeDtypeStruct((TOP_K, n_total), F32)],
        compiler_params=_cparams(("arbitrary",)),
        name="postmix",
    )(xp, op, zp, ysp.reshape(t, nb * S5_WIDTH), gap, gbp, *sample, *weights)


def _wait_slabs(buf, sem):
    pltpu.make_async_copy(buf, buf, sem).wait()


def _moe_kernel(texp_ref, tsrc_ref, tnv_ref, otok_ref, hn_hbm, wu_ref, wd_ref, y_ref, xbuf0, xbuf1, gsem):
    del texp_ref
    i = pl.program_id(0)
    nsteps = pl.num_programs(0)
    tm = MOE_TILE
    rs = ROW_SLAB
    slot = lax.rem(i, 2)
    nv = tnv_ref[i]
    prev_valid = jnp.logical_and(i >= 1, tnv_ref[jnp.maximum(i - 1, 0)] > 0)
    xbuf = (xbuf0, xbuf1)

    def start_gather(step, sl):
        src0 = tsrc_ref[step]
        for r in range(tm):
            tok8 = pl.multiple_of(otok_ref[src0 + r], rs)
            pltpu.make_async_copy(hn_hbm.at[pl.ds(tok8, rs), :], xbuf[sl].at[pl.ds(r * rs, rs), :],
                                  gsem.at[sl]).start()

    @pl.when(i == 0)
    def _():
        start_gather(0, 0)

    @pl.when(nv == 0)
    def _():
        y_ref[...] = jnp.zeros_like(y_ref)

    for sl in range(2):
        @pl.when(jnp.logical_and(slot == sl, nv > 0))
        def _():
            _wait_slabs(xbuf[sl], gsem.at[sl])
            start_gather(jnp.minimum(i + 1, nsteps - 1), 1 - sl)
            x = _slab_load(xbuf[sl], tm).astype(BF16)
            hu = jnp.dot(x, wu_ref[0].astype(BF16), preferred_element_type=F32)
            gate = hu[:, :EXPERT_FF]
            up = hu[:, EXPERT_FF:]
            act = gate * jax.nn.sigmoid(gate) * up
            _slab_store(y_ref, jnp.dot(act.astype(BF16), wd_ref[0].astype(BF16), preferred_element_type=F32))

        @pl.when(jnp.logical_and(slot == sl, jnp.logical_and(nv == 0, prev_valid)))
        def _():
            _wait_slabs(xbuf[sl], gsem.at[sl])

        @pl.when(jnp.logical_and(slot == sl, jnp.logical_and(i == nsteps - 1, nv > 0)))
        def _():
            _wait_slabs(xbuf[1 - sl], gsem.at[1 - sl])


def _moe(hn, w_up, w_down, texp, tsrc, tnv, otok):
    ntiles = texp.shape[0]
    wmap = lambda i, te, ts, tn, ot: (te[i], 0, 0)
    grid_spec = pltpu.PrefetchScalarGridSpec(
        num_scalar_prefetch=4,
        grid=(ntiles,),
        in_specs=[pl.BlockSpec(memory_space=pl.ANY),
                  pl.BlockSpec((1, D_MODEL, 2 * EXPERT_FF), wmap),
                  pl.BlockSpec((1, EXPERT_FF, D_MODEL), wmap)],
        out_specs=pl.BlockSpec((MOE_TILE * ROW_SLAB, LANES), lambda i, te, ts, tn, ot: (i, 0)),
        scratch_shapes=[pltpu.VMEM((MOE_TILE * ROW_SLAB, LANES), F32),
                        pltpu.VMEM((MOE_TILE * ROW_SLAB, LANES), F32),
                        pltpu.SemaphoreType.DMA((2,))])
    return pl.pallas_call(
        _moe_kernel,
        grid_spec=grid_spec,
        out_shape=jax.ShapeDtypeStruct((ntiles * MOE_TILE * ROW_SLAB, LANES), F32),
        compiler_params=_cparams(("arbitrary",)),
        name="moe",
    )(texp, tsrc, tnv, otok, hn, w_up, w_down)


def _combine_kernel(pos_ref, x1_ref, ys_hbm, w_ref, nw_ref, outp_ref, outs_ref, ybuf0, ybuf1, sem, *, nblk_p):
    i = pl.program_id(0)
    nsteps = pl.num_programs(0)
    tt = x1_ref.shape[0]
    rs = ROW_SLAB
    slot = lax.rem(i, 2)
    ybuf = (ybuf0, ybuf1)

    def start_gather(step, sl):
        base = step * (tt * TOP_K)
        for r in range(tt * TOP_K):
            j, s = divmod(r, TOP_K)
            p8 = pl.multiple_of(pos_ref[base + r], rs)
            pltpu.make_async_copy(ys_hbm.at[pl.ds(p8, rs), :], ybuf[sl].at[pl.ds((s * tt + j) * rs, rs), :],
                                  sem.at[sl]).start()

    @pl.when(i == 0)
    def _():
        start_gather(0, 0)

    for sl in range(2):
        @pl.when(slot == sl)
        def _():
            _wait_slabs(ybuf[sl], sem.at[sl])
            start_gather(jnp.minimum(i + 1, nsteps - 1), 1 - sl)
            w = w_ref[...]
            y0 = _slab_load(ybuf[sl], tt, 0)
            y1 = _slab_load(ybuf[sl], tt, tt * rs)
            x = x1_ref[...] + w[:, 0:1] * y0 + w[:, 1:2] * y1
            res = x * lax.rsqrt(jnp.mean(x * x, axis=-1, keepdims=True) + RMS_EPS) * nw_ref[...]

            @pl.when(i < nblk_p)
            def _():
                outp_ref[...] = res

            @pl.when(i >= nblk_p)
            def _():
                outs_ref[...] = res

        @pl.when(jnp.logical_and(slot == sl, i == nsteps - 1))
        def _():
            _wait_slabs(ybuf[1 - sl], sem.at[1 - sl])


def _combine(x1, ysorted, pos8, wtok, nw, n_p):
    n = x1.shape[0]
    tt = math.gcd(math.gcd(n_p, n - n_p), COMBINE_TILE)
    nblk_p = n_p // tt
    grid_spec = pltpu.PrefetchScalarGridSpec(
        num_scalar_prefetch=1,
        grid=(n // tt,),
        in_specs=[pl.BlockSpec((tt, D_MODEL), lambda i, ps: (i, 0)),
                  pl.BlockSpec(memory_space=pl.ANY),
                  pl.BlockSpec((tt, TOP_K), lambda i, ps: (i, 0)),
                  pl.BlockSpec((1, D_MODEL), lambda i, ps: (0, 0))],
        out_specs=[pl.BlockSpec((tt, D_MODEL), lambda i, ps: (jnp.minimum(i, nblk_p - 1), 0)),
                   pl.BlockSpec((tt, D_MODEL), lambda i, ps: (jnp.maximum(i - nblk_p, 0), 0))],
        scratch_shapes=[pltpu.VMEM((tt * TOP_K * ROW_SLAB, LANES), F32),
                        pltpu.VMEM((tt * TOP_K * ROW_SLAB, LANES), F32),
                        pltpu.SemaphoreType.DMA((2,))])
    return pl.pallas_call(
        functools.partial(_combine_kernel, nblk_p=nblk_p),
        grid_spec=grid_spec,
        out_shape=[jax.ShapeDtypeStruct((n_p, D_MODEL), F32),
                   jax.ShapeDtypeStruct((n - n_p, D_MODEL), F32)],
        compiler_params=_cparams(("arbitrary",)),
        name="combine",
    )(pos8, x1, ysorted, wtok, nw)


def _route_plan(ridx, n_tok):
    tm = MOE_TILE
    n_assign = n_tok * TOP_K
    ntiles = n_assign // tm + N_EXPERTS
    e_flat = ridx.T.reshape(n_assign)
    order = jnp.argsort(e_flat, stable=True).astype(I32)
    counts = jnp.sum((e_flat[:, None] == jnp.arange(N_EXPERTS, dtype=I32)[None, :]).astype(I32), axis=0)
    cstart = jnp.cumsum(counts) - counts
    tiles_e = (counts + tm - 1) // tm
    tend = jnp.cumsum(tiles_e)
    tstart = tend - tiles_e
    tile_id = jnp.arange(ntiles, dtype=I32)
    texp = jnp.minimum(jnp.sum((tile_id[:, None] >= tend[None, :]).astype(I32), axis=1), N_EXPERTS - 1)
    onehot = (texp[:, None] == jnp.arange(N_EXPERTS, dtype=I32)[None, :]).astype(I32)
    pick = lambda v: jnp.sum(onehot * v[None, :], axis=1)
    done = (tile_id - pick(tstart)) * tm
    tnv = jnp.where(tile_id < tend[-1], jnp.clip(pick(counts) - done, 0, tm), 0)
    tsrc = jnp.where(tnv > 0, pick(cstart) + done, 0)
    otok8 = jnp.concatenate([(order // TOP_K) * ROW_SLAB, jnp.zeros((tm,), I32)])
    rank = jnp.argsort(order).astype(I32)
    eoh = (e_flat[:, None] == jnp.arange(N_EXPERTS, dtype=I32)[None, :]).astype(I32)
    pos8 = (rank + jnp.sum(eoh * (tstart * tm - cstart)[None, :], axis=1)) * ROW_SLAB
    return texp.astype(I32), tsrc.astype(I32), tnv.astype(I32), otok8, pos8.astype(I32)


def _block_diag(m):
    g, a, b = m.shape
    eye = jnp.eye(g, dtype=m.dtype)
    return (eye[:, None, :, None] * m[:, :, None, :]).reshape(g * a, g * b)


def kernel(x_prompt, x_sample, state_conv, state_delta, state_ssm_re, state_ssm_im, norm_mix_w, w_in, conv_w, a_log, dt_bias, head_norm_w, w_a_up, s5_lambda_re, s5_lambda_im, s5_log_step, s5_b_re, s5_b_im, s5_c_re, s5_c_im, s5_d, w_glu, w_b_up, w_o, norm_ffn_w, w_router_coarse, w_router_fine, w_expert_up, w_expert_down, norm_final_w):
    bp, tp, _ = x_prompt.shape
    bs, ts, _ = x_sample.shape
    n_p = bp * tp
    n_s = bs * ts
    n_tok = n_p + n_s
    l = 0

    w = w_in[l]
    cuts = np.cumsum([0, QKV_DIM, DN_WIDTH, DN_HEADS, DN_HEADS, S5_WIDTH, D_MODEL, D_MODEL])
    w_qkv, w_z, w_a, w_b, w_u, w_ga, w_gb = [w[:, cuts[i]:cuts[i + 1]] for i in range(7)]
    w_ab = jnp.concatenate([w_a, w_b, jnp.zeros((D_MODEL, LANES - 2 * DN_HEADS), F32)], axis=1)
    wcat = jnp.concatenate([w_qkv, w_z, w_u, w_ga, w_gb, w_ab], axis=1).astype(BF16)
    nw_mix = norm_mix_w[l].reshape(1, D_MODEL)
    pad8 = lambda v: jnp.concatenate([v, jnp.zeros((LANES - DN_HEADS,), F32)]).reshape(1, LANES)
    gate_p = jnp.concatenate([pad8(a_log[l]), pad8(dt_bias[l])], axis=0)
    seg = _block_diag(jnp.ones((DN_HEADS, DN_HEAD_DIM, DN_HEAD_DIM), BF16))
    bre = _block_diag(jnp.swapaxes(s5_b_re[l], 1, 2))
    bim = _block_diag(jnp.swapaxes(s5_b_im[l], 1, 2))
    lam = jnp.concatenate([s5_lambda_re[l].reshape(1, S5_FLAT), s5_lambda_im[l].reshape(1, S5_FLAT),
                           jnp.repeat(s5_log_step[l], S5_STATE).reshape(1, S5_FLAT),
                           jnp.zeros((SUBLANES - 3, S5_FLAT), F32)], axis=0)
    cmat = jnp.concatenate([_block_diag(jnp.swapaxes(s5_c_re[l], 1, 2)),
                            -_block_diag(jnp.swapaxes(s5_c_im[l], 1, 2))], axis=0).astype(BF16)
    dvec = s5_d[l].reshape(1, S5_WIDTH)
    hw = jnp.tile(head_norm_w[l], DN_HEADS).reshape(1, DN_WIDTH)
    wr = jnp.concatenate([w_router_fine[l].T, w_router_coarse[l].T,
                          jnp.zeros((ROUTER_ROWS - N_EXPERTS - MOE_GROUPS, D_MODEL), F32)], axis=0)
    pm_weights = (hw, seg, w_a_up[l].astype(BF16), w_glu[l].astype(BF16), w_b_up[l].astype(BF16),
                  w_o[l].astype(BF16), norm_ffn_w[l].reshape(1, D_MODEL), wr)

    xp2 = x_prompt.reshape(n_p, D_MODEL)
    qkv_p, z_p, u_p, ga_p, gb_p, ab_p = _inproj(xp2, nw_mix, wcat, bp)
    q_p, k_p, v_p, gates_p, conv_p = _prep(qkv_p, jnp.zeros((bp, SUBLANES, QKV_DIM), F32), conv_w[l], ab_p,
                                              gate_p, seg, bp, 1)
    o_p, delta_p = _delta_prompt(q_p, k_p, v_p, gates_p, bp)
    ys_p, h_p = _s5(u_p, bre, bim, lam, cmat, dvec, jnp.zeros((bp, 2 * S5_FLAT), F32), bp)

    xs2 = jnp.swapaxes(x_sample, 0, 1).reshape(n_s, D_MODEL)
    qkv_s, z_s, u_s, ga_s, gb_s, ab_s = _inproj(xs2, nw_mix, wcat, 1)
    cinit_s = jnp.swapaxes(state_conv[l], 0, 1).reshape(1, (CONV_W - 1) * bs, QKV_DIM)
    q_s, k_s, v_s, gate_s, conv_s = _prep(qkv_s, cinit_s, conv_w[l], ab_s, gate_p, seg, 1, bs)
    s0 = state_delta[l].reshape(bs, DN_HEADS * DN_HEAD_DIM * DN_HEAD_DIM)
    o_s, delta_s = _delta_sample(q_s, k_s, v_s, gate_s, s0, bs, ts)
    h0_s = jnp.concatenate([state_ssm_re[l].reshape(bs, S5_FLAT), state_ssm_im[l].reshape(bs, S5_FLAT)], axis=1)
    ys_s, h_s = _s5(u_s, bre, bim, lam, cmat, dvec, h0_s, bs)
    x1, hn, ridx, rw = _postmix((xp2, o_p, z_p, ys_p, ga_p, gb_p), (xs2, o_s, z_s, ys_s, ga_s, gb_s),
                                pm_weights, bp)

    texp, tsrc, tnv, otok8, pos8 = _route_plan(ridx, n_tok)
    ysorted = _moe(hn, w_expert_up[l], w_expert_down[l], texp, tsrc, tnv, otok8)
    y_p, y_s = _combine(x1, ysorted, pos8, rw.T, norm_final_w.reshape(1, D_MODEL), n_p)

    y_prompt = y_p.reshape(bp, tp, D_MODEL)
    y_sample = jnp.swapaxes(y_s.reshape(ts, bs, D_MODEL), 0, 1)
    conv_sample = jnp.swapaxes(conv_s.reshape(CONV_W - 1, bs, QKV_DIM), 0, 1)
    return (y_prompt, y_sample,
            conv_p[None], delta_p[None],
            h_p[:, :S5_FLAT].reshape(1, bp, S5_GROUPS, S5_STATE), h_p[:, S5_FLAT:].reshape(1, bp, S5_GROUPS, S5_STATE),
            conv_sample[None], delta_s.reshape(1, bs, DN_HEADS, DN_HEAD_DIM, DN_HEAD_DIM),
            h_s[:, :S5_FLAT].reshape(1, bs, S5_GROUPS, S5_STATE), h_s[:, S5_FLAT:].reshape(1, bs, S5_GROUPS, S5_STATE))
```

```python
import functools
import math

import jax
import jax.numpy as jnp
import numpy as np
from jax import lax
from jax.experimental import pallas as pl
from jax.experimental.pallas import tpu as pltpu

F32 = jnp.float32
BF16 = jnp.bfloat16
I32 = jnp.int32

D_MODEL = 1024
DN_HEADS = 8
DN_HEAD_DIM = 64
DN_WIDTH = DN_HEADS * DN_HEAD_DIM
QKV_DIM = 3 * DN_WIDTH
CONV_W = 4
DN_CHUNK = 64
S5_GROUP_CH = 16
S5_WIDTH = D_MODEL // 2
S5_GROUPS = S5_WIDTH // S5_GROUP_CH
S5_STATE = 64
S5_FLAT = S5_GROUPS * S5_STATE
MOE_GROUPS = 4
EXPERTS_PER_GROUP = 8
N_EXPERTS = MOE_GROUPS * EXPERTS_PER_GROUP
TOP_K = 2
EXPERT_FF = 256
RMS_EPS = 1e-6
L2_EPS = 1e-6

LANES = 128
SUBLANES = 8
VMEM_LIMIT = 56 * 1024 * 1024

C_QKV, C_Z, C_U, C_GA, C_GB, C_AB = 0, 1536, 2048, 2560, 3584, 4608
IN_PACKED = C_AB + LANES

ROW_TILE = 512
MOE_TILE = 256
COMBINE_TILE = 256
DMA_QUEUES = 2
DELTA_SUBCHUNKS = 4
S5_SUPER = 2
S5_SCAN_SPLIT = 2
ROUTER_ROWS = 40


def _mm(a, b):
    return jnp.dot(a.astype(BF16), b.astype(BF16), preferred_element_type=F32)


def _mm_nt(a, b):
    return lax.dot_general(a.astype(BF16), b.astype(BF16), (((1,), (1,)), ((), ())),
                           preferred_element_type=F32)


def _split3_dot(a, b01):
    a1 = a.astype(BF16)
    r1 = a - a1.astype(F32)
    a2 = r1.astype(BF16)
    a3 = (r1 - a2.astype(F32)).astype(BF16)
    out = jnp.dot(a3, b01, preferred_element_type=F32)
    out = out + jnp.dot(a2, b01, preferred_element_type=F32)
    return out + jnp.dot(a1, b01, preferred_element_type=F32)


def _cparams(sem):
    return pltpu.CompilerParams(dimension_semantics=sem, vmem_limit_bytes=VMEM_LIMIT)


ROW_SLAB = D_MODEL // LANES


def _slab_load(ref, rows, first=0, pitch=ROW_SLAB):
    return jnp.concatenate([ref[pl.ds(first + j, rows, stride=pitch), :] for j in range(ROW_SLAB)], axis=1)


def _slab_store(ref, x):
    for j in range(ROW_SLAB):
        ref[pl.ds(j, x.shape[0], stride=ROW_SLAB), :] = x[:, j * LANES:(j + 1) * LANES]


def _softplus(x):
    return jnp.maximum(x, 0.0) + jnp.log1p(jnp.exp(-jnp.abs(x)))


def _inprep_kernel(x_ref, nw_ref, w_ref, cinit_ref, cw_ref, gp_ref, seg_ref,
                   q_ref, k_ref, v_ref, gate_ref, cnew_ref, z_ref, u_ref, ga_ref, gb_ref, xp_ref,
                   *, shift, rc, rows):
    x = x_ref[...]
    h = x * lax.rsqrt(jnp.mean(x * x, axis=-1, keepdims=True) + RMS_EPS) * nw_ref[...]
    hb = h.astype(BF16)

    def proj(lo, hi):
        return jnp.dot(hb, w_ref[:, lo:hi], preferred_element_type=F32)

    @pl.when(pl.program_id(1) == 0)
    def _():
        xp_ref[0:rc, :] = cinit_ref[0]

    xp_ref[rc:rc + rows, :] = proj(C_QKV, C_Z)
    ab = proj(C_AB, IN_PACKED)
    z_ref[...] = proj(C_Z, C_U).astype(z_ref.dtype)
    u_ref[...] = proj(C_U, C_GA).astype(u_ref.dtype)
    ga_ref[...] = proj(C_GA, C_GB).astype(ga_ref.dtype)
    gb_ref[...] = proj(C_GB, C_AB).astype(gb_ref.dtype)
    acc = None
    for i in range(CONV_W):
        lo = rc + (i - (CONV_W - 1)) * shift
        term = xp_ref[lo:lo + rows, :] * cw_ref[i:i + 1, :]
        acc = term if acc is None else acc + term
    y = acc * jax.nn.sigmoid(acc)
    keep = (CONV_W - 1) * shift
    cnew_ref[0] = xp_ref[rc + rows - keep:rc + rows, :]
    xp_ref[0:rc, :] = xp_ref[rows:rows + rc, :]

    seg = seg_ref[...]
    q = y[:, 0:DN_WIDTH]
    k = y[:, DN_WIDTH:2 * DN_WIDTH]
    q_ref[...] = q * lax.rsqrt(jnp.dot((q * q).astype(BF16), seg, preferred_element_type=F32) + L2_EPS)
    k_ref[...] = k * lax.rsqrt(jnp.dot((k * k).astype(BF16), seg, preferred_element_type=F32) + L2_EPS)
    v_ref[...] = y[:, 2 * DN_WIDTH:]

    g = -jnp.exp(gp_ref[0:1, :]) * _softplus(ab + gp_ref[1:2, :])
    beta = jax.nn.sigmoid(ab)
    lane = lax.broadcasted_iota(I32, ab.shape, 1)
    gate_ref[...] = jnp.where(lane < DN_HEADS, g, beta)


def _inprep(x2d, nw, wcat, cinit, conv_w, gate_p, seg, nb, shift):
    n = x2d.shape[0]
    r = n // nb
    rows = min(ROW_TILE, r)
    nt = r // rows
    rc = cinit.shape[1]
    keep = (CONV_W - 1) * shift
    row = lambda b, i: (b * nt + i, 0)
    const = lambda b, i: (0, 0)
    kern = functools.partial(_inprep_kernel, shift=shift, rc=rc, rows=rows)
    outs = pl.pallas_call(
        kern,
        grid=(nb, nt),
        in_specs=[pl.BlockSpec((rows, D_MODEL), row),
                  pl.BlockSpec((1, D_MODEL), const),
                  pl.BlockSpec((D_MODEL, IN_PACKED), const),
                  pl.BlockSpec((1, rc, QKV_DIM), lambda b, i: (b, 0, 0)),
                  pl.BlockSpec((CONV_W, QKV_DIM), const),
                  pl.BlockSpec((2, LANES), const),
                  pl.BlockSpec((DN_WIDTH, DN_WIDTH), const)],
        out_specs=[pl.BlockSpec((rows, DN_WIDTH), row),
                   pl.BlockSpec((rows, DN_WIDTH), row),
                   pl.BlockSpec((rows, DN_WIDTH), row),
                   pl.BlockSpec((rows, LANES), row),
                   pl.BlockSpec((1, keep, QKV_DIM), lambda b, i: (b, 0, 0)),
                   pl.BlockSpec((rows, DN_WIDTH), row),
                   pl.BlockSpec((rows, S5_WIDTH), lambda b, i: (i, b)),
                   pl.BlockSpec((rows, D_MODEL), row),
                   pl.BlockSpec((rows, D_MODEL), row)],
        out_shape=[jax.ShapeDtypeStruct((n, DN_WIDTH), F32),
                   jax.ShapeDtypeStruct((n, DN_WIDTH), F32),
                   jax.ShapeDtypeStruct((n, DN_WIDTH), F32),
                   jax.ShapeDtypeStruct((n, LANES), F32),
                   jax.ShapeDtypeStruct((nb, keep, QKV_DIM), F32),
                   jax.ShapeDtypeStruct((n, DN_WIDTH), BF16),
                   jax.ShapeDtypeStruct((r, nb * S5_WIDTH), BF16),
                   jax.ShapeDtypeStruct((n, D_MODEL), BF16),
                   jax.ShapeDtypeStruct((n, D_MODEL), BF16)],
        scratch_shapes=[pltpu.VMEM((rc + rows, QKV_DIM), F32)],
        compiler_params=_cparams(("arbitrary", "arbitrary")),
        name="inprep",
    )(x2d, nw, wcat, cinit, conv_w, gate_p, seg)
    q, k, v, gate, cnew, z, u, ga, gb = outs
    return q, k, v, gate, cnew, z, u.reshape(r * nb, S5_WIDTH), ga, gb


def _delta_chunk_kernel(q_ref, k_ref, v_ref, gate_ref, tril_ref, o_ref, sfin_ref, s_ref, *, nsub):
    c = DN_CHUNK
    dk = DN_HEAD_DIM

    @pl.when(pl.program_id(1) == 0)
    def _():
        s_ref[...] = jnp.zeros_like(s_ref)

    rowi = lax.broadcasted_iota(I32, (c, c), 0)
    coli = lax.broadcasted_iota(I32, (c, c), 1)
    causal = rowi >= coli
    strict = rowi > coli
    tril = tril_ref[...]
    pairs = [(j, h) for j in range(nsub) for h in range(DN_HEADS)]
    rows = [slice(j * c, (j + 1) * c) for j in range(nsub)]
    gate = [gate_ref[rows[j], :] for j in range(nsub)]
    gc_all = [_split3_dot_left(tril, gate[j]) for j in range(nsub)]
    gc_t = [gc_all[j].T for j in range(nsub)]

    def head(ref, j, h):
        return ref[rows[j], h * dk:(h + 1) * dk]

    qh = {p: head(q_ref, *p) * (dk ** -0.5) for p in pairs}
    kh = {p: head(k_ref, *p) for p in pairs}
    gcol = {(j, h): gc_all[j][:, h:h + 1] for j, h in pairs}
    beta = {(j, h): gate[j][:, DN_HEADS + h:DN_HEADS + h + 1] for j, h in pairs}
    decay = {(j, h): jnp.where(causal, jnp.exp(jnp.where(causal, gcol[j, h] - gc_t[j][h:h + 1, :], 0.0)), 0.0)
             for j, h in pairs}
    kb = {p: kh[p] * beta[p] for p in pairs}
    egc = {p: jnp.exp(gcol[p]) for p in pairs}
    gram = {p: _mm_nt(jnp.concatenate([kb[p], qh[p]], axis=0), kh[p]) for p in pairs}
    mat = {p: jnp.where(strict, gram[p][:c] * decay[p], 0.0) for p in pairs}
    qk = {p: jnp.where(causal, gram[p][c:] * decay[p], 0.0) for p in pairs}
    sol = {p: jnp.concatenate([head(v_ref, *p) * beta[p], kb[p] * egc[p]], axis=1) for p in pairs}
    levels = int(math.log2(c))
    for lvl in range(levels):
        if lvl < levels - 1:
            y = {p: _mm(mat[p], jnp.concatenate([sol[p], mat[p]], axis=1)) for p in pairs}
            mat = {p: y[p][:, 2 * dk:] for p in pairs}
            upd = {p: y[p][:, :2 * dk] for p in pairs}
        else:
            upd = {p: _mm(mat[p], sol[p]) for p in pairs}
        sol = {p: (sol[p] - upd[p]) if lvl == 0 else (sol[p] + upd[p]) for p in pairs}
    g_last = {(j, h): gc_all[j][c - 1:c, h:h + 1] for j, h in pairs}
    wq = {p: jnp.concatenate([sol[p][:, dk:], qh[p] * egc[p]], axis=0) for p in pairs}
    k_dec_t = {p: (kh[p] * jnp.exp(g_last[p] - gcol[p])).T for p in pairs}
    d_last = {p: jnp.exp(g_last[p]) for p in pairs}

    s = [s_ref[h] for h in range(DN_HEADS)]
    for j in range(nsub):
        ws = [_mm(wq[j, h], s[h]) for h in range(DN_HEADS)]
        v_new = [sol[j, h][:, :dk] - ws[h][:c] for h in range(DN_HEADS)]
        o_ref[rows[j], :] = jnp.concatenate(
            [ws[h][c:] + _mm(qk[j, h], v_new[h]) for h in range(DN_HEADS)], axis=1)
        s = [s[h] * d_last[j, h] + _mm(k_dec_t[j, h], v_new[h]) for h in range(DN_HEADS)]
    for h in range(DN_HEADS):
        s_ref[h] = s[h]
    sfin_ref[0] = s_ref[...]


def _split3_dot_left(b01, a):
    a1 = a.astype(BF16)
    r1 = a - a1.astype(F32)
    a2 = r1.astype(BF16)
    a3 = (r1 - a2.astype(F32)).astype(BF16)
    out = jnp.dot(b01, a3, preferred_element_type=F32)
    out = out + jnp.dot(b01, a2, preferred_element_type=F32)
    return out + jnp.dot(b01, a1, preferred_element_type=F32)


def _delta_prompt(q, k, v, gate, nb):
    n = q.shape[0]
    t = n // nb
    c = DN_CHUNK
    nsub = DELTA_SUBCHUNKS
    rows = nsub * c
    nc = t // rows
    row = lambda b, i: (b * nc + i, 0)
    tril = jnp.tril(jnp.ones((c, c), F32)).astype(BF16)
    return pl.pallas_call(
        functools.partial(_delta_chunk_kernel, nsub=nsub),
        grid=(nb, nc),
        in_specs=[pl.BlockSpec((rows, DN_WIDTH), row),
                  pl.BlockSpec((rows, DN_WIDTH), row),
                  pl.BlockSpec((rows, DN_WIDTH), row),
                  pl.BlockSpec((rows, LANES), row),
                  pl.BlockSpec((c, c), lambda b, i: (0, 0))],
        out_specs=[pl.BlockSpec((rows, DN_WIDTH), row),
                   pl.BlockSpec((1, DN_HEADS, DN_HEAD_DIM, DN_HEAD_DIM), lambda b, i: (b, 0, 0, 0))],
        out_shape=[jax.ShapeDtypeStruct((n, DN_WIDTH), F32),
                   jax.ShapeDtypeStruct((nb, DN_HEADS, DN_HEAD_DIM, DN_HEAD_DIM), F32)],
        scratch_shapes=[pltpu.VMEM((DN_HEADS, DN_HEAD_DIM, DN_HEAD_DIM), F32)],
        compiler_params=_cparams(("arbitrary", "arbitrary")),
        name="delta_prompt",
    )(q, k, v, gate, tril)


def _delta_step_kernel(q_ref, k_ref, v_ref, gate_ref, s0_ref, ex_ref, o_ref, s_ref, kx_ref, qx_ref, *, nt, nb):
    dk = DN_HEAD_DIM
    flat = dk * dk
    nv = flat // LANES
    p = pl.program_id(0)
    lane = lax.broadcasted_iota(I32, (SUBLANES, LANES), 1)
    low = lane < dk
    ex = ex_ref[...]

    for t in range(nt):
        rs = slice(t * nb, (t + 1) * nb)
        kx_ref[...] = jnp.dot(k_ref[rs, :].astype(BF16), ex, preferred_element_type=F32)
        qx_ref[...] = jnp.dot((q_ref[rs, :] * (dk ** -0.5)).astype(BF16), ex, preferred_element_type=F32)
        src_ref = s0_ref if t == 0 else s_ref

        def tile_body(bt, carry, t=t, src_ref=src_ref):
            b0 = pl.multiple_of(bt * SUBLANES, SUBLANES)
            r0 = pl.multiple_of(t * nb + b0, SUBLANES)
            gate = gate_ref[pl.ds(r0, SUBLANES), :]
            vv = v_ref[pl.ds(r0, SUBLANES), :]
            vsw = pltpu.roll(vv, dk, axis=1)
            o_pair = None
            for j in range(2):
                c0 = j * flat
                g = jnp.sum(jnp.where(lane == 2 * p + j, gate, 0.0), axis=1, keepdims=True)
                beta = jnp.sum(jnp.where(lane == 2 * p + j + DN_HEADS, gate, 0.0), axis=1, keepdims=True)
                a = jnp.exp(g)
                vdup = jnp.where(low, vv, vsw) if j == 0 else jnp.where(low, vsw, vv)
                s = [src_ref[pl.ds(b0, SUBLANES), c0 + i * LANES:c0 + (i + 1) * LANES] for i in range(nv)]
                kx = [kx_ref[pl.ds(b0, SUBLANES), c0 + i * LANES:c0 + (i + 1) * LANES] for i in range(nv)]
                ks = kx[0] * s[0]
                for i in range(1, nv):
                    ks = ks + kx[i] * s[i]
                ks = ks + pltpu.roll(ks, dk, axis=1)
                delta = beta * (vdup - a * ks)
                oh = None
                for i in range(nv):
                    si = a * s[i] + kx[i] * delta
                    s_ref[pl.ds(b0, SUBLANES), c0 + i * LANES:c0 + (i + 1) * LANES] = si
                    term = qx_ref[pl.ds(b0, SUBLANES), c0 + i * LANES:c0 + (i + 1) * LANES] * si
                    oh = term if oh is None else oh + term
                oh = oh + pltpu.roll(oh, dk, axis=1)
                o_pair = oh if j == 0 else jnp.where(low, o_pair, oh)
            o_ref[pl.ds(r0, SUBLANES), :] = o_pair
            return carry

        lax.fori_loop(0, nb // SUBLANES, tile_body, 0)


def _delta_sample(q, k, v, gate, s0, nb, nt):
    dk = DN_HEAD_DIM
    flat = dk * dk
    n = nt * nb
    col = np.arange(2 * flat)
    ex = np.arange(LANES)[:, None] == ((col // flat) * dk + (col % flat) // dk)[None, :]
    ex = jnp.asarray(ex, BF16)
    kern = functools.partial(_delta_step_kernel, nt=nt, nb=nb)
    pair = lambda p: (0, p)
    return pl.pallas_call(
        kern,
        grid=(DN_HEADS // 2,),
        in_specs=[pl.BlockSpec((n, LANES), pair),
                  pl.BlockSpec((n, LANES), pair),
                  pl.BlockSpec((n, LANES), pair),
                  pl.BlockSpec((n, LANES), lambda p: (0, 0)),
                  pl.BlockSpec((nb, 2 * flat), pair),
                  pl.BlockSpec((LANES, 2 * flat), lambda p: (0, 0))],
        out_specs=[pl.BlockSpec((n, LANES), pair),
                   pl.BlockSpec((nb, 2 * flat), pair)],
        out_shape=[jax.ShapeDtypeStruct((n, DN_WIDTH), F32),
                   jax.ShapeDtypeStruct((nb, DN_HEADS * flat), F32)],
        scratch_shapes=[pltpu.VMEM((nb, 2 * flat), F32),
                        pltpu.VMEM((nb, 2 * flat), F32)],
        compiler_params=_cparams(("arbitrary",)),
        name="delta_sample",
    )(q, k, v, gate, s0, ex)


def _s5_kernel(u_ref, bre_ref, bim_ref, lam_ref, c_ref, d_ref, h0_ref, y_ref, hfin_ref,
               bw_ref, ab_ref, x_ref, h_ref, *, nb, tt):
    p2 = S5_FLAT

    @pl.when(pl.program_id(0) == 0)
    def _():
        lr = lam_ref[0:1, :]
        li = lam_ref[1:2, :]
        dt = jnp.exp(lam_ref[2:3, :])
        mag = jnp.exp(lr * dt)
        ab_re = mag * jnp.cos(li * dt)
        ab_im = mag * jnp.sin(li * dt)
        den = lr * lr + li * li
        nr = ab_re - 1.0
        ni = ab_im
        f_re = (nr * lr + ni * li) / den
        f_im = (ni * lr - nr * li) / den
        ab_ref[0:1, :] = ab_re
        ab_ref[1:2, :] = ab_im
        bre = bre_ref[...]
        bim = bim_ref[...]
        bw_ref[:, 0:p2] = (bre * f_re - bim * f_im).astype(BF16)
        bw_ref[:, p2:2 * p2] = (bim * f_re + bre * f_im).astype(BF16)
        h_ref[...] = h0_ref[...]

    u = u_ref[...]
    cw = S5_WIDTH // S5_SUPER
    sw = S5_FLAT // S5_SUPER
    for part in (0, p2):
        for b in range(S5_SUPER):
            x_ref[:, part + b * sw:part + (b + 1) * sw] = jnp.dot(
                u[:, b * cw:(b + 1) * cw], bw_ref[b * cw:(b + 1) * cw, part + b * sw:part + (b + 1) * sw],
                preferred_element_type=F32)
    a_re = ab_ref[0:1, :]
    a_im = ab_ref[1:2, :]

    if nb == SUBLANES:
        wsl = p2 // S5_SCAN_SPLIT
        for sp in range(S5_SCAN_SPLIT):
            c0 = sp * wsl
            are = jnp.broadcast_to(a_re[:, c0:c0 + wsl], (nb, wsl))
            aim = jnp.broadcast_to(a_im[:, c0:c0 + wsl], (nb, wsl))

            def step(t, carry, c0=c0, are=are, aim=aim):
                hr, hi = carry
                r0 = pl.multiple_of(t * nb, nb)
                nr = are * hr - aim * hi + x_ref[pl.ds(r0, nb), c0:c0 + wsl]
                ni = are * hi + aim * hr + x_ref[pl.ds(r0, nb), p2 + c0:p2 + c0 + wsl]
                x_ref[pl.ds(r0, nb), c0:c0 + wsl] = nr
                x_ref[pl.ds(r0, nb), p2 + c0:p2 + c0 + wsl] = ni
                return nr, ni

            hr, hi = lax.fori_loop(0, tt, step, (h_ref[:, c0:c0 + wsl], h_ref[:, p2 + c0:p2 + c0 + wsl]),
                                   unroll=2)
            h_ref[:, c0:c0 + wsl] = hr
            h_ref[:, p2 + c0:p2 + c0 + wsl] = hi
    else:
        for t in range(tt):
            rs = slice(t * nb, (t + 1) * nb)
            hr = h_ref[:, 0:p2]
            hi = h_ref[:, p2:2 * p2]
            nr = a_re * hr - a_im * hi + x_ref[rs, 0:p2]
            ni = a_re * hi + a_im * hr + x_ref[rs, p2:2 * p2]
            h_ref[:, 0:p2] = nr
            h_ref[:, p2:2 * p2] = ni
            x_ref[rs, 0:p2] = nr
            x_ref[rs, p2:2 * p2] = ni

    for b in range(S5_SUPER):
        cols = slice(b * cw, (b + 1) * cw)
        y = None
        for part in (0, p2):
            rws = slice(part + b * sw, part + (b + 1) * sw)
            term = jnp.dot(x_ref[:, rws].astype(BF16), c_ref[rws, cols], preferred_element_type=F32)
            y = term if y is None else y + term
        y_ref[:, cols] = y + d_ref[:, cols] * u[:, cols].astype(F32)
    hfin_ref[...] = h_ref[...]


def _s5(u, bre, bim, lam, cmat, dvec, h0, nb):
    n = u.shape[0]
    t = n // nb
    tt = min(ROW_TILE // nb, t)
    rows = tt * nb
    const = lambda i: (0, 0)
    kern = functools.partial(_s5_kernel, nb=nb, tt=tt)
    return pl.pallas_call(
        kern,
        grid=(t // tt,),
        in_specs=[pl.BlockSpec((rows, S5_WIDTH), lambda i: (i, 0)),
                  pl.BlockSpec((S5_WIDTH, S5_FLAT), const),
                  pl.BlockSpec((S5_WIDTH, S5_FLAT), const),
                  pl.BlockSpec((SUBLANES, S5_FLAT), const),
                  pl.BlockSpec((2 * S5_FLAT, S5_WIDTH), const),
                  pl.BlockSpec((1, S5_WIDTH), const),
                  pl.BlockSpec((nb, 2 * S5_FLAT), const)],
        out_specs=[pl.BlockSpec((rows, S5_WIDTH), lambda i: (i, 0)),
                   pl.BlockSpec((nb, 2 * S5_FLAT), const)],
        out_shape=[jax.ShapeDtypeStruct((n, S5_WIDTH), F32),
                   jax.ShapeDtypeStruct((nb, 2 * S5_FLAT), F32)],
        scratch_shapes=[pltpu.VMEM((S5_WIDTH, 2 * S5_FLAT), BF16),
                        pltpu.VMEM((SUBLANES, S5_FLAT), F32),
                        pltpu.VMEM((rows, 2 * S5_FLAT), F32),
                        pltpu.VMEM((nb, 2 * S5_FLAT), F32)],
        compiler_params=_cparams(("arbitrary",)),
        name="s5",
    )(u, bre, bim, lam, cmat, dvec, h0)


def _postmix_kernel(xp_ref, op_ref, zp_ref, ysp_ref, gap_ref, gbp_ref,
                    xs_ref, os_ref, zs_ref, yss_ref, gas_ref, gbs_ref, *rest, nblk_p):
    @pl.when(pl.program_id(0) < nblk_p)
    def _():
        _postmix_body(xp_ref, op_ref, zp_ref, ysp_ref, gap_ref, gbp_ref, *rest)

    @pl.when(pl.program_id(0) >= nblk_p)
    def _():
        _postmix_body(xs_ref, os_ref, zs_ref, yss_ref, gas_ref, gbs_ref, *rest)


def _postmix_body(x_ref, o_ref, z_ref, ys_ref, ga_ref, gb_ref, hw_ref, seg_ref, wa_ref, wglu_ref, wb_ref,
                  wo_ref, nf_ref, wr_ref, x1_ref, hn_ref, ridx_ref, rw_ref):
    o = o_ref[...]
    ms = jnp.dot((o * o).astype(BF16), seg_ref[...], preferred_element_type=F32) * (1.0 / DN_HEAD_DIM)
    on = o * lax.rsqrt(ms + RMS_EPS) * hw_ref[...]
    z = z_ref[...]
    oa = on * (z * jax.nn.sigmoid(z)).astype(F32)
    y_a = _mm(oa, wa_ref[...])
    ys = jax.nn.gelu(ys_ref[...])
    ys = ys * jax.nn.sigmoid(_mm(ys, wglu_ref[...]))
    y_b = _mm(ys, wb_ref[...])
    mixed = jax.nn.sigmoid(ga_ref[...]).astype(F32) * y_a + jax.nn.sigmoid(gb_ref[...]).astype(F32) * y_b
    x1 = x_ref[...] + _mm(mixed, wo_ref[...])
    x1_ref[...] = x1
    hn = x1 * lax.rsqrt(jnp.mean(x1 * x1, axis=-1, keepdims=True) + RMS_EPS) * nf_ref[...]
    _slab_store(hn_ref, hn)

    wr = wr_ref[...]
    w_hi = wr.astype(BF16)
    w_lo = (wr - w_hi.astype(F32)).astype(BF16)
    hn_hi = hn.astype(BF16)
    hn_lo = (hn - hn_hi.astype(F32)).astype(BF16)
    both = _mm_nt(jnp.concatenate([w_hi, w_lo], axis=0), hn_hi)
    logits = both[:ROUTER_ROWS] + both[ROUTER_ROWS:] + _mm_nt(w_hi, hn_lo)
    coarse = logits[N_EXPERTS:N_EXPERTS + MOE_GROUPS, :]
    cm = jnp.max(coarse, axis=0, keepdims=True)
    ce = jnp.exp(coarse - cm)
    pc = ce / jnp.sum(ce, axis=0, keepdims=True)
    p_sel = jnp.max(pc, axis=0, keepdims=True)
    gi = lax.broadcasted_iota(I32, pc.shape, 0)
    g_sel = jnp.min(jnp.where(pc == p_sel, gi, MOE_GROUPS), axis=0, keepdims=True)
    fine = jnp.zeros((EXPERTS_PER_GROUP, logits.shape[1]), F32)
    for g in range(MOE_GROUPS):
        fine = fine + jnp.where(g_sel == g, logits[g * EXPERTS_PER_GROUP:(g + 1) * EXPERTS_PER_GROUP, :], 0.0)
    fm = jnp.max(fine, axis=0, keepdims=True)
    fe = jnp.exp(fine - fm)
    pf = fe / jnp.sum(fe, axis=0, keepdims=True)
    ei = lax.broadcasted_iota(I32, pf.shape, 0)
    v1 = jnp.max(pf, axis=0, keepdims=True)
    i1 = jnp.min(jnp.where(pf == v1, ei, EXPERTS_PER_GROUP), axis=0, keepdims=True)
    rest = jnp.where(ei == i1, -1.0, pf)
    v2 = jnp.max(rest, axis=0, keepdims=True)
    i2 = jnp.min(jnp.where(rest == v2, ei, EXPERTS_PER_GROUP), axis=0, keepdims=True)
    tot = v1 + v2
    ridx_ref[0:1, :] = g_sel * EXPERTS_PER_GROUP + i1
    ridx_ref[1:2, :] = g_sel * EXPERTS_PER_GROUP + i2
    rw_ref[0:1, :] = v1 / tot * p_sel
    rw_ref[1:2, :] = v2 / tot * p_sel


def _postmix(prompt, sample, weights, nb):
    n_p = prompt[0].shape[0]
    n_s = sample[0].shape[0]
    t = n_p // nb
    tt = min(ROW_TILE, t, n_s)
    nt = t // tt
    nblk_p = n_p // tt
    nblk = nblk_p + n_s // tt
    n_total = n_p + n_s
    prow = lambda i: (jnp.minimum(i, nblk_p - 1), 0)
    pys = lambda i: (jnp.minimum(i, nblk_p - 1) % nt, jnp.minimum(i, nblk_p - 1) // nt)
    srow = lambda i: (jnp.maximum(i - nblk_p, 0), 0)
    const = lambda i: (0, 0)

    def stream_specs(row, ysmap):
        return [pl.BlockSpec((tt, D_MODEL), row),
                pl.BlockSpec((tt, DN_WIDTH), row),
                pl.BlockSpec((tt, DN_WIDTH), row),
                pl.BlockSpec((tt, S5_WIDTH), ysmap),
                pl.BlockSpec((tt, D_MODEL), row),
                pl.BlockSpec((tt, D_MODEL), row)]

    weight_specs = [pl.BlockSpec((1, DN_WIDTH), const),
                    pl.BlockSpec((DN_WIDTH, DN_WIDTH), const),
                    pl.BlockSpec((DN_WIDTH, D_MODEL), const),
                    pl.BlockSpec((S5_WIDTH, S5_WIDTH), const),
                    pl.BlockSpec((S5_WIDTH, D_MODEL), const),
                    pl.BlockSpec((D_MODEL, D_MODEL), const),
                    pl.BlockSpec((1, D_MODEL), const),
                    pl.BlockSpec((ROUTER_ROWS, D_MODEL), const)]
    xp, op, zp, ysp, gap, gbp = prompt
    return pl.pallas_call(
        functools.partial(_postmix_kernel, nblk_p=nblk_p),
        grid=(nblk,),
        in_specs=stream_specs(prow, pys) + stream_specs(srow, srow) + weight_specs,
        out_specs=[pl.BlockSpec((tt, D_MODEL), lambda i: (i, 0)),
                   pl.BlockSpec((tt * ROW_SLAB, LANES), lambda i: (i, 0)),
                   pl.BlockSpec((TOP_K, tt), lambda i: (0, i)),
                   pl.BlockSpec((TOP_K, tt), lambda i: (0, i))],
        out_shape=[jax.ShapeDtypeStruct((n_total, D_MODEL), F32),
                   jax.ShapeDtypeStruct((n_total * ROW_SLAB, LANES), F32),
                   jax.ShapeDtypeStruct((TOP_K, n_total), I32),
                   jax.ShapeDtypeStruct((TOP_K, n_total), F32)],
        compiler_params=_cparams(("arbitrary",)),
        name="postmix",
    )(xp, op, zp, ysp.reshape(t, nb * S5_WIDTH), gap, gbp, *sample, *weights)


def _wait_slabs(buf, sem):
    pltpu.make_async_copy(buf, buf, sem).wait()


def _moe_kernel(texp_ref, tsrc_ref, tnv_ref, otok_ref, hn_hbm, wu_ref, wd_ref, y_ref, xbuf0, xbuf1, gsem):
    del texp_ref
    i = pl.program_id(0)
    nsteps = pl.num_programs(0)
    tm = MOE_TILE
    rs = ROW_SLAB
    slot = lax.rem(i, 2)
    nv = tnv_ref[i]
    prev_valid = jnp.logical_and(i >= 1, tnv_ref[jnp.maximum(i - 1, 0)] > 0)
    xbuf = (xbuf0, xbuf1)

    def start_gather(step, sl):
        src0 = tsrc_ref[step]
        for r in range(tm):
            tok8 = pl.multiple_of(otok_ref[src0 + r], rs)
            pltpu.make_async_copy(hn_hbm.at[pl.ds(tok8, rs), :], xbuf[sl].at[pl.ds(r * rs, rs), :],
                                  gsem.at[sl]).start(priority=r % DMA_QUEUES)

    @pl.when(i == 0)
    def _():
        start_gather(0, 0)

    @pl.when(nv == 0)
    def _():
        y_ref[...] = jnp.zeros_like(y_ref)

    for sl in range(2):
        @pl.when(jnp.logical_and(slot == sl, nv > 0))
        def _():
            _wait_slabs(xbuf[sl], gsem.at[sl])
            start_gather(jnp.minimum(i + 1, nsteps - 1), 1 - sl)
            x = _slab_load(xbuf[sl], tm).astype(BF16)
            hu = jnp.dot(x, wu_ref[0].astype(BF16), preferred_element_type=F32)
            gate = hu[:, :EXPERT_FF]
            up = hu[:, EXPERT_FF:]
            act = gate * jax.nn.sigmoid(gate) * up
            _slab_store(y_ref, jnp.dot(act.astype(BF16), wd_ref[0].astype(BF16), preferred_element_type=F32))

        @pl.when(jnp.logical_and(slot == sl, jnp.logical_and(nv == 0, prev_valid)))
        def _():
            _wait_slabs(xbuf[sl], gsem.at[sl])

        @pl.when(jnp.logical_and(slot == sl, jnp.logical_and(i == nsteps - 1, nv > 0)))
        def _():
            _wait_slabs(xbuf[1 - sl], gsem.at[1 - sl])


def _moe(hn, w_up, w_down, texp, tsrc, tnv, otok):
    ntiles = texp.shape[0]
    wmap = lambda i, te, ts, tn, ot: (te[i], 0, 0)
    grid_spec = pltpu.PrefetchScalarGridSpec(
        num_scalar_prefetch=4,
        grid=(ntiles,),
        in_specs=[pl.BlockSpec(memory_space=pl.ANY),
                  pl.BlockSpec((1, D_MODEL, 2 * EXPERT_FF), wmap),
                  pl.BlockSpec((1, EXPERT_FF, D_MODEL), wmap)],
        out_specs=pl.BlockSpec((MOE_TILE * ROW_SLAB, LANES), lambda i, te, ts, tn, ot: (i, 0)),
        scratch_shapes=[pltpu.VMEM((MOE_TILE * ROW_SLAB, LANES), F32),
                        pltpu.VMEM((MOE_TILE * ROW_SLAB, LANES), F32),
                        pltpu.SemaphoreType.DMA((2,))])
    return pl.pallas_call(
        _moe_kernel,
        grid_spec=grid_spec,
        out_shape=jax.ShapeDtypeStruct((ntiles * MOE_TILE * ROW_SLAB, LANES), F32),
        compiler_params=_cparams(("arbitrary",)),
        name="moe",
    )(texp, tsrc, tnv, otok, hn, w_up, w_down)


def _combine_kernel(pos_ref, x1_ref, ys_hbm, w_ref, nw_ref, outp_ref, outs_ref, ybuf0, ybuf1, sem, *, nblk_p):
    i = pl.program_id(0)
    nsteps = pl.num_programs(0)
    tt = x1_ref.shape[0]
    rs = ROW_SLAB
    slot = lax.rem(i, 2)
    ybuf = (ybuf0, ybuf1)

    def start_gather(step, sl):
        base = step * (tt * TOP_K)
        for r in range(tt * TOP_K):
            j, s = divmod(r, TOP_K)
            p8 = pl.multiple_of(pos_ref[base + r], rs)
            pltpu.make_async_copy(ys_hbm.at[pl.ds(p8, rs), :], ybuf[sl].at[pl.ds((s * tt + j) * rs, rs), :],
                                  sem.at[sl]).start(priority=r % DMA_QUEUES)

    @pl.when(i == 0)
    def _():
        start_gather(0, 0)

    for sl in range(2):
        @pl.when(slot == sl)
        def _():
            _wait_slabs(ybuf[sl], sem.at[sl])
            start_gather(jnp.minimum(i + 1, nsteps - 1), 1 - sl)
            w = w_ref[...]
            y0 = _slab_load(ybuf[sl], tt, 0)
            y1 = _slab_load(ybuf[sl], tt, tt * rs)
            x = x1_ref[...] + w[:, 0:1] * y0 + w[:, 1:2] * y1
            res = x * lax.rsqrt(jnp.mean(x * x, axis=-1, keepdims=True) + RMS_EPS) * nw_ref[...]

            @pl.when(i < nblk_p)
            def _():
                outp_ref[...] = res

            @pl.when(i >= nblk_p)
            def _():
                outs_ref[...] = res

        @pl.when(jnp.logical_and(slot == sl, i == nsteps - 1))
        def _():
            _wait_slabs(ybuf[1 - sl], sem.at[1 - sl])


def _combine(x1, ysorted, pos8, wtok, nw, n_p):
    n = x1.shape[0]
    tt = math.gcd(math.gcd(n_p, n - n_p), COMBINE_TILE)
    nblk_p = n_p // tt
    grid_spec = pltpu.PrefetchScalarGridSpec(
        num_scalar_prefetch=1,
        grid=(n // tt,),
        in_specs=[pl.BlockSpec((tt, D_MODEL), lambda i, ps: (i, 0)),
                  pl.BlockSpec(memory_space=pl.ANY),
                  pl.BlockSpec((tt, TOP_K), lambda i, ps: (i, 0)),
                  pl.BlockSpec((1, D_MODEL), lambda i, ps: (0, 0))],
        out_specs=[pl.BlockSpec((tt, D_MODEL), lambda i, ps: (jnp.minimum(i, nblk_p - 1), 0)),
                   pl.BlockSpec((tt, D_MODEL), lambda i, ps: (jnp.maximum(i - nblk_p, 0), 0))],
        scratch_shapes=[pltpu.VMEM((tt * TOP_K * ROW_SLAB, LANES), F32),
                        pltpu.VMEM((tt * TOP_K * ROW_SLAB, LANES), F32),
                        pltpu.SemaphoreType.DMA((2,))])
    return pl.pallas_call(
        functools.partial(_combine_kernel, nblk_p=nblk_p),
        grid_spec=grid_spec,
        out_shape=[jax.ShapeDtypeStruct((n_p, D_MODEL), F32),
                   jax.ShapeDtypeStruct((n - n_p, D_MODEL), F32)],
        compiler_params=_cparams(("arbitrary",)),
        name="combine",
    )(pos8, x1, ysorted, wtok, nw)


def _route_plan(ridx, n_tok):
    tm = MOE_TILE
    n_assign = n_tok * TOP_K
    ntiles = n_assign // tm + N_EXPERTS
    e_flat = ridx.T.reshape(n_assign)
    order = jnp.argsort(e_flat, stable=True).astype(I32)
    counts = jnp.sum((e_flat[:, None] == jnp.arange(N_EXPERTS, dtype=I32)[None, :]).astype(I32), axis=0)
    cstart = jnp.cumsum(counts) - counts
    tiles_e = (counts + tm - 1) // tm
    tend = jnp.cumsum(tiles_e)
    tstart = tend - tiles_e
    tile_id = jnp.arange(ntiles, dtype=I32)
    texp = jnp.minimum(jnp.sum((tile_id[:, None] >= tend[None, :]).astype(I32), axis=1), N_EXPERTS - 1)
    onehot = (texp[:, None] == jnp.arange(N_EXPERTS, dtype=I32)[None, :]).astype(I32)
    pick = lambda v: jnp.sum(onehot * v[None, :], axis=1)
    done = (tile_id - pick(tstart)) * tm
    tnv = jnp.where(tile_id < tend[-1], jnp.clip(pick(counts) - done, 0, tm), 0)
    tsrc = jnp.where(tnv > 0, pick(cstart) + done, 0)
    otok8 = jnp.concatenate([(order // TOP_K) * ROW_SLAB, jnp.zeros((tm,), I32)])
    rank = jnp.argsort(order).astype(I32)
    eoh = (e_flat[:, None] == jnp.arange(N_EXPERTS, dtype=I32)[None, :]).astype(I32)
    pos8 = (rank + jnp.sum(eoh * (tstart * tm - cstart)[None, :], axis=1)) * ROW_SLAB
    return texp.astype(I32), tsrc.astype(I32), tnv.astype(I32), otok8, pos8.astype(I32)


def _block_diag(m):
    g, a, b = m.shape
    eye = jnp.eye(g, dtype=m.dtype)
    return (eye[:, None, :, None] * m[:, :, None, :]).reshape(g * a, g * b)


def kernel(x_prompt, x_sample, state_conv, state_delta, state_ssm_re, state_ssm_im, norm_mix_w, w_in, conv_w, a_log, dt_bias, head_norm_w, w_a_up, s5_lambda_re, s5_lambda_im, s5_log_step, s5_b_re, s5_b_im, s5_c_re, s5_c_im, s5_d, w_glu, w_b_up, w_o, norm_ffn_w, w_router_coarse, w_router_fine, w_expert_up, w_expert_down, norm_final_w):
    bp, tp, _ = x_prompt.shape
    bs, ts, _ = x_sample.shape
    n_p = bp * tp
    n_s = bs * ts
    n_tok = n_p + n_s
    l = 0

    w = w_in[l]
    cuts = np.cumsum([0, QKV_DIM, DN_WIDTH, DN_HEADS, DN_HEADS, S5_WIDTH, D_MODEL, D_MODEL])
    w_qkv, w_z, w_a, w_b, w_u, w_ga, w_gb = [w[:, cuts[i]:cuts[i + 1]] for i in range(7)]
    w_ab = jnp.concatenate([w_a, w_b, jnp.zeros((D_MODEL, LANES - 2 * DN_HEADS), F32)], axis=1)
    wcat = jnp.concatenate([w_qkv, w_z, w_u, w_ga, w_gb, w_ab], axis=1).astype(BF16)
    nw_mix = norm_mix_w[l].reshape(1, D_MODEL)
    pad8 = lambda v: jnp.concatenate([v, jnp.zeros((LANES - DN_HEADS,), F32)]).reshape(1, LANES)
    gate_p = jnp.concatenate([pad8(a_log[l]), pad8(dt_bias[l])], axis=0)
    seg = _block_diag(jnp.ones((DN_HEADS, DN_HEAD_DIM, DN_HEAD_DIM), BF16))
    bre = _block_diag(jnp.swapaxes(s5_b_re[l], 1, 2))
    bim = _block_diag(jnp.swapaxes(s5_b_im[l], 1, 2))
    lam = jnp.concatenate([s5_lambda_re[l].reshape(1, S5_FLAT), s5_lambda_im[l].reshape(1, S5_FLAT),
                           jnp.repeat(s5_log_step[l], S5_STATE).reshape(1, S5_FLAT),
                           jnp.zeros((SUBLANES - 3, S5_FLAT), F32)], axis=0)
    cmat = jnp.concatenate([_block_diag(jnp.swapaxes(s5_c_re[l], 1, 2)),
                            -_block_diag(jnp.swapaxes(s5_c_im[l], 1, 2))], axis=0).astype(BF16)
    dvec = s5_d[l].reshape(1, S5_WIDTH)
    hw = jnp.tile(head_norm_w[l], DN_HEADS).reshape(1, DN_WIDTH)
    wr = jnp.concatenate([w_router_fine[l].T, w_router_coarse[l].T,
                          jnp.zeros((ROUTER_ROWS - N_EXPERTS - MOE_GROUPS, D_MODEL), F32)], axis=0)
    pm_weights = (hw, seg, w_a_up[l].astype(BF16), w_glu[l].astype(BF16), w_b_up[l].astype(BF16),
                  w_o[l].astype(BF16), norm_ffn_w[l].reshape(1, D_MODEL), wr)

    xp2 = x_prompt.reshape(n_p, D_MODEL)
    q_p, k_p, v_p, gates_p, conv_p, z_p, u_p, ga_p, gb_p = _inprep(
        xp2, nw_mix, wcat, jnp.zeros((bp, SUBLANES, QKV_DIM), F32), conv_w[l], gate_p, seg, bp, 1)
    o_p, delta_p = _delta_prompt(q_p, k_p, v_p, gates_p, bp)
    ys_p, h_p = _s5(u_p, bre, bim, lam, cmat, dvec, jnp.zeros((bp, 2 * S5_FLAT), F32), bp)

    xs2 = jnp.swapaxes(x_sample, 0, 1).reshape(n_s, D_MODEL)
    cinit_s = jnp.swapaxes(state_conv[l], 0, 1).reshape(1, (CONV_W - 1) * bs, QKV_DIM)
    q_s, k_s, v_s, gate_s, conv_s, z_s, u_s, ga_s, gb_s = _inprep(
        xs2, nw_mix, wcat, cinit_s, conv_w[l], gate_p, seg, 1, bs)
    s0 = state_delta[l].reshape(bs, DN_HEADS * DN_HEAD_DIM * DN_HEAD_DIM)
    o_s, delta_s = _delta_sample(q_s, k_s, v_s, gate_s, s0, bs, ts)
    h0_s = jnp.concatenate([state_ssm_re[l].reshape(bs, S5_FLAT), state_ssm_im[l].reshape(bs, S5_FLAT)], axis=1)
    ys_s, h_s = _s5(u_s, bre, bim, lam, cmat, dvec, h0_s, bs)
    x1, hn, ridx, rw = _postmix((xp2, o_p, z_p, ys_p, ga_p, gb_p), (xs2, o_s, z_s, ys_s, ga_s, gb_s),
                                pm_weights, bp)

    texp, tsrc, tnv, otok8, pos8 = _route_plan(ridx, n_tok)
    ysorted = _moe(hn, w_expert_up[l], w_expert_down[l], texp, tsrc, tnv, otok8)
    y_p, y_s = _combine(x1, ysorted, pos8, rw.T, norm_final_w.reshape(1, D_MODEL), n_p)

    y_prompt = y_p.reshape(bp, tp, D_MODEL)
    y_sample = jnp.swapaxes(y_s.reshape(ts, bs, D_MODEL), 0, 1)
    conv_sample = jnp.swapaxes(conv_s.reshape(CONV_W - 1, bs, QKV_DIM), 0, 1)
    return (y_prompt, y_sample,
            conv_p[None], delta_p[None],
            h_p[:, :S5_FLAT].reshape(1, bp, S5_GROUPS, S5_STATE), h_p[:, S5_FLAT:].reshape(1, bp, S5_GROUPS, S5_STATE),
            conv_sample[None], delta_s.reshape(1, bs, DN_HEADS, DN_HEAD_DIM, DN_HEAD_DIM),
            h_s[:, :S5_FLAT].reshape(1, bs, S5_GROUPS, S5_STATE), h_s[:, S5_FLAT:].reshape(1, bs, S5_GROUPS, S5_STATE))
```

```python
import functools
import math

import jax
import jax.numpy as jnp
import numpy as np
from jax import lax
from jax.experimental import pallas as pl
from jax.experimental.pallas import tpu as pltpu

F32 = jnp.float32
BF16 = jnp.bfloat16
I32 = jnp.int32

D_MODEL = 1024
DN_HEADS = 8
DN_HEAD_DIM = 64
DN_WIDTH = DN_HEADS * DN_HEAD_DIM
QKV_DIM = 3 * DN_WIDTH
CONV_W = 4
DN_CHUNK = 64
S5_GROUP_CH = 16
S5_WIDTH = D_MODEL // 2
S5_GROUPS = S5_WIDTH // S5_GROUP_CH
S5_STATE = 64
S5_FLAT = S5_GROUPS * S5_STATE
MOE_GROUPS = 4
EXPERTS_PER_GROUP = 8
N_EXPERTS = MOE_GROUPS * EXPERTS_PER_GROUP
TOP_K = 2
EXPERT_FF = 256
RMS_EPS = 1e-6
L2_EPS = 1e-6

LANES = 128
SUBLANES = 8
VMEM_LIMIT = 56 * 1024 * 1024

C_QKV, C_Z, C_U, C_GA, C_GB, C_AB = 0, 1536, 2048, 2560, 3584, 4608
IN_PACKED = C_AB + LANES

ROW_TILE = 512
MOE_TILE = 256
MOE_PHASES = 2
COMBINE_TILE = 256
DMA_QUEUES = 2
DELTA_SUBCHUNKS = 4
S5_SUPER = 2
S5_SCAN_SPLIT = 2
ROUTER_ROWS = 40


def _mm(a, b):
    return jnp.dot(a.astype(BF16), b.astype(BF16), preferred_element_type=F32)


def _mm_nt(a, b):
    return lax.dot_general(a.astype(BF16), b.astype(BF16), (((1,), (1,)), ((), ())),
                           preferred_element_type=F32)


def _split3_dot(a, b01):
    a1 = a.astype(BF16)
    r1 = a - a1.astype(F32)
    a2 = r1.astype(BF16)
    a3 = (r1 - a2.astype(F32)).astype(BF16)
    out = jnp.dot(a3, b01, preferred_element_type=F32)
    out = out + jnp.dot(a2, b01, preferred_element_type=F32)
    return out + jnp.dot(a1, b01, preferred_element_type=F32)


def _cparams(sem):
    return pltpu.CompilerParams(dimension_semantics=sem, vmem_limit_bytes=VMEM_LIMIT)


ROW_SLAB = D_MODEL // LANES


def _slab_load(ref, rows, first=0, pitch=ROW_SLAB):
    return jnp.concatenate([ref[pl.ds(first + j, rows, stride=pitch), :] for j in range(ROW_SLAB)], axis=1)


def _slab_store(ref, x):
    for j in range(ROW_SLAB):
        ref[pl.ds(j, x.shape[0], stride=ROW_SLAB), :] = x[:, j * LANES:(j + 1) * LANES]


def _softplus(x):
    return jnp.maximum(x, 0.0) + jnp.log1p(jnp.exp(-jnp.abs(x)))


def _inprep_kernel(x_ref, nw_ref, w_ref, cinit_ref, cw_ref, gp_ref, seg_ref,
                   q_ref, k_ref, v_ref, gate_ref, cnew_ref, z_ref, u_ref, ga_ref, gb_ref, xp_ref,
                   *, shift, rc, rows):
    x = x_ref[...]
    h = x * lax.rsqrt(jnp.mean(x * x, axis=-1, keepdims=True) + RMS_EPS) * nw_ref[...]
    hb = h.astype(BF16)

    def proj(lo, hi):
        return jnp.dot(hb, w_ref[:, lo:hi], preferred_element_type=F32)

    @pl.when(pl.program_id(1) == 0)
    def _():
        xp_ref[0:rc, :] = cinit_ref[0]

    xp_ref[rc:rc + rows, :] = proj(C_QKV, C_Z)
    ab = proj(C_AB, IN_PACKED)
    z_ref[...] = proj(C_Z, C_U).astype(z_ref.dtype)
    u_ref[...] = proj(C_U, C_GA).astype(u_ref.dtype)
    ga_ref[...] = proj(C_GA, C_GB).astype(ga_ref.dtype)
    gb_ref[...] = proj(C_GB, C_AB).astype(gb_ref.dtype)
    acc = None
    for i in range(CONV_W):
        lo = rc + (i - (CONV_W - 1)) * shift
        term = xp_ref[lo:lo + rows, :] * cw_ref[i:i + 1, :]
        acc = term if acc is None else acc + term
    y = acc * jax.nn.sigmoid(acc)
    keep = (CONV_W - 1) * shift
    cnew_ref[0] = xp_ref[rc + rows - keep:rc + rows, :]
    xp_ref[0:rc, :] = xp_ref[rows:rows + rc, :]

    seg = seg_ref[...]
    q = y[:, 0:DN_WIDTH]
    k = y[:, DN_WIDTH:2 * DN_WIDTH]
    q_ref[...] = q * lax.rsqrt(jnp.dot((q * q).astype(BF16), seg, preferred_element_type=F32) + L2_EPS)
    k_ref[...] = k * lax.rsqrt(jnp.dot((k * k).astype(BF16), seg, preferred_element_type=F32) + L2_EPS)
    v_ref[...] = y[:, 2 * DN_WIDTH:]

    g = -jnp.exp(gp_ref[0:1, :]) * _softplus(ab + gp_ref[1:2, :])
    beta = jax.nn.sigmoid(ab)
    lane = lax.broadcasted_iota(I32, ab.shape, 1)
    gate_ref[...] = jnp.where(lane < DN_HEADS, g, beta)


def _inprep(x2d, nw, wcat, cinit, conv_w, gate_p, seg, nb, shift):
    n = x2d.shape[0]
    r = n // nb
    rows = min(ROW_TILE, r)
    nt = r // rows
    rc = cinit.shape[1]
    keep = (CONV_W - 1) * shift
    row = lambda b, i: (b * nt + i, 0)
    const = lambda b, i: (0, 0)
    kern = functools.partial(_inprep_kernel, shift=shift, rc=rc, rows=rows)
    outs = pl.pallas_call(
        kern,
        grid=(nb, nt),
        in_specs=[pl.BlockSpec((rows, D_MODEL), row),
                  pl.BlockSpec((1, D_MODEL), const),
                  pl.BlockSpec((D_MODEL, IN_PACKED), const),
                  pl.BlockSpec((1, rc, QKV_DIM), lambda b, i: (b, 0, 0)),
                  pl.BlockSpec((CONV_W, QKV_DIM), const),
                  pl.BlockSpec((2, LANES), const),
                  pl.BlockSpec((DN_WIDTH, DN_WIDTH), const)],
        out_specs=[pl.BlockSpec((rows, DN_WIDTH), row),
                   pl.BlockSpec((rows, DN_WIDTH), row),
                   pl.BlockSpec((rows, DN_WIDTH), row),
                   pl.BlockSpec((rows, LANES), row),
                   pl.BlockSpec((1, keep, QKV_DIM), lambda b, i: (b, 0, 0)),
                   pl.BlockSpec((rows, DN_WIDTH), row),
                   pl.BlockSpec((rows, S5_WIDTH), lambda b, i: (i, b)),
                   pl.BlockSpec((rows, D_MODEL), row),
                   pl.BlockSpec((rows, D_MODEL), row)],
        out_shape=[jax.ShapeDtypeStruct((n, DN_WIDTH), F32),
                   jax.ShapeDtypeStruct((n, DN_WIDTH), F32),
                   jax.ShapeDtypeStruct((n, DN_WIDTH), F32),
                   jax.ShapeDtypeStruct((n, LANES), F32),
                   jax.ShapeDtypeStruct((nb, keep, QKV_DIM), F32),
                   jax.ShapeDtypeStruct((n, DN_WIDTH), BF16),
                   jax.ShapeDtypeStruct((r, nb * S5_WIDTH), BF16),
                   jax.ShapeDtypeStruct((n, D_MODEL), BF16),
                   jax.ShapeDtypeStruct((n, D_MODEL), BF16)],
        scratch_shapes=[pltpu.VMEM((rc + rows, QKV_DIM), F32)],
        compiler_params=_cparams(("arbitrary", "arbitrary")),
        name="inprep",
    )(x2d, nw, wcat, cinit, conv_w, gate_p, seg)
    q, k, v, gate, cnew, z, u, ga, gb = outs
    return q, k, v, gate, cnew, z, u.reshape(r * nb, S5_WIDTH), ga, gb


def _delta_chunk_kernel(q_ref, k_ref, v_ref, gate_ref, tril_ref, o_ref, sfin_ref, s_ref, *, nsub):
    c = DN_CHUNK
    dk = DN_HEAD_DIM

    @pl.when(pl.program_id(1) == 0)
    def _():
        s_ref[...] = jnp.zeros_like(s_ref)

    rowi = lax.broadcasted_iota(I32, (c, c), 0)
    coli = lax.broadcasted_iota(I32, (c, c), 1)
    causal = rowi >= coli
    strict = rowi > coli
    tril = tril_ref[...]
    pairs = [(j, h) for j in range(nsub) for h in range(DN_HEADS)]
    rows = [slice(j * c, (j + 1) * c) for j in range(nsub)]
    gate = [gate_ref[rows[j], :] for j in range(nsub)]
    gc_all = [_split3_dot_left(tril, gate[j]) for j in range(nsub)]
    gc_t = [gc_all[j].T for j in range(nsub)]

    def head(ref, j, h):
        return ref[rows[j], h * dk:(h + 1) * dk]

    qh = {p: head(q_ref, *p) * (dk ** -0.5) for p in pairs}
    kh = {p: head(k_ref, *p) for p in pairs}
    gcol = {(j, h): gc_all[j][:, h:h + 1] for j, h in pairs}
    beta = {(j, h): gate[j][:, DN_HEADS + h:DN_HEADS + h + 1] for j, h in pairs}
    grow2 = {(j, h): jnp.concatenate([gc_t[j][h:h + 1, :], gc_t[j][h:h + 1, :]], axis=1) for j, h in pairs}
    rowi2 = lax.broadcasted_iota(I32, (c, 2 * c), 0)
    coli2 = lax.broadcasted_iota(I32, (c, 2 * c), 1) & (c - 1)
    causal2 = rowi2 >= coli2
    strict2 = rowi2 > coli2
    decay = {p: jnp.where(causal2, jnp.exp(jnp.where(causal2, gcol[p] - grow2[p], 0.0)), 0.0) for p in pairs}
    kb = {p: kh[p] * beta[p] for p in pairs}
    egc = {p: jnp.exp(gcol[p]) for p in pairs}
    gram = {p: _mm_nt(jnp.concatenate([kb[p], qh[p]], axis=0), jnp.concatenate([kh[p], kh[p]], axis=0))
            for p in pairs}
    mat = {p: jnp.where(strict2, gram[p][:c] * decay[p], 0.0).astype(BF16) for p in pairs}
    qk = {p: jnp.where(causal, gram[p][c:, :c] * decay[p][:, :c], 0.0) for p in pairs}
    sol = {p: jnp.concatenate([head(v_ref, *p) * beta[p], kb[p] * egc[p]], axis=1) for p in pairs}
    levels = int(math.log2(c))
    zeros2 = jnp.zeros((c, 2 * c), BF16)
    for lvl in range(levels):
        hi = {p: sol[p].astype(BF16) for p in pairs}
        lo = {p: (sol[p] - hi[p].astype(F32)).astype(BF16) for p in pairs}
        if lvl < levels - 1:
            y = {p: jnp.dot(mat[p], jnp.concatenate([jnp.concatenate([hi[p], mat[p]], axis=1),
                                                     jnp.concatenate([lo[p], zeros2], axis=1)], axis=0),
                            preferred_element_type=F32) for p in pairs}
            mat = {p: y[p][:, 2 * dk:].astype(BF16) for p in pairs}
            upd = {p: y[p][:, :2 * dk] for p in pairs}
        else:
            upd = {p: jnp.dot(mat[p], jnp.concatenate([hi[p], lo[p]], axis=0), preferred_element_type=F32)
                   for p in pairs}
        sol = {p: (sol[p] - upd[p]) if lvl == 0 else (sol[p] + upd[p]) for p in pairs}
    g_last = {(j, h): gc_all[j][c - 1:c, h:h + 1] for j, h in pairs}
    wq = {p: jnp.concatenate([sol[p][:, dk:], qh[p] * egc[p]], axis=0) for p in pairs}
    k_dec_t = {p: (kh[p] * jnp.exp(g_last[p] - gcol[p])).T for p in pairs}
    d_last = {p: jnp.exp(g_last[p]) for p in pairs}

    s = [s_ref[h] for h in range(DN_HEADS)]
    for j in range(nsub):
        ws = [_mm(wq[j, h], s[h]) for h in range(DN_HEADS)]
        v_new = [sol[j, h][:, :dk] - ws[h][:c] for h in range(DN_HEADS)]
        o_ref[rows[j], :] = jnp.concatenate(
            [ws[h][c:] + _mm(qk[j, h], v_new[h]) for h in range(DN_HEADS)], axis=1)
        s = [s[h] * d_last[j, h] + _mm(k_dec_t[j, h], v_new[h]) for h in range(DN_HEADS)]
    for h in range(DN_HEADS):
        s_ref[h] = s[h]
    sfin_ref[0] = s_ref[...]


def _split3_dot_left(b01, a):
    a1 = a.astype(BF16)
    r1 = a - a1.astype(F32)
    a2 = r1.astype(BF16)
    a3 = (r1 - a2.astype(F32)).astype(BF16)
    out = jnp.dot(b01, a3, preferred_element_type=F32)
    out = out + jnp.dot(b01, a2, preferred_element_type=F32)
    return out + jnp.dot(b01, a1, preferred_element_type=F32)


def _delta_prompt(q, k, v, gate, nb):
    n = q.shape[0]
    t = n // nb
    c = DN_CHUNK
    nsub = DELTA_SUBCHUNKS
    rows = nsub * c
    nc = t // rows
    row = lambda b, i: (b * nc + i, 0)
    tril = jnp.tril(jnp.ones((c, c), F32)).astype(BF16)
    return pl.pallas_call(
        functools.partial(_delta_chunk_kernel, nsub=nsub),
        grid=(nb, nc),
        in_specs=[pl.BlockSpec((rows, DN_WIDTH), row),
                  pl.BlockSpec((rows, DN_WIDTH), row),
                  pl.BlockSpec((rows, DN_WIDTH), row),
                  pl.BlockSpec((rows, LANES), row),
                  pl.BlockSpec((c, c), lambda b, i: (0, 0))],
        out_specs=[pl.BlockSpec((rows, DN_WIDTH), row),
                   pl.BlockSpec((1, DN_HEADS, DN_HEAD_DIM, DN_HEAD_DIM), lambda b, i: (b, 0, 0, 0))],
        out_shape=[jax.ShapeDtypeStruct((n, DN_WIDTH), F32),
                   jax.ShapeDtypeStruct((nb, DN_HEADS, DN_HEAD_DIM, DN_HEAD_DIM), F32)],
        scratch_shapes=[pltpu.VMEM((DN_HEADS, DN_HEAD_DIM, DN_HEAD_DIM), F32)],
        compiler_params=_cparams(("arbitrary", "arbitrary")),
        name="delta_prompt",
    )(q, k, v, gate, tril)


def _delta_step_kernel(q_ref, k_ref, v_ref, gate_ref, s0_ref, ex_ref, o_ref, s_ref, kx_ref, qx_ref, *, nt, nb):
    dk = DN_HEAD_DIM
    flat = dk * dk
    nv = flat // LANES
    p = pl.program_id(0)
    lane = lax.broadcasted_iota(I32, (SUBLANES, LANES), 1)
    low = lane < dk
    ex = ex_ref[...]

    for t in range(nt):
        rs = slice(t * nb, (t + 1) * nb)
        kx_ref[...] = jnp.dot(k_ref[rs, :].astype(BF16), ex, preferred_element_type=F32)
        qx_ref[...] = jnp.dot((q_ref[rs, :] * (dk ** -0.5)).astype(BF16), ex, preferred_element_type=F32)
        src_ref = s0_ref if t == 0 else s_ref

        def tile_body(bt, carry, t=t, src_ref=src_ref):
            b0 = pl.multiple_of(bt * SUBLANES, SUBLANES)
            r0 = pl.multiple_of(t * nb + b0, SUBLANES)
            gate = gate_ref[pl.ds(r0, SUBLANES), :]
            vv = v_ref[pl.ds(r0, SUBLANES), :]
            vsw = pltpu.roll(vv, dk, axis=1)
            o_pair = None
            for j in range(2):
                c0 = j * flat
                g = jnp.sum(jnp.where(lane == 2 * p + j, gate, 0.0), axis=1, keepdims=True)
                beta = jnp.sum(jnp.where(lane == 2 * p + j + DN_HEADS, gate, 0.0), axis=1, keepdims=True)
                a = jnp.exp(g)
                vdup = jnp.where(low, vv, vsw) if j == 0 else jnp.where(low, vsw, vv)
                s = [src_ref[pl.ds(b0, SUBLANES), c0 + i * LANES:c0 + (i + 1) * LANES] for i in range(nv)]
                kx = [kx_ref[pl.ds(b0, SUBLANES), c0 + i * LANES:c0 + (i + 1) * LANES] for i in range(nv)]
                ks = kx[0] * s[0]
                for i in range(1, nv):
                    ks = ks + kx[i] * s[i]
                ks = ks + pltpu.roll(ks, dk, axis=1)
                delta = beta * (vdup - a * ks)
                oh = None
                for i in range(nv):
                    si = a * s[i] + kx[i] * delta
                    s_ref[pl.ds(b0, SUBLANES), c0 + i * LANES:c0 + (i + 1) * LANES] = si
                    term = qx_ref[pl.ds(b0, SUBLANES), c0 + i * LANES:c0 + (i + 1) * LANES] * si
                    oh = term if oh is None else oh + term
                oh = oh + pltpu.roll(oh, dk, axis=1)
                o_pair = oh if j == 0 else jnp.where(low, o_pair, oh)
            o_ref[pl.ds(r0, SUBLANES), :] = o_pair
            return carry

        lax.fori_loop(0, nb // SUBLANES, tile_body, 0)


def _delta_sample(q, k, v, gate, s0, nb, nt):
    dk = DN_HEAD_DIM
    flat = dk * dk
    n = nt * nb
    col = np.arange(2 * flat)
    ex = np.arange(LANES)[:, None] == ((col // flat) * dk + (col % flat) // dk)[None, :]
    ex = jnp.asarray(ex, BF16)
    kern = functools.partial(_delta_step_kernel, nt=nt, nb=nb)
    pair = lambda p: (0, p)
    return pl.pallas_call(
        kern,
        grid=(DN_HEADS // 2,),
        in_specs=[pl.BlockSpec((n, LANES), pair),
                  pl.BlockSpec((n, LANES), pair),
                  pl.BlockSpec((n, LANES), pair),
                  pl.BlockSpec((n, LANES), lambda p: (0, 0)),
                  pl.BlockSpec((nb, 2 * flat), pair),
                  pl.BlockSpec((LANES, 2 * flat), lambda p: (0, 0))],
        out_specs=[pl.BlockSpec((n, LANES), pair),
                   pl.BlockSpec((nb, 2 * flat), pair)],
        out_shape=[jax.ShapeDtypeStruct((n, DN_WIDTH), F32),
                   jax.ShapeDtypeStruct((nb, DN_HEADS * flat), F32)],
        scratch_shapes=[pltpu.VMEM((nb, 2 * flat), F32),
                        pltpu.VMEM((nb, 2 * flat), F32)],
        compiler_params=_cparams(("arbitrary",)),
        name="delta_sample",
    )(q, k, v, gate, s0, ex)


def _s5_kernel(u_ref, bre_ref, bim_ref, lam_ref, c_ref, d_ref, h0_ref, y_ref, hfin_ref,
               bw_ref, ab_ref, x_ref, h_ref, *, nb, tt):
    p2 = S5_FLAT

    @pl.when(pl.program_id(0) == 0)
    def _():
        lr = lam_ref[0:1, :]
        li = lam_ref[1:2, :]
        dt = jnp.exp(lam_ref[2:3, :])
        mag = jnp.exp(lr * dt)
        ab_re = mag * jnp.cos(li * dt)
        ab_im = mag * jnp.sin(li * dt)
        den = lr * lr + li * li
        nr = ab_re - 1.0
        ni = ab_im
        f_re = (nr * lr + ni * li) / den
        f_im = (ni * lr - nr * li) / den
        ab_ref[0:1, :] = ab_re
        ab_ref[1:2, :] = ab_im
        bre = bre_ref[...]
        bim = bim_ref[...]
        bw_ref[:, 0:p2] = (bre * f_re - bim * f_im).astype(BF16)
        bw_ref[:, p2:2 * p2] = (bim * f_re + bre * f_im).astype(BF16)
        h_ref[...] = h0_ref[...]

    u = u_ref[...]
    cw = S5_WIDTH // S5_SUPER
    sw = S5_FLAT // S5_SUPER
    for part in (0, p2):
        for b in range(S5_SUPER):
            x_ref[:, part + b * sw:part + (b + 1) * sw] = jnp.dot(
                u[:, b * cw:(b + 1) * cw], bw_ref[b * cw:(b + 1) * cw, part + b * sw:part + (b + 1) * sw],
                preferred_element_type=F32)
    a_re = ab_ref[0:1, :]
    a_im = ab_ref[1:2, :]

    if nb == SUBLANES:
        wsl = p2 // S5_SCAN_SPLIT
        for sp in range(S5_SCAN_SPLIT):
            c0 = sp * wsl
            are = jnp.broadcast_to(a_re[:, c0:c0 + wsl], (nb, wsl))
            aim = jnp.broadcast_to(a_im[:, c0:c0 + wsl], (nb, wsl))

            def step(t, carry, c0=c0, are=are, aim=aim):
                hr, hi = carry
                r0 = pl.multiple_of(t * nb, nb)
                nr = are * hr - aim * hi + x_ref[pl.ds(r0, nb), c0:c0 + wsl]
                ni = are * hi + aim * hr + x_ref[pl.ds(r0, nb), p2 + c0:p2 + c0 + wsl]
                x_ref[pl.ds(r0, nb), c0:c0 + wsl] = nr
                x_ref[pl.ds(r0, nb), p2 + c0:p2 + c0 + wsl] = ni
                return nr, ni

            hr, hi = lax.fori_loop(0, tt, step, (h_ref[:, c0:c0 + wsl], h_ref[:, p2 + c0:p2 + c0 + wsl]),
                                   unroll=2)
            h_ref[:, c0:c0 + wsl] = hr
            h_ref[:, p2 + c0:p2 + c0 + wsl] = hi
    else:
        for t in range(tt):
            rs = slice(t * nb, (t + 1) * nb)
            hr = h_ref[:, 0:p2]
            hi = h_ref[:, p2:2 * p2]
            nr = a_re * hr - a_im * hi + x_ref[rs, 0:p2]
            ni = a_re * hi + a_im * hr + x_ref[rs, p2:2 * p2]
            h_ref[:, 0:p2] = nr
            h_ref[:, p2:2 * p2] = ni
            x_ref[rs, 0:p2] = nr
            x_ref[rs, p2:2 * p2] = ni

    for b in range(S5_SUPER):
        cols = slice(b * cw, (b + 1) * cw)
        y = None
        for part in (0, p2):
            rws = slice(part + b * sw, part + (b + 1) * sw)
            term = jnp.dot(x_ref[:, rws].astype(BF16), c_ref[rws, cols], preferred_element_type=F32)
            y = term if y is None else y + term
        y_ref[:, cols] = y + d_ref[:, cols] * u[:, cols].astype(F32)
    hfin_ref[...] = h_ref[...]


def _s5(u, bre, bim, lam, cmat, dvec, h0, nb):
    n = u.shape[0]
    t = n // nb
    tt = min(ROW_TILE // nb, t)
    rows = tt * nb
    const = lambda i: (0, 0)
    kern = functools.partial(_s5_kernel, nb=nb, tt=tt)
    return pl.pallas_call(
        kern,
        grid=(t // tt,),
        in_specs=[pl.BlockSpec((rows, S5_WIDTH), lambda i: (i, 0)),
                  pl.BlockSpec((S5_WIDTH, S5_FLAT), const),
                  pl.BlockSpec((S5_WIDTH, S5_FLAT), const),
                  pl.BlockSpec((SUBLANES, S5_FLAT), const),
                  pl.BlockSpec((2 * S5_FLAT, S5_WIDTH), const),
                  pl.BlockSpec((1, S5_WIDTH), const),
                  pl.BlockSpec((nb, 2 * S5_FLAT), const)],
        out_specs=[pl.BlockSpec((rows, S5_WIDTH), lambda i: (i, 0)),
                   pl.BlockSpec((nb, 2 * S5_FLAT), const)],
        out_shape=[jax.ShapeDtypeStruct((n, S5_WIDTH), F32),
                   jax.ShapeDtypeStruct((nb, 2 * S5_FLAT), F32)],
        scratch_shapes=[pltpu.VMEM((S5_WIDTH, 2 * S5_FLAT), BF16),
                        pltpu.VMEM((SUBLANES, S5_FLAT), F32),
                        pltpu.VMEM((rows, 2 * S5_FLAT), F32),
                        pltpu.VMEM((nb, 2 * S5_FLAT), F32)],
        compiler_params=_cparams(("arbitrary",)),
        name="s5",
    )(u, bre, bim, lam, cmat, dvec, h0)


def _postmix_kernel(xp_ref, op_ref, zp_ref, ysp_ref, gap_ref, gbp_ref,
                    xs_ref, os_ref, zs_ref, yss_ref, gas_ref, gbs_ref, *rest, nblk_p):
    @pl.when(pl.program_id(0) < nblk_p)
    def _():
        _postmix_body(xp_ref, op_ref, zp_ref, ysp_ref, gap_ref, gbp_ref, *rest)

    @pl.when(pl.program_id(0) >= nblk_p)
    def _():
        _postmix_body(xs_ref, os_ref, zs_ref, yss_ref, gas_ref, gbs_ref, *rest)


def _postmix_body(x_ref, o_ref, z_ref, ys_ref, ga_ref, gb_ref, hw_ref, seg_ref, wa_ref, wglu_ref, wb_ref,
                  wo_ref, nf_ref, wr_ref, x1_ref, hn_ref, ridx_ref, rw_ref):
    o = o_ref[...]
    ms = jnp.dot((o * o).astype(BF16), seg_ref[...], preferred_element_type=F32) * (1.0 / DN_HEAD_DIM)
    on = o * lax.rsqrt(ms + RMS_EPS) * hw_ref[...]
    z = z_ref[...]
    oa = on * (z * jax.nn.sigmoid(z)).astype(F32)
    y_a = _mm(oa, wa_ref[...])
    ys = jax.nn.gelu(ys_ref[...])
    ys = ys * jax.nn.sigmoid(_mm(ys, wglu_ref[...]))
    y_b = _mm(ys, wb_ref[...])
    mixed = jax.nn.sigmoid(ga_ref[...]).astype(F32) * y_a + jax.nn.sigmoid(gb_ref[...]).astype(F32) * y_b
    x1 = x_ref[...] + _mm(mixed, wo_ref[...])
    x1_ref[...] = x1
    hn = x1 * lax.rsqrt(jnp.mean(x1 * x1, axis=-1, keepdims=True) + RMS_EPS) * nf_ref[...]
    _slab_store(hn_ref, hn)

    wr = wr_ref[...]
    w_hi = wr.astype(BF16)
    w_lo = (wr - w_hi.astype(F32)).astype(BF16)
    hn_hi = hn.astype(BF16)
    hn_lo = (hn - hn_hi.astype(F32)).astype(BF16)
    both = _mm_nt(jnp.concatenate([w_hi, w_lo], axis=0), hn_hi)
    logits = both[:ROUTER_ROWS] + both[ROUTER_ROWS:] + _mm_nt(w_hi, hn_lo)
    coarse = logits[N_EXPERTS:N_EXPERTS + MOE_GROUPS, :]
    cm = jnp.max(coarse, axis=0, keepdims=True)
    ce = jnp.exp(coarse - cm)
    pc = ce / jnp.sum(ce, axis=0, keepdims=True)
    p_sel = jnp.max(pc, axis=0, keepdims=True)
    gi = lax.broadcasted_iota(I32, pc.shape, 0)
    g_sel = jnp.min(jnp.where(pc == p_sel, gi, MOE_GROUPS), axis=0, keepdims=True)
    fine = jnp.zeros((EXPERTS_PER_GROUP, logits.shape[1]), F32)
    for g in range(MOE_GROUPS):
        fine = fine + jnp.where(g_sel == g, logits[g * EXPERTS_PER_GROUP:(g + 1) * EXPERTS_PER_GROUP, :], 0.0)
    fm = jnp.max(fine, axis=0, keepdims=True)
    fe = jnp.exp(fine - fm)
    pf = fe / jnp.sum(fe, axis=0, keepdims=True)
    ei = lax.broadcasted_iota(I32, pf.shape, 0)
    v1 = jnp.max(pf, axis=0, keepdims=True)
    i1 = jnp.min(jnp.where(pf == v1, ei, EXPERTS_PER_GROUP), axis=0, keepdims=True)
    rest = jnp.where(ei == i1, -1.0, pf)
    v2 = jnp.max(rest, axis=0, keepdims=True)
    i2 = jnp.min(jnp.where(rest == v2, ei, EXPERTS_PER_GROUP), axis=0, keepdims=True)
    tot = v1 + v2
    ridx_ref[0:1, :] = g_sel * EXPERTS_PER_GROUP + i1
    ridx_ref[1:2, :] = g_sel * EXPERTS_PER_GROUP + i2
    rw_ref[0:1, :] = v1 / tot * p_sel
    rw_ref[1:2, :] = v2 / tot * p_sel


def _postmix(prompt, sample, weights, nb):
    n_p = prompt[0].shape[0]
    n_s = sample[0].shape[0]
    t = n_p // nb
    tt = min(ROW_TILE, t, n_s)
    nt = t // tt
    nblk_p = n_p // tt
    nblk = nblk_p + n_s // tt
    n_total = n_p + n_s
    prow = lambda i: (jnp.minimum(i, nblk_p - 1), 0)
    pys = lambda i: (jnp.minimum(i, nblk_p - 1) % nt, jnp.minimum(i, nblk_p - 1) // nt)
    srow = lambda i: (jnp.maximum(i - nblk_p, 0), 0)
    const = lambda i: (0, 0)

    def stream_specs(row, ysmap):
        return [pl.BlockSpec((tt, D_MODEL), row),
                pl.BlockSpec((tt, DN_WIDTH), row),
                pl.BlockSpec((tt, DN_WIDTH), row),
                pl.BlockSpec((tt, S5_WIDTH), ysmap),
                pl.BlockSpec((tt, D_MODEL), row),
                pl.BlockSpec((tt, D_MODEL), row)]

    weight_specs = [pl.BlockSpec((1, DN_WIDTH), const),
                    pl.BlockSpec((DN_WIDTH, DN_WIDTH), const),
                    pl.BlockSpec((DN_WIDTH, D_MODEL), const),
                    pl.BlockSpec((S5_WIDTH, S5_WIDTH), const),
                    pl.BlockSpec((S5_WIDTH, D_MODEL), const),
                    pl.BlockSpec((D_MODEL, D_MODEL), const),
                    pl.BlockSpec((1, D_MODEL), const),
                    pl.BlockSpec((ROUTER_ROWS, D_MODEL), const)]
    xp, op, zp, ysp, gap, gbp = prompt
    return pl.pallas_call(
        functools.partial(_postmix_kernel, nblk_p=nblk_p),
        grid=(nblk,),
        in_specs=stream_specs(prow, pys) + stream_specs(srow, srow) + weight_specs,
        out_specs=[pl.BlockSpec((tt, D_MODEL), lambda i: (i, 0)),
                   pl.BlockSpec((tt * ROW_SLAB, LANES), lambda i: (i, 0)),
                   pl.BlockSpec((TOP_K, tt), lambda i: (0, i)),
                   pl.BlockSpec((TOP_K, tt), lambda i: (0, i))],
        out_shape=[jax.ShapeDtypeStruct((n_total, D_MODEL), F32),
                   jax.ShapeDtypeStruct((n_total * ROW_SLAB, LANES), F32),
                   jax.ShapeDtypeStruct((TOP_K, n_total), I32),
                   jax.ShapeDtypeStruct((TOP_K, n_total), F32)],
        compiler_params=_cparams(("arbitrary",)),
        name="postmix",
    )(xp, op, zp, ysp.reshape(t, nb * S5_WIDTH), gap, gbp, *sample, *weights)


def _wait_slabs(buf, sem):
    pltpu.make_async_copy(buf, buf, sem).wait()


def _moe_kernel(texp_ref, tph_ref, tsrc_ref, tnv_ref, otok_ref, hn_hbm, wu_ref, wd_ref, y_ref, hnv, xbuf, sem):
    del texp_ref
    i = pl.program_id(0)
    tm = MOE_TILE
    rs = ROW_SLAB
    nv = tnv_ref[i]
    ph = tph_ref[i]
    range_rows = hnv.shape[0]

    @pl.when(jnp.logical_and(nv > 0, jnp.logical_or(i == 0, ph != tph_ref[jnp.maximum(i - 1, 0)])))
    def _():
        start = pl.multiple_of(ph * range_rows, rs)
        whole = pltpu.make_async_copy(hn_hbm.at[pl.ds(start, range_rows), :], hnv, sem)
        whole.start()
        whole.wait()

    @pl.when(nv == 0)
    def _():
        y_ref[...] = jnp.zeros_like(y_ref)

    @pl.when(nv > 0)
    def _():
        src0 = tsrc_ref[i]
        for r in range(tm):
            tok8 = pl.multiple_of(otok_ref[src0 + r], rs)
            xbuf[pl.ds(r * rs, rs), :] = hnv[pl.ds(tok8, rs), :]
        x = _slab_load(xbuf, tm).astype(BF16)
        hu = jnp.dot(x, wu_ref[0].astype(BF16), preferred_element_type=F32)
        gate = hu[:, :EXPERT_FF]
        up = hu[:, EXPERT_FF:]
        act = gate * jax.nn.sigmoid(gate) * up
        _slab_store(y_ref, jnp.dot(act.astype(BF16), wd_ref[0].astype(BF16), preferred_element_type=F32))


def _moe(hn, w_up, w_down, texp, tph, tsrc, tnv, otok):
    ntiles = texp.shape[0]
    wmap = lambda i, te, tp, ts, tn, ot: (te[i], 0, 0)
    grid_spec = pltpu.PrefetchScalarGridSpec(
        num_scalar_prefetch=5,
        grid=(ntiles,),
        in_specs=[pl.BlockSpec(memory_space=pl.ANY),
                  pl.BlockSpec((1, D_MODEL, 2 * EXPERT_FF), wmap),
                  pl.BlockSpec((1, EXPERT_FF, D_MODEL), wmap)],
        out_specs=pl.BlockSpec((MOE_TILE * ROW_SLAB, LANES), lambda i, te, tp, ts, tn, ot: (i, 0)),
        scratch_shapes=[pltpu.VMEM((hn.shape[0] // MOE_PHASES, LANES), F32),
                        pltpu.VMEM((MOE_TILE * ROW_SLAB, LANES), F32),
                        pltpu.SemaphoreType.DMA])
    return pl.pallas_call(
        _moe_kernel,
        grid_spec=grid_spec,
        out_shape=jax.ShapeDtypeStruct((ntiles * MOE_TILE * ROW_SLAB, LANES), F32),
        compiler_params=_cparams(("arbitrary",)),
        name="moe",
    )(texp, tph, tsrc, tnv, otok, hn, w_up, w_down)


def _combine_kernel(pos_ref, x1_ref, ys_hbm, w_ref, nw_ref, outp_ref, outs_ref, ybuf0, ybuf1, sem, *, nblk_p):
    i = pl.program_id(0)
    nsteps = pl.num_programs(0)
    tt = x1_ref.shape[0]
    rs = ROW_SLAB
    slot = lax.rem(i, 2)
    ybuf = (ybuf0, ybuf1)

    def start_gather(step, sl):
        base = step * (tt * TOP_K)
        for r in range(tt * TOP_K):
            j, s = divmod(r, TOP_K)
            p8 = pl.multiple_of(pos_ref[base + r], rs)
            pltpu.make_async_copy(ys_hbm.at[pl.ds(p8, rs), :], ybuf[sl].at[pl.ds((s * tt + j) * rs, rs), :],
                                  sem.at[sl]).start(priority=r % DMA_QUEUES)

    @pl.when(i == 0)
    def _():
        start_gather(0, 0)

    for sl in range(2):
        @pl.when(slot == sl)
        def _():
            _wait_slabs(ybuf[sl], sem.at[sl])
            start_gather(jnp.minimum(i + 1, nsteps - 1), 1 - sl)
            w = w_ref[...]
            y0 = _slab_load(ybuf[sl], tt, 0)
            y1 = _slab_load(ybuf[sl], tt, tt * rs)
            x = x1_ref[...] + w[:, 0:1] * y0 + w[:, 1:2] * y1
            res = x * lax.rsqrt(jnp.mean(x * x, axis=-1, keepdims=True) + RMS_EPS) * nw_ref[...]

            @pl.when(i < nblk_p)
            def _():
                outp_ref[...] = res

            @pl.when(i >= nblk_p)
            def _():
                outs_ref[...] = res

        @pl.when(jnp.logical_and(slot == sl, i == nsteps - 1))
        def _():
            _wait_slabs(ybuf[1 - sl], sem.at[1 - sl])


def _combine(x1, ysorted, pos8, wtok, nw, n_p):
    n = x1.shape[0]
    tt = math.gcd(math.gcd(n_p, n - n_p), COMBINE_TILE)
    nblk_p = n_p // tt
    grid_spec = pltpu.PrefetchScalarGridSpec(
        num_scalar_prefetch=1,
        grid=(n // tt,),
        in_specs=[pl.BlockSpec((tt, D_MODEL), lambda i, ps: (i, 0)),
                  pl.BlockSpec(memory_space=pl.ANY),
                  pl.BlockSpec((tt, TOP_K), lambda i, ps: (i, 0)),
                  pl.BlockSpec((1, D_MODEL), lambda i, ps: (0, 0))],
        out_specs=[pl.BlockSpec((tt, D_MODEL), lambda i, ps: (jnp.minimum(i, nblk_p - 1), 0)),
                   pl.BlockSpec((tt, D_MODEL), lambda i, ps: (jnp.maximum(i - nblk_p, 0), 0))],
        scratch_shapes=[pltpu.VMEM((tt * TOP_K * ROW_SLAB, LANES), F32),
                        pltpu.VMEM((tt * TOP_K * ROW_SLAB, LANES), F32),
                        pltpu.SemaphoreType.DMA((2,))])
    return pl.pallas_call(
        functools.partial(_combine_kernel, nblk_p=nblk_p),
        grid_spec=grid_spec,
        out_shape=[jax.ShapeDtypeStruct((n_p, D_MODEL), F32),
                   jax.ShapeDtypeStruct((n - n_p, D_MODEL), F32)],
        compiler_params=_cparams(("arbitrary",)),
        name="combine",
    )(pos8, x1, ysorted, wtok, nw)


def _route_plan(ridx, n_tok):
    tm = MOE_TILE
    n_assign = n_tok * TOP_K
    nbk = MOE_PHASES * N_EXPERTS
    ntiles = n_assign // tm + nbk
    range_tok = n_tok // MOE_PHASES
    e_flat = ridx.T.reshape(n_assign)
    tok = jnp.arange(n_assign, dtype=I32) // TOP_K
    b_flat = (tok // range_tok) * N_EXPERTS + e_flat
    order = jnp.argsort(b_flat, stable=True).astype(I32)
    boh = (b_flat[:, None] == jnp.arange(nbk, dtype=I32)[None, :]).astype(I32)
    counts = jnp.sum(boh, axis=0)
    cstart = jnp.cumsum(counts) - counts
    tiles_b = (counts + tm - 1) // tm
    tend = jnp.cumsum(tiles_b)
    tstart = tend - tiles_b
    tile_id = jnp.arange(ntiles, dtype=I32)
    tbk = jnp.minimum(jnp.sum((tile_id[:, None] >= tend[None, :]).astype(I32), axis=1), nbk - 1)
    onehot = (tbk[:, None] == jnp.arange(nbk, dtype=I32)[None, :]).astype(I32)
    pick = lambda v: jnp.sum(onehot * v[None, :], axis=1)
    done = (tile_id - pick(tstart)) * tm
    tnv = jnp.where(tile_id < tend[-1], jnp.clip(pick(counts) - done, 0, tm), 0)
    tsrc = jnp.where(tnv > 0, pick(cstart) + done, 0)
    texp = tbk % N_EXPERTS
    tph = tbk // N_EXPERTS
    otok8 = jnp.concatenate([((order // TOP_K) % range_tok) * ROW_SLAB, jnp.zeros((tm,), I32)])
    rank = jnp.argsort(order).astype(I32)
    pos8 = (rank + jnp.sum(boh * (tstart * tm - cstart)[None, :], axis=1)) * ROW_SLAB
    return texp.astype(I32), tph.astype(I32), tsrc.astype(I32), tnv.astype(I32), otok8, pos8.astype(I32)


def _block_diag(m):
    g, a, b = m.shape
    eye = jnp.eye(g, dtype=m.dtype)
    return (eye[:, None, :, None] * m[:, :, None, :]).reshape(g * a, g * b)


def kernel(x_prompt, x_sample, state_conv, state_delta, state_ssm_re, state_ssm_im, norm_mix_w, w_in, conv_w, a_log, dt_bias, head_norm_w, w_a_up, s5_lambda_re, s5_lambda_im, s5_log_step, s5_b_re, s5_b_im, s5_c_re, s5_c_im, s5_d, w_glu, w_b_up, w_o, norm_ffn_w, w_router_coarse, w_router_fine, w_expert_up, w_expert_down, norm_final_w):
    bp, tp, _ = x_prompt.shape
    bs, ts, _ = x_sample.shape
    n_p = bp * tp
    n_s = bs * ts
    n_tok = n_p + n_s
    l = 0

    w = w_in[l]
    cuts = np.cumsum([0, QKV_DIM, DN_WIDTH, DN_HEADS, DN_HEADS, S5_WIDTH, D_MODEL, D_MODEL])
    w_qkv, w_z, w_a, w_b, w_u, w_ga, w_gb = [w[:, cuts[i]:cuts[i + 1]] for i in range(7)]
    w_ab = jnp.concatenate([w_a, w_b, jnp.zeros((D_MODEL, LANES - 2 * DN_HEADS), F32)], axis=1)
    wcat = jnp.concatenate([w_qkv, w_z, w_u, w_ga, w_gb, w_ab], axis=1).astype(BF16)
    nw_mix = norm_mix_w[l].reshape(1, D_MODEL)
    pad8 = lambda v: jnp.concatenate([v, jnp.zeros((LANES - DN_HEADS,), F32)]).reshape(1, LANES)
    gate_p = jnp.concatenate([pad8(a_log[l]), pad8(dt_bias[l])], axis=0)
    seg = _block_diag(jnp.ones((DN_HEADS, DN_HEAD_DIM, DN_HEAD_DIM), BF16))
    bre = _block_diag(jnp.swapaxes(s5_b_re[l], 1, 2))
    bim = _block_diag(jnp.swapaxes(s5_b_im[l], 1, 2))
    lam = jnp.concatenate([s5_lambda_re[l].reshape(1, S5_FLAT), s5_lambda_im[l].reshape(1, S5_FLAT),
                           jnp.repeat(s5_log_step[l], S5_STATE).reshape(1, S5_FLAT),
                           jnp.zeros((SUBLANES - 3, S5_FLAT), F32)], axis=0)
    cmat = jnp.concatenate([_block_diag(jnp.swapaxes(s5_c_re[l], 1, 2)),
                            -_block_diag(jnp.swapaxes(s5_c_im[l], 1, 2))], axis=0).astype(BF16)
    dvec = s5_d[l].reshape(1, S5_WIDTH)
    hw = jnp.tile(head_norm_w[l], DN_HEADS).reshape(1, DN_WIDTH)
    wr = jnp.concatenate([w_router_fine[l].T, w_router_coarse[l].T,
                          jnp.zeros((ROUTER_ROWS - N_EXPERTS - MOE_GROUPS, D_MODEL), F32)], axis=0)
    pm_weights = (hw, seg, w_a_up[l].astype(BF16), w_glu[l].astype(BF16), w_b_up[l].astype(BF16),
                  w_o[l].astype(BF16), norm_ffn_w[l].reshape(1, D_MODEL), wr)

    xp2 = x_prompt.reshape(n_p, D_MODEL)
    q_p, k_p, v_p, gates_p, conv_p, z_p, u_p, ga_p, gb_p = _inprep(
        xp2, nw_mix, wcat, jnp.zeros((bp, SUBLANES, QKV_DIM), F32), conv_w[l], gate_p, seg, bp, 1)
    o_p, delta_p = _delta_prompt(q_p, k_p, v_p, gates_p, bp)
    ys_p, h_p = _s5(u_p, bre, bim, lam, cmat, dvec, jnp.zeros((bp, 2 * S5_FLAT), F32), bp)

    xs2 = jnp.swapaxes(x_sample, 0, 1).reshape(n_s, D_MODEL)
    cinit_s = jnp.swapaxes(state_conv[l], 0, 1).reshape(1, (CONV_W - 1) * bs, QKV_DIM)
    q_s, k_s, v_s, gate_s, conv_s, z_s, u_s, ga_s, gb_s = _inprep(
        xs2, nw_mix, wcat, cinit_s, conv_w[l], gate_p, seg, 1, bs)
    s0 = state_delta[l].reshape(bs, DN_HEADS * DN_HEAD_DIM * DN_HEAD_DIM)
    o_s, delta_s = _delta_sample(q_s, k_s, v_s, gate_s, s0, bs, ts)
    h0_s = jnp.concatenate([state_ssm_re[l].reshape(bs, S5_FLAT), state_ssm_im[l].reshape(bs, S5_FLAT)], axis=1)
    ys_s, h_s = _s5(u_s, bre, bim, lam, cmat, dvec, h0_s, bs)
    x1, hn, ridx, rw = _postmix((xp2, o_p, z_p, ys_p, ga_p, gb_p), (xs2, o_s, z_s, ys_s, ga_s, gb_s),
                                pm_weights, bp)

    texp, tph, tsrc, tnv, otok8, pos8 = _route_plan(ridx, n_tok)
    ysorted = _moe(hn, w_expert_up[l], w_expert_down[l], texp, tph, tsrc, tnv, otok8)
    y_p, y_s = _combine(x1, ysorted, pos8, rw.T, norm_final_w.reshape(1, D_MODEL), n_p)

    y_prompt = y_p.reshape(bp, tp, D_MODEL)
    y_sample = jnp.swapaxes(y_s.reshape(ts, bs, D_MODEL), 0, 1)
    conv_sample = jnp.swapaxes(conv_s.reshape(CONV_W - 1, bs, QKV_DIM), 0, 1)
    return (y_prompt, y_sample,
            conv_p[None], delta_p[None],
            h_p[:, :S5_FLAT].reshape(1, bp, S5_GROUPS, S5_STATE), h_p[:, S5_FLAT:].reshape(1, bp, S5_GROUPS, S5_STATE),
            conv_sample[None], delta_s.reshape(1, bs, DN_HEADS, DN_HEAD_DIM, DN_HEAD_DIM),
            h_s[:, :S5_FLAT].reshape(1, bs, S5_GROUPS, S5_STATE), h_s[:, S5_FLAT:].reshape(1, bs, S5_GROUPS, S5_STATE))
```

```python
import functools
import math

import jax
import jax.numpy as jnp
import numpy as np
from jax import lax
from jax.experimental import pallas as pl
from jax.experimental.pallas import tpu as pltpu

F32 = jnp.float32
BF16 = jnp.bfloat16
I32 = jnp.int32

D_MODEL = 1024
DN_HEADS = 8
DN_HEAD_DIM = 64
DN_WIDTH = DN_HEADS * DN_HEAD_DIM
QKV_DIM = 3 * DN_WIDTH
CONV_W = 4
DN_CHUNK = 64
S5_GROUP_CH = 16
S5_WIDTH = D_MODEL // 2
S5_GROUPS = S5_WIDTH // S5_GROUP_CH
S5_STATE = 64
S5_FLAT = S5_GROUPS * S5_STATE
MOE_GROUPS = 4
EXPERTS_PER_GROUP = 8
N_EXPERTS = MOE_GROUPS * EXPERTS_PER_GROUP
TOP_K = 2
EXPERT_FF = 256
RMS_EPS = 1e-6
L2_EPS = 1e-6

LANES = 128
SUBLANES = 8
VMEM_LIMIT = 56 * 1024 * 1024

C_QKV, C_Z, C_U, C_GA, C_GB, C_AB = 0, 1536, 2048, 2560, 3584, 4608
IN_PACKED = C_AB + LANES

ROW_TILE = 512
MOE_TILE = 256
MOE_PHASES = 2
COMBINE_TILE = 256
DMA_QUEUES = 2
DELTA_SUBCHUNKS = 4
S5_SUPER = 2
S5_SCAN_SPLIT = 2
ROUTER_ROWS = 40


def _mm(a, b):
    return jnp.dot(a.astype(BF16), b.astype(BF16), preferred_element_type=F32)


def _mm_nt(a, b):
    return lax.dot_general(a.astype(BF16), b.astype(BF16), (((1,), (1,)), ((), ())),
                           preferred_element_type=F32)


def _split3_dot(a, b01):
    a1 = a.astype(BF16)
    r1 = a - a1.astype(F32)
    a2 = r1.astype(BF16)
    a3 = (r1 - a2.astype(F32)).astype(BF16)
    out = jnp.dot(a3, b01, preferred_element_type=F32)
    out = out + jnp.dot(a2, b01, preferred_element_type=F32)
    return out + jnp.dot(a1, b01, preferred_element_type=F32)


def _cparams(sem):
    return pltpu.CompilerParams(dimension_semantics=sem, vmem_limit_bytes=VMEM_LIMIT)


ROW_SLAB = D_MODEL // LANES


def _slab_load(ref, rows, first=0, pitch=ROW_SLAB):
    return jnp.concatenate([ref[pl.ds(first + j, rows, stride=pitch), :] for j in range(ROW_SLAB)], axis=1)


def _slab_store(ref, x):
    for j in range(ROW_SLAB):
        ref[pl.ds(j, x.shape[0], stride=ROW_SLAB), :] = x[:, j * LANES:(j + 1) * LANES]


def _softplus(x):
    return jnp.maximum(x, 0.0) + jnp.log1p(jnp.exp(-jnp.abs(x)))


def _inprep_kernel(x_ref, nw_ref, w_ref, cinit_ref, cw_ref, gp_ref, seg_ref,
                   q_ref, k_ref, v_ref, gate_ref, cnew_ref, z_ref, u_ref, ga_ref, gb_ref, xp_ref,
                   *, shift, rc, rows):
    x = x_ref[...]
    h = x * lax.rsqrt(jnp.mean(x * x, axis=-1, keepdims=True) + RMS_EPS) * nw_ref[...]
    hb = h.astype(BF16)

    def proj(lo, hi):
        return jnp.dot(hb, w_ref[:, lo:hi], preferred_element_type=F32)

    @pl.when(pl.program_id(1) == 0)
    def _():
        xp_ref[0:rc, :] = cinit_ref[0]

    xp_ref[rc:rc + rows, :] = proj(C_QKV, C_Z)
    ab = proj(C_AB, IN_PACKED)
    z_ref[...] = proj(C_Z, C_U).astype(z_ref.dtype)
    u_ref[...] = proj(C_U, C_GA).astype(u_ref.dtype)
    ga_ref[...] = proj(C_GA, C_GB).astype(ga_ref.dtype)
    gb_ref[...] = proj(C_GB, C_AB).astype(gb_ref.dtype)
    acc = None
    for i in range(CONV_W):
        lo = rc + (i - (CONV_W - 1)) * shift
        term = xp_ref[lo:lo + rows, :] * cw_ref[i:i + 1, :]
        acc = term if acc is None else acc + term
    y = acc * jax.nn.sigmoid(acc)
    keep = (CONV_W - 1) * shift
    cnew_ref[0] = xp_ref[rc + rows - keep:rc + rows, :]
    xp_ref[0:rc, :] = xp_ref[rows:rows + rc, :]

    seg = seg_ref[...]
    q = y[:, 0:DN_WIDTH]
    k = y[:, DN_WIDTH:2 * DN_WIDTH]
    q_ref[...] = q * lax.rsqrt(jnp.dot((q * q).astype(BF16), seg, preferred_element_type=F32) + L2_EPS)
    k_ref[...] = k * lax.rsqrt(jnp.dot((k * k).astype(BF16), seg, preferred_element_type=F32) + L2_EPS)
    v_ref[...] = y[:, 2 * DN_WIDTH:]

    g = -jnp.exp(gp_ref[0:1, :]) * _softplus(ab + gp_ref[1:2, :])
    beta = jax.nn.sigmoid(ab)
    lane = lax.broadcasted_iota(I32, ab.shape, 1)
    gate_ref[...] = jnp.where(lane < DN_HEADS, g, beta)


def _inprep(x2d, nw, wcat, cinit, conv_w, gate_p, seg, nb, shift):
    n = x2d.shape[0]
    r = n // nb
    rows = min(ROW_TILE, r)
    nt = r // rows
    rc = cinit.shape[1]
    keep = (CONV_W - 1) * shift
    row = lambda b, i: (b * nt + i, 0)
    const = lambda b, i: (0, 0)
    kern = functools.partial(_inprep_kernel, shift=shift, rc=rc, rows=rows)
    outs = pl.pallas_call(
        kern,
        grid=(nb, nt),
        in_specs=[pl.BlockSpec((rows, D_MODEL), row),
                  pl.BlockSpec((1, D_MODEL), const),
                  pl.BlockSpec((D_MODEL, IN_PACKED), const),
                  pl.BlockSpec((1, rc, QKV_DIM), lambda b, i: (b, 0, 0)),
                  pl.BlockSpec((CONV_W, QKV_DIM), const),
                  pl.BlockSpec((2, LANES), const),
                  pl.BlockSpec((DN_WIDTH, DN_WIDTH), const)],
        out_specs=[pl.BlockSpec((rows, DN_WIDTH), row),
                   pl.BlockSpec((rows, DN_WIDTH), row),
                   pl.BlockSpec((rows, DN_WIDTH), row),
                   pl.BlockSpec((rows, LANES), row),
                   pl.BlockSpec((1, keep, QKV_DIM), lambda b, i: (b, 0, 0)),
                   pl.BlockSpec((rows, DN_WIDTH), row),
                   pl.BlockSpec((rows, S5_WIDTH), lambda b, i: (i, b)),
                   pl.BlockSpec((rows, D_MODEL), row),
                   pl.BlockSpec((rows, D_MODEL), row)],
        out_shape=[jax.ShapeDtypeStruct((n, DN_WIDTH), F32),
                   jax.ShapeDtypeStruct((n, DN_WIDTH), F32),
                   jax.ShapeDtypeStruct((n, DN_WIDTH), F32),
                   jax.ShapeDtypeStruct((n, LANES), F32),
                   jax.ShapeDtypeStruct((nb, keep, QKV_DIM), F32),
                   jax.ShapeDtypeStruct((n, DN_WIDTH), BF16),
                   jax.ShapeDtypeStruct((r, nb * S5_WIDTH), BF16),
                   jax.ShapeDtypeStruct((n, D_MODEL), BF16),
                   jax.ShapeDtypeStruct((n, D_MODEL), BF16)],
        scratch_shapes=[pltpu.VMEM((rc + rows, QKV_DIM), F32)],
        compiler_params=_cparams(("arbitrary", "arbitrary")),
        name="inprep",
    )(x2d, nw, wcat, cinit, conv_w, gate_p, seg)
    q, k, v, gate, cnew, z, u, ga, gb = outs
    return q, k, v, gate, cnew, z, u.reshape(r * nb, S5_WIDTH), ga, gb


def _delta_chunk_kernel(q_ref, k_ref, v_ref, gate_ref, tril_ref, o_ref, sfin_ref, s_ref, *, nsub):
    c = DN_CHUNK
    dk = DN_HEAD_DIM

    @pl.when(pl.program_id(1) == 0)
    def _():
        s_ref[...] = jnp.zeros_like(s_ref)

    rowi = lax.broadcasted_iota(I32, (c, c), 0)
    coli = lax.broadcasted_iota(I32, (c, c), 1)
    causal = rowi >= coli
    strict = rowi > coli
    tril = tril_ref[...]
    pairs = [(j, h) for j in range(nsub) for h in range(DN_HEADS)]
    rows = [slice(j * c, (j + 1) * c) for j in range(nsub)]
    gate = [gate_ref[rows[j], :] for j in range(nsub)]
    gc_all = [_split3_dot_left(tril, gate[j]) for j in range(nsub)]
    gc_t = [gc_all[j].T for j in range(nsub)]

    def head(ref, j, h):
        return ref[rows[j], h * dk:(h + 1) * dk]

    qh = {p: head(q_ref, *p) * (dk ** -0.5) for p in pairs}
    kh = {p: head(k_ref, *p) for p in pairs}
    gcol = {(j, h): gc_all[j][:, h:h + 1] for j, h in pairs}
    beta = {(j, h): gate[j][:, DN_HEADS + h:DN_HEADS + h + 1] for j, h in pairs}
    grow2 = {(j, h): jnp.concatenate([gc_t[j][h:h + 1, :], gc_t[j][h:h + 1, :]], axis=1) for j, h in pairs}
    rowi2 = lax.broadcasted_iota(I32, (c, 2 * c), 0)
    coli2 = lax.broadcasted_iota(I32, (c, 2 * c), 1) & (c - 1)
    causal2 = rowi2 >= coli2
    strict2 = rowi2 > coli2
    decay = {p: jnp.where(causal2, jnp.exp(jnp.where(causal2, gcol[p] - grow2[p], 0.0)), 0.0) for p in pairs}
    kb = {p: kh[p] * beta[p] for p in pairs}
    egc = {p: jnp.exp(gcol[p]) for p in pairs}
    gram = {p: _mm_nt(jnp.concatenate([kb[p], qh[p]], axis=0), jnp.concatenate([kh[p], kh[p]], axis=0))
            for p in pairs}
    mat = {p: jnp.where(strict2, gram[p][:c] * decay[p], 0.0).astype(BF16) for p in pairs}
    qk = {p: jnp.where(causal, gram[p][c:, :c] * decay[p][:, :c], 0.0) for p in pairs}
    sol = {p: jnp.concatenate([head(v_ref, *p) * beta[p], kb[p] * egc[p]], axis=1) for p in pairs}
    levels = int(math.log2(c))
    zeros2 = jnp.zeros((c, 2 * c), BF16)
    for lvl in range(levels):
        hi = {p: sol[p].astype(BF16) for p in pairs}
        lo = {p: (sol[p] - hi[p].astype(F32)).astype(BF16) for p in pairs}
        if lvl < levels - 1:
            y = {p: jnp.dot(mat[p], jnp.concatenate([jnp.concatenate([hi[p], mat[p]], axis=1),
                                                     jnp.concatenate([lo[p], zeros2], axis=1)], axis=0),
                            preferred_element_type=F32) for p in pairs}
            mat = {p: y[p][:, 2 * dk:].astype(BF16) for p in pairs}
            upd = {p: y[p][:, :2 * dk] for p in pairs}
        else:
            upd = {p: jnp.dot(mat[p], jnp.concatenate([hi[p], lo[p]], axis=0), preferred_element_type=F32)
                   for p in pairs}
        sol = {p: (sol[p] - upd[p]) if lvl == 0 else (sol[p] + upd[p]) for p in pairs}
    g_last = {(j, h): gc_all[j][c - 1:c, h:h + 1] for j, h in pairs}
    wq = {p: jnp.concatenate([sol[p][:, dk:], qh[p] * egc[p]], axis=0) for p in pairs}
    k_dec_t = {p: (kh[p] * jnp.exp(g_last[p] - gcol[p])).T for p in pairs}
    d_last = {p: jnp.exp(g_last[p]) for p in pairs}

    s = [s_ref[h] for h in range(DN_HEADS)]
    for j in range(nsub):
        ws = [_mm(wq[j, h], s[h]) for h in range(DN_HEADS)]
        v_new = [sol[j, h][:, :dk] - ws[h][:c] for h in range(DN_HEADS)]
        o_ref[rows[j], :] = jnp.concatenate(
            [ws[h][c:] + _mm(qk[j, h], v_new[h]) for h in range(DN_HEADS)], axis=1)
        s = [s[h] * d_last[j, h] + _mm(k_dec_t[j, h], v_new[h]) for h in range(DN_HEADS)]
    for h in range(DN_HEADS):
        s_ref[h] = s[h]
    sfin_ref[0] = s_ref[...]


def _split3_dot_left(b01, a):
    a1 = a.astype(BF16)
    r1 = a - a1.astype(F32)
    a2 = r1.astype(BF16)
    a3 = (r1 - a2.astype(F32)).astype(BF16)
    out = jnp.dot(b01, a3, preferred_element_type=F32)
    out = out + jnp.dot(b01, a2, preferred_element_type=F32)
    return out + jnp.dot(b01, a1, preferred_element_type=F32)


def _delta_prompt(q, k, v, gate, nb):
    n = q.shape[0]
    t = n // nb
    c = DN_CHUNK
    nsub = DELTA_SUBCHUNKS
    rows = nsub * c
    nc = t // rows
    row = lambda b, i: (b * nc + i, 0)
    tril = jnp.tril(jnp.ones((c, c), F32)).astype(BF16)
    return pl.pallas_call(
        functools.partial(_delta_chunk_kernel, nsub=nsub),
        grid=(nb, nc),
        in_specs=[pl.BlockSpec((rows, DN_WIDTH), row),
                  pl.BlockSpec((rows, DN_WIDTH), row),
                  pl.BlockSpec((rows, DN_WIDTH), row),
                  pl.BlockSpec((rows, LANES), row),
                  pl.BlockSpec((c, c), lambda b, i: (0, 0))],
        out_specs=[pl.BlockSpec((rows, DN_WIDTH), row),
                   pl.BlockSpec((1, DN_HEADS, DN_HEAD_DIM, DN_HEAD_DIM), lambda b, i: (b, 0, 0, 0))],
        out_shape=[jax.ShapeDtypeStruct((n, DN_WIDTH), F32),
                   jax.ShapeDtypeStruct((nb, DN_HEADS, DN_HEAD_DIM, DN_HEAD_DIM), F32)],
        scratch_shapes=[pltpu.VMEM((DN_HEADS, DN_HEAD_DIM, DN_HEAD_DIM), F32)],
        compiler_params=_cparams(("arbitrary", "arbitrary")),
        name="delta_prompt",
    )(q, k, v, gate, tril)


def _delta_step_kernel(q_ref, k_ref, v_ref, gate_ref, s0_ref, ex_ref, o_ref, s_ref, kx_ref, qx_ref, *, nt, nb):
    dk = DN_HEAD_DIM
    flat = dk * dk
    nv = flat // LANES
    p = pl.program_id(0)
    lane = lax.broadcasted_iota(I32, (SUBLANES, LANES), 1)
    low = lane < dk
    ex = ex_ref[...]

    for t in range(nt):
        rs = slice(t * nb, (t + 1) * nb)
        kx_ref[...] = jnp.dot(k_ref[rs, :].astype(BF16), ex, preferred_element_type=F32)
        qx_ref[...] = jnp.dot((q_ref[rs, :] * (dk ** -0.5)).astype(BF16), ex, preferred_element_type=F32)
        src_ref = s0_ref if t == 0 else s_ref

        def tile_body(bt, carry, t=t, src_ref=src_ref):
            b0 = pl.multiple_of(bt * SUBLANES, SUBLANES)
            r0 = pl.multiple_of(t * nb + b0, SUBLANES)
            gate = gate_ref[pl.ds(r0, SUBLANES), :]
            vv = v_ref[pl.ds(r0, SUBLANES), :]
            vsw = pltpu.roll(vv, dk, axis=1)
            o_pair = None
            for j in range(2):
                c0 = j * flat
                g = jnp.sum(jnp.where(lane == 2 * p + j, gate, 0.0), axis=1, keepdims=True)
                beta = jnp.sum(jnp.where(lane == 2 * p + j + DN_HEADS, gate, 0.0), axis=1, keepdims=True)
                a = jnp.exp(g)
                vdup = jnp.where(low, vv, vsw) if j == 0 else jnp.where(low, vsw, vv)
                s = [src_ref[pl.ds(b0, SUBLANES), c0 + i * LANES:c0 + (i + 1) * LANES] for i in range(nv)]
                kx = [kx_ref[pl.ds(b0, SUBLANES), c0 + i * LANES:c0 + (i + 1) * LANES] for i in range(nv)]
                ks = kx[0] * s[0]
                for i in range(1, nv):
                    ks = ks + kx[i] * s[i]
                ks = ks + pltpu.roll(ks, dk, axis=1)
                delta = beta * (vdup - a * ks)
                oh = None
                for i in range(nv):
                    si = a * s[i] + kx[i] * delta
                    s_ref[pl.ds(b0, SUBLANES), c0 + i * LANES:c0 + (i + 1) * LANES] = si
                    term = qx_ref[pl.ds(b0, SUBLANES), c0 + i * LANES:c0 + (i + 1) * LANES] * si
                    oh = term if oh is None else oh + term
                oh = oh + pltpu.roll(oh, dk, axis=1)
                o_pair = oh if j == 0 else jnp.where(low, o_pair, oh)
            o_ref[pl.ds(r0, SUBLANES), :] = o_pair
            return carry

        lax.fori_loop(0, nb // SUBLANES, tile_body, 0)


def _delta_sample(q, k, v, gate, s0, nb, nt):
    dk = DN_HEAD_DIM
    flat = dk * dk
    n = nt * nb
    col = np.arange(2 * flat)
    ex = np.arange(LANES)[:, None] == ((col // flat) * dk + (col % flat) // dk)[None, :]
    ex = jnp.asarray(ex, BF16)
    kern = functools.partial(_delta_step_kernel, nt=nt, nb=nb)
    pair = lambda p: (0, p)
    return pl.pallas_call(
        kern,
        grid=(DN_HEADS // 2,),
        in_specs=[pl.BlockSpec((n, LANES), pair),
                  pl.BlockSpec((n, LANES), pair),
                  pl.BlockSpec((n, LANES), pair),
                  pl.BlockSpec((n, LANES), lambda p: (0, 0)),
                  pl.BlockSpec((nb, 2 * flat), pair),
                  pl.BlockSpec((LANES, 2 * flat), lambda p: (0, 0))],
        out_specs=[pl.BlockSpec((n, LANES), pair),
                   pl.BlockSpec((nb, 2 * flat), pair)],
        out_shape=[jax.ShapeDtypeStruct((n, DN_WIDTH), F32),
                   jax.ShapeDtypeStruct((nb, DN_HEADS * flat), F32)],
        scratch_shapes=[pltpu.VMEM((nb, 2 * flat), F32),
                        pltpu.VMEM((nb, 2 * flat), F32)],
        compiler_params=_cparams(("arbitrary",)),
        name="delta_sample",
    )(q, k, v, gate, s0, ex)


def _s5_kernel(u_ref, bre_ref, bim_ref, lam_ref, c_ref, d_ref, h0_ref, y_ref, hfin_ref,
               bw_ref, ab_ref, x_ref, h_ref, *, nb, tt):
    p2 = S5_FLAT

    @pl.when(pl.program_id(0) == 0)
    def _():
        lr = lam_ref[0:1, :]
        li = lam_ref[1:2, :]
        dt = jnp.exp(lam_ref[2:3, :])
        mag = jnp.exp(lr * dt)
        ab_re = mag * jnp.cos(li * dt)
        ab_im = mag * jnp.sin(li * dt)
        den = lr * lr + li * li
        nr = ab_re - 1.0
        ni = ab_im
        f_re = (nr * lr + ni * li) / den
        f_im = (ni * lr - nr * li) / den
        ab_ref[0:1, :] = ab_re
        ab_ref[1:2, :] = ab_im
        bre = bre_ref[...]
        bim = bim_ref[...]
        bw_ref[:, 0:p2] = (bre * f_re - bim * f_im).astype(BF16)
        bw_ref[:, p2:2 * p2] = (bim * f_re + bre * f_im).astype(BF16)
        h_ref[...] = h0_ref[...]

    u = u_ref[...]
    cw = S5_WIDTH // S5_SUPER
    sw = S5_FLAT // S5_SUPER
    for part in (0, p2):
        for b in range(S5_SUPER):
            x_ref[:, part + b * sw:part + (b + 1) * sw] = jnp.dot(
                u[:, b * cw:(b + 1) * cw], bw_ref[b * cw:(b + 1) * cw, part + b * sw:part + (b + 1) * sw],
                preferred_element_type=F32)
    a_re = ab_ref[0:1, :]
    a_im = ab_ref[1:2, :]

    if nb == SUBLANES:
        wsl = p2 // S5_SCAN_SPLIT
        for sp in range(S5_SCAN_SPLIT):
            c0 = sp * wsl
            are = jnp.broadcast_to(a_re[:, c0:c0 + wsl], (nb, wsl))
            aim = jnp.broadcast_to(a_im[:, c0:c0 + wsl], (nb, wsl))

            def step(t, carry, c0=c0, are=are, aim=aim):
                hr, hi = carry
                r0 = pl.multiple_of(t * nb, nb)
                nr = are * hr - aim * hi + x_ref[pl.ds(r0, nb), c0:c0 + wsl]
                ni = are * hi + aim * hr + x_ref[pl.ds(r0, nb), p2 + c0:p2 + c0 + wsl]
                x_ref[pl.ds(r0, nb), c0:c0 + wsl] = nr
                x_ref[pl.ds(r0, nb), p2 + c0:p2 + c0 + wsl] = ni
                return nr, ni

            hr, hi = lax.fori_loop(0, tt, step, (h_ref[:, c0:c0 + wsl], h_ref[:, p2 + c0:p2 + c0 + wsl]),
                                   unroll=2)
            h_ref[:, c0:c0 + wsl] = hr
            h_ref[:, p2 + c0:p2 + c0 + wsl] = hi
    else:
        for t in range(tt):
            rs = slice(t * nb, (t + 1) * nb)
            hr = h_ref[:, 0:p2]
            hi = h_ref[:, p2:2 * p2]
            nr = a_re * hr - a_im * hi + x_ref[rs, 0:p2]
            ni = a_re * hi + a_im * hr + x_ref[rs, p2:2 * p2]
            h_ref[:, 0:p2] = nr
            h_ref[:, p2:2 * p2] = ni
            x_ref[rs, 0:p2] = nr
            x_ref[rs, p2:2 * p2] = ni

    for b in range(S5_SUPER):
        cols = slice(b * cw, (b + 1) * cw)
        y = None
        for part in (0, p2):
            rws = slice(part + b * sw, part + (b + 1) * sw)
            term = jnp.dot(x_ref[:, rws].astype(BF16), c_ref[rws, cols], preferred_element_type=F32)
            y = term if y is None else y + term
        y_ref[:, cols] = y + d_ref[:, cols] * u[:, cols].astype(F32)
    hfin_ref[...] = h_ref[...]


def _s5(u, bre, bim, lam, cmat, dvec, h0, nb):
    n = u.shape[0]
    t = n // nb
    tt = min(ROW_TILE // nb, t)
    rows = tt * nb
    const = lambda i: (0, 0)
    kern = functools.partial(_s5_kernel, nb=nb, tt=tt)
    return pl.pallas_call(
        kern,
        grid=(t // tt,),
        in_specs=[pl.BlockSpec((rows, S5_WIDTH), lambda i: (i, 0)),
                  pl.BlockSpec((S5_WIDTH, S5_FLAT), const),
                  pl.BlockSpec((S5_WIDTH, S5_FLAT), const),
                  pl.BlockSpec((SUBLANES, S5_FLAT), const),
                  pl.BlockSpec((2 * S5_FLAT, S5_WIDTH), const),
                  pl.BlockSpec((1, S5_WIDTH), const),
                  pl.BlockSpec((nb, 2 * S5_FLAT), const)],
        out_specs=[pl.BlockSpec((rows, S5_WIDTH), lambda i: (i, 0)),
                   pl.BlockSpec((nb, 2 * S5_FLAT), const)],
        out_shape=[jax.ShapeDtypeStruct((n, S5_WIDTH), F32),
                   jax.ShapeDtypeStruct((nb, 2 * S5_FLAT), F32)],
        scratch_shapes=[pltpu.VMEM((S5_WIDTH, 2 * S5_FLAT), BF16),
                        pltpu.VMEM((SUBLANES, S5_FLAT), F32),
                        pltpu.VMEM((rows, 2 * S5_FLAT), F32),
                        pltpu.VMEM((nb, 2 * S5_FLAT), F32)],
        compiler_params=_cparams(("arbitrary",)),
        name="s5",
    )(u, bre, bim, lam, cmat, dvec, h0)


def _postmix_kernel(xp_ref, op_ref, zp_ref, ysp_ref, gap_ref, gbp_ref,
                    xs_ref, os_ref, zs_ref, yss_ref, gas_ref, gbs_ref, *rest, nblk_p):
    @pl.when(pl.program_id(0) < nblk_p)
    def _():
        _postmix_body(xp_ref, op_ref, zp_ref, ysp_ref, gap_ref, gbp_ref, *rest)

    @pl.when(pl.program_id(0) >= nblk_p)
    def _():
        _postmix_body(xs_ref, os_ref, zs_ref, yss_ref, gas_ref, gbs_ref, *rest)


def _postmix_body(x_ref, o_ref, z_ref, ys_ref, ga_ref, gb_ref, hw_ref, seg_ref, wa_ref, wglu_ref, wb_ref,
                  wo_ref, nf_ref, wr_ref, x1_ref, hn_ref, ridx_ref, rw_ref):
    o = o_ref[...]
    ms = jnp.dot((o * o).astype(BF16), seg_ref[...], preferred_element_type=F32) * (1.0 / DN_HEAD_DIM)
    on = o * lax.rsqrt(ms + RMS_EPS) * hw_ref[...]
    z = z_ref[...]
    oa = on * (z * jax.nn.sigmoid(z)).astype(F32)
    y_a = _mm(oa, wa_ref[...])
    ys = jax.nn.gelu(ys_ref[...])
    ys = ys * jax.nn.sigmoid(_mm(ys, wglu_ref[...]))
    y_b = _mm(ys, wb_ref[...])
    mixed = jax.nn.sigmoid(ga_ref[...]).astype(F32) * y_a + jax.nn.sigmoid(gb_ref[...]).astype(F32) * y_b
    x1 = x_ref[...] + _mm(mixed, wo_ref[...])
    x1_ref[...] = x1
    hn = x1 * lax.rsqrt(jnp.mean(x1 * x1, axis=-1, keepdims=True) + RMS_EPS) * nf_ref[...]
    _slab_store(hn_ref, hn)

    wr = wr_ref[...]
    w_hi = wr.astype(BF16)
    w_lo = (wr - w_hi.astype(F32)).astype(BF16)
    hn_hi = hn.astype(BF16)
    hn_lo = (hn - hn_hi.astype(F32)).astype(BF16)
    both = _mm_nt(jnp.concatenate([w_hi, w_lo], axis=0), hn_hi)
    logits = both[:ROUTER_ROWS] + both[ROUTER_ROWS:] + _mm_nt(w_hi, hn_lo)
    coarse = logits[N_EXPERTS:N_EXPERTS + MOE_GROUPS, :]
    cm = jnp.max(coarse, axis=0, keepdims=True)
    ce = jnp.exp(coarse - cm)
    pc = ce / jnp.sum(ce, axis=0, keepdims=True)
    p_sel = jnp.max(pc, axis=0, keepdims=True)
    gi = lax.broadcasted_iota(I32, pc.shape, 0)
    g_sel = jnp.min(jnp.where(pc == p_sel, gi, MOE_GROUPS), axis=0, keepdims=True)
    fine = jnp.zeros((EXPERTS_PER_GROUP, logits.shape[1]), F32)
    for g in range(MOE_GROUPS):
        fine = fine + jnp.where(g_sel == g, logits[g * EXPERTS_PER_GROUP:(g + 1) * EXPERTS_PER_GROUP, :], 0.0)
    fm = jnp.max(fine, axis=0, keepdims=True)
    fe = jnp.exp(fine - fm)
    pf = fe / jnp.sum(fe, axis=0, keepdims=True)
    ei = lax.broadcasted_iota(I32, pf.shape, 0)
    v1 = jnp.max(pf, axis=0, keepdims=True)
    i1 = jnp.min(jnp.where(pf == v1, ei, EXPERTS_PER_GROUP), axis=0, keepdims=True)
    rest = jnp.where(ei == i1, -1.0, pf)
    v2 = jnp.max(rest, axis=0, keepdims=True)
    i2 = jnp.min(jnp.where(rest == v2, ei, EXPERTS_PER_GROUP), axis=0, keepdims=True)
    tot = v1 + v2
    ridx_ref[0:1, :] = g_sel * EXPERTS_PER_GROUP + i1
    ridx_ref[1:2, :] = g_sel * EXPERTS_PER_GROUP + i2
    rw_ref[0:1, :] = v1 / tot * p_sel
    rw_ref[1:2, :] = v2 / tot * p_sel


def _postmix(prompt, sample, weights, nb):
    n_p = prompt[0].shape[0]
    n_s = sample[0].shape[0]
    t = n_p // nb
    tt = min(ROW_TILE, t, n_s)
    nt = t // tt
    nblk_p = n_p // tt
    nblk = nblk_p + n_s // tt
    n_total = n_p + n_s
    prow = lambda i: (jnp.minimum(i, nblk_p - 1), 0)
    pys = lambda i: (jnp.minimum(i, nblk_p - 1) % nt, jnp.minimum(i, nblk_p - 1) // nt)
    srow = lambda i: (jnp.maximum(i - nblk_p, 0), 0)
    const = lambda i: (0, 0)

    def stream_specs(row, ysmap):
        return [pl.BlockSpec((tt, D_MODEL), row),
                pl.BlockSpec((tt, DN_WIDTH), row),
                pl.BlockSpec((tt, DN_WIDTH), row),
                pl.BlockSpec((tt, S5_WIDTH), ysmap),
                pl.BlockSpec((tt, D_MODEL), row),
                pl.BlockSpec((tt, D_MODEL), row)]

    weight_specs = [pl.BlockSpec((1, DN_WIDTH), const),
                    pl.BlockSpec((DN_WIDTH, DN_WIDTH), const),
                    pl.BlockSpec((DN_WIDTH, D_MODEL), const),
                    pl.BlockSpec((S5_WIDTH, S5_WIDTH), const),
                    pl.BlockSpec((S5_WIDTH, D_MODEL), const),
                    pl.BlockSpec((D_MODEL, D_MODEL), const),
                    pl.BlockSpec((1, D_MODEL), const),
                    pl.BlockSpec((ROUTER_ROWS, D_MODEL), const)]
    xp, op, zp, ysp, gap, gbp = prompt
    return pl.pallas_call(
        functools.partial(_postmix_kernel, nblk_p=nblk_p),
        grid=(nblk,),
        in_specs=stream_specs(prow, pys) + stream_specs(srow, srow) + weight_specs,
        out_specs=[pl.BlockSpec((tt, D_MODEL), lambda i: (i, 0)),
                   pl.BlockSpec((tt * ROW_SLAB, LANES), lambda i: (i, 0)),
                   pl.BlockSpec((TOP_K, tt), lambda i: (0, i)),
                   pl.BlockSpec((TOP_K, tt), lambda i: (0, i))],
        out_shape=[jax.ShapeDtypeStruct((n_total, D_MODEL), F32),
                   jax.ShapeDtypeStruct((n_total * ROW_SLAB, LANES), F32),
                   jax.ShapeDtypeStruct((TOP_K, n_total), I32),
                   jax.ShapeDtypeStruct((TOP_K, n_total), F32)],
        compiler_params=_cparams(("arbitrary",)),
        name="postmix",
    )(xp, op, zp, ysp.reshape(t, nb * S5_WIDTH), gap, gbp, *sample, *weights)


def _wait_slabs(buf, sem):
    pltpu.make_async_copy(buf, buf, sem).wait()


def _moe_kernel(texp_ref, tph_ref, tsrc_ref, tnv_ref, tfirst_ref, tslot_ref, tnext_ref, otok_ref,
                hn_hbm, wu_hbm, wd_hbm, y_ref, hnv, xbuf, wu_buf, wd_buf, wub, wdb, sem, wsem):
    i = pl.program_id(0)
    tm = MOE_TILE
    rs = ROW_SLAB
    nv = tnv_ref[i]
    ph = tph_ref[i]
    range_rows = hnv.shape[0]

    def weight_copies(e, sl):
        return (pltpu.make_async_copy(wu_hbm.at[e], wu_buf.at[sl], wsem.at[sl]),
                pltpu.make_async_copy(wd_hbm.at[e], wd_buf.at[sl], wsem.at[sl]))

    @pl.when(i == 0)
    def _():
        for c in weight_copies(texp_ref[0], 0):
            c.start()

    @pl.when(jnp.logical_and(nv > 0, jnp.logical_or(i == 0, ph != tph_ref[jnp.maximum(i - 1, 0)])))
    def _():
        start = pl.multiple_of(ph * range_rows, rs)
        whole = pltpu.make_async_copy(hn_hbm.at[pl.ds(start, range_rows), :], hnv, sem)
        whole.start()
        whole.wait()

    for sl in range(2):
        @pl.when(jnp.logical_and(jnp.logical_and(nv > 0, tfirst_ref[i] == 1), tslot_ref[i] == sl))
        def _():
            for c in weight_copies(texp_ref[i], sl):
                c.wait()

            @pl.when(tnext_ref[i] >= 0)
            def _():
                for c in weight_copies(tnext_ref[i], 1 - sl):
                    c.start()

            wub[...] = wu_buf[sl].astype(BF16)
            wdb[...] = wd_buf[sl].astype(BF16)

    @pl.when(nv == 0)
    def _():
        y_ref[...] = jnp.zeros_like(y_ref)

    @pl.when(nv > 0)
    def _():
        src0 = tsrc_ref[i]
        for r in range(tm):
            tok8 = pl.multiple_of(otok_ref[src0 + r], rs)
            xbuf[pl.ds(r * rs, rs), :] = hnv[pl.ds(tok8, rs), :]
        x = _slab_load(xbuf, tm).astype(BF16)
        hu = jnp.dot(x, wub[...], preferred_element_type=F32)
        gate = hu[:, :EXPERT_FF]
        up = hu[:, EXPERT_FF:]
        act = gate * jax.nn.sigmoid(gate) * up
        _slab_store(y_ref, jnp.dot(act.astype(BF16), wdb[...], preferred_element_type=F32))


def _moe(hn, w_up, w_down, plan):
    ntiles = plan[0].shape[0]
    grid_spec = pltpu.PrefetchScalarGridSpec(
        num_scalar_prefetch=len(plan),
        grid=(ntiles,),
        in_specs=[pl.BlockSpec(memory_space=pl.ANY),
                  pl.BlockSpec(memory_space=pl.ANY),
                  pl.BlockSpec(memory_space=pl.ANY)],
        out_specs=pl.BlockSpec((MOE_TILE * ROW_SLAB, LANES), lambda i, *_: (i, 0)),
        scratch_shapes=[pltpu.VMEM((hn.shape[0] // MOE_PHASES, LANES), F32),
                        pltpu.VMEM((MOE_TILE * ROW_SLAB, LANES), F32),
                        pltpu.VMEM((2, D_MODEL, 2 * EXPERT_FF), F32),
                        pltpu.VMEM((2, EXPERT_FF, D_MODEL), F32),
                        pltpu.VMEM((D_MODEL, 2 * EXPERT_FF), BF16),
                        pltpu.VMEM((EXPERT_FF, D_MODEL), BF16),
                        pltpu.SemaphoreType.DMA,
                        pltpu.SemaphoreType.DMA((2,))])
    return pl.pallas_call(
        _moe_kernel,
        grid_spec=grid_spec,
        out_shape=jax.ShapeDtypeStruct((ntiles * MOE_TILE * ROW_SLAB, LANES), F32),
        compiler_params=_cparams(("arbitrary",)),
        name="moe",
    )(*plan, hn, w_up, w_down)


def _combine_kernel(pos_ref, x1_ref, ys_hbm, w_ref, nw_ref, outp_ref, outs_ref, ybuf0, ybuf1, sem, *, nblk_p):
    i = pl.program_id(0)
    nsteps = pl.num_programs(0)
    tt = x1_ref.shape[0]
    rs = ROW_SLAB
    slot = lax.rem(i, 2)
    ybuf = (ybuf0, ybuf1)

    def start_gather(step, sl):
        base = step * (tt * TOP_K)
        for r in range(tt * TOP_K):
            j, s = divmod(r, TOP_K)
            p8 = pl.multiple_of(pos_ref[base + r], rs)
            pltpu.make_async_copy(ys_hbm.at[pl.ds(p8, rs), :], ybuf[sl].at[pl.ds((s * tt + j) * rs, rs), :],
                                  sem.at[sl]).start(priority=r % DMA_QUEUES)

    @pl.when(i == 0)
    def _():
        start_gather(0, 0)

    for sl in range(2):
        @pl.when(slot == sl)
        def _():
            _wait_slabs(ybuf[sl], sem.at[sl])
            start_gather(jnp.minimum(i + 1, nsteps - 1), 1 - sl)
            w = w_ref[...]
            y0 = _slab_load(ybuf[sl], tt, 0)
            y1 = _slab_load(ybuf[sl], tt, tt * rs)
            x = x1_ref[...] + w[:, 0:1] * y0 + w[:, 1:2] * y1
            res = x * lax.rsqrt(jnp.mean(x * x, axis=-1, keepdims=True) + RMS_EPS) * nw_ref[...]

            @pl.when(i < nblk_p)
            def _():
                outp_ref[...] = res

            @pl.when(i >= nblk_p)
            def _():
                outs_ref[...] = res

        @pl.when(jnp.logical_and(slot == sl, i == nsteps - 1))
        def _():
            _wait_slabs(ybuf[1 - sl], sem.at[1 - sl])


def _combine(x1, ysorted, pos8, wtok, nw, n_p):
    n = x1.shape[0]
    tt = math.gcd(math.gcd(n_p, n - n_p), COMBINE_TILE)
    nblk_p = n_p // tt
    grid_spec = pltpu.PrefetchScalarGridSpec(
        num_scalar_prefetch=1,
        grid=(n // tt,),
        in_specs=[pl.BlockSpec((tt, D_MODEL), lambda i, ps: (i, 0)),
                  pl.BlockSpec(memory_space=pl.ANY),
                  pl.BlockSpec((tt, TOP_K), lambda i, ps: (i, 0)),
                  pl.BlockSpec((1, D_MODEL), lambda i, ps: (0, 0))],
        out_specs=[pl.BlockSpec((tt, D_MODEL), lambda i, ps: (jnp.minimum(i, nblk_p - 1), 0)),
                   pl.BlockSpec((tt, D_MODEL), lambda i, ps: (jnp.maximum(i - nblk_p, 0), 0))],
        scratch_shapes=[pltpu.VMEM((tt * TOP_K * ROW_SLAB, LANES), F32),
                        pltpu.VMEM((tt * TOP_K * ROW_SLAB, LANES), F32),
                        pltpu.SemaphoreType.DMA((2,))])
    return pl.pallas_call(
        functools.partial(_combine_kernel, nblk_p=nblk_p),
        grid_spec=grid_spec,
        out_shape=[jax.ShapeDtypeStruct((n_p, D_MODEL), F32),
                   jax.ShapeDtypeStruct((n - n_p, D_MODEL), F32)],
        compiler_params=_cparams(("arbitrary",)),
        name="combine",
    )(pos8, x1, ysorted, wtok, nw)


def _route_plan(ridx, n_tok):
    tm = MOE_TILE
    n_assign = n_tok * TOP_K
    nbk = MOE_PHASES * N_EXPERTS
    ntiles = n_assign // tm + nbk
    range_tok = n_tok // MOE_PHASES
    e_flat = ridx.T.reshape(n_assign)
    tok = jnp.arange(n_assign, dtype=I32) // TOP_K
    b_flat = (tok // range_tok) * N_EXPERTS + e_flat
    order = jnp.argsort(b_flat, stable=True).astype(I32)
    boh = (b_flat[:, None] == jnp.arange(nbk, dtype=I32)[None, :]).astype(I32)
    counts = jnp.sum(boh, axis=0)
    cstart = jnp.cumsum(counts) - counts
    tiles_b = (counts + tm - 1) // tm
    tend = jnp.cumsum(tiles_b)
    tstart = tend - tiles_b
    tile_id = jnp.arange(ntiles, dtype=I32)
    tbk = jnp.minimum(jnp.sum((tile_id[:, None] >= tend[None, :]).astype(I32), axis=1), nbk - 1)
    onehot = (tbk[:, None] == jnp.arange(nbk, dtype=I32)[None, :]).astype(I32)
    pick = lambda v: jnp.sum(onehot * v[None, :], axis=1)
    done = (tile_id - pick(tstart)) * tm
    tnv = jnp.where(tile_id < tend[-1], jnp.clip(pick(counts) - done, 0, tm), 0)
    tsrc = jnp.where(tnv > 0, pick(cstart) + done, 0)
    texp = tbk % N_EXPERTS
    tph = tbk // N_EXPERTS
    nonempty = counts > 0
    bslot = (jnp.cumsum(nonempty.astype(I32)) - 1) % 2
    bidx = jnp.where(nonempty, jnp.arange(nbk, dtype=I32), nbk)
    nxt = jnp.concatenate([lax.cummin(bidx[::-1])[::-1][1:], jnp.full((1,), nbk, I32)])
    bnext = jnp.where(nxt < nbk, nxt % N_EXPERTS, -1)
    tfirst = jnp.logical_and(tnv > 0, done == 0).astype(I32)
    tslot = pick(bslot)
    tnext = pick(bnext)
    otok8 = jnp.concatenate([((order // TOP_K) % range_tok) * ROW_SLAB, jnp.zeros((tm,), I32)])
    rank = jnp.argsort(order).astype(I32)
    pos8 = (rank + jnp.sum(boh * (tstart * tm - cstart)[None, :], axis=1)) * ROW_SLAB
    plan = tuple(a.astype(I32) for a in (texp, tph, tsrc, tnv, tfirst, tslot, tnext, otok8))
    return plan, pos8.astype(I32)


def _block_diag(m):
    g, a, b = m.shape
    eye = jnp.eye(g, dtype=m.dtype)
    return (eye[:, None, :, None] * m[:, :, None, :]).reshape(g * a, g * b)


def kernel(x_prompt, x_sample, state_conv, state_delta, state_ssm_re, state_ssm_im, norm_mix_w, w_in, conv_w, a_log, dt_bias, head_norm_w, w_a_up, s5_lambda_re, s5_lambda_im, s5_log_step, s5_b_re, s5_b_im, s5_c_re, s5_c_im, s5_d, w_glu, w_b_up, w_o, norm_ffn_w, w_router_coarse, w_router_fine, w_expert_up, w_expert_down, norm_final_w):
    bp, tp, _ = x_prompt.shape
    bs, ts, _ = x_sample.shape
    n_p = bp * tp
    n_s = bs * ts
    n_tok = n_p + n_s
    l = 0

    w = w_in[l]
    cuts = np.cumsum([0, QKV_DIM, DN_WIDTH, DN_HEADS, DN_HEADS, S5_WIDTH, D_MODEL, D_MODEL])
    w_qkv, w_z, w_a, w_b, w_u, w_ga, w_gb = [w[:, cuts[i]:cuts[i + 1]] for i in range(7)]
    w_ab = jnp.concatenate([w_a, w_b, jnp.zeros((D_MODEL, LANES - 2 * DN_HEADS), F32)], axis=1)
    wcat = jnp.concatenate([w_qkv, w_z, w_u, w_ga, w_gb, w_ab], axis=1).astype(BF16)
    nw_mix = norm_mix_w[l].reshape(1, D_MODEL)
    pad8 = lambda v: jnp.concatenate([v, jnp.zeros((LANES - DN_HEADS,), F32)]).reshape(1, LANES)
    gate_p = jnp.concatenate([pad8(a_log[l]), pad8(dt_bias[l])], axis=0)
    seg = _block_diag(jnp.ones((DN_HEADS, DN_HEAD_DIM, DN_HEAD_DIM), BF16))
    bre = _block_diag(jnp.swapaxes(s5_b_re[l], 1, 2))
    bim = _block_diag(jnp.swapaxes(s5_b_im[l], 1, 2))
    lam = jnp.concatenate([s5_lambda_re[l].reshape(1, S5_FLAT), s5_lambda_im[l].reshape(1, S5_FLAT),
                           jnp.repeat(s5_log_step[l], S5_STATE).reshape(1, S5_FLAT),
                           jnp.zeros((SUBLANES - 3, S5_FLAT), F32)], axis=0)
    cmat = jnp.concatenate([_block_diag(jnp.swapaxes(s5_c_re[l], 1, 2)),
                            -_block_diag(jnp.swapaxes(s5_c_im[l], 1, 2))], axis=0).astype(BF16)
    dvec = s5_d[l].reshape(1, S5_WIDTH)
    hw = jnp.tile(head_norm_w[l], DN_HEADS).reshape(1, DN_WIDTH)
    wr = jnp.concatenate([w_router_fine[l].T, w_router_coarse[l].T,
                          jnp.zeros((ROUTER_ROWS - N_EXPERTS - MOE_GROUPS, D_MODEL), F32)], axis=0)
    pm_weights = (hw, seg, w_a_up[l].astype(BF16), w_glu[l].astype(BF16), w_b_up[l].astype(BF16),
                  w_o[l].astype(BF16), norm_ffn_w[l].reshape(1, D_MODEL), wr)

    xp2 = x_prompt.reshape(n_p, D_MODEL)
    q_p, k_p, v_p, gates_p, conv_p, z_p, u_p, ga_p, gb_p = _inprep(
        xp2, nw_mix, wcat, jnp.zeros((bp, SUBLANES, QKV_DIM), F32), conv_w[l], gate_p, seg, bp, 1)
    o_p, delta_p = _delta_prompt(q_p, k_p, v_p, gates_p, bp)
    ys_p, h_p = _s5(u_p, bre, bim, lam, cmat, dvec, jnp.zeros((bp, 2 * S5_FLAT), F32), bp)

    xs2 = jnp.swapaxes(x_sample, 0, 1).reshape(n_s, D_MODEL)
    cinit_s = jnp.swapaxes(state_conv[l], 0, 1).reshape(1, (CONV_W - 1) * bs, QKV_DIM)
    q_s, k_s, v_s, gate_s, conv_s, z_s, u_s, ga_s, gb_s = _inprep(
        xs2, nw_mix, wcat, cinit_s, conv_w[l], gate_p, seg, 1, bs)
    s0 = state_delta[l].reshape(bs, DN_HEADS * DN_HEAD_DIM * DN_HEAD_DIM)
    o_s, delta_s = _delta_sample(q_s, k_s, v_s, gate_s, s0, bs, ts)
    h0_s = jnp.concatenate([state_ssm_re[l].reshape(bs, S5_FLAT), state_ssm_im[l].reshape(bs, S5_FLAT)], axis=1)
    ys_s, h_s = _s5(u_s, bre, bim, lam, cmat, dvec, h0_s, bs)
    x1, hn, ridx, rw = _postmix((xp2, o_p, z_p, ys_p, ga_p, gb_p), (xs2, o_s, z_s, ys_s, ga_s, gb_s),
                                pm_weights, bp)

    plan, pos8 = _route_plan(ridx, n_tok)
    ysorted = _moe(hn, w_expert_up[l], w_expert_down[l], plan)
    y_p, y_s = _combine(x1, ysorted, pos8, rw.T, norm_final_w.reshape(1, D_MODEL), n_p)

    y_prompt = y_p.reshape(bp, tp, D_MODEL)
    y_sample = jnp.swapaxes(y_s.reshape(ts, bs, D_MODEL), 0, 1)
    conv_sample = jnp.swapaxes(conv_s.reshape(CONV_W - 1, bs, QKV_DIM), 0, 1)
    return (y_prompt, y_sample,
            conv_p[None], delta_p[None],
            h_p[:, :S5_FLAT].reshape(1, bp, S5_GROUPS, S5_STATE), h_p[:, S5_FLAT:].reshape(1, bp, S5_GROUPS, S5_STATE),
            conv_sample[None], delta_s.reshape(1, bs, DN_HEADS, DN_HEAD_DIM, DN_HEAD_DIM),
            h_s[:, :S5_FLAT].reshape(1, bs, S5_GROUPS, S5_STATE), h_s[:, S5_FLAT:].reshape(1, bs, S5_GROUPS, S5_STATE))
```

```python
import functools
import math

import jax
import jax.numpy as jnp
import numpy as np
from jax import lax
from jax.experimental import pallas as pl
from jax.experimental.pallas import tpu as pltpu

F32 = jnp.float32
BF16 = jnp.bfloat16
I32 = jnp.int32

D_MODEL = 1024
DN_HEADS = 8
DN_HEAD_DIM = 64
DN_WIDTH = DN_HEADS * DN_HEAD_DIM
QKV_DIM = 3 * DN_WIDTH
CONV_W = 4
DN_CHUNK = 64
S5_GROUP_CH = 16
S5_WIDTH = D_MODEL // 2
S5_GROUPS = S5_WIDTH // S5_GROUP_CH
S5_STATE = 64
S5_FLAT = S5_GROUPS * S5_STATE
MOE_GROUPS = 4
EXPERTS_PER_GROUP = 8
N_EXPERTS = MOE_GROUPS * EXPERTS_PER_GROUP
TOP_K = 2
EXPERT_FF = 256
RMS_EPS = 1e-6
L2_EPS = 1e-6

LANES = 128
SUBLANES = 8
VMEM_LIMIT = 56 * 1024 * 1024

C_QKV, C_Z, C_U, C_GA, C_GB, C_AB = 0, 1536, 2048, 2560, 3584, 4608
IN_PACKED = C_AB + LANES

ROW_TILE = 512
INPREP_PARTS = 2
MOE_TILE = 256
MOE_PHASES = 2
COMBINE_TILE = 256
DMA_QUEUES = 2
DELTA_SUBCHUNKS = 4
S5_SUPER = 2
S5_SCAN_SPLIT = 2
ROUTER_ROWS = 40


def _mm(a, b):
    return jnp.dot(a.astype(BF16), b.astype(BF16), preferred_element_type=F32)


def _mm_nt(a, b):
    return lax.dot_general(a.astype(BF16), b.astype(BF16), (((1,), (1,)), ((), ())),
                           preferred_element_type=F32)


def _split3_dot(a, b01):
    a1 = a.astype(BF16)
    r1 = a - a1.astype(F32)
    a2 = r1.astype(BF16)
    a3 = (r1 - a2.astype(F32)).astype(BF16)
    out = jnp.dot(a3, b01, preferred_element_type=F32)
    out = out + jnp.dot(a2, b01, preferred_element_type=F32)
    return out + jnp.dot(a1, b01, preferred_element_type=F32)


def _cparams(sem):
    return pltpu.CompilerParams(dimension_semantics=sem, vmem_limit_bytes=VMEM_LIMIT)


ROW_SLAB = D_MODEL // LANES


def _slab_load(ref, rows, first=0, pitch=ROW_SLAB):
    return jnp.concatenate([ref[pl.ds(first + j, rows, stride=pitch), :] for j in range(ROW_SLAB)], axis=1)


def _slab_store(ref, x):
    for j in range(ROW_SLAB):
        ref[pl.ds(j, x.shape[0], stride=ROW_SLAB), :] = x[:, j * LANES:(j + 1) * LANES]


def _softplus(x):
    return jnp.maximum(x, 0.0) + jnp.log1p(jnp.exp(-jnp.abs(x)))


def _inprep_kernel(x_ref, nw_ref, w_ref, cinit_ref, cw_ref, gp_ref, seg_ref,
                   q_ref, k_ref, v_ref, gate_ref, cnew_ref, z_ref, u_ref, ga_ref, gb_ref, xp_ref,
                   *, shift, rc, rows):
    @pl.when(pl.program_id(1) == 0)
    def _():
        xp_ref[0:rc, :] = cinit_ref[0]

    seg = seg_ref[...]
    pr = rows // INPREP_PARTS
    for part in range(INPREP_PARTS):
        rs = slice(part * pr, (part + 1) * pr)
        x = x_ref[rs, :]
        h = x * lax.rsqrt(jnp.mean(x * x, axis=-1, keepdims=True) + RMS_EPS) * nw_ref[...]
        hb = h.astype(BF16)

        def proj(lo, hi, hb=hb):
            return jnp.dot(hb, w_ref[:, lo:hi], preferred_element_type=F32)

        xp_ref[rc + part * pr:rc + (part + 1) * pr, :] = proj(C_QKV, C_Z)
        ab = proj(C_AB, IN_PACKED)
        z_ref[rs, :] = proj(C_Z, C_U).astype(z_ref.dtype)
        u_ref[rs, :] = proj(C_U, C_GA).astype(u_ref.dtype)
        ga_ref[rs, :] = proj(C_GA, C_GB).astype(ga_ref.dtype)
        gb_ref[rs, :] = proj(C_GB, C_AB).astype(gb_ref.dtype)
        acc = None
        for i in range(CONV_W):
            lo = rc + part * pr + (i - (CONV_W - 1)) * shift
            term = xp_ref[lo:lo + pr, :] * cw_ref[i:i + 1, :]
            acc = term if acc is None else acc + term
        y = acc * jax.nn.sigmoid(acc)
        q = y[:, 0:DN_WIDTH]
        k = y[:, DN_WIDTH:2 * DN_WIDTH]
        q_ref[rs, :] = q * lax.rsqrt(jnp.dot((q * q).astype(BF16), seg, preferred_element_type=F32) + L2_EPS)
        k_ref[rs, :] = k * lax.rsqrt(jnp.dot((k * k).astype(BF16), seg, preferred_element_type=F32) + L2_EPS)
        v_ref[rs, :] = y[:, 2 * DN_WIDTH:]
        g = -jnp.exp(gp_ref[0:1, :]) * _softplus(ab + gp_ref[1:2, :])
        beta = jax.nn.sigmoid(ab)
        lane = lax.broadcasted_iota(I32, ab.shape, 1)
        gate_ref[rs, :] = jnp.where(lane < DN_HEADS, g, beta)

    keep = (CONV_W - 1) * shift
    cnew_ref[0] = xp_ref[rc + rows - keep:rc + rows, :]
    xp_ref[0:rc, :] = xp_ref[rows:rows + rc, :]


def _inprep(x2d, nw, wcat, cinit, conv_w, gate_p, seg, nb, shift):
    n = x2d.shape[0]
    r = n // nb
    rows = min(ROW_TILE, r)
    nt = r // rows
    rc = cinit.shape[1]
    keep = (CONV_W - 1) * shift
    row = lambda b, i: (b * nt + i, 0)
    const = lambda b, i: (0, 0)
    kern = functools.partial(_inprep_kernel, shift=shift, rc=rc, rows=rows)
    outs = pl.pallas_call(
        kern,
        grid=(nb, nt),
        in_specs=[pl.BlockSpec((rows, D_MODEL), row),
                  pl.BlockSpec((1, D_MODEL), const),
                  pl.BlockSpec((D_MODEL, IN_PACKED), const),
                  pl.BlockSpec((1, rc, QKV_DIM), lambda b, i: (b, 0, 0)),
                  pl.BlockSpec((CONV_W, QKV_DIM), const),
                  pl.BlockSpec((2, LANES), const),
                  pl.BlockSpec((DN_WIDTH, DN_WIDTH), const)],
        out_specs=[pl.BlockSpec((rows, DN_WIDTH), row),
                   pl.BlockSpec((rows, DN_WIDTH), row),
                   pl.BlockSpec((rows, DN_WIDTH), row),
                   pl.BlockSpec((rows, LANES), row),
                   pl.BlockSpec((1, keep, QKV_DIM), lambda b, i: (b, 0, 0)),
                   pl.BlockSpec((rows, DN_WIDTH), row),
                   pl.BlockSpec((rows, S5_WIDTH), lambda b, i: (i, b)),
                   pl.BlockSpec((rows, D_MODEL), row),
                   pl.BlockSpec((rows, D_MODEL), row)],
        out_shape=[jax.ShapeDtypeStruct((n, DN_WIDTH), F32),
                   jax.ShapeDtypeStruct((n, DN_WIDTH), F32),
                   jax.ShapeDtypeStruct((n, DN_WIDTH), F32),
                   jax.ShapeDtypeStruct((n, LANES), F32),
                   jax.ShapeDtypeStruct((nb, keep, QKV_DIM), F32),
                   jax.ShapeDtypeStruct((n, DN_WIDTH), BF16),
                   jax.ShapeDtypeStruct((r, nb * S5_WIDTH), BF16),
                   jax.ShapeDtypeStruct((n, D_MODEL), BF16),
                   jax.ShapeDtypeStruct((n, D_MODEL), BF16)],
        scratch_shapes=[pltpu.VMEM((rc + rows, QKV_DIM), F32)],
        compiler_params=_cparams(("arbitrary", "arbitrary")),
        name="inprep",
    )(x2d, nw, wcat, cinit, conv_w, gate_p, seg)
    q, k, v, gate, cnew, z, u, ga, gb = outs
    return q, k, v, gate, cnew, z, u.reshape(r * nb, S5_WIDTH), ga, gb


def _delta_chunk_kernel(q_ref, k_ref, v_ref, gate_ref, tril_ref, o_ref, sfin_ref, s_ref, *, nsub):
    c = DN_CHUNK
    dk = DN_HEAD_DIM

    @pl.when(pl.program_id(1) == 0)
    def _():
        s_ref[...] = jnp.zeros_like(s_ref)

    rowi = lax.broadcasted_iota(I32, (c, c), 0)
    coli = lax.broadcasted_iota(I32, (c, c), 1)
    causal = rowi >= coli
    strict = rowi > coli
    tril = tril_ref[...]
    pairs = [(j, h) for j in range(nsub) for h in range(DN_HEADS)]
    rows = [slice(j * c, (j + 1) * c) for j in range(nsub)]
    gate = [gate_ref[rows[j], :] for j in range(nsub)]
    gc_all = [_split3_dot_left(tril, gate[j]) for j in range(nsub)]
    gc_t = [gc_all[j].T for j in range(nsub)]

    def head(ref, j, h):
        return ref[rows[j], h * dk:(h + 1) * dk]

    qh = {p: head(q_ref, *p) * (dk ** -0.5) for p in pairs}
    kh = {p: head(k_ref, *p) for p in pairs}
    gcol = {(j, h): gc_all[j][:, h:h + 1] for j, h in pairs}
    beta = {(j, h): gate[j][:, DN_HEADS + h:DN_HEADS + h + 1] for j, h in pairs}
    grow2 = {(j, h): jnp.concatenate([gc_t[j][h:h + 1, :], gc_t[j][h:h + 1, :]], axis=1) for j, h in pairs}
    rowi2 = lax.broadcasted_iota(I32, (c, 2 * c), 0)
    coli2 = lax.broadcasted_iota(I32, (c, 2 * c), 1) & (c - 1)
    causal2 = rowi2 >= coli2
    strict2 = rowi2 > coli2
    decay = {p: jnp.where(causal2, jnp.exp(jnp.where(causal2, gcol[p] - grow2[p], 0.0)), 0.0) for p in pairs}
    kb = {p: kh[p] * beta[p] for p in pairs}
    egc = {p: jnp.exp(gcol[p]) for p in pairs}
    gram = {p: _mm_nt(jnp.concatenate([kb[p], qh[p]], axis=0), jnp.concatenate([kh[p], kh[p]], axis=0))
            for p in pairs}
    mat = {p: jnp.where(strict2, gram[p][:c] * decay[p], 0.0).astype(BF16) for p in pairs}
    qk = {p: jnp.where(causal, gram[p][c:, :c] * decay[p][:, :c], 0.0) for p in pairs}
    sol = {p: jnp.concatenate([head(v_ref, *p) * beta[p], kb[p] * egc[p]], axis=1) for p in pairs}
    levels = int(math.log2(c))
    zeros2 = jnp.zeros((c, 2 * c), BF16)
    for lvl in range(levels):
        hi = {p: sol[p].astype(BF16) for p in pairs}
        lo = {p: (sol[p] - hi[p].astype(F32)).astype(BF16) for p in pairs}
        if lvl < levels - 1:
            y = {p: jnp.dot(mat[p], jnp.concatenate([jnp.concatenate([hi[p], mat[p]], axis=1),
                                                     jnp.concatenate([lo[p], zeros2], axis=1)], axis=0),
                            preferred_element_type=F32) for p in pairs}
            mat = {p: y[p][:, 2 * dk:].astype(BF16) for p in pairs}
            upd = {p: y[p][:, :2 * dk] for p in pairs}
        else:
            upd = {p: jnp.dot(mat[p], jnp.concatenate([hi[p], lo[p]], axis=0), preferred_element_type=F32)
                   for p in pairs}
        sol = {p: (sol[p] - upd[p]) if lvl == 0 else (sol[p] + upd[p]) for p in pairs}
    g_last = {(j, h): gc_all[j][c - 1:c, h:h + 1] for j, h in pairs}
    wq = {p: jnp.concatenate([sol[p][:, dk:], qh[p] * egc[p]], axis=0) for p in pairs}
    k_dec_t = {p: (kh[p] * jnp.exp(g_last[p] - gcol[p])).T for p in pairs}
    d_last = {p: jnp.exp(g_last[p]) for p in pairs}

    s = [s_ref[h] for h in range(DN_HEADS)]
    for j in range(nsub):
        ws = [_mm(wq[j, h], s[h]) for h in range(DN_HEADS)]
        v_new = [sol[j, h][:, :dk] - ws[h][:c] for h in range(DN_HEADS)]
        o_ref[rows[j], :] = jnp.concatenate(
            [ws[h][c:] + _mm(qk[j, h], v_new[h]) for h in range(DN_HEADS)], axis=1)
        s = [s[h] * d_last[j, h] + _mm(k_dec_t[j, h], v_new[h]) for h in range(DN_HEADS)]
    for h in range(DN_HEADS):
        s_ref[h] = s[h]
    sfin_ref[0] = s_ref[...]


def _split3_dot_left(b01, a):
    a1 = a.astype(BF16)
    r1 = a - a1.astype(F32)
    a2 = r1.astype(BF16)
    a3 = (r1 - a2.astype(F32)).astype(BF16)
    out = jnp.dot(b01, a3, preferred_element_type=F32)
    out = out + jnp.dot(b01, a2, preferred_element_type=F32)
    return out + jnp.dot(b01, a1, preferred_element_type=F32)


def _delta_prompt(q, k, v, gate, nb):
    n = q.shape[0]
    t = n // nb
    c = DN_CHUNK
    nsub = DELTA_SUBCHUNKS
    rows = nsub * c
    nc = t // rows
    row = lambda b, i: (b * nc + i, 0)
    tril = jnp.tril(jnp.ones((c, c), F32)).astype(BF16)
    return pl.pallas_call(
        functools.partial(_delta_chunk_kernel, nsub=nsub),
        grid=(nb, nc),
        in_specs=[pl.BlockSpec((rows, DN_WIDTH), row),
                  pl.BlockSpec((rows, DN_WIDTH), row),
                  pl.BlockSpec((rows, DN_WIDTH), row),
                  pl.BlockSpec((rows, LANES), row),
                  pl.BlockSpec((c, c), lambda b, i: (0, 0))],
        out_specs=[pl.BlockSpec((rows, DN_WIDTH), row),
                   pl.BlockSpec((1, DN_HEADS, DN_HEAD_DIM, DN_HEAD_DIM), lambda b, i: (b, 0, 0, 0))],
        out_shape=[jax.ShapeDtypeStruct((n, DN_WIDTH), F32),
                   jax.ShapeDtypeStruct((nb, DN_HEADS, DN_HEAD_DIM, DN_HEAD_DIM), F32)],
        scratch_shapes=[pltpu.VMEM((DN_HEADS, DN_HEAD_DIM, DN_HEAD_DIM), F32)],
        compiler_params=_cparams(("arbitrary", "arbitrary")),
        name="delta_prompt",
    )(q, k, v, gate, tril)


def _delta_step_kernel(q_ref, k_ref, v_ref, gate_ref, s0_ref, ex_ref, o_ref, s_ref, kx_ref, qx_ref, *, nt, nb):
    dk = DN_HEAD_DIM
    flat = dk * dk
    nv = flat // LANES
    p = pl.program_id(0)
    lane = lax.broadcasted_iota(I32, (SUBLANES, LANES), 1)
    low = lane < dk
    ex = ex_ref[...]

    for t in range(nt):
        rs = slice(t * nb, (t + 1) * nb)
        kx_ref[...] = jnp.dot(k_ref[rs, :].astype(BF16), ex, preferred_element_type=F32)
        qx_ref[...] = jnp.dot((q_ref[rs, :] * (dk ** -0.5)).astype(BF16), ex, preferred_element_type=F32)
        src_ref = s0_ref if t == 0 else s_ref

        def tile_body(bt, carry, t=t, src_ref=src_ref):
            b0 = pl.multiple_of(bt * SUBLANES, SUBLANES)
            r0 = pl.multiple_of(t * nb + b0, SUBLANES)
            gate = gate_ref[pl.ds(r0, SUBLANES), :]
            vv = v_ref[pl.ds(r0, SUBLANES), :]
            vsw = pltpu.roll(vv, dk, axis=1)
            o_pair = None
            for j in range(2):
                c0 = j * flat
                g = jnp.sum(jnp.where(lane == 2 * p + j, gate, 0.0), axis=1, keepdims=True)
                beta = jnp.sum(jnp.where(lane == 2 * p + j + DN_HEADS, gate, 0.0), axis=1, keepdims=True)
                a = jnp.exp(g)
                vdup = jnp.where(low, vv, vsw) if j == 0 else jnp.where(low, vsw, vv)
                s = [src_ref[pl.ds(b0, SUBLANES), c0 + i * LANES:c0 + (i + 1) * LANES] for i in range(nv)]
                kx = [kx_ref[pl.ds(b0, SUBLANES), c0 + i * LANES:c0 + (i + 1) * LANES] for i in range(nv)]
                ks = kx[0] * s[0]
                for i in range(1, nv):
                    ks = ks + kx[i] * s[i]
                ks = ks + pltpu.roll(ks, dk, axis=1)
                delta = beta * (vdup - a * ks)
                oh = None
                for i in range(nv):
                    si = a * s[i] + kx[i] * delta
                    s_ref[pl.ds(b0, SUBLANES), c0 + i * LANES:c0 + (i + 1) * LANES] = si
                    term = qx_ref[pl.ds(b0, SUBLANES), c0 + i * LANES:c0 + (i + 1) * LANES] * si
                    oh = term if oh is None else oh + term
                oh = oh + pltpu.roll(oh, dk, axis=1)
                o_pair = oh if j == 0 else jnp.where(low, o_pair, oh)
            o_ref[pl.ds(r0, SUBLANES), :] = o_pair
            return carry

        lax.fori_loop(0, nb // SUBLANES, tile_body, 0)


def _delta_sample(q, k, v, gate, s0, nb, nt):
    dk = DN_HEAD_DIM
    flat = dk * dk
    n = nt * nb
    col = np.arange(2 * flat)
    ex = np.arange(LANES)[:, None] == ((col // flat) * dk + (col % flat) // dk)[None, :]
    ex = jnp.asarray(ex, BF16)
    kern = functools.partial(_delta_step_kernel, nt=nt, nb=nb)
    pair = lambda p: (0, p)
    return pl.pallas_call(
        kern,
        grid=(DN_HEADS // 2,),
        in_specs=[pl.BlockSpec((n, LANES), pair),
                  pl.BlockSpec((n, LANES), pair),
                  pl.BlockSpec((n, LANES), pair),
                  pl.BlockSpec((n, LANES), lambda p: (0, 0)),
                  pl.BlockSpec((nb, 2 * flat), pair),
                  pl.BlockSpec((LANES, 2 * flat), lambda p: (0, 0))],
        out_specs=[pl.BlockSpec((n, LANES), pair),
                   pl.BlockSpec((nb, 2 * flat), pair)],
        out_shape=[jax.ShapeDtypeStruct((n, DN_WIDTH), F32),
                   jax.ShapeDtypeStruct((nb, DN_HEADS * flat), F32)],
        scratch_shapes=[pltpu.VMEM((nb, 2 * flat), F32),
                        pltpu.VMEM((nb, 2 * flat), F32)],
        compiler_params=_cparams(("arbitrary",)),
        name="delta_sample",
    )(q, k, v, gate, s0, ex)


def _s5_kernel(u_ref, btre_ref, btim_ref, lam_ref, ctre_ref, ctim_ref, d_ref, h0_ref, y_ref, hfin_ref,
               bw_ref, c_ref, ab_ref, x_ref, h_ref, *, nb, tt):
    p2 = S5_FLAT

    @pl.when(pl.program_id(0) == 0)
    def _():
        lr = lam_ref[0:1, :]
        li = lam_ref[1:2, :]
        dt = jnp.exp(lam_ref[2:3, :])
        mag = jnp.exp(lr * dt)
        ab_re = mag * jnp.cos(li * dt)
        ab_im = mag * jnp.sin(li * dt)
        den = lr * lr + li * li
        nr = ab_re - 1.0
        ni = ab_im
        f_re = (nr * lr + ni * li) / den
        f_im = (ni * lr - nr * li) / den
        ab_ref[0:1, :] = ab_re
        ab_ref[1:2, :] = ab_im
        gpl = LANES // S5_STATE
        ch_g = lax.broadcasted_iota(I32, (S5_WIDTH, LANES), 0) // S5_GROUP_CH
        lane_g = lax.broadcasted_iota(I32, (S5_WIDTH, LANES), 1) // S5_STATE
        bre2 = jnp.concatenate([btre_ref[...]] * gpl, axis=1)
        bim2 = jnp.concatenate([btim_ref[...]] * gpl, axis=1)
        for j in range(p2 // LANES):
            cols = slice(j * LANES, (j + 1) * LANES)
            own = ch_g == gpl * j + lane_g
            bre = jnp.where(own, bre2, 0.0)
            bim = jnp.where(own, bim2, 0.0)
            bw_ref[:, cols] = (bre * f_re[:, cols] - bim * f_im[:, cols]).astype(BF16)
            bw_ref[:, p2 + j * LANES:p2 + (j + 1) * LANES] = (bim * f_re[:, cols] + bre * f_im[:, cols]).astype(BF16)
        cpl = LANES // S5_GROUP_CH
        st_g = lax.broadcasted_iota(I32, (p2, LANES), 0) // S5_STATE
        lane_cg = lax.broadcasted_iota(I32, (p2, LANES), 1) // S5_GROUP_CH
        for j in range(S5_WIDTH // LANES):
            cols = slice(j * LANES, (j + 1) * LANES)
            own = st_g == cpl * j + lane_cg
            c_ref[0:p2, cols] = jnp.where(own, ctre_ref[...], 0.0).astype(BF16)
            c_ref[p2:2 * p2, cols] = jnp.where(own, -ctim_ref[...], 0.0).astype(BF16)
        h_ref[...] = h0_ref[...]

    u = u_ref[...]
    cw = S5_WIDTH // S5_SUPER
    sw = S5_FLAT // S5_SUPER
    for part in (0, p2):
        for b in range(S5_SUPER):
            x_ref[:, part + b * sw:part + (b + 1) * sw] = jnp.dot(
                u[:, b * cw:(b + 1) * cw], bw_ref[b * cw:(b + 1) * cw, part + b * sw:part + (b + 1) * sw],
                preferred_element_type=F32)
    a_re = ab_ref[0:1, :]
    a_im = ab_ref[1:2, :]

    if nb == SUBLANES:
        wsl = p2 // S5_SCAN_SPLIT
        for sp in range(S5_SCAN_SPLIT):
            c0 = sp * wsl
            are = jnp.broadcast_to(a_re[:, c0:c0 + wsl], (nb, wsl))
            aim = jnp.broadcast_to(a_im[:, c0:c0 + wsl], (nb, wsl))

            def step(t, carry, c0=c0, are=are, aim=aim):
                hr, hi = carry
                r0 = pl.multiple_of(t * nb, nb)
                nr = are * hr - aim * hi + x_ref[pl.ds(r0, nb), c0:c0 + wsl]
                ni = are * hi + aim * hr + x_ref[pl.ds(r0, nb), p2 + c0:p2 + c0 + wsl]
                x_ref[pl.ds(r0, nb), c0:c0 + wsl] = nr
                x_ref[pl.ds(r0, nb), p2 + c0:p2 + c0 + wsl] = ni
                return nr, ni

            hr, hi = lax.fori_loop(0, tt, step, (h_ref[:, c0:c0 + wsl], h_ref[:, p2 + c0:p2 + c0 + wsl]),
                                   unroll=2)
            h_ref[:, c0:c0 + wsl] = hr
            h_ref[:, p2 + c0:p2 + c0 + wsl] = hi
    else:
        for t in range(tt):
            rs = slice(t * nb, (t + 1) * nb)
            hr = h_ref[:, 0:p2]
            hi = h_ref[:, p2:2 * p2]
            nr = a_re * hr - a_im * hi + x_ref[rs, 0:p2]
            ni = a_re * hi + a_im * hr + x_ref[rs, p2:2 * p2]
            h_ref[:, 0:p2] = nr
            h_ref[:, p2:2 * p2] = ni
            x_ref[rs, 0:p2] = nr
            x_ref[rs, p2:2 * p2] = ni

    for b in range(S5_SUPER):
        cols = slice(b * cw, (b + 1) * cw)
        y = None
        for part in (0, p2):
            rws = slice(part + b * sw, part + (b + 1) * sw)
            term = jnp.dot(x_ref[:, rws].astype(BF16), c_ref[rws, cols], preferred_element_type=F32)
            y = term if y is None else y + term
        y_ref[:, cols] = y + d_ref[:, cols] * u[:, cols].astype(F32)
    hfin_ref[...] = h_ref[...]


def _s5(u, params, h0, nb):
    btre, btim, lam, ctre, ctim, dvec = params
    n = u.shape[0]
    t = n // nb
    tt = min(ROW_TILE // nb, t)
    rows = tt * nb
    const = lambda i: (0, 0)
    kern = functools.partial(_s5_kernel, nb=nb, tt=tt)
    return pl.pallas_call(
        kern,
        grid=(t // tt,),
        in_specs=[pl.BlockSpec((rows, S5_WIDTH), lambda i: (i, 0)),
                  pl.BlockSpec((S5_WIDTH, S5_STATE), const),
                  pl.BlockSpec((S5_WIDTH, S5_STATE), const),
                  pl.BlockSpec((SUBLANES, S5_FLAT), const),
                  pl.BlockSpec((S5_FLAT, LANES), const),
                  pl.BlockSpec((S5_FLAT, LANES), const),
                  pl.BlockSpec((1, S5_WIDTH), const),
                  pl.BlockSpec((nb, 2 * S5_FLAT), const)],
        out_specs=[pl.BlockSpec((rows, S5_WIDTH), lambda i: (i, 0)),
                   pl.BlockSpec((nb, 2 * S5_FLAT), const)],
        out_shape=[jax.ShapeDtypeStruct((n, S5_WIDTH), F32),
                   jax.ShapeDtypeStruct((nb, 2 * S5_FLAT), F32)],
        scratch_shapes=[pltpu.VMEM((S5_WIDTH, 2 * S5_FLAT), BF16),
                        pltpu.VMEM((2 * S5_FLAT, S5_WIDTH), BF16),
                        pltpu.VMEM((SUBLANES, S5_FLAT), F32),
                        pltpu.VMEM((rows, 2 * S5_FLAT), F32),
                        pltpu.VMEM((nb, 2 * S5_FLAT), F32)],
        compiler_params=_cparams(("arbitrary",)),
        name="s5",
    )(u, btre, btim, lam, ctre, ctim, dvec, h0)


def _postmix_kernel(xp_ref, op_ref, zp_ref, ysp_ref, gap_ref, gbp_ref,
                    xs_ref, os_ref, zs_ref, yss_ref, gas_ref, gbs_ref, *rest, nblk_p, range_tok):
    carry_ref = rest[-1]

    @pl.when(pl.program_id(0) == 0)
    def _():
        carry_ref[...] = jnp.zeros_like(carry_ref)

    @pl.when(pl.program_id(0) < nblk_p)
    def _():
        _postmix_body(xp_ref, op_ref, zp_ref, ysp_ref, gap_ref, gbp_ref, *rest, range_tok=range_tok)

    @pl.when(pl.program_id(0) >= nblk_p)
    def _():
        _postmix_body(xs_ref, os_ref, zs_ref, yss_ref, gas_ref, gbs_ref, *rest, range_tok=range_tok)


def _postmix_body(x_ref, o_ref, z_ref, ys_ref, ga_ref, gb_ref, hw_ref, seg_ref, wa_ref, wglu_ref, wb_ref,
                  wo_ref, nf_ref, wr_ref, su_ref, x1_ref, hn_ref, bkt_ref, rank_ref, rw_ref, cnt_ref, carry_ref,
                  *, range_tok):
    o = o_ref[...]
    ms = jnp.dot((o * o).astype(BF16), seg_ref[...], preferred_element_type=F32) * (1.0 / DN_HEAD_DIM)
    on = o * lax.rsqrt(ms + RMS_EPS) * hw_ref[...]
    z = z_ref[...]
    oa = on * (z * jax.nn.sigmoid(z)).astype(F32)
    y_a = _mm(oa, wa_ref[...])
    ys = jax.nn.gelu(ys_ref[...])
    ys = ys * jax.nn.sigmoid(_mm(ys, wglu_ref[...]))
    y_b = _mm(ys, wb_ref[...])
    mixed = jax.nn.sigmoid(ga_ref[...]).astype(F32) * y_a + jax.nn.sigmoid(gb_ref[...]).astype(F32) * y_b
    x1 = x_ref[...] + _mm(mixed, wo_ref[...])
    x1_ref[...] = x1
    hn = x1 * lax.rsqrt(jnp.mean(x1 * x1, axis=-1, keepdims=True) + RMS_EPS) * nf_ref[...]
    _slab_store(hn_ref, hn)

    wr = wr_ref[...]
    w_hi = wr.astype(BF16)
    w_lo = (wr - w_hi.astype(F32)).astype(BF16)
    hn_hi = hn.astype(BF16)
    hn_lo = (hn - hn_hi.astype(F32)).astype(BF16)
    both = _mm_nt(jnp.concatenate([w_hi, w_lo], axis=0), hn_hi)
    logits = both[:ROUTER_ROWS] + both[ROUTER_ROWS:] + _mm_nt(w_hi, hn_lo)
    coarse = logits[N_EXPERTS:N_EXPERTS + MOE_GROUPS, :]
    cm = jnp.max(coarse, axis=0, keepdims=True)
    ce = jnp.exp(coarse - cm)
    pc = ce / jnp.sum(ce, axis=0, keepdims=True)
    p_sel = jnp.max(pc, axis=0, keepdims=True)
    gi = lax.broadcasted_iota(I32, pc.shape, 0)
    g_sel = jnp.min(jnp.where(pc == p_sel, gi, MOE_GROUPS), axis=0, keepdims=True)
    fine = jnp.zeros((EXPERTS_PER_GROUP, logits.shape[1]), F32)
    for g in range(MOE_GROUPS):
        fine = fine + jnp.where(g_sel == g, logits[g * EXPERTS_PER_GROUP:(g + 1) * EXPERTS_PER_GROUP, :], 0.0)
    fm = jnp.max(fine, axis=0, keepdims=True)
    fe = jnp.exp(fine - fm)
    pf = fe / jnp.sum(fe, axis=0, keepdims=True)
    ei = lax.broadcasted_iota(I32, pf.shape, 0)
    v1 = jnp.max(pf, axis=0, keepdims=True)
    i1 = jnp.min(jnp.where(pf == v1, ei, EXPERTS_PER_GROUP), axis=0, keepdims=True)
    rest = jnp.where(ei == i1, -1.0, pf)
    v2 = jnp.max(rest, axis=0, keepdims=True)
    i2 = jnp.min(jnp.where(rest == v2, ei, EXPERTS_PER_GROUP), axis=0, keepdims=True)
    tot = v1 + v2
    rw_ref[0:1, :] = v1 / tot * p_sel
    rw_ref[1:2, :] = v2 / tot * p_sel

    tt = logits.shape[1]
    tok = pl.program_id(0) * tt + lax.broadcasted_iota(I32, (1, tt), 1)
    ph = jnp.zeros((1, tt), I32)
    for r in range(1, MOE_PHASES):
        ph = ph + (tok >= r * range_tok).astype(I32)
    bsel = [ph * N_EXPERTS + g_sel * EXPERTS_PER_GROUP + ix for ix in (i1, i2)]
    bi = lax.broadcasted_iota(I32, (MOE_PHASES * N_EXPERTS, tt), 0)
    onehot = [(bi == b).astype(F32) for b in bsel]
    cnt = onehot[0] + onehot[1]
    before = carry_ref[:, 0:1] + jnp.dot(cnt.astype(BF16), su_ref[...], preferred_element_type=F32)
    for s in range(TOP_K):
        bkt_ref[s:s + 1, :] = bsel[s]
        rank_ref[s:s + 1, :] = jnp.sum(onehot[s] * before, axis=0, keepdims=True).astype(I32)
    carry_ref[...] = carry_ref[...] + jnp.sum(cnt, axis=1, keepdims=True)
    cnt_ref[...] = carry_ref[...]


def _postmix(prompt, sample, weights, nb):
    n_p = prompt[0].shape[0]
    n_s = sample[0].shape[0]
    t = n_p // nb
    tt = min(ROW_TILE, t, n_s)
    nt = t // tt
    nblk_p = n_p // tt
    nblk = nblk_p + n_s // tt
    n_total = n_p + n_s
    prow = lambda i: (jnp.minimum(i, nblk_p - 1), 0)
    pys = lambda i: (jnp.minimum(i, nblk_p - 1) % nt, jnp.minimum(i, nblk_p - 1) // nt)
    srow = lambda i: (jnp.maximum(i - nblk_p, 0), 0)
    const = lambda i: (0, 0)

    def stream_specs(row, ysmap):
        return [pl.BlockSpec((tt, D_MODEL), row),
                pl.BlockSpec((tt, DN_WIDTH), row),
                pl.BlockSpec((tt, DN_WIDTH), row),
                pl.BlockSpec((tt, S5_WIDTH), ysmap),
                pl.BlockSpec((tt, D_MODEL), row),
                pl.BlockSpec((tt, D_MODEL), row)]

    weight_specs = [pl.BlockSpec((1, DN_WIDTH), const),
                    pl.BlockSpec((DN_WIDTH, DN_WIDTH), const),
                    pl.BlockSpec((DN_WIDTH, D_MODEL), const),
                    pl.BlockSpec((S5_WIDTH, S5_WIDTH), const),
                    pl.BlockSpec((S5_WIDTH, D_MODEL), const),
                    pl.BlockSpec((D_MODEL, D_MODEL), const),
                    pl.BlockSpec((1, D_MODEL), const),
                    pl.BlockSpec((ROUTER_ROWS, D_MODEL), const),
                    pl.BlockSpec((tt, tt), const)]
    xp, op, zp, ysp, gap, gbp = prompt
    nbk = MOE_PHASES * N_EXPERTS
    earlier = jnp.triu(jnp.ones((tt, tt), F32), k=1).astype(BF16)
    return pl.pallas_call(
        functools.partial(_postmix_kernel, nblk_p=nblk_p, range_tok=n_total // MOE_PHASES),
        grid=(nblk,),
        in_specs=stream_specs(prow, pys) + stream_specs(srow, srow) + weight_specs,
        out_specs=[pl.BlockSpec((tt, D_MODEL), lambda i: (i, 0)),
                   pl.BlockSpec((tt * ROW_SLAB, LANES), lambda i: (i, 0)),
                   pl.BlockSpec((TOP_K, tt), lambda i: (0, i)),
                   pl.BlockSpec((TOP_K, tt), lambda i: (0, i)),
                   pl.BlockSpec((TOP_K, tt), lambda i: (0, i)),
                   pl.BlockSpec((nbk, LANES), const)],
        out_shape=[jax.ShapeDtypeStruct((n_total, D_MODEL), F32),
                   jax.ShapeDtypeStruct((n_total * ROW_SLAB, LANES), F32),
                   jax.ShapeDtypeStruct((TOP_K, n_total), I32),
                   jax.ShapeDtypeStruct((TOP_K, n_total), I32),
                   jax.ShapeDtypeStruct((TOP_K, n_total), F32),
                   jax.ShapeDtypeStruct((nbk, LANES), F32)],
        scratch_shapes=[pltpu.VMEM((nbk, LANES), F32)],
        compiler_params=_cparams(("arbitrary",)),
        name="postmix",
    )(xp, op, zp, ysp.reshape(t, nb * S5_WIDTH), gap, gbp, *sample, *weights, earlier)


def _wait_slabs(buf, sem):
    pltpu.make_async_copy(buf, buf, sem).wait()


def _moe_kernel(texp_ref, tph_ref, tsrc_ref, tnv_ref, tfirst_ref, tslot_ref, tnext_ref, otok_ref,
                hn_hbm, wu_hbm, wd_hbm, y_ref, hnv, xbuf, wu_buf, wd_buf, wub, wdb, sem, wsem):
    i = pl.program_id(0)
    tm = MOE_TILE
    rs = ROW_SLAB
    nv = tnv_ref[i]
    ph = tph_ref[i]
    range_rows = hnv.shape[0]

    def weight_copies(e, sl):
        return (pltpu.make_async_copy(wu_hbm.at[e], wu_buf.at[sl], wsem.at[sl]),
                pltpu.make_async_copy(wd_hbm.at[e], wd_buf.at[sl], wsem.at[sl]))

    @pl.when(i == 0)
    def _():
        for c in weight_copies(texp_ref[0], 0):
            c.start()

    @pl.when(jnp.logical_and(nv > 0, jnp.logical_or(i == 0, ph != tph_ref[jnp.maximum(i - 1, 0)])))
    def _():
        start = pl.multiple_of(ph * range_rows, rs)
        whole = pltpu.make_async_copy(hn_hbm.at[pl.ds(start, range_rows), :], hnv, sem)
        whole.start()
        whole.wait()

    for sl in range(2):
        @pl.when(jnp.logical_and(jnp.logical_and(nv > 0, tfirst_ref[i] == 1), tslot_ref[i] == sl))
        def _():
            for c in weight_copies(texp_ref[i], sl):
                c.wait()

            @pl.when(tnext_ref[i] >= 0)
            def _():
                for c in weight_copies(tnext_ref[i], 1 - sl):
                    c.start()

            wub[...] = wu_buf[sl].astype(BF16)
            wdb[...] = wd_buf[sl].astype(BF16)

    @pl.when(nv == 0)
    def _():
        y_ref[...] = jnp.zeros_like(y_ref)

    @pl.when(nv > 0)
    def _():
        src0 = tsrc_ref[i]
        for r in range(tm):
            tok8 = pl.multiple_of(otok_ref[src0 + r], rs)
            xbuf[pl.ds(r * rs, rs), :] = hnv[pl.ds(tok8, rs), :]
        x = _slab_load(xbuf, tm).astype(BF16)
        hu = jnp.dot(x, wub[...], preferred_element_type=F32)
        gate = hu[:, :EXPERT_FF]
        up = hu[:, EXPERT_FF:]
        act = gate * jax.nn.sigmoid(gate) * up
        _slab_store(y_ref, jnp.dot(act.astype(BF16), wdb[...], preferred_element_type=F32))


def _moe(hn, w_up, w_down, plan):
    ntiles = plan[0].shape[0]
    grid_spec = pltpu.PrefetchScalarGridSpec(
        num_scalar_prefetch=len(plan),
        grid=(ntiles,),
        in_specs=[pl.BlockSpec(memory_space=pl.ANY),
                  pl.BlockSpec(memory_space=pl.ANY),
                  pl.BlockSpec(memory_space=pl.ANY)],
        out_specs=pl.BlockSpec((MOE_TILE * ROW_SLAB, LANES), lambda i, *_: (i, 0)),
        scratch_shapes=[pltpu.VMEM((hn.shape[0] // MOE_PHASES, LANES), F32),
                        pltpu.VMEM((MOE_TILE * ROW_SLAB, LANES), F32),
                        pltpu.VMEM((2, D_MODEL, 2 * EXPERT_FF), F32),
                        pltpu.VMEM((2, EXPERT_FF, D_MODEL), F32),
                        pltpu.VMEM((D_MODEL, 2 * EXPERT_FF), BF16),
                        pltpu.VMEM((EXPERT_FF, D_MODEL), BF16),
                        pltpu.SemaphoreType.DMA,
                        pltpu.SemaphoreType.DMA((2,))])
    return pl.pallas_call(
        _moe_kernel,
        grid_spec=grid_spec,
        out_shape=jax.ShapeDtypeStruct((ntiles * MOE_TILE * ROW_SLAB, LANES), F32),
        compiler_params=_cparams(("arbitrary",)),
        name="moe",
    )(*plan, hn, w_up, w_down)


def _combine_kernel(bkt_ref, rank_ref, pad_ref, x1_ref, ys_hbm, w_ref, nw_ref, outp_ref, outs_ref,
                    ybuf0, ybuf1, sem, *, nblk_p, n_tok):
    i = pl.program_id(0)
    nsteps = pl.num_programs(0)
    tt = x1_ref.shape[0]
    rs = ROW_SLAB
    slot = lax.rem(i, 2)
    ybuf = (ybuf0, ybuf1)

    def start_gather(step, sl):
        base = step * tt
        for r in range(tt * TOP_K):
            j, s = divmod(r, TOP_K)
            a = s * n_tok + base + j
            p8 = pl.multiple_of(pad_ref[bkt_ref[a]] + rank_ref[a], rs)
            pltpu.make_async_copy(ys_hbm.at[pl.ds(p8, rs), :], ybuf[sl].at[pl.ds((s * tt + j) * rs, rs), :],
                                  sem.at[sl]).start(priority=r % DMA_QUEUES)

    @pl.when(i == 0)
    def _():
        start_gather(0, 0)

    for sl in range(2):
        @pl.when(slot == sl)
        def _():
            _wait_slabs(ybuf[sl], sem.at[sl])
            start_gather(jnp.minimum(i + 1, nsteps - 1), 1 - sl)
            w = w_ref[...]
            y0 = _slab_load(ybuf[sl], tt, 0)
            y1 = _slab_load(ybuf[sl], tt, tt * rs)
            x = x1_ref[...] + w[:, 0:1] * y0 + w[:, 1:2] * y1
            res = x * lax.rsqrt(jnp.mean(x * x, axis=-1, keepdims=True) + RMS_EPS) * nw_ref[...]

            @pl.when(i < nblk_p)
            def _():
                outp_ref[...] = res

            @pl.when(i >= nblk_p)
            def _():
                outs_ref[...] = res

        @pl.when(jnp.logical_and(slot == sl, i == nsteps - 1))
        def _():
            _wait_slabs(ybuf[1 - sl], sem.at[1 - sl])


def _combine(x1, ysorted, where, wtok, nw, n_p):
    n = x1.shape[0]
    tt = math.gcd(math.gcd(n_p, n - n_p), COMBINE_TILE)
    nblk_p = n_p // tt
    grid_spec = pltpu.PrefetchScalarGridSpec(
        num_scalar_prefetch=3,
        grid=(n // tt,),
        in_specs=[pl.BlockSpec((tt, D_MODEL), lambda i, *_: (i, 0)),
                  pl.BlockSpec(memory_space=pl.ANY),
                  pl.BlockSpec((tt, TOP_K), lambda i, *_: (i, 0)),
                  pl.BlockSpec((1, D_MODEL), lambda i, *_: (0, 0))],
        out_specs=[pl.BlockSpec((tt, D_MODEL), lambda i, *_: (jnp.minimum(i, nblk_p - 1), 0)),
                   pl.BlockSpec((tt, D_MODEL), lambda i, *_: (jnp.maximum(i - nblk_p, 0), 0))],
        scratch_shapes=[pltpu.VMEM((tt * TOP_K * ROW_SLAB, LANES), F32),
                        pltpu.VMEM((tt * TOP_K * ROW_SLAB, LANES), F32),
                        pltpu.SemaphoreType.DMA((2,))])
    return pl.pallas_call(
        functools.partial(_combine_kernel, nblk_p=nblk_p, n_tok=n),
        grid_spec=grid_spec,
        out_shape=[jax.ShapeDtypeStruct((n_p, D_MODEL), F32),
                   jax.ShapeDtypeStruct((n - n_p, D_MODEL), F32)],
        compiler_params=_cparams(("arbitrary",)),
        name="combine",
    )(*where, x1, ysorted, wtok, nw)


def _route_plan(bkt, rank, cnt, n_tok):
    tm = MOE_TILE
    n_assign = n_tok * TOP_K
    nbk = MOE_PHASES * N_EXPERTS
    ntiles = n_assign // tm + nbk
    range_tok = n_tok // MOE_PHASES
    b_flat = bkt.T.reshape(n_assign)
    order = jnp.argsort(b_flat, stable=True).astype(I32)
    counts = cnt[:, 0].astype(I32)
    cstart = jnp.cumsum(counts) - counts
    tiles_b = (counts + tm - 1) // tm
    tend = jnp.cumsum(tiles_b)
    tstart = tend - tiles_b
    tile_id = jnp.arange(ntiles, dtype=I32)
    tbk = jnp.minimum(jnp.sum((tile_id[:, None] >= tend[None, :]).astype(I32), axis=1), nbk - 1)
    onehot = (tbk[:, None] == jnp.arange(nbk, dtype=I32)[None, :]).astype(I32)
    pick = lambda v: jnp.sum(onehot * v[None, :], axis=1)
    done = (tile_id - pick(tstart)) * tm
    tnv = jnp.where(tile_id < tend[-1], jnp.clip(pick(counts) - done, 0, tm), 0)
    tsrc = jnp.where(tnv > 0, pick(cstart) + done, 0)
    texp = tbk % N_EXPERTS
    tph = tbk // N_EXPERTS
    nonempty = counts > 0
    bslot = (jnp.cumsum(nonempty.astype(I32)) - 1) % 2
    bidx = jnp.where(nonempty, jnp.arange(nbk, dtype=I32), nbk)
    nxt = jnp.concatenate([lax.cummin(bidx[::-1])[::-1][1:], jnp.full((1,), nbk, I32)])
    bnext = jnp.where(nxt < nbk, nxt % N_EXPERTS, -1)
    tfirst = jnp.logical_and(tnv > 0, done == 0).astype(I32)
    tslot = pick(bslot)
    tnext = pick(bnext)
    otok8 = jnp.concatenate([((order // TOP_K) % range_tok) * ROW_SLAB, jnp.zeros((tm,), I32)])
    plan = tuple(a.astype(I32) for a in (texp, tph, tsrc, tnv, tfirst, tslot, tnext, otok8))
    where = (bkt.reshape(n_assign), rank.reshape(n_assign) * ROW_SLAB, (tstart * tm * ROW_SLAB).astype(I32))
    return plan, where


def _block_diag(m):
    g, a, b = m.shape
    eye = jnp.eye(g, dtype=m.dtype)
    return (eye[:, None, :, None] * m[:, :, None, :]).reshape(g * a, g * b)


def kernel(x_prompt, x_sample, state_conv, state_delta, state_ssm_re, state_ssm_im, norm_mix_w, w_in, conv_w, a_log, dt_bias, head_norm_w, w_a_up, s5_lambda_re, s5_lambda_im, s5_log_step, s5_b_re, s5_b_im, s5_c_re, s5_c_im, s5_d, w_glu, w_b_up, w_o, norm_ffn_w, w_router_coarse, w_router_fine, w_expert_up, w_expert_down, norm_final_w):
    bp, tp, _ = x_prompt.shape
    bs, ts, _ = x_sample.shape
    n_p = bp * tp
    n_s = bs * ts
    n_tok = n_p + n_s
    l = 0

    w = w_in[l]
    cuts = np.cumsum([0, QKV_DIM, DN_WIDTH, DN_HEADS, DN_HEADS, S5_WIDTH, D_MODEL, D_MODEL])
    w_qkv, w_z, w_a, w_b, w_u, w_ga, w_gb = [w[:, cuts[i]:cuts[i + 1]] for i in range(7)]
    w_ab = jnp.concatenate([w_a, w_b, jnp.zeros((D_MODEL, LANES - 2 * DN_HEADS), F32)], axis=1)
    wcat = jnp.concatenate([w_qkv, w_z, w_u, w_ga, w_gb, w_ab], axis=1).astype(BF16)
    nw_mix = norm_mix_w[l].reshape(1, D_MODEL)
    pad8 = lambda v: jnp.concatenate([v, jnp.zeros((LANES - DN_HEADS,), F32)]).reshape(1, LANES)
    gate_p = jnp.concatenate([pad8(a_log[l]), pad8(dt_bias[l])], axis=0)
    seg = _block_diag(jnp.ones((DN_HEADS, DN_HEAD_DIM, DN_HEAD_DIM), BF16))
    chan_rows = lambda b: jnp.swapaxes(b, 1, 2).reshape(S5_WIDTH, S5_STATE)
    state_rows = lambda c: jnp.tile(jnp.swapaxes(c, 1, 2).reshape(S5_FLAT, S5_GROUP_CH),
                                    (1, LANES // S5_GROUP_CH))
    lam = jnp.concatenate([s5_lambda_re[l].reshape(1, S5_FLAT), s5_lambda_im[l].reshape(1, S5_FLAT),
                           jnp.repeat(s5_log_step[l], S5_STATE).reshape(1, S5_FLAT),
                           jnp.zeros((SUBLANES - 3, S5_FLAT), F32)], axis=0)
    s5_params = (chan_rows(s5_b_re[l]), chan_rows(s5_b_im[l]), lam,
                 state_rows(s5_c_re[l]), state_rows(s5_c_im[l]), s5_d[l].reshape(1, S5_WIDTH))
    hw = jnp.tile(head_norm_w[l], DN_HEADS).reshape(1, DN_WIDTH)
    wr = jnp.concatenate([w_router_fine[l].T, w_router_coarse[l].T,
                          jnp.zeros((ROUTER_ROWS - N_EXPERTS - MOE_GROUPS, D_MODEL), F32)], axis=0)
    pm_weights = (hw, seg, w_a_up[l].astype(BF16), w_glu[l].astype(BF16), w_b_up[l].astype(BF16),
                  w_o[l].astype(BF16), norm_ffn_w[l].reshape(1, D_MODEL), wr)

    xp2 = x_prompt.reshape(n_p, D_MODEL)
    q_p, k_p, v_p, gates_p, conv_p, z_p, u_p, ga_p, gb_p = _inprep(
        xp2, nw_mix, wcat, jnp.zeros((bp, SUBLANES, QKV_DIM), F32), conv_w[l], gate_p, seg, bp, 1)
    o_p, delta_p = _delta_prompt(q_p, k_p, v_p, gates_p, bp)
    ys_p, h_p = _s5(u_p, s5_params, jnp.zeros((bp, 2 * S5_FLAT), F32), bp)

    xs2 = jnp.swapaxes(x_sample, 0, 1).reshape(n_s, D_MODEL)
    cinit_s = jnp.swapaxes(state_conv[l], 0, 1).reshape(1, (CONV_W - 1) * bs, QKV_DIM)
    q_s, k_s, v_s, gate_s, conv_s, z_s, u_s, ga_s, gb_s = _inprep(
        xs2, nw_mix, wcat, cinit_s, conv_w[l], gate_p, seg, 1, bs)
    s0 = state_delta[l].reshape(bs, DN_HEADS * DN_HEAD_DIM * DN_HEAD_DIM)
    o_s, delta_s = _delta_sample(q_s, k_s, v_s, gate_s, s0, bs, ts)
    h0_s = jnp.concatenate([state_ssm_re[l].reshape(bs, S5_FLAT), state_ssm_im[l].reshape(bs, S5_FLAT)], axis=1)
    ys_s, h_s = _s5(u_s, s5_params, h0_s, bs)
    x1, hn, bkt, rank, rw, cnt = _postmix((xp2, o_p, z_p, ys_p, ga_p, gb_p), (xs2, o_s, z_s, ys_s, ga_s, gb_s),
                                          pm_weights, bp)

    plan, where = _route_plan(bkt, rank, cnt, n_tok)
    ysorted = _moe(hn, w_expert_up[l], w_expert_down[l], plan)
    y_p, y_s = _combine(x1, ysorted, where, rw.T, norm_final_w.reshape(1, D_MODEL), n_p)

    y_prompt = y_p.reshape(bp, tp, D_MODEL)
    y_sample = jnp.swapaxes(y_s.reshape(ts, bs, D_MODEL), 0, 1)
    conv_sample = jnp.swapaxes(conv_s.reshape(CONV_W - 1, bs, QKV_DIM), 0, 1)
    return (y_prompt, y_sample,
            conv_p[None], delta_p[None],
            h_p[:, :S5_FLAT].reshape(1, bp, S5_GROUPS, S5_STATE), h_p[:, S5_FLAT:].reshape(1, bp, S5_GROUPS, S5_STATE),
            conv_sample[None], delta_s.reshape(1, bs, DN_HEADS, DN_HEAD_DIM, DN_HEAD_DIM),
            h_s[:, :S5_FLAT].reshape(1, bs, S5_GROUPS, S5_STATE), h_s[:, S5_FLAT:].reshape(1, bs, S5_GROUPS, S5_STATE))
```

```python
import functools
import math

import jax
import jax.numpy as jnp
import numpy as np
from jax import lax
from jax.experimental import pallas as pl
from jax.experimental.pallas import tpu as pltpu

F32 = jnp.float32
BF16 = jnp.bfloat16
I32 = jnp.int32

D_MODEL = 1024
DN_HEADS = 8
DN_HEAD_DIM = 64
DN_WIDTH = DN_HEADS * DN_HEAD_DIM
QKV_DIM = 3 * DN_WIDTH
CONV_W = 4
DN_CHUNK = 64
S5_GROUP_CH = 16
S5_WIDTH = D_MODEL // 2
S5_GROUPS = S5_WIDTH // S5_GROUP_CH
S5_STATE = 64
S5_FLAT = S5_GROUPS * S5_STATE
MOE_GROUPS = 4
EXPERTS_PER_GROUP = 8
N_EXPERTS = MOE_GROUPS * EXPERTS_PER_GROUP
TOP_K = 2
EXPERT_FF = 256
RMS_EPS = 1e-6
L2_EPS = 1e-6

LANES = 128
SUBLANES = 8
VMEM_LIMIT = 56 * 1024 * 1024

W1_COLS = QKV_DIM + DN_WIDTH
W2_COLS = S5_WIDTH + 2 * D_MODEL

ROW_TILE = 512
INPREP_PARTS = 2
MOE_TILE = 256
MOE_PHASES = 2
COMBINE_TILE = 256
DMA_QUEUES = 2
DELTA_SUBCHUNKS = 4
S5_SUPER = 2
S5_SCAN_SPLIT = 2
ROUTER_ROWS = 40


def _mm(a, b):
    return jnp.dot(a.astype(BF16), b.astype(BF16), preferred_element_type=F32)


def _mm_nt(a, b):
    return lax.dot_general(a.astype(BF16), b.astype(BF16), (((1,), (1,)), ((), ())),
                           preferred_element_type=F32)


def _split3_dot(a, b01):
    a1 = a.astype(BF16)
    r1 = a - a1.astype(F32)
    a2 = r1.astype(BF16)
    a3 = (r1 - a2.astype(F32)).astype(BF16)
    out = jnp.dot(a3, b01, preferred_element_type=F32)
    out = out + jnp.dot(a2, b01, preferred_element_type=F32)
    return out + jnp.dot(a1, b01, preferred_element_type=F32)


def _cparams(sem):
    return pltpu.CompilerParams(dimension_semantics=sem, vmem_limit_bytes=VMEM_LIMIT)


ROW_SLAB = D_MODEL // LANES


def _slab_load(ref, rows, first=0, pitch=ROW_SLAB):
    return jnp.concatenate([ref[pl.ds(first + j, rows, stride=pitch), :] for j in range(ROW_SLAB)], axis=1)


def _slab_store(ref, x):
    for j in range(ROW_SLAB):
        ref[pl.ds(j, x.shape[0], stride=ROW_SLAB), :] = x[:, j * LANES:(j + 1) * LANES]


def _softplus(x):
    return jnp.maximum(x, 0.0) + jnp.log1p(jnp.exp(-jnp.abs(x)))


def _inprep_kernel(x_ref, nw_ref, w1_ref, w2_ref, wab_ref, cinit_ref, cw_ref, gp_ref, seg_ref,
                   q_ref, k_ref, v_ref, gate_ref, cnew_ref, z_ref, u_ref, ga_ref, gb_ref, xp_ref,
                   *, shift, rc, rows):
    @pl.when(pl.program_id(1) == 0)
    def _():
        xp_ref[0:rc, :] = cinit_ref[0]

    seg = seg_ref[...]
    pr = rows // INPREP_PARTS
    for part in range(INPREP_PARTS):
        rs = slice(part * pr, (part + 1) * pr)
        x = x_ref[rs, :]
        h = x * lax.rsqrt(jnp.mean(x * x, axis=-1, keepdims=True) + RMS_EPS) * nw_ref[...]
        hb = h.astype(BF16)

        def proj(w_ref, lo, hi, hb=hb):
            return jnp.dot(hb, w_ref[:, lo:hi], preferred_element_type=F32)

        xp_ref[rc + part * pr:rc + (part + 1) * pr, :] = proj(w1_ref, 0, QKV_DIM)
        ab = proj(wab_ref, 0, LANES)
        z_ref[rs, :] = proj(w1_ref, QKV_DIM, W1_COLS).astype(z_ref.dtype)
        u_ref[rs, :] = proj(w2_ref, 0, S5_WIDTH).astype(u_ref.dtype)
        ga_ref[rs, :] = proj(w2_ref, S5_WIDTH, S5_WIDTH + D_MODEL).astype(ga_ref.dtype)
        gb_ref[rs, :] = proj(w2_ref, S5_WIDTH + D_MODEL, W2_COLS).astype(gb_ref.dtype)
        acc = None
        for i in range(CONV_W):
            lo = rc + part * pr + (i - (CONV_W - 1)) * shift
            term = xp_ref[lo:lo + pr, :] * cw_ref[i:i + 1, :]
            acc = term if acc is None else acc + term
        y = acc * jax.nn.sigmoid(acc)
        q = y[:, 0:DN_WIDTH]
        k = y[:, DN_WIDTH:2 * DN_WIDTH]
        q_ref[rs, :] = q * lax.rsqrt(jnp.dot((q * q).astype(BF16), seg, preferred_element_type=F32) + L2_EPS)
        k_ref[rs, :] = k * lax.rsqrt(jnp.dot((k * k).astype(BF16), seg, preferred_element_type=F32) + L2_EPS)
        v_ref[rs, :] = y[:, 2 * DN_WIDTH:]
        g = -jnp.exp(gp_ref[0:1, :]) * _softplus(ab + gp_ref[1:2, :])
        beta = jax.nn.sigmoid(ab)
        lane = lax.broadcasted_iota(I32, ab.shape, 1)
        gate_ref[rs, :] = jnp.where(lane < DN_HEADS, g, beta)

    keep = (CONV_W - 1) * shift
    cnew_ref[0] = xp_ref[rc + rows - keep:rc + rows, :]
    xp_ref[0:rc, :] = xp_ref[rows:rows + rc, :]


def _inprep(x2d, nw, w_parts, cinit, conv_w, gate_p, seg, nb, shift):
    n = x2d.shape[0]
    r = n // nb
    rows = min(ROW_TILE, r)
    nt = r // rows
    rc = cinit.shape[1]
    keep = (CONV_W - 1) * shift
    row = lambda b, i: (b * nt + i, 0)
    const = lambda b, i: (0, 0)
    kern = functools.partial(_inprep_kernel, shift=shift, rc=rc, rows=rows)
    outs = pl.pallas_call(
        kern,
        grid=(nb, nt),
        in_specs=[pl.BlockSpec((rows, D_MODEL), row),
                  pl.BlockSpec((1, D_MODEL), const),
                  pl.BlockSpec((D_MODEL, W1_COLS), const),
                  pl.BlockSpec((D_MODEL, W2_COLS), const),
                  pl.BlockSpec((D_MODEL, LANES), const),
                  pl.BlockSpec((1, rc, QKV_DIM), lambda b, i: (b, 0, 0)),
                  pl.BlockSpec((CONV_W, QKV_DIM), const),
                  pl.BlockSpec((2, LANES), const),
                  pl.BlockSpec((DN_WIDTH, DN_WIDTH), const)],
        out_specs=[pl.BlockSpec((rows, DN_WIDTH), row),
                   pl.BlockSpec((rows, DN_WIDTH), row),
                   pl.BlockSpec((rows, DN_WIDTH), row),
                   pl.BlockSpec((rows, LANES), row),
                   pl.BlockSpec((1, keep, QKV_DIM), lambda b, i: (b, 0, 0)),
                   pl.BlockSpec((rows, DN_WIDTH), row),
                   pl.BlockSpec((rows, S5_WIDTH), lambda b, i: (i, b)),
                   pl.BlockSpec((rows, D_MODEL), row),
                   pl.BlockSpec((rows, D_MODEL), row)],
        out_shape=[jax.ShapeDtypeStruct((n, DN_WIDTH), F32),
                   jax.ShapeDtypeStruct((n, DN_WIDTH), F32),
                   jax.ShapeDtypeStruct((n, DN_WIDTH), F32),
                   jax.ShapeDtypeStruct((n, LANES), F32),
                   jax.ShapeDtypeStruct((nb, keep, QKV_DIM), F32),
                   jax.ShapeDtypeStruct((n, DN_WIDTH), BF16),
                   jax.ShapeDtypeStruct((r, nb * S5_WIDTH), BF16),
                   jax.ShapeDtypeStruct((n, D_MODEL), BF16),
                   jax.ShapeDtypeStruct((n, D_MODEL), BF16)],
        scratch_shapes=[pltpu.VMEM((rc + rows, QKV_DIM), F32)],
        compiler_params=_cparams(("arbitrary", "arbitrary")),
        name="inprep",
    )(x2d, nw, *w_parts, cinit, conv_w, gate_p, seg)
    q, k, v, gate, cnew, z, u, ga, gb = outs
    return q, k, v, gate, cnew, z, u.reshape(r * nb, S5_WIDTH), ga, gb


def _delta_chunk_kernel(q_ref, k_ref, v_ref, gate_ref, tril_ref, o_ref, sfin_ref, s_ref, *, nsub):
    c = DN_CHUNK
    dk = DN_HEAD_DIM

    @pl.when(pl.program_id(1) == 0)
    def _():
        s_ref[...] = jnp.zeros_like(s_ref)

    rowi = lax.broadcasted_iota(I32, (c, c), 0)
    coli = lax.broadcasted_iota(I32, (c, c), 1)
    causal = rowi >= coli
    strict = rowi > coli
    tril = tril_ref[...]
    pairs = [(j, h) for j in range(nsub) for h in range(DN_HEADS)]
    rows = [slice(j * c, (j + 1) * c) for j in range(nsub)]
    gate = [gate_ref[rows[j], :] for j in range(nsub)]
    gc_all = [_split3_dot_left(tril, gate[j]) for j in range(nsub)]
    gc_t = [gc_all[j].T for j in range(nsub)]

    def head(ref, j, h):
        return ref[rows[j], h * dk:(h + 1) * dk]

    qh = {p: head(q_ref, *p) * (dk ** -0.5) for p in pairs}
    kh = {p: head(k_ref, *p) for p in pairs}
    gcol = {(j, h): gc_all[j][:, h:h + 1] for j, h in pairs}
    beta = {(j, h): gate[j][:, DN_HEADS + h:DN_HEADS + h + 1] for j, h in pairs}
    grow2 = {(j, h): jnp.concatenate([gc_t[j][h:h + 1, :], gc_t[j][h:h + 1, :]], axis=1) for j, h in pairs}
    rowi2 = lax.broadcasted_iota(I32, (c, 2 * c), 0)
    coli2 = lax.broadcasted_iota(I32, (c, 2 * c), 1) & (c - 1)
    causal2 = rowi2 >= coli2
    strict2 = rowi2 > coli2
    decay = {p: jnp.where(causal2, jnp.exp(jnp.where(causal2, gcol[p] - grow2[p], 0.0)), 0.0) for p in pairs}
    kb = {p: kh[p] * beta[p] for p in pairs}
    egc = {p: jnp.exp(gcol[p]) for p in pairs}
    gram = {p: _mm_nt(jnp.concatenate([kb[p], qh[p]], axis=0), jnp.concatenate([kh[p], kh[p]], axis=0))
            for p in pairs}
    mat = {p: jnp.where(strict2, gram[p][:c] * decay[p], 0.0).astype(BF16) for p in pairs}
    qk = {p: jnp.where(causal, gram[p][c:, :c] * decay[p][:, :c], 0.0) for p in pairs}
    sol = {p: jnp.concatenate([head(v_ref, *p) * beta[p], kb[p] * egc[p]], axis=1) for p in pairs}
    levels = int(math.log2(c))
    zeros2 = jnp.zeros((c, 2 * c), BF16)
    for lvl in range(levels):
        hi = {p: sol[p].astype(BF16) for p in pairs}
        lo = {p: (sol[p] - hi[p].astype(F32)).astype(BF16) for p in pairs}
        if lvl < levels - 1:
            y = {p: jnp.dot(mat[p], jnp.concatenate([jnp.concatenate([hi[p], mat[p]], axis=1),
                                                     jnp.concatenate([lo[p], zeros2], axis=1)], axis=0),
                            preferred_element_type=F32) for p in pairs}
            mat = {p: y[p][:, 2 * dk:].astype(BF16) for p in pairs}
            upd = {p: y[p][:, :2 * dk] for p in pairs}
        else:
            upd = {p: jnp.dot(mat[p], jnp.concatenate([hi[p], lo[p]], axis=0), preferred_element_type=F32)
                   for p in pairs}
        sol = {p: (sol[p] - upd[p]) if lvl == 0 else (sol[p] + upd[p]) for p in pairs}
    g_last = {(j, h): gc_all[j][c - 1:c, h:h + 1] for j, h in pairs}
    wq = {p: jnp.concatenate([sol[p][:, dk:], qh[p] * egc[p]], axis=0) for p in pairs}
    k_dec_t = {p: (kh[p] * jnp.exp(g_last[p] - gcol[p])).T for p in pairs}
    d_last = {p: jnp.exp(g_last[p]) for p in pairs}

    s = [s_ref[h] for h in range(DN_HEADS)]
    for j in range(nsub):
        ws = [_mm(wq[j, h], s[h]) for h in range(DN_HEADS)]
        v_new = [sol[j, h][:, :dk] - ws[h][:c] for h in range(DN_HEADS)]
        o_ref[rows[j], :] = jnp.concatenate(
            [ws[h][c:] + _mm(qk[j, h], v_new[h]) for h in range(DN_HEADS)], axis=1)
        s = [s[h] * d_last[j, h] + _mm(k_dec_t[j, h], v_new[h]) for h in range(DN_HEADS)]
    for h in range(DN_HEADS):
        s_ref[h] = s[h]
    sfin_ref[0] = s_ref[...]


def _split3_dot_left(b01, a):
    a1 = a.astype(BF16)
    r1 = a - a1.astype(F32)
    a2 = r1.astype(BF16)
    a3 = (r1 - a2.astype(F32)).astype(BF16)
    out = jnp.dot(b01, a3, preferred_element_type=F32)
    out = out + jnp.dot(b01, a2, preferred_element_type=F32)
    return out + jnp.dot(b01, a1, preferred_element_type=F32)


def _delta_prompt(q, k, v, gate, nb):
    n = q.shape[0]
    t = n // nb
    c = DN_CHUNK
    nsub = DELTA_SUBCHUNKS
    rows = nsub * c
    nc = t // rows
    row = lambda b, i: (b * nc + i, 0)
    tril = jnp.tril(jnp.ones((c, c), F32)).astype(BF16)
    return pl.pallas_call(
        functools.partial(_delta_chunk_kernel, nsub=nsub),
        grid=(nb, nc),
        in_specs=[pl.BlockSpec((rows, DN_WIDTH), row),
                  pl.BlockSpec((rows, DN_WIDTH), row),
                  pl.BlockSpec((rows, DN_WIDTH), row),
                  pl.BlockSpec((rows, LANES), row),
                  pl.BlockSpec((c, c), lambda b, i: (0, 0))],
        out_specs=[pl.BlockSpec((rows, DN_WIDTH), row),
                   pl.BlockSpec((1, DN_HEADS, DN_HEAD_DIM, DN_HEAD_DIM), lambda b, i: (b, 0, 0, 0))],
        out_shape=[jax.ShapeDtypeStruct((n, DN_WIDTH), F32),
                   jax.ShapeDtypeStruct((nb, DN_HEADS, DN_HEAD_DIM, DN_HEAD_DIM), F32)],
        scratch_shapes=[pltpu.VMEM((DN_HEADS, DN_HEAD_DIM, DN_HEAD_DIM), F32)],
        compiler_params=_cparams(("arbitrary", "arbitrary")),
        name="delta_prompt",
    )(q, k, v, gate, tril)


def _delta_step_kernel(q_ref, k_ref, v_ref, gate_ref, s0_ref, o_ref, s_ref, kt_ref, qt_ref, gt_ref, *, nt, nb):
    dk = DN_HEAD_DIM
    p = pl.program_id(0)
    for t in range(nt):
        rs = slice(t * nb, (t + 1) * nb)
        gt_ref[...] = gate_ref[rs, :].T
        kt_ref[...] = k_ref[rs, :].T
        qt_ref[...] = (q_ref[rs, :] * (dk ** -0.5)).T
        vt = v_ref[rs, :].T
        src = s0_ref if t == 0 else s_ref
        o_heads = []
        for j in range(2):
            a = jnp.exp(gt_ref[pl.ds(2 * p + j, 1), :])
            beta = gt_ref[pl.ds(2 * p + j + DN_HEADS, 1), :]
            base = j * dk * dk

            def k_dot_s(d, acc, j=j, base=base, src=src):
                sd = src[pl.ds(pl.multiple_of(base + d * dk, dk), dk), :]
                return acc + kt_ref[pl.ds(j * dk + d, 1), :] * sd

            ks = lax.fori_loop(0, dk, k_dot_s, jnp.zeros((dk, nb), F32), unroll=4)
            delta = beta * (vt[j * dk:(j + 1) * dk, :] - a * ks)

            def update(d, acc, j=j, base=base, src=src, a=a, delta=delta):
                r0 = pl.multiple_of(base + d * dk, dk)
                sn = a * src[pl.ds(r0, dk), :] + kt_ref[pl.ds(j * dk + d, 1), :] * delta
                s_ref[pl.ds(r0, dk), :] = sn
                return acc + qt_ref[pl.ds(j * dk + d, 1), :] * sn

            o_heads.append(lax.fori_loop(0, dk, update, jnp.zeros((dk, nb), F32), unroll=4))
        o_ref[rs, :] = jnp.concatenate(o_heads, axis=0).T


def _delta_sample(q, k, v, gate, s0t, nb, nt):
    dk = DN_HEAD_DIM
    flat = dk * dk
    n = nt * nb
    kern = functools.partial(_delta_step_kernel, nt=nt, nb=nb)
    pair = lambda p: (0, p)
    return pl.pallas_call(
        kern,
        grid=(DN_HEADS // 2,),
        in_specs=[pl.BlockSpec((n, LANES), pair),
                  pl.BlockSpec((n, LANES), pair),
                  pl.BlockSpec((n, LANES), pair),
                  pl.BlockSpec((n, LANES), lambda p: (0, 0)),
                  pl.BlockSpec((2 * flat, nb), lambda p: (p, 0))],
        out_specs=[pl.BlockSpec((n, LANES), pair),
                   pl.BlockSpec((2 * flat, nb), lambda p: (p, 0))],
        out_shape=[jax.ShapeDtypeStruct((n, DN_WIDTH), F32),
                   jax.ShapeDtypeStruct((DN_HEADS * flat, nb), F32)],
        scratch_shapes=[pltpu.VMEM((LANES, nb), F32),
                        pltpu.VMEM((LANES, nb), F32),
                        pltpu.VMEM((LANES, nb), F32)],
        compiler_params=_cparams(("arbitrary",)),
        name="delta_sample",
    )(q, k, v, gate, s0t)


def _s5_kernel(u_ref, btre_ref, btim_ref, lam_ref, ctre_ref, ctim_ref, d_ref, h0_ref, y_ref, hfin_ref,
               bw_ref, c_ref, ab_ref, x_ref, h_ref, *, nb, tt):
    p2 = S5_FLAT

    @pl.when(pl.program_id(0) == 0)
    def _():
        lr = lam_ref[0:1, :]
        li = lam_ref[1:2, :]
        dt = jnp.exp(lam_ref[2:3, :])
        mag = jnp.exp(lr * dt)
        ab_re = mag * jnp.cos(li * dt)
        ab_im = mag * jnp.sin(li * dt)
        den = lr * lr + li * li
        nr = ab_re - 1.0
        ni = ab_im
        f_re = (nr * lr + ni * li) / den
        f_im = (ni * lr - nr * li) / den
        ab_ref[0:1, :] = ab_re
        ab_ref[1:2, :] = ab_im
        gpl = LANES // S5_STATE
        ch_g = lax.broadcasted_iota(I32, (S5_WIDTH, LANES), 0) // S5_GROUP_CH
        lane_g = lax.broadcasted_iota(I32, (S5_WIDTH, LANES), 1) // S5_STATE
        bre2 = jnp.concatenate([btre_ref[...]] * gpl, axis=1)
        bim2 = jnp.concatenate([btim_ref[...]] * gpl, axis=1)
        for j in range(p2 // LANES):
            cols = slice(j * LANES, (j + 1) * LANES)
            own = ch_g == gpl * j + lane_g
            bre = jnp.where(own, bre2, 0.0)
            bim = jnp.where(own, bim2, 0.0)
            bw_ref[:, cols] = (bre * f_re[:, cols] - bim * f_im[:, cols]).astype(BF16)
            bw_ref[:, p2 + j * LANES:p2 + (j + 1) * LANES] = (bim * f_re[:, cols] + bre * f_im[:, cols]).astype(BF16)
        cpl = LANES // S5_GROUP_CH
        st_g = lax.broadcasted_iota(I32, (p2, LANES), 0) // S5_STATE
        lane_cg = lax.broadcasted_iota(I32, (p2, LANES), 1) // S5_GROUP_CH
        for j in range(S5_WIDTH // LANES):
            cols = slice(j * LANES, (j + 1) * LANES)
            own = st_g == cpl * j + lane_cg
            c_ref[0:p2, cols] = jnp.where(own, ctre_ref[...], 0.0).astype(BF16)
            c_ref[p2:2 * p2, cols] = jnp.where(own, -ctim_ref[...], 0.0).astype(BF16)
        h_ref[...] = h0_ref[...]

    u = u_ref[...]
    cw = S5_WIDTH // S5_SUPER
    sw = S5_FLAT // S5_SUPER
    for part in (0, p2):
        for b in range(S5_SUPER):
            x_ref[:, part + b * sw:part + (b + 1) * sw] = jnp.dot(
                u[:, b * cw:(b + 1) * cw], bw_ref[b * cw:(b + 1) * cw, part + b * sw:part + (b + 1) * sw],
                preferred_element_type=F32)
    a_re = ab_ref[0:1, :]
    a_im = ab_ref[1:2, :]

    if nb == SUBLANES:
        wsl = p2 // S5_SCAN_SPLIT
        for sp in range(S5_SCAN_SPLIT):
            c0 = sp * wsl
            are = jnp.broadcast_to(a_re[:, c0:c0 + wsl], (nb, wsl))
            aim = jnp.broadcast_to(a_im[:, c0:c0 + wsl], (nb, wsl))

            def step(t, carry, c0=c0, are=are, aim=aim):
                hr, hi = carry
                r0 = pl.multiple_of(t * nb, nb)
                nr = are * hr - aim * hi + x_ref[pl.ds(r0, nb), c0:c0 + wsl]
                ni = are * hi + aim * hr + x_ref[pl.ds(r0, nb), p2 + c0:p2 + c0 + wsl]
                x_ref[pl.ds(r0, nb), c0:c0 + wsl] = nr
                x_ref[pl.ds(r0, nb), p2 + c0:p2 + c0 + wsl] = ni
                return nr, ni

            hr, hi = lax.fori_loop(0, tt, step, (h_ref[:, c0:c0 + wsl], h_ref[:, p2 + c0:p2 + c0 + wsl]),
                                   unroll=2)
            h_ref[:, c0:c0 + wsl] = hr
            h_ref[:, p2 + c0:p2 + c0 + wsl] = hi
    else:
        for t in range(tt):
            rs = slice(t * nb, (t + 1) * nb)
            hr = h_ref[:, 0:p2]
            hi = h_ref[:, p2:2 * p2]
            nr = a_re * hr - a_im * hi + x_ref[rs, 0:p2]
            ni = a_re * hi + a_im * hr + x_ref[rs, p2:2 * p2]
            h_ref[:, 0:p2] = nr
            h_ref[:, p2:2 * p2] = ni
            x_ref[rs, 0:p2] = nr
            x_ref[rs, p2:2 * p2] = ni

    for b in range(S5_SUPER):
        cols = slice(b * cw, (b + 1) * cw)
        y = None
        for part in (0, p2):
            rws = slice(part + b * sw, part + (b + 1) * sw)
            term = jnp.dot(x_ref[:, rws].astype(BF16), c_ref[rws, cols], preferred_element_type=F32)
            y = term if y is None else y + term
        y_ref[:, cols] = y + d_ref[:, cols] * u[:, cols].astype(F32)
    hfin_ref[...] = h_ref[...]


def _s5(u, params, h0, nb):
    btre, btim, lam, ctre, ctim, dvec = params
    n = u.shape[0]
    t = n // nb
    tt = min(ROW_TILE // nb, t)
    rows = tt * nb
    const = lambda i: (0, 0)
    kern = functools.partial(_s5_kernel, nb=nb, tt=tt)
    return pl.pallas_call(
        kern,
        grid=(t // tt,),
        in_specs=[pl.BlockSpec((rows, S5_WIDTH), lambda i: (i, 0)),
                  pl.BlockSpec((S5_WIDTH, S5_STATE), const),
                  pl.BlockSpec((S5_WIDTH, S5_STATE), const),
                  pl.BlockSpec((SUBLANES, S5_FLAT), const),
                  pl.BlockSpec((S5_FLAT, LANES), const),
                  pl.BlockSpec((S5_FLAT, LANES), const),
                  pl.BlockSpec((1, S5_WIDTH), const),
                  pl.BlockSpec((nb, 2 * S5_FLAT), const)],
        out_specs=[pl.BlockSpec((rows, S5_WIDTH), lambda i: (i, 0)),
                   pl.BlockSpec((nb, 2 * S5_FLAT), const)],
        out_shape=[jax.ShapeDtypeStruct((n, S5_WIDTH), F32),
                   jax.ShapeDtypeStruct((nb, 2 * S5_FLAT), F32)],
        scratch_shapes=[pltpu.VMEM((S5_WIDTH, 2 * S5_FLAT), BF16),
                        pltpu.VMEM((2 * S5_FLAT, S5_WIDTH), BF16),
                        pltpu.VMEM((SUBLANES, S5_FLAT), F32),
                        pltpu.VMEM((rows, 2 * S5_FLAT), F32),
                        pltpu.VMEM((nb, 2 * S5_FLAT), F32)],
        compiler_params=_cparams(("arbitrary",)),
        name="s5",
    )(u, btre, btim, lam, ctre, ctim, dvec, h0)


def _postmix_kernel(xp_ref, op_ref, zp_ref, ysp_ref, gap_ref, gbp_ref,
                    xs_ref, os_ref, zs_ref, yss_ref, gas_ref, gbs_ref, *rest, nblk_p, range_tok):
    carry_ref = rest[-1]

    @pl.when(pl.program_id(0) == 0)
    def _():
        carry_ref[...] = jnp.zeros_like(carry_ref)

    @pl.when(pl.program_id(0) < nblk_p)
    def _():
        _postmix_body(xp_ref, op_ref, zp_ref, ysp_ref, gap_ref, gbp_ref, *rest, range_tok=range_tok)

    @pl.when(pl.program_id(0) >= nblk_p)
    def _():
        _postmix_body(xs_ref, os_ref, zs_ref, yss_ref, gas_ref, gbs_ref, *rest, range_tok=range_tok)


def _postmix_body(x_ref, o_ref, z_ref, ys_ref, ga_ref, gb_ref, hw_ref, seg_ref, wa_ref, wglu_ref, wb_ref,
                  wo_ref, nf_ref, wr_ref, su_ref, x1_ref, hn_ref, bkt_ref, rank_ref, rw_ref, cnt_ref, carry_ref,
                  *, range_tok):
    o = o_ref[...]
    ms = jnp.dot((o * o).astype(BF16), seg_ref[...], preferred_element_type=F32) * (1.0 / DN_HEAD_DIM)
    on = o * lax.rsqrt(ms + RMS_EPS) * hw_ref[...]
    z = z_ref[...]
    oa = on * (z * jax.nn.sigmoid(z)).astype(F32)
    y_a = _mm(oa, wa_ref[...])
    ys = jax.nn.gelu(ys_ref[...])
    ys = ys * jax.nn.sigmoid(_mm(ys, wglu_ref[...]))
    y_b = _mm(ys, wb_ref[...])
    mixed = jax.nn.sigmoid(ga_ref[...]).astype(F32) * y_a + jax.nn.sigmoid(gb_ref[...]).astype(F32) * y_b
    x1 = x_ref[...] + _mm(mixed, wo_ref[...])
    x1_ref[...] = x1
    hn = x1 * lax.rsqrt(jnp.mean(x1 * x1, axis=-1, keepdims=True) + RMS_EPS) * nf_ref[...]
    _slab_store(hn_ref, hn)

    wr = wr_ref[...]
    w_hi = wr.astype(BF16)
    w_lo = (wr - w_hi.astype(F32)).astype(BF16)
    hn_hi = hn.astype(BF16)
    hn_lo = (hn - hn_hi.astype(F32)).astype(BF16)
    both = _mm_nt(jnp.concatenate([w_hi, w_lo], axis=0), hn_hi)
    logits = both[:ROUTER_ROWS] + both[ROUTER_ROWS:] + _mm_nt(w_hi, hn_lo)
    coarse = logits[N_EXPERTS:N_EXPERTS + MOE_GROUPS, :]
    cm = jnp.max(coarse, axis=0, keepdims=True)
    ce = jnp.exp(coarse - cm)
    pc = ce / jnp.sum(ce, axis=0, keepdims=True)
    p_sel = jnp.max(pc, axis=0, keepdims=True)
    gi = lax.broadcasted_iota(I32, pc.shape, 0)
    g_sel = jnp.min(jnp.where(pc == p_sel, gi, MOE_GROUPS), axis=0, keepdims=True)
    fine = jnp.zeros((EXPERTS_PER_GROUP, logits.shape[1]), F32)
    for g in range(MOE_GROUPS):
        fine = fine + jnp.where(g_sel == g, logits[g * EXPERTS_PER_GROUP:(g + 1) * EXPERTS_PER_GROUP, :], 0.0)
    fm = jnp.max(fine, axis=0, keepdims=True)
    fe = jnp.exp(fine - fm)
    pf = fe / jnp.sum(fe, axis=0, keepdims=True)
    ei = lax.broadcasted_iota(I32, pf.shape, 0)
    v1 = jnp.max(pf, axis=0, keepdims=True)
    i1 = jnp.min(jnp.where(pf == v1, ei, EXPERTS_PER_GROUP), axis=0, keepdims=True)
    rest = jnp.where(ei == i1, -1.0, pf)
    v2 = jnp.max(rest, axis=0, keepdims=True)
    i2 = jnp.min(jnp.where(rest == v2, ei, EXPERTS_PER_GROUP), axis=0, keepdims=True)
    tot = v1 + v2
    rw_ref[0:1, :] = v1 / tot * p_sel
    rw_ref[1:2, :] = v2 / tot * p_sel

    tt = logits.shape[1]
    tok = pl.program_id(0) * tt + lax.broadcasted_iota(I32, (1, tt), 1)
    ph = jnp.zeros((1, tt), I32)
    for r in range(1, MOE_PHASES):
        ph = ph + (tok >= r * range_tok).astype(I32)
    bsel = [ph * N_EXPERTS + g_sel * EXPERTS_PER_GROUP + ix for ix in (i1, i2)]
    bi = lax.broadcasted_iota(I32, (MOE_PHASES * N_EXPERTS, tt), 0)
    onehot = [(bi == b).astype(F32) for b in bsel]
    cnt = onehot[0] + onehot[1]
    before = carry_ref[:, 0:1] + jnp.dot(cnt.astype(BF16), su_ref[...], preferred_element_type=F32)
    for s in range(TOP_K):
        bkt_ref[s:s + 1, :] = bsel[s]
        rank_ref[s:s + 1, :] = jnp.sum(onehot[s] * before, axis=0, keepdims=True).astype(I32)
    carry_ref[...] = carry_ref[...] + jnp.sum(cnt, axis=1, keepdims=True)
    cnt_ref[...] = carry_ref[...]


def _postmix(prompt, sample, weights, nb):
    n_p = prompt[0].shape[0]
    n_s = sample[0].shape[0]
    t = n_p // nb
    tt = min(ROW_TILE, t, n_s)
    nt = t // tt
    nblk_p = n_p // tt
    nblk = nblk_p + n_s // tt
    n_total = n_p + n_s
    prow = lambda i: (jnp.minimum(i, nblk_p - 1), 0)
    pys = lambda i: (jnp.minimum(i, nblk_p - 1) % nt, jnp.minimum(i, nblk_p - 1) // nt)
    srow = lambda i: (jnp.maximum(i - nblk_p, 0), 0)
    const = lambda i: (0, 0)

    def stream_specs(row, ysmap):
        return [pl.BlockSpec((tt, D_MODEL), row),
                pl.BlockSpec((tt, DN_WIDTH), row),
                pl.BlockSpec((tt, DN_WIDTH), row),
                pl.BlockSpec((tt, S5_WIDTH), ysmap),
                pl.BlockSpec((tt, D_MODEL), row),
                pl.BlockSpec((tt, D_MODEL), row)]

    weight_specs = [pl.BlockSpec((1, DN_WIDTH), const),
                    pl.BlockSpec((DN_WIDTH, DN_WIDTH), const),
                    pl.BlockSpec((DN_WIDTH, D_MODEL), const),
                    pl.BlockSpec((S5_WIDTH, S5_WIDTH), const),
                    pl.BlockSpec((S5_WIDTH, D_MODEL), const),
                    pl.BlockSpec((D_MODEL, D_MODEL), const),
                    pl.BlockSpec((1, D_MODEL), const),
                    pl.BlockSpec((ROUTER_ROWS, D_MODEL), const),
                    pl.BlockSpec((tt, tt), const)]
    xp, op, zp, ysp, gap, gbp = prompt
    nbk = MOE_PHASES * N_EXPERTS
    earlier = jnp.triu(jnp.ones((tt, tt), F32), k=1).astype(BF16)
    return pl.pallas_call(
        functools.partial(_postmix_kernel, nblk_p=nblk_p, range_tok=n_total // MOE_PHASES),
        grid=(nblk,),
        in_specs=stream_specs(prow, pys) + stream_specs(srow, srow) + weight_specs,
        out_specs=[pl.BlockSpec((tt, D_MODEL), lambda i: (i, 0)),
                   pl.BlockSpec((tt * ROW_SLAB, LANES), lambda i: (i, 0)),
                   pl.BlockSpec((TOP_K, tt), lambda i: (0, i)),
                   pl.BlockSpec((TOP_K, tt), lambda i: (0, i)),
                   pl.BlockSpec((TOP_K, tt), lambda i: (0, i)),
                   pl.BlockSpec((nbk, LANES), const)],
        out_shape=[jax.ShapeDtypeStruct((n_total, D_MODEL), F32),
                   jax.ShapeDtypeStruct((n_total * ROW_SLAB, LANES), F32),
                   jax.ShapeDtypeStruct((TOP_K, n_total), I32),
                   jax.ShapeDtypeStruct((TOP_K, n_total), I32),
                   jax.ShapeDtypeStruct((TOP_K, n_total), F32),
                   jax.ShapeDtypeStruct((nbk, LANES), F32)],
        scratch_shapes=[pltpu.VMEM((nbk, LANES), F32)],
        compiler_params=_cparams(("arbitrary",)),
        name="postmix",
    )(xp, op, zp, ysp.reshape(t, nb * S5_WIDTH), gap, gbp, *sample, *weights, earlier)


def _wait_slabs(buf, sem):
    pltpu.make_async_copy(buf, buf, sem).wait()


def _moe_kernel(texp_ref, tph_ref, tsrc_ref, tnv_ref, tfirst_ref, tslot_ref, tnext_ref, otok_ref,
                hn_hbm, wu_hbm, wd_hbm, y_ref, hnv, xbuf, wu_buf, wd_buf, wub, wdb, sem, wsem):
    i = pl.program_id(0)
    tm = MOE_TILE
    rs = ROW_SLAB
    nv = tnv_ref[i]
    ph = tph_ref[i]
    range_rows = hnv.shape[0]

    def weight_copies(e, sl):
        return (pltpu.make_async_copy(wu_hbm.at[e], wu_buf.at[sl], wsem.at[sl]),
                pltpu.make_async_copy(wd_hbm.at[e], wd_buf.at[sl], wsem.at[sl]))

    @pl.when(i == 0)
    def _():
        for c in weight_copies(texp_ref[0], 0):
            c.start()

    @pl.when(jnp.logical_and(nv > 0, jnp.logical_or(i == 0, ph != tph_ref[jnp.maximum(i - 1, 0)])))
    def _():
        start = pl.multiple_of(ph * range_rows, rs)
        whole = pltpu.make_async_copy(hn_hbm.at[pl.ds(start, range_rows), :], hnv, sem)
        whole.start()
        whole.wait()

    for sl in range(2):
        @pl.when(jnp.logical_and(jnp.logical_and(nv > 0, tfirst_ref[i] == 1), tslot_ref[i] == sl))
        def _():
            for c in weight_copies(texp_ref[i], sl):
                c.wait()

            @pl.when(tnext_ref[i] >= 0)
            def _():
                for c in weight_copies(tnext_ref[i], 1 - sl):
                    c.start()

            wub[...] = wu_buf[sl].astype(BF16)
            wdb[...] = wd_buf[sl].astype(BF16)

    @pl.when(nv == 0)
    def _():
        y_ref[...] = jnp.zeros_like(y_ref)

    @pl.when(nv > 0)
    def _():
        src0 = tsrc_ref[i]
        for r in range(tm):
            tok8 = pl.multiple_of(otok_ref[src0 + r], rs)
            xbuf[pl.ds(r * rs, rs), :] = hnv[pl.ds(tok8, rs), :]
        x = _slab_load(xbuf, tm).astype(BF16)
        hu = jnp.dot(x, wub[...], preferred_element_type=F32)
        gate = hu[:, :EXPERT_FF]
        up = hu[:, EXPERT_FF:]
        act = gate * jax.nn.sigmoid(gate) * up
        _slab_store(y_ref, jnp.dot(act.astype(BF16), wdb[...], preferred_element_type=F32))


def _moe(hn, w_up, w_down, plan):
    ntiles = plan[0].shape[0]
    grid_spec = pltpu.PrefetchScalarGridSpec(
        num_scalar_prefetch=len(plan),
        grid=(ntiles,),
        in_specs=[pl.BlockSpec(memory_space=pl.ANY),
                  pl.BlockSpec(memory_space=pl.ANY),
                  pl.BlockSpec(memory_space=pl.ANY)],
        out_specs=pl.BlockSpec((MOE_TILE * ROW_SLAB, LANES), lambda i, *_: (i, 0)),
        scratch_shapes=[pltpu.VMEM((hn.shape[0] // MOE_PHASES, LANES), F32),
                        pltpu.VMEM((MOE_TILE * ROW_SLAB, LANES), F32),
                        pltpu.VMEM((2, D_MODEL, 2 * EXPERT_FF), F32),
                        pltpu.VMEM((2, EXPERT_FF, D_MODEL), F32),
                        pltpu.VMEM((D_MODEL, 2 * EXPERT_FF), BF16),
                        pltpu.VMEM((EXPERT_FF, D_MODEL), BF16),
                        pltpu.SemaphoreType.DMA,
                        pltpu.SemaphoreType.DMA((2,))])
    return pl.pallas_call(
        _moe_kernel,
        grid_spec=grid_spec,
        out_shape=jax.ShapeDtypeStruct((ntiles * MOE_TILE * ROW_SLAB, LANES), F32),
        compiler_params=_cparams(("arbitrary",)),
        name="moe",
    )(*plan, hn, w_up, w_down)


def _combine_kernel(pos_ref, x1_ref, ys_hbm, w_ref, nw_ref, outp_ref, outs_ref,
                    ybuf0, ybuf1, sem, *, nblk_p, n_tok):
    i = pl.program_id(0)
    nsteps = pl.num_programs(0)
    tt = x1_ref.shape[0]
    rs = ROW_SLAB
    slot = lax.rem(i, 2)
    ybuf = (ybuf0, ybuf1)

    def start_gather(step, sl):
        base = step * tt
        for r in range(tt * TOP_K):
            j, s = divmod(r, TOP_K)
            p8 = pl.multiple_of(pos_ref[s * n_tok + base + j], rs)
            pltpu.make_async_copy(ys_hbm.at[pl.ds(p8, rs), :], ybuf[sl].at[pl.ds((s * tt + j) * rs, rs), :],
                                  sem.at[sl]).start(priority=r % DMA_QUEUES)

    @pl.when(i == 0)
    def _():
        start_gather(0, 0)

    for sl in range(2):
        @pl.when(slot == sl)
        def _():
            _wait_slabs(ybuf[sl], sem.at[sl])
            start_gather(jnp.minimum(i + 1, nsteps - 1), 1 - sl)
            w = w_ref[...]
            y0 = _slab_load(ybuf[sl], tt, 0)
            y1 = _slab_load(ybuf[sl], tt, tt * rs)
            x = x1_ref[...] + w[:, 0:1] * y0 + w[:, 1:2] * y1
            res = x * lax.rsqrt(jnp.mean(x * x, axis=-1, keepdims=True) + RMS_EPS) * nw_ref[...]

            @pl.when(i < nblk_p)
            def _():
                outp_ref[...] = res

            @pl.when(i >= nblk_p)
            def _():
                outs_ref[...] = res

        @pl.when(jnp.logical_and(slot == sl, i == nsteps - 1))
        def _():
            _wait_slabs(ybuf[1 - sl], sem.at[1 - sl])


def _combine(x1, ysorted, pos8, wtok, nw, n_p):
    n = x1.shape[0]
    tt = math.gcd(math.gcd(n_p, n - n_p), COMBINE_TILE)
    nblk_p = n_p // tt
    grid_spec = pltpu.PrefetchScalarGridSpec(
        num_scalar_prefetch=1,
        grid=(n // tt,),
        in_specs=[pl.BlockSpec((tt, D_MODEL), lambda i, *_: (i, 0)),
                  pl.BlockSpec(memory_space=pl.ANY),
                  pl.BlockSpec((tt, TOP_K), lambda i, *_: (i, 0)),
                  pl.BlockSpec((1, D_MODEL), lambda i, *_: (0, 0))],
        out_specs=[pl.BlockSpec((tt, D_MODEL), lambda i, *_: (jnp.minimum(i, nblk_p - 1), 0)),
                   pl.BlockSpec((tt, D_MODEL), lambda i, *_: (jnp.maximum(i - nblk_p, 0), 0))],
        scratch_shapes=[pltpu.VMEM((tt * TOP_K * ROW_SLAB, LANES), F32),
                        pltpu.VMEM((tt * TOP_K * ROW_SLAB, LANES), F32),
                        pltpu.SemaphoreType.DMA((2,))])
    return pl.pallas_call(
        functools.partial(_combine_kernel, nblk_p=nblk_p, n_tok=n),
        grid_spec=grid_spec,
        out_shape=[jax.ShapeDtypeStruct((n_p, D_MODEL), F32),
                   jax.ShapeDtypeStruct((n - n_p, D_MODEL), F32)],
        compiler_params=_cparams(("arbitrary",)),
        name="combine",
    )(pos8, x1, ysorted, wtok, nw)


def _route_plan(bkt, rank, cnt, n_tok):
    tm = MOE_TILE
    n_assign = n_tok * TOP_K
    nbk = MOE_PHASES * N_EXPERTS
    ntiles = n_assign // tm + nbk
    range_tok = n_tok // MOE_PHASES
    b_flat = bkt.T.reshape(n_assign)
    order = jnp.argsort(b_flat, stable=True).astype(I32)
    counts = cnt[:, 0].astype(I32)
    cstart = jnp.cumsum(counts) - counts
    tiles_b = (counts + tm - 1) // tm
    tend = jnp.cumsum(tiles_b)
    tstart = tend - tiles_b
    tile_id = jnp.arange(ntiles, dtype=I32)
    tbk = jnp.minimum(jnp.sum((tile_id[:, None] >= tend[None, :]).astype(I32), axis=1), nbk - 1)
    onehot = (tbk[:, None] == jnp.arange(nbk, dtype=I32)[None, :]).astype(I32)
    pick = lambda v: jnp.sum(onehot * v[None, :], axis=1)
    done = (tile_id - pick(tstart)) * tm
    tnv = jnp.where(tile_id < tend[-1], jnp.clip(pick(counts) - done, 0, tm), 0)
    tsrc = jnp.where(tnv > 0, pick(cstart) + done, 0)
    texp = tbk % N_EXPERTS
    tph = tbk // N_EXPERTS
    nonempty = counts > 0
    bslot = (jnp.cumsum(nonempty.astype(I32)) - 1) % 2
    bidx = jnp.where(nonempty, jnp.arange(nbk, dtype=I32), nbk)
    nxt = jnp.concatenate([lax.cummin(bidx[::-1])[::-1][1:], jnp.full((1,), nbk, I32)])
    bnext = jnp.where(nxt < nbk, nxt % N_EXPERTS, -1)
    tfirst = jnp.logical_and(tnv > 0, done == 0).astype(I32)
    tslot = pick(bslot)
    tnext = pick(bnext)
    otok8 = jnp.concatenate([((order // TOP_K) % range_tok) * ROW_SLAB, jnp.zeros((tm,), I32)])
    plan = tuple(a.astype(I32) for a in (texp, tph, tsrc, tnv, tfirst, tslot, tnext, otok8))
    first = jnp.sum((bkt[:, :, None] == jnp.arange(nbk, dtype=I32)[None, None, :]).astype(I32)
                    * (tstart * tm)[None, None, :], axis=2)
    pos8 = ((first + rank) * ROW_SLAB).reshape(n_assign)
    return plan, pos8.astype(I32)


def _block_diag(m):
    g, a, b = m.shape
    eye = jnp.eye(g, dtype=m.dtype)
    return (eye[:, None, :, None] * m[:, :, None, :]).reshape(g * a, g * b)


def kernel(x_prompt, x_sample, state_conv, state_delta, state_ssm_re, state_ssm_im, norm_mix_w, w_in, conv_w, a_log, dt_bias, head_norm_w, w_a_up, s5_lambda_re, s5_lambda_im, s5_log_step, s5_b_re, s5_b_im, s5_c_re, s5_c_im, s5_d, w_glu, w_b_up, w_o, norm_ffn_w, w_router_coarse, w_router_fine, w_expert_up, w_expert_down, norm_final_w):
    bp, tp, _ = x_prompt.shape
    bs, ts, _ = x_sample.shape
    n_p = bp * tp
    n_s = bs * ts
    n_tok = n_p + n_s
    l = 0

    w = w_in[l].astype(BF16)
    c_ab = W1_COLS + 2 * DN_HEADS
    w_parts = (w[:, :W1_COLS], w[:, c_ab:],
               jnp.concatenate([w[:, W1_COLS:c_ab], jnp.zeros((D_MODEL, LANES - 2 * DN_HEADS), BF16)], axis=1))
    nw_mix = norm_mix_w[l].reshape(1, D_MODEL)
    pad8 = lambda v: jnp.concatenate([v, jnp.zeros((LANES - DN_HEADS,), F32)]).reshape(1, LANES)
    gate_p = jnp.concatenate([pad8(a_log[l]), pad8(dt_bias[l])], axis=0)
    seg = _block_diag(jnp.ones((DN_HEADS, DN_HEAD_DIM, DN_HEAD_DIM), BF16))
    chan_rows = lambda b: jnp.swapaxes(b, 1, 2).reshape(S5_WIDTH, S5_STATE)
    state_rows = lambda c: jnp.tile(jnp.swapaxes(c, 1, 2).reshape(S5_FLAT, S5_GROUP_CH),
                                    (1, LANES // S5_GROUP_CH))
    lam = jnp.concatenate([s5_lambda_re[l].reshape(1, S5_FLAT), s5_lambda_im[l].reshape(1, S5_FLAT),
                           jnp.repeat(s5_log_step[l], S5_STATE).reshape(1, S5_FLAT),
                           jnp.zeros((SUBLANES - 3, S5_FLAT), F32)], axis=0)
    s5_params = (chan_rows(s5_b_re[l]), chan_rows(s5_b_im[l]), lam,
                 state_rows(s5_c_re[l]), state_rows(s5_c_im[l]), s5_d[l].reshape(1, S5_WIDTH))
    hw = jnp.tile(head_norm_w[l], DN_HEADS).reshape(1, DN_WIDTH)
    wr = jnp.concatenate([w_router_fine[l].T, w_router_coarse[l].T,
                          jnp.zeros((ROUTER_ROWS - N_EXPERTS - MOE_GROUPS, D_MODEL), F32)], axis=0)
    pm_weights = (hw, seg, w_a_up[l].astype(BF16), w_glu[l].astype(BF16), w_b_up[l].astype(BF16),
                  w_o[l].astype(BF16), norm_ffn_w[l].reshape(1, D_MODEL), wr)

    xp2 = x_prompt.reshape(n_p, D_MODEL)
    q_p, k_p, v_p, gates_p, conv_p, z_p, u_p, ga_p, gb_p = _inprep(
        xp2, nw_mix, w_parts, jnp.zeros((bp, SUBLANES, QKV_DIM), F32), conv_w[l], gate_p, seg, bp, 1)
    o_p, delta_p = _delta_prompt(q_p, k_p, v_p, gates_p, bp)
    ys_p, h_p = _s5(u_p, s5_params, jnp.zeros((bp, 2 * S5_FLAT), F32), bp)

    xs2 = jnp.swapaxes(x_sample, 0, 1).reshape(n_s, D_MODEL)
    cinit_s = jnp.swapaxes(state_conv[l], 0, 1).reshape(1, (CONV_W - 1) * bs, QKV_DIM)
    q_s, k_s, v_s, gate_s, conv_s, z_s, u_s, ga_s, gb_s = _inprep(
        xs2, nw_mix, w_parts, cinit_s, conv_w[l], gate_p, seg, 1, bs)
    s0t = jnp.transpose(state_delta[l], (1, 2, 3, 0)).reshape(DN_HEADS * DN_HEAD_DIM * DN_HEAD_DIM, bs)
    o_s, delta_st = _delta_sample(q_s, k_s, v_s, gate_s, s0t, bs, ts)
    delta_s = jnp.transpose(delta_st.reshape(DN_HEADS, DN_HEAD_DIM, DN_HEAD_DIM, bs), (3, 0, 1, 2))
    h0_s = jnp.concatenate([state_ssm_re[l].reshape(bs, S5_FLAT), state_ssm_im[l].reshape(bs, S5_FLAT)], axis=1)
    ys_s, h_s = _s5(u_s, s5_params, h0_s, bs)
    x1, hn, bkt, rank, rw, cnt = _postmix((xp2, o_p, z_p, ys_p, ga_p, gb_p), (xs2, o_s, z_s, ys_s, ga_s, gb_s),
                                          pm_weights, bp)

    plan, pos8 = _route_plan(bkt, rank, cnt, n_tok)
    ysorted = _moe(hn, w_expert_up[l], w_expert_down[l], plan)
    y_p, y_s = _combine(x1, ysorted, pos8, rw.T, norm_final_w.reshape(1, D_MODEL), n_p)

    y_prompt = y_p.reshape(bp, tp, D_MODEL)
    y_sample = jnp.swapaxes(y_s.reshape(ts, bs, D_MODEL), 0, 1)
    conv_sample = jnp.swapaxes(conv_s.reshape(CONV_W - 1, bs, QKV_DIM), 0, 1)
    return (y_prompt, y_sample,
            conv_p[None], delta_p[None],
            h_p[:, :S5_FLAT].reshape(1, bp, S5_GROUPS, S5_STATE), h_p[:, S5_FLAT:].reshape(1, bp, S5_GROUPS, S5_STATE),
            conv_sample[None], delta_s[None],
            h_s[:, :S5_FLAT].reshape(1, bs, S5_GROUPS, S5_STATE), h_s[:, S5_FLAT:].reshape(1, bs, S5_GROUPS, S5_STATE))
```

```python
import functools
import math

import jax
import jax.numpy as jnp
import numpy as np
from jax import lax
from jax.experimental import pallas as pl
from jax.experimental.pallas import tpu as pltpu

F32 = jnp.float32
BF16 = jnp.bfloat16
I32 = jnp.int32

D_MODEL = 1024
DN_HEADS = 8
DN_HEAD_DIM = 64
DN_WIDTH = DN_HEADS * DN_HEAD_DIM
QKV_DIM = 3 * DN_WIDTH
CONV_W = 4
DN_CHUNK = 64
S5_GROUP_CH = 16
S5_WIDTH = D_MODEL // 2
S5_GROUPS = S5_WIDTH // S5_GROUP_CH
S5_STATE = 64
S5_FLAT = S5_GROUPS * S5_STATE
MOE_GROUPS = 4
EXPERTS_PER_GROUP = 8
N_EXPERTS = MOE_GROUPS * EXPERTS_PER_GROUP
TOP_K = 2
EXPERT_FF = 256
RMS_EPS = 1e-6
L2_EPS = 1e-6

LANES = 128
SUBLANES = 8
VMEM_LIMIT = 56 * 1024 * 1024

W1_COLS = QKV_DIM + DN_WIDTH
W2_COLS = S5_WIDTH + 2 * D_MODEL

ROW_TILE = 512
INPREP_PARTS = 2
MOE_TILE = 256
MOE_PHASES = 2
COMBINE_TILE = 256
DMA_QUEUES = 2
DELTA_SUBCHUNKS = 4
S5_SUPER = 2
S5_SCAN_SPLIT = 2
ROUTER_ROWS = 40


def _mm(a, b):
    return jnp.dot(a.astype(BF16), b.astype(BF16), preferred_element_type=F32)


def _mm_nt(a, b):
    return lax.dot_general(a.astype(BF16), b.astype(BF16), (((1,), (1,)), ((), ())),
                           preferred_element_type=F32)


def _split3_dot(a, b01):
    a1 = a.astype(BF16)
    r1 = a - a1.astype(F32)
    a2 = r1.astype(BF16)
    a3 = (r1 - a2.astype(F32)).astype(BF16)
    out = jnp.dot(a3, b01, preferred_element_type=F32)
    out = out + jnp.dot(a2, b01, preferred_element_type=F32)
    return out + jnp.dot(a1, b01, preferred_element_type=F32)


def _cparams(sem):
    return pltpu.CompilerParams(dimension_semantics=sem, vmem_limit_bytes=VMEM_LIMIT)


ROW_SLAB = D_MODEL // LANES


def _slab_load(ref, rows, first=0, pitch=ROW_SLAB):
    return jnp.concatenate([ref[pl.ds(first + j, rows, stride=pitch), :] for j in range(ROW_SLAB)], axis=1)


def _slab_store(ref, x):
    for j in range(ROW_SLAB):
        ref[pl.ds(j, x.shape[0], stride=ROW_SLAB), :] = x[:, j * LANES:(j + 1) * LANES]


def _softplus(x):
    return jnp.maximum(x, 0.0) + jnp.log1p(jnp.exp(-jnp.abs(x)))


def _inprep_kernel(x_ref, nw_ref, w1_ref, w2_ref, wab_ref, cinit_ref, cw_ref, gp_ref, seg_ref,
                   q_ref, k_ref, v_ref, gate_ref, cnew_ref, z_ref, u_ref, ga_ref, gb_ref, xp_ref,
                   *, shift, rc, rows):
    @pl.when(pl.program_id(1) == 0)
    def _():
        xp_ref[0:rc, :] = cinit_ref[0]

    seg = seg_ref[...]
    pr = rows // INPREP_PARTS
    for part in range(INPREP_PARTS):
        rs = slice(part * pr, (part + 1) * pr)
        x = x_ref[rs, :]
        h = x * lax.rsqrt(jnp.mean(x * x, axis=-1, keepdims=True) + RMS_EPS) * nw_ref[...]
        hb = h.astype(BF16)

        def proj(w_ref, lo, hi, hb=hb):
            return jnp.dot(hb, w_ref[:, lo:hi], preferred_element_type=F32)

        xp_ref[rc + part * pr:rc + (part + 1) * pr, :] = proj(w1_ref, 0, QKV_DIM)
        ab = proj(wab_ref, 0, LANES)
        z_ref[rs, :] = proj(w1_ref, QKV_DIM, W1_COLS).astype(z_ref.dtype)
        u_ref[rs, :] = proj(w2_ref, 0, S5_WIDTH).astype(u_ref.dtype)
        ga_ref[rs, :] = proj(w2_ref, S5_WIDTH, S5_WIDTH + D_MODEL).astype(ga_ref.dtype)
        gb_ref[rs, :] = proj(w2_ref, S5_WIDTH + D_MODEL, W2_COLS).astype(gb_ref.dtype)
        acc = None
        for i in range(CONV_W):
            lo = rc + part * pr + (i - (CONV_W - 1)) * shift
            term = xp_ref[lo:lo + pr, :] * cw_ref[i:i + 1, :]
            acc = term if acc is None else acc + term
        y = acc * jax.nn.sigmoid(acc)
        q = y[:, 0:DN_WIDTH]
        k = y[:, DN_WIDTH:2 * DN_WIDTH]
        q_ref[rs, :] = q * lax.rsqrt(jnp.dot((q * q).astype(BF16), seg, preferred_element_type=F32) + L2_EPS)
        k_ref[rs, :] = k * lax.rsqrt(jnp.dot((k * k).astype(BF16), seg, preferred_element_type=F32) + L2_EPS)
        v_ref[rs, :] = y[:, 2 * DN_WIDTH:]
        g = -jnp.exp(gp_ref[0:1, :]) * _softplus(ab + gp_ref[1:2, :])
        beta = jax.nn.sigmoid(ab)
        lane = lax.broadcasted_iota(I32, ab.shape, 1)
        gate_ref[rs, :] = jnp.where(lane < DN_HEADS, g, beta)

    keep = (CONV_W - 1) * shift
    cnew_ref[0] = xp_ref[rc + rows - keep:rc + rows, :]
    xp_ref[0:rc, :] = xp_ref[rows:rows + rc, :]


def _inprep(x2d, nw, w_parts, cinit, conv_w, gate_p, seg, nb, shift):
    n = x2d.shape[0]
    r = n // nb
    rows = min(ROW_TILE, r)
    nt = r // rows
    rc = cinit.shape[1]
    keep = (CONV_W - 1) * shift
    row = lambda b, i: (b * nt + i, 0)
    const = lambda b, i: (0, 0)
    kern = functools.partial(_inprep_kernel, shift=shift, rc=rc, rows=rows)
    outs = pl.pallas_call(
        kern,
        grid=(nb, nt),
        in_specs=[pl.BlockSpec((rows, D_MODEL), row),
                  pl.BlockSpec((1, D_MODEL), const),
                  pl.BlockSpec((D_MODEL, W1_COLS), const),
                  pl.BlockSpec((D_MODEL, W2_COLS), const),
                  pl.BlockSpec((D_MODEL, LANES), const),
                  pl.BlockSpec((1, rc, QKV_DIM), lambda b, i: (b, 0, 0)),
                  pl.BlockSpec((CONV_W, QKV_DIM), const),
                  pl.BlockSpec((2, LANES), const),
                  pl.BlockSpec((DN_WIDTH, DN_WIDTH), const)],
        out_specs=[pl.BlockSpec((rows, DN_WIDTH), row),
                   pl.BlockSpec((rows, DN_WIDTH), row),
                   pl.BlockSpec((rows, DN_WIDTH), row),
                   pl.BlockSpec((rows, LANES), row),
                   pl.BlockSpec((1, keep, QKV_DIM), lambda b, i: (b, 0, 0)),
                   pl.BlockSpec((rows, DN_WIDTH), row),
                   pl.BlockSpec((rows, S5_WIDTH), lambda b, i: (i, b)),
                   pl.BlockSpec((rows, D_MODEL), row),
                   pl.BlockSpec((rows, D_MODEL), row)],
        out_shape=[jax.ShapeDtypeStruct((n, DN_WIDTH), F32),
                   jax.ShapeDtypeStruct((n, DN_WIDTH), F32),
                   jax.ShapeDtypeStruct((n, DN_WIDTH), F32),
                   jax.ShapeDtypeStruct((n, LANES), F32),
                   jax.ShapeDtypeStruct((nb, keep, QKV_DIM), F32),
                   jax.ShapeDtypeStruct((n, DN_WIDTH), BF16),
                   jax.ShapeDtypeStruct((r, nb * S5_WIDTH), BF16),
                   jax.ShapeDtypeStruct((n, D_MODEL), BF16),
                   jax.ShapeDtypeStruct((n, D_MODEL), BF16)],
        scratch_shapes=[pltpu.VMEM((rc + rows, QKV_DIM), F32)],
        compiler_params=_cparams(("arbitrary", "arbitrary")),
        name="inprep",
    )(x2d, nw, *w_parts, cinit, conv_w, gate_p, seg)
    q, k, v, gate, cnew, z, u, ga, gb = outs
    return q, k, v, gate, cnew, z, u.reshape(r * nb, S5_WIDTH), ga, gb


def _delta_chunk_kernel(q_ref, k_ref, v_ref, gate_ref, tril_ref, o_ref, sfin_ref, s_ref, *, nsub):
    c = DN_CHUNK
    dk = DN_HEAD_DIM

    @pl.when(pl.program_id(1) == 0)
    def _():
        s_ref[...] = jnp.zeros_like(s_ref)

    rowi2 = lax.broadcasted_iota(I32, (c, 2 * c), 0)
    lane2 = lax.broadcasted_iota(I32, (c, 2 * c), 1)
    coli2 = lane2 & (c - 1)
    causal2 = rowi2 >= coli2
    strict2 = rowi2 > coli2
    low = lane2 < dk
    tril = tril_ref[...]
    pairs = [(j, h) for j in range(nsub) for h in range(DN_HEADS)]
    units = [(j, pr) for j in range(nsub) for pr in range(DN_HEADS // 2)]
    rows = [slice(j * c, (j + 1) * c) for j in range(nsub)]
    gate = [gate_ref[rows[j], :] for j in range(nsub)]
    gc_all = [_split3_dot_left(tril, gate[j]) for j in range(nsub)]
    gc_t = [gc_all[j].T for j in range(nsub)]

    def home(h, x, other=0.0):
        return jnp.where(low, x, other) if h % 2 == 0 else jnp.where(low, other, x)

    def block(ref, j, pr):
        return ref[rows[j], pr * LANES:(pr + 1) * LANES]

    gfull = {(j, h): jnp.broadcast_to(gc_all[j][:, h:h + 1], (c, 2 * c)) for j, h in pairs}
    g2 = {(j, pr): jnp.where(low, gfull[j, 2 * pr], gfull[j, 2 * pr + 1]) for j, pr in units}
    b2 = {(j, pr): jnp.where(low, gate[j][:, DN_HEADS + 2 * pr:DN_HEADS + 2 * pr + 1],
                             gate[j][:, DN_HEADS + 2 * pr + 1:DN_HEADS + 2 * pr + 2]) for j, pr in units}
    kp = {u: block(k_ref, *u) for u in units}
    qp = {u: block(q_ref, *u) * (dk ** -0.5) for u in units}
    egc2 = {u: jnp.exp(g2[u]) for u in units}
    kb2 = {u: kp[u] * b2[u] for u in units}
    vb2 = {u: block(v_ref, *u) * b2[u] for u in units}
    kw2s = {u: pltpu.roll(kb2[u] * egc2[u], dk, axis=1) for u in units}
    qd2 = {u: qp[u] * egc2[u] for u in units}
    glast2 = {u: g2[u][c - 1:c, :] for u in units}
    kdec_t2 = {u: (kp[u] * jnp.exp(glast2[u] - g2[u])).T for u in units}
    dlast2 = {u: jnp.exp(glast2[u]) for u in units}
    kk = {u: jnp.concatenate([kp[u], kp[u]], axis=0) for u in units}

    grow2 = {(j, h): jnp.concatenate([gc_t[j][h:h + 1, :], gc_t[j][h:h + 1, :]], axis=1) for j, h in pairs}
    decay = {p: jnp.where(causal2, jnp.exp(jnp.where(causal2, gfull[p] - grow2[p], 0.0)), 0.0) for p in pairs}
    gram = {(j, h): _mm_nt(jnp.concatenate([home(h, kb2[j, h // 2]), home(h, qp[j, h // 2])], axis=0), kk[j, h // 2])
            for j, h in pairs}
    mat = {p: jnp.where(strict2, gram[p][:c] * decay[p], 0.0).astype(BF16) for p in pairs}
    qk = {p: jnp.where(causal2, gram[p][c:] * decay[p], 0.0)[:, :c] for p in pairs}
    sol = {(j, h): home(h, vb2[j, h // 2], kw2s[j, h // 2]) for j, h in pairs}
    levels = int(math.log2(c))
    zeros2 = jnp.zeros((c, 2 * c), BF16)
    for lvl in range(levels):
        hi = {p: sol[p].astype(BF16) for p in pairs}
        lo = {p: (sol[p] - hi[p].astype(F32)).astype(BF16) for p in pairs}
        if lvl < levels - 1:
            y = {p: jnp.dot(mat[p], jnp.concatenate([jnp.concatenate([hi[p], mat[p]], axis=1),
                                                     jnp.concatenate([lo[p], zeros2], axis=1)], axis=0),
                            preferred_element_type=F32) for p in pairs}
            mat = {p: y[p][:, 2 * dk:].astype(BF16) for p in pairs}
            upd = {p: y[p][:, :2 * dk] for p in pairs}
        else:
            upd = {p: jnp.dot(mat[p], jnp.concatenate([hi[p], lo[p]], axis=0), preferred_element_type=F32)
                   for p in pairs}
        sol = {p: (sol[p] - upd[p]) if lvl == 0 else (sol[p] + upd[p]) for p in pairs}
    wq = {(j, h): jnp.concatenate([home(h, 0.0, sol[j, h]), home(h, qd2[j, h // 2])], axis=0) for j, h in pairs}
    heads = range(DN_HEADS)
    kdec_t = {(j, h): kdec_t2[j, h // 2][(h % 2) * dk:(h % 2 + 1) * dk, :] for j, h in pairs}

    s = [s_ref[h] for h in heads]
    for j in range(nsub):
        ws = [_mm(wq[j, h], jnp.concatenate([s[h], s[h]], axis=0)) for h in heads]
        v_new = [sol[j, h] - ws[h][:c] for h in heads]
        o_h = [ws[h][c:] + _mm(qk[j, h], v_new[h]) for h in heads]
        for pr in range(DN_HEADS // 2):
            o_ref[rows[j], pr * LANES:(pr + 1) * LANES] = jnp.where(low, o_h[2 * pr], o_h[2 * pr + 1])
        s = [home(h, s[h] * dlast2[j, h // 2] + _mm(kdec_t[j, h], v_new[h])) for h in heads]
    for h in heads:
        s_ref[h] = s[h]

    @pl.when(pl.program_id(1) == pl.num_programs(1) - 1)
    def _():
        for h in heads:
            sfin_ref[0, h] = s[h][:, (h % 2) * dk:(h % 2 + 1) * dk]


def _split3_dot_left(b01, a):
    a1 = a.astype(BF16)
    r1 = a - a1.astype(F32)
    a2 = r1.astype(BF16)
    a3 = (r1 - a2.astype(F32)).astype(BF16)
    out = jnp.dot(b01, a3, preferred_element_type=F32)
    out = out + jnp.dot(b01, a2, preferred_element_type=F32)
    return out + jnp.dot(b01, a1, preferred_element_type=F32)


def _delta_prompt(q, k, v, gate, nb):
    n = q.shape[0]
    t = n // nb
    c = DN_CHUNK
    nsub = DELTA_SUBCHUNKS
    rows = nsub * c
    nc = t // rows
    row = lambda b, i: (b * nc + i, 0)
    tril = jnp.tril(jnp.ones((c, c), F32)).astype(BF16)
    return pl.pallas_call(
        functools.partial(_delta_chunk_kernel, nsub=nsub),
        grid=(nb, nc),
        in_specs=[pl.BlockSpec((rows, DN_WIDTH), row),
                  pl.BlockSpec((rows, DN_WIDTH), row),
                  pl.BlockSpec((rows, DN_WIDTH), row),
                  pl.BlockSpec((rows, LANES), row),
                  pl.BlockSpec((c, c), lambda b, i: (0, 0))],
        out_specs=[pl.BlockSpec((rows, DN_WIDTH), row),
                   pl.BlockSpec((1, DN_HEADS, DN_HEAD_DIM, DN_HEAD_DIM), lambda b, i: (b, 0, 0, 0))],
        out_shape=[jax.ShapeDtypeStruct((n, DN_WIDTH), F32),
                   jax.ShapeDtypeStruct((nb, DN_HEADS, DN_HEAD_DIM, DN_HEAD_DIM), F32)],
        scratch_shapes=[pltpu.VMEM((DN_HEADS, DN_HEAD_DIM, 2 * DN_HEAD_DIM), F32)],
        compiler_params=_cparams(("arbitrary", "arbitrary")),
        name="delta_prompt",
    )(q, k, v, gate, tril)


def _delta_step_kernel(q_ref, k_ref, v_ref, gate_ref, s0_ref, o_ref, s_ref, kt_ref, qt_ref, gt_ref, *, nt, nb):
    dk = DN_HEAD_DIM
    p = pl.program_id(0)
    for t in range(nt):
        rs = slice(t * nb, (t + 1) * nb)
        gt_ref[...] = gate_ref[rs, :].T
        kt_ref[...] = k_ref[rs, :].T
        qt_ref[...] = (q_ref[rs, :] * (dk ** -0.5)).T
        vt = v_ref[rs, :].T
        src = s0_ref if t == 0 else s_ref
        o_heads = []
        for j in range(2):
            a = jnp.exp(gt_ref[pl.ds(2 * p + j, 1), :])
            beta = gt_ref[pl.ds(2 * p + j + DN_HEADS, 1), :]
            base = j * dk * dk

            def k_dot_s(d, acc, j=j, base=base, src=src):
                sd = src[pl.ds(pl.multiple_of(base + d * dk, dk), dk), :]
                return acc + kt_ref[pl.ds(j * dk + d, 1), :] * sd

            ks = lax.fori_loop(0, dk, k_dot_s, jnp.zeros((dk, nb), F32), unroll=4)
            delta = beta * (vt[j * dk:(j + 1) * dk, :] - a * ks)

            def update(d, acc, j=j, base=base, src=src, a=a, delta=delta):
                r0 = pl.multiple_of(base + d * dk, dk)
                sn = a * src[pl.ds(r0, dk), :] + kt_ref[pl.ds(j * dk + d, 1), :] * delta
                s_ref[pl.ds(r0, dk), :] = sn
                return acc + qt_ref[pl.ds(j * dk + d, 1), :] * sn

            o_heads.append(lax.fori_loop(0, dk, update, jnp.zeros((dk, nb), F32), unroll=4))
        o_ref[rs, :] = jnp.concatenate(o_heads, axis=0).T


def _delta_sample(q, k, v, gate, s0t, nb, nt):
    dk = DN_HEAD_DIM
    flat = dk * dk
    n = nt * nb
    kern = functools.partial(_delta_step_kernel, nt=nt, nb=nb)
    pair = lambda p: (0, p)
    return pl.pallas_call(
        kern,
        grid=(DN_HEADS // 2,),
        in_specs=[pl.BlockSpec((n, LANES), pair),
                  pl.BlockSpec((n, LANES), pair),
                  pl.BlockSpec((n, LANES), pair),
                  pl.BlockSpec((n, LANES), lambda p: (0, 0)),
                  pl.BlockSpec((2 * flat, nb), lambda p: (p, 0))],
        out_specs=[pl.BlockSpec((n, LANES), pair),
                   pl.BlockSpec((2 * flat, nb), lambda p: (p, 0))],
        out_shape=[jax.ShapeDtypeStruct((n, DN_WIDTH), F32),
                   jax.ShapeDtypeStruct((DN_HEADS * flat, nb), F32)],
        scratch_shapes=[pltpu.VMEM((LANES, nb), F32),
                        pltpu.VMEM((LANES, nb), F32),
                        pltpu.VMEM((LANES, nb), F32)],
        compiler_params=_cparams(("arbitrary",)),
        name="delta_sample",
    )(q, k, v, gate, s0t)


def _s5_kernel(u_ref, btre_ref, btim_ref, lam_ref, ctre_ref, ctim_ref, d_ref, h0_ref, y_ref, hfin_ref,
               bw_ref, c_ref, ab_ref, x_ref, h_ref, *, nb, tt):
    p2 = S5_FLAT

    @pl.when(pl.program_id(0) == 0)
    def _():
        lr = lam_ref[0:1, :]
        li = lam_ref[1:2, :]
        dt = jnp.exp(lam_ref[2:3, :])
        mag = jnp.exp(lr * dt)
        ab_re = mag * jnp.cos(li * dt)
        ab_im = mag * jnp.sin(li * dt)
        den = lr * lr + li * li
        nr = ab_re - 1.0
        ni = ab_im
        f_re = (nr * lr + ni * li) / den
        f_im = (ni * lr - nr * li) / den
        ab_ref[0:1, :] = ab_re
        ab_ref[1:2, :] = ab_im
        gpl = LANES // S5_STATE
        ch_g = lax.broadcasted_iota(I32, (S5_WIDTH, LANES), 0) // S5_GROUP_CH
        lane_g = lax.broadcasted_iota(I32, (S5_WIDTH, LANES), 1) // S5_STATE
        bre2 = jnp.concatenate([btre_ref[...]] * gpl, axis=1)
        bim2 = jnp.concatenate([btim_ref[...]] * gpl, axis=1)
        for j in range(p2 // LANES):
            cols = slice(j * LANES, (j + 1) * LANES)
            own = ch_g == gpl * j + lane_g
            bre = jnp.where(own, bre2, 0.0)
            bim = jnp.where(own, bim2, 0.0)
            bw_ref[:, cols] = (bre * f_re[:, cols] - bim * f_im[:, cols]).astype(BF16)
            bw_ref[:, p2 + j * LANES:p2 + (j + 1) * LANES] = (bim * f_re[:, cols] + bre * f_im[:, cols]).astype(BF16)
        cpl = LANES // S5_GROUP_CH
        st_g = lax.broadcasted_iota(I32, (p2, LANES), 0) // S5_STATE
        lane_cg = lax.broadcasted_iota(I32, (p2, LANES), 1) // S5_GROUP_CH
        for j in range(S5_WIDTH // LANES):
            cols = slice(j * LANES, (j + 1) * LANES)
            own = st_g == cpl * j + lane_cg
            c_ref[0:p2, cols] = jnp.where(own, ctre_ref[...], 0.0).astype(BF16)
            c_ref[p2:2 * p2, cols] = jnp.where(own, -ctim_ref[...], 0.0).astype(BF16)
        h_ref[...] = h0_ref[...]

    u = u_ref[...]
    cw = S5_WIDTH // S5_SUPER
    sw = S5_FLAT // S5_SUPER
    for part in (0, p2):
        for b in range(S5_SUPER):
            x_ref[:, part + b * sw:part + (b + 1) * sw] = jnp.dot(
                u[:, b * cw:(b + 1) * cw], bw_ref[b * cw:(b + 1) * cw, part + b * sw:part + (b + 1) * sw],
                preferred_element_type=F32)
    a_re = ab_ref[0:1, :]
    a_im = ab_ref[1:2, :]

    if nb == SUBLANES:
        wsl = p2 // S5_SCAN_SPLIT
        for sp in range(S5_SCAN_SPLIT):
            c0 = sp * wsl
            are = jnp.broadcast_to(a_re[:, c0:c0 + wsl], (nb, wsl))
            aim = jnp.broadcast_to(a_im[:, c0:c0 + wsl], (nb, wsl))

            def step(t, carry, c0=c0, are=are, aim=aim):
                hr, hi = carry
                r0 = pl.multiple_of(t * nb, nb)
                nr = are * hr - aim * hi + x_ref[pl.ds(r0, nb), c0:c0 + wsl]
                ni = are * hi + aim * hr + x_ref[pl.ds(r0, nb), p2 + c0:p2 + c0 + wsl]
                x_ref[pl.ds(r0, nb), c0:c0 + wsl] = nr
                x_ref[pl.ds(r0, nb), p2 + c0:p2 + c0 + wsl] = ni
                return nr, ni

            hr, hi = lax.fori_loop(0, tt, step, (h_ref[:, c0:c0 + wsl], h_ref[:, p2 + c0:p2 + c0 + wsl]),
                                   unroll=2)
            h_ref[:, c0:c0 + wsl] = hr
            h_ref[:, p2 + c0:p2 + c0 + wsl] = hi
    else:
        for t in range(tt):
            rs = slice(t * nb, (t + 1) * nb)
            hr = h_ref[:, 0:p2]
            hi = h_ref[:, p2:2 * p2]
            nr = a_re * hr - a_im * hi + x_ref[rs, 0:p2]
            ni = a_re * hi + a_im * hr + x_ref[rs, p2:2 * p2]
            h_ref[:, 0:p2] = nr
            h_ref[:, p2:2 * p2] = ni
            x_ref[rs, 0:p2] = nr
            x_ref[rs, p2:2 * p2] = ni

    for b in range(S5_SUPER):
        cols = slice(b * cw, (b + 1) * cw)
        y = None
        for part in (0, p2):
            rws = slice(part + b * sw, part + (b + 1) * sw)
            term = jnp.dot(x_ref[:, rws].astype(BF16), c_ref[rws, cols], preferred_element_type=F32)
            y = term if y is None else y + term
        y_ref[:, cols] = y + d_ref[:, cols] * u[:, cols].astype(F32)
    hfin_ref[...] = h_ref[...]


def _s5(u, params, h0, nb):
    btre, btim, lam, ctre, ctim, dvec = params
    n = u.shape[0]
    t = n // nb
    tt = min(ROW_TILE // nb, t)
    rows = tt * nb
    const = lambda i: (0, 0)
    kern = functools.partial(_s5_kernel, nb=nb, tt=tt)
    return pl.pallas_call(
        kern,
        grid=(t // tt,),
        in_specs=[pl.BlockSpec((rows, S5_WIDTH), lambda i: (i, 0)),
                  pl.BlockSpec((S5_WIDTH, S5_STATE), const),
                  pl.BlockSpec((S5_WIDTH, S5_STATE), const),
                  pl.BlockSpec((SUBLANES, S5_FLAT), const),
                  pl.BlockSpec((S5_FLAT, LANES), const),
                  pl.BlockSpec((S5_FLAT, LANES), const),
                  pl.BlockSpec((1, S5_WIDTH), const),
                  pl.BlockSpec((nb, 2 * S5_FLAT), const)],
        out_specs=[pl.BlockSpec((rows, S5_WIDTH), lambda i: (i, 0)),
                   pl.BlockSpec((nb, 2 * S5_FLAT), const)],
        out_shape=[jax.ShapeDtypeStruct((n, S5_WIDTH), F32),
                   jax.ShapeDtypeStruct((nb, 2 * S5_FLAT), F32)],
        scratch_shapes=[pltpu.VMEM((S5_WIDTH, 2 * S5_FLAT), BF16),
                        pltpu.VMEM((2 * S5_FLAT, S5_WIDTH), BF16),
                        pltpu.VMEM((SUBLANES, S5_FLAT), F32),
                        pltpu.VMEM((rows, 2 * S5_FLAT), F32),
                        pltpu.VMEM((nb, 2 * S5_FLAT), F32)],
        compiler_params=_cparams(("arbitrary",)),
        name="s5",
    )(u, btre, btim, lam, ctre, ctim, dvec, h0)


def _postmix_kernel(xp_ref, op_ref, zp_ref, ysp_ref, gap_ref, gbp_ref,
                    xs_ref, os_ref, zs_ref, yss_ref, gas_ref, gbs_ref, *rest, nblk_p, range_tok):
    carry_ref = rest[-1]

    @pl.when(pl.program_id(0) == 0)
    def _():
        carry_ref[...] = jnp.zeros_like(carry_ref)

    @pl.when(pl.program_id(0) < nblk_p)
    def _():
        _postmix_body(xp_ref, op_ref, zp_ref, ysp_ref, gap_ref, gbp_ref, *rest, range_tok=range_tok)

    @pl.when(pl.program_id(0) >= nblk_p)
    def _():
        _postmix_body(xs_ref, os_ref, zs_ref, yss_ref, gas_ref, gbs_ref, *rest, range_tok=range_tok)


def _postmix_body(x_ref, o_ref, z_ref, ys_ref, ga_ref, gb_ref, hw_ref, seg_ref, wa_ref, wglu_ref, wb_ref,
                  wo_ref, nf_ref, wr_ref, su_ref, x1_ref, hn_ref, bkt_ref, rank_ref, rw_ref, cnt_ref, carry_ref,
                  *, range_tok):
    o = o_ref[...]
    ms = jnp.dot((o * o).astype(BF16), seg_ref[...], preferred_element_type=F32) * (1.0 / DN_HEAD_DIM)
    on = o * lax.rsqrt(ms + RMS_EPS) * hw_ref[...]
    z = z_ref[...]
    oa = on * (z * jax.nn.sigmoid(z)).astype(F32)
    y_a = _mm(oa, wa_ref[...])
    ys = jax.nn.gelu(ys_ref[...])
    ys = ys * jax.nn.sigmoid(_mm(ys, wglu_ref[...]))
    y_b = _mm(ys, wb_ref[...])
    mixed = jax.nn.sigmoid(ga_ref[...]).astype(F32) * y_a + jax.nn.sigmoid(gb_ref[...]).astype(F32) * y_b
    x1 = x_ref[...] + _mm(mixed, wo_ref[...])
    x1_ref[...] = x1
    hn = x1 * lax.rsqrt(jnp.mean(x1 * x1, axis=-1, keepdims=True) + RMS_EPS) * nf_ref[...]
    _slab_store(hn_ref, hn)

    wr = wr_ref[...]
    w_hi = wr.astype(BF16)
    w_lo = (wr - w_hi.astype(F32)).astype(BF16)
    hn_hi = hn.astype(BF16)
    hn_lo = (hn - hn_hi.astype(F32)).astype(BF16)
    both = _mm_nt(jnp.concatenate([w_hi, w_lo], axis=0), hn_hi)
    logits = both[:ROUTER_ROWS] + both[ROUTER_ROWS:] + _mm_nt(w_hi, hn_lo)
    coarse = logits[N_EXPERTS:N_EXPERTS + MOE_GROUPS, :]
    cm = jnp.max(coarse, axis=0, keepdims=True)
    ce = jnp.exp(coarse - cm)
    pc = ce / jnp.sum(ce, axis=0, keepdims=True)
    p_sel = jnp.max(pc, axis=0, keepdims=True)
    gi = lax.broadcasted_iota(I32, pc.shape, 0)
    g_sel = jnp.min(jnp.where(pc == p_sel, gi, MOE_GROUPS), axis=0, keepdims=True)
    fine = jnp.zeros((EXPERTS_PER_GROUP, logits.shape[1]), F32)
    for g in range(MOE_GROUPS):
        fine = fine + jnp.where(g_sel == g, logits[g * EXPERTS_PER_GROUP:(g + 1) * EXPERTS_PER_GROUP, :], 0.0)
    fm = jnp.max(fine, axis=0, keepdims=True)
    fe = jnp.exp(fine - fm)
    pf = fe / jnp.sum(fe, axis=0, keepdims=True)
    ei = lax.broadcasted_iota(I32, pf.shape, 0)
    v1 = jnp.max(pf, axis=0, keepdims=True)
    i1 = jnp.min(jnp.where(pf == v1, ei, EXPERTS_PER_GROUP), axis=0, keepdims=True)
    rest = jnp.where(ei == i1, -1.0, pf)
    v2 = jnp.max(rest, axis=0, keepdims=True)
    i2 = jnp.min(jnp.where(rest == v2, ei, EXPERTS_PER_GROUP), axis=0, keepdims=True)
    tot = v1 + v2
    rw_ref[0:1, :] = v1 / tot * p_sel
    rw_ref[1:2, :] = v2 / tot * p_sel

    tt = logits.shape[1]
    tok = pl.program_id(0) * tt + lax.broadcasted_iota(I32, (1, tt), 1)
    ph = jnp.zeros((1, tt), I32)
    for r in range(1, MOE_PHASES):
        ph = ph + (tok >= r * range_tok).astype(I32)
    bsel = [ph * N_EXPERTS + g_sel * EXPERTS_PER_GROUP + ix for ix in (i1, i2)]
    bi = lax.broadcasted_iota(I32, (MOE_PHASES * N_EXPERTS, tt), 0)
    onehot = [(bi == b).astype(F32) for b in bsel]
    cnt = onehot[0] + onehot[1]
    before = carry_ref[:, 0:1] + jnp.dot(cnt.astype(BF16), su_ref[...], preferred_element_type=F32)
    for s in range(TOP_K):
        bkt_ref[s:s + 1, :] = bsel[s]
        rank_ref[s:s + 1, :] = jnp.sum(onehot[s] * before, axis=0, keepdims=True).astype(I32)
    carry_ref[...] = carry_ref[...] + jnp.sum(cnt, axis=1, keepdims=True)
    cnt_ref[...] = carry_ref[...]


def _postmix(prompt, sample, weights, nb):
    n_p = prompt[0].shape[0]
    n_s = sample[0].shape[0]
    t = n_p // nb
    tt = min(ROW_TILE, t, n_s)
    nt = t // tt
    nblk_p = n_p // tt
    nblk = nblk_p + n_s // tt
    n_total = n_p + n_s
    prow = lambda i: (jnp.minimum(i, nblk_p - 1), 0)
    pys = lambda i: (jnp.minimum(i, nblk_p - 1) % nt, jnp.minimum(i, nblk_p - 1) // nt)
    srow = lambda i: (jnp.maximum(i - nblk_p, 0), 0)
    const = lambda i: (0, 0)

    def stream_specs(row, ysmap):
        return [pl.BlockSpec((tt, D_MODEL), row),
                pl.BlockSpec((tt, DN_WIDTH), row),
                pl.BlockSpec((tt, DN_WIDTH), row),
                pl.BlockSpec((tt, S5_WIDTH), ysmap),
                pl.BlockSpec((tt, D_MODEL), row),
                pl.BlockSpec((tt, D_MODEL), row)]

    weight_specs = [pl.BlockSpec((1, DN_WIDTH), const),
                    pl.BlockSpec((DN_WIDTH, DN_WIDTH), const),
                    pl.BlockSpec((DN_WIDTH, D_MODEL), const),
                    pl.BlockSpec((S5_WIDTH, S5_WIDTH), const),
                    pl.BlockSpec((S5_WIDTH, D_MODEL), const),
                    pl.BlockSpec((D_MODEL, D_MODEL), const),
                    pl.BlockSpec((1, D_MODEL), const),
                    pl.BlockSpec((ROUTER_ROWS, D_MODEL), const),
                    pl.BlockSpec((tt, tt), const)]
    xp, op, zp, ysp, gap, gbp = prompt
    nbk = MOE_PHASES * N_EXPERTS
    earlier = jnp.triu(jnp.ones((tt, tt), F32), k=1).astype(BF16)
    return pl.pallas_call(
        functools.partial(_postmix_kernel, nblk_p=nblk_p, range_tok=n_total // MOE_PHASES),
        grid=(nblk,),
        in_specs=stream_specs(prow, pys) + stream_specs(srow, srow) + weight_specs,
        out_specs=[pl.BlockSpec((tt, D_MODEL), lambda i: (i, 0)),
                   pl.BlockSpec((tt * ROW_SLAB, LANES), lambda i: (i, 0)),
                   pl.BlockSpec((TOP_K, tt), lambda i: (0, i)),
                   pl.BlockSpec((TOP_K, tt), lambda i: (0, i)),
                   pl.BlockSpec((TOP_K, tt), lambda i: (0, i)),
                   pl.BlockSpec((nbk, LANES), const)],
        out_shape=[jax.ShapeDtypeStruct((n_total, D_MODEL), F32),
                   jax.ShapeDtypeStruct((n_total * ROW_SLAB, LANES), F32),
                   jax.ShapeDtypeStruct((TOP_K, n_total), I32),
                   jax.ShapeDtypeStruct((TOP_K, n_total), I32),
                   jax.ShapeDtypeStruct((TOP_K, n_total), F32),
                   jax.ShapeDtypeStruct((nbk, LANES), F32)],
        scratch_shapes=[pltpu.VMEM((nbk, LANES), F32)],
        compiler_params=_cparams(("arbitrary",)),
        name="postmix",
    )(xp, op, zp, ysp.reshape(t, nb * S5_WIDTH), gap, gbp, *sample, *weights, earlier)


def _wait_slabs(buf, sem):
    pltpu.make_async_copy(buf, buf, sem).wait()


def _moe_kernel(texp_ref, tph_ref, tsrc_ref, tnv_ref, tfirst_ref, tslot_ref, tnext_ref, otok_ref,
                hn_hbm, wu_hbm, wd_hbm, y_ref, hnv, xbuf, wu_buf, wd_buf, wub, wdb, sem, wsem):
    i = pl.program_id(0)
    tm = MOE_TILE
    rs = ROW_SLAB
    nv = tnv_ref[i]
    ph = tph_ref[i]
    range_rows = hnv.shape[0]

    def weight_copies(e, sl):
        return (pltpu.make_async_copy(wu_hbm.at[e], wu_buf.at[sl], wsem.at[sl]),
                pltpu.make_async_copy(wd_hbm.at[e], wd_buf.at[sl], wsem.at[sl]))

    @pl.when(i == 0)
    def _():
        for c in weight_copies(texp_ref[0], 0):
            c.start()

    @pl.when(jnp.logical_and(nv > 0, jnp.logical_or(i == 0, ph != tph_ref[jnp.maximum(i - 1, 0)])))
    def _():
        start = pl.multiple_of(ph * range_rows, rs)
        whole = pltpu.make_async_copy(hn_hbm.at[pl.ds(start, range_rows), :], hnv, sem)
        whole.start()
        whole.wait()

    for sl in range(2):
        @pl.when(jnp.logical_and(jnp.logical_and(nv > 0, tfirst_ref[i] == 1), tslot_ref[i] == sl))
        def _():
            for c in weight_copies(texp_ref[i], sl):
                c.wait()

            @pl.when(tnext_ref[i] >= 0)
            def _():
                for c in weight_copies(tnext_ref[i], 1 - sl):
                    c.start()

            wub[...] = wu_buf[sl].astype(BF16)
            wdb[...] = wd_buf[sl].astype(BF16)

    @pl.when(nv == 0)
    def _():
        y_ref[...] = jnp.zeros_like(y_ref)

    @pl.when(nv > 0)
    def _():
        src0 = tsrc_ref[i]
        for r in range(tm):
            tok8 = pl.multiple_of(otok_ref[src0 + r], rs)
            xbuf[pl.ds(r * rs, rs), :] = hnv[pl.ds(tok8, rs), :]
        x = _slab_load(xbuf, tm).astype(BF16)
        hu = jnp.dot(x, wub[...], preferred_element_type=F32)
        gate = hu[:, :EXPERT_FF]
        up = hu[:, EXPERT_FF:]
        act = gate * jax.nn.sigmoid(gate) * up
        _slab_store(y_ref, jnp.dot(act.astype(BF16), wdb[...], preferred_element_type=F32))


def _moe(hn, w_up, w_down, plan):
    ntiles = plan[0].shape[0]
    grid_spec = pltpu.PrefetchScalarGridSpec(
        num_scalar_prefetch=len(plan),
        grid=(ntiles,),
        in_specs=[pl.BlockSpec(memory_space=pl.ANY),
                  pl.BlockSpec(memory_space=pl.ANY),
                  pl.BlockSpec(memory_space=pl.ANY)],
        out_specs=pl.BlockSpec((MOE_TILE * ROW_SLAB, LANES), lambda i, *_: (i, 0)),
        scratch_shapes=[pltpu.VMEM((hn.shape[0] // MOE_PHASES, LANES), F32),
                        pltpu.VMEM((MOE_TILE * ROW_SLAB, LANES), F32),
                        pltpu.VMEM((2, D_MODEL, 2 * EXPERT_FF), F32),
                        pltpu.VMEM((2, EXPERT_FF, D_MODEL), F32),
                        pltpu.VMEM((D_MODEL, 2 * EXPERT_FF), BF16),
                        pltpu.VMEM((EXPERT_FF, D_MODEL), BF16),
                        pltpu.SemaphoreType.DMA,
                        pltpu.SemaphoreType.DMA((2,))])
    return pl.pallas_call(
        _moe_kernel,
        grid_spec=grid_spec,
        out_shape=jax.ShapeDtypeStruct((ntiles * MOE_TILE * ROW_SLAB, LANES), F32),
        compiler_params=_cparams(("arbitrary",)),
        name="moe",
    )(*plan, hn, w_up, w_down)


def _combine_kernel(pos_ref, x1_ref, ys_hbm, w_ref, nw_ref, outp_ref, outs_ref,
                    ybuf0, ybuf1, sem, *, nblk_p, n_tok):
    i = pl.program_id(0)
    nsteps = pl.num_programs(0)
    tt = x1_ref.shape[0]
    rs = ROW_SLAB
    slot = lax.rem(i, 2)
    ybuf = (ybuf0, ybuf1)

    def start_gather(step, sl):
        base = step * tt
        for r in range(tt * TOP_K):
            j, s = divmod(r, TOP_K)
            p8 = pl.multiple_of(pos_ref[s * n_tok + base + j], rs)
            pltpu.make_async_copy(ys_hbm.at[pl.ds(p8, rs), :], ybuf[sl].at[pl.ds((s * tt + j) * rs, rs), :],
                                  sem.at[sl]).start(priority=r % DMA_QUEUES)

    @pl.when(i == 0)
    def _():
        start_gather(0, 0)

    for sl in range(2):
        @pl.when(slot == sl)
        def _():
            _wait_slabs(ybuf[sl], sem.at[sl])
            start_gather(jnp.minimum(i + 1, nsteps - 1), 1 - sl)
            w = w_ref[...]
            y0 = _slab_load(ybuf[sl], tt, 0)
            y1 = _slab_load(ybuf[sl], tt, tt * rs)
            x = x1_ref[...] + w[:, 0:1] * y0 + w[:, 1:2] * y1
            res = x * lax.rsqrt(jnp.mean(x * x, axis=-1, keepdims=True) + RMS_EPS) * nw_ref[...]

            @pl.when(i < nblk_p)
            def _():
                outp_ref[...] = res

            @pl.when(i >= nblk_p)
            def _():
                outs_ref[...] = res

        @pl.when(jnp.logical_and(slot == sl, i == nsteps - 1))
        def _():
            _wait_slabs(ybuf[1 - sl], sem.at[1 - sl])


def _combine(x1, ysorted, pos8, wtok, nw, n_p):
    n = x1.shape[0]
    tt = math.gcd(math.gcd(n_p, n - n_p), COMBINE_TILE)
    nblk_p = n_p // tt
    grid_spec = pltpu.PrefetchScalarGridSpec(
        num_scalar_prefetch=1,
        grid=(n // tt,),
        in_specs=[pl.BlockSpec((tt, D_MODEL), lambda i, *_: (i, 0)),
                  pl.BlockSpec(memory_space=pl.ANY),
                  pl.BlockSpec((tt, TOP_K), lambda i, *_: (i, 0)),
                  pl.BlockSpec((1, D_MODEL), lambda i, *_: (0, 0))],
        out_specs=[pl.BlockSpec((tt, D_MODEL), lambda i, *_: (jnp.minimum(i, nblk_p - 1), 0)),
                   pl.BlockSpec((tt, D_MODEL), lambda i, *_: (jnp.maximum(i - nblk_p, 0), 0))],
        scratch_shapes=[pltpu.VMEM((tt * TOP_K * ROW_SLAB, LANES), F32),
                        pltpu.VMEM((tt * TOP_K * ROW_SLAB, LANES), F32),
                        pltpu.SemaphoreType.DMA((2,))])
    return pl.pallas_call(
        functools.partial(_combine_kernel, nblk_p=nblk_p, n_tok=n),
        grid_spec=grid_spec,
        out_shape=[jax.ShapeDtypeStruct((n_p, D_MODEL), F32),
                   jax.ShapeDtypeStruct((n - n_p, D_MODEL), F32)],
        compiler_params=_cparams(("arbitrary",)),
        name="combine",
    )(pos8, x1, ysorted, wtok, nw)


def _route_plan(bkt, rank, cnt, n_tok):
    tm = MOE_TILE
    n_assign = n_tok * TOP_K
    nbk = MOE_PHASES * N_EXPERTS
    ntiles = n_assign // tm + nbk
    range_tok = n_tok // MOE_PHASES
    b_flat = bkt.T.reshape(n_assign)
    order = jnp.argsort(b_flat, stable=True).astype(I32)
    counts = cnt[:, 0].astype(I32)
    cstart = jnp.cumsum(counts) - counts
    tiles_b = (counts + tm - 1) // tm
    tend = jnp.cumsum(tiles_b)
    tstart = tend - tiles_b
    tile_id = jnp.arange(ntiles, dtype=I32)
    tbk = jnp.minimum(jnp.sum((tile_id[:, None] >= tend[None, :]).astype(I32), axis=1), nbk - 1)
    onehot = (tbk[:, None] == jnp.arange(nbk, dtype=I32)[None, :]).astype(I32)
    pick = lambda v: jnp.sum(onehot * v[None, :], axis=1)
    done = (tile_id - pick(tstart)) * tm
    tnv = jnp.where(tile_id < tend[-1], jnp.clip(pick(counts) - done, 0, tm), 0)
    tsrc = jnp.where(tnv > 0, pick(cstart) + done, 0)
    texp = tbk % N_EXPERTS
    tph = tbk // N_EXPERTS
    nonempty = counts > 0
    bslot = (jnp.cumsum(nonempty.astype(I32)) - 1) % 2
    bidx = jnp.where(nonempty, jnp.arange(nbk, dtype=I32), nbk)
    nxt = jnp.concatenate([lax.cummin(bidx[::-1])[::-1][1:], jnp.full((1,), nbk, I32)])
    bnext = jnp.where(nxt < nbk, nxt % N_EXPERTS, -1)
    tfirst = jnp.logical_and(tnv > 0, done == 0).astype(I32)
    tslot = pick(bslot)
    tnext = pick(bnext)
    otok8 = jnp.concatenate([((order // TOP_K) % range_tok) * ROW_SLAB, jnp.zeros((tm,), I32)])
    plan = tuple(a.astype(I32) for a in (texp, tph, tsrc, tnv, tfirst, tslot, tnext, otok8))
    first = jnp.sum((bkt[:, :, None] == jnp.arange(nbk, dtype=I32)[None, None, :]).astype(I32)
                    * (tstart * tm)[None, None, :], axis=2)
    pos8 = ((first + rank) * ROW_SLAB).reshape(n_assign)
    return plan, pos8.astype(I32)


def _block_diag(m):
    g, a, b = m.shape
    eye = jnp.eye(g, dtype=m.dtype)
    return (eye[:, None, :, None] * m[:, :, None, :]).reshape(g * a, g * b)


def kernel(x_prompt, x_sample, state_conv, state_delta, state_ssm_re, state_ssm_im, norm_mix_w, w_in, conv_w, a_log, dt_bias, head_norm_w, w_a_up, s5_lambda_re, s5_lambda_im, s5_log_step, s5_b_re, s5_b_im, s5_c_re, s5_c_im, s5_d, w_glu, w_b_up, w_o, norm_ffn_w, w_router_coarse, w_router_fine, w_expert_up, w_expert_down, norm_final_w):
    bp, tp, _ = x_prompt.shape
    bs, ts, _ = x_sample.shape
    n_p = bp * tp
    n_s = bs * ts
    n_tok = n_p + n_s
    l = 0

    w = w_in[l].astype(BF16)
    c_ab = W1_COLS + 2 * DN_HEADS
    w_parts = (w[:, :W1_COLS], w[:, c_ab:],
               jnp.concatenate([w[:, W1_COLS:c_ab], jnp.zeros((D_MODEL, LANES - 2 * DN_HEADS), BF16)], axis=1))
    nw_mix = norm_mix_w[l].reshape(1, D_MODEL)
    pad8 = lambda v: jnp.concatenate([v, jnp.zeros((LANES - DN_HEADS,), F32)]).reshape(1, LANES)
    gate_p = jnp.concatenate([pad8(a_log[l]), pad8(dt_bias[l])], axis=0)
    seg = _block_diag(jnp.ones((DN_HEADS, DN_HEAD_DIM, DN_HEAD_DIM), BF16))
    chan_rows = lambda b: jnp.swapaxes(b, 1, 2).reshape(S5_WIDTH, S5_STATE)
    state_rows = lambda c: jnp.tile(jnp.swapaxes(c, 1, 2).reshape(S5_FLAT, S5_GROUP_CH),
                                    (1, LANES // S5_GROUP_CH))
    lam = jnp.concatenate([s5_lambda_re[l].reshape(1, S5_FLAT), s5_lambda_im[l].reshape(1, S5_FLAT),
                           jnp.repeat(s5_log_step[l], S5_STATE).reshape(1, S5_FLAT),
                           jnp.zeros((SUBLANES - 3, S5_FLAT), F32)], axis=0)
    s5_params = (chan_rows(s5_b_re[l]), chan_rows(s5_b_im[l]), lam,
                 state_rows(s5_c_re[l]), state_rows(s5_c_im[l]), s5_d[l].reshape(1, S5_WIDTH))
    hw = jnp.tile(head_norm_w[l], DN_HEADS).reshape(1, DN_WIDTH)
    wr = jnp.concatenate([w_router_fine[l].T, w_router_coarse[l].T,
                          jnp.zeros((ROUTER_ROWS - N_EXPERTS - MOE_GROUPS, D_MODEL), F32)], axis=0)
    pm_weights = (hw, seg, w_a_up[l].astype(BF16), w_glu[l].astype(BF16), w_b_up[l].astype(BF16),
                  w_o[l].astype(BF16), norm_ffn_w[l].reshape(1, D_MODEL), wr)

    xp2 = x_prompt.reshape(n_p, D_MODEL)
    q_p, k_p, v_p, gates_p, conv_p, z_p, u_p, ga_p, gb_p = _inprep(
        xp2, nw_mix, w_parts, jnp.zeros((bp, SUBLANES, QKV_DIM), F32), conv_w[l], gate_p, seg, bp, 1)
    o_p, delta_p = _delta_prompt(q_p, k_p, v_p, gates_p, bp)
    ys_p, h_p = _s5(u_p, s5_params, jnp.zeros((bp, 2 * S5_FLAT), F32), bp)

    xs2 = jnp.swapaxes(x_sample, 0, 1).reshape(n_s, D_MODEL)
    cinit_s = jnp.swapaxes(state_conv[l], 0, 1).reshape(1, (CONV_W - 1) * bs, QKV_DIM)
    q_s, k_s, v_s, gate_s, conv_s, z_s, u_s, ga_s, gb_s = _inprep(
        xs2, nw_mix, w_parts, cinit_s, conv_w[l], gate_p, seg, 1, bs)
    s0t = jnp.transpose(state_delta[l], (1, 2, 3, 0)).reshape(DN_HEADS * DN_HEAD_DIM * DN_HEAD_DIM, bs)
    o_s, delta_st = _delta_sample(q_s, k_s, v_s, gate_s, s0t, bs, ts)
    delta_s = jnp.transpose(delta_st.reshape(DN_HEADS, DN_HEAD_DIM, DN_HEAD_DIM, bs), (3, 0, 1, 2))
    h0_s = jnp.concatenate([state_ssm_re[l].reshape(bs, S5_FLAT), state_ssm_im[l].reshape(bs, S5_FLAT)], axis=1)
    ys_s, h_s = _s5(u_s, s5_params, h0_s, bs)
    x1, hn, bkt, rank, rw, cnt = _postmix((xp2, o_p, z_p, ys_p, ga_p, gb_p), (xs2, o_s, z_s, ys_s, ga_s, gb_s),
                                          pm_weights, bp)

    plan, pos8 = _route_plan(bkt, rank, cnt, n_tok)
    ysorted = _moe(hn, w_expert_up[l], w_expert_down[l], plan)
    y_p, y_s = _combine(x1, ysorted, pos8, rw.T, norm_final_w.reshape(1, D_MODEL), n_p)

    y_prompt = y_p.reshape(bp, tp, D_MODEL)
    y_sample = jnp.swapaxes(y_s.reshape(ts, bs, D_MODEL), 0, 1)
    conv_sample = jnp.swapaxes(conv_s.reshape(CONV_W - 1, bs, QKV_DIM), 0, 1)
    return (y_prompt, y_sample,
            conv_p[None], delta_p[None],
            h_p[:, :S5_FLAT].reshape(1, bp, S5_GROUPS, S5_STATE), h_p[:, S5_FLAT:].reshape(1, bp, S5_GROUPS, S5_STATE),
            conv_sample[None], delta_s[None],
            h_s[:, :S5_FLAT].reshape(1, bs, S5_GROUPS, S5_STATE), h_s[:, S5_FLAT:].reshape(1, bs, S5_GROUPS, S5_STATE))
```

```python
import functools
import math

import jax
import jax.numpy as jnp
import numpy as np
from jax import lax
from jax.experimental import pallas as pl
from jax.experimental.pallas import tpu as pltpu

F32 = jnp.float32
BF16 = jnp.bfloat16
I32 = jnp.int32

D_MODEL = 1024
DN_HEADS = 8
DN_HEAD_DIM = 64
DN_WIDTH = DN_HEADS * DN_HEAD_DIM
QKV_DIM = 3 * DN_WIDTH
CONV_W = 4
DN_CHUNK = 64
S5_GROUP_CH = 16
S5_WIDTH = D_MODEL // 2
S5_GROUPS = S5_WIDTH // S5_GROUP_CH
S5_STATE = 64
S5_FLAT = S5_GROUPS * S5_STATE
MOE_GROUPS = 4
EXPERTS_PER_GROUP = 8
N_EXPERTS = MOE_GROUPS * EXPERTS_PER_GROUP
TOP_K = 2
EXPERT_FF = 256
RMS_EPS = 1e-6
L2_EPS = 1e-6

LANES = 128
SUBLANES = 8
VMEM_LIMIT = 56 * 1024 * 1024

W1_COLS = QKV_DIM + DN_WIDTH
W2_COLS = S5_WIDTH + 2 * D_MODEL

ROW_TILE = 512
INPREP_PARTS = 2
MOE_TILE = 256
MOE_PHASES = 2
COMBINE_TILE = 256
DMA_QUEUES = 2
DELTA_SUBCHUNKS = 4
S5_SUPER = 2
S5_SCAN_SPLIT = 2
ROUTER_ROWS = 40


def _mm(a, b):
    return jnp.dot(a.astype(BF16), b.astype(BF16), preferred_element_type=F32)


def _mm_nt(a, b):
    return lax.dot_general(a.astype(BF16), b.astype(BF16), (((1,), (1,)), ((), ())),
                           preferred_element_type=F32)


def _split3_dot(a, b01):
    a1 = a.astype(BF16)
    r1 = a - a1.astype(F32)
    a2 = r1.astype(BF16)
    a3 = (r1 - a2.astype(F32)).astype(BF16)
    out = jnp.dot(a3, b01, preferred_element_type=F32)
    out = out + jnp.dot(a2, b01, preferred_element_type=F32)
    return out + jnp.dot(a1, b01, preferred_element_type=F32)


def _cparams(sem):
    return pltpu.CompilerParams(dimension_semantics=sem, vmem_limit_bytes=VMEM_LIMIT)


ROW_SLAB = D_MODEL // LANES


def _slab_load(ref, rows, first=0, pitch=ROW_SLAB):
    return jnp.concatenate([ref[pl.ds(first + j, rows, stride=pitch), :] for j in range(ROW_SLAB)], axis=1)


def _slab_store(ref, x):
    for j in range(ROW_SLAB):
        ref[pl.ds(j, x.shape[0], stride=ROW_SLAB), :] = x[:, j * LANES:(j + 1) * LANES]


def _softplus(x):
    return jnp.maximum(x, 0.0) + jnp.log1p(jnp.exp(-jnp.abs(x)))


def _inprep_kernel(x_ref, nw_ref, w1_ref, w2_ref, wab_ref, cinit_ref, cw_ref, gp_ref, seg_ref,
                   q_ref, k_ref, v_ref, gate_ref, cnew_ref, z_ref, u_ref, ga_ref, gb_ref, xp_ref,
                   *, shift, rc, rows):
    @pl.when(pl.program_id(1) == 0)
    def _():
        xp_ref[0:rc, :] = cinit_ref[0]

    seg = seg_ref[...]
    pr = rows // INPREP_PARTS
    for part in range(INPREP_PARTS):
        rs = slice(part * pr, (part + 1) * pr)
        x = x_ref[rs, :]
        h = x * lax.rsqrt(jnp.mean(x * x, axis=-1, keepdims=True) + RMS_EPS) * nw_ref[...]
        hb = h.astype(BF16)

        def proj(w_ref, lo, hi, hb=hb):
            return jnp.dot(hb, w_ref[:, lo:hi], preferred_element_type=F32)

        xp_ref[rc + part * pr:rc + (part + 1) * pr, :] = proj(w1_ref, 0, QKV_DIM)
        ab = proj(wab_ref, 0, LANES)
        z_ref[rs, :] = proj(w1_ref, QKV_DIM, W1_COLS).astype(z_ref.dtype)
        u_ref[rs, :] = proj(w2_ref, 0, S5_WIDTH).astype(u_ref.dtype)
        ga_ref[rs, :] = proj(w2_ref, S5_WIDTH, S5_WIDTH + D_MODEL).astype(ga_ref.dtype)
        gb_ref[rs, :] = proj(w2_ref, S5_WIDTH + D_MODEL, W2_COLS).astype(gb_ref.dtype)
        acc = None
        for i in range(CONV_W):
            lo = rc + part * pr + (i - (CONV_W - 1)) * shift
            term = xp_ref[lo:lo + pr, :] * cw_ref[i:i + 1, :]
            acc = term if acc is None else acc + term
        y = acc * jax.nn.sigmoid(acc)
        q = y[:, 0:DN_WIDTH]
        k = y[:, DN_WIDTH:2 * DN_WIDTH]
        q_ref[rs, :] = q * lax.rsqrt(jnp.dot((q * q).astype(BF16), seg, preferred_element_type=F32) + L2_EPS)
        k_ref[rs, :] = k * lax.rsqrt(jnp.dot((k * k).astype(BF16), seg, preferred_element_type=F32) + L2_EPS)
        v_ref[rs, :] = y[:, 2 * DN_WIDTH:]
        g = -jnp.exp(gp_ref[0:1, :]) * _softplus(ab + gp_ref[1:2, :])
        beta = jax.nn.sigmoid(ab)
        lane = lax.broadcasted_iota(I32, ab.shape, 1)
        gate_ref[rs, :] = jnp.where(lane < DN_HEADS, g, beta)

    keep = (CONV_W - 1) * shift
    cnew_ref[0] = xp_ref[rc + rows - keep:rc + rows, :]
    xp_ref[0:rc, :] = xp_ref[rows:rows + rc, :]


def _inprep(x2d, nw, w_parts, cinit, conv_w, gate_p, seg, nb, shift):
    n = x2d.shape[0]
    r = n // nb
    rows = min(ROW_TILE, r)
    nt = r // rows
    rc = cinit.shape[1]
    keep = (CONV_W - 1) * shift
    row = lambda b, i: (b * nt + i, 0)
    const = lambda b, i: (0, 0)
    kern = functools.partial(_inprep_kernel, shift=shift, rc=rc, rows=rows)
    outs = pl.pallas_call(
        kern,
        grid=(nb, nt),
        in_specs=[pl.BlockSpec((rows, D_MODEL), row),
                  pl.BlockSpec((1, D_MODEL), const),
                  pl.BlockSpec((D_MODEL, W1_COLS), const),
                  pl.BlockSpec((D_MODEL, W2_COLS), const),
                  pl.BlockSpec((D_MODEL, LANES), const),
                  pl.BlockSpec((1, rc, QKV_DIM), lambda b, i: (b, 0, 0)),
                  pl.BlockSpec((CONV_W, QKV_DIM), const),
                  pl.BlockSpec((2, LANES), const),
                  pl.BlockSpec((DN_WIDTH, DN_WIDTH), const)],
        out_specs=[pl.BlockSpec((rows, DN_WIDTH), row),
                   pl.BlockSpec((rows, DN_WIDTH), row),
                   pl.BlockSpec((rows, DN_WIDTH), row),
                   pl.BlockSpec((rows, LANES), row),
                   pl.BlockSpec((1, keep, QKV_DIM), lambda b, i: (b, 0, 0)),
                   pl.BlockSpec((rows, DN_WIDTH), row),
                   pl.BlockSpec((rows, S5_WIDTH), lambda b, i: (i, b)),
                   pl.BlockSpec((rows, D_MODEL), row),
                   pl.BlockSpec((rows, D_MODEL), row)],
        out_shape=[jax.ShapeDtypeStruct((n, DN_WIDTH), F32),
                   jax.ShapeDtypeStruct((n, DN_WIDTH), F32),
                   jax.ShapeDtypeStruct((n, DN_WIDTH), F32),
                   jax.ShapeDtypeStruct((n, LANES), F32),
                   jax.ShapeDtypeStruct((nb, keep, QKV_DIM), F32),
                   jax.ShapeDtypeStruct((n, DN_WIDTH), BF16),
                   jax.ShapeDtypeStruct((r, nb * S5_WIDTH), BF16),
                   jax.ShapeDtypeStruct((n, D_MODEL), BF16),
                   jax.ShapeDtypeStruct((n, D_MODEL), BF16)],
        scratch_shapes=[pltpu.VMEM((rc + rows, QKV_DIM), F32)],
        compiler_params=_cparams(("arbitrary", "arbitrary")),
        name="inprep",
    )(x2d, nw, *w_parts, cinit, conv_w, gate_p, seg)
    q, k, v, gate, cnew, z, u, ga, gb = outs
    return q, k, v, gate, cnew, z, u.reshape(r * nb, S5_WIDTH), ga, gb


def _delta_home(low, h, x, other=0.0):
    return jnp.where(low, x, other) if h % 2 == 0 else jnp.where(low, other, x)


def _delta_prepare(q_ref, k_ref, v_ref, gate_ref, tril_ref, bufs, *, nsub):
    sol_buf, wq_buf, qk_buf, kdec_buf, dl_buf = bufs
    c = DN_CHUNK
    dk = DN_HEAD_DIM

    def home(h, x, other=0.0):
        return _delta_home(low, h, x, other)

    rowi2 = lax.broadcasted_iota(I32, (c, 2 * c), 0)
    lane2 = lax.broadcasted_iota(I32, (c, 2 * c), 1)
    coli2 = lane2 & (c - 1)
    causal2 = rowi2 >= coli2
    strict2 = rowi2 > coli2
    low = lane2 < dk
    tril = tril_ref[...]
    pairs = [(j, h) for j in range(nsub) for h in range(DN_HEADS)]
    units = [(j, pr) for j in range(nsub) for pr in range(DN_HEADS // 2)]
    rows = [slice(j * c, (j + 1) * c) for j in range(nsub)]
    gate = [gate_ref[rows[j], :] for j in range(nsub)]
    gc_all = [_split3_dot_left(tril, gate[j]) for j in range(nsub)]
    gc_t = [gc_all[j].T for j in range(nsub)]

    def block(ref, j, pr):
        return ref[rows[j], pr * LANES:(pr + 1) * LANES]

    gfull = {(j, h): jnp.broadcast_to(gc_all[j][:, h:h + 1], (c, 2 * c)) for j, h in pairs}
    g2 = {(j, pr): jnp.where(low, gfull[j, 2 * pr], gfull[j, 2 * pr + 1]) for j, pr in units}
    b2 = {(j, pr): jnp.where(low, gate[j][:, DN_HEADS + 2 * pr:DN_HEADS + 2 * pr + 1],
                             gate[j][:, DN_HEADS + 2 * pr + 1:DN_HEADS + 2 * pr + 2]) for j, pr in units}
    kp = {u: block(k_ref, *u) for u in units}
    qp = {u: block(q_ref, *u) * (dk ** -0.5) for u in units}
    egc2 = {u: jnp.exp(g2[u]) for u in units}
    kb2 = {u: kp[u] * b2[u] for u in units}
    vb2 = {u: block(v_ref, *u) * b2[u] for u in units}
    kw2s = {u: pltpu.roll(kb2[u] * egc2[u], dk, axis=1) for u in units}
    qd2 = {u: qp[u] * egc2[u] for u in units}
    glast2 = {u: g2[u][c - 1:c, :] for u in units}
    kdec_t2 = {u: (kp[u] * jnp.exp(glast2[u] - g2[u])).T for u in units}
    dlast2 = {u: jnp.exp(glast2[u]) for u in units}
    kk = {u: jnp.concatenate([kp[u], kp[u]], axis=0) for u in units}
    yield

    grow2 = {(j, h): jnp.concatenate([gc_t[j][h:h + 1, :], gc_t[j][h:h + 1, :]], axis=1) for j, h in pairs}
    decay = {p: jnp.where(causal2, jnp.exp(jnp.where(causal2, gfull[p] - grow2[p], 0.0)), 0.0) for p in pairs}
    gram = {(j, h): _mm_nt(jnp.concatenate([home(h, kb2[j, h // 2]), home(h, qp[j, h // 2])], axis=0), kk[j, h // 2])
            for j, h in pairs}
    mat = {p: jnp.where(strict2, gram[p][:c] * decay[p], 0.0).astype(BF16) for p in pairs}
    qk = {p: jnp.where(causal2, gram[p][c:] * decay[p], 0.0) for p in pairs}
    sol = {(j, h): home(h, vb2[j, h // 2], kw2s[j, h // 2]) for j, h in pairs}
    yield
    levels = int(math.log2(c))
    zeros2 = jnp.zeros((c, 2 * c), BF16)
    for lvl in range(levels):
        hi = {p: sol[p].astype(BF16) for p in pairs}
        lo = {p: (sol[p] - hi[p].astype(F32)).astype(BF16) for p in pairs}
        if lvl < levels - 1:
            y = {p: jnp.dot(mat[p], jnp.concatenate([jnp.concatenate([hi[p], mat[p]], axis=1),
                                                     jnp.concatenate([lo[p], zeros2], axis=1)], axis=0),
                            preferred_element_type=F32) for p in pairs}
            mat = {p: y[p][:, 2 * dk:].astype(BF16) for p in pairs}
            upd = {p: y[p][:, :2 * dk] for p in pairs}
        else:
            upd = {p: jnp.dot(mat[p], jnp.concatenate([hi[p], lo[p]], axis=0), preferred_element_type=F32)
                   for p in pairs}
        sol = {p: (sol[p] - upd[p]) if lvl == 0 else (sol[p] + upd[p]) for p in pairs}
        yield
    for j, h in pairs:
        n = j * DN_HEADS + h
        sol_buf[n] = sol[j, h]
        wq_buf[n] = jnp.concatenate([home(h, 0.0, sol[j, h]), home(h, qd2[j, h // 2])], axis=0).astype(BF16)
        qk_buf[n] = qk[j, h].astype(BF16)
    for j, pr in units:
        n = j * (DN_HEADS // 2) + pr
        kdec_buf[n] = kdec_t2[j, pr].astype(BF16)
        dl_buf[n] = jnp.broadcast_to(dlast2[j, pr], (SUBLANES, LANES))


def _delta_apply(bufs, o_ref, s_ref, *, nsub):
    sol_buf, wq_buf, qk_buf, kdec_buf, dl_buf = bufs
    c = DN_CHUNK
    dk = DN_HEAD_DIM
    heads = range(DN_HEADS)
    low = lax.broadcasted_iota(I32, (c, 2 * c), 1) < dk
    s = [s_ref[h] for h in heads]
    for j in range(nsub):
        ws, v_new, o_h = [], [], []
        for h in heads:
            n = j * DN_HEADS + h
            ws.append(jnp.dot(wq_buf[n], jnp.concatenate([s[h], s[h]], axis=0).astype(BF16),
                              preferred_element_type=F32))
        yield
        for h in heads:
            v_new.append(sol_buf[j * DN_HEADS + h] - ws[h][:c])
        for h in heads:
            o_h.append(ws[h][c:] + jnp.dot(qk_buf[j * DN_HEADS + h][:, :c], v_new[h].astype(BF16),
                                           preferred_element_type=F32))
        for pr in range(DN_HEADS // 2):
            o_ref[j * c:(j + 1) * c, pr * LANES:(pr + 1) * LANES] = jnp.where(low, o_h[2 * pr], o_h[2 * pr + 1])
        nxt = []
        for h in heads:
            u = j * (DN_HEADS // 2) + h // 2
            kdt = kdec_buf[u][(h % 2) * dk:(h % 2 + 1) * dk, :]
            d = dl_buf[u][0:1, :]
            nxt.append(_delta_home(low, h, s[h] * d + jnp.dot(kdt, v_new[h].astype(BF16),
                                                               preferred_element_type=F32)))
        s = nxt
        yield
    for h in heads:
        s_ref[h] = s[h]


def _delta_chunk_kernel(q_ref, k_ref, v_ref, gate_ref, tril_ref, o_ref, sfin_ref, s_ref, *bufs, nsub):
    i = pl.program_id(1)
    half = len(bufs) // 2
    sets = (bufs[:half], bufs[half:])

    @pl.when(i == 0)
    def _():
        s_ref[...] = jnp.zeros_like(s_ref)
        for b in sets[1]:
            b[...] = jnp.zeros_like(b)

    for par in range(2):
        @pl.when(lax.rem(i, 2) == par)
        def _(par=par):
            parts = [_delta_prepare(q_ref, k_ref, v_ref, gate_ref, tril_ref, sets[par], nsub=nsub),
                     _delta_apply(sets[1 - par], o_ref, s_ref, nsub=nsub)]
            while parts:
                parts = [g for g in parts if next(g, StopIteration) is not StopIteration]

    @pl.when(i == pl.num_programs(1) - 1)
    def _():
        dk = DN_HEAD_DIM
        for h in range(DN_HEADS):
            sfin_ref[0, h] = s_ref[h][:, (h % 2) * dk:(h % 2 + 1) * dk]


def _split3_dot_left(b01, a):
    a1 = a.astype(BF16)
    r1 = a - a1.astype(F32)
    a2 = r1.astype(BF16)
    a3 = (r1 - a2.astype(F32)).astype(BF16)
    out = jnp.dot(b01, a3, preferred_element_type=F32)
    out = out + jnp.dot(b01, a2, preferred_element_type=F32)
    return out + jnp.dot(b01, a1, preferred_element_type=F32)


def _delta_prompt(q, k, v, gate, nb):
    n = q.shape[0]
    t = n // nb
    c = DN_CHUNK
    nsub = DELTA_SUBCHUNKS
    rows = nsub * c
    nc = t // rows
    row_in = lambda b, i: (b * nc + jnp.minimum(i, nc - 1), 0)
    row_out = lambda b, i: (b * nc + jnp.maximum(i - 1, 0), 0)
    tril = jnp.tril(jnp.ones((c, c), F32)).astype(BF16)
    nh = nsub * DN_HEADS
    npair = nsub * DN_HEADS // 2
    buf_set = [pltpu.VMEM((nh, c, 2 * DN_HEAD_DIM), F32),
               pltpu.VMEM((nh, 2 * c, 2 * DN_HEAD_DIM), BF16),
               pltpu.VMEM((nh, c, 2 * c), BF16),
               pltpu.VMEM((npair, 2 * DN_HEAD_DIM, c), BF16),
               pltpu.VMEM((npair, SUBLANES, LANES), F32)]
    return pl.pallas_call(
        functools.partial(_delta_chunk_kernel, nsub=nsub),
        grid=(nb, nc + 1),
        in_specs=[pl.BlockSpec((rows, DN_WIDTH), row_in),
                  pl.BlockSpec((rows, DN_WIDTH), row_in),
                  pl.BlockSpec((rows, DN_WIDTH), row_in),
                  pl.BlockSpec((rows, LANES), row_in),
                  pl.BlockSpec((c, c), lambda b, i: (0, 0))],
        out_specs=[pl.BlockSpec((rows, DN_WIDTH), row_out),
                   pl.BlockSpec((1, DN_HEADS, DN_HEAD_DIM, DN_HEAD_DIM), lambda b, i: (b, 0, 0, 0))],
        out_shape=[jax.ShapeDtypeStruct((n, DN_WIDTH), F32),
                   jax.ShapeDtypeStruct((nb, DN_HEADS, DN_HEAD_DIM, DN_HEAD_DIM), F32)],
        scratch_shapes=[pltpu.VMEM((DN_HEADS, DN_HEAD_DIM, 2 * DN_HEAD_DIM), F32)] + buf_set + buf_set,
        compiler_params=_cparams(("arbitrary", "arbitrary")),
        name="delta_prompt",
    )(q, k, v, gate, tril)


def _delta_step_kernel(q_ref, k_ref, v_ref, gate_ref, s0_ref, o_ref, s_ref, kt_ref, qt_ref, gt_ref, *, nt, nb):
    dk = DN_HEAD_DIM
    p = pl.program_id(0)
    for t in range(nt):
        rs = slice(t * nb, (t + 1) * nb)
        gt_ref[...] = gate_ref[rs, :].T
        kt_ref[...] = k_ref[rs, :].T
        qt_ref[...] = (q_ref[rs, :] * (dk ** -0.5)).T
        vt = v_ref[rs, :].T
        src = s0_ref if t == 0 else s_ref
        o_heads = []
        for j in range(2):
            a = jnp.exp(gt_ref[pl.ds(2 * p + j, 1), :])
            beta = gt_ref[pl.ds(2 * p + j + DN_HEADS, 1), :]
            base = j * dk * dk

            def k_dot_s(d, acc, j=j, base=base, src=src):
                sd = src[pl.ds(pl.multiple_of(base + d * dk, dk), dk), :]
                return acc + kt_ref[pl.ds(j * dk + d, 1), :] * sd

            ks = lax.fori_loop(0, dk, k_dot_s, jnp.zeros((dk, nb), F32), unroll=4)
            delta = beta * (vt[j * dk:(j + 1) * dk, :] - a * ks)

            def update(d, acc, j=j, base=base, src=src, a=a, delta=delta):
                r0 = pl.multiple_of(base + d * dk, dk)
                sn = a * src[pl.ds(r0, dk), :] + kt_ref[pl.ds(j * dk + d, 1), :] * delta
                s_ref[pl.ds(r0, dk), :] = sn
                return acc + qt_ref[pl.ds(j * dk + d, 1), :] * sn

            o_heads.append(lax.fori_loop(0, dk, update, jnp.zeros((dk, nb), F32), unroll=4))
        o_ref[rs, :] = jnp.concatenate(o_heads, axis=0).T


def _delta_sample(q, k, v, gate, s0t, nb, nt):
    dk = DN_HEAD_DIM
    flat = dk * dk
    n = nt * nb
    kern = functools.partial(_delta_step_kernel, nt=nt, nb=nb)
    pair = lambda p: (0, p)
    return pl.pallas_call(
        kern,
        grid=(DN_HEADS // 2,),
        in_specs=[pl.BlockSpec((n, LANES), pair),
                  pl.BlockSpec((n, LANES), pair),
                  pl.BlockSpec((n, LANES), pair),
                  pl.BlockSpec((n, LANES), lambda p: (0, 0)),
                  pl.BlockSpec((2 * flat, nb), lambda p: (p, 0))],
        out_specs=[pl.BlockSpec((n, LANES), pair),
                   pl.BlockSpec((2 * flat, nb), lambda p: (p, 0))],
        out_shape=[jax.ShapeDtypeStruct((n, DN_WIDTH), F32),
                   jax.ShapeDtypeStruct((DN_HEADS * flat, nb), F32)],
        scratch_shapes=[pltpu.VMEM((LANES, nb), F32),
                        pltpu.VMEM((LANES, nb), F32),
                        pltpu.VMEM((LANES, nb), F32)],
        compiler_params=_cparams(("arbitrary",)),
        name="delta_sample",
    )(q, k, v, gate, s0t)


def _s5_kernel(u_ref, btre_ref, btim_ref, lam_ref, ctre_ref, ctim_ref, d_ref, h0_ref, y_ref, hfin_ref,
               bw_ref, c_ref, ab_ref, x_ref, h_ref, *, nb, tt):
    p2 = S5_FLAT

    @pl.when(pl.program_id(0) == 0)
    def _():
        lr = lam_ref[0:1, :]
        li = lam_ref[1:2, :]
        dt = jnp.exp(lam_ref[2:3, :])
        mag = jnp.exp(lr * dt)
        ab_re = mag * jnp.cos(li * dt)
        ab_im = mag * jnp.sin(li * dt)
        den = lr * lr + li * li
        nr = ab_re - 1.0
        ni = ab_im
        f_re = (nr * lr + ni * li) / den
        f_im = (ni * lr - nr * li) / den
        ab_ref[0:1, :] = ab_re
        ab_ref[1:2, :] = ab_im
        gpl = LANES // S5_STATE
        ch_g = lax.broadcasted_iota(I32, (S5_WIDTH, LANES), 0) // S5_GROUP_CH
        lane_g = lax.broadcasted_iota(I32, (S5_WIDTH, LANES), 1) // S5_STATE
        bre2 = jnp.concatenate([btre_ref[...]] * gpl, axis=1)
        bim2 = jnp.concatenate([btim_ref[...]] * gpl, axis=1)
        for j in range(p2 // LANES):
            cols = slice(j * LANES, (j + 1) * LANES)
            own = ch_g == gpl * j + lane_g
            bre = jnp.where(own, bre2, 0.0)
            bim = jnp.where(own, bim2, 0.0)
            bw_ref[:, cols] = (bre * f_re[:, cols] - bim * f_im[:, cols]).astype(BF16)
            bw_ref[:, p2 + j * LANES:p2 + (j + 1) * LANES] = (bim * f_re[:, cols] + bre * f_im[:, cols]).astype(BF16)
        cpl = LANES // S5_GROUP_CH
        st_g = lax.broadcasted_iota(I32, (p2, LANES), 0) // S5_STATE
        lane_cg = lax.broadcasted_iota(I32, (p2, LANES), 1) // S5_GROUP_CH
        for j in range(S5_WIDTH // LANES):
            cols = slice(j * LANES, (j + 1) * LANES)
            own = st_g == cpl * j + lane_cg
            c_ref[0:p2, cols] = jnp.where(own, ctre_ref[...], 0.0).astype(BF16)
            c_ref[p2:2 * p2, cols] = jnp.where(own, -ctim_ref[...], 0.0).astype(BF16)
        h_ref[...] = h0_ref[...]

    u = u_ref[...]
    cw = S5_WIDTH // S5_SUPER
    sw = S5_FLAT // S5_SUPER
    for part in (0, p2):
        for b in range(S5_SUPER):
            x_ref[:, part + b * sw:part + (b + 1) * sw] = jnp.dot(
                u[:, b * cw:(b + 1) * cw], bw_ref[b * cw:(b + 1) * cw, part + b * sw:part + (b + 1) * sw],
                preferred_element_type=F32)
    a_re = ab_ref[0:1, :]
    a_im = ab_ref[1:2, :]

    if nb == SUBLANES:
        wsl = p2 // S5_SCAN_SPLIT
        for sp in range(S5_SCAN_SPLIT):
            c0 = sp * wsl
            are = jnp.broadcast_to(a_re[:, c0:c0 + wsl], (nb, wsl))
            aim = jnp.broadcast_to(a_im[:, c0:c0 + wsl], (nb, wsl))

            def step(t, carry, c0=c0, are=are, aim=aim):
                hr, hi = carry
                r0 = pl.multiple_of(t * nb, nb)
                nr = are * hr - aim * hi + x_ref[pl.ds(r0, nb), c0:c0 + wsl]
                ni = are * hi + aim * hr + x_ref[pl.ds(r0, nb), p2 + c0:p2 + c0 + wsl]
                x_ref[pl.ds(r0, nb), c0:c0 + wsl] = nr
                x_ref[pl.ds(r0, nb), p2 + c0:p2 + c0 + wsl] = ni
                return nr, ni

            hr, hi = lax.fori_loop(0, tt, step, (h_ref[:, c0:c0 + wsl], h_ref[:, p2 + c0:p2 + c0 + wsl]),
                                   unroll=2)
            h_ref[:, c0:c0 + wsl] = hr
            h_ref[:, p2 + c0:p2 + c0 + wsl] = hi
    else:
        for t in range(tt):
            rs = slice(t * nb, (t + 1) * nb)
            hr = h_ref[:, 0:p2]
            hi = h_ref[:, p2:2 * p2]
            nr = a_re * hr - a_im * hi + x_ref[rs, 0:p2]
            ni = a_re * hi + a_im * hr + x_ref[rs, p2:2 * p2]
            h_ref[:, 0:p2] = nr
            h_ref[:, p2:2 * p2] = ni
            x_ref[rs, 0:p2] = nr
            x_ref[rs, p2:2 * p2] = ni

    for b in range(S5_SUPER):
        cols = slice(b * cw, (b + 1) * cw)
        y = None
        for part in (0, p2):
            rws = slice(part + b * sw, part + (b + 1) * sw)
            term = jnp.dot(x_ref[:, rws].astype(BF16), c_ref[rws, cols], preferred_element_type=F32)
            y = term if y is None else y + term
        y_ref[:, cols] = y + d_ref[:, cols] * u[:, cols].astype(F32)
    hfin_ref[...] = h_ref[...]


def _s5(u, params, h0, nb):
    btre, btim, lam, ctre, ctim, dvec = params
    n = u.shape[0]
    t = n // nb
    tt = min(ROW_TILE // nb, t)
    rows = tt * nb
    const = lambda i: (0, 0)
    kern = functools.partial(_s5_kernel, nb=nb, tt=tt)
    return pl.pallas_call(
        kern,
        grid=(t // tt,),
        in_specs=[pl.BlockSpec((rows, S5_WIDTH), lambda i: (i, 0)),
                  pl.BlockSpec((S5_WIDTH, S5_STATE), const),
                  pl.BlockSpec((S5_WIDTH, S5_STATE), const),
                  pl.BlockSpec((SUBLANES, S5_FLAT), const),
                  pl.BlockSpec((S5_FLAT, LANES), const),
                  pl.BlockSpec((S5_FLAT, LANES), const),
                  pl.BlockSpec((1, S5_WIDTH), const),
                  pl.BlockSpec((nb, 2 * S5_FLAT), const)],
        out_specs=[pl.BlockSpec((rows, S5_WIDTH), lambda i: (i, 0)),
                   pl.BlockSpec((nb, 2 * S5_FLAT), const)],
        out_shape=[jax.ShapeDtypeStruct((n, S5_WIDTH), F32),
                   jax.ShapeDtypeStruct((nb, 2 * S5_FLAT), F32)],
        scratch_shapes=[pltpu.VMEM((S5_WIDTH, 2 * S5_FLAT), BF16),
                        pltpu.VMEM((2 * S5_FLAT, S5_WIDTH), BF16),
                        pltpu.VMEM((SUBLANES, S5_FLAT), F32),
                        pltpu.VMEM((rows, 2 * S5_FLAT), F32),
                        pltpu.VMEM((nb, 2 * S5_FLAT), F32)],
        compiler_params=_cparams(("arbitrary",)),
        name="s5",
    )(u, btre, btim, lam, ctre, ctim, dvec, h0)


def _postmix_kernel(xp_ref, op_ref, zp_ref, ysp_ref, gap_ref, gbp_ref,
                    xs_ref, os_ref, zs_ref, yss_ref, gas_ref, gbs_ref, *rest, nblk_p, range_tok):
    carry_ref = rest[-1]

    @pl.when(pl.program_id(0) == 0)
    def _():
        carry_ref[...] = jnp.zeros_like(carry_ref)

    @pl.when(pl.program_id(0) < nblk_p)
    def _():
        _postmix_body(xp_ref, op_ref, zp_ref, ysp_ref, gap_ref, gbp_ref, *rest, range_tok=range_tok)

    @pl.when(pl.program_id(0) >= nblk_p)
    def _():
        _postmix_body(xs_ref, os_ref, zs_ref, yss_ref, gas_ref, gbs_ref, *rest, range_tok=range_tok)


def _postmix_body(x_ref, o_ref, z_ref, ys_ref, ga_ref, gb_ref, hw_ref, seg_ref, wa_ref, wglu_ref, wb_ref,
                  wo_ref, nf_ref, wr_ref, su_ref, x1_ref, hn_ref, bkt_ref, rank_ref, rw_ref, cnt_ref, carry_ref,
                  *, range_tok):
    o = o_ref[...]
    ms = jnp.dot((o * o).astype(BF16), seg_ref[...], preferred_element_type=F32) * (1.0 / DN_HEAD_DIM)
    on = o * lax.rsqrt(ms + RMS_EPS) * hw_ref[...]
    z = z_ref[...]
    oa = on * (z * jax.nn.sigmoid(z)).astype(F32)
    y_a = _mm(oa, wa_ref[...])
    ys = jax.nn.gelu(ys_ref[...])
    ys = ys * jax.nn.sigmoid(_mm(ys, wglu_ref[...]))
    y_b = _mm(ys, wb_ref[...])
    mixed = jax.nn.sigmoid(ga_ref[...]).astype(F32) * y_a + jax.nn.sigmoid(gb_ref[...]).astype(F32) * y_b
    x1 = x_ref[...] + _mm(mixed, wo_ref[...])
    x1_ref[...] = x1
    hn = x1 * lax.rsqrt(jnp.mean(x1 * x1, axis=-1, keepdims=True) + RMS_EPS) * nf_ref[...]
    _slab_store(hn_ref, hn)

    wr = wr_ref[...]
    w_hi = wr.astype(BF16)
    w_lo = (wr - w_hi.astype(F32)).astype(BF16)
    hn_hi = hn.astype(BF16)
    hn_lo = (hn - hn_hi.astype(F32)).astype(BF16)
    both = _mm_nt(jnp.concatenate([w_hi, w_lo], axis=0), hn_hi)
    logits = both[:ROUTER_ROWS] + both[ROUTER_ROWS:] + _mm_nt(w_hi, hn_lo)
    coarse = logits[N_EXPERTS:N_EXPERTS + MOE_GROUPS, :]
    cm = jnp.max(coarse, axis=0, keepdims=True)
    ce = jnp.exp(coarse - cm)
    pc = ce / jnp.sum(ce, axis=0, keepdims=True)
    p_sel = jnp.max(pc, axis=0, keepdims=True)
    gi = lax.broadcasted_iota(I32, pc.shape, 0)
    g_sel = jnp.min(jnp.where(pc == p_sel, gi, MOE_GROUPS), axis=0, keepdims=True)
    fine = jnp.zeros((EXPERTS_PER_GROUP, logits.shape[1]), F32)
    for g in range(MOE_GROUPS):
        fine = fine + jnp.where(g_sel == g, logits[g * EXPERTS_PER_GROUP:(g + 1) * EXPERTS_PER_GROUP, :], 0.0)
    fm = jnp.max(fine, axis=0, keepdims=True)
    fe = jnp.exp(fine - fm)
    pf = fe / jnp.sum(fe, axis=0, keepdims=True)
    ei = lax.broadcasted_iota(I32, pf.shape, 0)
    v1 = jnp.max(pf, axis=0, keepdims=True)
    i1 = jnp.min(jnp.where(pf == v1, ei, EXPERTS_PER_GROUP), axis=0, keepdims=True)
    rest = jnp.where(ei == i1, -1.0, pf)
    v2 = jnp.max(rest, axis=0, keepdims=True)
    i2 = jnp.min(jnp.where(rest == v2, ei, EXPERTS_PER_GROUP), axis=0, keepdims=True)
    tot = v1 + v2
    rw_ref[0:1, :] = v1 / tot * p_sel
    rw_ref[1:2, :] = v2 / tot * p_sel

    tt = logits.shape[1]
    tok = pl.program_id(0) * tt + lax.broadcasted_iota(I32, (1, tt), 1)
    ph = jnp.zeros((1, tt), I32)
    for r in range(1, MOE_PHASES):
        ph = ph + (tok >= r * range_tok).astype(I32)
    bsel = [ph * N_EXPERTS + g_sel * EXPERTS_PER_GROUP + ix for ix in (i1, i2)]
    bi = lax.broadcasted_iota(I32, (MOE_PHASES * N_EXPERTS, tt), 0)
    onehot = [(bi == b).astype(F32) for b in bsel]
    cnt = onehot[0] + onehot[1]
    before = carry_ref[:, 0:1] + jnp.dot(cnt.astype(BF16), su_ref[...], preferred_element_type=F32)
    for s in range(TOP_K):
        bkt_ref[s:s + 1, :] = bsel[s]
        rank_ref[s:s + 1, :] = jnp.sum(onehot[s] * before, axis=0, keepdims=True).astype(I32)
    carry_ref[...] = carry_ref[...] + jnp.sum(cnt, axis=1, keepdims=True)
    cnt_ref[...] = carry_ref[...]


def _postmix(prompt, sample, weights, nb):
    n_p = prompt[0].shape[0]
    n_s = sample[0].shape[0]
    t = n_p // nb
    tt = min(ROW_TILE, t, n_s)
    nt = t // tt
    nblk_p = n_p // tt
    nblk = nblk_p + n_s // tt
    n_total = n_p + n_s
    prow = lambda i: (jnp.minimum(i, nblk_p - 1), 0)
    pys = lambda i: (jnp.minimum(i, nblk_p - 1) % nt, jnp.minimum(i, nblk_p - 1) // nt)
    srow = lambda i: (jnp.maximum(i - nblk_p, 0), 0)
    const = lambda i: (0, 0)

    def stream_specs(row, ysmap):
        return [pl.BlockSpec((tt, D_MODEL), row),
                pl.BlockSpec((tt, DN_WIDTH), row),
                pl.BlockSpec((tt, DN_WIDTH), row),
                pl.BlockSpec((tt, S5_WIDTH), ysmap),
                pl.BlockSpec((tt, D_MODEL), row),
                pl.BlockSpec((tt, D_MODEL), row)]

    weight_specs = [pl.BlockSpec((1, DN_WIDTH), const),
                    pl.BlockSpec((DN_WIDTH, DN_WIDTH), const),
                    pl.BlockSpec((DN_WIDTH, D_MODEL), const),
                    pl.BlockSpec((S5_WIDTH, S5_WIDTH), const),
                    pl.BlockSpec((S5_WIDTH, D_MODEL), const),
                    pl.BlockSpec((D_MODEL, D_MODEL), const),
                    pl.BlockSpec((1, D_MODEL), const),
                    pl.BlockSpec((ROUTER_ROWS, D_MODEL), const),
                    pl.BlockSpec((tt, tt), const)]
    xp, op, zp, ysp, gap, gbp = prompt
    nbk = MOE_PHASES * N_EXPERTS
    earlier = jnp.triu(jnp.ones((tt, tt), F32), k=1).astype(BF16)
    return pl.pallas_call(
        functools.partial(_postmix_kernel, nblk_p=nblk_p, range_tok=n_total // MOE_PHASES),
        grid=(nblk,),
        in_specs=stream_specs(prow, pys) + stream_specs(srow, srow) + weight_specs,
        out_specs=[pl.BlockSpec((tt, D_MODEL), lambda i: (i, 0)),
                   pl.BlockSpec((tt * ROW_SLAB, LANES), lambda i: (i, 0)),
                   pl.BlockSpec((TOP_K, tt), lambda i: (0, i)),
                   pl.BlockSpec((TOP_K, tt), lambda i: (0, i)),
                   pl.BlockSpec((TOP_K, tt), lambda i: (0, i)),
                   pl.BlockSpec((nbk, LANES), const)],
        out_shape=[jax.ShapeDtypeStruct((n_total, D_MODEL), F32),
                   jax.ShapeDtypeStruct((n_total * ROW_SLAB, LANES), F32),
                   jax.ShapeDtypeStruct((TOP_K, n_total), I32),
                   jax.ShapeDtypeStruct((TOP_K, n_total), I32),
                   jax.ShapeDtypeStruct((TOP_K, n_total), F32),
                   jax.ShapeDtypeStruct((nbk, LANES), F32)],
        scratch_shapes=[pltpu.VMEM((nbk, LANES), F32)],
        compiler_params=_cparams(("arbitrary",)),
        name="postmix",
    )(xp, op, zp, ysp.reshape(t, nb * S5_WIDTH), gap, gbp, *sample, *weights, earlier)


def _wait_slabs(buf, sem):
    pltpu.make_async_copy(buf, buf, sem).wait()


def _moe_kernel(texp_ref, tph_ref, tsrc_ref, tnv_ref, tfirst_ref, tslot_ref, tnext_ref, otok_ref,
                hn_hbm, wu_hbm, wd_hbm, y_ref, hnv, xbuf, wu_buf, wd_buf, wub, wdb, sem, wsem):
    i = pl.program_id(0)
    tm = MOE_TILE
    rs = ROW_SLAB
    nv = tnv_ref[i]
    ph = tph_ref[i]
    range_rows = hnv.shape[0]

    def weight_copies(e, sl):
        return (pltpu.make_async_copy(wu_hbm.at[e], wu_buf.at[sl], wsem.at[sl]),
                pltpu.make_async_copy(wd_hbm.at[e], wd_buf.at[sl], wsem.at[sl]))

    @pl.when(i == 0)
    def _():
        for c in weight_copies(texp_ref[0], 0):
            c.start()

    @pl.when(jnp.logical_and(nv > 0, jnp.logical_or(i == 0, ph != tph_ref[jnp.maximum(i - 1, 0)])))
    def _():
        start = pl.multiple_of(ph * range_rows, rs)
        whole = pltpu.make_async_copy(hn_hbm.at[pl.ds(start, range_rows), :], hnv, sem)
        whole.start()
        whole.wait()

    for sl in range(2):
        @pl.when(jnp.logical_and(jnp.logical_and(nv > 0, tfirst_ref[i] == 1), tslot_ref[i] == sl))
        def _():
            for c in weight_copies(texp_ref[i], sl):
                c.wait()

            @pl.when(tnext_ref[i] >= 0)
            def _():
                for c in weight_copies(tnext_ref[i], 1 - sl):
                    c.start()

            wub[...] = wu_buf[sl].astype(BF16)
            wdb[...] = wd_buf[sl].astype(BF16)

    @pl.when(nv == 0)
    def _():
        y_ref[...] = jnp.zeros_like(y_ref)

    @pl.when(nv > 0)
    def _():
        src0 = tsrc_ref[i]
        for r in range(tm):
            tok8 = pl.multiple_of(otok_ref[src0 + r], rs)
            xbuf[pl.ds(r * rs, rs), :] = hnv[pl.ds(tok8, rs), :]
        x = _slab_load(xbuf, tm).astype(BF16)
        hu = jnp.dot(x, wub[...], preferred_element_type=F32)
        gate = hu[:, :EXPERT_FF]
        up = hu[:, EXPERT_FF:]
        act = gate * jax.nn.sigmoid(gate) * up
        _slab_store(y_ref, jnp.dot(act.astype(BF16), wdb[...], preferred_element_type=F32))


def _moe(hn, w_up, w_down, plan):
    ntiles = plan[0].shape[0]
    grid_spec = pltpu.PrefetchScalarGridSpec(
        num_scalar_prefetch=len(plan),
        grid=(ntiles,),
        in_specs=[pl.BlockSpec(memory_space=pl.ANY),
                  pl.BlockSpec(memory_space=pl.ANY),
                  pl.BlockSpec(memory_space=pl.ANY)],
        out_specs=pl.BlockSpec((MOE_TILE * ROW_SLAB, LANES), lambda i, *_: (i, 0)),
        scratch_shapes=[pltpu.VMEM((hn.shape[0] // MOE_PHASES, LANES), F32),
                        pltpu.VMEM((MOE_TILE * ROW_SLAB, LANES), F32),
                        pltpu.VMEM((2, D_MODEL, 2 * EXPERT_FF), F32),
                        pltpu.VMEM((2, EXPERT_FF, D_MODEL), F32),
                        pltpu.VMEM((D_MODEL, 2 * EXPERT_FF), BF16),
                        pltpu.VMEM((EXPERT_FF, D_MODEL), BF16),
                        pltpu.SemaphoreType.DMA,
                        pltpu.SemaphoreType.DMA((2,))])
    return pl.pallas_call(
        _moe_kernel,
        grid_spec=grid_spec,
        out_shape=jax.ShapeDtypeStruct((ntiles * MOE_TILE * ROW_SLAB, LANES), F32),
        compiler_params=_cparams(("arbitrary",)),
        name="moe",
    )(*plan, hn, w_up, w_down)


def _combine_kernel(pos_ref, x1_ref, ys_hbm, w_ref, nw_ref, outp_ref, outs_ref,
                    ybuf0, ybuf1, sem, *, nblk_p, n_tok):
    i = pl.program_id(0)
    nsteps = pl.num_programs(0)
    tt = x1_ref.shape[0]
    rs = ROW_SLAB
    slot = lax.rem(i, 2)
    ybuf = (ybuf0, ybuf1)

    def start_gather(step, sl):
        base = step * tt
        for r in range(tt * TOP_K):
            j, s = divmod(r, TOP_K)
            p8 = pl.multiple_of(pos_ref[s * n_tok + base + j], rs)
            pltpu.make_async_copy(ys_hbm.at[pl.ds(p8, rs), :], ybuf[sl].at[pl.ds((s * tt + j) * rs, rs), :],
                                  sem.at[sl]).start(priority=r % DMA_QUEUES)

    @pl.when(i == 0)
    def _():
        start_gather(0, 0)

    for sl in range(2):
        @pl.when(slot == sl)
        def _():
            _wait_slabs(ybuf[sl], sem.at[sl])
            start_gather(jnp.minimum(i + 1, nsteps - 1), 1 - sl)
            w = w_ref[...]
            y0 = _slab_load(ybuf[sl], tt, 0)
            y1 = _slab_load(ybuf[sl], tt, tt * rs)
            x = x1_ref[...] + w[:, 0:1] * y0 + w[:, 1:2] * y1
            res = x * lax.rsqrt(jnp.mean(x * x, axis=-1, keepdims=True) + RMS_EPS) * nw_ref[...]

            @pl.when(i < nblk_p)
            def _():
                outp_ref[...] = res

            @pl.when(i >= nblk_p)
            def _():
                outs_ref[...] = res

        @pl.when(jnp.logical_and(slot == sl, i == nsteps - 1))
        def _():
            _wait_slabs(ybuf[1 - sl], sem.at[1 - sl])


def _combine(x1, ysorted, pos8, wtok, nw, n_p):
    n = x1.shape[0]
    tt = math.gcd(math.gcd(n_p, n - n_p), COMBINE_TILE)
    nblk_p = n_p // tt
    grid_spec = pltpu.PrefetchScalarGridSpec(
        num_scalar_prefetch=1,
        grid=(n // tt,),
        in_specs=[pl.BlockSpec((tt, D_MODEL), lambda i, *_: (i, 0)),
                  pl.BlockSpec(memory_space=pl.ANY),
                  pl.BlockSpec((tt, TOP_K), lambda i, *_: (i, 0)),
                  pl.BlockSpec((1, D_MODEL), lambda i, *_: (0, 0))],
        out_specs=[pl.BlockSpec((tt, D_MODEL), lambda i, *_: (jnp.minimum(i, nblk_p - 1), 0)),
                   pl.BlockSpec((tt, D_MODEL), lambda i, *_: (jnp.maximum(i - nblk_p, 0), 0))],
        scratch_shapes=[pltpu.VMEM((tt * TOP_K * ROW_SLAB, LANES), F32),
                        pltpu.VMEM((tt * TOP_K * ROW_SLAB, LANES), F32),
                        pltpu.SemaphoreType.DMA((2,))])
    return pl.pallas_call(
        functools.partial(_combine_kernel, nblk_p=nblk_p, n_tok=n),
        grid_spec=grid_spec,
        out_shape=[jax.ShapeDtypeStruct((n_p, D_MODEL), F32),
                   jax.ShapeDtypeStruct((n - n_p, D_MODEL), F32)],
        compiler_params=_cparams(("arbitrary",)),
        name="combine",
    )(pos8, x1, ysorted, wtok, nw)


def _route_plan(bkt, rank, cnt, n_tok):
    tm = MOE_TILE
    n_assign = n_tok * TOP_K
    nbk = MOE_PHASES * N_EXPERTS
    ntiles = n_assign // tm + nbk
    range_tok = n_tok // MOE_PHASES
    b_flat = bkt.T.reshape(n_assign)
    order = jnp.argsort(b_flat, stable=True).astype(I32)
    counts = cnt[:, 0].astype(I32)
    cstart = jnp.cumsum(counts) - counts
    tiles_b = (counts + tm - 1) // tm
    tend = jnp.cumsum(tiles_b)
    tstart = tend - tiles_b
    tile_id = jnp.arange(ntiles, dtype=I32)
    tbk = jnp.minimum(jnp.sum((tile_id[:, None] >= tend[None, :]).astype(I32), axis=1), nbk - 1)
    onehot = (tbk[:, None] == jnp.arange(nbk, dtype=I32)[None, :]).astype(I32)
    pick = lambda v: jnp.sum(onehot * v[None, :], axis=1)
    done = (tile_id - pick(tstart)) * tm
    tnv = jnp.where(tile_id < tend[-1], jnp.clip(pick(counts) - done, 0, tm), 0)
    tsrc = jnp.where(tnv > 0, pick(cstart) + done, 0)
    texp = tbk % N_EXPERTS
    tph = tbk // N_EXPERTS
    nonempty = counts > 0
    bslot = (jnp.cumsum(nonempty.astype(I32)) - 1) % 2
    bidx = jnp.where(nonempty, jnp.arange(nbk, dtype=I32), nbk)
    nxt = jnp.concatenate([lax.cummin(bidx[::-1])[::-1][1:], jnp.full((1,), nbk, I32)])
    bnext = jnp.where(nxt < nbk, nxt % N_EXPERTS, -1)
    tfirst = jnp.logical_and(tnv > 0, done == 0).astype(I32)
    tslot = pick(bslot)
    tnext = pick(bnext)
    otok8 = jnp.concatenate([((order // TOP_K) % range_tok) * ROW_SLAB, jnp.zeros((tm,), I32)])
    plan = tuple(a.astype(I32) for a in (texp, tph, tsrc, tnv, tfirst, tslot, tnext, otok8))
    first = jnp.sum((bkt[:, :, None] == jnp.arange(nbk, dtype=I32)[None, None, :]).astype(I32)
                    * (tstart * tm)[None, None, :], axis=2)
    pos8 = ((first + rank) * ROW_SLAB).reshape(n_assign)
    return plan, pos8.astype(I32)


def _block_diag(m):
    g, a, b = m.shape
    eye = jnp.eye(g, dtype=m.dtype)
    return (eye[:, None, :, None] * m[:, :, None, :]).reshape(g * a, g * b)


def kernel(x_prompt, x_sample, state_conv, state_delta, state_ssm_re, state_ssm_im, norm_mix_w, w_in, conv_w, a_log, dt_bias, head_norm_w, w_a_up, s5_lambda_re, s5_lambda_im, s5_log_step, s5_b_re, s5_b_im, s5_c_re, s5_c_im, s5_d, w_glu, w_b_up, w_o, norm_ffn_w, w_router_coarse, w_router_fine, w_expert_up, w_expert_down, norm_final_w):
    bp, tp, _ = x_prompt.shape
    bs, ts, _ = x_sample.shape
    n_p = bp * tp
    n_s = bs * ts
    n_tok = n_p + n_s
    l = 0

    w = w_in[l].astype(BF16)
    c_ab = W1_COLS + 2 * DN_HEADS
    w_parts = (w[:, :W1_COLS], w[:, c_ab:],
               jnp.concatenate([w[:, W1_COLS:c_ab], jnp.zeros((D_MODEL, LANES - 2 * DN_HEADS), BF16)], axis=1))
    nw_mix = norm_mix_w[l].reshape(1, D_MODEL)
    pad8 = lambda v: jnp.concatenate([v, jnp.zeros((LANES - DN_HEADS,), F32)]).reshape(1, LANES)
    gate_p = jnp.concatenate([pad8(a_log[l]), pad8(dt_bias[l])], axis=0)
    seg = _block_diag(jnp.ones((DN_HEADS, DN_HEAD_DIM, DN_HEAD_DIM), BF16))
    chan_rows = lambda b: jnp.swapaxes(b, 1, 2).reshape(S5_WIDTH, S5_STATE)
    state_rows = lambda c: jnp.tile(jnp.swapaxes(c, 1, 2).reshape(S5_FLAT, S5_GROUP_CH),
                                    (1, LANES // S5_GROUP_CH))
    lam = jnp.concatenate([s5_lambda_re[l].reshape(1, S5_FLAT), s5_lambda_im[l].reshape(1, S5_FLAT),
                           jnp.repeat(s5_log_step[l], S5_STATE).reshape(1, S5_FLAT),
                           jnp.zeros((SUBLANES - 3, S5_FLAT), F32)], axis=0)
    s5_params = (chan_rows(s5_b_re[l]), chan_rows(s5_b_im[l]), lam,
                 state_rows(s5_c_re[l]), state_rows(s5_c_im[l]), s5_d[l].reshape(1, S5_WIDTH))
    hw = jnp.tile(head_norm_w[l], DN_HEADS).reshape(1, DN_WIDTH)
    wr = jnp.concatenate([w_router_fine[l].T, w_router_coarse[l].T,
                          jnp.zeros((ROUTER_ROWS - N_EXPERTS - MOE_GROUPS, D_MODEL), F32)], axis=0)
    pm_weights = (hw, seg, w_a_up[l].astype(BF16), w_glu[l].astype(BF16), w_b_up[l].astype(BF16),
                  w_o[l].astype(BF16), norm_ffn_w[l].reshape(1, D_MODEL), wr)

    xp2 = x_prompt.reshape(n_p, D_MODEL)
    q_p, k_p, v_p, gates_p, conv_p, z_p, u_p, ga_p, gb_p = _inprep(
        xp2, nw_mix, w_parts, jnp.zeros((bp, SUBLANES, QKV_DIM), F32), conv_w[l], gate_p, seg, bp, 1)
    o_p, delta_p = _delta_prompt(q_p, k_p, v_p, gates_p, bp)
    ys_p, h_p = _s5(u_p, s5_params, jnp.zeros((bp, 2 * S5_FLAT), F32), bp)

    xs2 = jnp.swapaxes(x_sample, 0, 1).reshape(n_s, D_MODEL)
    cinit_s = jnp.swapaxes(state_conv[l], 0, 1).reshape(1, (CONV_W - 1) * bs, QKV_DIM)
    q_s, k_s, v_s, gate_s, conv_s, z_s, u_s, ga_s, gb_s = _inprep(
        xs2, nw_mix, w_parts, cinit_s, conv_w[l], gate_p, seg, 1, bs)
    s0t = jnp.transpose(state_delta[l], (1, 2, 3, 0)).reshape(DN_HEADS * DN_HEAD_DIM * DN_HEAD_DIM, bs)
    o_s, delta_st = _delta_sample(q_s, k_s, v_s, gate_s, s0t, bs, ts)
    delta_s = jnp.transpose(delta_st.reshape(DN_HEADS, DN_HEAD_DIM, DN_HEAD_DIM, bs), (3, 0, 1, 2))
    h0_s = jnp.concatenate([state_ssm_re[l].reshape(bs, S5_FLAT), state_ssm_im[l].reshape(bs, S5_FLAT)], axis=1)
    ys_s, h_s = _s5(u_s, s5_params, h0_s, bs)
    x1, hn, bkt, rank, rw, cnt = _postmix((xp2, o_p, z_p, ys_p, ga_p, gb_p), (xs2, o_s, z_s, ys_s, ga_s, gb_s),
                                          pm_weights, bp)

    plan, pos8 = _route_plan(bkt, rank, cnt, n_tok)
    ysorted = _moe(hn, w_expert_up[l], w_expert_down[l], plan)
    y_p, y_s = _combine(x1, ysorted, pos8, rw.T, norm_final_w.reshape(1, D_MODEL), n_p)

    y_prompt = y_p.reshape(bp, tp, D_MODEL)
    y_sample = jnp.swapaxes(y_s.reshape(ts, bs, D_MODEL), 0, 1)
    conv_sample = jnp.swapaxes(conv_s.reshape(CONV_W - 1, bs, QKV_DIM), 0, 1)
    return (y_prompt, y_sample,
            conv_p[None], delta_p[None],
            h_p[:, :S5_FLAT].reshape(1, bp, S5_GROUPS, S5_STATE), h_p[:, S5_FLAT:].reshape(1, bp, S5_GROUPS, S5_STATE),
            conv_sample[None], delta_s[None],
            h_s[:, :S5_FLAT].reshape(1, bs, S5_GROUPS, S5_STATE), h_s[:, S5_FLAT:].reshape(1, bs, S5_GROUPS, S5_STATE))
```

```python
import functools
import math

import jax
import jax.numpy as jnp
import numpy as np
from jax import lax
from jax.experimental import pallas as pl
from jax.experimental.pallas import tpu as pltpu

F32 = jnp.float32
BF16 = jnp.bfloat16
I32 = jnp.int32

D_MODEL = 1024
DN_HEADS = 8
DN_HEAD_DIM = 64
DN_WIDTH = DN_HEADS * DN_HEAD_DIM
QKV_DIM = 3 * DN_WIDTH
CONV_W = 4
DN_CHUNK = 64
S5_GROUP_CH = 16
S5_WIDTH = D_MODEL // 2
S5_GROUPS = S5_WIDTH // S5_GROUP_CH
S5_STATE = 64
S5_FLAT = S5_GROUPS * S5_STATE
MOE_GROUPS = 4
EXPERTS_PER_GROUP = 8
N_EXPERTS = MOE_GROUPS * EXPERTS_PER_GROUP
TOP_K = 2
EXPERT_FF = 256
RMS_EPS = 1e-6
L2_EPS = 1e-6

LANES = 128
SUBLANES = 8
VMEM_LIMIT = 56 * 1024 * 1024

W1_COLS = QKV_DIM + DN_WIDTH
W2_COLS = S5_WIDTH + 2 * D_MODEL

ROW_TILE = 512
INPREP_PARTS = 2
MOE_TILE = 256
MOE_PHASES = 2
COMBINE_TILE = 256
DMA_QUEUES = 2
DELTA_SUBCHUNKS = 4
S5_SUPER = 2
S5_SCAN_SPLIT = 2
ROUTER_ROWS = 40


def _mm(a, b):
    return jnp.dot(a.astype(BF16), b.astype(BF16), preferred_element_type=F32)


def _mm_nt(a, b):
    return lax.dot_general(a.astype(BF16), b.astype(BF16), (((1,), (1,)), ((), ())),
                           preferred_element_type=F32)


def _split3_dot(a, b01):
    a1 = a.astype(BF16)
    r1 = a - a1.astype(F32)
    a2 = r1.astype(BF16)
    a3 = (r1 - a2.astype(F32)).astype(BF16)
    out = jnp.dot(a3, b01, preferred_element_type=F32)
    out = out + jnp.dot(a2, b01, preferred_element_type=F32)
    return out + jnp.dot(a1, b01, preferred_element_type=F32)


def _cparams(sem):
    return pltpu.CompilerParams(dimension_semantics=sem, vmem_limit_bytes=VMEM_LIMIT)


ROW_SLAB = D_MODEL // LANES


def _slab_load(ref, rows, first=0, pitch=ROW_SLAB):
    return jnp.concatenate([ref[pl.ds(first + j, rows, stride=pitch), :] for j in range(ROW_SLAB)], axis=1)


def _slab_store(ref, x):
    for j in range(ROW_SLAB):
        ref[pl.ds(j, x.shape[0], stride=ROW_SLAB), :] = x[:, j * LANES:(j + 1) * LANES]


def _softplus(x):
    return jnp.maximum(x, 0.0) + jnp.log1p(jnp.exp(-jnp.abs(x)))


def _inprep_kernel(x_ref, nw_ref, w1_ref, w2_ref, wab_ref, cinit_ref, cw_ref, gp_ref, seg_ref,
                   q_ref, k_ref, v_ref, gate_ref, cnew_ref, z_ref, u_ref, ga_ref, gb_ref, xp_ref,
                   *, shift, rc, rows):
    @pl.when(pl.program_id(1) == 0)
    def _():
        xp_ref[0:rc, :] = cinit_ref[0]

    seg = seg_ref[...]
    pr = rows // INPREP_PARTS
    for part in range(INPREP_PARTS):
        rs = slice(part * pr, (part + 1) * pr)
        x = x_ref[rs, :]
        h = x * lax.rsqrt(jnp.mean(x * x, axis=-1, keepdims=True) + RMS_EPS) * nw_ref[...]
        hb = h.astype(BF16)

        def proj(w_ref, lo, hi, hb=hb):
            return jnp.dot(hb, w_ref[:, lo:hi], preferred_element_type=F32)

        xp_ref[rc + part * pr:rc + (part + 1) * pr, :] = proj(w1_ref, 0, QKV_DIM)
        ab = proj(wab_ref, 0, LANES)
        z_ref[rs, :] = proj(w1_ref, QKV_DIM, W1_COLS).astype(z_ref.dtype)
        u_ref[rs, :] = proj(w2_ref, 0, S5_WIDTH).astype(u_ref.dtype)
        ga_ref[rs, :] = proj(w2_ref, S5_WIDTH, S5_WIDTH + D_MODEL).astype(ga_ref.dtype)
        gb_ref[rs, :] = proj(w2_ref, S5_WIDTH + D_MODEL, W2_COLS).astype(gb_ref.dtype)
        acc = None
        for i in range(CONV_W):
            lo = rc + part * pr + (i - (CONV_W - 1)) * shift
            term = xp_ref[lo:lo + pr, :] * cw_ref[i:i + 1, :]
            acc = term if acc is None else acc + term
        y = acc * jax.nn.sigmoid(acc)
        q = y[:, 0:DN_WIDTH]
        k = y[:, DN_WIDTH:2 * DN_WIDTH]
        q_ref[rs, :] = q * lax.rsqrt(jnp.dot((q * q).astype(BF16), seg, preferred_element_type=F32) + L2_EPS)
        k_ref[rs, :] = k * lax.rsqrt(jnp.dot((k * k).astype(BF16), seg, preferred_element_type=F32) + L2_EPS)
        v_ref[rs, :] = y[:, 2 * DN_WIDTH:]
        g = -jnp.exp(gp_ref[0:1, :]) * _softplus(ab + gp_ref[1:2, :])
        beta = jax.nn.sigmoid(ab)
        lane = lax.broadcasted_iota(I32, ab.shape, 1)
        gate_ref[rs, :] = jnp.where(lane < DN_HEADS, g, beta)

    keep = (CONV_W - 1) * shift
    cnew_ref[0] = xp_ref[rc + rows - keep:rc + rows, :]
    xp_ref[0:rc, :] = xp_ref[rows:rows + rc, :]


def _inprep(x2d, nw, w_parts, cinit, conv_w, gate_p, seg, nb, shift):
    n = x2d.shape[0]
    r = n // nb
    rows = min(ROW_TILE, r)
    nt = r // rows
    rc = cinit.shape[1]
    keep = (CONV_W - 1) * shift
    row = lambda b, i: (b * nt + i, 0)
    const = lambda b, i: (0, 0)
    kern = functools.partial(_inprep_kernel, shift=shift, rc=rc, rows=rows)
    outs = pl.pallas_call(
        kern,
        grid=(nb, nt),
        in_specs=[pl.BlockSpec((rows, D_MODEL), row),
                  pl.BlockSpec((1, D_MODEL), const),
                  pl.BlockSpec((D_MODEL, W1_COLS), const),
                  pl.BlockSpec((D_MODEL, W2_COLS), const),
                  pl.BlockSpec((D_MODEL, LANES), const),
                  pl.BlockSpec((1, rc, QKV_DIM), lambda b, i: (b, 0, 0)),
                  pl.BlockSpec((CONV_W, QKV_DIM), const),
                  pl.BlockSpec((2, LANES), const),
                  pl.BlockSpec((DN_WIDTH, DN_WIDTH), const)],
        out_specs=[pl.BlockSpec((rows, DN_WIDTH), row),
                   pl.BlockSpec((rows, DN_WIDTH), row),
                   pl.BlockSpec((rows, DN_WIDTH), row),
                   pl.BlockSpec((rows, LANES), row),
                   pl.BlockSpec((1, keep, QKV_DIM), lambda b, i: (b, 0, 0)),
                   pl.BlockSpec((rows, DN_WIDTH), row),
                   pl.BlockSpec((rows, S5_WIDTH), lambda b, i: (i, b)),
                   pl.BlockSpec((rows, D_MODEL), row),
                   pl.BlockSpec((rows, D_MODEL), row)],
        out_shape=[jax.ShapeDtypeStruct((n, DN_WIDTH), F32),
                   jax.ShapeDtypeStruct((n, DN_WIDTH), F32),
                   jax.ShapeDtypeStruct((n, DN_WIDTH), F32),
                   jax.ShapeDtypeStruct((n, LANES), F32),
                   jax.ShapeDtypeStruct((nb, keep, QKV_DIM), F32),
                   jax.ShapeDtypeStruct((n, DN_WIDTH), BF16),
                   jax.ShapeDtypeStruct((r, nb * S5_WIDTH), BF16),
                   jax.ShapeDtypeStruct((n, D_MODEL), BF16),
                   jax.ShapeDtypeStruct((n, D_MODEL), BF16)],
        scratch_shapes=[pltpu.VMEM((rc + rows, QKV_DIM), F32)],
        compiler_params=_cparams(("arbitrary", "arbitrary")),
        name="inprep",
    )(x2d, nw, *w_parts, cinit, conv_w, gate_p, seg)
    q, k, v, gate, cnew, z, u, ga, gb = outs
    return q, k, v, gate, cnew, z, u.reshape(r * nb, S5_WIDTH), ga, gb


def _delta_home(low, h, x, other=0.0):
    return jnp.where(low, x, other) if h % 2 == 0 else jnp.where(low, other, x)


def _delta_prepare(q_ref, k_ref, v_ref, gate_ref, tril_ref, bufs, *, nsub):
    sol_buf, wq_buf, qk_buf, kdec_buf, dl_buf = bufs
    c = DN_CHUNK
    dk = DN_HEAD_DIM

    def home(h, x, other=0.0):
        return _delta_home(low, h, x, other)

    rowi2 = lax.broadcasted_iota(I32, (c, 2 * c), 0)
    lane2 = lax.broadcasted_iota(I32, (c, 2 * c), 1)
    coli2 = lane2 & (c - 1)
    causal2 = rowi2 >= coli2
    strict2 = rowi2 > coli2
    low = lane2 < dk
    tril = tril_ref[...]
    pairs = [(j, h) for j in range(nsub) for h in range(DN_HEADS)]
    units = [(j, pr) for j in range(nsub) for pr in range(DN_HEADS // 2)]
    rows = [slice(j * c, (j + 1) * c) for j in range(nsub)]
    gate = [gate_ref[rows[j], :] for j in range(nsub)]
    gc_all = [_split3_dot_left(tril, gate[j]) for j in range(nsub)]
    gc_t = [gc_all[j].T for j in range(nsub)]

    def block(ref, j, pr):
        return ref[rows[j], pr * LANES:(pr + 1) * LANES]

    gfull = {(j, h): jnp.broadcast_to(gc_all[j][:, h:h + 1], (c, 2 * c)) for j, h in pairs}
    g2 = {(j, pr): jnp.where(low, gfull[j, 2 * pr], gfull[j, 2 * pr + 1]) for j, pr in units}
    b2 = {(j, pr): jnp.where(low, gate[j][:, DN_HEADS + 2 * pr:DN_HEADS + 2 * pr + 1],
                             gate[j][:, DN_HEADS + 2 * pr + 1:DN_HEADS + 2 * pr + 2]) for j, pr in units}
    kp = {u: block(k_ref, *u) for u in units}
    qp = {u: block(q_ref, *u) * (dk ** -0.5) for u in units}
    egc2 = {u: jnp.exp(g2[u]) for u in units}
    kb2 = {u: kp[u] * b2[u] for u in units}
    vb2 = {u: block(v_ref, *u) * b2[u] for u in units}
    kw2s = {u: pltpu.roll(kb2[u] * egc2[u], dk, axis=1) for u in units}
    qd2 = {u: qp[u] * egc2[u] for u in units}
    glast2 = {u: g2[u][c - 1:c, :] for u in units}
    kdec_t2 = {u: (kp[u] * jnp.exp(glast2[u] - g2[u])).T for u in units}
    dlast2 = {u: jnp.exp(glast2[u]) for u in units}
    kk = {u: jnp.concatenate([kp[u], kp[u]], axis=0) for u in units}
    yield

    grow2 = {(j, h): jnp.concatenate([gc_t[j][h:h + 1, :], gc_t[j][h:h + 1, :]], axis=1) for j, h in pairs}
    decay = {p: jnp.where(causal2, jnp.exp(jnp.where(causal2, gfull[p] - grow2[p], 0.0)), 0.0) for p in pairs}
    gram = {(j, h): _mm_nt(jnp.concatenate([home(h, kb2[j, h // 2]), home(h, qp[j, h // 2])], axis=0), kk[j, h // 2])
            for j, h in pairs}
    mat = {p: jnp.where(strict2, gram[p][:c] * decay[p], 0.0).astype(BF16) for p in pairs}
    qk = {p: jnp.where(causal2, gram[p][c:] * decay[p], 0.0) for p in pairs}
    sol = {(j, h): home(h, vb2[j, h // 2], kw2s[j, h // 2]) for j, h in pairs}
    yield
    levels = int(math.log2(c))
    zeros2 = jnp.zeros((c, 2 * c), BF16)
    for lvl in range(levels):
        hi = {p: sol[p].astype(BF16) for p in pairs}
        lo = {p: (sol[p] - hi[p].astype(F32)).astype(BF16) for p in pairs}
        if lvl < levels - 1:
            y = {p: jnp.dot(mat[p], jnp.concatenate([jnp.concatenate([hi[p], mat[p]], axis=1),
                                                     jnp.concatenate([lo[p], zeros2], axis=1)], axis=0),
                            preferred_element_type=F32) for p in pairs}
            mat = {p: y[p][:, 2 * dk:].astype(BF16) for p in pairs}
            upd = {p: y[p][:, :2 * dk] for p in pairs}
        else:
            upd = {p: jnp.dot(mat[p], jnp.concatenate([hi[p], lo[p]], axis=0), preferred_element_type=F32)
                   for p in pairs}
        sol = {p: (sol[p] - upd[p]) if lvl == 0 else (sol[p] + upd[p]) for p in pairs}
        yield
    for j, h in pairs:
        n = j * DN_HEADS + h
        sol_buf[n] = sol[j, h]
        wq_buf[n] = jnp.concatenate([home(h, 0.0, sol[j, h]), home(h, qd2[j, h // 2])], axis=0).astype(BF16)
        qk_buf[n] = qk[j, h].astype(BF16)
    for j, pr in units:
        n = j * (DN_HEADS // 2) + pr
        kdec_buf[n] = kdec_t2[j, pr].astype(BF16)
        dl_buf[n] = jnp.broadcast_to(dlast2[j, pr], (SUBLANES, LANES))


def _delta_apply(bufs, o_ref, s_ref, *, nsub):
    sol_buf, wq_buf, qk_buf, kdec_buf, dl_buf = bufs
    c = DN_CHUNK
    dk = DN_HEAD_DIM
    heads = range(DN_HEADS)
    low = lax.broadcasted_iota(I32, (c, 2 * c), 1) < dk
    s = [s_ref[h] for h in heads]
    for j in range(nsub):
        ws, v_new, o_h = [], [], []
        for h in heads:
            n = j * DN_HEADS + h
            ws.append(jnp.dot(wq_buf[n], jnp.concatenate([s[h], s[h]], axis=0).astype(BF16),
                              preferred_element_type=F32))
        yield
        for h in heads:
            v_new.append(sol_buf[j * DN_HEADS + h] - ws[h][:c])
        for h in heads:
            o_h.append(ws[h][c:] + jnp.dot(qk_buf[j * DN_HEADS + h][:, :c], v_new[h].astype(BF16),
                                           preferred_element_type=F32))
        for pr in range(DN_HEADS // 2):
            o_ref[j * c:(j + 1) * c, pr * LANES:(pr + 1) * LANES] = jnp.where(low, o_h[2 * pr], o_h[2 * pr + 1])
        nxt = []
        for h in heads:
            u = j * (DN_HEADS // 2) + h // 2
            kdt = kdec_buf[u][(h % 2) * dk:(h % 2 + 1) * dk, :]
            d = dl_buf[u][0:1, :]
            nxt.append(_delta_home(low, h, s[h] * d + jnp.dot(kdt, v_new[h].astype(BF16),
                                                               preferred_element_type=F32)))
        s = nxt
        yield
    for h in heads:
        s_ref[h] = s[h]


def _delta_chunk_kernel(q_ref, k_ref, v_ref, gate_ref, tril_ref, o_ref, sfin_ref, s_ref, *bufs, nsub, nc):
    i = pl.program_id(0)
    half = len(bufs) // 2
    sets = (bufs[:half], bufs[half:])
    local = lax.rem(jnp.maximum(i - 1, 0), nc)

    @pl.when(i == 0)
    def _():
        for b in sets[1]:
            b[...] = jnp.zeros_like(b)

    @pl.when(local == 0)
    def _():
        s_ref[...] = jnp.zeros_like(s_ref)

    for par in range(2):
        @pl.when(lax.rem(i, 2) == par)
        def _(par=par):
            parts = [_delta_prepare(q_ref, k_ref, v_ref, gate_ref, tril_ref, sets[par], nsub=nsub),
                     _delta_apply(sets[1 - par], o_ref, s_ref, nsub=nsub)]
            while parts:
                parts = [g for g in parts if next(g, StopIteration) is not StopIteration]

    @pl.when(jnp.logical_and(i >= 1, local == nc - 1))
    def _():
        dk = DN_HEAD_DIM
        for h in range(DN_HEADS):
            sfin_ref[0, h] = s_ref[h][:, (h % 2) * dk:(h % 2 + 1) * dk]


def _split3_dot_left(b01, a):
    a1 = a.astype(BF16)
    r1 = a - a1.astype(F32)
    a2 = r1.astype(BF16)
    a3 = (r1 - a2.astype(F32)).astype(BF16)
    out = jnp.dot(b01, a3, preferred_element_type=F32)
    out = out + jnp.dot(b01, a2, preferred_element_type=F32)
    return out + jnp.dot(b01, a1, preferred_element_type=F32)


def _delta_prompt(q, k, v, gate, nb):
    n = q.shape[0]
    t = n // nb
    c = DN_CHUNK
    nsub = DELTA_SUBCHUNKS
    rows = nsub * c
    nc = t // rows
    nblk = nb * nc
    row_in = lambda i: (jnp.minimum(i, nblk - 1), 0)
    row_out = lambda i: (jnp.maximum(i - 1, 0), 0)
    tril = jnp.tril(jnp.ones((c, c), F32)).astype(BF16)
    nh = nsub * DN_HEADS
    npair = nsub * DN_HEADS // 2
    buf_set = [pltpu.VMEM((nh, c, 2 * DN_HEAD_DIM), F32),
               pltpu.VMEM((nh, 2 * c, 2 * DN_HEAD_DIM), BF16),
               pltpu.VMEM((nh, c, 2 * c), BF16),
               pltpu.VMEM((npair, 2 * DN_HEAD_DIM, c), BF16),
               pltpu.VMEM((npair, SUBLANES, LANES), F32)]
    return pl.pallas_call(
        functools.partial(_delta_chunk_kernel, nsub=nsub, nc=nc),
        grid=(nblk + 1,),
        in_specs=[pl.BlockSpec((rows, DN_WIDTH), row_in),
                  pl.BlockSpec((rows, DN_WIDTH), row_in),
                  pl.BlockSpec((rows, DN_WIDTH), row_in),
                  pl.BlockSpec((rows, LANES), row_in),
                  pl.BlockSpec((c, c), lambda i: (0, 0))],
        out_specs=[pl.BlockSpec((rows, DN_WIDTH), row_out),
                   pl.BlockSpec((1, DN_HEADS, DN_HEAD_DIM, DN_HEAD_DIM),
                                lambda i: (jnp.maximum(i - 1, 0) // nc, 0, 0, 0))],
        out_shape=[jax.ShapeDtypeStruct((n, DN_WIDTH), F32),
                   jax.ShapeDtypeStruct((nb, DN_HEADS, DN_HEAD_DIM, DN_HEAD_DIM), F32)],
        scratch_shapes=[pltpu.VMEM((DN_HEADS, DN_HEAD_DIM, 2 * DN_HEAD_DIM), F32)] + buf_set + buf_set,
        compiler_params=_cparams(("arbitrary",)),
        name="delta_prompt",
    )(q, k, v, gate, tril)


def _delta_step_kernel(q_ref, k_ref, v_ref, gate_ref, s0_ref, o_ref, s_ref, kt_ref, qt_ref, gt_ref, *, nt, nb):
    dk = DN_HEAD_DIM
    p = pl.program_id(0)
    for t in range(nt):
        rs = slice(t * nb, (t + 1) * nb)
        gt_ref[...] = gate_ref[rs, :].T
        kt_ref[...] = k_ref[rs, :].T
        qt_ref[...] = (q_ref[rs, :] * (dk ** -0.5)).T
        vt = v_ref[rs, :].T
        src = s0_ref if t == 0 else s_ref
        o_heads = []
        for j in range(2):
            a = jnp.exp(gt_ref[pl.ds(2 * p + j, 1), :])
            beta = gt_ref[pl.ds(2 * p + j + DN_HEADS, 1), :]
            base = j * dk * dk

            def k_dot_s(d, acc, j=j, base=base, src=src):
                sd = src[pl.ds(pl.multiple_of(base + d * dk, dk), dk), :]
                return acc + kt_ref[pl.ds(j * dk + d, 1), :] * sd

            ks = lax.fori_loop(0, dk, k_dot_s, jnp.zeros((dk, nb), F32), unroll=4)
            delta = beta * (vt[j * dk:(j + 1) * dk, :] - a * ks)

            def update(d, acc, j=j, base=base, src=src, a=a, delta=delta):
                r0 = pl.multiple_of(base + d * dk, dk)
                sn = a * src[pl.ds(r0, dk), :] + kt_ref[pl.ds(j * dk + d, 1), :] * delta
                s_ref[pl.ds(r0, dk), :] = sn
                return acc + qt_ref[pl.ds(j * dk + d, 1), :] * sn

            o_heads.append(lax.fori_loop(0, dk, update, jnp.zeros((dk, nb), F32), unroll=4))
        o_ref[rs, :] = jnp.concatenate(o_heads, axis=0).T


def _delta_sample(q, k, v, gate, s0t, nb, nt):
    dk = DN_HEAD_DIM
    flat = dk * dk
    n = nt * nb
    kern = functools.partial(_delta_step_kernel, nt=nt, nb=nb)
    pair = lambda p: (0, p)
    return pl.pallas_call(
        kern,
        grid=(DN_HEADS // 2,),
        in_specs=[pl.BlockSpec((n, LANES), pair),
                  pl.BlockSpec((n, LANES), pair),
                  pl.BlockSpec((n, LANES), pair),
                  pl.BlockSpec((n, LANES), lambda p: (0, 0)),
                  pl.BlockSpec((2 * flat, nb), lambda p: (p, 0))],
        out_specs=[pl.BlockSpec((n, LANES), pair),
                   pl.BlockSpec((2 * flat, nb), lambda p: (p, 0))],
        out_shape=[jax.ShapeDtypeStruct((n, DN_WIDTH), F32),
                   jax.ShapeDtypeStruct((DN_HEADS * flat, nb), F32)],
        scratch_shapes=[pltpu.VMEM((LANES, nb), F32),
                        pltpu.VMEM((LANES, nb), F32),
                        pltpu.VMEM((LANES, nb), F32)],
        compiler_params=_cparams(("arbitrary",)),
        name="delta_sample",
    )(q, k, v, gate, s0t)


def _s5_kernel(u_ref, btre_ref, btim_ref, lam_ref, ctre_ref, ctim_ref, d_ref, h0_ref, y_ref, hfin_ref,
               bw_ref, c_ref, ab_ref, x_ref, h_ref, *, nb, tt):
    p2 = S5_FLAT

    @pl.when(pl.program_id(0) == 0)
    def _():
        lr = lam_ref[0:1, :]
        li = lam_ref[1:2, :]
        dt = jnp.exp(lam_ref[2:3, :])
        mag = jnp.exp(lr * dt)
        ab_re = mag * jnp.cos(li * dt)
        ab_im = mag * jnp.sin(li * dt)
        den = lr * lr + li * li
        nr = ab_re - 1.0
        ni = ab_im
        f_re = (nr * lr + ni * li) / den
        f_im = (ni * lr - nr * li) / den
        ab_ref[0:1, :] = ab_re
        ab_ref[1:2, :] = ab_im
        gpl = LANES // S5_STATE
        ch_g = lax.broadcasted_iota(I32, (S5_WIDTH, LANES), 0) // S5_GROUP_CH
        lane_g = lax.broadcasted_iota(I32, (S5_WIDTH, LANES), 1) // S5_STATE
        bre2 = jnp.concatenate([btre_ref[...]] * gpl, axis=1)
        bim2 = jnp.concatenate([btim_ref[...]] * gpl, axis=1)
        for j in range(p2 // LANES):
            cols = slice(j * LANES, (j + 1) * LANES)
            own = ch_g == gpl * j + lane_g
            bre = jnp.where(own, bre2, 0.0)
            bim = jnp.where(own, bim2, 0.0)
            bw_ref[:, cols] = (bre * f_re[:, cols] - bim * f_im[:, cols]).astype(BF16)
            bw_ref[:, p2 + j * LANES:p2 + (j + 1) * LANES] = (bim * f_re[:, cols] + bre * f_im[:, cols]).astype(BF16)
        cpl = LANES // S5_GROUP_CH
        st_g = lax.broadcasted_iota(I32, (p2, LANES), 0) // S5_STATE
        lane_cg = lax.broadcasted_iota(I32, (p2, LANES), 1) // S5_GROUP_CH
        for j in range(S5_WIDTH // LANES):
            cols = slice(j * LANES, (j + 1) * LANES)
            own = st_g == cpl * j + lane_cg
            c_ref[0:p2, cols] = jnp.where(own, ctre_ref[...], 0.0).astype(BF16)
            c_ref[p2:2 * p2, cols] = jnp.where(own, -ctim_ref[...], 0.0).astype(BF16)
        h_ref[...] = h0_ref[...]

    u = u_ref[...]
    cw = S5_WIDTH // S5_SUPER
    sw = S5_FLAT // S5_SUPER
    for part in (0, p2):
        for b in range(S5_SUPER):
            x_ref[:, part + b * sw:part + (b + 1) * sw] = jnp.dot(
                u[:, b * cw:(b + 1) * cw], bw_ref[b * cw:(b + 1) * cw, part + b * sw:part + (b + 1) * sw],
                preferred_element_type=F32)
    a_re = ab_ref[0:1, :]
    a_im = ab_ref[1:2, :]

    if nb == SUBLANES:
        wsl = p2 // S5_SCAN_SPLIT
        for sp in range(S5_SCAN_SPLIT):
            c0 = sp * wsl
            are = jnp.broadcast_to(a_re[:, c0:c0 + wsl], (nb, wsl))
            aim = jnp.broadcast_to(a_im[:, c0:c0 + wsl], (nb, wsl))

            def step(t, carry, c0=c0, are=are, aim=aim):
                hr, hi = carry
                r0 = pl.multiple_of(t * nb, nb)
                nr = are * hr - aim * hi + x_ref[pl.ds(r0, nb), c0:c0 + wsl]
                ni = are * hi + aim * hr + x_ref[pl.ds(r0, nb), p2 + c0:p2 + c0 + wsl]
                x_ref[pl.ds(r0, nb), c0:c0 + wsl] = nr
                x_ref[pl.ds(r0, nb), p2 + c0:p2 + c0 + wsl] = ni
                return nr, ni

            hr, hi = lax.fori_loop(0, tt, step, (h_ref[:, c0:c0 + wsl], h_ref[:, p2 + c0:p2 + c0 + wsl]),
                                   unroll=2)
            h_ref[:, c0:c0 + wsl] = hr
            h_ref[:, p2 + c0:p2 + c0 + wsl] = hi
    else:
        for t in range(tt):
            rs = slice(t * nb, (t + 1) * nb)
            hr = h_ref[:, 0:p2]
            hi = h_ref[:, p2:2 * p2]
            nr = a_re * hr - a_im * hi + x_ref[rs, 0:p2]
            ni = a_re * hi + a_im * hr + x_ref[rs, p2:2 * p2]
            h_ref[:, 0:p2] = nr
            h_ref[:, p2:2 * p2] = ni
            x_ref[rs, 0:p2] = nr
            x_ref[rs, p2:2 * p2] = ni

    for b in range(S5_SUPER):
        cols = slice(b * cw, (b + 1) * cw)
        y = None
        for part in (0, p2):
            rws = slice(part + b * sw, part + (b + 1) * sw)
            term = jnp.dot(x_ref[:, rws].astype(BF16), c_ref[rws, cols], preferred_element_type=F32)
            y = term if y is None else y + term
        y_ref[:, cols] = y + d_ref[:, cols] * u[:, cols].astype(F32)
    hfin_ref[...] = h_ref[...]


def _s5(u, params, h0, nb):
    btre, btim, lam, ctre, ctim, dvec = params
    n = u.shape[0]
    t = n // nb
    tt = min(ROW_TILE // nb, t)
    rows = tt * nb
    const = lambda i: (0, 0)
    kern = functools.partial(_s5_kernel, nb=nb, tt=tt)
    return pl.pallas_call(
        kern,
        grid=(t // tt,),
        in_specs=[pl.BlockSpec((rows, S5_WIDTH), lambda i: (i, 0)),
                  pl.BlockSpec((S5_WIDTH, S5_STATE), const),
                  pl.BlockSpec((S5_WIDTH, S5_STATE), const),
                  pl.BlockSpec((SUBLANES, S5_FLAT), const),
                  pl.BlockSpec((S5_FLAT, LANES), const),
                  pl.BlockSpec((S5_FLAT, LANES), const),
                  pl.BlockSpec((1, S5_WIDTH), const),
                  pl.BlockSpec((nb, 2 * S5_FLAT), const)],
        out_specs=[pl.BlockSpec((rows, S5_WIDTH), lambda i: (i, 0)),
                   pl.BlockSpec((nb, 2 * S5_FLAT), const)],
        out_shape=[jax.ShapeDtypeStruct((n, S5_WIDTH), F32),
                   jax.ShapeDtypeStruct((nb, 2 * S5_FLAT), F32)],
        scratch_shapes=[pltpu.VMEM((S5_WIDTH, 2 * S5_FLAT), BF16),
                        pltpu.VMEM((2 * S5_FLAT, S5_WIDTH), BF16),
                        pltpu.VMEM((SUBLANES, S5_FLAT), F32),
                        pltpu.VMEM((rows, 2 * S5_FLAT), F32),
                        pltpu.VMEM((nb, 2 * S5_FLAT), F32)],
        compiler_params=_cparams(("arbitrary",)),
        name="s5",
    )(u, btre, btim, lam, ctre, ctim, dvec, h0)


def _postmix_kernel(xp_ref, op_ref, zp_ref, ysp_ref, gap_ref, gbp_ref,
                    xs_ref, os_ref, zs_ref, yss_ref, gas_ref, gbs_ref, *rest, nblk_p, range_tok):
    carry_ref = rest[-1]

    @pl.when(pl.program_id(0) == 0)
    def _():
        carry_ref[...] = jnp.zeros_like(carry_ref)

    @pl.when(pl.program_id(0) < nblk_p)
    def _():
        _postmix_body(xp_ref, op_ref, zp_ref, ysp_ref, gap_ref, gbp_ref, *rest, range_tok=range_tok)

    @pl.when(pl.program_id(0) >= nblk_p)
    def _():
        _postmix_body(xs_ref, os_ref, zs_ref, yss_ref, gas_ref, gbs_ref, *rest, range_tok=range_tok)


def _postmix_body(x_ref, o_ref, z_ref, ys_ref, ga_ref, gb_ref, hw_ref, seg_ref, wa_ref, wglu_ref, wb_ref,
                  wo_ref, nf_ref, wr_ref, su_ref, x1_ref, hn_ref, bkt_ref, rank_ref, rw_ref, cnt_ref, carry_ref,
                  *, range_tok):
    o = o_ref[...]
    ms = jnp.dot((o * o).astype(BF16), seg_ref[...], preferred_element_type=F32) * (1.0 / DN_HEAD_DIM)
    on = o * lax.rsqrt(ms + RMS_EPS) * hw_ref[...]
    z = z_ref[...]
    oa = on * (z * jax.nn.sigmoid(z)).astype(F32)
    y_a = _mm(oa, wa_ref[...])
    ys = jax.nn.gelu(ys_ref[...])
    ys = ys * jax.nn.sigmoid(_mm(ys, wglu_ref[...]))
    y_b = _mm(ys, wb_ref[...])
    mixed = jax.nn.sigmoid(ga_ref[...]).astype(F32) * y_a + jax.nn.sigmoid(gb_ref[...]).astype(F32) * y_b
    x1 = x_ref[...] + _mm(mixed, wo_ref[...])
    x1_ref[...] = x1
    hn = x1 * lax.rsqrt(jnp.mean(x1 * x1, axis=-1, keepdims=True) + RMS_EPS) * nf_ref[...]
    _slab_store(hn_ref, hn)

    wr = wr_ref[...]
    w_hi = wr.astype(BF16)
    w_lo = (wr - w_hi.astype(F32)).astype(BF16)
    hn_hi = hn.astype(BF16)
    hn_lo = (hn - hn_hi.astype(F32)).astype(BF16)
    both = _mm_nt(jnp.concatenate([w_hi, w_lo], axis=0), hn_hi)
    logits = both[:ROUTER_ROWS] + both[ROUTER_ROWS:] + _mm_nt(w_hi, hn_lo)
    coarse = logits[N_EXPERTS:N_EXPERTS + MOE_GROUPS, :]
    cm = jnp.max(coarse, axis=0, keepdims=True)
    ce = jnp.exp(coarse - cm)
    pc = ce / jnp.sum(ce, axis=0, keepdims=True)
    p_sel = jnp.max(pc, axis=0, keepdims=True)
    gi = lax.broadcasted_iota(I32, pc.shape, 0)
    g_sel = jnp.min(jnp.where(pc == p_sel, gi, MOE_GROUPS), axis=0, keepdims=True)
    fine = jnp.zeros((EXPERTS_PER_GROUP, logits.shape[1]), F32)
    for g in range(MOE_GROUPS):
        fine = fine + jnp.where(g_sel == g, logits[g * EXPERTS_PER_GROUP:(g + 1) * EXPERTS_PER_GROUP, :], 0.0)
    fm = jnp.max(fine, axis=0, keepdims=True)
    fe = jnp.exp(fine - fm)
    pf = fe / jnp.sum(fe, axis=0, keepdims=True)
    ei = lax.broadcasted_iota(I32, pf.shape, 0)
    v1 = jnp.max(pf, axis=0, keepdims=True)
    i1 = jnp.min(jnp.where(pf == v1, ei, EXPERTS_PER_GROUP), axis=0, keepdims=True)
    rest = jnp.where(ei == i1, -1.0, pf)
    v2 = jnp.max(rest, axis=0, keepdims=True)
    i2 = jnp.min(jnp.where(rest == v2, ei, EXPERTS_PER_GROUP), axis=0, keepdims=True)
    tot = v1 + v2
    rw_ref[0:1, :] = v1 / tot * p_sel
    rw_ref[1:2, :] = v2 / tot * p_sel

    tt = logits.shape[1]
    tok = pl.program_id(0) * tt + lax.broadcasted_iota(I32, (1, tt), 1)
    ph = jnp.zeros((1, tt), I32)
    for r in range(1, MOE_PHASES):
        ph = ph + (tok >= r * range_tok).astype(I32)
    bsel = [ph * N_EXPERTS + g_sel * EXPERTS_PER_GROUP + ix for ix in (i1, i2)]
    bi = lax.broadcasted_iota(I32, (MOE_PHASES * N_EXPERTS, tt), 0)
    onehot = [(bi == b).astype(F32) for b in bsel]
    cnt = onehot[0] + onehot[1]
    before = carry_ref[:, 0:1] + jnp.dot(cnt.astype(BF16), su_ref[...], preferred_element_type=F32)
    for s in range(TOP_K):
        bkt_ref[s:s + 1, :] = bsel[s]
        rank_ref[s:s + 1, :] = jnp.sum(onehot[s] * before, axis=0, keepdims=True).astype(I32)
    carry_ref[...] = carry_ref[...] + jnp.sum(cnt, axis=1, keepdims=True)
    cnt_ref[...] = carry_ref[...]


def _postmix(prompt, sample, weights, nb):
    n_p = prompt[0].shape[0]
    n_s = sample[0].shape[0]
    t = n_p // nb
    tt = min(ROW_TILE, t, n_s)
    nt = t // tt
    nblk_p = n_p // tt
    nblk = nblk_p + n_s // tt
    n_total = n_p + n_s
    prow = lambda i: (jnp.minimum(i, nblk_p - 1), 0)
    pys = lambda i: (jnp.minimum(i, nblk_p - 1) % nt, jnp.minimum(i, nblk_p - 1) // nt)
    srow = lambda i: (jnp.maximum(i - nblk_p, 0), 0)
    const = lambda i: (0, 0)

    def stream_specs(row, ysmap):
        return [pl.BlockSpec((tt, D_MODEL), row),
                pl.BlockSpec((tt, DN_WIDTH), row),
                pl.BlockSpec((tt, DN_WIDTH), row),
                pl.BlockSpec((tt, S5_WIDTH), ysmap),
                pl.BlockSpec((tt, D_MODEL), row),
                pl.BlockSpec((tt, D_MODEL), row)]

    weight_specs = [pl.BlockSpec((1, DN_WIDTH), const),
                    pl.BlockSpec((DN_WIDTH, DN_WIDTH), const),
                    pl.BlockSpec((DN_WIDTH, D_MODEL), const),
                    pl.BlockSpec((S5_WIDTH, S5_WIDTH), const),
                    pl.BlockSpec((S5_WIDTH, D_MODEL), const),
                    pl.BlockSpec((D_MODEL, D_MODEL), const),
                    pl.BlockSpec((1, D_MODEL), const),
                    pl.BlockSpec((ROUTER_ROWS, D_MODEL), const),
                    pl.BlockSpec((tt, tt), const)]
    xp, op, zp, ysp, gap, gbp = prompt
    nbk = MOE_PHASES * N_EXPERTS
    earlier = jnp.triu(jnp.ones((tt, tt), F32), k=1).astype(BF16)
    return pl.pallas_call(
        functools.partial(_postmix_kernel, nblk_p=nblk_p, range_tok=n_total // MOE_PHASES),
        grid=(nblk,),
        in_specs=stream_specs(prow, pys) + stream_specs(srow, srow) + weight_specs,
        out_specs=[pl.BlockSpec((tt, D_MODEL), lambda i: (i, 0)),
                   pl.BlockSpec((tt * ROW_SLAB, LANES), lambda i: (i, 0)),
                   pl.BlockSpec((TOP_K, tt), lambda i: (0, i)),
                   pl.BlockSpec((TOP_K, tt), lambda i: (0, i)),
                   pl.BlockSpec((TOP_K, tt), lambda i: (0, i)),
                   pl.BlockSpec((nbk, LANES), const)],
        out_shape=[jax.ShapeDtypeStruct((n_total, D_MODEL), F32),
                   jax.ShapeDtypeStruct((n_total * ROW_SLAB, LANES), F32),
                   jax.ShapeDtypeStruct((TOP_K, n_total), I32),
                   jax.ShapeDtypeStruct((TOP_K, n_total), I32),
                   jax.ShapeDtypeStruct((TOP_K, n_total), F32),
                   jax.ShapeDtypeStruct((nbk, LANES), F32)],
        scratch_shapes=[pltpu.VMEM((nbk, LANES), F32)],
        compiler_params=_cparams(("arbitrary",)),
        name="postmix",
    )(xp, op, zp, ysp.reshape(t, nb * S5_WIDTH), gap, gbp, *sample, *weights, earlier)


def _wait_slabs(buf, sem):
    pltpu.make_async_copy(buf, buf, sem).wait()


def _moe_kernel(texp_ref, tph_ref, tsrc_ref, tnv_ref, tfirst_ref, tslot_ref, tnext_ref, otok_ref,
                hn_hbm, wu_hbm, wd_hbm, y_ref, hnv, xbuf, wu_buf, wd_buf, wub, wdb, sem, wsem):
    i = pl.program_id(0)
    tm = MOE_TILE
    rs = ROW_SLAB
    nv = tnv_ref[i]
    ph = tph_ref[i]
    range_rows = hnv.shape[0]

    def weight_copies(e, sl):
        return (pltpu.make_async_copy(wu_hbm.at[e], wu_buf.at[sl], wsem.at[sl]),
                pltpu.make_async_copy(wd_hbm.at[e], wd_buf.at[sl], wsem.at[sl]))

    @pl.when(i == 0)
    def _():
        for c in weight_copies(texp_ref[0], 0):
            c.start()

    @pl.when(jnp.logical_and(nv > 0, jnp.logical_or(i == 0, ph != tph_ref[jnp.maximum(i - 1, 0)])))
    def _():
        start = pl.multiple_of(ph * range_rows, rs)
        whole = pltpu.make_async_copy(hn_hbm.at[pl.ds(start, range_rows), :], hnv, sem)
        whole.start()
        whole.wait()

    for sl in range(2):
        @pl.when(jnp.logical_and(jnp.logical_and(nv > 0, tfirst_ref[i] == 1), tslot_ref[i] == sl))
        def _():
            for c in weight_copies(texp_ref[i], sl):
                c.wait()

            @pl.when(tnext_ref[i] >= 0)
            def _():
                for c in weight_copies(tnext_ref[i], 1 - sl):
                    c.start()

            wub[...] = wu_buf[sl].astype(BF16)
            wdb[...] = wd_buf[sl].astype(BF16)

    @pl.when(nv == 0)
    def _():
        y_ref[...] = jnp.zeros_like(y_ref)

    @pl.when(nv > 0)
    def _():
        src0 = tsrc_ref[i]
        for r in range(tm):
            tok8 = pl.multiple_of(otok_ref[src0 + r], rs)
            xbuf[pl.ds(r * rs, rs), :] = hnv[pl.ds(tok8, rs), :]
        x = _slab_load(xbuf, tm).astype(BF16)
        hu = jnp.dot(x, wub[...], preferred_element_type=F32)
        gate = hu[:, :EXPERT_FF]
        up = hu[:, EXPERT_FF:]
        act = gate * jax.nn.sigmoid(gate) * up
        _slab_store(y_ref, jnp.dot(act.astype(BF16), wdb[...], preferred_element_type=F32))


def _moe(hn, w_up, w_down, plan):
    ntiles = plan[0].shape[0]
    grid_spec = pltpu.PrefetchScalarGridSpec(
        num_scalar_prefetch=len(plan),
        grid=(ntiles,),
        in_specs=[pl.BlockSpec(memory_space=pl.ANY),
                  pl.BlockSpec(memory_space=pl.ANY),
                  pl.BlockSpec(memory_space=pl.ANY)],
        out_specs=pl.BlockSpec((MOE_TILE * ROW_SLAB, LANES), lambda i, *_: (i, 0)),
        scratch_shapes=[pltpu.VMEM((hn.shape[0] // MOE_PHASES, LANES), F32),
                        pltpu.VMEM((MOE_TILE * ROW_SLAB, LANES), F32),
                        pltpu.VMEM((2, D_MODEL, 2 * EXPERT_FF), F32),
                        pltpu.VMEM((2, EXPERT_FF, D_MODEL), F32),
                        pltpu.VMEM((D_MODEL, 2 * EXPERT_FF), BF16),
                        pltpu.VMEM((EXPERT_FF, D_MODEL), BF16),
                        pltpu.SemaphoreType.DMA,
                        pltpu.SemaphoreType.DMA((2,))])
    return pl.pallas_call(
        _moe_kernel,
        grid_spec=grid_spec,
        out_shape=jax.ShapeDtypeStruct((ntiles * MOE_TILE * ROW_SLAB, LANES), F32),
        compiler_params=_cparams(("arbitrary",)),
        name="moe",
    )(*plan, hn, w_up, w_down)


def _combine_kernel(pos_ref, x1_ref, ys_hbm, w_ref, nw_ref, outp_ref, outs_ref,
                    ybuf0, ybuf1, sem, *, nblk_p, n_tok):
    i = pl.program_id(0)
    nsteps = pl.num_programs(0)
    tt = x1_ref.shape[0]
    rs = ROW_SLAB
    slot = lax.rem(i, 2)
    ybuf = (ybuf0, ybuf1)

    def start_gather(step, sl):
        base = step * tt
        for r in range(tt * TOP_K):
            j, s = divmod(r, TOP_K)
            p8 = pl.multiple_of(pos_ref[s * n_tok + base + j], rs)
            pltpu.make_async_copy(ys_hbm.at[pl.ds(p8, rs), :], ybuf[sl].at[pl.ds((s * tt + j) * rs, rs), :],
                                  sem.at[sl]).start(priority=r % DMA_QUEUES)

    @pl.when(i == 0)
    def _():
        start_gather(0, 0)

    for sl in range(2):
        @pl.when(slot == sl)
        def _():
            _wait_slabs(ybuf[sl], sem.at[sl])
            start_gather(jnp.minimum(i + 1, nsteps - 1), 1 - sl)
            w = w_ref[...]
            y0 = _slab_load(ybuf[sl], tt, 0)
            y1 = _slab_load(ybuf[sl], tt, tt * rs)
            x = x1_ref[...] + w[:, 0:1] * y0 + w[:, 1:2] * y1
            res = x * lax.rsqrt(jnp.mean(x * x, axis=-1, keepdims=True) + RMS_EPS) * nw_ref[...]

            @pl.when(i < nblk_p)
            def _():
                outp_ref[...] = res

            @pl.when(i >= nblk_p)
            def _():
                outs_ref[...] = res

        @pl.when(jnp.logical_and(slot == sl, i == nsteps - 1))
        def _():
            _wait_slabs(ybuf[1 - sl], sem.at[1 - sl])


def _combine(x1, ysorted, pos8, wtok, nw, n_p):
    n = x1.shape[0]
    tt = math.gcd(math.gcd(n_p, n - n_p), COMBINE_TILE)
    nblk_p = n_p // tt
    grid_spec = pltpu.PrefetchScalarGridSpec(
        num_scalar_prefetch=1,
        grid=(n // tt,),
        in_specs=[pl.BlockSpec((tt, D_MODEL), lambda i, *_: (i, 0)),
                  pl.BlockSpec(memory_space=pl.ANY),
                  pl.BlockSpec((tt, TOP_K), lambda i, *_: (i, 0)),
                  pl.BlockSpec((1, D_MODEL), lambda i, *_: (0, 0))],
        out_specs=[pl.BlockSpec((tt, D_MODEL), lambda i, *_: (jnp.minimum(i, nblk_p - 1), 0)),
                   pl.BlockSpec((tt, D_MODEL), lambda i, *_: (jnp.maximum(i - nblk_p, 0), 0))],
        scratch_shapes=[pltpu.VMEM((tt * TOP_K * ROW_SLAB, LANES), F32),
                        pltpu.VMEM((tt * TOP_K * ROW_SLAB, LANES), F32),
                        pltpu.SemaphoreType.DMA((2,))])
    return pl.pallas_call(
        functools.partial(_combine_kernel, nblk_p=nblk_p, n_tok=n),
        grid_spec=grid_spec,
        out_shape=[jax.ShapeDtypeStruct((n_p, D_MODEL), F32),
                   jax.ShapeDtypeStruct((n - n_p, D_MODEL), F32)],
        compiler_params=_cparams(("arbitrary",)),
        name="combine",
    )(pos8, x1, ysorted, wtok, nw)


def _route_plan(bkt, rank, cnt, n_tok):
    tm = MOE_TILE
    n_assign = n_tok * TOP_K
    nbk = MOE_PHASES * N_EXPERTS
    ntiles = n_assign // tm + nbk
    range_tok = n_tok // MOE_PHASES
    b_flat = bkt.T.reshape(n_assign)
    order = jnp.argsort(b_flat, stable=True).astype(I32)
    counts = cnt[:, 0].astype(I32)
    cstart = jnp.cumsum(counts) - counts
    tiles_b = (counts + tm - 1) // tm
    tend = jnp.cumsum(tiles_b)
    tstart = tend - tiles_b
    tile_id = jnp.arange(ntiles, dtype=I32)
    tbk = jnp.minimum(jnp.sum((tile_id[:, None] >= tend[None, :]).astype(I32), axis=1), nbk - 1)
    onehot = (tbk[:, None] == jnp.arange(nbk, dtype=I32)[None, :]).astype(I32)
    pick = lambda v: jnp.sum(onehot * v[None, :], axis=1)
    done = (tile_id - pick(tstart)) * tm
    tnv = jnp.where(tile_id < tend[-1], jnp.clip(pick(counts) - done, 0, tm), 0)
    tsrc = jnp.where(tnv > 0, pick(cstart) + done, 0)
    texp = tbk % N_EXPERTS
    tph = tbk // N_EXPERTS
    nonempty = counts > 0
    bslot = (jnp.cumsum(nonempty.astype(I32)) - 1) % 2
    bidx = jnp.where(nonempty, jnp.arange(nbk, dtype=I32), nbk)
    nxt = jnp.concatenate([lax.cummin(bidx[::-1])[::-1][1:], jnp.full((1,), nbk, I32)])
    bnext = jnp.where(nxt < nbk, nxt % N_EXPERTS, -1)
    tfirst = jnp.logical_and(tnv > 0, done == 0).astype(I32)
    tslot = pick(bslot)
    tnext = pick(bnext)
    otok8 = jnp.concatenate([((order // TOP_K) % range_tok) * ROW_SLAB, jnp.zeros((tm,), I32)])
    plan = tuple(a.astype(I32) for a in (texp, tph, tsrc, tnv, tfirst, tslot, tnext, otok8))
    first = jnp.sum((bkt[:, :, None] == jnp.arange(nbk, dtype=I32)[None, None, :]).astype(I32)
                    * (tstart * tm)[None, None, :], axis=2)
    pos8 = ((first + rank) * ROW_SLAB).reshape(n_assign)
    return plan, pos8.astype(I32)


def _block_diag(m):
    g, a, b = m.shape
    eye = jnp.eye(g, dtype=m.dtype)
    return (eye[:, None, :, None] * m[:, :, None, :]).reshape(g * a, g * b)


def kernel(x_prompt, x_sample, state_conv, state_delta, state_ssm_re, state_ssm_im, norm_mix_w, w_in, conv_w, a_log, dt_bias, head_norm_w, w_a_up, s5_lambda_re, s5_lambda_im, s5_log_step, s5_b_re, s5_b_im, s5_c_re, s5_c_im, s5_d, w_glu, w_b_up, w_o, norm_ffn_w, w_router_coarse, w_router_fine, w_expert_up, w_expert_down, norm_final_w):
    bp, tp, _ = x_prompt.shape
    bs, ts, _ = x_sample.shape
    n_p = bp * tp
    n_s = bs * ts
    n_tok = n_p + n_s
    l = 0

    w = w_in[l].astype(BF16)
    c_ab = W1_COLS + 2 * DN_HEADS
    w_parts = (w[:, :W1_COLS], w[:, c_ab:],
               jnp.concatenate([w[:, W1_COLS:c_ab], jnp.zeros((D_MODEL, LANES - 2 * DN_HEADS), BF16)], axis=1))
    nw_mix = norm_mix_w[l].reshape(1, D_MODEL)
    pad8 = lambda v: jnp.concatenate([v, jnp.zeros((LANES - DN_HEADS,), F32)]).reshape(1, LANES)
    gate_p = jnp.concatenate([pad8(a_log[l]), pad8(dt_bias[l])], axis=0)
    seg = _block_diag(jnp.ones((DN_HEADS, DN_HEAD_DIM, DN_HEAD_DIM), BF16))
    chan_rows = lambda b: jnp.swapaxes(b, 1, 2).reshape(S5_WIDTH, S5_STATE)
    state_rows = lambda c: jnp.tile(jnp.swapaxes(c, 1, 2).reshape(S5_FLAT, S5_GROUP_CH),
                                    (1, LANES // S5_GROUP_CH))
    lam = jnp.concatenate([s5_lambda_re[l].reshape(1, S5_FLAT), s5_lambda_im[l].reshape(1, S5_FLAT),
                           jnp.repeat(s5_log_step[l], S5_STATE).reshape(1, S5_FLAT),
                           jnp.zeros((SUBLANES - 3, S5_FLAT), F32)], axis=0)
    s5_params = (chan_rows(s5_b_re[l]), chan_rows(s5_b_im[l]), lam,
                 state_rows(s5_c_re[l]), state_rows(s5_c_im[l]), s5_d[l].reshape(1, S5_WIDTH))
    hw = jnp.tile(head_norm_w[l], DN_HEADS).reshape(1, DN_WIDTH)
    wr = jnp.concatenate([w_router_fine[l].T, w_router_coarse[l].T,
                          jnp.zeros((ROUTER_ROWS - N_EXPERTS - MOE_GROUPS, D_MODEL), F32)], axis=0)
    pm_weights = (hw, seg, w_a_up[l].astype(BF16), w_glu[l].astype(BF16), w_b_up[l].astype(BF16),
                  w_o[l].astype(BF16), norm_ffn_w[l].reshape(1, D_MODEL), wr)

    xp2 = x_prompt.reshape(n_p, D_MODEL)
    q_p, k_p, v_p, gates_p, conv_p, z_p, u_p, ga_p, gb_p = _inprep(
        xp2, nw_mix, w_parts, jnp.zeros((bp, SUBLANES, QKV_DIM), F32), conv_w[l], gate_p, seg, bp, 1)
    o_p, delta_p = _delta_prompt(q_p, k_p, v_p, gates_p, bp)
    ys_p, h_p = _s5(u_p, s5_params, jnp.zeros((bp, 2 * S5_FLAT), F32), bp)

    xs2 = jnp.swapaxes(x_sample, 0, 1).reshape(n_s, D_MODEL)
    cinit_s = jnp.swapaxes(state_conv[l], 0, 1).reshape(1, (CONV_W - 1) * bs, QKV_DIM)
    q_s, k_s, v_s, gate_s, conv_s, z_s, u_s, ga_s, gb_s = _inprep(
        xs2, nw_mix, w_parts, cinit_s, conv_w[l], gate_p, seg, 1, bs)
    s0t = jnp.transpose(state_delta[l], (1, 2, 3, 0)).reshape(DN_HEADS * DN_HEAD_DIM * DN_HEAD_DIM, bs)
    o_s, delta_st = _delta_sample(q_s, k_s, v_s, gate_s, s0t, bs, ts)
    delta_s = jnp.transpose(delta_st.reshape(DN_HEADS, DN_HEAD_DIM, DN_HEAD_DIM, bs), (3, 0, 1, 2))
    h0_s = jnp.concatenate([state_ssm_re[l].reshape(bs, S5_FLAT), state_ssm_im[l].reshape(bs, S5_FLAT)], axis=1)
    ys_s, h_s = _s5(u_s, s5_params, h0_s, bs)
    x1, hn, bkt, rank, rw, cnt = _postmix((xp2, o_p, z_p, ys_p, ga_p, gb_p), (xs2, o_s, z_s, ys_s, ga_s, gb_s),
                                          pm_weights, bp)

    plan, pos8 = _route_plan(bkt, rank, cnt, n_tok)
    ysorted = _moe(hn, w_expert_up[l], w_expert_down[l], plan)
    y_p, y_s = _combine(x1, ysorted, pos8, rw.T, norm_final_w.reshape(1, D_MODEL), n_p)

    y_prompt = y_p.reshape(bp, tp, D_MODEL)
    y_sample = jnp.swapaxes(y_s.reshape(ts, bs, D_MODEL), 0, 1)
    conv_sample = jnp.swapaxes(conv_s.reshape(CONV_W - 1, bs, QKV_DIM), 0, 1)
    return (y_prompt, y_sample,
            conv_p[None], delta_p[None],
            h_p[:, :S5_FLAT].reshape(1, bp, S5_GROUPS, S5_STATE), h_p[:, S5_FLAT:].reshape(1, bp, S5_GROUPS, S5_STATE),
            conv_sample[None], delta_s[None],
            h_s[:, :S5_FLAT].reshape(1, bs, S5_GROUPS, S5_STATE), h_s[:, S5_FLAT:].reshape(1, bs, S5_GROUPS, S5_STATE))
```

```python
import functools
import math

import jax
import jax.numpy as jnp
import numpy as np
from jax import lax
from jax.experimental import pallas as pl
from jax.experimental.pallas import tpu as pltpu

F32 = jnp.float32
BF16 = jnp.bfloat16
I32 = jnp.int32

D_MODEL = 1024
DN_HEADS = 8
DN_HEAD_DIM = 64
DN_WIDTH = DN_HEADS * DN_HEAD_DIM
QKV_DIM = 3 * DN_WIDTH
CONV_W = 4
DN_CHUNK = 64
S5_GROUP_CH = 16
S5_WIDTH = D_MODEL // 2
S5_GROUPS = S5_WIDTH // S5_GROUP_CH
S5_STATE = 64
S5_FLAT = S5_GROUPS * S5_STATE
MOE_GROUPS = 4
EXPERTS_PER_GROUP = 8
N_EXPERTS = MOE_GROUPS * EXPERTS_PER_GROUP
TOP_K = 2
EXPERT_FF = 256
RMS_EPS = 1e-6
L2_EPS = 1e-6

LANES = 128
SUBLANES = 8
VMEM_LIMIT = 56 * 1024 * 1024

W1_COLS = QKV_DIM + DN_WIDTH
W2_COLS = S5_WIDTH + 2 * D_MODEL

ROW_TILE = 512
INPREP_PARTS = 2
MOE_TILE = 256
MOE_PHASES = 2
COMBINE_TILE = 256
DMA_QUEUES = 2
DELTA_SUBCHUNKS = 4
S5_SUPER = 2
S5_SCAN_SPLIT = 2
ROUTER_ROWS = 40


def _mm(a, b):
    return jnp.dot(a.astype(BF16), b.astype(BF16), preferred_element_type=F32)


def _mm_nt(a, b):
    return lax.dot_general(a.astype(BF16), b.astype(BF16), (((1,), (1,)), ((), ())),
                           preferred_element_type=F32)


def _split3_dot(a, b01):
    a1 = a.astype(BF16)
    r1 = a - a1.astype(F32)
    a2 = r1.astype(BF16)
    a3 = (r1 - a2.astype(F32)).astype(BF16)
    out = jnp.dot(a3, b01, preferred_element_type=F32)
    out = out + jnp.dot(a2, b01, preferred_element_type=F32)
    return out + jnp.dot(a1, b01, preferred_element_type=F32)


def _cparams(sem):
    return pltpu.CompilerParams(dimension_semantics=sem, vmem_limit_bytes=VMEM_LIMIT)


ROW_SLAB = D_MODEL // LANES


def _slab_load(ref, rows, first=0, pitch=ROW_SLAB):
    return jnp.concatenate([ref[pl.ds(first + j, rows, stride=pitch), :] for j in range(ROW_SLAB)], axis=1)


def _slab_store(ref, x):
    for j in range(ROW_SLAB):
        ref[pl.ds(j, x.shape[0], stride=ROW_SLAB), :] = x[:, j * LANES:(j + 1) * LANES]


def _softplus(x):
    return jnp.maximum(x, 0.0) + jnp.log1p(jnp.exp(-jnp.abs(x)))


def _inprep_kernel(x_ref, nw_ref, w1_ref, w2_ref, wab_ref, cinit_ref, cw_ref, gp_ref, seg_ref,
                   q_ref, k_ref, v_ref, gate_ref, cnew_ref, z_ref, u_ref, ga_ref, gb_ref, xp_ref,
                   *, shift, rc, rows):
    @pl.when(pl.program_id(1) == 0)
    def _():
        xp_ref[0:rc, :] = cinit_ref[0]

    seg = seg_ref[...]
    pr = rows // INPREP_PARTS
    for part in range(INPREP_PARTS):
        rs = slice(part * pr, (part + 1) * pr)
        x = x_ref[rs, :]
        h = x * lax.rsqrt(jnp.mean(x * x, axis=-1, keepdims=True) + RMS_EPS) * nw_ref[...]
        hb = h.astype(BF16)

        def proj(w_ref, lo, hi, hb=hb):
            return jnp.dot(hb, w_ref[:, lo:hi], preferred_element_type=F32)

        xp_ref[rc + part * pr:rc + (part + 1) * pr, :] = proj(w1_ref, 0, QKV_DIM)
        ab = proj(wab_ref, 0, LANES)
        z_ref[rs, :] = proj(w1_ref, QKV_DIM, W1_COLS).astype(z_ref.dtype)
        u_ref[rs, :] = proj(w2_ref, 0, S5_WIDTH).astype(u_ref.dtype)
        ga_ref[rs, :] = proj(w2_ref, S5_WIDTH, S5_WIDTH + D_MODEL).astype(ga_ref.dtype)
        gb_ref[rs, :] = proj(w2_ref, S5_WIDTH + D_MODEL, W2_COLS).astype(gb_ref.dtype)
        acc = None
        for i in range(CONV_W):
            lo = rc + part * pr + (i - (CONV_W - 1)) * shift
            term = xp_ref[lo:lo + pr, :] * cw_ref[i:i + 1, :]
            acc = term if acc is None else acc + term
        y = acc * jax.nn.sigmoid(acc)
        q = y[:, 0:DN_WIDTH]
        k = y[:, DN_WIDTH:2 * DN_WIDTH]
        q_ref[rs, :] = q * lax.rsqrt(jnp.dot((q * q).astype(BF16), seg, preferred_element_type=F32) + L2_EPS)
        k_ref[rs, :] = k * lax.rsqrt(jnp.dot((k * k).astype(BF16), seg, preferred_element_type=F32) + L2_EPS)
        v_ref[rs, :] = y[:, 2 * DN_WIDTH:]
        g = -jnp.exp(gp_ref[0:1, :]) * _softplus(ab + gp_ref[1:2, :])
        beta = jax.nn.sigmoid(ab)
        lane = lax.broadcasted_iota(I32, ab.shape, 1)
        gate_ref[rs, :] = jnp.where(lane < DN_HEADS, g, beta)

    keep = (CONV_W - 1) * shift
    cnew_ref[0] = xp_ref[rc + rows - keep:rc + rows, :]
    xp_ref[0:rc, :] = xp_ref[rows:rows + rc, :]


def _inprep(x2d, nw, w_parts, cinit, conv_w, gate_p, seg, nb, shift):
    n = x2d.shape[0]
    r = n // nb
    rows = min(ROW_TILE, r)
    nt = r // rows
    rc = cinit.shape[1]
    keep = (CONV_W - 1) * shift
    row = lambda b, i: (b * nt + i, 0)
    const = lambda b, i: (0, 0)
    kern = functools.partial(_inprep_kernel, shift=shift, rc=rc, rows=rows)
    outs = pl.pallas_call(
        kern,
        grid=(nb, nt),
        in_specs=[pl.BlockSpec((rows, D_MODEL), row),
                  pl.BlockSpec((1, D_MODEL), const),
                  pl.BlockSpec((D_MODEL, W1_COLS), const),
                  pl.BlockSpec((D_MODEL, W2_COLS), const),
                  pl.BlockSpec((D_MODEL, LANES), const),
                  pl.BlockSpec((1, rc, QKV_DIM), lambda b, i: (b, 0, 0)),
                  pl.BlockSpec((CONV_W, QKV_DIM), const),
                  pl.BlockSpec((2, LANES), const),
                  pl.BlockSpec((DN_WIDTH, DN_WIDTH), const)],
        out_specs=[pl.BlockSpec((rows, DN_WIDTH), row),
                   pl.BlockSpec((rows, DN_WIDTH), row),
                   pl.BlockSpec((rows, DN_WIDTH), row),
                   pl.BlockSpec((rows, LANES), row),
                   pl.BlockSpec((1, keep, QKV_DIM), lambda b, i: (b, 0, 0)),
                   pl.BlockSpec((rows, DN_WIDTH), row),
                   pl.BlockSpec((rows, S5_WIDTH), lambda b, i: (i, b)),
                   pl.BlockSpec((rows, D_MODEL), row),
                   pl.BlockSpec((rows, D_MODEL), row)],
        out_shape=[jax.ShapeDtypeStruct((n, DN_WIDTH), F32),
                   jax.ShapeDtypeStruct((n, DN_WIDTH), F32),
                   jax.ShapeDtypeStruct((n, DN_WIDTH), F32),
                   jax.ShapeDtypeStruct((n, LANES), F32),
                   jax.ShapeDtypeStruct((nb, keep, QKV_DIM), F32),
                   jax.ShapeDtypeStruct((n, DN_WIDTH), BF16),
                   jax.ShapeDtypeStruct((r, nb * S5_WIDTH), BF16),
                   jax.ShapeDtypeStruct((n, D_MODEL), BF16),
                   jax.ShapeDtypeStruct((n, D_MODEL), BF16)],
        scratch_shapes=[pltpu.VMEM((rc + rows, QKV_DIM), F32)],
        compiler_params=_cparams(("arbitrary", "arbitrary")),
        name="inprep",
    )(x2d, nw, *w_parts, cinit, conv_w, gate_p, seg)
    return outs


def _delta_home(low, h, x, other=0.0):
    return jnp.where(low, x, other) if h % 2 == 0 else jnp.where(low, other, x)


def _delta_prepare(q_ref, k_ref, v_ref, gate_ref, tril_ref, bufs, *, nsub):
    sol_buf, wq_buf, qk_buf, kdec_buf, dl_buf = bufs
    c = DN_CHUNK
    dk = DN_HEAD_DIM

    def home(h, x, other=0.0):
        return _delta_home(low, h, x, other)

    rowi2 = lax.broadcasted_iota(I32, (c, 2 * c), 0)
    lane2 = lax.broadcasted_iota(I32, (c, 2 * c), 1)
    coli2 = lane2 & (c - 1)
    causal2 = rowi2 >= coli2
    strict2 = rowi2 > coli2
    low = lane2 < dk
    tril = tril_ref[...]
    pairs = [(j, h) for j in range(nsub) for h in range(DN_HEADS)]
    units = [(j, pr) for j in range(nsub) for pr in range(DN_HEADS // 2)]
    rows = [slice(j * c, (j + 1) * c) for j in range(nsub)]
    gate = [gate_ref[rows[j], :] for j in range(nsub)]
    gc_all = [_split3_dot_left(tril, gate[j]) for j in range(nsub)]
    gc_t = [gc_all[j].T for j in range(nsub)]

    def block(ref, j, pr):
        return ref[rows[j], pr * LANES:(pr + 1) * LANES]

    gfull = {(j, h): jnp.broadcast_to(gc_all[j][:, h:h + 1], (c, 2 * c)) for j, h in pairs}
    g2 = {(j, pr): jnp.where(low, gfull[j, 2 * pr], gfull[j, 2 * pr + 1]) for j, pr in units}
    b2 = {(j, pr): jnp.where(low, gate[j][:, DN_HEADS + 2 * pr:DN_HEADS + 2 * pr + 1],
                             gate[j][:, DN_HEADS + 2 * pr + 1:DN_HEADS + 2 * pr + 2]) for j, pr in units}
    kp = {u: block(k_ref, *u) for u in units}
    qp = {u: block(q_ref, *u) * (dk ** -0.5) for u in units}
    egc2 = {u: jnp.exp(g2[u]) for u in units}
    kb2 = {u: kp[u] * b2[u] for u in units}
    vb2 = {u: block(v_ref, *u) * b2[u] for u in units}
    kw2s = {u: pltpu.roll(kb2[u] * egc2[u], dk, axis=1) for u in units}
    qd2 = {u: qp[u] * egc2[u] for u in units}
    glast2 = {u: g2[u][c - 1:c, :] for u in units}
    kdec_t2 = {u: (kp[u] * jnp.exp(glast2[u] - g2[u])).T for u in units}
    dlast2 = {u: jnp.exp(glast2[u]) for u in units}
    kk = {u: jnp.concatenate([kp[u], kp[u]], axis=0) for u in units}
    yield

    grow2 = {(j, h): jnp.concatenate([gc_t[j][h:h + 1, :], gc_t[j][h:h + 1, :]], axis=1) for j, h in pairs}
    decay = {p: jnp.where(causal2, jnp.exp(jnp.where(causal2, gfull[p] - grow2[p], 0.0)), 0.0) for p in pairs}
    gram = {(j, h): _mm_nt(jnp.concatenate([home(h, kb2[j, h // 2]), home(h, qp[j, h // 2])], axis=0), kk[j, h // 2])
            for j, h in pairs}
    mat = {p: jnp.where(strict2, gram[p][:c] * decay[p], 0.0).astype(BF16) for p in pairs}
    qk = {p: jnp.where(causal2, gram[p][c:] * decay[p], 0.0) for p in pairs}
    sol = {(j, h): home(h, vb2[j, h // 2], kw2s[j, h // 2]) for j, h in pairs}
    yield
    levels = int(math.log2(c))
    zeros2 = jnp.zeros((c, 2 * c), BF16)
    for lvl in range(levels):
        hi = {p: sol[p].astype(BF16) for p in pairs}
        lo = {p: (sol[p] - hi[p].astype(F32)).astype(BF16) for p in pairs}
        if lvl < levels - 1:
            y = {p: jnp.dot(mat[p], jnp.concatenate([jnp.concatenate([hi[p], mat[p]], axis=1),
                                                     jnp.concatenate([lo[p], zeros2], axis=1)], axis=0),
                            preferred_element_type=F32) for p in pairs}
            mat = {p: y[p][:, 2 * dk:].astype(BF16) for p in pairs}
            upd = {p: y[p][:, :2 * dk] for p in pairs}
        else:
            upd = {p: jnp.dot(mat[p], jnp.concatenate([hi[p], lo[p]], axis=0), preferred_element_type=F32)
                   for p in pairs}
        sol = {p: (sol[p] - upd[p]) if lvl == 0 else (sol[p] + upd[p]) for p in pairs}
        yield
    for j, h in pairs:
        n = j * DN_HEADS + h
        sol_buf[n] = sol[j, h]
        wq_buf[n] = jnp.concatenate([home(h, 0.0, sol[j, h]), home(h, qd2[j, h // 2])], axis=0).astype(BF16)
        qk_buf[n] = qk[j, h].astype(BF16)
    for j, pr in units:
        n = j * (DN_HEADS // 2) + pr
        kdec_buf[n] = kdec_t2[j, pr].astype(BF16)
        dl_buf[n] = jnp.broadcast_to(dlast2[j, pr], (SUBLANES, LANES))


def _delta_apply(bufs, o_ref, s_ref, *, nsub):
    sol_buf, wq_buf, qk_buf, kdec_buf, dl_buf = bufs
    c = DN_CHUNK
    dk = DN_HEAD_DIM
    heads = range(DN_HEADS)
    low = lax.broadcasted_iota(I32, (c, 2 * c), 1) < dk
    s = [s_ref[h] for h in heads]
    for j in range(nsub):
        ws, v_new, o_h = [], [], []
        for h in heads:
            n = j * DN_HEADS + h
            ws.append(jnp.dot(wq_buf[n], jnp.concatenate([s[h], s[h]], axis=0).astype(BF16),
                              preferred_element_type=F32))
        yield
        for h in heads:
            v_new.append(sol_buf[j * DN_HEADS + h] - ws[h][:c])
        for h in heads:
            o_h.append(ws[h][c:] + jnp.dot(qk_buf[j * DN_HEADS + h][:, :c], v_new[h].astype(BF16),
                                           preferred_element_type=F32))
        for pr in range(DN_HEADS // 2):
            o_ref[j * c:(j + 1) * c, pr * LANES:(pr + 1) * LANES] = jnp.where(low, o_h[2 * pr], o_h[2 * pr + 1])
        nxt = []
        for h in heads:
            u = j * (DN_HEADS // 2) + h // 2
            kdt = kdec_buf[u][(h % 2) * dk:(h % 2 + 1) * dk, :]
            d = dl_buf[u][0:1, :]
            nxt.append(_delta_home(low, h, s[h] * d + jnp.dot(kdt, v_new[h].astype(BF16),
                                                               preferred_element_type=F32)))
        s = nxt
        yield
    for h in heads:
        s_ref[h] = s[h]


def _delta_chunk_kernel(q_ref, k_ref, v_ref, gate_ref, tril_ref, o_ref, sfin_ref, s_ref, *bufs, nsub, nc):
    i = pl.program_id(0)
    half = len(bufs) // 2
    sets = (bufs[:half], bufs[half:])
    local = lax.rem(jnp.maximum(i - 1, 0), nc)

    @pl.when(i == 0)
    def _():
        for b in sets[1]:
            b[...] = jnp.zeros_like(b)

    @pl.when(local == 0)
    def _():
        s_ref[...] = jnp.zeros_like(s_ref)

    for par in range(2):
        @pl.when(lax.rem(i, 2) == par)
        def _(par=par):
            parts = [_delta_prepare(q_ref, k_ref, v_ref, gate_ref, tril_ref, sets[par], nsub=nsub),
                     _delta_apply(sets[1 - par], o_ref, s_ref, nsub=nsub)]
            while parts:
                parts = [g for g in parts if next(g, StopIteration) is not StopIteration]

    @pl.when(jnp.logical_and(i >= 1, local == nc - 1))
    def _():
        dk = DN_HEAD_DIM
        for h in range(DN_HEADS):
            sfin_ref[0, h] = s_ref[h][:, (h % 2) * dk:(h % 2 + 1) * dk]


def _split3_dot_left(b01, a):
    a1 = a.astype(BF16)
    r1 = a - a1.astype(F32)
    a2 = r1.astype(BF16)
    a3 = (r1 - a2.astype(F32)).astype(BF16)
    out = jnp.dot(b01, a3, preferred_element_type=F32)
    out = out + jnp.dot(b01, a2, preferred_element_type=F32)
    return out + jnp.dot(b01, a1, preferred_element_type=F32)


def _delta_prompt(q, k, v, gate, nb):
    n = q.shape[0]
    t = n // nb
    c = DN_CHUNK
    nsub = DELTA_SUBCHUNKS
    rows = nsub * c
    nc = t // rows
    nblk = nb * nc
    row_in = lambda i: (jnp.minimum(i, nblk - 1), 0)
    row_out = lambda i: (jnp.maximum(i - 1, 0), 0)
    tril = jnp.tril(jnp.ones((c, c), F32)).astype(BF16)
    nh = nsub * DN_HEADS
    npair = nsub * DN_HEADS // 2
    buf_set = [pltpu.VMEM((nh, c, 2 * DN_HEAD_DIM), F32),
               pltpu.VMEM((nh, 2 * c, 2 * DN_HEAD_DIM), BF16),
               pltpu.VMEM((nh, c, 2 * c), BF16),
               pltpu.VMEM((npair, 2 * DN_HEAD_DIM, c), BF16),
               pltpu.VMEM((npair, SUBLANES, LANES), F32)]
    return pl.pallas_call(
        functools.partial(_delta_chunk_kernel, nsub=nsub, nc=nc),
        grid=(nblk + 1,),
        in_specs=[pl.BlockSpec((rows, DN_WIDTH), row_in),
                  pl.BlockSpec((rows, DN_WIDTH), row_in),
                  pl.BlockSpec((rows, DN_WIDTH), row_in),
                  pl.BlockSpec((rows, LANES), row_in),
                  pl.BlockSpec((c, c), lambda i: (0, 0))],
        out_specs=[pl.BlockSpec((rows, DN_WIDTH), row_out),
                   pl.BlockSpec((1, DN_HEADS, DN_HEAD_DIM, DN_HEAD_DIM),
                                lambda i: (jnp.maximum(i - 1, 0) // nc, 0, 0, 0))],
        out_shape=[jax.ShapeDtypeStruct((n, DN_WIDTH), F32),
                   jax.ShapeDtypeStruct((nb, DN_HEADS, DN_HEAD_DIM, DN_HEAD_DIM), F32)],
        scratch_shapes=[pltpu.VMEM((DN_HEADS, DN_HEAD_DIM, 2 * DN_HEAD_DIM), F32)] + buf_set + buf_set,
        compiler_params=_cparams(("arbitrary",)),
        name="delta_prompt",
    )(q, k, v, gate, tril)


def _delta_step_kernel(q_ref, k_ref, v_ref, gate_ref, s0_ref, o_ref, s_ref, kt_ref, qt_ref, gt_ref, *, nt, nb):
    dk = DN_HEAD_DIM
    p = pl.program_id(0)
    for t in range(nt):
        rs = slice(t * nb, (t + 1) * nb)
        gt_ref[...] = gate_ref[rs, :].T
        kt_ref[...] = k_ref[rs, :].T
        qt_ref[...] = (q_ref[rs, :] * (dk ** -0.5)).T
        vt = v_ref[rs, :].T
        src = s0_ref if t == 0 else s_ref
        o_heads = []
        for j in range(2):
            a = jnp.exp(gt_ref[pl.ds(2 * p + j, 1), :])
            beta = gt_ref[pl.ds(2 * p + j + DN_HEADS, 1), :]
            base = j * dk * dk

            def k_dot_s(d, acc, j=j, base=base, src=src):
                sd = src[pl.ds(pl.multiple_of(base + d * dk, dk), dk), :]
                return acc + kt_ref[pl.ds(j * dk + d, 1), :] * sd

            ks = lax.fori_loop(0, dk, k_dot_s, jnp.zeros((dk, nb), F32), unroll=4)
            delta = beta * (vt[j * dk:(j + 1) * dk, :] - a * ks)

            def update(d, acc, j=j, base=base, src=src, a=a, delta=delta):
                r0 = pl.multiple_of(base + d * dk, dk)
                sn = a * src[pl.ds(r0, dk), :] + kt_ref[pl.ds(j * dk + d, 1), :] * delta
                s_ref[pl.ds(r0, dk), :] = sn
                return acc + qt_ref[pl.ds(j * dk + d, 1), :] * sn

            o_heads.append(lax.fori_loop(0, dk, update, jnp.zeros((dk, nb), F32), unroll=4))
        o_ref[rs, :] = jnp.concatenate(o_heads, axis=0).T


def _delta_sample(q, k, v, gate, s0t, nb, nt):
    dk = DN_HEAD_DIM
    flat = dk * dk
    n = nt * nb
    kern = functools.partial(_delta_step_kernel, nt=nt, nb=nb)
    pair = lambda p: (0, p)
    return pl.pallas_call(
        kern,
        grid=(DN_HEADS // 2,),
        in_specs=[pl.BlockSpec((n, LANES), pair),
                  pl.BlockSpec((n, LANES), pair),
                  pl.BlockSpec((n, LANES), pair),
                  pl.BlockSpec((n, LANES), lambda p: (0, 0)),
                  pl.BlockSpec((2 * flat, nb), lambda p: (p, 0))],
        out_specs=[pl.BlockSpec((n, LANES), pair),
                   pl.BlockSpec((2 * flat, nb), lambda p: (p, 0))],
        out_shape=[jax.ShapeDtypeStruct((n, DN_WIDTH), F32),
                   jax.ShapeDtypeStruct((DN_HEADS * flat, nb), F32)],
        scratch_shapes=[pltpu.VMEM((LANES, nb), F32),
                        pltpu.VMEM((LANES, nb), F32),
                        pltpu.VMEM((LANES, nb), F32)],
        compiler_params=_cparams(("arbitrary",)),
        name="delta_sample",
    )(q, k, v, gate, s0t)


def _s5_kernel(u_ref, btre_ref, btim_ref, lam_ref, ctre_ref, ctim_ref, d_ref, h0_ref, y_ref, hfin_ref,
               bw_ref, c_ref, ab_ref, x_ref, h_ref, ru_ref, ry_ref, *, nb, tt, wide):
    p2 = S5_FLAT

    @pl.when(pl.program_id(0) == 0)
    def _():
        lr = lam_ref[0:1, :]
        li = lam_ref[1:2, :]
        dt = jnp.exp(lam_ref[2:3, :])
        mag = jnp.exp(lr * dt)
        ab_re = mag * jnp.cos(li * dt)
        ab_im = mag * jnp.sin(li * dt)
        den = lr * lr + li * li
        nr = ab_re - 1.0
        ni = ab_im
        f_re = (nr * lr + ni * li) / den
        f_im = (ni * lr - nr * li) / den
        ab_ref[0:1, :] = ab_re
        ab_ref[1:2, :] = ab_im
        gpl = LANES // S5_STATE
        ch_g = lax.broadcasted_iota(I32, (S5_WIDTH, LANES), 0) // S5_GROUP_CH
        lane_g = lax.broadcasted_iota(I32, (S5_WIDTH, LANES), 1) // S5_STATE
        bre2 = jnp.concatenate([btre_ref[...]] * gpl, axis=1)
        bim2 = jnp.concatenate([btim_ref[...]] * gpl, axis=1)
        for j in range(p2 // LANES):
            cols = slice(j * LANES, (j + 1) * LANES)
            own = ch_g == gpl * j + lane_g
            bre = jnp.where(own, bre2, 0.0)
            bim = jnp.where(own, bim2, 0.0)
            bw_ref[:, cols] = (bre * f_re[:, cols] - bim * f_im[:, cols]).astype(BF16)
            bw_ref[:, p2 + j * LANES:p2 + (j + 1) * LANES] = (bim * f_re[:, cols] + bre * f_im[:, cols]).astype(BF16)
        cpl = LANES // S5_GROUP_CH
        st_g = lax.broadcasted_iota(I32, (p2, LANES), 0) // S5_STATE
        lane_cg = lax.broadcasted_iota(I32, (p2, LANES), 1) // S5_GROUP_CH
        for j in range(S5_WIDTH // LANES):
            cols = slice(j * LANES, (j + 1) * LANES)
            own = st_g == cpl * j + lane_cg
            c_ref[0:p2, cols] = jnp.where(own, ctre_ref[...], 0.0).astype(BF16)
            c_ref[p2:2 * p2, cols] = jnp.where(own, -ctim_ref[...], 0.0).astype(BF16)
        h_ref[...] = h0_ref[...]

    nck = S5_WIDTH // LANES
    if wide:
        for b in range(nb):
            for ck in range(nck):
                lo = b * S5_WIDTH + ck * LANES
                ru_ref[ck, pl.ds(b, tt, stride=nb), :] = u_ref[:, lo:lo + LANES].astype(F32)
        u = jnp.concatenate([ru_ref[ck] for ck in range(nck)], axis=1)
    else:
        u = u_ref[...].astype(F32)
    ub = u.astype(BF16)
    cw = S5_WIDTH // S5_SUPER
    sw = S5_FLAT // S5_SUPER
    for part in (0, p2):
        for b in range(S5_SUPER):
            x_ref[:, part + b * sw:part + (b + 1) * sw] = jnp.dot(
                ub[:, b * cw:(b + 1) * cw], bw_ref[b * cw:(b + 1) * cw, part + b * sw:part + (b + 1) * sw],
                preferred_element_type=F32)
    a_re = ab_ref[0:1, :]
    a_im = ab_ref[1:2, :]

    if nb == SUBLANES:
        wsl = p2 // S5_SCAN_SPLIT
        for sp in range(S5_SCAN_SPLIT):
            c0 = sp * wsl
            are = jnp.broadcast_to(a_re[:, c0:c0 + wsl], (nb, wsl))
            aim = jnp.broadcast_to(a_im[:, c0:c0 + wsl], (nb, wsl))

            def step(t, carry, c0=c0, are=are, aim=aim):
                hr, hi = carry
                r0 = pl.multiple_of(t * nb, nb)
                nr = are * hr - aim * hi + x_ref[pl.ds(r0, nb), c0:c0 + wsl]
                ni = are * hi + aim * hr + x_ref[pl.ds(r0, nb), p2 + c0:p2 + c0 + wsl]
                x_ref[pl.ds(r0, nb), c0:c0 + wsl] = nr
                x_ref[pl.ds(r0, nb), p2 + c0:p2 + c0 + wsl] = ni
                return nr, ni

            hr, hi = lax.fori_loop(0, tt, step, (h_ref[:, c0:c0 + wsl], h_ref[:, p2 + c0:p2 + c0 + wsl]),
                                   unroll=2)
            h_ref[:, c0:c0 + wsl] = hr
            h_ref[:, p2 + c0:p2 + c0 + wsl] = hi
    else:
        for t in range(tt):
            rs = slice(t * nb, (t + 1) * nb)
            hr = h_ref[:, 0:p2]
            hi = h_ref[:, p2:2 * p2]
            nr = a_re * hr - a_im * hi + x_ref[rs, 0:p2]
            ni = a_re * hi + a_im * hr + x_ref[rs, p2:2 * p2]
            h_ref[:, 0:p2] = nr
            h_ref[:, p2:2 * p2] = ni
            x_ref[rs, 0:p2] = nr
            x_ref[rs, p2:2 * p2] = ni

    for b in range(S5_SUPER):
        cols = slice(b * cw, (b + 1) * cw)
        y = None
        for part in (0, p2):
            rws = slice(part + b * sw, part + (b + 1) * sw)
            term = jnp.dot(x_ref[:, rws].astype(BF16), c_ref[rws, cols], preferred_element_type=F32)
            y = term if y is None else y + term
        y = y + d_ref[:, cols] * u[:, cols]
        if wide:
            for ck in range(cw // LANES):
                ry_ref[b * (cw // LANES) + ck] = y[:, ck * LANES:(ck + 1) * LANES]
        else:
            y_ref[:, cols] = y
    if wide:
        for b in range(nb):
            for ck in range(nck):
                lo = b * S5_WIDTH + ck * LANES
                y_ref[:, lo:lo + LANES] = ry_ref[ck, pl.ds(b, tt, stride=nb), :]
    hfin_ref[...] = h_ref[...]


def _s5(u, params, h0, nb, wide):
    btre, btim, lam, ctre, ctim, dvec = params
    t = u.shape[0] if wide else u.shape[0] // nb
    tt = min(ROW_TILE // nb, t)
    rows = tt * nb
    const = lambda i: (0, 0)
    kern = functools.partial(_s5_kernel, nb=nb, tt=tt, wide=wide)
    io_block = (tt, nb * S5_WIDTH) if wide else (rows, S5_WIDTH)
    return pl.pallas_call(
        kern,
        grid=(t // tt,),
        in_specs=[pl.BlockSpec(io_block, lambda i: (i, 0)),
                  pl.BlockSpec((S5_WIDTH, S5_STATE), const),
                  pl.BlockSpec((S5_WIDTH, S5_STATE), const),
                  pl.BlockSpec((SUBLANES, S5_FLAT), const),
                  pl.BlockSpec((S5_FLAT, LANES), const),
                  pl.BlockSpec((S5_FLAT, LANES), const),
                  pl.BlockSpec((1, S5_WIDTH), const),
                  pl.BlockSpec((nb, 2 * S5_FLAT), const)],
        out_specs=[pl.BlockSpec(io_block, lambda i: (i, 0)),
                   pl.BlockSpec((nb, 2 * S5_FLAT), const)],
        out_shape=[jax.ShapeDtypeStruct(u.shape, F32),
                   jax.ShapeDtypeStruct((nb, 2 * S5_FLAT), F32)],
        scratch_shapes=[pltpu.VMEM((S5_WIDTH, 2 * S5_FLAT), BF16),
                        pltpu.VMEM((2 * S5_FLAT, S5_WIDTH), BF16),
                        pltpu.VMEM((SUBLANES, S5_FLAT), F32),
                        pltpu.VMEM((rows, 2 * S5_FLAT), F32),
                        pltpu.VMEM((nb, 2 * S5_FLAT), F32),
                        pltpu.VMEM((S5_WIDTH // LANES, rows, LANES), F32),
                        pltpu.VMEM((S5_WIDTH // LANES, rows, LANES), F32)],
        compiler_params=_cparams(("arbitrary",)),
        name="s5",
    )(u, btre, btim, lam, ctre, ctim, dvec, h0)


def _postmix_kernel(xp_ref, op_ref, zp_ref, ysp_ref, gap_ref, gbp_ref,
                    xs_ref, os_ref, zs_ref, yss_ref, gas_ref, gbs_ref, *rest, nblk_p, range_tok):
    carry_ref = rest[-1]

    @pl.when(pl.program_id(0) == 0)
    def _():
        carry_ref[...] = jnp.zeros_like(carry_ref)

    @pl.when(pl.program_id(0) < nblk_p)
    def _():
        _postmix_body(xp_ref, op_ref, zp_ref, ysp_ref, gap_ref, gbp_ref, *rest, range_tok=range_tok)

    @pl.when(pl.program_id(0) >= nblk_p)
    def _():
        _postmix_body(xs_ref, os_ref, zs_ref, yss_ref, gas_ref, gbs_ref, *rest, range_tok=range_tok)


def _postmix_body(x_ref, o_ref, z_ref, ys_ref, ga_ref, gb_ref, hw_ref, seg_ref, wa_ref, wglu_ref, wb_ref,
                  wo_ref, nf_ref, wr_ref, su_ref, x1_ref, hn_ref, bkt_ref, rank_ref, rw_ref, cnt_ref, carry_ref,
                  *, range_tok):
    o = o_ref[...]
    ms = jnp.dot((o * o).astype(BF16), seg_ref[...], preferred_element_type=F32) * (1.0 / DN_HEAD_DIM)
    on = o * lax.rsqrt(ms + RMS_EPS) * hw_ref[...]
    z = z_ref[...]
    oa = on * (z * jax.nn.sigmoid(z)).astype(F32)
    y_a = _mm(oa, wa_ref[...])
    ys = jax.nn.gelu(ys_ref[...])
    ys = ys * jax.nn.sigmoid(_mm(ys, wglu_ref[...]))
    y_b = _mm(ys, wb_ref[...])
    mixed = jax.nn.sigmoid(ga_ref[...]).astype(F32) * y_a + jax.nn.sigmoid(gb_ref[...]).astype(F32) * y_b
    x1 = x_ref[...] + _mm(mixed, wo_ref[...])
    x1_ref[...] = x1
    hn = x1 * lax.rsqrt(jnp.mean(x1 * x1, axis=-1, keepdims=True) + RMS_EPS) * nf_ref[...]
    _slab_store(hn_ref, hn)

    wr = wr_ref[...]
    w_hi = wr.astype(BF16)
    w_lo = (wr - w_hi.astype(F32)).astype(BF16)
    hn_hi = hn.astype(BF16)
    hn_lo = (hn - hn_hi.astype(F32)).astype(BF16)
    both = _mm_nt(jnp.concatenate([w_hi, w_lo], axis=0), hn_hi)
    logits = both[:ROUTER_ROWS] + both[ROUTER_ROWS:] + _mm_nt(w_hi, hn_lo)
    coarse = logits[N_EXPERTS:N_EXPERTS + MOE_GROUPS, :]
    cm = jnp.max(coarse, axis=0, keepdims=True)
    ce = jnp.exp(coarse - cm)
    pc = ce / jnp.sum(ce, axis=0, keepdims=True)
    p_sel = jnp.max(pc, axis=0, keepdims=True)
    gi = lax.broadcasted_iota(I32, pc.shape, 0)
    g_sel = jnp.min(jnp.where(pc == p_sel, gi, MOE_GROUPS), axis=0, keepdims=True)
    fine = jnp.zeros((EXPERTS_PER_GROUP, logits.shape[1]), F32)
    for g in range(MOE_GROUPS):
        fine = fine + jnp.where(g_sel == g, logits[g * EXPERTS_PER_GROUP:(g + 1) * EXPERTS_PER_GROUP, :], 0.0)
    fm = jnp.max(fine, axis=0, keepdims=True)
    fe = jnp.exp(fine - fm)
    pf = fe / jnp.sum(fe, axis=0, keepdims=True)
    ei = lax.broadcasted_iota(I32, pf.shape, 0)
    v1 = jnp.max(pf, axis=0, keepdims=True)
    i1 = jnp.min(jnp.where(pf == v1, ei, EXPERTS_PER_GROUP), axis=0, keepdims=True)
    rest = jnp.where(ei == i1, -1.0, pf)
    v2 = jnp.max(rest, axis=0, keepdims=True)
    i2 = jnp.min(jnp.where(rest == v2, ei, EXPERTS_PER_GROUP), axis=0, keepdims=True)
    tot = v1 + v2
    rw_ref[0:1, :] = v1 / tot * p_sel
    rw_ref[1:2, :] = v2 / tot * p_sel

    tt = logits.shape[1]
    tok = pl.program_id(0) * tt + lax.broadcasted_iota(I32, (1, tt), 1)
    ph = jnp.zeros((1, tt), I32)
    for r in range(1, MOE_PHASES):
        ph = ph + (tok >= r * range_tok).astype(I32)
    bsel = [ph * N_EXPERTS + g_sel * EXPERTS_PER_GROUP + ix for ix in (i1, i2)]
    bi = lax.broadcasted_iota(I32, (MOE_PHASES * N_EXPERTS, tt), 0)
    onehot = [(bi == b).astype(F32) for b in bsel]
    cnt = onehot[0] + onehot[1]
    before = carry_ref[:, 0:1] + jnp.dot(cnt.astype(BF16), su_ref[...], preferred_element_type=F32)
    for s in range(TOP_K):
        bkt_ref[s:s + 1, :] = bsel[s]
        rank_ref[s:s + 1, :] = jnp.sum(onehot[s] * before, axis=0, keepdims=True).astype(I32)
    carry_ref[...] = carry_ref[...] + jnp.sum(cnt, axis=1, keepdims=True)
    cnt_ref[...] = carry_ref[...]


def _postmix(prompt, sample, weights, nb):
    n_p = prompt[0].shape[0]
    n_s = sample[0].shape[0]
    t = n_p // nb
    tt = min(ROW_TILE, t, n_s)
    nt = t // tt
    nblk_p = n_p // tt
    nblk = nblk_p + n_s // tt
    n_total = n_p + n_s
    prow = lambda i: (jnp.minimum(i, nblk_p - 1), 0)
    pys = lambda i: (jnp.minimum(i, nblk_p - 1) % nt, jnp.minimum(i, nblk_p - 1) // nt)
    srow = lambda i: (jnp.maximum(i - nblk_p, 0), 0)
    const = lambda i: (0, 0)

    def stream_specs(row, ysmap):
        return [pl.BlockSpec((tt, D_MODEL), row),
                pl.BlockSpec((tt, DN_WIDTH), row),
                pl.BlockSpec((tt, DN_WIDTH), row),
                pl.BlockSpec((tt, S5_WIDTH), ysmap),
                pl.BlockSpec((tt, D_MODEL), row),
                pl.BlockSpec((tt, D_MODEL), row)]

    weight_specs = [pl.BlockSpec((1, DN_WIDTH), const),
                    pl.BlockSpec((DN_WIDTH, DN_WIDTH), const),
                    pl.BlockSpec((DN_WIDTH, D_MODEL), const),
                    pl.BlockSpec((S5_WIDTH, S5_WIDTH), const),
                    pl.BlockSpec((S5_WIDTH, D_MODEL), const),
                    pl.BlockSpec((D_MODEL, D_MODEL), const),
                    pl.BlockSpec((1, D_MODEL), const),
                    pl.BlockSpec((ROUTER_ROWS, D_MODEL), const),
                    pl.BlockSpec((tt, tt), const)]
    xp, op, zp, ysp, gap, gbp = prompt
    nbk = MOE_PHASES * N_EXPERTS
    earlier = jnp.triu(jnp.ones((tt, tt), F32), k=1).astype(BF16)
    return pl.pallas_call(
        functools.partial(_postmix_kernel, nblk_p=nblk_p, range_tok=n_total // MOE_PHASES),
        grid=(nblk,),
        in_specs=stream_specs(prow, pys) + stream_specs(srow, srow) + weight_specs,
        out_specs=[pl.BlockSpec((tt, D_MODEL), lambda i: (i, 0)),
                   pl.BlockSpec((tt * ROW_SLAB, LANES), lambda i: (i, 0)),
                   pl.BlockSpec((TOP_K, tt), lambda i: (0, i)),
                   pl.BlockSpec((TOP_K, tt), lambda i: (0, i)),
                   pl.BlockSpec((TOP_K, tt), lambda i: (0, i)),
                   pl.BlockSpec((nbk, LANES), const)],
        out_shape=[jax.ShapeDtypeStruct((n_total, D_MODEL), F32),
                   jax.ShapeDtypeStruct((n_total * ROW_SLAB, LANES), F32),
                   jax.ShapeDtypeStruct((TOP_K, n_total), I32),
                   jax.ShapeDtypeStruct((TOP_K, n_total), I32),
                   jax.ShapeDtypeStruct((TOP_K, n_total), F32),
                   jax.ShapeDtypeStruct((nbk, LANES), F32)],
        scratch_shapes=[pltpu.VMEM((nbk, LANES), F32)],
        compiler_params=_cparams(("arbitrary",)),
        name="postmix",
    )(xp, op, zp, ysp, gap, gbp, *sample, *weights, earlier)


def _wait_slabs(buf, sem):
    pltpu.make_async_copy(buf, buf, sem).wait()


def _moe_kernel(texp_ref, tph_ref, tsrc_ref, tnv_ref, tfirst_ref, tslot_ref, tnext_ref, otok_ref,
                hn_hbm, wu_hbm, wd_hbm, y_ref, hnv, xbuf, wu_buf, wd_buf, wub, wdb, sem, wsem):
    i = pl.program_id(0)
    tm = MOE_TILE
    rs = ROW_SLAB
    nv = tnv_ref[i]
    ph = tph_ref[i]
    range_rows = hnv.shape[0]

    def weight_copies(e, sl):
        return (pltpu.make_async_copy(wu_hbm.at[e], wu_buf.at[sl], wsem.at[sl]),
                pltpu.make_async_copy(wd_hbm.at[e], wd_buf.at[sl], wsem.at[sl]))

    @pl.when(i == 0)
    def _():
        for c in weight_copies(texp_ref[0], 0):
            c.start()

    @pl.when(jnp.logical_and(nv > 0, jnp.logical_or(i == 0, ph != tph_ref[jnp.maximum(i - 1, 0)])))
    def _():
        start = pl.multiple_of(ph * range_rows, rs)
        whole = pltpu.make_async_copy(hn_hbm.at[pl.ds(start, range_rows), :], hnv, sem)
        whole.start()
        whole.wait()

    for sl in range(2):
        @pl.when(jnp.logical_and(jnp.logical_and(nv > 0, tfirst_ref[i] == 1), tslot_ref[i] == sl))
        def _():
            for c in weight_copies(texp_ref[i], sl):
                c.wait()

            @pl.when(tnext_ref[i] >= 0)
            def _():
                for c in weight_copies(tnext_ref[i], 1 - sl):
                    c.start()

            wub[...] = wu_buf[sl].astype(BF16)
            wdb[...] = wd_buf[sl].astype(BF16)

    @pl.when(nv == 0)
    def _():
        y_ref[...] = jnp.zeros_like(y_ref)

    @pl.when(nv > 0)
    def _():
        src0 = tsrc_ref[i]
        for r in range(tm):
            tok8 = pl.multiple_of(otok_ref[src0 + r], rs)
            xbuf[pl.ds(r * rs, rs), :] = hnv[pl.ds(tok8, rs), :]
        x = _slab_load(xbuf, tm).astype(BF16)
        hu = jnp.dot(x, wub[...], preferred_element_type=F32)
        gate = hu[:, :EXPERT_FF]
        up = hu[:, EXPERT_FF:]
        act = gate * jax.nn.sigmoid(gate) * up
        _slab_store(y_ref, jnp.dot(act.astype(BF16), wdb[...], preferred_element_type=F32))


def _moe(hn, w_up, w_down, plan):
    ntiles = plan[0].shape[0]
    grid_spec = pltpu.PrefetchScalarGridSpec(
        num_scalar_prefetch=len(plan),
        grid=(ntiles,),
        in_specs=[pl.BlockSpec(memory_space=pl.ANY),
                  pl.BlockSpec(memory_space=pl.ANY),
                  pl.BlockSpec(memory_space=pl.ANY)],
        out_specs=pl.BlockSpec((MOE_TILE * ROW_SLAB, LANES), lambda i, *_: (i, 0)),
        scratch_shapes=[pltpu.VMEM((hn.shape[0] // MOE_PHASES, LANES), F32),
                        pltpu.VMEM((MOE_TILE * ROW_SLAB, LANES), F32),
                        pltpu.VMEM((2, D_MODEL, 2 * EXPERT_FF), F32),
                        pltpu.VMEM((2, EXPERT_FF, D_MODEL), F32),
                        pltpu.VMEM((D_MODEL, 2 * EXPERT_FF), BF16),
                        pltpu.VMEM((EXPERT_FF, D_MODEL), BF16),
                        pltpu.SemaphoreType.DMA,
                        pltpu.SemaphoreType.DMA((2,))])
    return pl.pallas_call(
        _moe_kernel,
        grid_spec=grid_spec,
        out_shape=jax.ShapeDtypeStruct((ntiles * MOE_TILE * ROW_SLAB, LANES), F32),
        compiler_params=_cparams(("arbitrary",)),
        name="moe",
    )(*plan, hn, w_up, w_down)


def _combine_kernel(pos_ref, x1_ref, ys_hbm, w_ref, nw_ref, outp_ref, outs_ref,
                    ybuf0, ybuf1, sem, *, nblk_p, n_tok):
    i = pl.program_id(0)
    nsteps = pl.num_programs(0)
    tt = x1_ref.shape[0]
    rs = ROW_SLAB
    slot = lax.rem(i, 2)
    ybuf = (ybuf0, ybuf1)

    def start_gather(step, sl):
        base = step * tt
        for r in range(tt * TOP_K):
            j, s = divmod(r, TOP_K)
            p8 = pl.multiple_of(pos_ref[s * n_tok + base + j], rs)
            pltpu.make_async_copy(ys_hbm.at[pl.ds(p8, rs), :], ybuf[sl].at[pl.ds((s * tt + j) * rs, rs), :],
                                  sem.at[sl]).start(priority=r % DMA_QUEUES)

    @pl.when(i == 0)
    def _():
        start_gather(0, 0)

    for sl in range(2):
        @pl.when(slot == sl)
        def _():
            _wait_slabs(ybuf[sl], sem.at[sl])
            start_gather(jnp.minimum(i + 1, nsteps - 1), 1 - sl)
            w = w_ref[...]
            y0 = _slab_load(ybuf[sl], tt, 0)
            y1 = _slab_load(ybuf[sl], tt, tt * rs)
            x = x1_ref[...] + w[:, 0:1] * y0 + w[:, 1:2] * y1
            res = x * lax.rsqrt(jnp.mean(x * x, axis=-1, keepdims=True) + RMS_EPS) * nw_ref[...]

            @pl.when(i < nblk_p)
            def _():
                outp_ref[...] = res

            @pl.when(i >= nblk_p)
            def _():
                outs_ref[...] = res

        @pl.when(jnp.logical_and(slot == sl, i == nsteps - 1))
        def _():
            _wait_slabs(ybuf[1 - sl], sem.at[1 - sl])


def _combine(x1, ysorted, pos8, wtok, nw, n_p):
    n = x1.shape[0]
    tt = math.gcd(math.gcd(n_p, n - n_p), COMBINE_TILE)
    nblk_p = n_p // tt
    grid_spec = pltpu.PrefetchScalarGridSpec(
        num_scalar_prefetch=1,
        grid=(n // tt,),
        in_specs=[pl.BlockSpec((tt, D_MODEL), lambda i, *_: (i, 0)),
                  pl.BlockSpec(memory_space=pl.ANY),
                  pl.BlockSpec((tt, TOP_K), lambda i, *_: (i, 0)),
                  pl.BlockSpec((1, D_MODEL), lambda i, *_: (0, 0))],
        out_specs=[pl.BlockSpec((tt, D_MODEL), lambda i, *_: (jnp.minimum(i, nblk_p - 1), 0)),
                   pl.BlockSpec((tt, D_MODEL), lambda i, *_: (jnp.maximum(i - nblk_p, 0), 0))],
        scratch_shapes=[pltpu.VMEM((tt * TOP_K * ROW_SLAB, LANES), F32),
                        pltpu.VMEM((tt * TOP_K * ROW_SLAB, LANES), F32),
                        pltpu.SemaphoreType.DMA((2,))])
    return pl.pallas_call(
        functools.partial(_combine_kernel, nblk_p=nblk_p, n_tok=n),
        grid_spec=grid_spec,
        out_shape=[jax.ShapeDtypeStruct((n_p, D_MODEL), F32),
                   jax.ShapeDtypeStruct((n - n_p, D_MODEL), F32)],
        compiler_params=_cparams(("arbitrary",)),
        name="combine",
    )(pos8, x1, ysorted, wtok, nw)


def _route_plan(bkt, rank, cnt, n_tok):
    tm = MOE_TILE
    n_assign = n_tok * TOP_K
    nbk = MOE_PHASES * N_EXPERTS
    ntiles = n_assign // tm + nbk
    range_tok = n_tok // MOE_PHASES
    b_flat = bkt.T.reshape(n_assign)
    order = jnp.argsort(b_flat, stable=True).astype(I32)
    counts = cnt[:, 0].astype(I32)
    cstart = jnp.cumsum(counts) - counts
    tiles_b = (counts + tm - 1) // tm
    tend = jnp.cumsum(tiles_b)
    tstart = tend - tiles_b
    tile_id = jnp.arange(ntiles, dtype=I32)
    tbk = jnp.minimum(jnp.sum((tile_id[:, None] >= tend[None, :]).astype(I32), axis=1), nbk - 1)
    onehot = (tbk[:, None] == jnp.arange(nbk, dtype=I32)[None, :]).astype(I32)
    pick = lambda v: jnp.sum(onehot * v[None, :], axis=1)
    done = (tile_id - pick(tstart)) * tm
    tnv = jnp.where(tile_id < tend[-1], jnp.clip(pick(counts) - done, 0, tm), 0)
    tsrc = jnp.where(tnv > 0, pick(cstart) + done, 0)
    texp = tbk % N_EXPERTS
    tph = tbk // N_EXPERTS
    nonempty = counts > 0
    bslot = (jnp.cumsum(nonempty.astype(I32)) - 1) % 2
    bidx = jnp.where(nonempty, jnp.arange(nbk, dtype=I32), nbk)
    nxt = jnp.concatenate([lax.cummin(bidx[::-1])[::-1][1:], jnp.full((1,), nbk, I32)])
    bnext = jnp.where(nxt < nbk, nxt % N_EXPERTS, -1)
    tfirst = jnp.logical_and(tnv > 0, done == 0).astype(I32)
    tslot = pick(bslot)
    tnext = pick(bnext)
    otok8 = jnp.concatenate([((order // TOP_K) % range_tok) * ROW_SLAB, jnp.zeros((tm,), I32)])
    plan = tuple(a.astype(I32) for a in (texp, tph, tsrc, tnv, tfirst, tslot, tnext, otok8))
    first = jnp.sum((bkt[:, :, None] == jnp.arange(nbk, dtype=I32)[None, None, :]).astype(I32)
                    * (tstart * tm)[None, None, :], axis=2)
    pos8 = ((first + rank) * ROW_SLAB).reshape(n_assign)
    return plan, pos8.astype(I32)


def _block_diag(m):
    g, a, b = m.shape
    eye = jnp.eye(g, dtype=m.dtype)
    return (eye[:, None, :, None] * m[:, :, None, :]).reshape(g * a, g * b)


def kernel(x_prompt, x_sample, state_conv, state_delta, state_ssm_re, state_ssm_im, norm_mix_w, w_in, conv_w, a_log, dt_bias, head_norm_w, w_a_up, s5_lambda_re, s5_lambda_im, s5_log_step, s5_b_re, s5_b_im, s5_c_re, s5_c_im, s5_d, w_glu, w_b_up, w_o, norm_ffn_w, w_router_coarse, w_router_fine, w_expert_up, w_expert_down, norm_final_w):
    bp, tp, _ = x_prompt.shape
    bs, ts, _ = x_sample.shape
    n_p = bp * tp
    n_s = bs * ts
    n_tok = n_p + n_s
    l = 0

    w = w_in[l].astype(BF16)
    c_ab = W1_COLS + 2 * DN_HEADS
    w_parts = (w[:, :W1_COLS], w[:, c_ab:],
               jnp.concatenate([w[:, W1_COLS:c_ab], jnp.zeros((D_MODEL, LANES - 2 * DN_HEADS), BF16)], axis=1))
    nw_mix = norm_mix_w[l].reshape(1, D_MODEL)
    pad8 = lambda v: jnp.concatenate([v, jnp.zeros((LANES - DN_HEADS,), F32)]).reshape(1, LANES)
    gate_p = jnp.concatenate([pad8(a_log[l]), pad8(dt_bias[l])], axis=0)
    seg = _block_diag(jnp.ones((DN_HEADS, DN_HEAD_DIM, DN_HEAD_DIM), BF16))
    chan_rows = lambda b: jnp.swapaxes(b, 1, 2).reshape(S5_WIDTH, S5_STATE)
    state_rows = lambda c: jnp.tile(jnp.swapaxes(c, 1, 2).reshape(S5_FLAT, S5_GROUP_CH),
                                    (1, LANES // S5_GROUP_CH))
    lam = jnp.concatenate([s5_lambda_re[l].reshape(1, S5_FLAT), s5_lambda_im[l].reshape(1, S5_FLAT),
                           jnp.repeat(s5_log_step[l], S5_STATE).reshape(1, S5_FLAT),
                           jnp.zeros((SUBLANES - 3, S5_FLAT), F32)], axis=0)
    s5_params = (chan_rows(s5_b_re[l]), chan_rows(s5_b_im[l]), lam,
                 state_rows(s5_c_re[l]), state_rows(s5_c_im[l]), s5_d[l].reshape(1, S5_WIDTH))
    hw = jnp.tile(head_norm_w[l], DN_HEADS).reshape(1, DN_WIDTH)
    wr = jnp.concatenate([w_router_fine[l].T, w_router_coarse[l].T,
                          jnp.zeros((ROUTER_ROWS - N_EXPERTS - MOE_GROUPS, D_MODEL), F32)], axis=0)
    pm_weights = (hw, seg, w_a_up[l].astype(BF16), w_glu[l].astype(BF16), w_b_up[l].astype(BF16),
                  w_o[l].astype(BF16), norm_ffn_w[l].reshape(1, D_MODEL), wr)

    xp2 = x_prompt.reshape(n_p, D_MODEL)
    q_p, k_p, v_p, gates_p, conv_p, z_p, u_p, ga_p, gb_p = _inprep(
        xp2, nw_mix, w_parts, jnp.zeros((bp, SUBLANES, QKV_DIM), F32), conv_w[l], gate_p, seg, bp, 1)
    o_p, delta_p = _delta_prompt(q_p, k_p, v_p, gates_p, bp)
    ys_p, h_p = _s5(u_p, s5_params, jnp.zeros((bp, 2 * S5_FLAT), F32), bp, True)

    xs2 = jnp.swapaxes(x_sample, 0, 1).reshape(n_s, D_MODEL)
    cinit_s = jnp.swapaxes(state_conv[l], 0, 1).reshape(1, (CONV_W - 1) * bs, QKV_DIM)
    q_s, k_s, v_s, gate_s, conv_s, z_s, u_s, ga_s, gb_s = _inprep(
        xs2, nw_mix, w_parts, cinit_s, conv_w[l], gate_p, seg, 1, bs)
    s0t = jnp.transpose(state_delta[l], (1, 2, 3, 0)).reshape(DN_HEADS * DN_HEAD_DIM * DN_HEAD_DIM, bs)
    o_s, delta_st = _delta_sample(q_s, k_s, v_s, gate_s, s0t, bs, ts)
    delta_s = jnp.transpose(delta_st.reshape(DN_HEADS, DN_HEAD_DIM, DN_HEAD_DIM, bs), (3, 0, 1, 2))
    h0_s = jnp.concatenate([state_ssm_re[l].reshape(bs, S5_FLAT), state_ssm_im[l].reshape(bs, S5_FLAT)], axis=1)
    ys_s, h_s = _s5(u_s, s5_params, h0_s, bs, False)
    x1, hn, bkt, rank, rw, cnt = _postmix((xp2, o_p, z_p, ys_p, ga_p, gb_p), (xs2, o_s, z_s, ys_s, ga_s, gb_s),
                                          pm_weights, bp)

    plan, pos8 = _route_plan(bkt, rank, cnt, n_tok)
    ysorted = _moe(hn, w_expert_up[l], w_expert_down[l], plan)
    y_p, y_s = _combine(x1, ysorted, pos8, rw.T, norm_final_w.reshape(1, D_MODEL), n_p)

    y_prompt = y_p.reshape(bp, tp, D_MODEL)
    y_sample = jnp.swapaxes(y_s.reshape(ts, bs, D_MODEL), 0, 1)
    conv_sample = jnp.swapaxes(conv_s.reshape(CONV_W - 1, bs, QKV_DIM), 0, 1)
    return (y_prompt, y_sample,
            conv_p[None], delta_p[None],
            h_p[:, :S5_FLAT].reshape(1, bp, S5_GROUPS, S5_STATE), h_p[:, S5_FLAT:].reshape(1, bp, S5_GROUPS, S5_STATE),
            conv_sample[None], delta_s[None],
            h_s[:, :S5_FLAT].reshape(1, bs, S5_GROUPS, S5_STATE), h_s[:, S5_FLAT:].reshape(1, bs, S5_GROUPS, S5_STATE))
```

```python
import functools
import math

import jax
import jax.numpy as jnp
import numpy as np
from jax import lax
from jax.experimental import pallas as pl
from jax.experimental.pallas import tpu as pltpu

F32 = jnp.float32
BF16 = jnp.bfloat16
I32 = jnp.int32

D_MODEL = 1024
DN_HEADS = 8
DN_HEAD_DIM = 64
DN_WIDTH = DN_HEADS * DN_HEAD_DIM
QKV_DIM = 3 * DN_WIDTH
CONV_W = 4
DN_CHUNK = 64
S5_GROUP_CH = 16
S5_WIDTH = D_MODEL // 2
S5_GROUPS = S5_WIDTH // S5_GROUP_CH
S5_STATE = 64
S5_FLAT = S5_GROUPS * S5_STATE
MOE_GROUPS = 4
EXPERTS_PER_GROUP = 8
N_EXPERTS = MOE_GROUPS * EXPERTS_PER_GROUP
TOP_K = 2
EXPERT_FF = 256
RMS_EPS = 1e-6
L2_EPS = 1e-6

LANES = 128
SUBLANES = 8
VMEM_LIMIT = 56 * 1024 * 1024

W1_COLS = QKV_DIM + DN_WIDTH
W2_COLS = S5_WIDTH + 2 * D_MODEL

ROW_TILE = 512
INPREP_PARTS = 2
POSTMIX_PARTS = 2
MOE_TILE = 256
MOE_PHASES = 2
COMBINE_TILE = 256
DMA_QUEUES = 2
DELTA_SUBCHUNKS = 4
S5_SUPER = 2
S5_SCAN_SPLIT = 2
ROUTER_ROWS = 40


def _mm(a, b):
    return jnp.dot(a.astype(BF16), b.astype(BF16), preferred_element_type=F32)


def _mm_nt(a, b):
    return lax.dot_general(a.astype(BF16), b.astype(BF16), (((1,), (1,)), ((), ())),
                           preferred_element_type=F32)


def _split3_dot(a, b01):
    a1 = a.astype(BF16)
    r1 = a - a1.astype(F32)
    a2 = r1.astype(BF16)
    a3 = (r1 - a2.astype(F32)).astype(BF16)
    out = jnp.dot(a3, b01, preferred_element_type=F32)
    out = out + jnp.dot(a2, b01, preferred_element_type=F32)
    return out + jnp.dot(a1, b01, preferred_element_type=F32)


def _cparams(sem):
    return pltpu.CompilerParams(dimension_semantics=sem, vmem_limit_bytes=VMEM_LIMIT)


ROW_SLAB = D_MODEL // LANES


def _slab_load(ref, rows, first=0, pitch=ROW_SLAB):
    return jnp.concatenate([ref[pl.ds(first + j, rows, stride=pitch), :] for j in range(ROW_SLAB)], axis=1)


def _slab_store(ref, x, first=0):
    for j in range(ROW_SLAB):
        ref[pl.ds(first * ROW_SLAB + j, x.shape[0], stride=ROW_SLAB), :] = x[:, j * LANES:(j + 1) * LANES]


def _softplus(x):
    return jnp.maximum(x, 0.0) + jnp.log1p(jnp.exp(-jnp.abs(x)))


def _inprep_kernel(x_ref, nw_ref, w1_ref, w2_ref, wab_ref, cinit_ref, cw_ref, gp_ref, seg_ref,
                   q_ref, k_ref, v_ref, gate_ref, cnew_ref, z_ref, u_ref, ga_ref, gb_ref, xp_ref,
                   *, shift, rc, rows):
    @pl.when(pl.program_id(1) == 0)
    def _():
        xp_ref[0:rc, :] = cinit_ref[0]

    seg = seg_ref[...]
    pr = rows // INPREP_PARTS

    def part_stages(part):
        rs = slice(part * pr, (part + 1) * pr)
        x = x_ref[rs, :]
        h = x * lax.rsqrt(jnp.mean(x * x, axis=-1, keepdims=True) + RMS_EPS) * nw_ref[...]
        hb = h.astype(BF16)

        def proj(w_ref, lo, hi):
            return jnp.dot(hb, w_ref[:, lo:hi], preferred_element_type=F32)

        xp_ref[rc + part * pr:rc + (part + 1) * pr, :] = proj(w1_ref, 0, QKV_DIM)
        ab = proj(wab_ref, 0, LANES)
        yield
        z_ref[rs, :] = proj(w1_ref, QKV_DIM, W1_COLS).astype(z_ref.dtype)
        u_ref[rs, :] = proj(w2_ref, 0, S5_WIDTH).astype(u_ref.dtype)
        acc = None
        for i in range(CONV_W):
            lo = rc + part * pr + (i - (CONV_W - 1)) * shift
            term = xp_ref[lo:lo + pr, :] * cw_ref[i:i + 1, :]
            acc = term if acc is None else acc + term
        y = acc * jax.nn.sigmoid(acc)
        yield
        ga_ref[rs, :] = proj(w2_ref, S5_WIDTH, S5_WIDTH + D_MODEL).astype(ga_ref.dtype)
        q = y[:, 0:DN_WIDTH]
        k = y[:, DN_WIDTH:2 * DN_WIDTH]
        q_ref[rs, :] = q * lax.rsqrt(jnp.dot((q * q).astype(BF16), seg, preferred_element_type=F32) + L2_EPS)
        k_ref[rs, :] = k * lax.rsqrt(jnp.dot((k * k).astype(BF16), seg, preferred_element_type=F32) + L2_EPS)
        v_ref[rs, :] = y[:, 2 * DN_WIDTH:]
        yield
        gb_ref[rs, :] = proj(w2_ref, S5_WIDTH + D_MODEL, W2_COLS).astype(gb_ref.dtype)
        g = -jnp.exp(gp_ref[0:1, :]) * _softplus(ab + gp_ref[1:2, :])
        beta = jax.nn.sigmoid(ab)
        lane = lax.broadcasted_iota(I32, ab.shape, 1)
        gate_ref[rs, :] = jnp.where(lane < DN_HEADS, g, beta)

    live = []
    pending = [part_stages(p) for p in range(INPREP_PARTS)]
    while live or pending:
        if pending:
            live.append(pending.pop(0))
        live = [g for g in live if next(g, StopIteration) is not StopIteration]

    keep = (CONV_W - 1) * shift
    cnew_ref[0] = xp_ref[rc + rows - keep:rc + rows, :]
    xp_ref[0:rc, :] = xp_ref[rows:rows + rc, :]


def _inprep(x2d, nw, w_parts, cinit, conv_w, gate_p, seg, nb, shift):
    n = x2d.shape[0]
    r = n // nb
    rows = min(ROW_TILE, r)
    nt = r // rows
    rc = cinit.shape[1]
    keep = (CONV_W - 1) * shift
    row = lambda b, i: (b * nt + i, 0)
    const = lambda b, i: (0, 0)
    kern = functools.partial(_inprep_kernel, shift=shift, rc=rc, rows=rows)
    outs = pl.pallas_call(
        kern,
        grid=(nb, nt),
        in_specs=[pl.BlockSpec((rows, D_MODEL), row),
                  pl.BlockSpec((1, D_MODEL), const),
                  pl.BlockSpec((D_MODEL, W1_COLS), const),
                  pl.BlockSpec((D_MODEL, W2_COLS), const),
                  pl.BlockSpec((D_MODEL, LANES), const),
                  pl.BlockSpec((1, rc, QKV_DIM), lambda b, i: (b, 0, 0)),
                  pl.BlockSpec((CONV_W, QKV_DIM), const),
                  pl.BlockSpec((2, LANES), const),
                  pl.BlockSpec((DN_WIDTH, DN_WIDTH), const)],
        out_specs=[pl.BlockSpec((rows, DN_WIDTH), row),
                   pl.BlockSpec((rows, DN_WIDTH), row),
                   pl.BlockSpec((rows, DN_WIDTH), row),
                   pl.BlockSpec((rows, LANES), row),
                   pl.BlockSpec((1, keep, QKV_DIM), lambda b, i: (b, 0, 0)),
                   pl.BlockSpec((rows, DN_WIDTH), row),
                   pl.BlockSpec((rows, S5_WIDTH), lambda b, i: (i, b)),
                   pl.BlockSpec((rows, D_MODEL), row),
                   pl.BlockSpec((rows, D_MODEL), row)],
        out_shape=[jax.ShapeDtypeStruct((n, DN_WIDTH), F32),
                   jax.ShapeDtypeStruct((n, DN_WIDTH), F32),
                   jax.ShapeDtypeStruct((n, DN_WIDTH), F32),
                   jax.ShapeDtypeStruct((n, LANES), F32),
                   jax.ShapeDtypeStruct((nb, keep, QKV_DIM), F32),
                   jax.ShapeDtypeStruct((n, DN_WIDTH), BF16),
                   jax.ShapeDtypeStruct((r, nb * S5_WIDTH), BF16),
                   jax.ShapeDtypeStruct((n, D_MODEL), BF16),
                   jax.ShapeDtypeStruct((n, D_MODEL), BF16)],
        scratch_shapes=[pltpu.VMEM((rc + rows, QKV_DIM), F32)],
        compiler_params=_cparams(("arbitrary", "arbitrary")),
        name="inprep",
    )(x2d, nw, *w_parts, cinit, conv_w, gate_p, seg)
    return outs


def _delta_home(low, h, x, other=0.0):
    return jnp.where(low, x, other) if h % 2 == 0 else jnp.where(low, other, x)


def _delta_prepare(q_ref, k_ref, v_ref, gate_ref, tril_ref, bufs, *, nsub):
    sol_buf, wq_buf, qk_buf, kdec_buf, dl_buf = bufs
    c = DN_CHUNK
    dk = DN_HEAD_DIM

    def home(h, x, other=0.0):
        return _delta_home(low, h, x, other)

    rowi2 = lax.broadcasted_iota(I32, (c, 2 * c), 0)
    lane2 = lax.broadcasted_iota(I32, (c, 2 * c), 1)
    coli2 = lane2 & (c - 1)
    causal2 = rowi2 >= coli2
    strict2 = rowi2 > coli2
    low = lane2 < dk
    tril = tril_ref[...]
    pairs = [(j, h) for j in range(nsub) for h in range(DN_HEADS)]
    units = [(j, pr) for j in range(nsub) for pr in range(DN_HEADS // 2)]
    rows = [slice(j * c, (j + 1) * c) for j in range(nsub)]
    gate = [gate_ref[rows[j], :] for j in range(nsub)]
    gc_all = [_split3_dot_left(tril, gate[j]) for j in range(nsub)]
    gc_t = [gc_all[j].T for j in range(nsub)]

    def block(ref, j, pr):
        return ref[rows[j], pr * LANES:(pr + 1) * LANES]

    gfull = {(j, h): jnp.broadcast_to(gc_all[j][:, h:h + 1], (c, 2 * c)) for j, h in pairs}
    g2 = {(j, pr): jnp.where(low, gfull[j, 2 * pr], gfull[j, 2 * pr + 1]) for j, pr in units}
    b2 = {(j, pr): jnp.where(low, gate[j][:, DN_HEADS + 2 * pr:DN_HEADS + 2 * pr + 1],
                             gate[j][:, DN_HEADS + 2 * pr + 1:DN_HEADS + 2 * pr + 2]) for j, pr in units}
    kp = {u: block(k_ref, *u) for u in units}
    qp = {u: block(q_ref, *u) * (dk ** -0.5) for u in units}
    egc2 = {u: jnp.exp(g2[u]) for u in units}
    kb2 = {u: kp[u] * b2[u] for u in units}
    vb2 = {u: block(v_ref, *u) * b2[u] for u in units}
    kw2s = {u: pltpu.roll(kb2[u] * egc2[u], dk, axis=1) for u in units}
    qd2 = {u: qp[u] * egc2[u] for u in units}
    glast2 = {u: g2[u][c - 1:c, :] for u in units}
    kdec_t2 = {u: (kp[u] * jnp.exp(glast2[u] - g2[u])).T for u in units}
    dlast2 = {u: jnp.exp(glast2[u]) for u in units}
    kk = {u: jnp.concatenate([kp[u], kp[u]], axis=0) for u in units}
    yield

    grow2 = {(j, h): jnp.concatenate([gc_t[j][h:h + 1, :], gc_t[j][h:h + 1, :]], axis=1) for j, h in pairs}
    decay = {p: jnp.where(causal2, jnp.exp(jnp.where(causal2, gfull[p] - grow2[p], 0.0)), 0.0) for p in pairs}
    gram = {(j, h): _mm_nt(jnp.concatenate([home(h, kb2[j, h // 2]), home(h, qp[j, h // 2])], axis=0), kk[j, h // 2])
            for j, h in pairs}
    mat = {p: jnp.where(strict2, gram[p][:c] * decay[p], 0.0).astype(BF16) for p in pairs}
    qk = {p: jnp.where(causal2, gram[p][c:] * decay[p], 0.0) for p in pairs}
    sol = {(j, h): home(h, vb2[j, h // 2], kw2s[j, h // 2]) for j, h in pairs}
    yield
    levels = int(math.log2(c))
    zeros2 = jnp.zeros((c, 2 * c), BF16)
    for lvl in range(levels):
        hi = {p: sol[p].astype(BF16) for p in pairs}
        lo = {p: (sol[p] - hi[p].astype(F32)).astype(BF16) for p in pairs}
        if lvl < levels - 1:
            y = {p: jnp.dot(mat[p], jnp.concatenate([jnp.concatenate([hi[p], mat[p]], axis=1),
                                                     jnp.concatenate([lo[p], zeros2], axis=1)], axis=0),
                            preferred_element_type=F32) for p in pairs}
            mat = {p: y[p][:, 2 * dk:].astype(BF16) for p in pairs}
            upd = {p: y[p][:, :2 * dk] for p in pairs}
        else:
            upd = {p: jnp.dot(mat[p], jnp.concatenate([hi[p], lo[p]], axis=0), preferred_element_type=F32)
                   for p in pairs}
        sol = {p: (sol[p] - upd[p]) if lvl == 0 else (sol[p] + upd[p]) for p in pairs}
        yield
    for j, h in pairs:
        n = j * DN_HEADS + h
        sol_buf[n] = sol[j, h]
        wq_buf[n] = jnp.concatenate([home(h, 0.0, sol[j, h]), home(h, qd2[j, h // 2])], axis=0).astype(BF16)
        qk_buf[n] = qk[j, h].astype(BF16)
    for j, pr in units:
        n = j * (DN_HEADS // 2) + pr
        kdec_buf[n] = kdec_t2[j, pr].astype(BF16)
        dl_buf[n] = jnp.broadcast_to(dlast2[j, pr], (SUBLANES, LANES))


def _delta_apply(bufs, o_ref, s_ref, *, nsub):
    sol_buf, wq_buf, qk_buf, kdec_buf, dl_buf = bufs
    c = DN_CHUNK
    dk = DN_HEAD_DIM
    heads = range(DN_HEADS)
    low = lax.broadcasted_iota(I32, (c, 2 * c), 1) < dk
    s = [s_ref[h] for h in heads]
    for j in range(nsub):
        ws, v_new, o_h = [], [], []
        for h in heads:
            n = j * DN_HEADS + h
            ws.append(jnp.dot(wq_buf[n], jnp.concatenate([s[h], s[h]], axis=0).astype(BF16),
                              preferred_element_type=F32))
        yield
        for h in heads:
            v_new.append(sol_buf[j * DN_HEADS + h] - ws[h][:c])
        for h in heads:
            o_h.append(ws[h][c:] + jnp.dot(qk_buf[j * DN_HEADS + h][:, :c], v_new[h].astype(BF16),
                                           preferred_element_type=F32))
        for pr in range(DN_HEADS // 2):
            o_ref[j * c:(j + 1) * c, pr * LANES:(pr + 1) * LANES] = jnp.where(low, o_h[2 * pr], o_h[2 * pr + 1])
        nxt = []
        for h in heads:
            u = j * (DN_HEADS // 2) + h // 2
            kdt = kdec_buf[u][(h % 2) * dk:(h % 2 + 1) * dk, :]
            d = dl_buf[u][0:1, :]
            nxt.append(_delta_home(low, h, s[h] * d + jnp.dot(kdt, v_new[h].astype(BF16),
                                                               preferred_element_type=F32)))
        s = nxt
        yield
    for h in heads:
        s_ref[h] = s[h]


def _delta_chunk_kernel(q_ref, k_ref, v_ref, gate_ref, tril_ref, o_ref, sfin_ref, s_ref, *bufs, nsub, nc):
    i = pl.program_id(0)
    half = len(bufs) // 2
    sets = (bufs[:half], bufs[half:])
    local = lax.rem(jnp.maximum(i - 1, 0), nc)

    @pl.when(i == 0)
    def _():
        for b in sets[1]:
            b[...] = jnp.zeros_like(b)

    @pl.when(local == 0)
    def _():
        s_ref[...] = jnp.zeros_like(s_ref)

    for par in range(2):
        @pl.when(lax.rem(i, 2) == par)
        def _(par=par):
            parts = [_delta_prepare(q_ref, k_ref, v_ref, gate_ref, tril_ref, sets[par], nsub=nsub),
                     _delta_apply(sets[1 - par], o_ref, s_ref, nsub=nsub)]
            while parts:
                parts = [g for g in parts if next(g, StopIteration) is not StopIteration]

    @pl.when(jnp.logical_and(i >= 1, local == nc - 1))
    def _():
        dk = DN_HEAD_DIM
        for h in range(DN_HEADS):
            sfin_ref[0, h] = s_ref[h][:, (h % 2) * dk:(h % 2 + 1) * dk]


def _split3_dot_left(b01, a):
    a1 = a.astype(BF16)
    r1 = a - a1.astype(F32)
    a2 = r1.astype(BF16)
    a3 = (r1 - a2.astype(F32)).astype(BF16)
    out = jnp.dot(b01, a3, preferred_element_type=F32)
    out = out + jnp.dot(b01, a2, preferred_element_type=F32)
    return out + jnp.dot(b01, a1, preferred_element_type=F32)


def _delta_prompt(q, k, v, gate, nb):
    n = q.shape[0]
    t = n // nb
    c = DN_CHUNK
    nsub = DELTA_SUBCHUNKS
    rows = nsub * c
    nc = t // rows
    nblk = nb * nc
    row_in = lambda i: (jnp.minimum(i, nblk - 1), 0)
    row_out = lambda i: (jnp.maximum(i - 1, 0), 0)
    tril = jnp.tril(jnp.ones((c, c), F32)).astype(BF16)
    nh = nsub * DN_HEADS
    npair = nsub * DN_HEADS // 2
    buf_set = [pltpu.VMEM((nh, c, 2 * DN_HEAD_DIM), F32),
               pltpu.VMEM((nh, 2 * c, 2 * DN_HEAD_DIM), BF16),
               pltpu.VMEM((nh, c, 2 * c), BF16),
               pltpu.VMEM((npair, 2 * DN_HEAD_DIM, c), BF16),
               pltpu.VMEM((npair, SUBLANES, LANES), F32)]
    return pl.pallas_call(
        functools.partial(_delta_chunk_kernel, nsub=nsub, nc=nc),
        grid=(nblk + 1,),
        in_specs=[pl.BlockSpec((rows, DN_WIDTH), row_in),
                  pl.BlockSpec((rows, DN_WIDTH), row_in),
                  pl.BlockSpec((rows, DN_WIDTH), row_in),
                  pl.BlockSpec((rows, LANES), row_in),
                  pl.BlockSpec((c, c), lambda i: (0, 0))],
        out_specs=[pl.BlockSpec((rows, DN_WIDTH), row_out),
                   pl.BlockSpec((1, DN_HEADS, DN_HEAD_DIM, DN_HEAD_DIM),
                                lambda i: (jnp.maximum(i - 1, 0) // nc, 0, 0, 0))],
        out_shape=[jax.ShapeDtypeStruct((n, DN_WIDTH), F32),
                   jax.ShapeDtypeStruct((nb, DN_HEADS, DN_HEAD_DIM, DN_HEAD_DIM), F32)],
        scratch_shapes=[pltpu.VMEM((DN_HEADS, DN_HEAD_DIM, 2 * DN_HEAD_DIM), F32)] + buf_set + buf_set,
        compiler_params=_cparams(("arbitrary",)),
        name="delta_prompt",
    )(q, k, v, gate, tril)


def _delta_step_kernel(q_ref, k_ref, v_ref, gate_ref, s0_ref, o_ref, s_ref, kt_ref, qt_ref, gt_ref, *, nt, nb):
    dk = DN_HEAD_DIM
    p = pl.program_id(0)
    for t in range(nt):
        rs = slice(t * nb, (t + 1) * nb)
        gt_ref[...] = gate_ref[rs, :].T
        kt_ref[...] = k_ref[rs, :].T
        qt_ref[...] = (q_ref[rs, :] * (dk ** -0.5)).T
        vt = v_ref[rs, :].T
        src = s0_ref if t == 0 else s_ref
        o_heads = []
        for j in range(2):
            a = jnp.exp(gt_ref[pl.ds(2 * p + j, 1), :])
            beta = gt_ref[pl.ds(2 * p + j + DN_HEADS, 1), :]
            base = j * dk * dk

            def k_dot_s(d, acc, j=j, base=base, src=src):
                sd = src[pl.ds(pl.multiple_of(base + d * dk, dk), dk), :]
                return acc + kt_ref[pl.ds(j * dk + d, 1), :] * sd

            ks = lax.fori_loop(0, dk, k_dot_s, jnp.zeros((dk, nb), F32), unroll=4)
            delta = beta * (vt[j * dk:(j + 1) * dk, :] - a * ks)

            def update(d, acc, j=j, base=base, src=src, a=a, delta=delta):
                r0 = pl.multiple_of(base + d * dk, dk)
                sn = a * src[pl.ds(r0, dk), :] + kt_ref[pl.ds(j * dk + d, 1), :] * delta
                s_ref[pl.ds(r0, dk), :] = sn
                return acc + qt_ref[pl.ds(j * dk + d, 1), :] * sn

            o_heads.append(lax.fori_loop(0, dk, update, jnp.zeros((dk, nb), F32), unroll=4))
        o_ref[rs, :] = jnp.concatenate(o_heads, axis=0).T


def _delta_sample(q, k, v, gate, s0t, nb, nt):
    dk = DN_HEAD_DIM
    flat = dk * dk
    n = nt * nb
    kern = functools.partial(_delta_step_kernel, nt=nt, nb=nb)
    pair = lambda p: (0, p)
    return pl.pallas_call(
        kern,
        grid=(DN_HEADS // 2,),
        in_specs=[pl.BlockSpec((n, LANES), pair),
                  pl.BlockSpec((n, LANES), pair),
                  pl.BlockSpec((n, LANES), pair),
                  pl.BlockSpec((n, LANES), lambda p: (0, 0)),
                  pl.BlockSpec((2 * flat, nb), lambda p: (p, 0))],
        out_specs=[pl.BlockSpec((n, LANES), pair),
                   pl.BlockSpec((2 * flat, nb), lambda p: (p, 0))],
        out_shape=[jax.ShapeDtypeStruct((n, DN_WIDTH), F32),
                   jax.ShapeDtypeStruct((DN_HEADS * flat, nb), F32)],
        scratch_shapes=[pltpu.VMEM((LANES, nb), F32),
                        pltpu.VMEM((LANES, nb), F32),
                        pltpu.VMEM((LANES, nb), F32)],
        compiler_params=_cparams(("arbitrary",)),
        name="delta_sample",
    )(q, k, v, gate, s0t)


def _s5_kernel(u_ref, btre_ref, btim_ref, lam_ref, ctre_ref, ctim_ref, d_ref, h0_ref, y_ref, hfin_ref,
               bw_ref, c_ref, ab_ref, x_ref, h_ref, ru_ref, ry_ref, *, nb, tt, wide):
    p2 = S5_FLAT

    @pl.when(pl.program_id(0) == 0)
    def _():
        lr = lam_ref[0:1, :]
        li = lam_ref[1:2, :]
        dt = jnp.exp(lam_ref[2:3, :])
        mag = jnp.exp(lr * dt)
        ab_re = mag * jnp.cos(li * dt)
        ab_im = mag * jnp.sin(li * dt)
        den = lr * lr + li * li
        nr = ab_re - 1.0
        ni = ab_im
        f_re = (nr * lr + ni * li) / den
        f_im = (ni * lr - nr * li) / den
        ab_ref[0:1, :] = ab_re
        ab_ref[1:2, :] = ab_im
        gpl = LANES // S5_STATE
        ch_g = lax.broadcasted_iota(I32, (S5_WIDTH, LANES), 0) // S5_GROUP_CH
        lane_g = lax.broadcasted_iota(I32, (S5_WIDTH, LANES), 1) // S5_STATE
        bre2 = jnp.concatenate([btre_ref[...]] * gpl, axis=1)
        bim2 = jnp.concatenate([btim_ref[...]] * gpl, axis=1)
        for j in range(p2 // LANES):
            cols = slice(j * LANES, (j + 1) * LANES)
            own = ch_g == gpl * j + lane_g
            bre = jnp.where(own, bre2, 0.0)
            bim = jnp.where(own, bim2, 0.0)
            bw_ref[:, cols] = (bre * f_re[:, cols] - bim * f_im[:, cols]).astype(BF16)
            bw_ref[:, p2 + j * LANES:p2 + (j + 1) * LANES] = (bim * f_re[:, cols] + bre * f_im[:, cols]).astype(BF16)
        cpl = LANES // S5_GROUP_CH
        st_g = lax.broadcasted_iota(I32, (p2, LANES), 0) // S5_STATE
        lane_cg = lax.broadcasted_iota(I32, (p2, LANES), 1) // S5_GROUP_CH
        for j in range(S5_WIDTH // LANES):
            cols = slice(j * LANES, (j + 1) * LANES)
            own = st_g == cpl * j + lane_cg
            c_ref[0:p2, cols] = jnp.where(own, ctre_ref[...], 0.0).astype(BF16)
            c_ref[p2:2 * p2, cols] = jnp.where(own, -ctim_ref[...], 0.0).astype(BF16)
        h_ref[...] = h0_ref[...]

    nck = S5_WIDTH // LANES
    if wide:
        for b in range(nb):
            for ck in range(nck):
                lo = b * S5_WIDTH + ck * LANES
                ru_ref[ck, pl.ds(b, tt, stride=nb), :] = u_ref[:, lo:lo + LANES].astype(F32)
        u = jnp.concatenate([ru_ref[ck] for ck in range(nck)], axis=1)
    else:
        u = u_ref[...].astype(F32)
    ub = u.astype(BF16)
    cw = S5_WIDTH // S5_SUPER
    sw = S5_FLAT // S5_SUPER
    for part in (0, p2):
        for b in range(S5_SUPER):
            x_ref[:, part + b * sw:part + (b + 1) * sw] = jnp.dot(
                ub[:, b * cw:(b + 1) * cw], bw_ref[b * cw:(b + 1) * cw, part + b * sw:part + (b + 1) * sw],
                preferred_element_type=F32)
    a_re = ab_ref[0:1, :]
    a_im = ab_ref[1:2, :]

    if nb == SUBLANES:
        wsl = p2 // S5_SCAN_SPLIT
        for sp in range(S5_SCAN_SPLIT):
            c0 = sp * wsl
            are = jnp.broadcast_to(a_re[:, c0:c0 + wsl], (nb, wsl))
            aim = jnp.broadcast_to(a_im[:, c0:c0 + wsl], (nb, wsl))

            def step(t, carry, c0=c0, are=are, aim=aim):
                hr, hi = carry
                r0 = pl.multiple_of(t * nb, nb)
                nr = are * hr - aim * hi + x_ref[pl.ds(r0, nb), c0:c0 + wsl]
                ni = are * hi + aim * hr + x_ref[pl.ds(r0, nb), p2 + c0:p2 + c0 + wsl]
                x_ref[pl.ds(r0, nb), c0:c0 + wsl] = nr
                x_ref[pl.ds(r0, nb), p2 + c0:p2 + c0 + wsl] = ni
                return nr, ni

            hr, hi = lax.fori_loop(0, tt, step, (h_ref[:, c0:c0 + wsl], h_ref[:, p2 + c0:p2 + c0 + wsl]),
                                   unroll=2)
            h_ref[:, c0:c0 + wsl] = hr
            h_ref[:, p2 + c0:p2 + c0 + wsl] = hi
    else:
        for t in range(tt):
            rs = slice(t * nb, (t + 1) * nb)
            hr = h_ref[:, 0:p2]
            hi = h_ref[:, p2:2 * p2]
            nr = a_re * hr - a_im * hi + x_ref[rs, 0:p2]
            ni = a_re * hi + a_im * hr + x_ref[rs, p2:2 * p2]
            h_ref[:, 0:p2] = nr
            h_ref[:, p2:2 * p2] = ni
            x_ref[rs, 0:p2] = nr
            x_ref[rs, p2:2 * p2] = ni

    for b in range(S5_SUPER):
        cols = slice(b * cw, (b + 1) * cw)
        y = None
        for part in (0, p2):
            rws = slice(part + b * sw, part + (b + 1) * sw)
            term = jnp.dot(x_ref[:, rws].astype(BF16), c_ref[rws, cols], preferred_element_type=F32)
            y = term if y is None else y + term
        y = y + d_ref[:, cols] * u[:, cols]
        if wide:
            for ck in range(cw // LANES):
                ry_ref[b * (cw // LANES) + ck] = y[:, ck * LANES:(ck + 1) * LANES]
        else:
            y_ref[:, cols] = y
    if wide:
        for b in range(nb):
            for ck in range(nck):
                lo = b * S5_WIDTH + ck * LANES
                y_ref[:, lo:lo + LANES] = ry_ref[ck, pl.ds(b, tt, stride=nb), :]
    hfin_ref[...] = h_ref[...]


def _s5(u, params, h0, nb, wide):
    btre, btim, lam, ctre, ctim, dvec = params
    t = u.shape[0] if wide else u.shape[0] // nb
    tt = min(ROW_TILE // nb, t)
    rows = tt * nb
    const = lambda i: (0, 0)
    kern = functools.partial(_s5_kernel, nb=nb, tt=tt, wide=wide)
    io_block = (tt, nb * S5_WIDTH) if wide else (rows, S5_WIDTH)
    return pl.pallas_call(
        kern,
        grid=(t // tt,),
        in_specs=[pl.BlockSpec(io_block, lambda i: (i, 0)),
                  pl.BlockSpec((S5_WIDTH, S5_STATE), const),
                  pl.BlockSpec((S5_WIDTH, S5_STATE), const),
                  pl.BlockSpec((SUBLANES, S5_FLAT), const),
                  pl.BlockSpec((S5_FLAT, LANES), const),
                  pl.BlockSpec((S5_FLAT, LANES), const),
                  pl.BlockSpec((1, S5_WIDTH), const),
                  pl.BlockSpec((nb, 2 * S5_FLAT), const)],
        out_specs=[pl.BlockSpec(io_block, lambda i: (i, 0)),
                   pl.BlockSpec((nb, 2 * S5_FLAT), const)],
        out_shape=[jax.ShapeDtypeStruct(u.shape, F32),
                   jax.ShapeDtypeStruct((nb, 2 * S5_FLAT), F32)],
        scratch_shapes=[pltpu.VMEM((S5_WIDTH, 2 * S5_FLAT), BF16),
                        pltpu.VMEM((2 * S5_FLAT, S5_WIDTH), BF16),
                        pltpu.VMEM((SUBLANES, S5_FLAT), F32),
                        pltpu.VMEM((rows, 2 * S5_FLAT), F32),
                        pltpu.VMEM((nb, 2 * S5_FLAT), F32),
                        pltpu.VMEM((S5_WIDTH // LANES, rows, LANES), F32),
                        pltpu.VMEM((S5_WIDTH // LANES, rows, LANES), F32)],
        compiler_params=_cparams(("arbitrary",)),
        name="s5",
    )(u, btre, btim, lam, ctre, ctim, dvec, h0)


def _postmix_kernel(xp_ref, op_ref, zp_ref, ysp_ref, gap_ref, gbp_ref,
                    xs_ref, os_ref, zs_ref, yss_ref, gas_ref, gbs_ref, *rest, nblk_p, range_tok):
    carry_ref = rest[-1]

    @pl.when(pl.program_id(0) == 0)
    def _():
        carry_ref[...] = jnp.zeros_like(carry_ref)

    @pl.when(pl.program_id(0) < nblk_p)
    def _():
        _postmix_body(xp_ref, op_ref, zp_ref, ysp_ref, gap_ref, gbp_ref, *rest, range_tok=range_tok)

    @pl.when(pl.program_id(0) >= nblk_p)
    def _():
        _postmix_body(xs_ref, os_ref, zs_ref, yss_ref, gas_ref, gbs_ref, *rest, range_tok=range_tok)


def _postmix_body(x_ref, o_ref, z_ref, ys_ref, ga_ref, gb_ref, hw_ref, seg_ref, wa_ref, wglu_ref, wb_ref,
                  wo_ref, nf_ref, wr_ref, su_ref, x1_ref, hn_ref, bkt_ref, rank_ref, rw_ref, cnt_ref, carry_ref,
                  *, range_tok):
    rows = x_ref.shape[0]
    pr = rows // POSTMIX_PARTS
    parts = [_postmix_part(p, pr, x_ref, o_ref, z_ref, ys_ref, ga_ref, gb_ref, hw_ref, seg_ref, wa_ref, wglu_ref,
                           wb_ref, wo_ref, nf_ref, wr_ref, su_ref, x1_ref, hn_ref, bkt_ref, rank_ref, rw_ref,
                           carry_ref, range_tok) for p in range(POSTMIX_PARTS)]
    live = []
    while live or parts:
        if parts:
            live.append(parts.pop(0))
        live = [g for g in live if next(g, StopIteration) is not StopIteration]
    cnt_ref[...] = carry_ref[...]


def _postmix_part(part, pr, x_ref, o_ref, z_ref, ys_ref, ga_ref, gb_ref, hw_ref, seg_ref, wa_ref, wglu_ref, wb_ref,
                  wo_ref, nf_ref, wr_ref, su_ref, x1_ref, hn_ref, bkt_ref, rank_ref, rw_ref, carry_ref, range_tok):
    rs = slice(part * pr, (part + 1) * pr)
    o = o_ref[rs, :]
    ms = jnp.dot((o * o).astype(BF16), seg_ref[...], preferred_element_type=F32) * (1.0 / DN_HEAD_DIM)
    on = o * lax.rsqrt(ms + RMS_EPS) * hw_ref[...]
    z = z_ref[rs, :]
    oa = on * (z * jax.nn.sigmoid(z)).astype(F32)
    yield
    y_a = _mm(oa, wa_ref[...])
    ys = jax.nn.gelu(ys_ref[rs, :])
    yield
    ys = ys * jax.nn.sigmoid(_mm(ys, wglu_ref[...]))
    yield
    y_b = _mm(ys, wb_ref[...])
    mixed = jax.nn.sigmoid(ga_ref[rs, :]).astype(F32) * y_a + jax.nn.sigmoid(gb_ref[rs, :]).astype(F32) * y_b
    yield
    x1 = x_ref[rs, :] + _mm(mixed, wo_ref[...])
    x1_ref[rs, :] = x1
    hn = x1 * lax.rsqrt(jnp.mean(x1 * x1, axis=-1, keepdims=True) + RMS_EPS) * nf_ref[...]
    _slab_store(hn_ref, hn, part * pr)
    yield

    wr = wr_ref[...]
    w_hi = wr.astype(BF16)
    w_lo = (wr - w_hi.astype(F32)).astype(BF16)
    hn_hi = hn.astype(BF16)
    hn_lo = (hn - hn_hi.astype(F32)).astype(BF16)
    both = _mm_nt(jnp.concatenate([w_hi, w_lo], axis=0), hn_hi)
    logits = both[:ROUTER_ROWS] + both[ROUTER_ROWS:] + _mm_nt(w_hi, hn_lo)
    yield
    coarse = logits[N_EXPERTS:N_EXPERTS + MOE_GROUPS, :]
    cm = jnp.max(coarse, axis=0, keepdims=True)
    ce = jnp.exp(coarse - cm)
    pc = ce / jnp.sum(ce, axis=0, keepdims=True)
    p_sel = jnp.max(pc, axis=0, keepdims=True)
    gi = lax.broadcasted_iota(I32, pc.shape, 0)
    g_sel = jnp.min(jnp.where(pc == p_sel, gi, MOE_GROUPS), axis=0, keepdims=True)
    fine = jnp.zeros((EXPERTS_PER_GROUP, logits.shape[1]), F32)
    for g in range(MOE_GROUPS):
        fine = fine + jnp.where(g_sel == g, logits[g * EXPERTS_PER_GROUP:(g + 1) * EXPERTS_PER_GROUP, :], 0.0)
    fm = jnp.max(fine, axis=0, keepdims=True)
    fe = jnp.exp(fine - fm)
    pf = fe / jnp.sum(fe, axis=0, keepdims=True)
    ei = lax.broadcasted_iota(I32, pf.shape, 0)
    v1 = jnp.max(pf, axis=0, keepdims=True)
    i1 = jnp.min(jnp.where(pf == v1, ei, EXPERTS_PER_GROUP), axis=0, keepdims=True)
    rest = jnp.where(ei == i1, -1.0, pf)
    v2 = jnp.max(rest, axis=0, keepdims=True)
    i2 = jnp.min(jnp.where(rest == v2, ei, EXPERTS_PER_GROUP), axis=0, keepdims=True)
    tot = v1 + v2
    rw_ref[0:1, rs] = v1 / tot * p_sel
    rw_ref[1:2, rs] = v2 / tot * p_sel

    tok = pl.program_id(0) * (pr * POSTMIX_PARTS) + part * pr + lax.broadcasted_iota(I32, (1, pr), 1)
    ph = jnp.zeros((1, pr), I32)
    for r in range(1, MOE_PHASES):
        ph = ph + (tok >= r * range_tok).astype(I32)
    bsel = [ph * N_EXPERTS + g_sel * EXPERTS_PER_GROUP + ix for ix in (i1, i2)]
    bi = lax.broadcasted_iota(I32, (MOE_PHASES * N_EXPERTS, pr), 0)
    onehot = [(bi == b).astype(F32) for b in bsel]
    cnt = onehot[0] + onehot[1]
    before = carry_ref[:, 0:1] + jnp.dot(cnt.astype(BF16), su_ref[0:pr, 0:pr], preferred_element_type=F32)
    for s in range(TOP_K):
        bkt_ref[s:s + 1, rs] = bsel[s]
        rank_ref[s:s + 1, rs] = jnp.sum(onehot[s] * before, axis=0, keepdims=True).astype(I32)
    carry_ref[...] = carry_ref[...] + jnp.sum(cnt, axis=1, keepdims=True)


def _postmix(prompt, sample, weights, nb):
    n_p = prompt[0].shape[0]
    n_s = sample[0].shape[0]
    t = n_p // nb
    tt = min(ROW_TILE, t, n_s)
    nt = t // tt
    nblk_p = n_p // tt
    nblk = nblk_p + n_s // tt
    n_total = n_p + n_s
    prow = lambda i: (jnp.minimum(i, nblk_p - 1), 0)
    pys = lambda i: (jnp.minimum(i, nblk_p - 1) % nt, jnp.minimum(i, nblk_p - 1) // nt)
    srow = lambda i: (jnp.maximum(i - nblk_p, 0), 0)
    const = lambda i: (0, 0)

    def stream_specs(row, ysmap):
        return [pl.BlockSpec((tt, D_MODEL), row),
                pl.BlockSpec((tt, DN_WIDTH), row),
                pl.BlockSpec((tt, DN_WIDTH), row),
                pl.BlockSpec((tt, S5_WIDTH), ysmap),
                pl.BlockSpec((tt, D_MODEL), row),
                pl.BlockSpec((tt, D_MODEL), row)]

    weight_specs = [pl.BlockSpec((1, DN_WIDTH), const),
                    pl.BlockSpec((DN_WIDTH, DN_WIDTH), const),
                    pl.BlockSpec((DN_WIDTH, D_MODEL), const),
                    pl.BlockSpec((S5_WIDTH, S5_WIDTH), const),
                    pl.BlockSpec((S5_WIDTH, D_MODEL), const),
                    pl.BlockSpec((D_MODEL, D_MODEL), const),
                    pl.BlockSpec((1, D_MODEL), const),
                    pl.BlockSpec((ROUTER_ROWS, D_MODEL), const),
                    pl.BlockSpec((tt, tt), const)]
    xp, op, zp, ysp, gap, gbp = prompt
    nbk = MOE_PHASES * N_EXPERTS
    earlier = jnp.triu(jnp.ones((tt, tt), F32), k=1).astype(BF16)
    return pl.pallas_call(
        functools.partial(_postmix_kernel, nblk_p=nblk_p, range_tok=n_total // MOE_PHASES),
        grid=(nblk,),
        in_specs=stream_specs(prow, pys) + stream_specs(srow, srow) + weight_specs,
        out_specs=[pl.BlockSpec((tt, D_MODEL), lambda i: (i, 0)),
                   pl.BlockSpec((tt * ROW_SLAB, LANES), lambda i: (i, 0)),
                   pl.BlockSpec((TOP_K, tt), lambda i: (0, i)),
                   pl.BlockSpec((TOP_K, tt), lambda i: (0, i)),
                   pl.BlockSpec((TOP_K, tt), lambda i: (0, i)),
                   pl.BlockSpec((nbk, LANES), const)],
        out_shape=[jax.ShapeDtypeStruct((n_total, D_MODEL), F32),
                   jax.ShapeDtypeStruct((n_total * ROW_SLAB, LANES), F32),
                   jax.ShapeDtypeStruct((TOP_K, n_total), I32),
                   jax.ShapeDtypeStruct((TOP_K, n_total), I32),
                   jax.ShapeDtypeStruct((TOP_K, n_total), F32),
                   jax.ShapeDtypeStruct((nbk, LANES), F32)],
        scratch_shapes=[pltpu.VMEM((nbk, LANES), F32)],
        compiler_params=_cparams(("arbitrary",)),
        name="postmix",
    )(xp, op, zp, ysp, gap, gbp, *sample, *weights, earlier)


def _wait_slabs(buf, sem):
    pltpu.make_async_copy(buf, buf, sem).wait()


def _moe_kernel(texp_ref, tph_ref, tsrc_ref, tnv_ref, tfirst_ref, tslot_ref, tnext_ref, otok_ref,
                hn_hbm, wu_hbm, wd_hbm, y_ref, hnv, xbuf, wu_buf, wd_buf, wub, wdb, sem, wsem):
    i = pl.program_id(0)
    tm = MOE_TILE
    rs = ROW_SLAB
    nv = tnv_ref[i]
    ph = tph_ref[i]
    range_rows = hnv.shape[0]

    def weight_copies(e, sl):
        return (pltpu.make_async_copy(wu_hbm.at[e], wu_buf.at[sl], wsem.at[sl]),
                pltpu.make_async_copy(wd_hbm.at[e], wd_buf.at[sl], wsem.at[sl]))

    @pl.when(i == 0)
    def _():
        for c in weight_copies(texp_ref[0], 0):
            c.start()

    @pl.when(jnp.logical_and(nv > 0, jnp.logical_or(i == 0, ph != tph_ref[jnp.maximum(i - 1, 0)])))
    def _():
        start = pl.multiple_of(ph * range_rows, rs)
        whole = pltpu.make_async_copy(hn_hbm.at[pl.ds(start, range_rows), :], hnv, sem)
        whole.start()
        whole.wait()

    for sl in range(2):
        @pl.when(jnp.logical_and(jnp.logical_and(nv > 0, tfirst_ref[i] == 1), tslot_ref[i] == sl))
        def _():
            for c in weight_copies(texp_ref[i], sl):
                c.wait()

            @pl.when(tnext_ref[i] >= 0)
            def _():
                for c in weight_copies(tnext_ref[i], 1 - sl):
                    c.start()

            wub[...] = wu_buf[sl].astype(BF16)
            wdb[...] = wd_buf[sl].astype(BF16)

    @pl.when(nv == 0)
    def _():
        y_ref[...] = jnp.zeros_like(y_ref)

    @pl.when(nv > 0)
    def _():
        src0 = tsrc_ref[i]
        for r in range(tm):
            tok8 = pl.multiple_of(otok_ref[src0 + r], rs)
            xbuf[pl.ds(r * rs, rs), :] = hnv[pl.ds(tok8, rs), :]
        x = _slab_load(xbuf, tm).astype(BF16)
        hu = jnp.dot(x, wub[...], preferred_element_type=F32)
        gate = hu[:, :EXPERT_FF]
        up = hu[:, EXPERT_FF:]
        act = gate * jax.nn.sigmoid(gate) * up
        _slab_store(y_ref, jnp.dot(act.astype(BF16), wdb[...], preferred_element_type=F32))


def _moe(hn, w_up, w_down, plan):
    ntiles = plan[0].shape[0]
    grid_spec = pltpu.PrefetchScalarGridSpec(
        num_scalar_prefetch=len(plan),
        grid=(ntiles,),
        in_specs=[pl.BlockSpec(memory_space=pl.ANY),
                  pl.BlockSpec(memory_space=pl.ANY),
                  pl.BlockSpec(memory_space=pl.ANY)],
        out_specs=pl.BlockSpec((MOE_TILE * ROW_SLAB, LANES), lambda i, *_: (i, 0)),
        scratch_shapes=[pltpu.VMEM((hn.shape[0] // MOE_PHASES, LANES), F32),
                        pltpu.VMEM((MOE_TILE * ROW_SLAB, LANES), F32),
                        pltpu.VMEM((2, D_MODEL, 2 * EXPERT_FF), F32),
                        pltpu.VMEM((2, EXPERT_FF, D_MODEL), F32),
                        pltpu.VMEM((D_MODEL, 2 * EXPERT_FF), BF16),
                        pltpu.VMEM((EXPERT_FF, D_MODEL), BF16),
                        pltpu.SemaphoreType.DMA,
                        pltpu.SemaphoreType.DMA((2,))])
    return pl.pallas_call(
        _moe_kernel,
        grid_spec=grid_spec,
        out_shape=jax.ShapeDtypeStruct((ntiles * MOE_TILE * ROW_SLAB, LANES), F32),
        compiler_params=_cparams(("arbitrary",)),
        name="moe",
    )(*plan, hn, w_up, w_down)


def _combine_kernel(pos_ref, x1_ref, ys_hbm, w_ref, nw_ref, outp_ref, outs_ref,
                    ybuf0, ybuf1, sem, *, nblk_p, n_tok):
    i = pl.program_id(0)
    nsteps = pl.num_programs(0)
    tt = x1_ref.shape[0]
    rs = ROW_SLAB
    slot = lax.rem(i, 2)
    ybuf = (ybuf0, ybuf1)

    def start_gather(step, sl):
        base = step * tt
        for r in range(tt * TOP_K):
            j, s = divmod(r, TOP_K)
            p8 = pl.multiple_of(pos_ref[s * n_tok + base + j], rs)
            pltpu.make_async_copy(ys_hbm.at[pl.ds(p8, rs), :], ybuf[sl].at[pl.ds((s * tt + j) * rs, rs), :],
                                  sem.at[sl]).start(priority=r % DMA_QUEUES)

    @pl.when(i == 0)
    def _():
        start_gather(0, 0)

    for sl in range(2):
        @pl.when(slot == sl)
        def _():
            _wait_slabs(ybuf[sl], sem.at[sl])
            start_gather(jnp.minimum(i + 1, nsteps - 1), 1 - sl)
            w = w_ref[...]
            y0 = _slab_load(ybuf[sl], tt, 0)
            y1 = _slab_load(ybuf[sl], tt, tt * rs)
            x = x1_ref[...] + w[:, 0:1] * y0 + w[:, 1:2] * y1
            res = x * lax.rsqrt(jnp.mean(x * x, axis=-1, keepdims=True) + RMS_EPS) * nw_ref[...]

            @pl.when(i < nblk_p)
            def _():
                outp_ref[...] = res

            @pl.when(i >= nblk_p)
            def _():
                outs_ref[...] = res

        @pl.when(jnp.logical_and(slot == sl, i == nsteps - 1))
        def _():
            _wait_slabs(ybuf[1 - sl], sem.at[1 - sl])


def _combine(x1, ysorted, pos8, wtok, nw, n_p):
    n = x1.shape[0]
    tt = math.gcd(math.gcd(n_p, n - n_p), COMBINE_TILE)
    nblk_p = n_p // tt
    grid_spec = pltpu.PrefetchScalarGridSpec(
        num_scalar_prefetch=1,
        grid=(n // tt,),
        in_specs=[pl.BlockSpec((tt, D_MODEL), lambda i, *_: (i, 0)),
                  pl.BlockSpec(memory_space=pl.ANY),
                  pl.BlockSpec((tt, TOP_K), lambda i, *_: (i, 0)),
                  pl.BlockSpec((1, D_MODEL), lambda i, *_: (0, 0))],
        out_specs=[pl.BlockSpec((tt, D_MODEL), lambda i, *_: (jnp.minimum(i, nblk_p - 1), 0)),
                   pl.BlockSpec((tt, D_MODEL), lambda i, *_: (jnp.maximum(i - nblk_p, 0), 0))],
        scratch_shapes=[pltpu.VMEM((tt * TOP_K * ROW_SLAB, LANES), F32),
                        pltpu.VMEM((tt * TOP_K * ROW_SLAB, LANES), F32),
                        pltpu.SemaphoreType.DMA((2,))])
    return pl.pallas_call(
        functools.partial(_combine_kernel, nblk_p=nblk_p, n_tok=n),
        grid_spec=grid_spec,
        out_shape=[jax.ShapeDtypeStruct((n_p, D_MODEL), F32),
                   jax.ShapeDtypeStruct((n - n_p, D_MODEL), F32)],
        compiler_params=_cparams(("arbitrary",)),
        name="combine",
    )(pos8, x1, ysorted, wtok, nw)


def _route_plan(bkt, rank, cnt, n_tok):
    tm = MOE_TILE
    n_assign = n_tok * TOP_K
    nbk = MOE_PHASES * N_EXPERTS
    ntiles = n_assign // tm + nbk
    range_tok = n_tok // MOE_PHASES
    b_flat = bkt.T.reshape(n_assign)
    order = jnp.argsort(b_flat, stable=True).astype(I32)
    counts = cnt[:, 0].astype(I32)
    cstart = jnp.cumsum(counts) - counts
    tiles_b = (counts + tm - 1) // tm
    tend = jnp.cumsum(tiles_b)
    tstart = tend - tiles_b
    tile_id = jnp.arange(ntiles, dtype=I32)
    tbk = jnp.minimum(jnp.sum((tile_id[:, None] >= tend[None, :]).astype(I32), axis=1), nbk - 1)
    onehot = (tbk[:, None] == jnp.arange(nbk, dtype=I32)[None, :]).astype(I32)
    pick = lambda v: jnp.sum(onehot * v[None, :], axis=1)
    done = (tile_id - pick(tstart)) * tm
    tnv = jnp.where(tile_id < tend[-1], jnp.clip(pick(counts) - done, 0, tm), 0)
    tsrc = jnp.where(tnv > 0, pick(cstart) + done, 0)
    texp = tbk % N_EXPERTS
    tph = tbk // N_EXPERTS
    nonempty = counts > 0
    bslot = (jnp.cumsum(nonempty.astype(I32)) - 1) % 2
    bidx = jnp.where(nonempty, jnp.arange(nbk, dtype=I32), nbk)
    nxt = jnp.concatenate([lax.cummin(bidx[::-1])[::-1][1:], jnp.full((1,), nbk, I32)])
    bnext = jnp.where(nxt < nbk, nxt % N_EXPERTS, -1)
    tfirst = jnp.logical_and(tnv > 0, done == 0).astype(I32)
    tslot = pick(bslot)
    tnext = pick(bnext)
    otok8 = jnp.concatenate([((order // TOP_K) % range_tok) * ROW_SLAB, jnp.zeros((tm,), I32)])
    plan = tuple(a.astype(I32) for a in (texp, tph, tsrc, tnv, tfirst, tslot, tnext, otok8))
    first = jnp.sum((bkt[:, :, None] == jnp.arange(nbk, dtype=I32)[None, None, :]).astype(I32)
                    * (tstart * tm)[None, None, :], axis=2)
    pos8 = ((first + rank) * ROW_SLAB).reshape(n_assign)
    return plan, pos8.astype(I32)


def _block_diag(m):
    g, a, b = m.shape
    eye = jnp.eye(g, dtype=m.dtype)
    return (eye[:, None, :, None] * m[:, :, None, :]).reshape(g * a, g * b)


def kernel(x_prompt, x_sample, state_conv, state_delta, state_ssm_re, state_ssm_im, norm_mix_w, w_in, conv_w, a_log, dt_bias, head_norm_w, w_a_up, s5_lambda_re, s5_lambda_im, s5_log_step, s5_b_re, s5_b_im, s5_c_re, s5_c_im, s5_d, w_glu, w_b_up, w_o, norm_ffn_w, w_router_coarse, w_router_fine, w_expert_up, w_expert_down, norm_final_w):
    bp, tp, _ = x_prompt.shape
    bs, ts, _ = x_sample.shape
    n_p = bp * tp
    n_s = bs * ts
    n_tok = n_p + n_s
    l = 0

    w = w_in[l].astype(BF16)
    c_ab = W1_COLS + 2 * DN_HEADS
    w_parts = (w[:, :W1_COLS], w[:, c_ab:],
               jnp.concatenate([w[:, W1_COLS:c_ab], jnp.zeros((D_MODEL, LANES - 2 * DN_HEADS), BF16)], axis=1))
    nw_mix = norm_mix_w[l].reshape(1, D_MODEL)
    pad8 = lambda v: jnp.concatenate([v, jnp.zeros((LANES - DN_HEADS,), F32)]).reshape(1, LANES)
    gate_p = jnp.concatenate([pad8(a_log[l]), pad8(dt_bias[l])], axis=0)
    seg = _block_diag(jnp.ones((DN_HEADS, DN_HEAD_DIM, DN_HEAD_DIM), BF16))
    chan_rows = lambda b: jnp.swapaxes(b, 1, 2).reshape(S5_WIDTH, S5_STATE)
    state_rows = lambda c: jnp.tile(jnp.swapaxes(c, 1, 2).reshape(S5_FLAT, S5_GROUP_CH),
                                    (1, LANES // S5_GROUP_CH))
    lam = jnp.concatenate([s5_lambda_re[l].reshape(1, S5_FLAT), s5_lambda_im[l].reshape(1, S5_FLAT),
                           jnp.repeat(s5_log_step[l], S5_STATE).reshape(1, S5_FLAT),
                           jnp.zeros((SUBLANES - 3, S5_FLAT), F32)], axis=0)
    s5_params = (chan_rows(s5_b_re[l]), chan_rows(s5_b_im[l]), lam,
                 state_rows(s5_c_re[l]), state_rows(s5_c_im[l]), s5_d[l].reshape(1, S5_WIDTH))
    hw = jnp.tile(head_norm_w[l], DN_HEADS).reshape(1, DN_WIDTH)
    wr = jnp.concatenate([w_router_fine[l].T, w_router_coarse[l].T,
                          jnp.zeros((ROUTER_ROWS - N_EXPERTS - MOE_GROUPS, D_MODEL), F32)], axis=0)
    pm_weights = (hw, seg, w_a_up[l].astype(BF16), w_glu[l].astype(BF16), w_b_up[l].astype(BF16),
                  w_o[l].astype(BF16), norm_ffn_w[l].reshape(1, D_MODEL), wr)

    xp2 = x_prompt.reshape(n_p, D_MODEL)
    q_p, k_p, v_p, gates_p, conv_p, z_p, u_p, ga_p, gb_p = _inprep(
        xp2, nw_mix, w_parts, jnp.zeros((bp, SUBLANES, QKV_DIM), F32), conv_w[l], gate_p, seg, bp, 1)
    o_p, delta_p = _delta_prompt(q_p, k_p, v_p, gates_p, bp)
    ys_p, h_p = _s5(u_p, s5_params, jnp.zeros((bp, 2 * S5_FLAT), F32), bp, True)

    xs2 = jnp.swapaxes(x_sample, 0, 1).reshape(n_s, D_MODEL)
    cinit_s = jnp.swapaxes(state_conv[l], 0, 1).reshape(1, (CONV_W - 1) * bs, QKV_DIM)
    q_s, k_s, v_s, gate_s, conv_s, z_s, u_s, ga_s, gb_s = _inprep(
        xs2, nw_mix, w_parts, cinit_s, conv_w[l], gate_p, seg, 1, bs)
    s0t = jnp.transpose(state_delta[l], (1, 2, 3, 0)).reshape(DN_HEADS * DN_HEAD_DIM * DN_HEAD_DIM, bs)
    o_s, delta_st = _delta_sample(q_s, k_s, v_s, gate_s, s0t, bs, ts)
    delta_s = jnp.transpose(delta_st.reshape(DN_HEADS, DN_HEAD_DIM, DN_HEAD_DIM, bs), (3, 0, 1, 2))
    h0_s = jnp.concatenate([state_ssm_re[l].reshape(bs, S5_FLAT), state_ssm_im[l].reshape(bs, S5_FLAT)], axis=1)
    ys_s, h_s = _s5(u_s, s5_params, h0_s, bs, False)
    x1, hn, bkt, rank, rw, cnt = _postmix((xp2, o_p, z_p, ys_p, ga_p, gb_p), (xs2, o_s, z_s, ys_s, ga_s, gb_s),
                                          pm_weights, bp)

    plan, pos8 = _route_plan(bkt, rank, cnt, n_tok)
    ysorted = _moe(hn, w_expert_up[l], w_expert_down[l], plan)
    y_p, y_s = _combine(x1, ysorted, pos8, rw.T, norm_final_w.reshape(1, D_MODEL), n_p)

    y_prompt = y_p.reshape(bp, tp, D_MODEL)
    y_sample = jnp.swapaxes(y_s.reshape(ts, bs, D_MODEL), 0, 1)
    conv_sample = jnp.swapaxes(conv_s.reshape(CONV_W - 1, bs, QKV_DIM), 0, 1)
    return (y_prompt, y_sample,
            conv_p[None], delta_p[None],
            h_p[:, :S5_FLAT].reshape(1, bp, S5_GROUPS, S5_STATE), h_p[:, S5_FLAT:].reshape(1, bp, S5_GROUPS, S5_STATE),
            conv_sample[None], delta_s[None],
            h_s[:, :S5_FLAT].reshape(1, bs, S5_GROUPS, S5_STATE), h_s[:, S5_FLAT:].reshape(1, bs, S5_GROUPS, S5_STATE))
```

```python
import functools
import math

import jax
import jax.numpy as jnp
import numpy as np
from jax import lax
from jax.experimental import pallas as pl
from jax.experimental.pallas import tpu as pltpu

F32 = jnp.float32
BF16 = jnp.bfloat16
I32 = jnp.int32

D_MODEL = 1024
DN_HEADS = 8
DN_HEAD_DIM = 64
DN_WIDTH = DN_HEADS * DN_HEAD_DIM
QKV_DIM = 3 * DN_WIDTH
CONV_W = 4
DN_CHUNK = 64
S5_GROUP_CH = 16
S5_WIDTH = D_MODEL // 2
S5_GROUPS = S5_WIDTH // S5_GROUP_CH
S5_STATE = 64
S5_FLAT = S5_GROUPS * S5_STATE
MOE_GROUPS = 4
EXPERTS_PER_GROUP = 8
N_EXPERTS = MOE_GROUPS * EXPERTS_PER_GROUP
TOP_K = 2
EXPERT_FF = 256
RMS_EPS = 1e-6
L2_EPS = 1e-6

LANES = 128
SUBLANES = 8
VMEM_LIMIT = 56 * 1024 * 1024

W1_COLS = QKV_DIM + DN_WIDTH
W2_COLS = S5_WIDTH + 2 * D_MODEL

ROW_TILE = 512
INPREP_PARTS = 2
POSTMIX_PARTS = 2
MOE_TILE = 256
MOE_PHASES = 2
COMBINE_TILE = 256
DMA_QUEUES = 2
DELTA_SUBCHUNKS = 4
S5_SUPER = 2
S5_SCAN_SPLIT = 2
ROUTER_ROWS = 40


def _mm(a, b):
    return jnp.dot(a.astype(BF16), b.astype(BF16), preferred_element_type=F32)


def _mm_nt(a, b):
    return lax.dot_general(a.astype(BF16), b.astype(BF16), (((1,), (1,)), ((), ())),
                           preferred_element_type=F32)


def _split3_dot(a, b01):
    a1 = a.astype(BF16)
    r1 = a - a1.astype(F32)
    a2 = r1.astype(BF16)
    a3 = (r1 - a2.astype(F32)).astype(BF16)
    out = jnp.dot(a3, b01, preferred_element_type=F32)
    out = out + jnp.dot(a2, b01, preferred_element_type=F32)
    return out + jnp.dot(a1, b01, preferred_element_type=F32)


def _sigmoid(x):
    return 0.5 * jnp.tanh(0.5 * x) + 0.5


def _cparams(sem):
    return pltpu.CompilerParams(dimension_semantics=sem, vmem_limit_bytes=VMEM_LIMIT)


ROW_SLAB = D_MODEL // LANES


def _slab_load(ref, rows, first=0, pitch=ROW_SLAB):
    return jnp.concatenate([ref[pl.ds(first + j, rows, stride=pitch), :] for j in range(ROW_SLAB)], axis=1)


def _slab_store(ref, x, first=0):
    for j in range(ROW_SLAB):
        ref[pl.ds(first * ROW_SLAB + j, x.shape[0], stride=ROW_SLAB), :] = x[:, j * LANES:(j + 1) * LANES]


def _softplus(x):
    return jnp.maximum(x, 0.0) + jnp.log1p(jnp.exp(-jnp.abs(x)))


def _inprep_kernel(x_ref, nw_ref, w1_ref, w2_ref, wab_ref, cinit_ref, cw_ref, gp_ref, seg_ref,
                   q_ref, k_ref, v_ref, gate_ref, cnew_ref, z_ref, u_ref, ga_ref, gb_ref, xp_ref,
                   *, shift, rc, rows):
    @pl.when(pl.program_id(1) == 0)
    def _():
        xp_ref[0:rc, :] = cinit_ref[0]

    seg = seg_ref[...]
    pr = rows // INPREP_PARTS

    def part_stages(part):
        rs = slice(part * pr, (part + 1) * pr)
        x = x_ref[rs, :]
        h = x * lax.rsqrt(jnp.mean(x * x, axis=-1, keepdims=True) + RMS_EPS) * nw_ref[...]
        hb = h.astype(BF16)

        def proj(w_ref, lo, hi):
            return jnp.dot(hb, w_ref[:, lo:hi], preferred_element_type=F32)

        xp_ref[rc + part * pr:rc + (part + 1) * pr, :] = proj(w1_ref, 0, QKV_DIM)
        ab = proj(wab_ref, 0, LANES)
        yield
        z_ref[rs, :] = proj(w1_ref, QKV_DIM, W1_COLS).astype(z_ref.dtype)
        u_ref[rs, :] = proj(w2_ref, 0, S5_WIDTH).astype(u_ref.dtype)
        acc = None
        for i in range(CONV_W):
            lo = rc + part * pr + (i - (CONV_W - 1)) * shift
            term = xp_ref[lo:lo + pr, :] * cw_ref[i:i + 1, :]
            acc = term if acc is None else acc + term
        y = acc * _sigmoid(acc)
        yield
        ga_ref[rs, :] = proj(w2_ref, S5_WIDTH, S5_WIDTH + D_MODEL).astype(ga_ref.dtype)
        q = y[:, 0:DN_WIDTH]
        k = y[:, DN_WIDTH:2 * DN_WIDTH]
        q_ref[rs, :] = q * lax.rsqrt(jnp.dot((q * q).astype(BF16), seg, preferred_element_type=F32) + L2_EPS)
        k_ref[rs, :] = k * lax.rsqrt(jnp.dot((k * k).astype(BF16), seg, preferred_element_type=F32) + L2_EPS)
        v_ref[rs, :] = y[:, 2 * DN_WIDTH:]
        yield
        gb_ref[rs, :] = proj(w2_ref, S5_WIDTH + D_MODEL, W2_COLS).astype(gb_ref.dtype)
        g = -jnp.exp(gp_ref[0:1, :]) * _softplus(ab + gp_ref[1:2, :])
        beta = _sigmoid(ab)
        lane = lax.broadcasted_iota(I32, ab.shape, 1)
        gate_ref[rs, :] = jnp.where(lane < DN_HEADS, g, beta)

    live = []
    pending = [part_stages(p) for p in range(INPREP_PARTS)]
    while live or pending:
        if pending:
            live.append(pending.pop(0))
        live = [g for g in live if next(g, StopIteration) is not StopIteration]

    keep = (CONV_W - 1) * shift
    cnew_ref[0] = xp_ref[rc + rows - keep:rc + rows, :]
    xp_ref[0:rc, :] = xp_ref[rows:rows + rc, :]


def _inprep(x2d, nw, w_parts, cinit, conv_w, gate_p, seg, nb, shift):
    n = x2d.shape[0]
    r = n // nb
    rows = min(ROW_TILE, r)
    nt = r // rows
    rc = cinit.shape[1]
    keep = (CONV_W - 1) * shift
    row = lambda b, i: (b * nt + i, 0)
    const = lambda b, i: (0, 0)
    kern = functools.partial(_inprep_kernel, shift=shift, rc=rc, rows=rows)
    outs = pl.pallas_call(
        kern,
        grid=(nb, nt),
        in_specs=[pl.BlockSpec((rows, D_MODEL), row),
                  pl.BlockSpec((1, D_MODEL), const),
                  pl.BlockSpec((D_MODEL, W1_COLS), const),
                  pl.BlockSpec((D_MODEL, W2_COLS), const),
                  pl.BlockSpec((D_MODEL, LANES), const),
                  pl.BlockSpec((1, rc, QKV_DIM), lambda b, i: (b, 0, 0)),
                  pl.BlockSpec((CONV_W, QKV_DIM), const),
                  pl.BlockSpec((2, LANES), const),
                  pl.BlockSpec((DN_WIDTH, DN_WIDTH), const)],
        out_specs=[pl.BlockSpec((rows, DN_WIDTH), row),
                   pl.BlockSpec((rows, DN_WIDTH), row),
                   pl.BlockSpec((rows, DN_WIDTH), row),
                   pl.BlockSpec((rows, LANES), row),
                   pl.BlockSpec((1, keep, QKV_DIM), lambda b, i: (b, 0, 0)),
                   pl.BlockSpec((rows, DN_WIDTH), row),
                   pl.BlockSpec((rows, S5_WIDTH), lambda b, i: (i, b)),
                   pl.BlockSpec((rows, D_MODEL), row),
                   pl.BlockSpec((rows, D_MODEL), row)],
        out_shape=[jax.ShapeDtypeStruct((n, DN_WIDTH), F32),
                   jax.ShapeDtypeStruct((n, DN_WIDTH), F32),
                   jax.ShapeDtypeStruct((n, DN_WIDTH), F32),
                   jax.ShapeDtypeStruct((n, LANES), F32),
                   jax.ShapeDtypeStruct((nb, keep, QKV_DIM), F32),
                   jax.ShapeDtypeStruct((n, DN_WIDTH), BF16),
                   jax.ShapeDtypeStruct((r, nb * S5_WIDTH), BF16),
                   jax.ShapeDtypeStruct((n, D_MODEL), BF16),
                   jax.ShapeDtypeStruct((n, D_MODEL), BF16)],
        scratch_shapes=[pltpu.VMEM((rc + rows, QKV_DIM), F32)],
        compiler_params=_cparams(("arbitrary", "arbitrary")),
        name="inprep",
    )(x2d, nw, *w_parts, cinit, conv_w, gate_p, seg)
    return outs


def _delta_home(low, h, x, other=0.0):
    return jnp.where(low, x, other) if h % 2 == 0 else jnp.where(low, other, x)


def _delta_prepare(q_ref, k_ref, v_ref, gate_ref, tril_ref, bufs, *, nsub):
    sol_buf, wq_buf, qk_buf, kdec_buf, dl_buf = bufs
    c = DN_CHUNK
    dk = DN_HEAD_DIM

    def home(h, x, other=0.0):
        return _delta_home(low, h, x, other)

    rowi2 = lax.broadcasted_iota(I32, (c, 2 * c), 0)
    lane2 = lax.broadcasted_iota(I32, (c, 2 * c), 1)
    coli2 = lane2 & (c - 1)
    causal2 = rowi2 >= coli2
    strict2 = rowi2 > coli2
    low = lane2 < dk
    tril = tril_ref[...]
    pairs = [(j, h) for j in range(nsub) for h in range(DN_HEADS)]
    units = [(j, pr) for j in range(nsub) for pr in range(DN_HEADS // 2)]
    rows = [slice(j * c, (j + 1) * c) for j in range(nsub)]
    gate = [gate_ref[rows[j], :] for j in range(nsub)]
    gc_all = [_split3_dot_left(tril, gate[j]) for j in range(nsub)]
    gc_t = [gc_all[j].T for j in range(nsub)]

    def block(ref, j, pr):
        return ref[rows[j], pr * LANES:(pr + 1) * LANES]

    gfull = {(j, h): jnp.broadcast_to(gc_all[j][:, h:h + 1], (c, 2 * c)) for j, h in pairs}
    g2 = {(j, pr): jnp.where(low, gfull[j, 2 * pr], gfull[j, 2 * pr + 1]) for j, pr in units}
    b2 = {(j, pr): jnp.where(low, gate[j][:, DN_HEADS + 2 * pr:DN_HEADS + 2 * pr + 1],
                             gate[j][:, DN_HEADS + 2 * pr + 1:DN_HEADS + 2 * pr + 2]) for j, pr in units}
    kp = {u: block(k_ref, *u) for u in units}
    qp = {u: block(q_ref, *u) * (dk ** -0.5) for u in units}
    egc2 = {u: jnp.exp(g2[u]) for u in units}
    kb2 = {u: kp[u] * b2[u] for u in units}
    vb2 = {u: block(v_ref, *u) * b2[u] for u in units}
    kw2s = {u: pltpu.roll(kb2[u] * egc2[u], dk, axis=1) for u in units}
    qd2 = {u: qp[u] * egc2[u] for u in units}
    glast2 = {u: g2[u][c - 1:c, :] for u in units}
    kdec_t2 = {u: (kp[u] * jnp.exp(glast2[u] - g2[u])).T for u in units}
    dlast2 = {u: jnp.exp(glast2[u]) for u in units}
    kk = {u: jnp.concatenate([kp[u], kp[u]], axis=0) for u in units}
    yield

    grow2 = {(j, h): jnp.concatenate([gc_t[j][h:h + 1, :], gc_t[j][h:h + 1, :]], axis=1) for j, h in pairs}
    decay = {p: jnp.where(causal2, jnp.exp(jnp.where(causal2, gfull[p] - grow2[p], 0.0)), 0.0) for p in pairs}
    gram = {(j, h): _mm_nt(jnp.concatenate([home(h, kb2[j, h // 2]), home(h, qp[j, h // 2])], axis=0), kk[j, h // 2])
            for j, h in pairs}
    mat = {p: jnp.where(strict2, gram[p][:c] * decay[p], 0.0).astype(BF16) for p in pairs}
    qk = {p: jnp.where(causal2, gram[p][c:] * decay[p], 0.0) for p in pairs}
    sol = {(j, h): home(h, vb2[j, h // 2], kw2s[j, h // 2]) for j, h in pairs}
    yield
    levels = int(math.log2(c))
    zeros2 = jnp.zeros((c, 2 * c), BF16)
    for lvl in range(levels):
        hi = {p: sol[p].astype(BF16) for p in pairs}
        lo = {p: (sol[p] - hi[p].astype(F32)).astype(BF16) for p in pairs}
        if lvl < levels - 1:
            y = {p: jnp.dot(mat[p], jnp.concatenate([jnp.concatenate([hi[p], mat[p]], axis=1),
                                                     jnp.concatenate([lo[p], zeros2], axis=1)], axis=0),
                            preferred_element_type=F32) for p in pairs}
            mat = {p: y[p][:, 2 * dk:].astype(BF16) for p in pairs}
            upd = {p: y[p][:, :2 * dk] for p in pairs}
        else:
            upd = {p: jnp.dot(mat[p], jnp.concatenate([hi[p], lo[p]], axis=0), preferred_element_type=F32)
                   for p in pairs}
        sol = {p: (sol[p] - upd[p]) if lvl == 0 else (sol[p] + upd[p]) for p in pairs}
        yield
    for j, h in pairs:
        n = j * DN_HEADS + h
        sol_buf[n] = sol[j, h]
        wq_buf[n] = jnp.concatenate([home(h, 0.0, sol[j, h]), home(h, qd2[j, h // 2])], axis=0).astype(BF16)
        qk_buf[n] = qk[j, h].astype(BF16)
    for j, pr in units:
        n = j * (DN_HEADS // 2) + pr
        kdec_buf[n] = kdec_t2[j, pr].astype(BF16)
        dl_buf[n] = jnp.broadcast_to(dlast2[j, pr], (SUBLANES, LANES))


def _delta_apply(bufs, o_ref, s_ref, *, nsub):
    sol_buf, wq_buf, qk_buf, kdec_buf, dl_buf = bufs
    c = DN_CHUNK
    dk = DN_HEAD_DIM
    heads = range(DN_HEADS)
    low = lax.broadcasted_iota(I32, (c, 2 * c), 1) < dk
    s = [s_ref[h] for h in heads]
    for j in range(nsub):
        ws, v_new, o_h = [], [], []
        for h in heads:
            n = j * DN_HEADS + h
            ws.append(jnp.dot(wq_buf[n], jnp.concatenate([s[h], s[h]], axis=0).astype(BF16),
                              preferred_element_type=F32))
        yield
        for h in heads:
            v_new.append(sol_buf[j * DN_HEADS + h] - ws[h][:c])
        for h in heads:
            o_h.append(ws[h][c:] + jnp.dot(qk_buf[j * DN_HEADS + h][:, :c], v_new[h].astype(BF16),
                                           preferred_element_type=F32))
        for pr in range(DN_HEADS // 2):
            o_ref[j * c:(j + 1) * c, pr * LANES:(pr + 1) * LANES] = jnp.where(low, o_h[2 * pr], o_h[2 * pr + 1])
        nxt = []
        for h in heads:
            u = j * (DN_HEADS // 2) + h // 2
            kdt = kdec_buf[u][(h % 2) * dk:(h % 2 + 1) * dk, :]
            d = dl_buf[u][0:1, :]
            nxt.append(_delta_home(low, h, s[h] * d + jnp.dot(kdt, v_new[h].astype(BF16),
                                                               preferred_element_type=F32)))
        s = nxt
        yield
    for h in heads:
        s_ref[h] = s[h]


def _delta_chunk_kernel(q_ref, k_ref, v_ref, gate_ref, tril_ref, o_ref, sfin_ref, s_ref, *bufs, nsub, nc):
    i = pl.program_id(0)
    half = len(bufs) // 2
    sets = (bufs[:half], bufs[half:])
    local = lax.rem(jnp.maximum(i - 1, 0), nc)

    @pl.when(i == 0)
    def _():
        for b in sets[1]:
            b[...] = jnp.zeros_like(b)

    @pl.when(local == 0)
    def _():
        s_ref[...] = jnp.zeros_like(s_ref)

    for par in range(2):
        @pl.when(lax.rem(i, 2) == par)
        def _(par=par):
            parts = [_delta_prepare(q_ref, k_ref, v_ref, gate_ref, tril_ref, sets[par], nsub=nsub),
                     _delta_apply(sets[1 - par], o_ref, s_ref, nsub=nsub)]
            while parts:
                parts = [g for g in parts if next(g, StopIteration) is not StopIteration]

    @pl.when(jnp.logical_and(i >= 1, local == nc - 1))
    def _():
        dk = DN_HEAD_DIM
        for h in range(DN_HEADS):
            sfin_ref[0, h] = s_ref[h][:, (h % 2) * dk:(h % 2 + 1) * dk]


def _split3_dot_left(b01, a):
    a1 = a.astype(BF16)
    r1 = a - a1.astype(F32)
    a2 = r1.astype(BF16)
    a3 = (r1 - a2.astype(F32)).astype(BF16)
    out = jnp.dot(b01, a3, preferred_element_type=F32)
    out = out + jnp.dot(b01, a2, preferred_element_type=F32)
    return out + jnp.dot(b01, a1, preferred_element_type=F32)


def _delta_prompt(q, k, v, gate, nb):
    n = q.shape[0]
    t = n // nb
    c = DN_CHUNK
    nsub = DELTA_SUBCHUNKS
    rows = nsub * c
    nc = t // rows
    nblk = nb * nc
    row_in = lambda i: (jnp.minimum(i, nblk - 1), 0)
    row_out = lambda i: (jnp.maximum(i - 1, 0), 0)
    tril = jnp.tril(jnp.ones((c, c), F32)).astype(BF16)
    nh = nsub * DN_HEADS
    npair = nsub * DN_HEADS // 2
    buf_set = [pltpu.VMEM((nh, c, 2 * DN_HEAD_DIM), F32),
               pltpu.VMEM((nh, 2 * c, 2 * DN_HEAD_DIM), BF16),
               pltpu.VMEM((nh, c, 2 * c), BF16),
               pltpu.VMEM((npair, 2 * DN_HEAD_DIM, c), BF16),
               pltpu.VMEM((npair, SUBLANES, LANES), F32)]
    return pl.pallas_call(
        functools.partial(_delta_chunk_kernel, nsub=nsub, nc=nc),
        grid=(nblk + 1,),
        in_specs=[pl.BlockSpec((rows, DN_WIDTH), row_in),
                  pl.BlockSpec((rows, DN_WIDTH), row_in),
                  pl.BlockSpec((rows, DN_WIDTH), row_in),
                  pl.BlockSpec((rows, LANES), row_in),
                  pl.BlockSpec((c, c), lambda i: (0, 0))],
        out_specs=[pl.BlockSpec((rows, DN_WIDTH), row_out),
                   pl.BlockSpec((1, DN_HEADS, DN_HEAD_DIM, DN_HEAD_DIM),
                                lambda i: (jnp.maximum(i - 1, 0) // nc, 0, 0, 0))],
        out_shape=[jax.ShapeDtypeStruct((n, DN_WIDTH), F32),
                   jax.ShapeDtypeStruct((nb, DN_HEADS, DN_HEAD_DIM, DN_HEAD_DIM), F32)],
        scratch_shapes=[pltpu.VMEM((DN_HEADS, DN_HEAD_DIM, 2 * DN_HEAD_DIM), F32)] + buf_set + buf_set,
        compiler_params=_cparams(("arbitrary",)),
        name="delta_prompt",
    )(q, k, v, gate, tril)


def _delta_step_kernel(q_ref, k_ref, v_ref, gate_ref, s0_ref, o_ref, s_ref, kt_ref, qt_ref, gt_ref, *, nt, nb):
    dk = DN_HEAD_DIM
    p = pl.program_id(0)
    for t in range(nt):
        rs = slice(t * nb, (t + 1) * nb)
        gt_ref[...] = gate_ref[rs, :].T
        kt_ref[...] = k_ref[rs, :].T
        qt_ref[...] = (q_ref[rs, :] * (dk ** -0.5)).T
        vt = v_ref[rs, :].T
        src = s0_ref if t == 0 else s_ref
        o_heads = []
        for j in range(2):
            a = jnp.exp(gt_ref[pl.ds(2 * p + j, 1), :])
            beta = gt_ref[pl.ds(2 * p + j + DN_HEADS, 1), :]
            base = j * dk * dk

            def k_dot_s(d, acc, j=j, base=base, src=src):
                sd = src[pl.ds(pl.multiple_of(base + d * dk, dk), dk), :]
                return acc + kt_ref[pl.ds(j * dk + d, 1), :] * sd

            ks = lax.fori_loop(0, dk, k_dot_s, jnp.zeros((dk, nb), F32), unroll=4)
            delta = beta * (vt[j * dk:(j + 1) * dk, :] - a * ks)

            def update(d, acc, j=j, base=base, src=src, a=a, delta=delta):
                r0 = pl.multiple_of(base + d * dk, dk)
                sn = a * src[pl.ds(r0, dk), :] + kt_ref[pl.ds(j * dk + d, 1), :] * delta
                s_ref[pl.ds(r0, dk), :] = sn
                return acc + qt_ref[pl.ds(j * dk + d, 1), :] * sn

            o_heads.append(lax.fori_loop(0, dk, update, jnp.zeros((dk, nb), F32), unroll=4))
        o_ref[rs, :] = jnp.concatenate(o_heads, axis=0).T


def _delta_sample(q, k, v, gate, s0t, nb, nt):
    dk = DN_HEAD_DIM
    flat = dk * dk
    n = nt * nb
    kern = functools.partial(_delta_step_kernel, nt=nt, nb=nb)
    pair = lambda p: (0, p)
    return pl.pallas_call(
        kern,
        grid=(DN_HEADS // 2,),
        in_specs=[pl.BlockSpec((n, LANES), pair),
                  pl.BlockSpec((n, LANES), pair),
                  pl.BlockSpec((n, LANES), pair),
                  pl.BlockSpec((n, LANES), lambda p: (0, 0)),
                  pl.BlockSpec((2 * flat, nb), lambda p: (p, 0))],
        out_specs=[pl.BlockSpec((n, LANES), pair),
                   pl.BlockSpec((2 * flat, nb), lambda p: (p, 0))],
        out_shape=[jax.ShapeDtypeStruct((n, DN_WIDTH), F32),
                   jax.ShapeDtypeStruct((DN_HEADS * flat, nb), F32)],
        scratch_shapes=[pltpu.VMEM((LANES, nb), F32),
                        pltpu.VMEM((LANES, nb), F32),
                        pltpu.VMEM((LANES, nb), F32)],
        compiler_params=_cparams(("arbitrary",)),
        name="delta_sample",
    )(q, k, v, gate, s0t)


def _s5_kernel(u_ref, btre_ref, btim_ref, lam_ref, ctre_ref, ctim_ref, d_ref, h0_ref, y_ref, hfin_ref,
               bw_ref, c_ref, ab_ref, x_ref, h_ref, ru_ref, ry_ref, *, nb, tt, wide):
    p2 = S5_FLAT

    @pl.when(pl.program_id(0) == 0)
    def _():
        lr = lam_ref[0:1, :]
        li = lam_ref[1:2, :]
        dt = jnp.exp(lam_ref[2:3, :])
        mag = jnp.exp(lr * dt)
        ab_re = mag * jnp.cos(li * dt)
        ab_im = mag * jnp.sin(li * dt)
        den = lr * lr + li * li
        nr = ab_re - 1.0
        ni = ab_im
        f_re = (nr * lr + ni * li) / den
        f_im = (ni * lr - nr * li) / den
        ab_ref[0:1, :] = ab_re
        ab_ref[1:2, :] = ab_im
        gpl = LANES // S5_STATE
        ch_g = lax.broadcasted_iota(I32, (S5_WIDTH, LANES), 0) // S5_GROUP_CH
        lane_g = lax.broadcasted_iota(I32, (S5_WIDTH, LANES), 1) // S5_STATE
        bre2 = jnp.concatenate([btre_ref[...]] * gpl, axis=1)
        bim2 = jnp.concatenate([btim_ref[...]] * gpl, axis=1)
        for j in range(p2 // LANES):
            cols = slice(j * LANES, (j + 1) * LANES)
            own = ch_g == gpl * j + lane_g
            bre = jnp.where(own, bre2, 0.0)
            bim = jnp.where(own, bim2, 0.0)
            bw_ref[:, cols] = (bre * f_re[:, cols] - bim * f_im[:, cols]).astype(BF16)
            bw_ref[:, p2 + j * LANES:p2 + (j + 1) * LANES] = (bim * f_re[:, cols] + bre * f_im[:, cols]).astype(BF16)
        cpl = LANES // S5_GROUP_CH
        st_g = lax.broadcasted_iota(I32, (p2, LANES), 0) // S5_STATE
        lane_cg = lax.broadcasted_iota(I32, (p2, LANES), 1) // S5_GROUP_CH
        for j in range(S5_WIDTH // LANES):
            cols = slice(j * LANES, (j + 1) * LANES)
            own = st_g == cpl * j + lane_cg
            c_ref[0:p2, cols] = jnp.where(own, ctre_ref[...], 0.0).astype(BF16)
            c_ref[p2:2 * p2, cols] = jnp.where(own, -ctim_ref[...], 0.0).astype(BF16)
        h_ref[...] = h0_ref[...]

    nck = S5_WIDTH // LANES
    if wide:
        for b in range(nb):
            for ck in range(nck):
                lo = b * S5_WIDTH + ck * LANES
                ru_ref[ck, pl.ds(b, tt, stride=nb), :] = u_ref[:, lo:lo + LANES].astype(F32)
        u = jnp.concatenate([ru_ref[ck] for ck in range(nck)], axis=1)
    else:
        u = u_ref[...].astype(F32)
    ub = u.astype(BF16)
    cw = S5_WIDTH // S5_SUPER
    sw = S5_FLAT // S5_SUPER
    for part in (0, p2):
        for b in range(S5_SUPER):
            x_ref[:, part + b * sw:part + (b + 1) * sw] = jnp.dot(
                ub[:, b * cw:(b + 1) * cw], bw_ref[b * cw:(b + 1) * cw, part + b * sw:part + (b + 1) * sw],
                preferred_element_type=F32)
    a_re = ab_ref[0:1, :]
    a_im = ab_ref[1:2, :]

    if nb == SUBLANES:
        wsl = p2 // S5_SCAN_SPLIT
        for sp in range(S5_SCAN_SPLIT):
            c0 = sp * wsl
            are = jnp.broadcast_to(a_re[:, c0:c0 + wsl], (nb, wsl))
            aim = jnp.broadcast_to(a_im[:, c0:c0 + wsl], (nb, wsl))

            def step(t, carry, c0=c0, are=are, aim=aim):
                hr, hi = carry
                r0 = pl.multiple_of(t * nb, nb)
                nr = are * hr - aim * hi + x_ref[pl.ds(r0, nb), c0:c0 + wsl]
                ni = are * hi + aim * hr + x_ref[pl.ds(r0, nb), p2 + c0:p2 + c0 + wsl]
                x_ref[pl.ds(r0, nb), c0:c0 + wsl] = nr
                x_ref[pl.ds(r0, nb), p2 + c0:p2 + c0 + wsl] = ni
                return nr, ni

            hr, hi = lax.fori_loop(0, tt, step, (h_ref[:, c0:c0 + wsl], h_ref[:, p2 + c0:p2 + c0 + wsl]),
                                   unroll=2)
            h_ref[:, c0:c0 + wsl] = hr
            h_ref[:, p2 + c0:p2 + c0 + wsl] = hi
    else:
        for t in range(tt):
            rs = slice(t * nb, (t + 1) * nb)
            hr = h_ref[:, 0:p2]
            hi = h_ref[:, p2:2 * p2]
            nr = a_re * hr - a_im * hi + x_ref[rs, 0:p2]
            ni = a_re * hi + a_im * hr + x_ref[rs, p2:2 * p2]
            h_ref[:, 0:p2] = nr
            h_ref[:, p2:2 * p2] = ni
            x_ref[rs, 0:p2] = nr
            x_ref[rs, p2:2 * p2] = ni

    for b in range(S5_SUPER):
        cols = slice(b * cw, (b + 1) * cw)
        y = None
        for part in (0, p2):
            rws = slice(part + b * sw, part + (b + 1) * sw)
            term = jnp.dot(x_ref[:, rws].astype(BF16), c_ref[rws, cols], preferred_element_type=F32)
            y = term if y is None else y + term
        y = y + d_ref[:, cols] * u[:, cols]
        if wide:
            for ck in range(cw // LANES):
                ry_ref[b * (cw // LANES) + ck] = y[:, ck * LANES:(ck + 1) * LANES]
        else:
            y_ref[:, cols] = y
    if wide:
        for b in range(nb):
            for ck in range(nck):
                lo = b * S5_WIDTH + ck * LANES
                y_ref[:, lo:lo + LANES] = ry_ref[ck, pl.ds(b, tt, stride=nb), :]
    hfin_ref[...] = h_ref[...]


def _s5(u, params, h0, nb, wide):
    btre, btim, lam, ctre, ctim, dvec = params
    t = u.shape[0] if wide else u.shape[0] // nb
    tt = min(ROW_TILE // nb, t)
    rows = tt * nb
    const = lambda i: (0, 0)
    kern = functools.partial(_s5_kernel, nb=nb, tt=tt, wide=wide)
    io_block = (tt, nb * S5_WIDTH) if wide else (rows, S5_WIDTH)
    return pl.pallas_call(
        kern,
        grid=(t // tt,),
        in_specs=[pl.BlockSpec(io_block, lambda i: (i, 0)),
                  pl.BlockSpec((S5_WIDTH, S5_STATE), const),
                  pl.BlockSpec((S5_WIDTH, S5_STATE), const),
                  pl.BlockSpec((SUBLANES, S5_FLAT), const),
                  pl.BlockSpec((S5_FLAT, LANES), const),
                  pl.BlockSpec((S5_FLAT, LANES), const),
                  pl.BlockSpec((1, S5_WIDTH), const),
                  pl.BlockSpec((nb, 2 * S5_FLAT), const)],
        out_specs=[pl.BlockSpec(io_block, lambda i: (i, 0)),
                   pl.BlockSpec((nb, 2 * S5_FLAT), const)],
        out_shape=[jax.ShapeDtypeStruct(u.shape, F32),
                   jax.ShapeDtypeStruct((nb, 2 * S5_FLAT), F32)],
        scratch_shapes=[pltpu.VMEM((S5_WIDTH, 2 * S5_FLAT), BF16),
                        pltpu.VMEM((2 * S5_FLAT, S5_WIDTH), BF16),
                        pltpu.VMEM((SUBLANES, S5_FLAT), F32),
                        pltpu.VMEM((rows, 2 * S5_FLAT), F32),
                        pltpu.VMEM((nb, 2 * S5_FLAT), F32),
                        pltpu.VMEM((S5_WIDTH // LANES, rows, LANES), F32),
                        pltpu.VMEM((S5_WIDTH // LANES, rows, LANES), F32)],
        compiler_params=_cparams(("arbitrary",)),
        name="s5",
    )(u, btre, btim, lam, ctre, ctim, dvec, h0)


def _postmix_kernel(xp_ref, op_ref, zp_ref, ysp_ref, gap_ref, gbp_ref,
                    xs_ref, os_ref, zs_ref, yss_ref, gas_ref, gbs_ref, *rest, nblk_p, range_tok):
    carry_ref = rest[-1]

    @pl.when(pl.program_id(0) == 0)
    def _():
        carry_ref[...] = jnp.zeros_like(carry_ref)

    @pl.when(pl.program_id(0) < nblk_p)
    def _():
        _postmix_body(xp_ref, op_ref, zp_ref, ysp_ref, gap_ref, gbp_ref, *rest, range_tok=range_tok)

    @pl.when(pl.program_id(0) >= nblk_p)
    def _():
        _postmix_body(xs_ref, os_ref, zs_ref, yss_ref, gas_ref, gbs_ref, *rest, range_tok=range_tok)


def _postmix_body(x_ref, o_ref, z_ref, ys_ref, ga_ref, gb_ref, hw_ref, seg_ref, wa_ref, wglu_ref, wb_ref,
                  wo_ref, nf_ref, wr_ref, su_ref, x1_ref, hn_ref, bkt_ref, rank_ref, rw_ref, cnt_ref, carry_ref,
                  *, range_tok):
    rows = x_ref.shape[0]
    pr = rows // POSTMIX_PARTS
    parts = [_postmix_part(p, pr, x_ref, o_ref, z_ref, ys_ref, ga_ref, gb_ref, hw_ref, seg_ref, wa_ref, wglu_ref,
                           wb_ref, wo_ref, nf_ref, wr_ref, su_ref, x1_ref, hn_ref, bkt_ref, rank_ref, rw_ref,
                           carry_ref, range_tok) for p in range(POSTMIX_PARTS)]
    live = []
    while live or parts:
        if parts:
            live.append(parts.pop(0))
        live = [g for g in live if next(g, StopIteration) is not StopIteration]
    cnt_ref[...] = carry_ref[...]


def _postmix_part(part, pr, x_ref, o_ref, z_ref, ys_ref, ga_ref, gb_ref, hw_ref, seg_ref, wa_ref, wglu_ref, wb_ref,
                  wo_ref, nf_ref, wr_ref, su_ref, x1_ref, hn_ref, bkt_ref, rank_ref, rw_ref, carry_ref, range_tok):
    rs = slice(part * pr, (part + 1) * pr)
    o = o_ref[rs, :]
    ms = jnp.dot((o * o).astype(BF16), seg_ref[...], preferred_element_type=F32) * (1.0 / DN_HEAD_DIM)
    on = o * lax.rsqrt(ms + RMS_EPS) * hw_ref[...]
    z = z_ref[rs, :]
    oa = on * (z * _sigmoid(z)).astype(F32)
    yield
    y_a = _mm(oa, wa_ref[...])
    ys = jax.nn.gelu(ys_ref[rs, :])
    yield
    ys = ys * _sigmoid(_mm(ys, wglu_ref[...]))
    yield
    y_b = _mm(ys, wb_ref[...])
    mixed = _sigmoid(ga_ref[rs, :]).astype(F32) * y_a + _sigmoid(gb_ref[rs, :]).astype(F32) * y_b
    yield
    x1 = x_ref[rs, :] + _mm(mixed, wo_ref[...])
    x1_ref[rs, :] = x1
    hn = x1 * lax.rsqrt(jnp.mean(x1 * x1, axis=-1, keepdims=True) + RMS_EPS) * nf_ref[...]
    _slab_store(hn_ref, hn, part * pr)
    yield

    wr = wr_ref[...]
    w_hi = wr.astype(BF16)
    w_lo = (wr - w_hi.astype(F32)).astype(BF16)
    hn_hi = hn.astype(BF16)
    hn_lo = (hn - hn_hi.astype(F32)).astype(BF16)
    both = _mm_nt(jnp.concatenate([w_hi, w_lo], axis=0), hn_hi)
    logits = both[:ROUTER_ROWS] + both[ROUTER_ROWS:] + _mm_nt(w_hi, hn_lo)
    yield
    coarse = logits[N_EXPERTS:N_EXPERTS + MOE_GROUPS, :]
    cm = jnp.max(coarse, axis=0, keepdims=True)
    ce = jnp.exp(coarse - cm)
    pc = ce / jnp.sum(ce, axis=0, keepdims=True)
    p_sel = jnp.max(pc, axis=0, keepdims=True)
    gi = lax.broadcasted_iota(I32, pc.shape, 0)
    g_sel = jnp.min(jnp.where(pc == p_sel, gi, MOE_GROUPS), axis=0, keepdims=True)
    fine = jnp.zeros((EXPERTS_PER_GROUP, logits.shape[1]), F32)
    for g in range(MOE_GROUPS):
        fine = fine + jnp.where(g_sel == g, logits[g * EXPERTS_PER_GROUP:(g + 1) * EXPERTS_PER_GROUP, :], 0.0)
    fm = jnp.max(fine, axis=0, keepdims=True)
    fe = jnp.exp(fine - fm)
    pf = fe / jnp.sum(fe, axis=0, keepdims=True)
    ei = lax.broadcasted_iota(I32, pf.shape, 0)
    v1 = jnp.max(pf, axis=0, keepdims=True)
    i1 = jnp.min(jnp.where(pf == v1, ei, EXPERTS_PER_GROUP), axis=0, keepdims=True)
    rest = jnp.where(ei == i1, -1.0, pf)
    v2 = jnp.max(rest, axis=0, keepdims=True)
    i2 = jnp.min(jnp.where(rest == v2, ei, EXPERTS_PER_GROUP), axis=0, keepdims=True)
    tot = v1 + v2
    rw_ref[0:1, rs] = v1 / tot * p_sel
    rw_ref[1:2, rs] = v2 / tot * p_sel

    tok = pl.program_id(0) * (pr * POSTMIX_PARTS) + part * pr + lax.broadcasted_iota(I32, (1, pr), 1)
    ph = jnp.zeros((1, pr), I32)
    for r in range(1, MOE_PHASES):
        ph = ph + (tok >= r * range_tok).astype(I32)
    bsel = [ph * N_EXPERTS + g_sel * EXPERTS_PER_GROUP + ix for ix in (i1, i2)]
    bi = lax.broadcasted_iota(I32, (MOE_PHASES * N_EXPERTS, pr), 0)
    onehot = [(bi == b).astype(F32) for b in bsel]
    cnt = onehot[0] + onehot[1]
    before = carry_ref[:, 0:1] + jnp.dot(cnt.astype(BF16), su_ref[0:pr, 0:pr], preferred_element_type=F32)
    for s in range(TOP_K):
        bkt_ref[s:s + 1, rs] = bsel[s]
        rank_ref[s:s + 1, rs] = jnp.sum(onehot[s] * before, axis=0, keepdims=True).astype(I32)
    carry_ref[...] = carry_ref[...] + jnp.sum(cnt, axis=1, keepdims=True)


def _postmix(prompt, sample, weights, nb):
    n_p = prompt[0].shape[0]
    n_s = sample[0].shape[0]
    t = n_p // nb
    tt = min(ROW_TILE, t, n_s)
    nt = t // tt
    nblk_p = n_p // tt
    nblk = nblk_p + n_s // tt
    n_total = n_p + n_s
    prow = lambda i: (jnp.minimum(i, nblk_p - 1), 0)
    pys = lambda i: (jnp.minimum(i, nblk_p - 1) % nt, jnp.minimum(i, nblk_p - 1) // nt)
    srow = lambda i: (jnp.maximum(i - nblk_p, 0), 0)
    const = lambda i: (0, 0)

    def stream_specs(row, ysmap):
        return [pl.BlockSpec((tt, D_MODEL), row),
                pl.BlockSpec((tt, DN_WIDTH), row),
                pl.BlockSpec((tt, DN_WIDTH), row),
                pl.BlockSpec((tt, S5_WIDTH), ysmap),
                pl.BlockSpec((tt, D_MODEL), row),
                pl.BlockSpec((tt, D_MODEL), row)]

    weight_specs = [pl.BlockSpec((1, DN_WIDTH), const),
                    pl.BlockSpec((DN_WIDTH, DN_WIDTH), const),
                    pl.BlockSpec((DN_WIDTH, D_MODEL), const),
                    pl.BlockSpec((S5_WIDTH, S5_WIDTH), const),
                    pl.BlockSpec((S5_WIDTH, D_MODEL), const),
                    pl.BlockSpec((D_MODEL, D_MODEL), const),
                    pl.BlockSpec((1, D_MODEL), const),
                    pl.BlockSpec((ROUTER_ROWS, D_MODEL), const),
                    pl.BlockSpec((tt, tt), const)]
    xp, op, zp, ysp, gap, gbp = prompt
    nbk = MOE_PHASES * N_EXPERTS
    earlier = jnp.triu(jnp.ones((tt, tt), F32), k=1).astype(BF16)
    return pl.pallas_call(
        functools.partial(_postmix_kernel, nblk_p=nblk_p, range_tok=n_total // MOE_PHASES),
        grid=(nblk,),
        in_specs=stream_specs(prow, pys) + stream_specs(srow, srow) + weight_specs,
        out_specs=[pl.BlockSpec((tt, D_MODEL), lambda i: (i, 0)),
                   pl.BlockSpec((tt * ROW_SLAB, LANES), lambda i: (i, 0)),
                   pl.BlockSpec((TOP_K, tt), lambda i: (0, i)),
                   pl.BlockSpec((TOP_K, tt), lambda i: (0, i)),
                   pl.BlockSpec((TOP_K, tt), lambda i: (0, i)),
                   pl.BlockSpec((nbk, LANES), const)],
        out_shape=[jax.ShapeDtypeStruct((n_total, D_MODEL), F32),
                   jax.ShapeDtypeStruct((n_total * ROW_SLAB, LANES), F32),
                   jax.ShapeDtypeStruct((TOP_K, n_total), I32),
                   jax.ShapeDtypeStruct((TOP_K, n_total), I32),
                   jax.ShapeDtypeStruct((TOP_K, n_total), F32),
                   jax.ShapeDtypeStruct((nbk, LANES), F32)],
        scratch_shapes=[pltpu.VMEM((nbk, LANES), F32)],
        compiler_params=_cparams(("arbitrary",)),
        name="postmix",
    )(xp, op, zp, ysp, gap, gbp, *sample, *weights, earlier)


def _wait_slabs(buf, sem):
    pltpu.make_async_copy(buf, buf, sem).wait()


def _moe_kernel(texp_ref, tph_ref, tsrc_ref, tnv_ref, tfirst_ref, tslot_ref, tnext_ref, otok_ref,
                hn_hbm, wu_hbm, wd_hbm, y_ref, hnv, xbuf, wu_buf, wd_buf, wub, wdb, sem, wsem):
    i = pl.program_id(0)
    tm = MOE_TILE
    rs = ROW_SLAB
    nv = tnv_ref[i]
    ph = tph_ref[i]
    range_rows = hnv.shape[0]

    def weight_copies(e, sl):
        return (pltpu.make_async_copy(wu_hbm.at[e], wu_buf.at[sl], wsem.at[sl]),
                pltpu.make_async_copy(wd_hbm.at[e], wd_buf.at[sl], wsem.at[sl]))

    @pl.when(i == 0)
    def _():
        for c in weight_copies(texp_ref[0], 0):
            c.start()

    @pl.when(jnp.logical_and(nv > 0, jnp.logical_or(i == 0, ph != tph_ref[jnp.maximum(i - 1, 0)])))
    def _():
        start = pl.multiple_of(ph * range_rows, rs)
        whole = pltpu.make_async_copy(hn_hbm.at[pl.ds(start, range_rows), :], hnv, sem)
        whole.start()
        whole.wait()

    for sl in range(2):
        @pl.when(jnp.logical_and(jnp.logical_and(nv > 0, tfirst_ref[i] == 1), tslot_ref[i] == sl))
        def _():
            for c in weight_copies(texp_ref[i], sl):
                c.wait()

            @pl.when(tnext_ref[i] >= 0)
            def _():
                for c in weight_copies(tnext_ref[i], 1 - sl):
                    c.start()

            wub[...] = wu_buf[sl].astype(BF16)
            wdb[...] = wd_buf[sl].astype(BF16)

    @pl.when(nv == 0)
    def _():
        y_ref[...] = jnp.zeros_like(y_ref)

    @pl.when(nv > 0)
    def _():
        src0 = tsrc_ref[i]
        for r in range(tm):
            tok8 = pl.multiple_of(otok_ref[src0 + r], rs)
            xbuf[pl.ds(r * rs, rs), :] = hnv[pl.ds(tok8, rs), :]
        x = _slab_load(xbuf, tm).astype(BF16)
        hu = jnp.dot(x, wub[...], preferred_element_type=F32)
        gate = hu[:, :EXPERT_FF]
        up = hu[:, EXPERT_FF:]
        act = gate * _sigmoid(gate) * up
        _slab_store(y_ref, jnp.dot(act.astype(BF16), wdb[...], preferred_element_type=F32))


def _moe(hn, w_up, w_down, plan):
    ntiles = plan[0].shape[0]
    grid_spec = pltpu.PrefetchScalarGridSpec(
        num_scalar_prefetch=len(plan),
        grid=(ntiles,),
        in_specs=[pl.BlockSpec(memory_space=pl.ANY),
                  pl.BlockSpec(memory_space=pl.ANY),
                  pl.BlockSpec(memory_space=pl.ANY)],
        out_specs=pl.BlockSpec((MOE_TILE * ROW_SLAB, LANES), lambda i, *_: (i, 0)),
        scratch_shapes=[pltpu.VMEM((hn.shape[0] // MOE_PHASES, LANES), F32),
                        pltpu.VMEM((MOE_TILE * ROW_SLAB, LANES), F32),
                        pltpu.VMEM((2, D_MODEL, 2 * EXPERT_FF), F32),
                        pltpu.VMEM((2, EXPERT_FF, D_MODEL), F32),
                        pltpu.VMEM((D_MODEL, 2 * EXPERT_FF), BF16),
                        pltpu.VMEM((EXPERT_FF, D_MODEL), BF16),
                        pltpu.SemaphoreType.DMA,
                        pltpu.SemaphoreType.DMA((2,))])
    return pl.pallas_call(
        _moe_kernel,
        grid_spec=grid_spec,
        out_shape=jax.ShapeDtypeStruct((ntiles * MOE_TILE * ROW_SLAB, LANES), F32),
        compiler_params=_cparams(("arbitrary",)),
        name="moe",
    )(*plan, hn, w_up, w_down)


def _combine_kernel(pos_ref, x1_ref, ys_hbm, w_ref, nw_ref, outp_ref, outs_ref,
                    ybuf0, ybuf1, sem, *, nblk_p, n_tok):
    i = pl.program_id(0)
    nsteps = pl.num_programs(0)
    tt = x1_ref.shape[0]
    rs = ROW_SLAB
    slot = lax.rem(i, 2)
    ybuf = (ybuf0, ybuf1)

    def start_gather(step, sl):
        base = step * tt
        for r in range(tt * TOP_K):
            j, s = divmod(r, TOP_K)
            p8 = pl.multiple_of(pos_ref[s * n_tok + base + j], rs)
            pltpu.make_async_copy(ys_hbm.at[pl.ds(p8, rs), :], ybuf[sl].at[pl.ds((s * tt + j) * rs, rs), :],
                                  sem.at[sl]).start(priority=r % DMA_QUEUES)

    @pl.when(i == 0)
    def _():
        start_gather(0, 0)

    for sl in range(2):
        @pl.when(slot == sl)
        def _():
            _wait_slabs(ybuf[sl], sem.at[sl])
            start_gather(jnp.minimum(i + 1, nsteps - 1), 1 - sl)
            w = w_ref[...]
            y0 = _slab_load(ybuf[sl], tt, 0)
            y1 = _slab_load(ybuf[sl], tt, tt * rs)
            x = x1_ref[...] + w[:, 0:1] * y0 + w[:, 1:2] * y1
            res = x * lax.rsqrt(jnp.mean(x * x, axis=-1, keepdims=True) + RMS_EPS) * nw_ref[...]

            @pl.when(i < nblk_p)
            def _():
                outp_ref[...] = res

            @pl.when(i >= nblk_p)
            def _():
                outs_ref[...] = res

        @pl.when(jnp.logical_and(slot == sl, i == nsteps - 1))
        def _():
            _wait_slabs(ybuf[1 - sl], sem.at[1 - sl])


def _combine(x1, ysorted, pos8, wtok, nw, n_p):
    n = x1.shape[0]
    tt = math.gcd(math.gcd(n_p, n - n_p), COMBINE_TILE)
    nblk_p = n_p // tt
    grid_spec = pltpu.PrefetchScalarGridSpec(
        num_scalar_prefetch=1,
        grid=(n // tt,),
        in_specs=[pl.BlockSpec((tt, D_MODEL), lambda i, *_: (i, 0)),
                  pl.BlockSpec(memory_space=pl.ANY),
                  pl.BlockSpec((tt, TOP_K), lambda i, *_: (i, 0)),
                  pl.BlockSpec((1, D_MODEL), lambda i, *_: (0, 0))],
        out_specs=[pl.BlockSpec((tt, D_MODEL), lambda i, *_: (jnp.minimum(i, nblk_p - 1), 0)),
                   pl.BlockSpec((tt, D_MODEL), lambda i, *_: (jnp.maximum(i - nblk_p, 0), 0))],
        scratch_shapes=[pltpu.VMEM((tt * TOP_K * ROW_SLAB, LANES), F32),
                        pltpu.VMEM((tt * TOP_K * ROW_SLAB, LANES), F32),
                        pltpu.SemaphoreType.DMA((2,))])
    return pl.pallas_call(
        functools.partial(_combine_kernel, nblk_p=nblk_p, n_tok=n),
        grid_spec=grid_spec,
        out_shape=[jax.ShapeDtypeStruct((n_p, D_MODEL), F32),
                   jax.ShapeDtypeStruct((n - n_p, D_MODEL), F32)],
        compiler_params=_cparams(("arbitrary",)),
        name="combine",
    )(pos8, x1, ysorted, wtok, nw)


def _route_plan(bkt, rank, cnt, n_tok):
    tm = MOE_TILE
    n_assign = n_tok * TOP_K
    nbk = MOE_PHASES * N_EXPERTS
    ntiles = n_assign // tm + nbk
    range_tok = n_tok // MOE_PHASES
    b_flat = bkt.T.reshape(n_assign)
    order = jnp.argsort(b_flat, stable=True).astype(I32)
    counts = cnt[:, 0].astype(I32)
    cstart = jnp.cumsum(counts) - counts
    tiles_b = (counts + tm - 1) // tm
    tend = jnp.cumsum(tiles_b)
    tstart = tend - tiles_b
    tile_id = jnp.arange(ntiles, dtype=I32)
    tbk = jnp.minimum(jnp.sum((tile_id[:, None] >= tend[None, :]).astype(I32), axis=1), nbk - 1)
    onehot = (tbk[:, None] == jnp.arange(nbk, dtype=I32)[None, :]).astype(I32)
    pick = lambda v: jnp.sum(onehot * v[None, :], axis=1)
    done = (tile_id - pick(tstart)) * tm
    tnv = jnp.where(tile_id < tend[-1], jnp.clip(pick(counts) - done, 0, tm), 0)
    tsrc = jnp.where(tnv > 0, pick(cstart) + done, 0)
    texp = tbk % N_EXPERTS
    tph = tbk // N_EXPERTS
    nonempty = counts > 0
    bslot = (jnp.cumsum(nonempty.astype(I32)) - 1) % 2
    bidx = jnp.where(nonempty, jnp.arange(nbk, dtype=I32), nbk)
    nxt = jnp.concatenate([lax.cummin(bidx[::-1])[::-1][1:], jnp.full((1,), nbk, I32)])
    bnext = jnp.where(nxt < nbk, nxt % N_EXPERTS, -1)
    tfirst = jnp.logical_and(tnv > 0, done == 0).astype(I32)
    tslot = pick(bslot)
    tnext = pick(bnext)
    otok8 = jnp.concatenate([((order // TOP_K) % range_tok) * ROW_SLAB, jnp.zeros((tm,), I32)])
    plan = tuple(a.astype(I32) for a in (texp, tph, tsrc, tnv, tfirst, tslot, tnext, otok8))
    first = jnp.sum((bkt[:, :, None] == jnp.arange(nbk, dtype=I32)[None, None, :]).astype(I32)
                    * (tstart * tm)[None, None, :], axis=2)
    pos8 = ((first + rank) * ROW_SLAB).reshape(n_assign)
    return plan, pos8.astype(I32)


def _block_diag(m):
    g, a, b = m.shape
    eye = jnp.eye(g, dtype=m.dtype)
    return (eye[:, None, :, None] * m[:, :, None, :]).reshape(g * a, g * b)


def kernel(x_prompt, x_sample, state_conv, state_delta, state_ssm_re, state_ssm_im, norm_mix_w, w_in, conv_w, a_log, dt_bias, head_norm_w, w_a_up, s5_lambda_re, s5_lambda_im, s5_log_step, s5_b_re, s5_b_im, s5_c_re, s5_c_im, s5_d, w_glu, w_b_up, w_o, norm_ffn_w, w_router_coarse, w_router_fine, w_expert_up, w_expert_down, norm_final_w):
    bp, tp, _ = x_prompt.shape
    bs, ts, _ = x_sample.shape
    n_p = bp * tp
    n_s = bs * ts
    n_tok = n_p + n_s
    l = 0

    w = w_in[l].astype(BF16)
    c_ab = W1_COLS + 2 * DN_HEADS
    w_parts = (w[:, :W1_COLS], w[:, c_ab:],
               jnp.concatenate([w[:, W1_COLS:c_ab], jnp.zeros((D_MODEL, LANES - 2 * DN_HEADS), BF16)], axis=1))
    nw_mix = norm_mix_w[l].reshape(1, D_MODEL)
    pad8 = lambda v: jnp.concatenate([v, jnp.zeros((LANES - DN_HEADS,), F32)]).reshape(1, LANES)
    gate_p = jnp.concatenate([pad8(a_log[l]), pad8(dt_bias[l])], axis=0)
    seg = _block_diag(jnp.ones((DN_HEADS, DN_HEAD_DIM, DN_HEAD_DIM), BF16))
    chan_rows = lambda b: jnp.swapaxes(b, 1, 2).reshape(S5_WIDTH, S5_STATE)
    state_rows = lambda c: jnp.tile(jnp.swapaxes(c, 1, 2).reshape(S5_FLAT, S5_GROUP_CH),
                                    (1, LANES // S5_GROUP_CH))
    lam = jnp.concatenate([s5_lambda_re[l].reshape(1, S5_FLAT), s5_lambda_im[l].reshape(1, S5_FLAT),
                           jnp.repeat(s5_log_step[l], S5_STATE).reshape(1, S5_FLAT),
                           jnp.zeros((SUBLANES - 3, S5_FLAT), F32)], axis=0)
    s5_params = (chan_rows(s5_b_re[l]), chan_rows(s5_b_im[l]), lam,
                 state_rows(s5_c_re[l]), state_rows(s5_c_im[l]), s5_d[l].reshape(1, S5_WIDTH))
    hw = jnp.tile(head_norm_w[l], DN_HEADS).reshape(1, DN_WIDTH)
    wr = jnp.concatenate([w_router_fine[l].T, w_router_coarse[l].T,
                          jnp.zeros((ROUTER_ROWS - N_EXPERTS - MOE_GROUPS, D_MODEL), F32)], axis=0)
    pm_weights = (hw, seg, w_a_up[l].astype(BF16), w_glu[l].astype(BF16), w_b_up[l].astype(BF16),
                  w_o[l].astype(BF16), norm_ffn_w[l].reshape(1, D_MODEL), wr)

    xp2 = x_prompt.reshape(n_p, D_MODEL)
    q_p, k_p, v_p, gates_p, conv_p, z_p, u_p, ga_p, gb_p = _inprep(
        xp2, nw_mix, w_parts, jnp.zeros((bp, SUBLANES, QKV_DIM), F32), conv_w[l], gate_p, seg, bp, 1)
    o_p, delta_p = _delta_prompt(q_p, k_p, v_p, gates_p, bp)
    ys_p, h_p = _s5(u_p, s5_params, jnp.zeros((bp, 2 * S5_FLAT), F32), bp, True)

    xs2 = jnp.swapaxes(x_sample, 0, 1).reshape(n_s, D_MODEL)
    cinit_s = jnp.swapaxes(state_conv[l], 0, 1).reshape(1, (CONV_W - 1) * bs, QKV_DIM)
    q_s, k_s, v_s, gate_s, conv_s, z_s, u_s, ga_s, gb_s = _inprep(
        xs2, nw_mix, w_parts, cinit_s, conv_w[l], gate_p, seg, 1, bs)
    s0t = jnp.transpose(state_delta[l], (1, 2, 3, 0)).reshape(DN_HEADS * DN_HEAD_DIM * DN_HEAD_DIM, bs)
    o_s, delta_st = _delta_sample(q_s, k_s, v_s, gate_s, s0t, bs, ts)
    delta_s = jnp.transpose(delta_st.reshape(DN_HEADS, DN_HEAD_DIM, DN_HEAD_DIM, bs), (3, 0, 1, 2))
    h0_s = jnp.concatenate([state_ssm_re[l].reshape(bs, S5_FLAT), state_ssm_im[l].reshape(bs, S5_FLAT)], axis=1)
    ys_s, h_s = _s5(u_s, s5_params, h0_s, bs, False)
    x1, hn, bkt, rank, rw, cnt = _postmix((xp2, o_p, z_p, ys_p, ga_p, gb_p), (xs2, o_s, z_s, ys_s, ga_s, gb_s),
                                          pm_weights, bp)

    plan, pos8 = _route_plan(bkt, rank, cnt, n_tok)
    ysorted = _moe(hn, w_expert_up[l], w_expert_down[l], plan)
    y_p, y_s = _combine(x1, ysorted, pos8, rw.T, norm_final_w.reshape(1, D_MODEL), n_p)

    y_prompt = y_p.reshape(bp, tp, D_MODEL)
    y_sample = jnp.swapaxes(y_s.reshape(ts, bs, D_MODEL), 0, 1)
    conv_sample = jnp.swapaxes(conv_s.reshape(CONV_W - 1, bs, QKV_DIM), 0, 1)
    return (y_prompt, y_sample,
            conv_p[None], delta_p[None],
            h_p[:, :S5_FLAT].reshape(1, bp, S5_GROUPS, S5_STATE), h_p[:, S5_FLAT:].reshape(1, bp, S5_GROUPS, S5_STATE),
            conv_sample[None], delta_s[None],
            h_s[:, :S5_FLAT].reshape(1, bs, S5_GROUPS, S5_STATE), h_s[:, S5_FLAT:].reshape(1, bs, S5_GROUPS, S5_STATE))
```

```python
import functools
import math

import jax
import jax.numpy as jnp
import numpy as np
from jax import lax
from jax.experimental import pallas as pl
from jax.experimental.pallas import tpu as pltpu

F32 = jnp.float32
BF16 = jnp.bfloat16
I32 = jnp.int32

D_MODEL = 1024
DN_HEADS = 8
DN_HEAD_DIM = 64
DN_WIDTH = DN_HEADS * DN_HEAD_DIM
QKV_DIM = 3 * DN_WIDTH
CONV_W = 4
DN_CHUNK = 64
S5_GROUP_CH = 16
S5_WIDTH = D_MODEL // 2
S5_GROUPS = S5_WIDTH // S5_GROUP_CH
S5_STATE = 64
S5_FLAT = S5_GROUPS * S5_STATE
MOE_GROUPS = 4
EXPERTS_PER_GROUP = 8
N_EXPERTS = MOE_GROUPS * EXPERTS_PER_GROUP
TOP_K = 2
EXPERT_FF = 256
RMS_EPS = 1e-6
L2_EPS = 1e-6

LANES = 128
SUBLANES = 8
VMEM_LIMIT = 56 * 1024 * 1024

W1_COLS = QKV_DIM + DN_WIDTH
W2_COLS = S5_WIDTH + 2 * D_MODEL

ROW_TILE = 512
INPREP_PARTS = 2
POSTMIX_PARTS = 2
MOE_TILE = 256
MOE_PHASES = 2
COMBINE_TILE = 256
DMA_QUEUES = 2
DELTA_SUBCHUNKS = 4
S5_SUPER = 2
S5_SCAN_SPLIT = 2
ROUTER_ROWS = 40


def _mm(a, b):
    return jnp.dot(a.astype(BF16), b.astype(BF16), preferred_element_type=F32)


def _mm_nt(a, b):
    return lax.dot_general(a.astype(BF16), b.astype(BF16), (((1,), (1,)), ((), ())),
                           preferred_element_type=F32)


def _sigmoid(x):
    return 0.5 * jnp.tanh(0.5 * x) + 0.5


def _cparams(sem):
    return pltpu.CompilerParams(dimension_semantics=sem, vmem_limit_bytes=VMEM_LIMIT)


ROW_SLAB = D_MODEL // LANES


def _slab_load(ref, rows, first=0, pitch=ROW_SLAB):
    return jnp.concatenate([ref[pl.ds(first + j, rows, stride=pitch), :] for j in range(ROW_SLAB)], axis=1)


def _slab_store(ref, x, first=0):
    for j in range(ROW_SLAB):
        ref[pl.ds(first * ROW_SLAB + j, x.shape[0], stride=ROW_SLAB), :] = x[:, j * LANES:(j + 1) * LANES]


def _softplus(x):
    return jnp.maximum(x, 0.0) + jnp.log1p(jnp.exp(-jnp.abs(x)))


def _inprep_kernel(x_ref, nw_ref, w1_ref, w2_ref, wab_ref, cinit_ref, cw_ref, gp_ref, seg_ref,
                   q_ref, k_ref, v_ref, gate_ref, cnew_ref, z_ref, u_ref, ga_ref, gb_ref, xp_ref,
                   *, shift, rc, rows):
    @pl.when(pl.program_id(1) == 0)
    def _():
        xp_ref[0:rc, :] = cinit_ref[0]

    seg = seg_ref[...]
    pr = rows // INPREP_PARTS

    def part_stages(part):
        rs = slice(part * pr, (part + 1) * pr)
        x = x_ref[rs, :]
        h = x * lax.rsqrt(jnp.mean(x * x, axis=-1, keepdims=True) + RMS_EPS) * nw_ref[...]
        hb = h.astype(BF16)

        def proj(w_ref, lo, hi):
            return jnp.dot(hb, w_ref[:, lo:hi], preferred_element_type=F32)

        xp_ref[rc + part * pr:rc + (part + 1) * pr, :] = proj(w1_ref, 0, QKV_DIM)
        ab = proj(wab_ref, 0, LANES)
        yield
        z_ref[rs, :] = proj(w1_ref, QKV_DIM, W1_COLS).astype(z_ref.dtype)
        u_ref[rs, :] = proj(w2_ref, 0, S5_WIDTH).astype(u_ref.dtype)
        acc = None
        for i in range(CONV_W):
            lo = rc + part * pr + (i - (CONV_W - 1)) * shift
            term = xp_ref[lo:lo + pr, :] * cw_ref[i:i + 1, :]
            acc = term if acc is None else acc + term
        y = acc * _sigmoid(acc)
        yield
        ga_ref[rs, :] = proj(w2_ref, S5_WIDTH, S5_WIDTH + D_MODEL).astype(ga_ref.dtype)
        q = y[:, 0:DN_WIDTH]
        k = y[:, DN_WIDTH:2 * DN_WIDTH]
        q_ref[rs, :] = q * lax.rsqrt(jnp.dot((q * q).astype(BF16), seg, preferred_element_type=F32) + L2_EPS)
        k_ref[rs, :] = k * lax.rsqrt(jnp.dot((k * k).astype(BF16), seg, preferred_element_type=F32) + L2_EPS)
        v_ref[rs, :] = y[:, 2 * DN_WIDTH:]
        yield
        gb_ref[rs, :] = proj(w2_ref, S5_WIDTH + D_MODEL, W2_COLS).astype(gb_ref.dtype)
        g = -jnp.exp(gp_ref[0:1, :]) * _softplus(ab + gp_ref[1:2, :])
        beta = _sigmoid(ab)
        lane = lax.broadcasted_iota(I32, ab.shape, 1)
        gate_ref[rs, :] = jnp.where(lane < DN_HEADS, g, beta)

    live = []
    pending = [part_stages(p) for p in range(INPREP_PARTS)]
    while live or pending:
        if pending:
            live.append(pending.pop(0))
        live = [g for g in live if next(g, StopIteration) is not StopIteration]

    keep = (CONV_W - 1) * shift
    cnew_ref[0] = xp_ref[rc + rows - keep:rc + rows, :]
    xp_ref[0:rc, :] = xp_ref[rows:rows + rc, :]


def _inprep(x2d, nw, w_parts, cinit, conv_w, gate_p, seg, nb, shift):
    n = x2d.shape[0]
    r = n // nb
    rows = min(ROW_TILE, r)
    nt = r // rows
    rc = cinit.shape[1]
    keep = (CONV_W - 1) * shift
    row = lambda b, i: (b * nt + i, 0)
    const = lambda b, i: (0, 0)
    kern = functools.partial(_inprep_kernel, shift=shift, rc=rc, rows=rows)
    outs = pl.pallas_call(
        kern,
        grid=(nb, nt),
        in_specs=[pl.BlockSpec((rows, D_MODEL), row),
                  pl.BlockSpec((1, D_MODEL), const),
                  pl.BlockSpec((D_MODEL, W1_COLS), const),
                  pl.BlockSpec((D_MODEL, W2_COLS), const),
                  pl.BlockSpec((D_MODEL, LANES), const),
                  pl.BlockSpec((1, rc, QKV_DIM), lambda b, i: (b, 0, 0)),
                  pl.BlockSpec((CONV_W, QKV_DIM), const),
                  pl.BlockSpec((2, LANES), const),
                  pl.BlockSpec((DN_WIDTH, DN_WIDTH), const)],
        out_specs=[pl.BlockSpec((rows, DN_WIDTH), row),
                   pl.BlockSpec((rows, DN_WIDTH), row),
                   pl.BlockSpec((rows, DN_WIDTH), row),
                   pl.BlockSpec((rows, LANES), row),
                   pl.BlockSpec((1, keep, QKV_DIM), lambda b, i: (b, 0, 0)),
                   pl.BlockSpec((rows, DN_WIDTH), row),
                   pl.BlockSpec((rows, S5_WIDTH), lambda b, i: (i, b)),
                   pl.BlockSpec((rows, D_MODEL), row),
                   pl.BlockSpec((rows, D_MODEL), row)],
        out_shape=[jax.ShapeDtypeStruct((n, DN_WIDTH), F32),
                   jax.ShapeDtypeStruct((n, DN_WIDTH), F32),
                   jax.ShapeDtypeStruct((n, DN_WIDTH), F32),
                   jax.ShapeDtypeStruct((n, LANES), F32),
                   jax.ShapeDtypeStruct((nb, keep, QKV_DIM), F32),
                   jax.ShapeDtypeStruct((n, DN_WIDTH), BF16),
                   jax.ShapeDtypeStruct((r, nb * S5_WIDTH), BF16),
                   jax.ShapeDtypeStruct((n, D_MODEL), BF16),
                   jax.ShapeDtypeStruct((n, D_MODEL), BF16)],
        scratch_shapes=[pltpu.VMEM((rc + rows, QKV_DIM), F32)],
        compiler_params=_cparams(("arbitrary", "arbitrary")),
        name="inprep",
    )(x2d, nw, *w_parts, cinit, conv_w, gate_p, seg)
    return outs


def _delta_home(low, h, x, other=0.0):
    return jnp.where(low, x, other) if h % 2 == 0 else jnp.where(low, other, x)


def _delta_prepare(q_ref, k_ref, v_ref, gate_ref, tril_ref, bufs, *, nsub):
    sol_buf, wq_buf, qk_buf, kdec_buf, dl_buf = bufs
    c = DN_CHUNK
    dk = DN_HEAD_DIM

    def home(h, x, other=0.0):
        return _delta_home(low, h, x, other)

    rowi2 = lax.broadcasted_iota(I32, (c, 2 * c), 0)
    lane2 = lax.broadcasted_iota(I32, (c, 2 * c), 1)
    coli2 = lane2 & (c - 1)
    causal2 = rowi2 >= coli2
    strict2 = rowi2 > coli2
    low = lane2 < dk
    tril = tril_ref[...]
    pairs = [(j, h) for j in range(nsub) for h in range(DN_HEADS)]
    units = [(j, pr) for j in range(nsub) for pr in range(DN_HEADS // 2)]
    rows = [slice(j * c, (j + 1) * c) for j in range(nsub)]
    gate = [gate_ref[rows[j], :] for j in range(nsub)]
    gc_all = [_split3_dot_left(tril, gate[j]) for j in range(nsub)]
    gc_t = [gc_all[j].T for j in range(nsub)]

    def block(ref, j, pr):
        return ref[rows[j], pr * LANES:(pr + 1) * LANES]

    gfull = {(j, h): jnp.broadcast_to(gc_all[j][:, h:h + 1], (c, 2 * c)) for j, h in pairs}
    g2 = {(j, pr): jnp.where(low, gfull[j, 2 * pr], gfull[j, 2 * pr + 1]) for j, pr in units}
    b2 = {(j, pr): jnp.where(low, gate[j][:, DN_HEADS + 2 * pr:DN_HEADS + 2 * pr + 1],
                             gate[j][:, DN_HEADS + 2 * pr + 1:DN_HEADS + 2 * pr + 2]) for j, pr in units}
    kp = {u: block(k_ref, *u) for u in units}
    qp = {u: block(q_ref, *u) * (dk ** -0.5) for u in units}
    egc2 = {u: jnp.exp(g2[u]) for u in units}
    kb2 = {u: kp[u] * b2[u] for u in units}
    vb2 = {u: block(v_ref, *u) * b2[u] for u in units}
    kw2s = {u: pltpu.roll(kb2[u] * egc2[u], dk, axis=1) for u in units}
    qd2 = {u: qp[u] * egc2[u] for u in units}
    glast2 = {u: g2[u][c - 1:c, :] for u in units}
    kdec_t2 = {u: (kp[u] * jnp.exp(glast2[u] - g2[u])).T for u in units}
    dlast2 = {u: jnp.exp(glast2[u]) for u in units}
    kk = {u: jnp.concatenate([kp[u], kp[u]], axis=0) for u in units}
    yield

    grow2 = {(j, h): jnp.concatenate([gc_t[j][h:h + 1, :], gc_t[j][h:h + 1, :]], axis=1) for j, h in pairs}
    decay = {p: jnp.where(causal2, jnp.exp(jnp.where(causal2, gfull[p] - grow2[p], 0.0)), 0.0) for p in pairs}
    gram = {(j, h): _mm_nt(jnp.concatenate([home(h, kb2[j, h // 2]), home(h, qp[j, h // 2])], axis=0), kk[j, h // 2])
            for j, h in pairs}
    mat = {p: jnp.where(strict2, gram[p][:c] * decay[p], 0.0).astype(BF16) for p in pairs}
    qk = {p: jnp.where(causal2, gram[p][c:] * decay[p], 0.0) for p in pairs}
    sol = {(j, h): home(h, vb2[j, h // 2], kw2s[j, h // 2]) for j, h in pairs}
    yield
    levels = int(math.log2(c))
    zeros2 = jnp.zeros((c, 2 * c), BF16)
    for lvl in range(levels):
        hi = {p: sol[p].astype(BF16) for p in pairs}
        lo = {p: (sol[p] - hi[p].astype(F32)).astype(BF16) for p in pairs}
        if lvl < levels - 1:
            y = {p: jnp.dot(mat[p], jnp.concatenate([jnp.concatenate([hi[p], mat[p]], axis=1),
                                                     jnp.concatenate([lo[p], zeros2], axis=1)], axis=0),
                            preferred_element_type=F32) for p in pairs}
            mat = {p: y[p][:, 2 * dk:].astype(BF16) for p in pairs}
            upd = {p: y[p][:, :2 * dk] for p in pairs}
        else:
            upd = {p: jnp.dot(mat[p], jnp.concatenate([hi[p], lo[p]], axis=0), preferred_element_type=F32)
                   for p in pairs}
        sol = {p: (sol[p] - upd[p]) if lvl == 0 else (sol[p] + upd[p]) for p in pairs}
        yield
    for j, h in pairs:
        n = j * DN_HEADS + h
        sol_buf[n] = sol[j, h]
        wq_buf[n] = jnp.concatenate([home(h, 0.0, sol[j, h]), home(h, qd2[j, h // 2])], axis=0).astype(BF16)
        qk_buf[n] = qk[j, h].astype(BF16)
    for j, pr in units:
        n = j * (DN_HEADS // 2) + pr
        kdec_buf[n] = kdec_t2[j, pr].astype(BF16)
        dl_buf[n] = jnp.broadcast_to(dlast2[j, pr], (SUBLANES, LANES))


def _delta_apply(bufs, o_ref, s_ref, *, nsub):
    sol_buf, wq_buf, qk_buf, kdec_buf, dl_buf = bufs
    c = DN_CHUNK
    dk = DN_HEAD_DIM
    heads = range(DN_HEADS)
    low = lax.broadcasted_iota(I32, (c, 2 * c), 1) < dk
    s = [s_ref[h] for h in heads]
    for j in range(nsub):
        ws, v_new, o_h = [], [], []
        for h in heads:
            n = j * DN_HEADS + h
            ws.append(jnp.dot(wq_buf[n], jnp.concatenate([s[h], s[h]], axis=0).astype(BF16),
                              preferred_element_type=F32))
        yield
        for h in heads:
            v_new.append(sol_buf[j * DN_HEADS + h] - ws[h][:c])
        for h in heads:
            o_h.append(ws[h][c:] + jnp.dot(qk_buf[j * DN_HEADS + h][:, :c], v_new[h].astype(BF16),
                                           preferred_element_type=F32))
        for pr in range(DN_HEADS // 2):
            o_ref[j * c:(j + 1) * c, pr * LANES:(pr + 1) * LANES] = jnp.where(low, o_h[2 * pr], o_h[2 * pr + 1])
        nxt = []
        for h in heads:
            u = j * (DN_HEADS // 2) + h // 2
            kdt = kdec_buf[u][(h % 2) * dk:(h % 2 + 1) * dk, :]
            d = dl_buf[u][0:1, :]
            nxt.append(_delta_home(low, h, s[h] * d + jnp.dot(kdt, v_new[h].astype(BF16),
                                                               preferred_element_type=F32)))
        s = nxt
        yield
    for h in heads:
        s_ref[h] = s[h]


def _delta_chunk_kernel(q_ref, k_ref, v_ref, gate_ref, tril_ref, o_ref, sfin_ref, s_ref, *bufs, nsub, nc):
    i = pl.program_id(0)
    half = len(bufs) // 2
    sets = (bufs[:half], bufs[half:])
    local = lax.rem(jnp.maximum(i - 1, 0), nc)

    @pl.when(i == 0)
    def _():
        for b in sets[1]:
            b[...] = jnp.zeros_like(b)

    @pl.when(local == 0)
    def _():
        s_ref[...] = jnp.zeros_like(s_ref)

    for par in range(2):
        @pl.when(lax.rem(i, 2) == par)
        def _(par=par):
            parts = [_delta_prepare(q_ref, k_ref, v_ref, gate_ref, tril_ref, sets[par], nsub=nsub),
                     _delta_apply(sets[1 - par], o_ref, s_ref, nsub=nsub)]
            while parts:
                parts = [g for g in parts if next(g, StopIteration) is not StopIteration]

    @pl.when(jnp.logical_and(i >= 1, local == nc - 1))
    def _():
        dk = DN_HEAD_DIM
        for h in range(DN_HEADS):
            sfin_ref[0, h] = s_ref[h][:, (h % 2) * dk:(h % 2 + 1) * dk]


def _split3_dot_left(b01, a):
    a1 = a.astype(BF16)
    r1 = a - a1.astype(F32)
    a2 = r1.astype(BF16)
    a3 = (r1 - a2.astype(F32)).astype(BF16)
    out = jnp.dot(b01, a3, preferred_element_type=F32)
    out = out + jnp.dot(b01, a2, preferred_element_type=F32)
    return out + jnp.dot(b01, a1, preferred_element_type=F32)


def _delta_prompt(q, k, v, gate, nb):
    n = q.shape[0]
    t = n // nb
    c = DN_CHUNK
    nsub = DELTA_SUBCHUNKS
    rows = nsub * c
    nc = t // rows
    nblk = nb * nc
    row_in = lambda i: (jnp.minimum(i, nblk - 1), 0)
    row_out = lambda i: (jnp.maximum(i - 1, 0), 0)
    tril = jnp.tril(jnp.ones((c, c), F32)).astype(BF16)
    nh = nsub * DN_HEADS
    npair = nsub * DN_HEADS // 2
    buf_set = [pltpu.VMEM((nh, c, 2 * DN_HEAD_DIM), F32),
               pltpu.VMEM((nh, 2 * c, 2 * DN_HEAD_DIM), BF16),
               pltpu.VMEM((nh, c, 2 * c), BF16),
               pltpu.VMEM((npair, 2 * DN_HEAD_DIM, c), BF16),
               pltpu.VMEM((npair, SUBLANES, LANES), F32)]
    return pl.pallas_call(
        functools.partial(_delta_chunk_kernel, nsub=nsub, nc=nc),
        grid=(nblk + 1,),
        in_specs=[pl.BlockSpec((rows, DN_WIDTH), row_in),
                  pl.BlockSpec((rows, DN_WIDTH), row_in),
                  pl.BlockSpec((rows, DN_WIDTH), row_in),
                  pl.BlockSpec((rows, LANES), row_in),
                  pl.BlockSpec((c, c), lambda i: (0, 0))],
        out_specs=[pl.BlockSpec((rows, DN_WIDTH), row_out),
                   pl.BlockSpec((1, DN_HEADS, DN_HEAD_DIM, DN_HEAD_DIM),
                                lambda i: (jnp.maximum(i - 1, 0) // nc, 0, 0, 0))],
        out_shape=[jax.ShapeDtypeStruct((n, DN_WIDTH), F32),
                   jax.ShapeDtypeStruct((nb, DN_HEADS, DN_HEAD_DIM, DN_HEAD_DIM), F32)],
        scratch_shapes=[pltpu.VMEM((DN_HEADS, DN_HEAD_DIM, 2 * DN_HEAD_DIM), F32)] + buf_set + buf_set,
        compiler_params=_cparams(("arbitrary",)),
        name="delta_prompt",
    )(q, k, v, gate, tril)


def _delta_step_kernel(q_ref, k_ref, v_ref, gate_ref, s0_ref, o_ref, s_ref, kt_ref, qt_ref, gt_ref, *, nt, nb):
    dk = DN_HEAD_DIM
    p = pl.program_id(0)
    for t in range(nt):
        rs = slice(t * nb, (t + 1) * nb)
        gt_ref[...] = gate_ref[rs, :].T
        kt_ref[...] = k_ref[rs, :].T
        qt_ref[...] = (q_ref[rs, :] * (dk ** -0.5)).T
        vt = v_ref[rs, :].T
        src = s0_ref if t == 0 else s_ref
        o_heads = []
        for j in range(2):
            a = jnp.exp(gt_ref[pl.ds(2 * p + j, 1), :])
            beta = gt_ref[pl.ds(2 * p + j + DN_HEADS, 1), :]
            base = j * dk * dk

            def k_dot_s(d, acc, j=j, base=base, src=src):
                sd = src[pl.ds(pl.multiple_of(base + d * dk, dk), dk), :]
                return acc + kt_ref[pl.ds(j * dk + d, 1), :] * sd

            ks = lax.fori_loop(0, dk, k_dot_s, jnp.zeros((dk, nb), F32), unroll=4)
            delta = beta * (vt[j * dk:(j + 1) * dk, :] - a * ks)

            def update(d, acc, j=j, base=base, src=src, a=a, delta=delta):
                r0 = pl.multiple_of(base + d * dk, dk)
                sn = a * src[pl.ds(r0, dk), :] + kt_ref[pl.ds(j * dk + d, 1), :] * delta
                s_ref[pl.ds(r0, dk), :] = sn
                return acc + qt_ref[pl.ds(j * dk + d, 1), :] * sn

            o_heads.append(lax.fori_loop(0, dk, update, jnp.zeros((dk, nb), F32), unroll=4))
        o_ref[rs, :] = jnp.concatenate(o_heads, axis=0).T


def _delta_sample(q, k, v, gate, s0t, nb, nt):
    dk = DN_HEAD_DIM
    flat = dk * dk
    n = nt * nb
    kern = functools.partial(_delta_step_kernel, nt=nt, nb=nb)
    pair = lambda p: (0, p)
    return pl.pallas_call(
        kern,
        grid=(DN_HEADS // 2,),
        in_specs=[pl.BlockSpec((n, LANES), pair),
                  pl.BlockSpec((n, LANES), pair),
                  pl.BlockSpec((n, LANES), pair),
                  pl.BlockSpec((n, LANES), lambda p: (0, 0)),
                  pl.BlockSpec((2 * flat, nb), lambda p: (p, 0))],
        out_specs=[pl.BlockSpec((n, LANES), pair),
                   pl.BlockSpec((2 * flat, nb), lambda p: (p, 0))],
        out_shape=[jax.ShapeDtypeStruct((n, DN_WIDTH), F32),
                   jax.ShapeDtypeStruct((DN_HEADS * flat, nb), F32)],
        scratch_shapes=[pltpu.VMEM((LANES, nb), F32),
                        pltpu.VMEM((LANES, nb), F32),
                        pltpu.VMEM((LANES, nb), F32)],
        compiler_params=_cparams(("arbitrary",)),
        name="delta_sample",
    )(q, k, v, gate, s0t)


def _s5_kernel(u_ref, btre_ref, btim_ref, lam_ref, ctre_ref, ctim_ref, d_ref, h0_ref, y_ref, hfin_ref,
               bw_ref, c_ref, ab_ref, x_ref, h_ref, ru_ref, ry_ref, *, nb, tt, wide):
    p2 = S5_FLAT

    @pl.when(pl.program_id(0) == 0)
    def _():
        lr = lam_ref[0:1, :]
        li = lam_ref[1:2, :]
        dt = jnp.exp(lam_ref[2:3, :])
        mag = jnp.exp(lr * dt)
        ab_re = mag * jnp.cos(li * dt)
        ab_im = mag * jnp.sin(li * dt)
        den = lr * lr + li * li
        nr = ab_re - 1.0
        ni = ab_im
        f_re = (nr * lr + ni * li) / den
        f_im = (ni * lr - nr * li) / den
        ab_ref[0:1, :] = ab_re
        ab_ref[1:2, :] = ab_im
        gpl = LANES // S5_STATE
        ch_g = lax.broadcasted_iota(I32, (S5_WIDTH, LANES), 0) // S5_GROUP_CH
        lane_g = lax.broadcasted_iota(I32, (S5_WIDTH, LANES), 1) // S5_STATE
        bre2 = jnp.concatenate([btre_ref[...]] * gpl, axis=1)
        bim2 = jnp.concatenate([btim_ref[...]] * gpl, axis=1)
        for j in range(p2 // LANES):
            cols = slice(j * LANES, (j + 1) * LANES)
            own = ch_g == gpl * j + lane_g
            bre = jnp.where(own, bre2, 0.0)
            bim = jnp.where(own, bim2, 0.0)
            bw_ref[:, cols] = (bre * f_re[:, cols] - bim * f_im[:, cols]).astype(BF16)
            bw_ref[:, p2 + j * LANES:p2 + (j + 1) * LANES] = (bim * f_re[:, cols] + bre * f_im[:, cols]).astype(BF16)
        cpl = LANES // S5_GROUP_CH
        st_g = lax.broadcasted_iota(I32, (p2, LANES), 0) // S5_STATE
        lane_cg = lax.broadcasted_iota(I32, (p2, LANES), 1) // S5_GROUP_CH
        for j in range(S5_WIDTH // LANES):
            cols = slice(j * LANES, (j + 1) * LANES)
            own = st_g == cpl * j + lane_cg
            c_ref[0:p2, cols] = jnp.where(own, ctre_ref[...], 0.0).astype(BF16)
            c_ref[p2:2 * p2, cols] = jnp.where(own, -ctim_ref[...], 0.0).astype(BF16)
        h_ref[...] = h0_ref[...]

    nck = S5_WIDTH // LANES
    if wide:
        for b in range(nb):
            for ck in range(nck):
                lo = b * S5_WIDTH + ck * LANES
                ru_ref[ck, pl.ds(b, tt, stride=nb), :] = u_ref[:, lo:lo + LANES].astype(F32)
        u = jnp.concatenate([ru_ref[ck] for ck in range(nck)], axis=1)
    else:
        u = u_ref[...].astype(F32)
    ub = u.astype(BF16)
    cw = S5_WIDTH // S5_SUPER
    sw = S5_FLAT // S5_SUPER
    for part in (0, p2):
        for b in range(S5_SUPER):
            x_ref[:, part + b * sw:part + (b + 1) * sw] = jnp.dot(
                ub[:, b * cw:(b + 1) * cw], bw_ref[b * cw:(b + 1) * cw, part + b * sw:part + (b + 1) * sw],
                preferred_element_type=F32)
    a_re = ab_ref[0:1, :]
    a_im = ab_ref[1:2, :]

    if nb == SUBLANES:
        wsl = p2 // S5_SCAN_SPLIT
        for sp in range(S5_SCAN_SPLIT):
            c0 = sp * wsl
            are = jnp.broadcast_to(a_re[:, c0:c0 + wsl], (nb, wsl))
            aim = jnp.broadcast_to(a_im[:, c0:c0 + wsl], (nb, wsl))

            def step(t, carry, c0=c0, are=are, aim=aim):
                hr, hi = carry
                r0 = pl.multiple_of(t * nb, nb)
                nr = are * hr - aim * hi + x_ref[pl.ds(r0, nb), c0:c0 + wsl]
                ni = are * hi + aim * hr + x_ref[pl.ds(r0, nb), p2 + c0:p2 + c0 + wsl]
                x_ref[pl.ds(r0, nb), c0:c0 + wsl] = nr
                x_ref[pl.ds(r0, nb), p2 + c0:p2 + c0 + wsl] = ni
                return nr, ni

            hr, hi = lax.fori_loop(0, tt, step, (h_ref[:, c0:c0 + wsl], h_ref[:, p2 + c0:p2 + c0 + wsl]),
                                   unroll=2)
            h_ref[:, c0:c0 + wsl] = hr
            h_ref[:, p2 + c0:p2 + c0 + wsl] = hi
    else:
        for t in range(tt):
            rs = slice(t * nb, (t + 1) * nb)
            hr = h_ref[:, 0:p2]
            hi = h_ref[:, p2:2 * p2]
            nr = a_re * hr - a_im * hi + x_ref[rs, 0:p2]
            ni = a_re * hi + a_im * hr + x_ref[rs, p2:2 * p2]
            h_ref[:, 0:p2] = nr
            h_ref[:, p2:2 * p2] = ni
            x_ref[rs, 0:p2] = nr
            x_ref[rs, p2:2 * p2] = ni

    for b in range(S5_SUPER):
        cols = slice(b * cw, (b + 1) * cw)
        y = None
        for part in (0, p2):
            rws = slice(part + b * sw, part + (b + 1) * sw)
            term = jnp.dot(x_ref[:, rws].astype(BF16), c_ref[rws, cols], preferred_element_type=F32)
            y = term if y is None else y + term
        if wide:
            skip = jnp.concatenate([ru_ref[b * (cw // LANES) + ck] for ck in range(cw // LANES)], axis=1)
        else:
            skip = u[:, cols]
        y = y + d_ref[:, cols] * skip
        if wide:
            for ck in range(cw // LANES):
                ry_ref[b * (cw // LANES) + ck] = y[:, ck * LANES:(ck + 1) * LANES]
        else:
            y_ref[:, cols] = y
    if wide:
        for b in range(nb):
            for ck in range(nck):
                lo = b * S5_WIDTH + ck * LANES
                y_ref[:, lo:lo + LANES] = ry_ref[ck, pl.ds(b, tt, stride=nb), :]
    hfin_ref[...] = h_ref[...]


def _s5(u, params, h0, nb, wide):
    btre, btim, lam, ctre, ctim, dvec = params
    t = u.shape[0] if wide else u.shape[0] // nb
    tt = min(ROW_TILE // nb, t)
    rows = tt * nb
    const = lambda i: (0, 0)
    kern = functools.partial(_s5_kernel, nb=nb, tt=tt, wide=wide)
    io_block = (tt, nb * S5_WIDTH) if wide else (rows, S5_WIDTH)
    return pl.pallas_call(
        kern,
        grid=(t // tt,),
        in_specs=[pl.BlockSpec(io_block, lambda i: (i, 0)),
                  pl.BlockSpec((S5_WIDTH, S5_STATE), const),
                  pl.BlockSpec((S5_WIDTH, S5_STATE), const),
                  pl.BlockSpec((SUBLANES, S5_FLAT), const),
                  pl.BlockSpec((S5_FLAT, LANES), const),
                  pl.BlockSpec((S5_FLAT, LANES), const),
                  pl.BlockSpec((1, S5_WIDTH), const),
                  pl.BlockSpec((nb, 2 * S5_FLAT), const)],
        out_specs=[pl.BlockSpec(io_block, lambda i: (i, 0)),
                   pl.BlockSpec((nb, 2 * S5_FLAT), const)],
        out_shape=[jax.ShapeDtypeStruct(u.shape, F32),
                   jax.ShapeDtypeStruct((nb, 2 * S5_FLAT), F32)],
        scratch_shapes=[pltpu.VMEM((S5_WIDTH, 2 * S5_FLAT), BF16),
                        pltpu.VMEM((2 * S5_FLAT, S5_WIDTH), BF16),
                        pltpu.VMEM((SUBLANES, S5_FLAT), F32),
                        pltpu.VMEM((rows, 2 * S5_FLAT), F32),
                        pltpu.VMEM((nb, 2 * S5_FLAT), F32),
                        pltpu.VMEM((S5_WIDTH // LANES, rows, LANES), F32),
                        pltpu.VMEM((S5_WIDTH // LANES, rows, LANES), F32)],
        compiler_params=_cparams(("arbitrary",)),
        name="s5",
    )(u, btre, btim, lam, ctre, ctim, dvec, h0)


def _postmix_kernel(xp_ref, op_ref, zp_ref, ysp_ref, gap_ref, gbp_ref,
                    xs_ref, os_ref, zs_ref, yss_ref, gas_ref, gbs_ref, *rest, nblk_p, range_tok):
    carry_ref = rest[-1]

    @pl.when(pl.program_id(0) == 0)
    def _():
        carry_ref[...] = jnp.zeros_like(carry_ref)

    @pl.when(pl.program_id(0) < nblk_p)
    def _():
        _postmix_body(xp_ref, op_ref, zp_ref, ysp_ref, gap_ref, gbp_ref, *rest, range_tok=range_tok)

    @pl.when(pl.program_id(0) >= nblk_p)
    def _():
        _postmix_body(xs_ref, os_ref, zs_ref, yss_ref, gas_ref, gbs_ref, *rest, range_tok=range_tok)


def _postmix_body(x_ref, o_ref, z_ref, ys_ref, ga_ref, gb_ref, hw_ref, seg_ref, wa_ref, wglu_ref, wb_ref,
                  wo_ref, nf_ref, wr_ref, su_ref, x1_ref, hn_ref, bkt_ref, rank_ref, rw_ref, cnt_ref, carry_ref,
                  *, range_tok):
    rows = x_ref.shape[0]
    pr = rows // POSTMIX_PARTS
    parts = [_postmix_part(p, pr, x_ref, o_ref, z_ref, ys_ref, ga_ref, gb_ref, hw_ref, seg_ref, wa_ref, wglu_ref,
                           wb_ref, wo_ref, nf_ref, wr_ref, su_ref, x1_ref, hn_ref, bkt_ref, rank_ref, rw_ref,
                           carry_ref, range_tok) for p in range(POSTMIX_PARTS)]
    live = []
    while live or parts:
        if parts:
            live.append(parts.pop(0))
        live = [g for g in live if next(g, StopIteration) is not StopIteration]
    cnt_ref[...] = carry_ref[...]


def _postmix_part(part, pr, x_ref, o_ref, z_ref, ys_ref, ga_ref, gb_ref, hw_ref, seg_ref, wa_ref, wglu_ref, wb_ref,
                  wo_ref, nf_ref, wr_ref, su_ref, x1_ref, hn_ref, bkt_ref, rank_ref, rw_ref, carry_ref, range_tok):
    rs = slice(part * pr, (part + 1) * pr)
    o = o_ref[rs, :]
    ms = jnp.dot((o * o).astype(BF16), seg_ref[...], preferred_element_type=F32) * (1.0 / DN_HEAD_DIM)
    on = o * lax.rsqrt(ms + RMS_EPS) * hw_ref[...]
    z = z_ref[rs, :]
    oa = on * (z * _sigmoid(z)).astype(F32)
    yield
    y_a = _mm(oa, wa_ref[...])
    ys = jax.nn.gelu(ys_ref[rs, :])
    yield
    ys = ys * _sigmoid(_mm(ys, wglu_ref[...]))
    yield
    y_b = _mm(ys, wb_ref[...])
    mixed = _sigmoid(ga_ref[rs, :]).astype(F32) * y_a + _sigmoid(gb_ref[rs, :]).astype(F32) * y_b
    yield
    x1 = x_ref[rs, :] + _mm(mixed, wo_ref[...])
    x1_ref[rs, :] = x1
    hn = x1 * lax.rsqrt(jnp.mean(x1 * x1, axis=-1, keepdims=True) + RMS_EPS) * nf_ref[...]
    _slab_store(hn_ref, hn, part * pr)
    yield

    wr = wr_ref[...]
    w_hi = wr.astype(BF16)
    w_lo = (wr - w_hi.astype(F32)).astype(BF16)
    hn_hi = hn.astype(BF16)
    hn_lo = (hn - hn_hi.astype(F32)).astype(BF16)
    both = _mm_nt(jnp.concatenate([w_hi, w_lo], axis=0), hn_hi)
    logits = both[:ROUTER_ROWS] + both[ROUTER_ROWS:] + _mm_nt(w_hi, hn_lo)
    yield
    coarse = logits[N_EXPERTS:N_EXPERTS + MOE_GROUPS, :]
    cm = jnp.max(coarse, axis=0, keepdims=True)
    ce = jnp.exp(coarse - cm)
    pc = ce / jnp.sum(ce, axis=0, keepdims=True)
    p_sel = jnp.max(pc, axis=0, keepdims=True)
    gi = lax.broadcasted_iota(I32, pc.shape, 0)
    g_sel = jnp.min(jnp.where(pc == p_sel, gi, MOE_GROUPS), axis=0, keepdims=True)
    fine = jnp.zeros((EXPERTS_PER_GROUP, logits.shape[1]), F32)
    for g in range(MOE_GROUPS):
        fine = fine + jnp.where(g_sel == g, logits[g * EXPERTS_PER_GROUP:(g + 1) * EXPERTS_PER_GROUP, :], 0.0)
    fm = jnp.max(fine, axis=0, keepdims=True)
    fe = jnp.exp(fine - fm)
    pf = fe / jnp.sum(fe, axis=0, keepdims=True)
    ei = lax.broadcasted_iota(I32, pf.shape, 0)
    v1 = jnp.max(pf, axis=0, keepdims=True)
    i1 = jnp.min(jnp.where(pf == v1, ei, EXPERTS_PER_GROUP), axis=0, keepdims=True)
    rest = jnp.where(ei == i1, -1.0, pf)
    v2 = jnp.max(rest, axis=0, keepdims=True)
    i2 = jnp.min(jnp.where(rest == v2, ei, EXPERTS_PER_GROUP), axis=0, keepdims=True)
    tot = v1 + v2
    rw_ref[0:1, rs] = v1 / tot * p_sel
    rw_ref[1:2, rs] = v2 / tot * p_sel

    tok = pl.program_id(0) * (pr * POSTMIX_PARTS) + part * pr + lax.broadcasted_iota(I32, (1, pr), 1)
    ph = jnp.zeros((1, pr), I32)
    for r in range(1, MOE_PHASES):
        ph = ph + (tok >= r * range_tok).astype(I32)
    bsel = [ph * N_EXPERTS + g_sel * EXPERTS_PER_GROUP + ix for ix in (i1, i2)]
    bi = lax.broadcasted_iota(I32, (MOE_PHASES * N_EXPERTS, pr), 0)
    onehot = [(bi == b).astype(F32) for b in bsel]
    cnt = onehot[0] + onehot[1]
    before = carry_ref[:, 0:1] + jnp.dot(cnt.astype(BF16), su_ref[0:pr, 0:pr], preferred_element_type=F32)
    for s in range(TOP_K):
        bkt_ref[s:s + 1, rs] = bsel[s]
        rank_ref[s:s + 1, rs] = jnp.sum(onehot[s] * before, axis=0, keepdims=True).astype(I32)
    carry_ref[...] = carry_ref[...] + jnp.sum(cnt, axis=1, keepdims=True)


def _postmix(prompt, sample, weights, nb):
    n_p = prompt[0].shape[0]
    n_s = sample[0].shape[0]
    t = n_p // nb
    tt = min(ROW_TILE, t, n_s)
    nt = t // tt
    nblk_p = n_p // tt
    nblk = nblk_p + n_s // tt
    n_total = n_p + n_s
    prow = lambda i: (jnp.minimum(i, nblk_p - 1), 0)
    pys = lambda i: (jnp.minimum(i, nblk_p - 1) % nt, jnp.minimum(i, nblk_p - 1) // nt)
    srow = lambda i: (jnp.maximum(i - nblk_p, 0), 0)
    const = lambda i: (0, 0)

    def stream_specs(row, ysmap):
        return [pl.BlockSpec((tt, D_MODEL), row),
                pl.BlockSpec((tt, DN_WIDTH), row),
                pl.BlockSpec((tt, DN_WIDTH), row),
                pl.BlockSpec((tt, S5_WIDTH), ysmap),
                pl.BlockSpec((tt, D_MODEL), row),
                pl.BlockSpec((tt, D_MODEL), row)]

    weight_specs = [pl.BlockSpec((1, DN_WIDTH), const),
                    pl.BlockSpec((DN_WIDTH, DN_WIDTH), const),
                    pl.BlockSpec((DN_WIDTH, D_MODEL), const),
                    pl.BlockSpec((S5_WIDTH, S5_WIDTH), const),
                    pl.BlockSpec((S5_WIDTH, D_MODEL), const),
                    pl.BlockSpec((D_MODEL, D_MODEL), const),
                    pl.BlockSpec((1, D_MODEL), const),
                    pl.BlockSpec((ROUTER_ROWS, D_MODEL), const),
                    pl.BlockSpec((tt, tt), const)]
    xp, op, zp, ysp, gap, gbp = prompt
    nbk = MOE_PHASES * N_EXPERTS
    earlier = jnp.triu(jnp.ones((tt, tt), F32), k=1).astype(BF16)
    return pl.pallas_call(
        functools.partial(_postmix_kernel, nblk_p=nblk_p, range_tok=n_total // MOE_PHASES),
        grid=(nblk,),
        in_specs=stream_specs(prow, pys) + stream_specs(srow, srow) + weight_specs,
        out_specs=[pl.BlockSpec((tt, D_MODEL), lambda i: (i, 0)),
                   pl.BlockSpec((tt * ROW_SLAB, LANES), lambda i: (i, 0)),
                   pl.BlockSpec((TOP_K, tt), lambda i: (0, i)),
                   pl.BlockSpec((TOP_K, tt), lambda i: (0, i)),
                   pl.BlockSpec((TOP_K, tt), lambda i: (0, i)),
                   pl.BlockSpec((nbk, LANES), const)],
        out_shape=[jax.ShapeDtypeStruct((n_total, D_MODEL), F32),
                   jax.ShapeDtypeStruct((n_total * ROW_SLAB, LANES), F32),
                   jax.ShapeDtypeStruct((TOP_K, n_total), I32),
                   jax.ShapeDtypeStruct((TOP_K, n_total), I32),
                   jax.ShapeDtypeStruct((TOP_K, n_total), F32),
                   jax.ShapeDtypeStruct((nbk, LANES), F32)],
        scratch_shapes=[pltpu.VMEM((nbk, LANES), F32)],
        compiler_params=_cparams(("arbitrary",)),
        name="postmix",
    )(xp, op, zp, ysp, gap, gbp, *sample, *weights, earlier)


def _wait_slabs(buf, sem):
    pltpu.make_async_copy(buf, buf, sem).wait()


def _moe_kernel(texp_ref, tph_ref, tsrc_ref, tnv_ref, tfirst_ref, tslot_ref, tnext_ref, otok_ref,
                hn_hbm, wu_hbm, wd_hbm, y_ref, hnv, xbuf, wu_buf, wd_buf, wub, wdb, sem, wsem):
    i = pl.program_id(0)
    tm = MOE_TILE
    rs = ROW_SLAB
    nv = tnv_ref[i]
    ph = tph_ref[i]
    range_rows = hnv.shape[0]

    def weight_copies(e, sl):
        return (pltpu.make_async_copy(wu_hbm.at[e], wu_buf.at[sl], wsem.at[sl]),
                pltpu.make_async_copy(wd_hbm.at[e], wd_buf.at[sl], wsem.at[sl]))

    @pl.when(i == 0)
    def _():
        for p, c in enumerate(weight_copies(texp_ref[0], 0)):
            c.start(priority=p % DMA_QUEUES)

    @pl.when(jnp.logical_and(nv > 0, jnp.logical_or(i == 0, ph != tph_ref[jnp.maximum(i - 1, 0)])))
    def _():
        piece = range_rows // DMA_QUEUES
        loads = [pltpu.make_async_copy(hn_hbm.at[pl.ds(pl.multiple_of(ph * range_rows + p * piece, rs), piece), :],
                                       hnv.at[pl.ds(p * piece, piece), :], sem) for p in range(DMA_QUEUES)]
        for p, c in enumerate(loads):
            c.start(priority=p)
        for c in loads:
            c.wait()

    for sl in range(2):
        @pl.when(jnp.logical_and(jnp.logical_and(nv > 0, tfirst_ref[i] == 1), tslot_ref[i] == sl))
        def _():
            for c in weight_copies(texp_ref[i], sl):
                c.wait()

            @pl.when(tnext_ref[i] >= 0)
            def _():
                for p, c in enumerate(weight_copies(tnext_ref[i], 1 - sl)):
                    c.start(priority=p % DMA_QUEUES)

            wub[...] = wu_buf[sl].astype(BF16)
            wdb[...] = wd_buf[sl].astype(BF16)

    @pl.when(nv == 0)
    def _():
        y_ref[...] = jnp.zeros_like(y_ref)

    @pl.when(nv > 0)
    def _():
        src0 = tsrc_ref[i]
        for r in range(tm):
            tok8 = pl.multiple_of(otok_ref[src0 + r], rs)
            xbuf[pl.ds(r * rs, rs), :] = hnv[pl.ds(tok8, rs), :]
        x = _slab_load(xbuf, tm).astype(BF16)
        hu = jnp.dot(x, wub[...], preferred_element_type=F32)
        gate = hu[:, :EXPERT_FF]
        up = hu[:, EXPERT_FF:]
        act = gate * _sigmoid(gate) * up
        _slab_store(y_ref, jnp.dot(act.astype(BF16), wdb[...], preferred_element_type=F32))


def _moe(hn, w_up, w_down, plan):
    ntiles = plan[0].shape[0]
    grid_spec = pltpu.PrefetchScalarGridSpec(
        num_scalar_prefetch=len(plan),
        grid=(ntiles,),
        in_specs=[pl.BlockSpec(memory_space=pl.ANY),
                  pl.BlockSpec(memory_space=pl.ANY),
                  pl.BlockSpec(memory_space=pl.ANY)],
        out_specs=pl.BlockSpec((MOE_TILE * ROW_SLAB, LANES), lambda i, *_: (i, 0)),
        scratch_shapes=[pltpu.VMEM((hn.shape[0] // MOE_PHASES, LANES), F32),
                        pltpu.VMEM((MOE_TILE * ROW_SLAB, LANES), F32),
                        pltpu.VMEM((2, D_MODEL, 2 * EXPERT_FF), F32),
                        pltpu.VMEM((2, EXPERT_FF, D_MODEL), F32),
                        pltpu.VMEM((D_MODEL, 2 * EXPERT_FF), BF16),
                        pltpu.VMEM((EXPERT_FF, D_MODEL), BF16),
                        pltpu.SemaphoreType.DMA,
                        pltpu.SemaphoreType.DMA((2,))])
    return pl.pallas_call(
        _moe_kernel,
        grid_spec=grid_spec,
        out_shape=jax.ShapeDtypeStruct((ntiles * MOE_TILE * ROW_SLAB, LANES), F32),
        compiler_params=_cparams(("arbitrary",)),
        name="moe",
    )(*plan, hn, w_up, w_down)


def _combine_kernel(pos_ref, x1_ref, ys_hbm, w_ref, nw_ref, outp_ref, outs_ref,
                    ybuf0, ybuf1, sem, *, nblk_p, n_tok):
    i = pl.program_id(0)
    nsteps = pl.num_programs(0)
    tt = x1_ref.shape[0]
    rs = ROW_SLAB
    slot = lax.rem(i, 2)
    ybuf = (ybuf0, ybuf1)

    def start_gather(step, sl):
        base = step * tt
        for r in range(tt * TOP_K):
            j, s = divmod(r, TOP_K)
            p8 = pl.multiple_of(pos_ref[s * n_tok + base + j], rs)
            pltpu.make_async_copy(ys_hbm.at[pl.ds(p8, rs), :], ybuf[sl].at[pl.ds((s * tt + j) * rs, rs), :],
                                  sem.at[sl]).start(priority=r % DMA_QUEUES)

    @pl.when(i == 0)
    def _():
        start_gather(0, 0)

    for sl in range(2):
        @pl.when(slot == sl)
        def _():
            _wait_slabs(ybuf[sl], sem.at[sl])
            start_gather(jnp.minimum(i + 1, nsteps - 1), 1 - sl)
            w = w_ref[...]
            y0 = _slab_load(ybuf[sl], tt, 0)
            y1 = _slab_load(ybuf[sl], tt, tt * rs)
            x = x1_ref[...] + w[:, 0:1] * y0 + w[:, 1:2] * y1
            res = x * lax.rsqrt(jnp.mean(x * x, axis=-1, keepdims=True) + RMS_EPS) * nw_ref[...]

            @pl.when(i < nblk_p)
            def _():
                outp_ref[...] = res

            @pl.when(i >= nblk_p)
            def _():
                outs_ref[...] = res

        @pl.when(jnp.logical_and(slot == sl, i == nsteps - 1))
        def _():
            _wait_slabs(ybuf[1 - sl], sem.at[1 - sl])


def _combine(x1, ysorted, pos8, wtok, nw, n_p):
    n = x1.shape[0]
    tt = math.gcd(math.gcd(n_p, n - n_p), COMBINE_TILE)
    nblk_p = n_p // tt
    grid_spec = pltpu.PrefetchScalarGridSpec(
        num_scalar_prefetch=1,
        grid=(n // tt,),
        in_specs=[pl.BlockSpec((tt, D_MODEL), lambda i, *_: (i, 0)),
                  pl.BlockSpec(memory_space=pl.ANY),
                  pl.BlockSpec((tt, TOP_K), lambda i, *_: (i, 0)),
                  pl.BlockSpec((1, D_MODEL), lambda i, *_: (0, 0))],
        out_specs=[pl.BlockSpec((tt, D_MODEL), lambda i, *_: (jnp.minimum(i, nblk_p - 1), 0)),
                   pl.BlockSpec((tt, D_MODEL), lambda i, *_: (jnp.maximum(i - nblk_p, 0), 0))],
        scratch_shapes=[pltpu.VMEM((tt * TOP_K * ROW_SLAB, LANES), F32),
                        pltpu.VMEM((tt * TOP_K * ROW_SLAB, LANES), F32),
                        pltpu.SemaphoreType.DMA((2,))])
    return pl.pallas_call(
        functools.partial(_combine_kernel, nblk_p=nblk_p, n_tok=n),
        grid_spec=grid_spec,
        out_shape=[jax.ShapeDtypeStruct((n_p, D_MODEL), F32),
                   jax.ShapeDtypeStruct((n - n_p, D_MODEL), F32)],
        compiler_params=_cparams(("arbitrary",)),
        name="combine",
    )(pos8, x1, ysorted, wtok, nw)


def _route_plan(bkt, rank, cnt, n_tok):
    tm = MOE_TILE
    n_assign = n_tok * TOP_K
    nbk = MOE_PHASES * N_EXPERTS
    ntiles = n_assign // tm + nbk
    range_tok = n_tok // MOE_PHASES
    b_flat = bkt.T.reshape(n_assign)
    order = jnp.argsort(b_flat, stable=True).astype(I32)
    counts = cnt[:, 0].astype(I32)
    cstart = jnp.cumsum(counts) - counts
    tiles_b = (counts + tm - 1) // tm
    tend = jnp.cumsum(tiles_b)
    tstart = tend - tiles_b
    tile_id = jnp.arange(ntiles, dtype=I32)
    tbk = jnp.minimum(jnp.sum((tile_id[:, None] >= tend[None, :]).astype(I32), axis=1), nbk - 1)
    onehot = (tbk[:, None] == jnp.arange(nbk, dtype=I32)[None, :]).astype(I32)
    pick = lambda v: jnp.sum(onehot * v[None, :], axis=1)
    done = (tile_id - pick(tstart)) * tm
    tnv = jnp.where(tile_id < tend[-1], jnp.clip(pick(counts) - done, 0, tm), 0)
    tsrc = jnp.where(tnv > 0, pick(cstart) + done, 0)
    texp = tbk % N_EXPERTS
    tph = tbk // N_EXPERTS
    nonempty = counts > 0
    bslot = (jnp.cumsum(nonempty.astype(I32)) - 1) % 2
    bidx = jnp.where(nonempty, jnp.arange(nbk, dtype=I32), nbk)
    nxt = jnp.concatenate([lax.cummin(bidx[::-1])[::-1][1:], jnp.full((1,), nbk, I32)])
    bnext = jnp.where(nxt < nbk, nxt % N_EXPERTS, -1)
    tfirst = jnp.logical_and(tnv > 0, done == 0).astype(I32)
    tslot = pick(bslot)
    tnext = pick(bnext)
    otok8 = jnp.concatenate([((order // TOP_K) % range_tok) * ROW_SLAB, jnp.zeros((tm,), I32)])
    plan = tuple(a.astype(I32) for a in (texp, tph, tsrc, tnv, tfirst, tslot, tnext, otok8))
    first = jnp.sum((bkt[:, :, None] == jnp.arange(nbk, dtype=I32)[None, None, :]).astype(I32)
                    * (tstart * tm)[None, None, :], axis=2)
    pos8 = ((first + rank) * ROW_SLAB).reshape(n_assign)
    return plan, pos8.astype(I32)


def _block_diag(m):
    g, a, b = m.shape
    eye = jnp.eye(g, dtype=m.dtype)
    return (eye[:, None, :, None] * m[:, :, None, :]).reshape(g * a, g * b)


def kernel(x_prompt, x_sample, state_conv, state_delta, state_ssm_re, state_ssm_im, norm_mix_w, w_in, conv_w, a_log, dt_bias, head_norm_w, w_a_up, s5_lambda_re, s5_lambda_im, s5_log_step, s5_b_re, s5_b_im, s5_c_re, s5_c_im, s5_d, w_glu, w_b_up, w_o, norm_ffn_w, w_router_coarse, w_router_fine, w_expert_up, w_expert_down, norm_final_w):
    bp, tp, _ = x_prompt.shape
    bs, ts, _ = x_sample.shape
    n_p = bp * tp
    n_s = bs * ts
    n_tok = n_p + n_s
    l = 0

    w = w_in[l].astype(BF16)
    c_ab = W1_COLS + 2 * DN_HEADS
    w_parts = (w[:, :W1_COLS], w[:, c_ab:],
               jnp.concatenate([w[:, W1_COLS:c_ab], jnp.zeros((D_MODEL, LANES - 2 * DN_HEADS), BF16)], axis=1))
    nw_mix = norm_mix_w[l].reshape(1, D_MODEL)
    pad8 = lambda v: jnp.concatenate([v, jnp.zeros((LANES - DN_HEADS,), F32)]).reshape(1, LANES)
    gate_p = jnp.concatenate([pad8(a_log[l]), pad8(dt_bias[l])], axis=0)
    seg = _block_diag(jnp.ones((DN_HEADS, DN_HEAD_DIM, DN_HEAD_DIM), BF16))
    chan_rows = lambda b: jnp.swapaxes(b, 1, 2).reshape(S5_WIDTH, S5_STATE)
    state_rows = lambda c: jnp.tile(jnp.swapaxes(c, 1, 2).reshape(S5_FLAT, S5_GROUP_CH),
                                    (1, LANES // S5_GROUP_CH))
    lam = jnp.concatenate([s5_lambda_re[l].reshape(1, S5_FLAT), s5_lambda_im[l].reshape(1, S5_FLAT),
                           jnp.repeat(s5_log_step[l], S5_STATE).reshape(1, S5_FLAT),
                           jnp.zeros((SUBLANES - 3, S5_FLAT), F32)], axis=0)
    s5_params = (chan_rows(s5_b_re[l]), chan_rows(s5_b_im[l]), lam,
                 state_rows(s5_c_re[l]), state_rows(s5_c_im[l]), s5_d[l].reshape(1, S5_WIDTH))
    hw = jnp.tile(head_norm_w[l], DN_HEADS).reshape(1, DN_WIDTH)
    wr = jnp.concatenate([w_router_fine[l].T, w_router_coarse[l].T,
                          jnp.zeros((ROUTER_ROWS - N_EXPERTS - MOE_GROUPS, D_MODEL), F32)], axis=0)
    pm_weights = (hw, seg, w_a_up[l].astype(BF16), w_glu[l].astype(BF16), w_b_up[l].astype(BF16),
                  w_o[l].astype(BF16), norm_ffn_w[l].reshape(1, D_MODEL), wr)

    xp2 = x_prompt.reshape(n_p, D_MODEL)
    q_p, k_p, v_p, gates_p, conv_p, z_p, u_p, ga_p, gb_p = _inprep(
        xp2, nw_mix, w_parts, jnp.zeros((bp, SUBLANES, QKV_DIM), F32), conv_w[l], gate_p, seg, bp, 1)
    o_p, delta_p = _delta_prompt(q_p, k_p, v_p, gates_p, bp)
    ys_p, h_p = _s5(u_p, s5_params, jnp.zeros((bp, 2 * S5_FLAT), F32), bp, True)

    xs2 = jnp.swapaxes(x_sample, 0, 1).reshape(n_s, D_MODEL)
    cinit_s = jnp.swapaxes(state_conv[l], 0, 1).reshape(1, (CONV_W - 1) * bs, QKV_DIM)
    q_s, k_s, v_s, gate_s, conv_s, z_s, u_s, ga_s, gb_s = _inprep(
        xs2, nw_mix, w_parts, cinit_s, conv_w[l], gate_p, seg, 1, bs)
    s0t = jnp.transpose(state_delta[l], (1, 2, 3, 0)).reshape(DN_HEADS * DN_HEAD_DIM * DN_HEAD_DIM, bs)
    o_s, delta_st = _delta_sample(q_s, k_s, v_s, gate_s, s0t, bs, ts)
    delta_s = jnp.transpose(delta_st.reshape(DN_HEADS, DN_HEAD_DIM, DN_HEAD_DIM, bs), (3, 0, 1, 2))
    h0_s = jnp.concatenate([state_ssm_re[l].reshape(bs, S5_FLAT), state_ssm_im[l].reshape(bs, S5_FLAT)], axis=1)
    ys_s, h_s = _s5(u_s, s5_params, h0_s, bs, False)
    x1, hn, bkt, rank, rw, cnt = _postmix((xp2, o_p, z_p, ys_p, ga_p, gb_p), (xs2, o_s, z_s, ys_s, ga_s, gb_s),
                                          pm_weights, bp)

    plan, pos8 = _route_plan(bkt, rank, cnt, n_tok)
    ysorted = _moe(hn, w_expert_up[l], w_expert_down[l], plan)
    y_p, y_s = _combine(x1, ysorted, pos8, rw.T, norm_final_w.reshape(1, D_MODEL), n_p)

    y_prompt = y_p.reshape(bp, tp, D_MODEL)
    y_sample = jnp.swapaxes(y_s.reshape(ts, bs, D_MODEL), 0, 1)
    conv_sample = jnp.swapaxes(conv_s.reshape(CONV_W - 1, bs, QKV_DIM), 0, 1)
    return (y_prompt, y_sample,
            conv_p[None], delta_p[None],
            h_p[:, :S5_FLAT].reshape(1, bp, S5_GROUPS, S5_STATE), h_p[:, S5_FLAT:].reshape(1, bp, S5_GROUPS, S5_STATE),
            conv_sample[None], delta_s[None],
            h_s[:, :S5_FLAT].reshape(1, bs, S5_GROUPS, S5_STATE), h_s[:, S5_FLAT:].reshape(1, bs, S5_GROUPS, S5_STATE))
```

```python
import functools
import math

import jax
import jax.numpy as jnp
import numpy as np
from jax import lax
from jax.experimental import pallas as pl
from jax.experimental.pallas import tpu as pltpu

F32 = jnp.float32
BF16 = jnp.bfloat16
I32 = jnp.int32

D_MODEL = 1024
DN_HEADS = 8
DN_HEAD_DIM = 64
DN_WIDTH = DN_HEADS * DN_HEAD_DIM
QKV_DIM = 3 * DN_WIDTH
CONV_W = 4
DN_CHUNK = 64
S5_GROUP_CH = 16
S5_WIDTH = D_MODEL // 2
S5_GROUPS = S5_WIDTH // S5_GROUP_CH
S5_STATE = 64
S5_FLAT = S5_GROUPS * S5_STATE
MOE_GROUPS = 4
EXPERTS_PER_GROUP = 8
N_EXPERTS = MOE_GROUPS * EXPERTS_PER_GROUP
TOP_K = 2
EXPERT_FF = 256
RMS_EPS = 1e-6
L2_EPS = 1e-6

LANES = 128
SUBLANES = 8
VMEM_LIMIT = 56 * 1024 * 1024

W1_COLS = QKV_DIM + DN_WIDTH
W2_COLS = S5_WIDTH + 2 * D_MODEL

ROW_TILE = 512
INPREP_PARTS = 2
POSTMIX_PARTS = 2
MOE_TILE = 256
MOE_PHASES = 2
COMBINE_TILE = 512
DMA_QUEUES = 2
DELTA_SUBCHUNKS = 4
S5_SUPER = 2
S5_TILE_ROWS = 1024
S5_SCAN_SPLIT = 2
ROUTER_ROWS = 40


def _mm(a, b):
    return jnp.dot(a.astype(BF16), b.astype(BF16), preferred_element_type=F32)


def _mm_nt(a, b):
    return lax.dot_general(a.astype(BF16), b.astype(BF16), (((1,), (1,)), ((), ())),
                           preferred_element_type=F32)


def _sigmoid(x):
    return 0.5 * jnp.tanh(0.5 * x) + 0.5


def _cparams(sem):
    return pltpu.CompilerParams(dimension_semantics=sem, vmem_limit_bytes=VMEM_LIMIT)


ROW_SLAB = D_MODEL // LANES


def _slab_load(ref, rows, first=0, pitch=ROW_SLAB):
    return jnp.concatenate([ref[pl.ds(first + j, rows, stride=pitch), :] for j in range(ROW_SLAB)], axis=1)


def _slab_store(ref, x, first=0):
    for j in range(ROW_SLAB):
        ref[pl.ds(first * ROW_SLAB + j, x.shape[0], stride=ROW_SLAB), :] = x[:, j * LANES:(j + 1) * LANES]


def _softplus(x):
    return jnp.maximum(x, 0.0) + jnp.log1p(jnp.exp(-jnp.abs(x)))


def _inprep_kernel(x_ref, nw_ref, w1_ref, w2_ref, wab_ref, cinit_ref, cw_ref, gp_ref, seg_ref,
                   q_ref, k_ref, v_ref, gate_ref, cnew_ref, z_ref, u_ref, ga_ref, gb_ref, xp_ref,
                   *, shift, rc, rows):
    @pl.when(pl.program_id(1) == 0)
    def _():
        xp_ref[0:rc, :] = cinit_ref[0]

    seg = seg_ref[...]
    pr = rows // INPREP_PARTS

    def part_stages(part):
        rs = slice(part * pr, (part + 1) * pr)
        x = x_ref[rs, :]
        h = x * lax.rsqrt(jnp.mean(x * x, axis=-1, keepdims=True) + RMS_EPS) * nw_ref[...]
        hb = h.astype(BF16)

        def proj(w_ref, lo, hi):
            return jnp.dot(hb, w_ref[:, lo:hi], preferred_element_type=F32)

        xp_ref[rc + part * pr:rc + (part + 1) * pr, :] = proj(w1_ref, 0, QKV_DIM)
        ab = proj(wab_ref, 0, LANES)
        yield
        z_ref[rs, :] = proj(w1_ref, QKV_DIM, W1_COLS).astype(z_ref.dtype)
        u_ref[rs, :] = proj(w2_ref, 0, S5_WIDTH).astype(u_ref.dtype)
        acc = None
        for i in range(CONV_W):
            lo = rc + part * pr + (i - (CONV_W - 1)) * shift
            term = xp_ref[lo:lo + pr, :] * cw_ref[i:i + 1, :]
            acc = term if acc is None else acc + term
        y = acc * _sigmoid(acc)
        yield
        ga_ref[rs, :] = proj(w2_ref, S5_WIDTH, S5_WIDTH + D_MODEL).astype(ga_ref.dtype)
        q = y[:, 0:DN_WIDTH]
        k = y[:, DN_WIDTH:2 * DN_WIDTH]
        q_ref[rs, :] = q * lax.rsqrt(jnp.dot((q * q).astype(BF16), seg, preferred_element_type=F32) + L2_EPS)
        k_ref[rs, :] = k * lax.rsqrt(jnp.dot((k * k).astype(BF16), seg, preferred_element_type=F32) + L2_EPS)
        v_ref[rs, :] = y[:, 2 * DN_WIDTH:]
        yield
        gb_ref[rs, :] = proj(w2_ref, S5_WIDTH + D_MODEL, W2_COLS).astype(gb_ref.dtype)
        g = -jnp.exp(gp_ref[0:1, :]) * _softplus(ab + gp_ref[1:2, :])
        beta = _sigmoid(ab)
        lane = lax.broadcasted_iota(I32, ab.shape, 1)
        gate_ref[rs, :] = jnp.where(lane < DN_HEADS, g, beta)

    live = []
    pending = [part_stages(p) for p in range(INPREP_PARTS)]
    while live or pending:
        if pending:
            live.append(pending.pop(0))
        live = [g for g in live if next(g, StopIteration) is not StopIteration]

    keep = (CONV_W - 1) * shift
    cnew_ref[0] = xp_ref[rc + rows - keep:rc + rows, :]
    xp_ref[0:rc, :] = xp_ref[rows:rows + rc, :]


def _inprep(x2d, nw, w_parts, cinit, conv_w, gate_p, seg, nb, shift):
    n = x2d.shape[0]
    r = n // nb
    rows = min(ROW_TILE, r)
    nt = r // rows
    rc = cinit.shape[1]
    keep = (CONV_W - 1) * shift
    row = lambda b, i: (b * nt + i, 0)
    const = lambda b, i: (0, 0)
    kern = functools.partial(_inprep_kernel, shift=shift, rc=rc, rows=rows)
    outs = pl.pallas_call(
        kern,
        grid=(nb, nt),
        in_specs=[pl.BlockSpec((rows, D_MODEL), row),
                  pl.BlockSpec((1, D_MODEL), const),
                  pl.BlockSpec((D_MODEL, W1_COLS), const),
                  pl.BlockSpec((D_MODEL, W2_COLS), const),
                  pl.BlockSpec((D_MODEL, LANES), const),
                  pl.BlockSpec((1, rc, QKV_DIM), lambda b, i: (b, 0, 0)),
                  pl.BlockSpec((CONV_W, QKV_DIM), const),
                  pl.BlockSpec((2, LANES), const),
                  pl.BlockSpec((DN_WIDTH, DN_WIDTH), const)],
        out_specs=[pl.BlockSpec((rows, DN_WIDTH), row),
                   pl.BlockSpec((rows, DN_WIDTH), row),
                   pl.BlockSpec((rows, DN_WIDTH), row),
                   pl.BlockSpec((rows, LANES), row),
                   pl.BlockSpec((1, keep, QKV_DIM), lambda b, i: (b, 0, 0)),
                   pl.BlockSpec((rows, DN_WIDTH), row),
                   pl.BlockSpec((rows, S5_WIDTH), lambda b, i: (i, b)),
                   pl.BlockSpec((rows, D_MODEL), row),
                   pl.BlockSpec((rows, D_MODEL), row)],
        out_shape=[jax.ShapeDtypeStruct((n, DN_WIDTH), F32),
                   jax.ShapeDtypeStruct((n, DN_WIDTH), F32),
                   jax.ShapeDtypeStruct((n, DN_WIDTH), F32),
                   jax.ShapeDtypeStruct((n, LANES), F32),
                   jax.ShapeDtypeStruct((nb, keep, QKV_DIM), F32),
                   jax.ShapeDtypeStruct((n, DN_WIDTH), BF16),
                   jax.ShapeDtypeStruct((r, nb * S5_WIDTH), BF16),
                   jax.ShapeDtypeStruct((n, D_MODEL), BF16),
                   jax.ShapeDtypeStruct((n, D_MODEL), BF16)],
        scratch_shapes=[pltpu.VMEM((rc + rows, QKV_DIM), F32)],
        compiler_params=_cparams(("arbitrary", "arbitrary")),
        name="inprep",
    )(x2d, nw, *w_parts, cinit, conv_w, gate_p, seg)
    return outs


def _delta_home(low, h, x, other=0.0):
    return jnp.where(low, x, other) if h % 2 == 0 else jnp.where(low, other, x)


def _delta_prepare(q_ref, k_ref, v_ref, gate_ref, tril_ref, bufs, *, nsub):
    sol_buf, wq_buf, qk_buf, kdec_buf, dl_buf = bufs
    c = DN_CHUNK
    dk = DN_HEAD_DIM

    def home(h, x, other=0.0):
        return _delta_home(low, h, x, other)

    rowi2 = lax.broadcasted_iota(I32, (c, 2 * c), 0)
    lane2 = lax.broadcasted_iota(I32, (c, 2 * c), 1)
    coli2 = lane2 & (c - 1)
    causal2 = rowi2 >= coli2
    strict2 = rowi2 > coli2
    low = lane2 < dk
    tril = tril_ref[...]
    pairs = [(j, h) for j in range(nsub) for h in range(DN_HEADS)]
    units = [(j, pr) for j in range(nsub) for pr in range(DN_HEADS // 2)]
    rows = [slice(j * c, (j + 1) * c) for j in range(nsub)]
    gate = [gate_ref[rows[j], :] for j in range(nsub)]
    gc_all = [_split3_dot_left(tril, gate[j]) for j in range(nsub)]
    gc_t = [gc_all[j].T for j in range(nsub)]

    def block(ref, j, pr):
        return ref[rows[j], pr * LANES:(pr + 1) * LANES]

    gfull = {(j, h): jnp.broadcast_to(gc_all[j][:, h:h + 1], (c, 2 * c)) for j, h in pairs}
    g2 = {(j, pr): jnp.where(low, gfull[j, 2 * pr], gfull[j, 2 * pr + 1]) for j, pr in units}
    b2 = {(j, pr): jnp.where(low, gate[j][:, DN_HEADS + 2 * pr:DN_HEADS + 2 * pr + 1],
                             gate[j][:, DN_HEADS + 2 * pr + 1:DN_HEADS + 2 * pr + 2]) for j, pr in units}
    kp = {u: block(k_ref, *u) for u in units}
    qp = {u: block(q_ref, *u) * (dk ** -0.5) for u in units}
    egc2 = {u: jnp.exp(g2[u]) for u in units}
    kb2 = {u: kp[u] * b2[u] for u in units}
    vb2 = {u: block(v_ref, *u) * b2[u] for u in units}
    kw2s = {u: pltpu.roll(kb2[u] * egc2[u], dk, axis=1) for u in units}
    qd2 = {u: qp[u] * egc2[u] for u in units}
    glast2 = {u: g2[u][c - 1:c, :] for u in units}
    kdec_t2 = {u: (kp[u] * jnp.exp(glast2[u] - g2[u])).T for u in units}
    dlast2 = {u: jnp.exp(glast2[u]) for u in units}
    kk = {u: jnp.concatenate([kp[u], kp[u]], axis=0) for u in units}
    yield

    grow2 = {(j, h): jnp.concatenate([gc_t[j][h:h + 1, :], gc_t[j][h:h + 1, :]], axis=1) for j, h in pairs}
    decay = {p: jnp.where(causal2, jnp.exp(jnp.where(causal2, gfull[p] - grow2[p], 0.0)), 0.0) for p in pairs}
    gram = {(j, h): _mm_nt(jnp.concatenate([home(h, kb2[j, h // 2]), home(h, qp[j, h // 2])], axis=0), kk[j, h // 2])
            for j, h in pairs}
    mat = {p: jnp.where(strict2, gram[p][:c] * decay[p], 0.0).astype(BF16) for p in pairs}
    qk = {p: jnp.where(causal2, gram[p][c:] * decay[p], 0.0) for p in pairs}
    sol = {(j, h): home(h, vb2[j, h // 2], kw2s[j, h // 2]) for j, h in pairs}
    yield
    levels = int(math.log2(c))
    zeros2 = jnp.zeros((c, 2 * c), BF16)
    for lvl in range(levels):
        hi = {p: sol[p].astype(BF16) for p in pairs}
        lo = {p: (sol[p] - hi[p].astype(F32)).astype(BF16) for p in pairs}
        if lvl < levels - 1:
            y = {p: jnp.dot(mat[p], jnp.concatenate([jnp.concatenate([hi[p], mat[p]], axis=1),
                                                     jnp.concatenate([lo[p], zeros2], axis=1)], axis=0),
                            preferred_element_type=F32) for p in pairs}
            mat = {p: y[p][:, 2 * dk:].astype(BF16) for p in pairs}
            upd = {p: y[p][:, :2 * dk] for p in pairs}
        else:
            upd = {p: jnp.dot(mat[p], jnp.concatenate([hi[p], lo[p]], axis=0), preferred_element_type=F32)
                   for p in pairs}
        sol = {p: (sol[p] - upd[p]) if lvl == 0 else (sol[p] + upd[p]) for p in pairs}
        yield
    for j, h in pairs:
        n = j * DN_HEADS + h
        sol_buf[n] = sol[j, h]
        wq_buf[n] = jnp.concatenate([home(h, 0.0, sol[j, h]), home(h, qd2[j, h // 2])], axis=0).astype(BF16)
        qk_buf[n] = qk[j, h].astype(BF16)
    for j, pr in units:
        n = j * (DN_HEADS // 2) + pr
        kdec_buf[n] = kdec_t2[j, pr].astype(BF16)
        dl_buf[n] = jnp.broadcast_to(dlast2[j, pr], (SUBLANES, LANES))


def _delta_apply(bufs, o_ref, s_ref, *, nsub):
    sol_buf, wq_buf, qk_buf, kdec_buf, dl_buf = bufs
    c = DN_CHUNK
    dk = DN_HEAD_DIM
    heads = range(DN_HEADS)
    low = lax.broadcasted_iota(I32, (c, 2 * c), 1) < dk
    s = [s_ref[h] for h in heads]
    for j in range(nsub):
        ws, v_new, o_h = [], [], []
        for h in heads:
            n = j * DN_HEADS + h
            ws.append(jnp.dot(wq_buf[n], jnp.concatenate([s[h], s[h]], axis=0).astype(BF16),
                              preferred_element_type=F32))
        yield
        for h in heads:
            v_new.append(sol_buf[j * DN_HEADS + h] - ws[h][:c])
        for h in heads:
            o_h.append(ws[h][c:] + jnp.dot(qk_buf[j * DN_HEADS + h][:, :c], v_new[h].astype(BF16),
                                           preferred_element_type=F32))
        for pr in range(DN_HEADS // 2):
            o_ref[j * c:(j + 1) * c, pr * LANES:(pr + 1) * LANES] = jnp.where(low, o_h[2 * pr], o_h[2 * pr + 1])
        nxt = []
        for h in heads:
            u = j * (DN_HEADS // 2) + h // 2
            kdt = kdec_buf[u][(h % 2) * dk:(h % 2 + 1) * dk, :]
            d = dl_buf[u][0:1, :]
            nxt.append(_delta_home(low, h, s[h] * d + jnp.dot(kdt, v_new[h].astype(BF16),
                                                               preferred_element_type=F32)))
        s = nxt
        yield
    for h in heads:
        s_ref[h] = s[h]


def _delta_chunk_kernel(q_ref, k_ref, v_ref, gate_ref, tril_ref, o_ref, sfin_ref, s_ref, *bufs, nsub, nc):
    i = pl.program_id(0)
    half = len(bufs) // 2
    sets = (bufs[:half], bufs[half:])
    local = lax.rem(jnp.maximum(i - 1, 0), nc)

    @pl.when(i == 0)
    def _():
        for b in sets[1]:
            b[...] = jnp.zeros_like(b)

    @pl.when(local == 0)
    def _():
        s_ref[...] = jnp.zeros_like(s_ref)

    for par in range(2):
        @pl.when(lax.rem(i, 2) == par)
        def _(par=par):
            parts = [_delta_prepare(q_ref, k_ref, v_ref, gate_ref, tril_ref, sets[par], nsub=nsub),
                     _delta_apply(sets[1 - par], o_ref, s_ref, nsub=nsub)]
            while parts:
                parts = [g for g in parts if next(g, StopIteration) is not StopIteration]

    @pl.when(jnp.logical_and(i >= 1, local == nc - 1))
    def _():
        dk = DN_HEAD_DIM
        for h in range(DN_HEADS):
            sfin_ref[0, h] = s_ref[h][:, (h % 2) * dk:(h % 2 + 1) * dk]


def _split3_dot_left(b01, a):
    a1 = a.astype(BF16)
    r1 = a - a1.astype(F32)
    a2 = r1.astype(BF16)
    a3 = (r1 - a2.astype(F32)).astype(BF16)
    out = jnp.dot(b01, a3, preferred_element_type=F32)
    out = out + jnp.dot(b01, a2, preferred_element_type=F32)
    return out + jnp.dot(b01, a1, preferred_element_type=F32)


def _delta_prompt(q, k, v, gate, nb):
    n = q.shape[0]
    t = n // nb
    c = DN_CHUNK
    nsub = DELTA_SUBCHUNKS
    rows = nsub * c
    nc = t // rows
    nblk = nb * nc
    row_in = lambda i: (jnp.minimum(i, nblk - 1), 0)
    row_out = lambda i: (jnp.maximum(i - 1, 0), 0)
    tril = jnp.tril(jnp.ones((c, c), F32)).astype(BF16)
    nh = nsub * DN_HEADS
    npair = nsub * DN_HEADS // 2
    buf_set = [pltpu.VMEM((nh, c, 2 * DN_HEAD_DIM), F32),
               pltpu.VMEM((nh, 2 * c, 2 * DN_HEAD_DIM), BF16),
               pltpu.VMEM((nh, c, 2 * c), BF16),
               pltpu.VMEM((npair, 2 * DN_HEAD_DIM, c), BF16),
               pltpu.VMEM((npair, SUBLANES, LANES), F32)]
    return pl.pallas_call(
        functools.partial(_delta_chunk_kernel, nsub=nsub, nc=nc),
        grid=(nblk + 1,),
        in_specs=[pl.BlockSpec((rows, DN_WIDTH), row_in),
                  pl.BlockSpec((rows, DN_WIDTH), row_in),
                  pl.BlockSpec((rows, DN_WIDTH), row_in),
                  pl.BlockSpec((rows, LANES), row_in),
                  pl.BlockSpec((c, c), lambda i: (0, 0))],
        out_specs=[pl.BlockSpec((rows, DN_WIDTH), row_out),
                   pl.BlockSpec((1, DN_HEADS, DN_HEAD_DIM, DN_HEAD_DIM),
                                lambda i: (jnp.maximum(i - 1, 0) // nc, 0, 0, 0))],
        out_shape=[jax.ShapeDtypeStruct((n, DN_WIDTH), F32),
                   jax.ShapeDtypeStruct((nb, DN_HEADS, DN_HEAD_DIM, DN_HEAD_DIM), F32)],
        scratch_shapes=[pltpu.VMEM((DN_HEADS, DN_HEAD_DIM, 2 * DN_HEAD_DIM), F32)] + buf_set + buf_set,
        compiler_params=_cparams(("arbitrary",)),
        name="delta_prompt",
    )(q, k, v, gate, tril)


def _delta_step_kernel(q_ref, k_ref, v_ref, gate_ref, s0_ref, o_ref, s_ref, kt_ref, qt_ref, gt_ref, *, nt, nb):
    dk = DN_HEAD_DIM
    p = pl.program_id(0)
    for t in range(nt):
        rs = slice(t * nb, (t + 1) * nb)
        gt_ref[...] = gate_ref[rs, :].T
        kt_ref[...] = k_ref[rs, :].T
        qt_ref[...] = (q_ref[rs, :] * (dk ** -0.5)).T
        vt = v_ref[rs, :].T
        src = s0_ref if t == 0 else s_ref
        o_heads = []
        for j in range(2):
            a = jnp.exp(gt_ref[pl.ds(2 * p + j, 1), :])
            beta = gt_ref[pl.ds(2 * p + j + DN_HEADS, 1), :]
            base = j * dk * dk

            def k_dot_s(d, acc, j=j, base=base, src=src):
                sd = src[pl.ds(pl.multiple_of(base + d * dk, dk), dk), :]
                return acc + kt_ref[pl.ds(j * dk + d, 1), :] * sd

            ks = lax.fori_loop(0, dk, k_dot_s, jnp.zeros((dk, nb), F32), unroll=4)
            delta = beta * (vt[j * dk:(j + 1) * dk, :] - a * ks)

            def update(d, acc, j=j, base=base, src=src, a=a, delta=delta):
                r0 = pl.multiple_of(base + d * dk, dk)
                sn = a * src[pl.ds(r0, dk), :] + kt_ref[pl.ds(j * dk + d, 1), :] * delta
                s_ref[pl.ds(r0, dk), :] = sn
                return acc + qt_ref[pl.ds(j * dk + d, 1), :] * sn

            o_heads.append(lax.fori_loop(0, dk, update, jnp.zeros((dk, nb), F32), unroll=4))
        o_ref[rs, :] = jnp.concatenate(o_heads, axis=0).T


def _delta_sample(q, k, v, gate, s0t, nb, nt):
    dk = DN_HEAD_DIM
    flat = dk * dk
    n = nt * nb
    kern = functools.partial(_delta_step_kernel, nt=nt, nb=nb)
    pair = lambda p: (0, p)
    return pl.pallas_call(
        kern,
        grid=(DN_HEADS // 2,),
        in_specs=[pl.BlockSpec((n, LANES), pair),
                  pl.BlockSpec((n, LANES), pair),
                  pl.BlockSpec((n, LANES), pair),
                  pl.BlockSpec((n, LANES), lambda p: (0, 0)),
                  pl.BlockSpec((2 * flat, nb), lambda p: (p, 0))],
        out_specs=[pl.BlockSpec((n, LANES), pair),
                   pl.BlockSpec((2 * flat, nb), lambda p: (p, 0))],
        out_shape=[jax.ShapeDtypeStruct((n, DN_WIDTH), F32),
                   jax.ShapeDtypeStruct((DN_HEADS * flat, nb), F32)],
        scratch_shapes=[pltpu.VMEM((LANES, nb), F32),
                        pltpu.VMEM((LANES, nb), F32),
                        pltpu.VMEM((LANES, nb), F32)],
        compiler_params=_cparams(("arbitrary",)),
        name="delta_sample",
    )(q, k, v, gate, s0t)


def _s5_kernel(u_ref, btre_ref, btim_ref, lam_ref, ctre_ref, ctim_ref, d_ref, h0_ref, y_ref, hfin_ref,
               bw_ref, c_ref, ab_ref, x_ref, h_ref, ru_ref, ry_ref, *, nb, tt, wide):
    p2 = S5_FLAT

    @pl.when(pl.program_id(0) == 0)
    def _():
        lr = lam_ref[0:1, :]
        li = lam_ref[1:2, :]
        dt = jnp.exp(lam_ref[2:3, :])
        mag = jnp.exp(lr * dt)
        ab_re = mag * jnp.cos(li * dt)
        ab_im = mag * jnp.sin(li * dt)
        den = lr * lr + li * li
        nr = ab_re - 1.0
        ni = ab_im
        f_re = (nr * lr + ni * li) / den
        f_im = (ni * lr - nr * li) / den
        ab_ref[0:1, :] = ab_re
        ab_ref[1:2, :] = ab_im
        gpl = LANES // S5_STATE
        ch_g = lax.broadcasted_iota(I32, (S5_WIDTH, LANES), 0) // S5_GROUP_CH
        lane_g = lax.broadcasted_iota(I32, (S5_WIDTH, LANES), 1) // S5_STATE
        bre2 = jnp.concatenate([btre_ref[...]] * gpl, axis=1)
        bim2 = jnp.concatenate([btim_ref[...]] * gpl, axis=1)
        for j in range(p2 // LANES):
            cols = slice(j * LANES, (j + 1) * LANES)
            own = ch_g == gpl * j + lane_g
            bre = jnp.where(own, bre2, 0.0)
            bim = jnp.where(own, bim2, 0.0)
            bw_ref[:, cols] = (bre * f_re[:, cols] - bim * f_im[:, cols]).astype(BF16)
            bw_ref[:, p2 + j * LANES:p2 + (j + 1) * LANES] = (bim * f_re[:, cols] + bre * f_im[:, cols]).astype(BF16)
        cpl = LANES // S5_GROUP_CH
        st_g = lax.broadcasted_iota(I32, (p2, LANES), 0) // S5_STATE
        lane_cg = lax.broadcasted_iota(I32, (p2, LANES), 1) // S5_GROUP_CH
        for j in range(S5_WIDTH // LANES):
            cols = slice(j * LANES, (j + 1) * LANES)
            own = st_g == cpl * j + lane_cg
            c_ref[0:p2, cols] = jnp.where(own, ctre_ref[...], 0.0).astype(BF16)
            c_ref[p2:2 * p2, cols] = jnp.where(own, -ctim_ref[...], 0.0).astype(BF16)
        h_ref[...] = h0_ref[...]

    nck = S5_WIDTH // LANES
    if wide:
        for b in range(nb):
            for ck in range(nck):
                lo = b * S5_WIDTH + ck * LANES
                ru_ref[ck, pl.ds(b, tt, stride=nb), :] = u_ref[:, lo:lo + LANES].astype(F32)
        u = jnp.concatenate([ru_ref[ck] for ck in range(nck)], axis=1)
    else:
        u = u_ref[...].astype(F32)
    ub = u.astype(BF16)
    cw = S5_WIDTH // S5_SUPER
    sw = S5_FLAT // S5_SUPER
    for part in (0, p2):
        for b in range(S5_SUPER):
            x_ref[:, part + b * sw:part + (b + 1) * sw] = jnp.dot(
                ub[:, b * cw:(b + 1) * cw], bw_ref[b * cw:(b + 1) * cw, part + b * sw:part + (b + 1) * sw],
                preferred_element_type=F32)
    a_re = ab_ref[0:1, :]
    a_im = ab_ref[1:2, :]

    if nb == SUBLANES:
        wsl = p2 // S5_SCAN_SPLIT
        for sp in range(S5_SCAN_SPLIT):
            c0 = sp * wsl
            are = jnp.broadcast_to(a_re[:, c0:c0 + wsl], (nb, wsl))
            aim = jnp.broadcast_to(a_im[:, c0:c0 + wsl], (nb, wsl))

            def step(t, carry, c0=c0, are=are, aim=aim):
                hr, hi = carry
                r0 = pl.multiple_of(t * nb, nb)
                nr = are * hr - aim * hi + x_ref[pl.ds(r0, nb), c0:c0 + wsl]
                ni = are * hi + aim * hr + x_ref[pl.ds(r0, nb), p2 + c0:p2 + c0 + wsl]
                x_ref[pl.ds(r0, nb), c0:c0 + wsl] = nr
                x_ref[pl.ds(r0, nb), p2 + c0:p2 + c0 + wsl] = ni
                return nr, ni

            hr, hi = lax.fori_loop(0, tt, step, (h_ref[:, c0:c0 + wsl], h_ref[:, p2 + c0:p2 + c0 + wsl]),
                                   unroll=2)
            h_ref[:, c0:c0 + wsl] = hr
            h_ref[:, p2 + c0:p2 + c0 + wsl] = hi
    else:
        for t in range(tt):
            rs = slice(t * nb, (t + 1) * nb)
            hr = h_ref[:, 0:p2]
            hi = h_ref[:, p2:2 * p2]
            nr = a_re * hr - a_im * hi + x_ref[rs, 0:p2]
            ni = a_re * hi + a_im * hr + x_ref[rs, p2:2 * p2]
            h_ref[:, 0:p2] = nr
            h_ref[:, p2:2 * p2] = ni
            x_ref[rs, 0:p2] = nr
            x_ref[rs, p2:2 * p2] = ni

    for b in range(S5_SUPER):
        cols = slice(b * cw, (b + 1) * cw)
        y = None
        for part in (0, p2):
            rws = slice(part + b * sw, part + (b + 1) * sw)
            term = jnp.dot(x_ref[:, rws].astype(BF16), c_ref[rws, cols], preferred_element_type=F32)
            y = term if y is None else y + term
        if wide:
            skip = jnp.concatenate([ru_ref[b * (cw // LANES) + ck] for ck in range(cw // LANES)], axis=1)
        else:
            skip = u[:, cols]
        y = y + d_ref[:, cols] * skip
        if wide:
            for ck in range(cw // LANES):
                ry_ref[b * (cw // LANES) + ck] = y[:, ck * LANES:(ck + 1) * LANES]
        else:
            y_ref[:, cols] = y
    if wide:
        for b in range(nb):
            for ck in range(nck):
                lo = b * S5_WIDTH + ck * LANES
                y_ref[:, lo:lo + LANES] = ry_ref[ck, pl.ds(b, tt, stride=nb), :]
    hfin_ref[...] = h_ref[...]


def _s5(u, params, h0, nb, wide):
    btre, btim, lam, ctre, ctim, dvec = params
    t = u.shape[0] if wide else u.shape[0] // nb
    tt = min(S5_TILE_ROWS // nb, t)
    rows = tt * nb
    const = lambda i: (0, 0)
    kern = functools.partial(_s5_kernel, nb=nb, tt=tt, wide=wide)
    io_block = (tt, nb * S5_WIDTH) if wide else (rows, S5_WIDTH)
    return pl.pallas_call(
        kern,
        grid=(t // tt,),
        in_specs=[pl.BlockSpec(io_block, lambda i: (i, 0)),
                  pl.BlockSpec((S5_WIDTH, S5_STATE), const),
                  pl.BlockSpec((S5_WIDTH, S5_STATE), const),
                  pl.BlockSpec((SUBLANES, S5_FLAT), const),
                  pl.BlockSpec((S5_FLAT, LANES), const),
                  pl.BlockSpec((S5_FLAT, LANES), const),
                  pl.BlockSpec((1, S5_WIDTH), const),
                  pl.BlockSpec((nb, 2 * S5_FLAT), const)],
        out_specs=[pl.BlockSpec(io_block, lambda i: (i, 0)),
                   pl.BlockSpec((nb, 2 * S5_FLAT), const)],
        out_shape=[jax.ShapeDtypeStruct(u.shape, F32),
                   jax.ShapeDtypeStruct((nb, 2 * S5_FLAT), F32)],
        scratch_shapes=[pltpu.VMEM((S5_WIDTH, 2 * S5_FLAT), BF16),
                        pltpu.VMEM((2 * S5_FLAT, S5_WIDTH), BF16),
                        pltpu.VMEM((SUBLANES, S5_FLAT), F32),
                        pltpu.VMEM((rows, 2 * S5_FLAT), F32),
                        pltpu.VMEM((nb, 2 * S5_FLAT), F32),
                        pltpu.VMEM((S5_WIDTH // LANES, rows, LANES), F32),
                        pltpu.VMEM((S5_WIDTH // LANES, rows, LANES), F32)],
        compiler_params=_cparams(("arbitrary",)),
        name="s5",
    )(u, btre, btim, lam, ctre, ctim, dvec, h0)


def _postmix_kernel(xp_ref, op_ref, zp_ref, ysp_ref, gap_ref, gbp_ref,
                    xs_ref, os_ref, zs_ref, yss_ref, gas_ref, gbs_ref, *rest, nblk_p, range_tok):
    carry_ref = rest[-1]

    @pl.when(pl.program_id(0) == 0)
    def _():
        carry_ref[...] = jnp.zeros_like(carry_ref)

    @pl.when(pl.program_id(0) < nblk_p)
    def _():
        _postmix_body(xp_ref, op_ref, zp_ref, ysp_ref, gap_ref, gbp_ref, *rest, range_tok=range_tok)

    @pl.when(pl.program_id(0) >= nblk_p)
    def _():
        _postmix_body(xs_ref, os_ref, zs_ref, yss_ref, gas_ref, gbs_ref, *rest, range_tok=range_tok)


def _postmix_body(x_ref, o_ref, z_ref, ys_ref, ga_ref, gb_ref, hw_ref, seg_ref, wa_ref, wglu_ref, wb_ref,
                  wo_ref, nf_ref, wr_ref, su_ref, x1_ref, hn_ref, bkt_ref, rank_ref, rw_ref, cnt_ref, carry_ref,
                  *, range_tok):
    rows = x_ref.shape[0]
    pr = rows // POSTMIX_PARTS
    parts = [_postmix_part(p, pr, x_ref, o_ref, z_ref, ys_ref, ga_ref, gb_ref, hw_ref, seg_ref, wa_ref, wglu_ref,
                           wb_ref, wo_ref, nf_ref, wr_ref, su_ref, x1_ref, hn_ref, bkt_ref, rank_ref, rw_ref,
                           carry_ref, range_tok) for p in range(POSTMIX_PARTS)]
    live = []
    while live or parts:
        if parts:
            live.append(parts.pop(0))
        live = [g for g in live if next(g, StopIteration) is not StopIteration]
    cnt_ref[...] = carry_ref[...]


def _postmix_part(part, pr, x_ref, o_ref, z_ref, ys_ref, ga_ref, gb_ref, hw_ref, seg_ref, wa_ref, wglu_ref, wb_ref,
                  wo_ref, nf_ref, wr_ref, su_ref, x1_ref, hn_ref, bkt_ref, rank_ref, rw_ref, carry_ref, range_tok):
    rs = slice(part * pr, (part + 1) * pr)
    o = o_ref[rs, :]
    ms = jnp.dot((o * o).astype(BF16), seg_ref[...], preferred_element_type=F32) * (1.0 / DN_HEAD_DIM)
    on = o * lax.rsqrt(ms + RMS_EPS) * hw_ref[...]
    z = z_ref[rs, :]
    oa = on * (z * _sigmoid(z)).astype(F32)
    yield
    y_a = _mm(oa, wa_ref[...])
    ys = jax.nn.gelu(ys_ref[rs, :])
    yield
    ys = ys * _sigmoid(_mm(ys, wglu_ref[...]))
    yield
    y_b = _mm(ys, wb_ref[...])
    mixed = _sigmoid(ga_ref[rs, :]).astype(F32) * y_a + _sigmoid(gb_ref[rs, :]).astype(F32) * y_b
    yield
    x1 = x_ref[rs, :] + _mm(mixed, wo_ref[...])
    x1_ref[rs, :] = x1
    hn = x1 * lax.rsqrt(jnp.mean(x1 * x1, axis=-1, keepdims=True) + RMS_EPS) * nf_ref[...]
    _slab_store(hn_ref, hn, part * pr)
    yield

    wr = wr_ref[...]
    w_hi = wr.astype(BF16)
    w_lo = (wr - w_hi.astype(F32)).astype(BF16)
    hn_hi = hn.astype(BF16)
    hn_lo = (hn - hn_hi.astype(F32)).astype(BF16)
    both = _mm_nt(jnp.concatenate([w_hi, w_lo], axis=0), hn_hi)
    logits = both[:ROUTER_ROWS] + both[ROUTER_ROWS:] + _mm_nt(w_hi, hn_lo)
    yield
    coarse = logits[N_EXPERTS:N_EXPERTS + MOE_GROUPS, :]
    cm = jnp.max(coarse, axis=0, keepdims=True)
    ce = jnp.exp(coarse - cm)
    pc = ce / jnp.sum(ce, axis=0, keepdims=True)
    p_sel = jnp.max(pc, axis=0, keepdims=True)
    gi = lax.broadcasted_iota(I32, pc.shape, 0)
    g_sel = jnp.min(jnp.where(pc == p_sel, gi, MOE_GROUPS), axis=0, keepdims=True)
    fine = jnp.zeros((EXPERTS_PER_GROUP, logits.shape[1]), F32)
    for g in range(MOE_GROUPS):
        fine = fine + jnp.where(g_sel == g, logits[g * EXPERTS_PER_GROUP:(g + 1) * EXPERTS_PER_GROUP, :], 0.0)
    fm = jnp.max(fine, axis=0, keepdims=True)
    fe = jnp.exp(fine - fm)
    pf = fe / jnp.sum(fe, axis=0, keepdims=True)
    ei = lax.broadcasted_iota(I32, pf.shape, 0)
    v1 = jnp.max(pf, axis=0, keepdims=True)
    i1 = jnp.min(jnp.where(pf == v1, ei, EXPERTS_PER_GROUP), axis=0, keepdims=True)
    rest = jnp.where(ei == i1, -1.0, pf)
    v2 = jnp.max(rest, axis=0, keepdims=True)
    i2 = jnp.min(jnp.where(rest == v2, ei, EXPERTS_PER_GROUP), axis=0, keepdims=True)
    tot = v1 + v2
    rw_ref[0:1, rs] = v1 / tot * p_sel
    rw_ref[1:2, rs] = v2 / tot * p_sel

    tok = pl.program_id(0) * (pr * POSTMIX_PARTS) + part * pr + lax.broadcasted_iota(I32, (1, pr), 1)
    ph = jnp.zeros((1, pr), I32)
    for r in range(1, MOE_PHASES):
        ph = ph + (tok >= r * range_tok).astype(I32)
    bsel = [ph * N_EXPERTS + g_sel * EXPERTS_PER_GROUP + ix for ix in (i1, i2)]
    bi = lax.broadcasted_iota(I32, (MOE_PHASES * N_EXPERTS, pr), 0)
    onehot = [(bi == b).astype(F32) for b in bsel]
    cnt = onehot[0] + onehot[1]
    before = carry_ref[:, 0:1] + jnp.dot(cnt.astype(BF16), su_ref[0:pr, 0:pr], preferred_element_type=F32)
    for s in range(TOP_K):
        bkt_ref[s:s + 1, rs] = bsel[s]
        rank_ref[s:s + 1, rs] = jnp.sum(onehot[s] * before, axis=0, keepdims=True).astype(I32)
    carry_ref[...] = carry_ref[...] + jnp.sum(cnt, axis=1, keepdims=True)


def _postmix(prompt, sample, weights, nb):
    n_p = prompt[0].shape[0]
    n_s = sample[0].shape[0]
    t = n_p // nb
    tt = min(ROW_TILE, t, n_s)
    nt = t // tt
    nblk_p = n_p // tt
    nblk = nblk_p + n_s // tt
    n_total = n_p + n_s
    prow = lambda i: (jnp.minimum(i, nblk_p - 1), 0)
    pys = lambda i: (jnp.minimum(i, nblk_p - 1) % nt, jnp.minimum(i, nblk_p - 1) // nt)
    srow = lambda i: (jnp.maximum(i - nblk_p, 0), 0)
    const = lambda i: (0, 0)

    def stream_specs(row, ysmap):
        return [pl.BlockSpec((tt, D_MODEL), row),
                pl.BlockSpec((tt, DN_WIDTH), row),
                pl.BlockSpec((tt, DN_WIDTH), row),
                pl.BlockSpec((tt, S5_WIDTH), ysmap),
                pl.BlockSpec((tt, D_MODEL), row),
                pl.BlockSpec((tt, D_MODEL), row)]

    weight_specs = [pl.BlockSpec((1, DN_WIDTH), const),
                    pl.BlockSpec((DN_WIDTH, DN_WIDTH), const),
                    pl.BlockSpec((DN_WIDTH, D_MODEL), const),
                    pl.BlockSpec((S5_WIDTH, S5_WIDTH), const),
                    pl.BlockSpec((S5_WIDTH, D_MODEL), const),
                    pl.BlockSpec((D_MODEL, D_MODEL), const),
                    pl.BlockSpec((1, D_MODEL), const),
                    pl.BlockSpec((ROUTER_ROWS, D_MODEL), const),
                    pl.BlockSpec((tt, tt), const)]
    xp, op, zp, ysp, gap, gbp = prompt
    nbk = MOE_PHASES * N_EXPERTS
    earlier = jnp.triu(jnp.ones((tt, tt), F32), k=1).astype(BF16)
    return pl.pallas_call(
        functools.partial(_postmix_kernel, nblk_p=nblk_p, range_tok=n_total // MOE_PHASES),
        grid=(nblk,),
        in_specs=stream_specs(prow, pys) + stream_specs(srow, srow) + weight_specs,
        out_specs=[pl.BlockSpec((tt, D_MODEL), lambda i: (i, 0)),
                   pl.BlockSpec((tt * ROW_SLAB, LANES), lambda i: (i, 0)),
                   pl.BlockSpec((TOP_K, tt), lambda i: (0, i)),
                   pl.BlockSpec((TOP_K, tt), lambda i: (0, i)),
                   pl.BlockSpec((TOP_K, tt), lambda i: (0, i)),
                   pl.BlockSpec((nbk, LANES), const)],
        out_shape=[jax.ShapeDtypeStruct((n_total, D_MODEL), F32),
                   jax.ShapeDtypeStruct((n_total * ROW_SLAB, LANES), F32),
                   jax.ShapeDtypeStruct((TOP_K, n_total), I32),
                   jax.ShapeDtypeStruct((TOP_K, n_total), I32),
                   jax.ShapeDtypeStruct((TOP_K, n_total), F32),
                   jax.ShapeDtypeStruct((nbk, LANES), F32)],
        scratch_shapes=[pltpu.VMEM((nbk, LANES), F32)],
        compiler_params=_cparams(("arbitrary",)),
        name="postmix",
    )(xp, op, zp, ysp, gap, gbp, *sample, *weights, earlier)


def _wait_slabs(buf, sem):
    pltpu.make_async_copy(buf, buf, sem).wait()


def _moe_kernel(texp_ref, tph_ref, tsrc_ref, tnv_ref, tfirst_ref, tslot_ref, tnext_ref, otok_ref,
                hn_hbm, wu_hbm, wd_hbm, y_ref, hnv, xbuf, wu_buf, wd_buf, wub, wdb, sem, wsem):
    i = pl.program_id(0)
    tm = MOE_TILE
    rs = ROW_SLAB
    nv = tnv_ref[i]
    ph = tph_ref[i]
    range_rows = hnv.shape[0]

    def weight_copies(e, sl):
        return (pltpu.make_async_copy(wu_hbm.at[e], wu_buf.at[sl], wsem.at[sl]),
                pltpu.make_async_copy(wd_hbm.at[e], wd_buf.at[sl], wsem.at[sl]))

    @pl.when(i == 0)
    def _():
        for p, c in enumerate(weight_copies(texp_ref[0], 0)):
            c.start(priority=p % DMA_QUEUES)

    @pl.when(jnp.logical_and(nv > 0, jnp.logical_or(i == 0, ph != tph_ref[jnp.maximum(i - 1, 0)])))
    def _():
        piece = range_rows // DMA_QUEUES
        loads = [pltpu.make_async_copy(hn_hbm.at[pl.ds(pl.multiple_of(ph * range_rows + p * piece, rs), piece), :],
                                       hnv.at[pl.ds(p * piece, piece), :], sem) for p in range(DMA_QUEUES)]
        for p, c in enumerate(loads):
            c.start(priority=p)
        for c in loads:
            c.wait()

    for sl in range(2):
        @pl.when(jnp.logical_and(jnp.logical_and(nv > 0, tfirst_ref[i] == 1), tslot_ref[i] == sl))
        def _():
            for c in weight_copies(texp_ref[i], sl):
                c.wait()

            @pl.when(tnext_ref[i] >= 0)
            def _():
                for p, c in enumerate(weight_copies(tnext_ref[i], 1 - sl)):
                    c.start(priority=p % DMA_QUEUES)

            wub[...] = wu_buf[sl].astype(BF16)
            wdb[...] = wd_buf[sl].astype(BF16)

    @pl.when(nv == 0)
    def _():
        y_ref[...] = jnp.zeros_like(y_ref)

    @pl.when(nv > 0)
    def _():
        src0 = tsrc_ref[i]
        for r in range(tm):
            tok8 = pl.multiple_of(otok_ref[src0 + r], rs)
            xbuf[pl.ds(r * rs, rs), :] = hnv[pl.ds(tok8, rs), :]
        x = _slab_load(xbuf, tm).astype(BF16)
        hu = jnp.dot(x, wub[...], preferred_element_type=F32)
        gate = hu[:, :EXPERT_FF]
        up = hu[:, EXPERT_FF:]
        act = gate * _sigmoid(gate) * up
        _slab_store(y_ref, jnp.dot(act.astype(BF16), wdb[...], preferred_element_type=F32))


def _moe(hn, w_up, w_down, plan):
    ntiles = plan[0].shape[0]
    grid_spec = pltpu.PrefetchScalarGridSpec(
        num_scalar_prefetch=len(plan),
        grid=(ntiles,),
        in_specs=[pl.BlockSpec(memory_space=pl.ANY),
                  pl.BlockSpec(memory_space=pl.ANY),
                  pl.BlockSpec(memory_space=pl.ANY)],
        out_specs=pl.BlockSpec((MOE_TILE * ROW_SLAB, LANES), lambda i, *_: (i, 0)),
        scratch_shapes=[pltpu.VMEM((hn.shape[0] // MOE_PHASES, LANES), F32),
                        pltpu.VMEM((MOE_TILE * ROW_SLAB, LANES), F32),
                        pltpu.VMEM((2, D_MODEL, 2 * EXPERT_FF), F32),
                        pltpu.VMEM((2, EXPERT_FF, D_MODEL), F32),
                        pltpu.VMEM((D_MODEL, 2 * EXPERT_FF), BF16),
                        pltpu.VMEM((EXPERT_FF, D_MODEL), BF16),
                        pltpu.SemaphoreType.DMA,
                        pltpu.SemaphoreType.DMA((2,))])
    return pl.pallas_call(
        _moe_kernel,
        grid_spec=grid_spec,
        out_shape=jax.ShapeDtypeStruct((ntiles * MOE_TILE * ROW_SLAB, LANES), F32),
        compiler_params=_cparams(("arbitrary",)),
        name="moe",
    )(*plan, hn, w_up, w_down)


def _combine_kernel(pos_ref, x1_ref, ys_hbm, w_ref, nw_ref, outp_ref, outs_ref,
                    ybuf0, ybuf1, sem, *, nblk_p, n_tok):
    i = pl.program_id(0)
    nsteps = pl.num_programs(0)
    tt = x1_ref.shape[0]
    rs = ROW_SLAB
    slot = lax.rem(i, 2)
    ybuf = (ybuf0, ybuf1)

    def start_gather(step, sl):
        base = step * tt
        for r in range(tt * TOP_K):
            j, s = divmod(r, TOP_K)
            p8 = pl.multiple_of(pos_ref[s * n_tok + base + j], rs)
            pltpu.make_async_copy(ys_hbm.at[pl.ds(p8, rs), :], ybuf[sl].at[pl.ds((s * tt + j) * rs, rs), :],
                                  sem.at[sl]).start(priority=r % DMA_QUEUES)

    @pl.when(i == 0)
    def _():
        start_gather(0, 0)

    for sl in range(2):
        @pl.when(slot == sl)
        def _():
            _wait_slabs(ybuf[sl], sem.at[sl])
            start_gather(jnp.minimum(i + 1, nsteps - 1), 1 - sl)
            w = w_ref[...]
            y0 = _slab_load(ybuf[sl], tt, 0)
            y1 = _slab_load(ybuf[sl], tt, tt * rs)
            x = x1_ref[...] + w[:, 0:1] * y0 + w[:, 1:2] * y1
            res = x * lax.rsqrt(jnp.mean(x * x, axis=-1, keepdims=True) + RMS_EPS) * nw_ref[...]

            @pl.when(i < nblk_p)
            def _():
                outp_ref[...] = res

            @pl.when(i >= nblk_p)
            def _():
                outs_ref[...] = res

        @pl.when(jnp.logical_and(slot == sl, i == nsteps - 1))
        def _():
            _wait_slabs(ybuf[1 - sl], sem.at[1 - sl])


def _combine(x1, ysorted, pos8, wtok, nw, n_p):
    n = x1.shape[0]
    tt = math.gcd(math.gcd(n_p, n - n_p), COMBINE_TILE)
    nblk_p = n_p // tt
    grid_spec = pltpu.PrefetchScalarGridSpec(
        num_scalar_prefetch=1,
        grid=(n // tt,),
        in_specs=[pl.BlockSpec((tt, D_MODEL), lambda i, *_: (i, 0)),
                  pl.BlockSpec(memory_space=pl.ANY),
                  pl.BlockSpec((tt, TOP_K), lambda i, *_: (i, 0)),
                  pl.BlockSpec((1, D_MODEL), lambda i, *_: (0, 0))],
        out_specs=[pl.BlockSpec((tt, D_MODEL), lambda i, *_: (jnp.minimum(i, nblk_p - 1), 0)),
                   pl.BlockSpec((tt, D_MODEL), lambda i, *_: (jnp.maximum(i - nblk_p, 0), 0))],
        scratch_shapes=[pltpu.VMEM((tt * TOP_K * ROW_SLAB, LANES), F32),
                        pltpu.VMEM((tt * TOP_K * ROW_SLAB, LANES), F32),
                        pltpu.SemaphoreType.DMA((2,))])
    return pl.pallas_call(
        functools.partial(_combine_kernel, nblk_p=nblk_p, n_tok=n),
        grid_spec=grid_spec,
        out_shape=[jax.ShapeDtypeStruct((n_p, D_MODEL), F32),
                   jax.ShapeDtypeStruct((n - n_p, D_MODEL), F32)],
        compiler_params=_cparams(("arbitrary",)),
        name="combine",
    )(pos8, x1, ysorted, wtok, nw)


def _route_plan(bkt, rank, cnt, n_tok):
    tm = MOE_TILE
    n_assign = n_tok * TOP_K
    nbk = MOE_PHASES * N_EXPERTS
    ntiles = n_assign // tm + nbk
    range_tok = n_tok // MOE_PHASES
    b_flat = bkt.T.reshape(n_assign)
    order = jnp.argsort(b_flat, stable=True).astype(I32)
    counts = cnt[:, 0].astype(I32)
    cstart = jnp.cumsum(counts) - counts
    tiles_b = (counts + tm - 1) // tm
    tend = jnp.cumsum(tiles_b)
    tstart = tend - tiles_b
    tile_id = jnp.arange(ntiles, dtype=I32)
    tbk = jnp.minimum(jnp.sum((tile_id[:, None] >= tend[None, :]).astype(I32), axis=1), nbk - 1)
    onehot = (tbk[:, None] == jnp.arange(nbk, dtype=I32)[None, :]).astype(I32)
    pick = lambda v: jnp.sum(onehot * v[None, :], axis=1)
    done = (tile_id - pick(tstart)) * tm
    tnv = jnp.where(tile_id < tend[-1], jnp.clip(pick(counts) - done, 0, tm), 0)
    tsrc = jnp.where(tnv > 0, pick(cstart) + done, 0)
    texp = tbk % N_EXPERTS
    tph = tbk // N_EXPERTS
    nonempty = counts > 0
    bslot = (jnp.cumsum(nonempty.astype(I32)) - 1) % 2
    bidx = jnp.where(nonempty, jnp.arange(nbk, dtype=I32), nbk)
    nxt = jnp.concatenate([lax.cummin(bidx[::-1])[::-1][1:], jnp.full((1,), nbk, I32)])
    bnext = jnp.where(nxt < nbk, nxt % N_EXPERTS, -1)
    tfirst = jnp.logical_and(tnv > 0, done == 0).astype(I32)
    tslot = pick(bslot)
    tnext = pick(bnext)
    otok8 = jnp.concatenate([((order // TOP_K) % range_tok) * ROW_SLAB, jnp.zeros((tm,), I32)])
    plan = tuple(a.astype(I32) for a in (texp, tph, tsrc, tnv, tfirst, tslot, tnext, otok8))
    first = jnp.sum((bkt[:, :, None] == jnp.arange(nbk, dtype=I32)[None, None, :]).astype(I32)
                    * (tstart * tm)[None, None, :], axis=2)
    pos8 = ((first + rank) * ROW_SLAB).reshape(n_assign)
    return plan, pos8.astype(I32)


def _block_diag(m):
    g, a, b = m.shape
    eye = jnp.eye(g, dtype=m.dtype)
    return (eye[:, None, :, None] * m[:, :, None, :]).reshape(g * a, g * b)


def kernel(x_prompt, x_sample, state_conv, state_delta, state_ssm_re, state_ssm_im, norm_mix_w, w_in, conv_w, a_log, dt_bias, head_norm_w, w_a_up, s5_lambda_re, s5_lambda_im, s5_log_step, s5_b_re, s5_b_im, s5_c_re, s5_c_im, s5_d, w_glu, w_b_up, w_o, norm_ffn_w, w_router_coarse, w_router_fine, w_expert_up, w_expert_down, norm_final_w):
    bp, tp, _ = x_prompt.shape
    bs, ts, _ = x_sample.shape
    n_p = bp * tp
    n_s = bs * ts
    n_tok = n_p + n_s
    l = 0

    w = w_in[l].astype(BF16)
    c_ab = W1_COLS + 2 * DN_HEADS
    w_parts = (w[:, :W1_COLS], w[:, c_ab:],
               jnp.concatenate([w[:, W1_COLS:c_ab], jnp.zeros((D_MODEL, LANES - 2 * DN_HEADS), BF16)], axis=1))
    nw_mix = norm_mix_w[l].reshape(1, D_MODEL)
    pad8 = lambda v: jnp.concatenate([v, jnp.zeros((LANES - DN_HEADS,), F32)]).reshape(1, LANES)
    gate_p = jnp.concatenate([pad8(a_log[l]), pad8(dt_bias[l])], axis=0)
    seg = _block_diag(jnp.ones((DN_HEADS, DN_HEAD_DIM, DN_HEAD_DIM), BF16))
    chan_rows = lambda b: jnp.swapaxes(b, 1, 2).reshape(S5_WIDTH, S5_STATE)
    state_rows = lambda c: jnp.tile(jnp.swapaxes(c, 1, 2).reshape(S5_FLAT, S5_GROUP_CH),
                                    (1, LANES // S5_GROUP_CH))
    lam = jnp.concatenate([s5_lambda_re[l].reshape(1, S5_FLAT), s5_lambda_im[l].reshape(1, S5_FLAT),
                           jnp.repeat(s5_log_step[l], S5_STATE).reshape(1, S5_FLAT),
                           jnp.zeros((SUBLANES - 3, S5_FLAT), F32)], axis=0)
    s5_params = (chan_rows(s5_b_re[l]), chan_rows(s5_b_im[l]), lam,
                 state_rows(s5_c_re[l]), state_rows(s5_c_im[l]), s5_d[l].reshape(1, S5_WIDTH))
    hw = jnp.tile(head_norm_w[l], DN_HEADS).reshape(1, DN_WIDTH)
    wr = jnp.concatenate([w_router_fine[l].T, w_router_coarse[l].T,
                          jnp.zeros((ROUTER_ROWS - N_EXPERTS - MOE_GROUPS, D_MODEL), F32)], axis=0)
    pm_weights = (hw, seg, w_a_up[l].astype(BF16), w_glu[l].astype(BF16), w_b_up[l].astype(BF16),
                  w_o[l].astype(BF16), norm_ffn_w[l].reshape(1, D_MODEL), wr)

    xp2 = x_prompt.reshape(n_p, D_MODEL)
    q_p, k_p, v_p, gates_p, conv_p, z_p, u_p, ga_p, gb_p = _inprep(
        xp2, nw_mix, w_parts, jnp.zeros((bp, SUBLANES, QKV_DIM), F32), conv_w[l], gate_p, seg, bp, 1)
    o_p, delta_p = _delta_prompt(q_p, k_p, v_p, gates_p, bp)
    ys_p, h_p = _s5(u_p, s5_params, jnp.zeros((bp, 2 * S5_FLAT), F32), bp, True)

    xs2 = jnp.swapaxes(x_sample, 0, 1).reshape(n_s, D_MODEL)
    cinit_s = jnp.swapaxes(state_conv[l], 0, 1).reshape(1, (CONV_W - 1) * bs, QKV_DIM)
    q_s, k_s, v_s, gate_s, conv_s, z_s, u_s, ga_s, gb_s = _inprep(
        xs2, nw_mix, w_parts, cinit_s, conv_w[l], gate_p, seg, 1, bs)
    s0t = jnp.transpose(state_delta[l], (1, 2, 3, 0)).reshape(DN_HEADS * DN_HEAD_DIM * DN_HEAD_DIM, bs)
    o_s, delta_st = _delta_sample(q_s, k_s, v_s, gate_s, s0t, bs, ts)
    delta_s = jnp.transpose(delta_st.reshape(DN_HEADS, DN_HEAD_DIM, DN_HEAD_DIM, bs), (3, 0, 1, 2))
    h0_s = jnp.concatenate([state_ssm_re[l].reshape(bs, S5_FLAT), state_ssm_im[l].reshape(bs, S5_FLAT)], axis=1)
    ys_s, h_s = _s5(u_s, s5_params, h0_s, bs, False)
    x1, hn, bkt, rank, rw, cnt = _postmix((xp2, o_p, z_p, ys_p, ga_p, gb_p), (xs2, o_s, z_s, ys_s, ga_s, gb_s),
                                          pm_weights, bp)

    plan, pos8 = _route_plan(bkt, rank, cnt, n_tok)
    ysorted = _moe(hn, w_expert_up[l], w_expert_down[l], plan)
    y_p, y_s = _combine(x1, ysorted, pos8, rw.T, norm_final_w.reshape(1, D_MODEL), n_p)

    y_prompt = y_p.reshape(bp, tp, D_MODEL)
    y_sample = jnp.swapaxes(y_s.reshape(ts, bs, D_MODEL), 0, 1)
    conv_sample = jnp.swapaxes(conv_s.reshape(CONV_W - 1, bs, QKV_DIM), 0, 1)
    return (y_prompt, y_sample,
            conv_p[None], delta_p[None],
            h_p[:, :S5_FLAT].reshape(1, bp, S5_GROUPS, S5_STATE), h_p[:, S5_FLAT:].reshape(1, bp, S5_GROUPS, S5_STATE),
            conv_sample[None], delta_s[None],
            h_s[:, :S5_FLAT].reshape(1, bs, S5_GROUPS, S5_STATE), h_s[:, S5_FLAT:].reshape(1, bs, S5_GROUPS, S5_STATE))
```

```python
import functools
import math

import jax
import jax.numpy as jnp
import numpy as np
from jax import lax
from jax.experimental import pallas as pl
from jax.experimental.pallas import tpu as pltpu

F32 = jnp.float32
BF16 = jnp.bfloat16
I32 = jnp.int32

D_MODEL = 1024
DN_HEADS = 8
DN_HEAD_DIM = 64
DN_WIDTH = DN_HEADS * DN_HEAD_DIM
QKV_DIM = 3 * DN_WIDTH
CONV_W = 4
DN_CHUNK = 64
S5_GROUP_CH = 16
S5_WIDTH = D_MODEL // 2
S5_GROUPS = S5_WIDTH // S5_GROUP_CH
S5_STATE = 64
S5_FLAT = S5_GROUPS * S5_STATE
MOE_GROUPS = 4
EXPERTS_PER_GROUP = 8
N_EXPERTS = MOE_GROUPS * EXPERTS_PER_GROUP
TOP_K = 2
EXPERT_FF = 256
RMS_EPS = 1e-6
L2_EPS = 1e-6

LANES = 128
SUBLANES = 8
VMEM_LIMIT = 56 * 1024 * 1024

W1_COLS = QKV_DIM + DN_WIDTH
W2_COLS = S5_WIDTH + 2 * D_MODEL

ROW_TILE = 512
INPREP_PARTS = 2
POSTMIX_PARTS = 2
MOE_TILE = 256
MOE_PHASES = 2
COMBINE_TILE = 512
DMA_QUEUES = 2
DELTA_SUBCHUNKS = 4
S5_SUPER = 2
S5_TILE_ROWS = 1024
S5_SCAN_SPLIT = 2
ROUTER_ROWS = 40


def _mm(a, b):
    return jnp.dot(a.astype(BF16), b.astype(BF16), preferred_element_type=F32)


def _mm_nt(a, b):
    return lax.dot_general(a.astype(BF16), b.astype(BF16), (((1,), (1,)), ((), ())),
                           preferred_element_type=F32)


def _sigmoid(x):
    return 0.5 * jnp.tanh(0.5 * x) + 0.5


def _cparams(sem):
    return pltpu.CompilerParams(dimension_semantics=sem, vmem_limit_bytes=VMEM_LIMIT)


ROW_SLAB = D_MODEL // LANES


def _slab_load(ref, rows, first=0, pitch=ROW_SLAB):
    return jnp.concatenate([ref[pl.ds(first + j, rows, stride=pitch), :] for j in range(ROW_SLAB)], axis=1)


def _slab_store(ref, x, first=0):
    for j in range(ROW_SLAB):
        ref[pl.ds(first * ROW_SLAB + j, x.shape[0], stride=ROW_SLAB), :] = x[:, j * LANES:(j + 1) * LANES]


def _softplus(x):
    return jnp.maximum(x, 0.0) + jnp.log1p(jnp.exp(-jnp.abs(x)))


def _inprep_kernel(x_ref, nw_ref, w1_ref, w2_ref, wab_ref, cinit_ref, cw_ref, gp_ref, seg_ref,
                   q_ref, k_ref, v_ref, gate_ref, cnew_ref, z_ref, u_ref, ga_ref, gb_ref, xp_ref,
                   *, shift, rc, rows):
    @pl.when(pl.program_id(1) == 0)
    def _():
        xp_ref[0:rc, :] = cinit_ref[0]

    seg = seg_ref[...]
    pr = rows // INPREP_PARTS

    def part_stages(part):
        rs = slice(part * pr, (part + 1) * pr)
        x = x_ref[rs, :]
        h = x * lax.rsqrt(jnp.mean(x * x, axis=-1, keepdims=True) + RMS_EPS) * nw_ref[...]
        hb = h.astype(BF16)

        def proj(w_ref, lo, hi):
            return jnp.dot(hb, w_ref[:, lo:hi], preferred_element_type=F32)

        xp_ref[rc + part * pr:rc + (part + 1) * pr, :] = proj(w1_ref, 0, QKV_DIM)
        ab = proj(wab_ref, 0, LANES)
        yield
        z_ref[rs, :] = proj(w1_ref, QKV_DIM, W1_COLS).astype(z_ref.dtype)
        u_ref[rs, :] = proj(w2_ref, 0, S5_WIDTH).astype(u_ref.dtype)
        acc = None
        for i in range(CONV_W):
            lo = rc + part * pr + (i - (CONV_W - 1)) * shift
            term = xp_ref[lo:lo + pr, :] * cw_ref[i:i + 1, :]
            acc = term if acc is None else acc + term
        y = acc * _sigmoid(acc)
        yield
        ga_ref[rs, :] = proj(w2_ref, S5_WIDTH, S5_WIDTH + D_MODEL).astype(ga_ref.dtype)
        q = y[:, 0:DN_WIDTH]
        k = y[:, DN_WIDTH:2 * DN_WIDTH]
        q_ref[rs, :] = q * lax.rsqrt(jnp.dot((q * q).astype(BF16), seg, preferred_element_type=F32) + L2_EPS)
        k_ref[rs, :] = k * lax.rsqrt(jnp.dot((k * k).astype(BF16), seg, preferred_element_type=F32) + L2_EPS)
        v_ref[rs, :] = y[:, 2 * DN_WIDTH:]
        yield
        gb_ref[rs, :] = proj(w2_ref, S5_WIDTH + D_MODEL, W2_COLS).astype(gb_ref.dtype)
        g = -jnp.exp(gp_ref[0:1, :]) * _softplus(ab + gp_ref[1:2, :])
        beta = _sigmoid(ab)
        lane = lax.broadcasted_iota(I32, ab.shape, 1)
        gate_ref[rs, :] = jnp.where(lane < DN_HEADS, g, beta)

    live = []
    pending = [part_stages(p) for p in range(INPREP_PARTS)]
    while live or pending:
        if pending:
            live.append(pending.pop(0))
        live = [g for g in live if next(g, StopIteration) is not StopIteration]

    keep = (CONV_W - 1) * shift
    cnew_ref[0] = xp_ref[rc + rows - keep:rc + rows, :]
    xp_ref[0:rc, :] = xp_ref[rows:rows + rc, :]


def _inprep(x2d, nw, w_parts, cinit, conv_w, gate_p, seg, nb, shift):
    n = x2d.shape[0]
    r = n // nb
    rows = min(ROW_TILE, r)
    nt = r // rows
    rc = cinit.shape[1]
    keep = (CONV_W - 1) * shift
    row = lambda b, i: (b * nt + i, 0)
    const = lambda b, i: (0, 0)
    kern = functools.partial(_inprep_kernel, shift=shift, rc=rc, rows=rows)
    outs = pl.pallas_call(
        kern,
        grid=(nb, nt),
        in_specs=[pl.BlockSpec((rows, D_MODEL), row),
                  pl.BlockSpec((1, D_MODEL), const),
                  pl.BlockSpec((D_MODEL, W1_COLS), const),
                  pl.BlockSpec((D_MODEL, W2_COLS), const),
                  pl.BlockSpec((D_MODEL, LANES), const),
                  pl.BlockSpec((1, rc, QKV_DIM), lambda b, i: (b, 0, 0)),
                  pl.BlockSpec((CONV_W, QKV_DIM), const),
                  pl.BlockSpec((2, LANES), const),
                  pl.BlockSpec((DN_WIDTH, DN_WIDTH), const)],
        out_specs=[pl.BlockSpec((rows, DN_WIDTH), row),
                   pl.BlockSpec((rows, DN_WIDTH), row),
                   pl.BlockSpec((rows, DN_WIDTH), row),
                   pl.BlockSpec((rows, LANES), row),
                   pl.BlockSpec((1, keep, QKV_DIM), lambda b, i: (b, 0, 0)),
                   pl.BlockSpec((rows, DN_WIDTH), row),
                   pl.BlockSpec((rows, S5_WIDTH), lambda b, i: (i, b)),
                   pl.BlockSpec((rows, D_MODEL), row),
                   pl.BlockSpec((rows, D_MODEL), row)],
        out_shape=[jax.ShapeDtypeStruct((n, DN_WIDTH), F32),
                   jax.ShapeDtypeStruct((n, DN_WIDTH), F32),
                   jax.ShapeDtypeStruct((n, DN_WIDTH), F32),
                   jax.ShapeDtypeStruct((n, LANES), F32),
                   jax.ShapeDtypeStruct((nb, keep, QKV_DIM), F32),
                   jax.ShapeDtypeStruct((n, DN_WIDTH), BF16),
                   jax.ShapeDtypeStruct((r, nb * S5_WIDTH), BF16),
                   jax.ShapeDtypeStruct((n, D_MODEL), BF16),
                   jax.ShapeDtypeStruct((n, D_MODEL), BF16)],
        scratch_shapes=[pltpu.VMEM((rc + rows, QKV_DIM), F32)],
        compiler_params=_cparams(("arbitrary", "arbitrary")),
        name="inprep",
    )(x2d, nw, *w_parts, cinit, conv_w, gate_p, seg)
    return outs


def _delta_home(low, h, x, other=0.0):
    return jnp.where(low, x, other) if h % 2 == 0 else jnp.where(low, other, x)


def _delta_prepare(q_ref, k_ref, v_ref, gate_ref, tril_ref, bufs, *, nsub):
    sol_buf, wq_buf, qk_buf, kdec_buf, dl_buf = bufs
    c = DN_CHUNK
    dk = DN_HEAD_DIM

    def home(h, x, other=0.0):
        return _delta_home(low, h, x, other)

    rowi2 = lax.broadcasted_iota(I32, (c, 2 * c), 0)
    lane2 = lax.broadcasted_iota(I32, (c, 2 * c), 1)
    coli2 = lane2 & (c - 1)
    causal2 = rowi2 >= coli2
    strict2 = rowi2 > coli2
    low = lane2 < dk
    tril = tril_ref[...]
    pairs = [(j, h) for j in range(nsub) for h in range(DN_HEADS)]
    units = [(j, pr) for j in range(nsub) for pr in range(DN_HEADS // 2)]
    rows = [slice(j * c, (j + 1) * c) for j in range(nsub)]
    gate = [gate_ref[rows[j], :] for j in range(nsub)]
    gc_all = [_split3_dot_left(tril, gate[j]) for j in range(nsub)]
    gc_t = [gc_all[j].T for j in range(nsub)]

    def block(ref, j, pr):
        return ref[rows[j], pr * LANES:(pr + 1) * LANES]

    gfull = {(j, h): jnp.broadcast_to(gc_all[j][:, h:h + 1], (c, 2 * c)) for j, h in pairs}
    g2 = {(j, pr): jnp.where(low, gfull[j, 2 * pr], gfull[j, 2 * pr + 1]) for j, pr in units}
    b2 = {(j, pr): jnp.where(low, gate[j][:, DN_HEADS + 2 * pr:DN_HEADS + 2 * pr + 1],
                             gate[j][:, DN_HEADS + 2 * pr + 1:DN_HEADS + 2 * pr + 2]) for j, pr in units}
    kp = {u: block(k_ref, *u) for u in units}
    qp = {u: block(q_ref, *u) * (dk ** -0.5) for u in units}
    egc2 = {u: jnp.exp(g2[u]) for u in units}
    kb2 = {u: kp[u] * b2[u] for u in units}
    vb2 = {u: block(v_ref, *u) * b2[u] for u in units}
    kw2s = {u: pltpu.roll(kb2[u] * egc2[u], dk, axis=1) for u in units}
    qd2 = {u: qp[u] * egc2[u] for u in units}
    glast2 = {u: g2[u][c - 1:c, :] for u in units}
    kdec_t2 = {u: (kp[u] * jnp.exp(glast2[u] - g2[u])).T for u in units}
    dlast2 = {u: jnp.exp(glast2[u]) for u in units}
    kk = {u: jnp.concatenate([kp[u], kp[u]], axis=0) for u in units}
    yield

    grow2 = {(j, h): jnp.concatenate([gc_t[j][h:h + 1, :], gc_t[j][h:h + 1, :]], axis=1) for j, h in pairs}
    decay = {p: jnp.where(causal2, jnp.exp(jnp.where(causal2, gfull[p] - grow2[p], 0.0)), 0.0) for p in pairs}
    gram = {(j, h): _mm_nt(jnp.concatenate([home(h, kb2[j, h // 2]), home(h, qp[j, h // 2])], axis=0), kk[j, h // 2])
            for j, h in pairs}
    mat = {p: jnp.where(strict2, gram[p][:c] * decay[p], 0.0).astype(BF16) for p in pairs}
    qk = {p: jnp.where(causal2, gram[p][c:] * decay[p], 0.0) for p in pairs}
    sol = {(j, h): home(h, vb2[j, h // 2], kw2s[j, h // 2]) for j, h in pairs}
    yield
    levels = int(math.log2(c))
    zeros2 = jnp.zeros((c, 2 * c), BF16)
    for lvl in range(levels):
        hi = {p: sol[p].astype(BF16) for p in pairs}
        lo = {p: (sol[p] - hi[p].astype(F32)).astype(BF16) for p in pairs}
        if lvl < levels - 1:
            y = {p: jnp.dot(mat[p], jnp.concatenate([jnp.concatenate([hi[p], mat[p]], axis=1),
                                                     jnp.concatenate([lo[p], zeros2], axis=1)], axis=0),
                            preferred_element_type=F32) for p in pairs}
            mat = {p: y[p][:, 2 * dk:].astype(BF16) for p in pairs}
            upd = {p: y[p][:, :2 * dk] for p in pairs}
        else:
            upd = {p: jnp.dot(mat[p], jnp.concatenate([hi[p], lo[p]], axis=0), preferred_element_type=F32)
                   for p in pairs}
        sol = {p: (sol[p] - upd[p]) if lvl == 0 else (sol[p] + upd[p]) for p in pairs}
        yield
    for j, h in pairs:
        n = j * DN_HEADS + h
        sol_buf[n] = sol[j, h]
        wq_buf[n] = jnp.concatenate([home(h, 0.0, sol[j, h]), home(h, qd2[j, h // 2])], axis=0).astype(BF16)
        qk_buf[n] = qk[j, h].astype(BF16)
    for j, pr in units:
        n = j * (DN_HEADS // 2) + pr
        kdec_buf[n] = kdec_t2[j, pr].astype(BF16)
        dl_buf[n] = jnp.broadcast_to(dlast2[j, pr], (SUBLANES, LANES))


def _delta_apply(bufs, o_ref, s_ref, *, nsub):
    sol_buf, wq_buf, qk_buf, kdec_buf, dl_buf = bufs
    c = DN_CHUNK
    dk = DN_HEAD_DIM
    heads = range(DN_HEADS)
    low = lax.broadcasted_iota(I32, (c, 2 * c), 1) < dk
    s = [s_ref[h] for h in heads]
    for j in range(nsub):
        ws, v_new, o_h = [], [], []
        for h in heads:
            n = j * DN_HEADS + h
            ws.append(jnp.dot(wq_buf[n], jnp.concatenate([s[h], s[h]], axis=0).astype(BF16),
                              preferred_element_type=F32))
        yield
        for h in heads:
            v_new.append(sol_buf[j * DN_HEADS + h] - ws[h][:c])
        for h in heads:
            o_h.append(ws[h][c:] + jnp.dot(qk_buf[j * DN_HEADS + h][:, :c], v_new[h].astype(BF16),
                                           preferred_element_type=F32))
        for pr in range(DN_HEADS // 2):
            o_ref[j * c:(j + 1) * c, pr * LANES:(pr + 1) * LANES] = jnp.where(low, o_h[2 * pr], o_h[2 * pr + 1])
        nxt = []
        for h in heads:
            u = j * (DN_HEADS // 2) + h // 2
            kdt = kdec_buf[u][(h % 2) * dk:(h % 2 + 1) * dk, :]
            d = dl_buf[u][0:1, :]
            nxt.append(_delta_home(low, h, s[h] * d + jnp.dot(kdt, v_new[h].astype(BF16),
                                                               preferred_element_type=F32)))
        s = nxt
        yield
    for h in heads:
        s_ref[h] = s[h]


def _delta_chunk_kernel(q_ref, k_ref, v_ref, gate_ref, tril_ref, o_ref, sfin_ref, s_ref, *bufs, nsub, nc):
    i = pl.program_id(0)
    half = len(bufs) // 2
    sets = (bufs[:half], bufs[half:])
    local = lax.rem(jnp.maximum(i - 1, 0), nc)

    @pl.when(i == 0)
    def _():
        for b in sets[1]:
            b[...] = jnp.zeros_like(b)

    @pl.when(local == 0)
    def _():
        s_ref[...] = jnp.zeros_like(s_ref)

    for par in range(2):
        @pl.when(lax.rem(i, 2) == par)
        def _(par=par):
            parts = [_delta_prepare(q_ref, k_ref, v_ref, gate_ref, tril_ref, sets[par], nsub=nsub),
                     _delta_apply(sets[1 - par], o_ref, s_ref, nsub=nsub)]
            while parts:
                parts = [g for g in parts if next(g, StopIteration) is not StopIteration]

    @pl.when(jnp.logical_and(i >= 1, local == nc - 1))
    def _():
        dk = DN_HEAD_DIM
        for h in range(DN_HEADS):
            sfin_ref[0, h] = s_ref[h][:, (h % 2) * dk:(h % 2 + 1) * dk]


def _split3_dot_left(b01, a):
    a1 = a.astype(BF16)
    r1 = a - a1.astype(F32)
    a2 = r1.astype(BF16)
    a3 = (r1 - a2.astype(F32)).astype(BF16)
    out = jnp.dot(b01, a3, preferred_element_type=F32)
    out = out + jnp.dot(b01, a2, preferred_element_type=F32)
    return out + jnp.dot(b01, a1, preferred_element_type=F32)


def _delta_prompt(q, k, v, gate, nb):
    n = q.shape[0]
    t = n // nb
    c = DN_CHUNK
    nsub = DELTA_SUBCHUNKS
    rows = nsub * c
    nc = t // rows
    nblk = nb * nc
    row_in = lambda i: (jnp.minimum(i, nblk - 1), 0)
    row_out = lambda i: (jnp.maximum(i - 1, 0), 0)
    tril = jnp.tril(jnp.ones((c, c), F32)).astype(BF16)
    nh = nsub * DN_HEADS
    npair = nsub * DN_HEADS // 2
    buf_set = [pltpu.VMEM((nh, c, 2 * DN_HEAD_DIM), F32),
               pltpu.VMEM((nh, 2 * c, 2 * DN_HEAD_DIM), BF16),
               pltpu.VMEM((nh, c, 2 * c), BF16),
               pltpu.VMEM((npair, 2 * DN_HEAD_DIM, c), BF16),
               pltpu.VMEM((npair, SUBLANES, LANES), F32)]
    return pl.pallas_call(
        functools.partial(_delta_chunk_kernel, nsub=nsub, nc=nc),
        grid=(nblk + 1,),
        in_specs=[pl.BlockSpec((rows, DN_WIDTH), row_in),
                  pl.BlockSpec((rows, DN_WIDTH), row_in),
                  pl.BlockSpec((rows, DN_WIDTH), row_in),
                  pl.BlockSpec((rows, LANES), row_in),
                  pl.BlockSpec((c, c), lambda i: (0, 0))],
        out_specs=[pl.BlockSpec((rows, DN_WIDTH), row_out),
                   pl.BlockSpec((1, DN_HEADS, DN_HEAD_DIM, DN_HEAD_DIM),
                                lambda i: (jnp.maximum(i - 1, 0) // nc, 0, 0, 0))],
        out_shape=[jax.ShapeDtypeStruct((n, DN_WIDTH), F32),
                   jax.ShapeDtypeStruct((nb, DN_HEADS, DN_HEAD_DIM, DN_HEAD_DIM), F32)],
        scratch_shapes=[pltpu.VMEM((DN_HEADS, DN_HEAD_DIM, 2 * DN_HEAD_DIM), F32)] + buf_set + buf_set,
        compiler_params=_cparams(("arbitrary",)),
        name="delta_prompt",
    )(q, k, v, gate, tril)


def _delta_step_kernel(q_ref, k_ref, v_ref, gate_ref, s0_ref, o_ref, s_ref, kt_ref, qt_ref, gt_ref, *, nt, nb):
    dk = DN_HEAD_DIM
    p = pl.program_id(0)
    for t in range(nt):
        rs = slice(t * nb, (t + 1) * nb)
        gt_ref[...] = gate_ref[rs, :].T
        kt_ref[...] = k_ref[rs, :].T
        qt_ref[...] = (q_ref[rs, :] * (dk ** -0.5)).T
        vt = v_ref[rs, :].T
        src = s0_ref if t == 0 else s_ref
        o_heads = []
        for j in range(2):
            a = jnp.exp(gt_ref[pl.ds(2 * p + j, 1), :])
            beta = gt_ref[pl.ds(2 * p + j + DN_HEADS, 1), :]
            base = j * dk * dk

            def k_dot_s(d, acc, j=j, base=base, src=src):
                sd = src[pl.ds(pl.multiple_of(base + d * dk, dk), dk), :]
                return acc + kt_ref[pl.ds(j * dk + d, 1), :] * sd

            ks = lax.fori_loop(0, dk, k_dot_s, jnp.zeros((dk, nb), F32), unroll=4)
            delta = beta * (vt[j * dk:(j + 1) * dk, :] - a * ks)

            def update(d, acc, j=j, base=base, src=src, a=a, delta=delta):
                r0 = pl.multiple_of(base + d * dk, dk)
                sn = a * src[pl.ds(r0, dk), :] + kt_ref[pl.ds(j * dk + d, 1), :] * delta
                s_ref[pl.ds(r0, dk), :] = sn
                return acc + qt_ref[pl.ds(j * dk + d, 1), :] * sn

            o_heads.append(lax.fori_loop(0, dk, update, jnp.zeros((dk, nb), F32), unroll=4))
        o_ref[rs, :] = jnp.concatenate(o_heads, axis=0).T


def _delta_sample(q, k, v, gate, s0t, nb, nt):
    dk = DN_HEAD_DIM
    flat = dk * dk
    n = nt * nb
    kern = functools.partial(_delta_step_kernel, nt=nt, nb=nb)
    pair = lambda p: (0, p)
    return pl.pallas_call(
        kern,
        grid=(DN_HEADS // 2,),
        in_specs=[pl.BlockSpec((n, LANES), pair),
                  pl.BlockSpec((n, LANES), pair),
                  pl.BlockSpec((n, LANES), pair),
                  pl.BlockSpec((n, LANES), lambda p: (0, 0)),
                  pl.BlockSpec((2 * flat, nb), lambda p: (p, 0))],
        out_specs=[pl.BlockSpec((n, LANES), pair),
                   pl.BlockSpec((2 * flat, nb), lambda p: (p, 0))],
        out_shape=[jax.ShapeDtypeStruct((n, DN_WIDTH), F32),
                   jax.ShapeDtypeStruct((DN_HEADS * flat, nb), F32)],
        scratch_shapes=[pltpu.VMEM((LANES, nb), F32),
                        pltpu.VMEM((LANES, nb), F32),
                        pltpu.VMEM((LANES, nb), F32)],
        compiler_params=_cparams(("arbitrary",)),
        name="delta_sample",
    )(q, k, v, gate, s0t)


def _s5_kernel(u_ref, btre_ref, btim_ref, lam_ref, ctre_ref, ctim_ref, d_ref, h0_ref, y_ref, hfin_ref,
               bw_ref, c_ref, ab_ref, x_ref, h_ref, ru_ref, ry_ref, *, nb, tt, wide):
    p2 = S5_FLAT

    @pl.when(pl.program_id(0) == 0)
    def _():
        lr = lam_ref[0:1, :]
        li = lam_ref[1:2, :]
        dt = jnp.exp(lam_ref[2:3, :])
        mag = jnp.exp(lr * dt)
        ab_re = mag * jnp.cos(li * dt)
        ab_im = mag * jnp.sin(li * dt)
        den = lr * lr + li * li
        nr = ab_re - 1.0
        ni = ab_im
        f_re = (nr * lr + ni * li) / den
        f_im = (ni * lr - nr * li) / den
        ab_ref[0:1, :] = ab_re
        ab_ref[1:2, :] = ab_im
        gpl = LANES // S5_STATE
        ch_g = lax.broadcasted_iota(I32, (S5_WIDTH, LANES), 0) // S5_GROUP_CH
        lane_g = lax.broadcasted_iota(I32, (S5_WIDTH, LANES), 1) // S5_STATE
        bre2 = jnp.concatenate([btre_ref[...]] * gpl, axis=1)
        bim2 = jnp.concatenate([btim_ref[...]] * gpl, axis=1)
        for j in range(p2 // LANES):
            cols = slice(j * LANES, (j + 1) * LANES)
            own = ch_g == gpl * j + lane_g
            bre = jnp.where(own, bre2, 0.0)
            bim = jnp.where(own, bim2, 0.0)
            bw_ref[:, cols] = (bre * f_re[:, cols] - bim * f_im[:, cols]).astype(BF16)
            bw_ref[:, p2 + j * LANES:p2 + (j + 1) * LANES] = (bim * f_re[:, cols] + bre * f_im[:, cols]).astype(BF16)
        cpl = LANES // S5_GROUP_CH
        st_g = lax.broadcasted_iota(I32, (p2, LANES), 0) // S5_STATE
        lane_cg = lax.broadcasted_iota(I32, (p2, LANES), 1) // S5_GROUP_CH
        for j in range(S5_WIDTH // LANES):
            cols = slice(j * LANES, (j + 1) * LANES)
            own = st_g == cpl * j + lane_cg
            c_ref[0:p2, cols] = jnp.where(own, ctre_ref[...], 0.0).astype(BF16)
            c_ref[p2:2 * p2, cols] = jnp.where(own, -ctim_ref[...], 0.0).astype(BF16)
        h_ref[...] = h0_ref[...]

    nck = S5_WIDTH // LANES
    if wide:
        for b in range(nb):
            for ck in range(nck):
                lo = b * S5_WIDTH + ck * LANES
                ru_ref[ck, pl.ds(b, tt, stride=nb), :] = u_ref[:, lo:lo + LANES].astype(F32)
        u = jnp.concatenate([ru_ref[ck] for ck in range(nck)], axis=1)
    else:
        u = u_ref[...].astype(F32)
    ub = u.astype(BF16)
    cw = S5_WIDTH // S5_SUPER
    sw = S5_FLAT // S5_SUPER
    for part in (0, p2):
        for b in range(S5_SUPER):
            x_ref[:, part + b * sw:part + (b + 1) * sw] = jnp.dot(
                ub[:, b * cw:(b + 1) * cw], bw_ref[b * cw:(b + 1) * cw, part + b * sw:part + (b + 1) * sw],
                preferred_element_type=F32)
    a_re = ab_ref[0:1, :]
    a_im = ab_ref[1:2, :]

    if nb == SUBLANES:
        wsl = p2 // S5_SCAN_SPLIT
        for sp in range(S5_SCAN_SPLIT):
            c0 = sp * wsl
            are = jnp.broadcast_to(a_re[:, c0:c0 + wsl], (nb, wsl))
            aim = jnp.broadcast_to(a_im[:, c0:c0 + wsl], (nb, wsl))

            def step(t, carry, c0=c0, are=are, aim=aim):
                hr, hi = carry
                r0 = pl.multiple_of(t * nb, nb)
                nr = are * hr - aim * hi + x_ref[pl.ds(r0, nb), c0:c0 + wsl]
                ni = are * hi + aim * hr + x_ref[pl.ds(r0, nb), p2 + c0:p2 + c0 + wsl]
                x_ref[pl.ds(r0, nb), c0:c0 + wsl] = nr
                x_ref[pl.ds(r0, nb), p2 + c0:p2 + c0 + wsl] = ni
                return nr, ni

            hr, hi = lax.fori_loop(0, tt, step, (h_ref[:, c0:c0 + wsl], h_ref[:, p2 + c0:p2 + c0 + wsl]),
                                   unroll=4)
            h_ref[:, c0:c0 + wsl] = hr
            h_ref[:, p2 + c0:p2 + c0 + wsl] = hi
    else:
        for t in range(tt):
            rs = slice(t * nb, (t + 1) * nb)
            hr = h_ref[:, 0:p2]
            hi = h_ref[:, p2:2 * p2]
            nr = a_re * hr - a_im * hi + x_ref[rs, 0:p2]
            ni = a_re * hi + a_im * hr + x_ref[rs, p2:2 * p2]
            h_ref[:, 0:p2] = nr
            h_ref[:, p2:2 * p2] = ni
            x_ref[rs, 0:p2] = nr
            x_ref[rs, p2:2 * p2] = ni

    for b in range(S5_SUPER):
        cols = slice(b * cw, (b + 1) * cw)
        y = None
        for part in (0, p2):
            rws = slice(part + b * sw, part + (b + 1) * sw)
            term = jnp.dot(x_ref[:, rws].astype(BF16), c_ref[rws, cols], preferred_element_type=F32)
            y = term if y is None else y + term
        if wide:
            skip = jnp.concatenate([ru_ref[b * (cw // LANES) + ck] for ck in range(cw // LANES)], axis=1)
        else:
            skip = u[:, cols]
        y = y + d_ref[:, cols] * skip
        if wide:
            for ck in range(cw // LANES):
                ry_ref[b * (cw // LANES) + ck] = y[:, ck * LANES:(ck + 1) * LANES]
        else:
            y_ref[:, cols] = y
    if wide:
        for b in range(nb):
            for ck in range(nck):
                lo = b * S5_WIDTH + ck * LANES
                y_ref[:, lo:lo + LANES] = ry_ref[ck, pl.ds(b, tt, stride=nb), :]
    hfin_ref[...] = h_ref[...]


def _s5(u, params, h0, nb, wide):
    btre, btim, lam, ctre, ctim, dvec = params
    t = u.shape[0] if wide else u.shape[0] // nb
    tt = min(S5_TILE_ROWS // nb, t)
    rows = tt * nb
    const = lambda i: (0, 0)
    kern = functools.partial(_s5_kernel, nb=nb, tt=tt, wide=wide)
    io_block = (tt, nb * S5_WIDTH) if wide else (rows, S5_WIDTH)
    return pl.pallas_call(
        kern,
        grid=(t // tt,),
        in_specs=[pl.BlockSpec(io_block, lambda i: (i, 0)),
                  pl.BlockSpec((S5_WIDTH, S5_STATE), const),
                  pl.BlockSpec((S5_WIDTH, S5_STATE), const),
                  pl.BlockSpec((SUBLANES, S5_FLAT), const),
                  pl.BlockSpec((S5_FLAT, LANES), const),
                  pl.BlockSpec((S5_FLAT, LANES), const),
                  pl.BlockSpec((1, S5_WIDTH), const),
                  pl.BlockSpec((nb, 2 * S5_FLAT), const)],
        out_specs=[pl.BlockSpec(io_block, lambda i: (i, 0)),
                   pl.BlockSpec((nb, 2 * S5_FLAT), const)],
        out_shape=[jax.ShapeDtypeStruct(u.shape, F32),
                   jax.ShapeDtypeStruct((nb, 2 * S5_FLAT), F32)],
        scratch_shapes=[pltpu.VMEM((S5_WIDTH, 2 * S5_FLAT), BF16),
                        pltpu.VMEM((2 * S5_FLAT, S5_WIDTH), BF16),
                        pltpu.VMEM((SUBLANES, S5_FLAT), F32),
                        pltpu.VMEM((rows, 2 * S5_FLAT), F32),
                        pltpu.VMEM((nb, 2 * S5_FLAT), F32),
                        pltpu.VMEM((S5_WIDTH // LANES, rows, LANES), F32),
                        pltpu.VMEM((S5_WIDTH // LANES, rows, LANES), F32)],
        compiler_params=_cparams(("arbitrary",)),
        name="s5",
    )(u, btre, btim, lam, ctre, ctim, dvec, h0)


def _postmix_kernel(xp_ref, op_ref, zp_ref, ysp_ref, gap_ref, gbp_ref,
                    xs_ref, os_ref, zs_ref, yss_ref, gas_ref, gbs_ref, *rest, nblk_p, range_tok):
    carry_ref = rest[-1]

    @pl.when(pl.program_id(0) == 0)
    def _():
        carry_ref[...] = jnp.zeros_like(carry_ref)

    @pl.when(pl.program_id(0) < nblk_p)
    def _():
        _postmix_body(xp_ref, op_ref, zp_ref, ysp_ref, gap_ref, gbp_ref, *rest, range_tok=range_tok)

    @pl.when(pl.program_id(0) >= nblk_p)
    def _():
        _postmix_body(xs_ref, os_ref, zs_ref, yss_ref, gas_ref, gbs_ref, *rest, range_tok=range_tok)


def _postmix_body(x_ref, o_ref, z_ref, ys_ref, ga_ref, gb_ref, hw_ref, seg_ref, wa_ref, wglu_ref, wb_ref,
                  wo_ref, nf_ref, wr_ref, su_ref, x1_ref, hn_ref, bkt_ref, rank_ref, rw_ref, cnt_ref, carry_ref,
                  *, range_tok):
    rows = x_ref.shape[0]
    pr = rows // POSTMIX_PARTS
    parts = [_postmix_part(p, pr, x_ref, o_ref, z_ref, ys_ref, ga_ref, gb_ref, hw_ref, seg_ref, wa_ref, wglu_ref,
                           wb_ref, wo_ref, nf_ref, wr_ref, su_ref, x1_ref, hn_ref, bkt_ref, rank_ref, rw_ref,
                           carry_ref, range_tok) for p in range(POSTMIX_PARTS)]
    live = []
    while live or parts:
        if parts:
            live.append(parts.pop(0))
        live = [g for g in live if next(g, StopIteration) is not StopIteration]
    cnt_ref[...] = carry_ref[...]


def _postmix_part(part, pr, x_ref, o_ref, z_ref, ys_ref, ga_ref, gb_ref, hw_ref, seg_ref, wa_ref, wglu_ref, wb_ref,
                  wo_ref, nf_ref, wr_ref, su_ref, x1_ref, hn_ref, bkt_ref, rank_ref, rw_ref, carry_ref, range_tok):
    rs = slice(part * pr, (part + 1) * pr)
    o = o_ref[rs, :]
    ms = jnp.dot((o * o).astype(BF16), seg_ref[...], preferred_element_type=F32) * (1.0 / DN_HEAD_DIM)
    on = o * lax.rsqrt(ms + RMS_EPS) * hw_ref[...]
    z = z_ref[rs, :]
    oa = on * (z * _sigmoid(z)).astype(F32)
    yield
    y_a = _mm(oa, wa_ref[...])
    ys = jax.nn.gelu(ys_ref[rs, :])
    yield
    ys = ys * _sigmoid(_mm(ys, wglu_ref[...]))
    yield
    y_b = _mm(ys, wb_ref[...])
    mixed = _sigmoid(ga_ref[rs, :]).astype(F32) * y_a + _sigmoid(gb_ref[rs, :]).astype(F32) * y_b
    yield
    x1 = x_ref[rs, :] + _mm(mixed, wo_ref[...])
    x1_ref[rs, :] = x1
    hn = x1 * lax.rsqrt(jnp.mean(x1 * x1, axis=-1, keepdims=True) + RMS_EPS) * nf_ref[...]
    _slab_store(hn_ref, hn, part * pr)
    yield

    wr = wr_ref[...]
    w_hi = wr.astype(BF16)
    w_lo = (wr - w_hi.astype(F32)).astype(BF16)
    hn_hi = hn.astype(BF16)
    hn_lo = (hn - hn_hi.astype(F32)).astype(BF16)
    both = _mm_nt(jnp.concatenate([w_hi, w_lo], axis=0), hn_hi)
    logits = both[:ROUTER_ROWS] + both[ROUTER_ROWS:] + _mm_nt(w_hi, hn_lo)
    yield
    coarse = logits[N_EXPERTS:N_EXPERTS + MOE_GROUPS, :]
    cm = jnp.max(coarse, axis=0, keepdims=True)
    ce = jnp.exp(coarse - cm)
    pc = ce / jnp.sum(ce, axis=0, keepdims=True)
    p_sel = jnp.max(pc, axis=0, keepdims=True)
    gi = lax.broadcasted_iota(I32, pc.shape, 0)
    g_sel = jnp.min(jnp.where(pc == p_sel, gi, MOE_GROUPS), axis=0, keepdims=True)
    fine = jnp.zeros((EXPERTS_PER_GROUP, logits.shape[1]), F32)
    for g in range(MOE_GROUPS):
        fine = fine + jnp.where(g_sel == g, logits[g * EXPERTS_PER_GROUP:(g + 1) * EXPERTS_PER_GROUP, :], 0.0)
    fm = jnp.max(fine, axis=0, keepdims=True)
    fe = jnp.exp(fine - fm)
    pf = fe / jnp.sum(fe, axis=0, keepdims=True)
    ei = lax.broadcasted_iota(I32, pf.shape, 0)
    v1 = jnp.max(pf, axis=0, keepdims=True)
    i1 = jnp.min(jnp.where(pf == v1, ei, EXPERTS_PER_GROUP), axis=0, keepdims=True)
    rest = jnp.where(ei == i1, -1.0, pf)
    v2 = jnp.max(rest, axis=0, keepdims=True)
    i2 = jnp.min(jnp.where(rest == v2, ei, EXPERTS_PER_GROUP), axis=0, keepdims=True)
    tot = v1 + v2
    rw_ref[0:1, rs] = v1 / tot * p_sel
    rw_ref[1:2, rs] = v2 / tot * p_sel

    tok = pl.program_id(0) * (pr * POSTMIX_PARTS) + part * pr + lax.broadcasted_iota(I32, (1, pr), 1)
    ph = jnp.zeros((1, pr), I32)
    for r in range(1, MOE_PHASES):
        ph = ph + (tok >= r * range_tok).astype(I32)
    bsel = [ph * N_EXPERTS + g_sel * EXPERTS_PER_GROUP + ix for ix in (i1, i2)]
    bi = lax.broadcasted_iota(I32, (MOE_PHASES * N_EXPERTS, pr), 0)
    onehot = [(bi == b).astype(F32) for b in bsel]
    cnt = onehot[0] + onehot[1]
    before = carry_ref[:, 0:1] + jnp.dot(cnt.astype(BF16), su_ref[0:pr, 0:pr], preferred_element_type=F32)
    for s in range(TOP_K):
        bkt_ref[s:s + 1, rs] = bsel[s]
        rank_ref[s:s + 1, rs] = jnp.sum(onehot[s] * before, axis=0, keepdims=True).astype(I32)
    carry_ref[...] = carry_ref[...] + jnp.sum(cnt, axis=1, keepdims=True)


def _postmix(prompt, sample, weights, nb):
    n_p = prompt[0].shape[0]
    n_s = sample[0].shape[0]
    t = n_p // nb
    tt = min(ROW_TILE, t, n_s)
    nt = t // tt
    nblk_p = n_p // tt
    nblk = nblk_p + n_s // tt
    n_total = n_p + n_s
    prow = lambda i: (jnp.minimum(i, nblk_p - 1), 0)
    pys = lambda i: (jnp.minimum(i, nblk_p - 1) % nt, jnp.minimum(i, nblk_p - 1) // nt)
    srow = lambda i: (jnp.maximum(i - nblk_p, 0), 0)
    const = lambda i: (0, 0)

    def stream_specs(row, ysmap):
        return [pl.BlockSpec((tt, D_MODEL), row),
                pl.BlockSpec((tt, DN_WIDTH), row),
                pl.BlockSpec((tt, DN_WIDTH), row),
                pl.BlockSpec((tt, S5_WIDTH), ysmap),
                pl.BlockSpec((tt, D_MODEL), row),
                pl.BlockSpec((tt, D_MODEL), row)]

    weight_specs = [pl.BlockSpec((1, DN_WIDTH), const),
                    pl.BlockSpec((DN_WIDTH, DN_WIDTH), const),
                    pl.BlockSpec((DN_WIDTH, D_MODEL), const),
                    pl.BlockSpec((S5_WIDTH, S5_WIDTH), const),
                    pl.BlockSpec((S5_WIDTH, D_MODEL), const),
                    pl.BlockSpec((D_MODEL, D_MODEL), const),
                    pl.BlockSpec((1, D_MODEL), const),
                    pl.BlockSpec((ROUTER_ROWS, D_MODEL), const),
                    pl.BlockSpec((tt, tt), const)]
    xp, op, zp, ysp, gap, gbp = prompt
    nbk = MOE_PHASES * N_EXPERTS
    earlier = jnp.triu(jnp.ones((tt, tt), F32), k=1).astype(BF16)
    return pl.pallas_call(
        functools.partial(_postmix_kernel, nblk_p=nblk_p, range_tok=n_total // MOE_PHASES),
        grid=(nblk,),
        in_specs=stream_specs(prow, pys) + stream_specs(srow, srow) + weight_specs,
        out_specs=[pl.BlockSpec((tt, D_MODEL), lambda i: (i, 0)),
                   pl.BlockSpec((tt * ROW_SLAB, LANES), lambda i: (i, 0)),
                   pl.BlockSpec((TOP_K, tt), lambda i: (0, i)),
                   pl.BlockSpec((TOP_K, tt), lambda i: (0, i)),
                   pl.BlockSpec((TOP_K, tt), lambda i: (0, i)),
                   pl.BlockSpec((nbk, LANES), const)],
        out_shape=[jax.ShapeDtypeStruct((n_total, D_MODEL), F32),
                   jax.ShapeDtypeStruct((n_total * ROW_SLAB, LANES), F32),
                   jax.ShapeDtypeStruct((TOP_K, n_total), I32),
                   jax.ShapeDtypeStruct((TOP_K, n_total), I32),
                   jax.ShapeDtypeStruct((TOP_K, n_total), F32),
                   jax.ShapeDtypeStruct((nbk, LANES), F32)],
        scratch_shapes=[pltpu.VMEM((nbk, LANES), F32)],
        compiler_params=_cparams(("arbitrary",)),
        name="postmix",
    )(xp, op, zp, ysp, gap, gbp, *sample, *weights, earlier)


def _wait_slabs(buf, sem):
    pltpu.make_async_copy(buf, buf, sem).wait()


def _moe_kernel(texp_ref, tph_ref, tsrc_ref, tnv_ref, tfirst_ref, tslot_ref, tnext_ref, otok_ref,
                hn_hbm, wu_hbm, wd_hbm, y_ref, hnv, xbuf, wu_buf, wd_buf, wub, wdb, sem, wsem):
    i = pl.program_id(0)
    tm = MOE_TILE
    rs = ROW_SLAB
    nv = tnv_ref[i]
    ph = tph_ref[i]
    range_rows = hnv.shape[0]

    def weight_copies(e, sl):
        return (pltpu.make_async_copy(wu_hbm.at[e], wu_buf.at[sl], wsem.at[sl]),
                pltpu.make_async_copy(wd_hbm.at[e], wd_buf.at[sl], wsem.at[sl]))

    @pl.when(i == 0)
    def _():
        for p, c in enumerate(weight_copies(texp_ref[0], 0)):
            c.start(priority=p % DMA_QUEUES)

    @pl.when(jnp.logical_and(nv > 0, jnp.logical_or(i == 0, ph != tph_ref[jnp.maximum(i - 1, 0)])))
    def _():
        piece = range_rows // DMA_QUEUES
        loads = [pltpu.make_async_copy(hn_hbm.at[pl.ds(pl.multiple_of(ph * range_rows + p * piece, rs), piece), :],
                                       hnv.at[pl.ds(p * piece, piece), :], sem) for p in range(DMA_QUEUES)]
        for p, c in enumerate(loads):
            c.start(priority=p)
        for c in loads:
            c.wait()

    for sl in range(2):
        @pl.when(jnp.logical_and(jnp.logical_and(nv > 0, tfirst_ref[i] == 1), tslot_ref[i] == sl))
        def _():
            for c in weight_copies(texp_ref[i], sl):
                c.wait()

            @pl.when(tnext_ref[i] >= 0)
            def _():
                for p, c in enumerate(weight_copies(tnext_ref[i], 1 - sl)):
                    c.start(priority=p % DMA_QUEUES)

            wub[...] = wu_buf[sl].astype(BF16)
            wdb[...] = wd_buf[sl].astype(BF16)

    @pl.when(nv == 0)
    def _():
        y_ref[...] = jnp.zeros_like(y_ref)

    @pl.when(nv > 0)
    def _():
        src0 = tsrc_ref[i]
        for r in range(tm):
            tok8 = pl.multiple_of(otok_ref[src0 + r], rs)
            xbuf[pl.ds(r * rs, rs), :] = hnv[pl.ds(tok8, rs), :]
        x = _slab_load(xbuf, tm).astype(BF16)
        hu = jnp.dot(x, wub[...], preferred_element_type=F32)
        gate = hu[:, :EXPERT_FF]
        up = hu[:, EXPERT_FF:]
        act = gate * _sigmoid(gate) * up
        _slab_store(y_ref, jnp.dot(act.astype(BF16), wdb[...], preferred_element_type=F32))


def _moe(hn, w_up, w_down, plan):
    ntiles = plan[0].shape[0]
    grid_spec = pltpu.PrefetchScalarGridSpec(
        num_scalar_prefetch=len(plan),
        grid=(ntiles,),
        in_specs=[pl.BlockSpec(memory_space=pl.ANY),
                  pl.BlockSpec(memory_space=pl.ANY),
                  pl.BlockSpec(memory_space=pl.ANY)],
        out_specs=pl.BlockSpec((MOE_TILE * ROW_SLAB, LANES), lambda i, *_: (i, 0)),
        scratch_shapes=[pltpu.VMEM((hn.shape[0] // MOE_PHASES, LANES), F32),
                        pltpu.VMEM((MOE_TILE * ROW_SLAB, LANES), F32),
                        pltpu.VMEM((2, D_MODEL, 2 * EXPERT_FF), F32),
                        pltpu.VMEM((2, EXPERT_FF, D_MODEL), F32),
                        pltpu.VMEM((D_MODEL, 2 * EXPERT_FF), BF16),
                        pltpu.VMEM((EXPERT_FF, D_MODEL), BF16),
                        pltpu.SemaphoreType.DMA,
                        pltpu.SemaphoreType.DMA((2,))])
    return pl.pallas_call(
        _moe_kernel,
        grid_spec=grid_spec,
        out_shape=jax.ShapeDtypeStruct((ntiles * MOE_TILE * ROW_SLAB, LANES), F32),
        compiler_params=_cparams(("arbitrary",)),
        name="moe",
    )(*plan, hn, w_up, w_down)


def _combine_kernel(pos_ref, x1_ref, ys_hbm, w_ref, nw_ref, outp_ref, outs_ref,
                    ybuf0, ybuf1, sem, *, nblk_p, n_tok):
    i = pl.program_id(0)
    nsteps = pl.num_programs(0)
    tt = x1_ref.shape[0]
    rs = ROW_SLAB
    slot = lax.rem(i, 2)
    ybuf = (ybuf0, ybuf1)

    def start_gather(step, sl):
        base = step * tt
        for r in range(tt * TOP_K):
            j, s = divmod(r, TOP_K)
            p8 = pl.multiple_of(pos_ref[s * n_tok + base + j], rs)
            pltpu.make_async_copy(ys_hbm.at[pl.ds(p8, rs), :], ybuf[sl].at[pl.ds((s * tt + j) * rs, rs), :],
                                  sem.at[sl]).start(priority=min(r % 3, DMA_QUEUES - 1))

    @pl.when(i == 0)
    def _():
        start_gather(0, 0)

    for sl in range(2):
        @pl.when(slot == sl)
        def _():
            _wait_slabs(ybuf[sl], sem.at[sl])
            start_gather(jnp.minimum(i + 1, nsteps - 1), 1 - sl)
            w = w_ref[...]
            y0 = _slab_load(ybuf[sl], tt, 0)
            y1 = _slab_load(ybuf[sl], tt, tt * rs)
            x = x1_ref[...] + w[:, 0:1] * y0 + w[:, 1:2] * y1
            res = x * lax.rsqrt(jnp.mean(x * x, axis=-1, keepdims=True) + RMS_EPS) * nw_ref[...]

            @pl.when(i < nblk_p)
            def _():
                outp_ref[...] = res

            @pl.when(i >= nblk_p)
            def _():
                outs_ref[...] = res

        @pl.when(jnp.logical_and(slot == sl, i == nsteps - 1))
        def _():
            _wait_slabs(ybuf[1 - sl], sem.at[1 - sl])


def _combine(x1, ysorted, pos8, wtok, nw, n_p):
    n = x1.shape[0]
    tt = math.gcd(math.gcd(n_p, n - n_p), COMBINE_TILE)
    nblk_p = n_p // tt
    grid_spec = pltpu.PrefetchScalarGridSpec(
        num_scalar_prefetch=1,
        grid=(n // tt,),
        in_specs=[pl.BlockSpec((tt, D_MODEL), lambda i, *_: (i, 0)),
                  pl.BlockSpec(memory_space=pl.ANY),
                  pl.BlockSpec((tt, TOP_K), lambda i, *_: (i, 0)),
                  pl.BlockSpec((1, D_MODEL), lambda i, *_: (0, 0))],
        out_specs=[pl.BlockSpec((tt, D_MODEL), lambda i, *_: (jnp.minimum(i, nblk_p - 1), 0)),
                   pl.BlockSpec((tt, D_MODEL), lambda i, *_: (jnp.maximum(i - nblk_p, 0), 0))],
        scratch_shapes=[pltpu.VMEM((tt * TOP_K * ROW_SLAB, LANES), F32),
                        pltpu.VMEM((tt * TOP_K * ROW_SLAB, LANES), F32),
                        pltpu.SemaphoreType.DMA((2,))])
    return pl.pallas_call(
        functools.partial(_combine_kernel, nblk_p=nblk_p, n_tok=n),
        grid_spec=grid_spec,
        out_shape=[jax.ShapeDtypeStruct((n_p, D_MODEL), F32),
                   jax.ShapeDtypeStruct((n - n_p, D_MODEL), F32)],
        compiler_params=_cparams(("arbitrary",)),
        name="combine",
    )(pos8, x1, ysorted, wtok, nw)


def _route_plan(bkt, rank, cnt, n_tok):
    tm = MOE_TILE
    n_assign = n_tok * TOP_K
    nbk = MOE_PHASES * N_EXPERTS
    ntiles = n_assign // tm + nbk
    range_tok = n_tok // MOE_PHASES
    b_flat = bkt.T.reshape(n_assign)
    order = jnp.argsort(b_flat, stable=True).astype(I32)
    counts = cnt[:, 0].astype(I32)
    cstart = jnp.cumsum(counts) - counts
    tiles_b = (counts + tm - 1) // tm
    tend = jnp.cumsum(tiles_b)
    tstart = tend - tiles_b
    tile_id = jnp.arange(ntiles, dtype=I32)
    tbk = jnp.minimum(jnp.sum((tile_id[:, None] >= tend[None, :]).astype(I32), axis=1), nbk - 1)
    onehot = (tbk[:, None] == jnp.arange(nbk, dtype=I32)[None, :]).astype(I32)
    pick = lambda v: jnp.sum(onehot * v[None, :], axis=1)
    done = (tile_id - pick(tstart)) * tm
    tnv = jnp.where(tile_id < tend[-1], jnp.clip(pick(counts) - done, 0, tm), 0)
    tsrc = jnp.where(tnv > 0, pick(cstart) + done, 0)
    texp = tbk % N_EXPERTS
    tph = tbk // N_EXPERTS
    nonempty = counts > 0
    bslot = (jnp.cumsum(nonempty.astype(I32)) - 1) % 2
    bidx = jnp.where(nonempty, jnp.arange(nbk, dtype=I32), nbk)
    nxt = jnp.concatenate([lax.cummin(bidx[::-1])[::-1][1:], jnp.full((1,), nbk, I32)])
    bnext = jnp.where(nxt < nbk, nxt % N_EXPERTS, -1)
    tfirst = jnp.logical_and(tnv > 0, done == 0).astype(I32)
    tslot = pick(bslot)
    tnext = pick(bnext)
    otok8 = jnp.concatenate([((order // TOP_K) % range_tok) * ROW_SLAB, jnp.zeros((tm,), I32)])
    plan = tuple(a.astype(I32) for a in (texp, tph, tsrc, tnv, tfirst, tslot, tnext, otok8))
    first = jnp.sum((bkt[:, :, None] == jnp.arange(nbk, dtype=I32)[None, None, :]).astype(I32)
                    * (tstart * tm)[None, None, :], axis=2)
    pos8 = ((first + rank) * ROW_SLAB).reshape(n_assign)
    return plan, pos8.astype(I32)


def _block_diag(m):
    g, a, b = m.shape
    eye = jnp.eye(g, dtype=m.dtype)
    return (eye[:, None, :, None] * m[:, :, None, :]).reshape(g * a, g * b)


def kernel(x_prompt, x_sample, state_conv, state_delta, state_ssm_re, state_ssm_im, norm_mix_w, w_in, conv_w, a_log, dt_bias, head_norm_w, w_a_up, s5_lambda_re, s5_lambda_im, s5_log_step, s5_b_re, s5_b_im, s5_c_re, s5_c_im, s5_d, w_glu, w_b_up, w_o, norm_ffn_w, w_router_coarse, w_router_fine, w_expert_up, w_expert_down, norm_final_w):
    bp, tp, _ = x_prompt.shape
    bs, ts, _ = x_sample.shape
    n_p = bp * tp
    n_s = bs * ts
    n_tok = n_p + n_s
    l = 0

    w = w_in[l].astype(BF16)
    c_ab = W1_COLS + 2 * DN_HEADS
    w_parts = (w[:, :W1_COLS], w[:, c_ab:],
               jnp.concatenate([w[:, W1_COLS:c_ab], jnp.zeros((D_MODEL, LANES - 2 * DN_HEADS), BF16)], axis=1))
    nw_mix = norm_mix_w[l].reshape(1, D_MODEL)
    pad8 = lambda v: jnp.concatenate([v, jnp.zeros((LANES - DN_HEADS,), F32)]).reshape(1, LANES)
    gate_p = jnp.concatenate([pad8(a_log[l]), pad8(dt_bias[l])], axis=0)
    seg = _block_diag(jnp.ones((DN_HEADS, DN_HEAD_DIM, DN_HEAD_DIM), BF16))
    chan_rows = lambda b: jnp.swapaxes(b, 1, 2).reshape(S5_WIDTH, S5_STATE)
    state_rows = lambda c: jnp.tile(jnp.swapaxes(c, 1, 2).reshape(S5_FLAT, S5_GROUP_CH),
                                    (1, LANES // S5_GROUP_CH))
    lam = jnp.concatenate([s5_lambda_re[l].reshape(1, S5_FLAT), s5_lambda_im[l].reshape(1, S5_FLAT),
                           jnp.repeat(s5_log_step[l], S5_STATE).reshape(1, S5_FLAT),
                           jnp.zeros((SUBLANES - 3, S5_FLAT), F32)], axis=0)
    s5_params = (chan_rows(s5_b_re[l]), chan_rows(s5_b_im[l]), lam,
                 state_rows(s5_c_re[l]), state_rows(s5_c_im[l]), s5_d[l].reshape(1, S5_WIDTH))
    hw = jnp.tile(head_norm_w[l], DN_HEADS).reshape(1, DN_WIDTH)
    wr = jnp.concatenate([w_router_fine[l].T, w_router_coarse[l].T,
                          jnp.zeros((ROUTER_ROWS - N_EXPERTS - MOE_GROUPS, D_MODEL), F32)], axis=0)
    pm_weights = (hw, seg, w_a_up[l].astype(BF16), w_glu[l].astype(BF16), w_b_up[l].astype(BF16),
                  w_o[l].astype(BF16), norm_ffn_w[l].reshape(1, D_MODEL), wr)

    xp2 = x_prompt.reshape(n_p, D_MODEL)
    q_p, k_p, v_p, gates_p, conv_p, z_p, u_p, ga_p, gb_p = _inprep(
        xp2, nw_mix, w_parts, jnp.zeros((bp, SUBLANES, QKV_DIM), F32), conv_w[l], gate_p, seg, bp, 1)
    o_p, delta_p = _delta_prompt(q_p, k_p, v_p, gates_p, bp)
    ys_p, h_p = _s5(u_p, s5_params, jnp.zeros((bp, 2 * S5_FLAT), F32), bp, True)

    xs2 = jnp.swapaxes(x_sample, 0, 1).reshape(n_s, D_MODEL)
    cinit_s = jnp.swapaxes(state_conv[l], 0, 1).reshape(1, (CONV_W - 1) * bs, QKV_DIM)
    q_s, k_s, v_s, gate_s, conv_s, z_s, u_s, ga_s, gb_s = _inprep(
        xs2, nw_mix, w_parts, cinit_s, conv_w[l], gate_p, seg, 1, bs)
    s0t = jnp.transpose(state_delta[l], (1, 2, 3, 0)).reshape(DN_HEADS * DN_HEAD_DIM * DN_HEAD_DIM, bs)
    o_s, delta_st = _delta_sample(q_s, k_s, v_s, gate_s, s0t, bs, ts)
    delta_s = jnp.transpose(delta_st.reshape(DN_HEADS, DN_HEAD_DIM, DN_HEAD_DIM, bs), (3, 0, 1, 2))
    h0_s = jnp.concatenate([state_ssm_re[l].reshape(bs, S5_FLAT), state_ssm_im[l].reshape(bs, S5_FLAT)], axis=1)
    ys_s, h_s = _s5(u_s, s5_params, h0_s, bs, False)
    x1, hn, bkt, rank, rw, cnt = _postmix((xp2, o_p, z_p, ys_p, ga_p, gb_p), (xs2, o_s, z_s, ys_s, ga_s, gb_s),
                                          pm_weights, bp)

    plan, pos8 = _route_plan(bkt, rank, cnt, n_tok)
    ysorted = _moe(hn, w_expert_up[l], w_expert_down[l], plan)
    y_p, y_s = _combine(x1, ysorted, pos8, rw.T, norm_final_w.reshape(1, D_MODEL), n_p)

    y_prompt = y_p.reshape(bp, tp, D_MODEL)
    y_sample = jnp.swapaxes(y_s.reshape(ts, bs, D_MODEL), 0, 1)
    conv_sample = jnp.swapaxes(conv_s.reshape(CONV_W - 1, bs, QKV_DIM), 0, 1)
    return (y_prompt, y_sample,
            conv_p[None], delta_p[None],
            h_p[:, :S5_FLAT].reshape(1, bp, S5_GROUPS, S5_STATE), h_p[:, S5_FLAT:].reshape(1, bp, S5_GROUPS, S5_STATE),
            conv_sample[None], delta_s[None],
            h_s[:, :S5_FLAT].reshape(1, bs, S5_GROUPS, S5_STATE), h_s[:, S5_FLAT:].reshape(1, bs, S5_GROUPS, S5_STATE))
```

```python
import functools
import math

import jax
import jax.numpy as jnp
import numpy as np
from jax import lax
from jax.experimental import pallas as pl
from jax.experimental.pallas import tpu as pltpu

F32 = jnp.float32
BF16 = jnp.bfloat16
I32 = jnp.int32

D_MODEL = 1024
DN_HEADS = 8
DN_HEAD_DIM = 64
DN_WIDTH = DN_HEADS * DN_HEAD_DIM
QKV_DIM = 3 * DN_WIDTH
CONV_W = 4
DN_CHUNK = 64
S5_GROUP_CH = 16
S5_WIDTH = D_MODEL // 2
S5_GROUPS = S5_WIDTH // S5_GROUP_CH
S5_STATE = 64
S5_FLAT = S5_GROUPS * S5_STATE
MOE_GROUPS = 4
EXPERTS_PER_GROUP = 8
N_EXPERTS = MOE_GROUPS * EXPERTS_PER_GROUP
TOP_K = 2
EXPERT_FF = 256
RMS_EPS = 1e-6
L2_EPS = 1e-6

LANES = 128
SUBLANES = 8
VMEM_LIMIT = 56 * 1024 * 1024

W1_COLS = QKV_DIM + DN_WIDTH
W2_COLS = S5_WIDTH + 2 * D_MODEL

ROW_TILE = 512
INPREP_PARTS = 2
POSTMIX_PARTS = 2
MOE_TILE = 256
MOE_PHASES = 2
COMBINE_TILE = 512
DMA_QUEUES = 2
DELTA_SUBCHUNKS = 4
S5_SUPER = 2
S5_TILE_ROWS = 1024
S5_SCAN_SPLIT = 2
ROUTER_ROWS = 40


def _mm(a, b):
    return jnp.dot(a.astype(BF16), b.astype(BF16), preferred_element_type=F32)


def _mm_nt(a, b):
    return lax.dot_general(a.astype(BF16), b.astype(BF16), (((1,), (1,)), ((), ())),
                           preferred_element_type=F32)


def _sigmoid(x):
    return 0.5 * jnp.tanh(0.5 * x) + 0.5


def _cparams(sem):
    return pltpu.CompilerParams(dimension_semantics=sem, vmem_limit_bytes=VMEM_LIMIT)


ROW_SLAB = D_MODEL // LANES


def _slab_load(ref, rows, first=0, pitch=ROW_SLAB):
    return jnp.concatenate([ref[pl.ds(first + j, rows, stride=pitch), :] for j in range(ROW_SLAB)], axis=1)


def _slab_store(ref, x, first=0):
    for j in range(ROW_SLAB):
        ref[pl.ds(first * ROW_SLAB + j, x.shape[0], stride=ROW_SLAB), :] = x[:, j * LANES:(j + 1) * LANES]


def _softplus(x):
    return jnp.maximum(x, 0.0) + jnp.log1p(jnp.exp(-jnp.abs(x)))


def _inprep_kernel(x_ref, nw_ref, w1_ref, w2_ref, wab_ref, cinit_ref, cw_ref, gp_ref, seg_ref,
                   q_ref, k_ref, v_ref, gate_ref, cnew_ref, z_ref, u_ref, ga_ref, gb_ref, xp_ref,
                   *, shift, rc, rows):
    @pl.when(pl.program_id(1) == 0)
    def _():
        xp_ref[0:rc, :] = cinit_ref[0]

    seg = seg_ref[...]
    pr = rows // INPREP_PARTS

    def part_stages(part):
        rs = slice(part * pr, (part + 1) * pr)
        x = x_ref[rs, :]
        h = x * lax.rsqrt(jnp.mean(x * x, axis=-1, keepdims=True) + RMS_EPS) * nw_ref[...]
        hb = h.astype(BF16)

        def proj(w_ref, lo, hi):
            return jnp.dot(hb, w_ref[:, lo:hi], preferred_element_type=F32)

        xp_ref[rc + part * pr:rc + (part + 1) * pr, :] = proj(w1_ref, 0, QKV_DIM)
        ab = proj(wab_ref, 0, LANES)
        yield
        z_ref[rs, :] = proj(w1_ref, QKV_DIM, W1_COLS).astype(z_ref.dtype)
        u_ref[rs, :] = proj(w2_ref, 0, S5_WIDTH).astype(u_ref.dtype)
        acc = None
        for i in range(CONV_W):
            lo = rc + part * pr + (i - (CONV_W - 1)) * shift
            term = xp_ref[lo:lo + pr, :] * cw_ref[i:i + 1, :]
            acc = term if acc is None else acc + term
        y = acc * _sigmoid(acc)
        yield
        ga_ref[rs, :] = proj(w2_ref, S5_WIDTH, S5_WIDTH + D_MODEL).astype(ga_ref.dtype)
        q = y[:, 0:DN_WIDTH]
        k = y[:, DN_WIDTH:2 * DN_WIDTH]
        q_ref[rs, :] = q * lax.rsqrt(jnp.dot((q * q).astype(BF16), seg, preferred_element_type=F32) + L2_EPS)
        k_ref[rs, :] = k * lax.rsqrt(jnp.dot((k * k).astype(BF16), seg, preferred_element_type=F32) + L2_EPS)
        v_ref[rs, :] = y[:, 2 * DN_WIDTH:]
        yield
        gb_ref[rs, :] = proj(w2_ref, S5_WIDTH + D_MODEL, W2_COLS).astype(gb_ref.dtype)
        g = -jnp.exp(gp_ref[0:1, :]) * _softplus(ab + gp_ref[1:2, :])
        beta = _sigmoid(ab)
        lane = lax.broadcasted_iota(I32, ab.shape, 1)
        gate_ref[rs, :] = jnp.where(lane < DN_HEADS, g, beta)

    live = []
    pending = [part_stages(p) for p in range(INPREP_PARTS)]
    while live or pending:
        if pending:
            live.append(pending.pop(0))
        live = [g for g in live if next(g, StopIteration) is not StopIteration]

    keep = (CONV_W - 1) * shift
    cnew_ref[0] = xp_ref[rc + rows - keep:rc + rows, :]
    xp_ref[0:rc, :] = xp_ref[rows:rows + rc, :]


def _inprep(x2d, nw, w_parts, cinit, conv_w, gate_p, seg, nb, shift):
    n = x2d.shape[0]
    r = n // nb
    rows = min(ROW_TILE, r)
    nt = r // rows
    rc = cinit.shape[1]
    keep = (CONV_W - 1) * shift
    row = lambda b, i: (b * nt + i, 0)
    const = lambda b, i: (0, 0)
    kern = functools.partial(_inprep_kernel, shift=shift, rc=rc, rows=rows)
    outs = pl.pallas_call(
        kern,
        grid=(nb, nt),
        in_specs=[pl.BlockSpec((rows, D_MODEL), row),
                  pl.BlockSpec((1, D_MODEL), const),
                  pl.BlockSpec((D_MODEL, W1_COLS), const),
                  pl.BlockSpec((D_MODEL, W2_COLS), const),
                  pl.BlockSpec((D_MODEL, LANES), const),
                  pl.BlockSpec((1, rc, QKV_DIM), lambda b, i: (b, 0, 0)),
                  pl.BlockSpec((CONV_W, QKV_DIM), const),
                  pl.BlockSpec((2, LANES), const),
                  pl.BlockSpec((DN_WIDTH, DN_WIDTH), const)],
        out_specs=[pl.BlockSpec((rows, DN_WIDTH), row),
                   pl.BlockSpec((rows, DN_WIDTH), row),
                   pl.BlockSpec((rows, DN_WIDTH), row),
                   pl.BlockSpec((rows, LANES), row),
                   pl.BlockSpec((1, keep, QKV_DIM), lambda b, i: (b, 0, 0)),
                   pl.BlockSpec((rows, DN_WIDTH), row),
                   pl.BlockSpec((rows, S5_WIDTH), lambda b, i: (i, b)),
                   pl.BlockSpec((rows, D_MODEL), row),
                   pl.BlockSpec((rows, D_MODEL), row)],
        out_shape=[jax.ShapeDtypeStruct((n, DN_WIDTH), F32),
                   jax.ShapeDtypeStruct((n, DN_WIDTH), F32),
                   jax.ShapeDtypeStruct((n, DN_WIDTH), F32),
                   jax.ShapeDtypeStruct((n, LANES), F32),
                   jax.ShapeDtypeStruct((nb, keep, QKV_DIM), F32),
                   jax.ShapeDtypeStruct((n, DN_WIDTH), BF16),
                   jax.ShapeDtypeStruct((r, nb * S5_WIDTH), BF16),
                   jax.ShapeDtypeStruct((n, D_MODEL), BF16),
                   jax.ShapeDtypeStruct((n, D_MODEL), BF16)],
        scratch_shapes=[pltpu.VMEM((rc + rows, QKV_DIM), F32)],
        compiler_params=_cparams(("arbitrary", "arbitrary")),
        name="inprep",
    )(x2d, nw, *w_parts, cinit, conv_w, gate_p, seg)
    return outs


def _delta_home(low, h, x, other=0.0):
    return jnp.where(low, x, other) if h % 2 == 0 else jnp.where(low, other, x)


def _delta_prepare(q_ref, k_ref, v_ref, gate_ref, tril_ref, bufs, *, nsub):
    sol_buf, wq_buf, qk_buf, kdec_buf, dl_buf = bufs
    c = DN_CHUNK
    dk = DN_HEAD_DIM

    def home(h, x, other=0.0):
        return _delta_home(low, h, x, other)

    rowi2 = lax.broadcasted_iota(I32, (c, 2 * c), 0)
    lane2 = lax.broadcasted_iota(I32, (c, 2 * c), 1)
    coli2 = lane2 & (c - 1)
    causal2 = rowi2 >= coli2
    strict2 = rowi2 > coli2
    low = lane2 < dk
    tril = tril_ref[...]
    pairs = [(j, h) for j in range(nsub) for h in range(DN_HEADS)]
    units = [(j, pr) for j in range(nsub) for pr in range(DN_HEADS // 2)]
    rows = [slice(j * c, (j + 1) * c) for j in range(nsub)]
    gate = [gate_ref[rows[j], :] for j in range(nsub)]
    gc_all = [_split3_dot_left(tril, gate[j]) for j in range(nsub)]
    gc_t = [gc_all[j].T for j in range(nsub)]

    def block(ref, j, pr):
        return ref[rows[j], pr * LANES:(pr + 1) * LANES]

    gfull = {(j, h): jnp.broadcast_to(gc_all[j][:, h:h + 1], (c, 2 * c)) for j, h in pairs}
    g2 = {(j, pr): jnp.where(low, gfull[j, 2 * pr], gfull[j, 2 * pr + 1]) for j, pr in units}
    b2 = {(j, pr): jnp.where(low, gate[j][:, DN_HEADS + 2 * pr:DN_HEADS + 2 * pr + 1],
                             gate[j][:, DN_HEADS + 2 * pr + 1:DN_HEADS + 2 * pr + 2]) for j, pr in units}
    kp = {u: block(k_ref, *u) for u in units}
    qp = {u: block(q_ref, *u) * (dk ** -0.5) for u in units}
    egc2 = {u: jnp.exp(g2[u]) for u in units}
    kb2 = {u: kp[u] * b2[u] for u in units}
    vb2 = {u: block(v_ref, *u) * b2[u] for u in units}
    kw2s = {u: pltpu.roll(kb2[u] * egc2[u], dk, axis=1) for u in units}
    qd2 = {u: qp[u] * egc2[u] for u in units}
    glast2 = {u: g2[u][c - 1:c, :] for u in units}
    kdec_t2 = {u: (kp[u] * jnp.exp(glast2[u] - g2[u])).T for u in units}
    dlast2 = {u: jnp.exp(glast2[u]) for u in units}
    kk = {u: jnp.concatenate([kp[u], kp[u]], axis=0) for u in units}
    yield

    grow2 = {(j, h): jnp.concatenate([gc_t[j][h:h + 1, :], gc_t[j][h:h + 1, :]], axis=1) for j, h in pairs}
    decay = {p: jnp.where(causal2, jnp.exp(jnp.where(causal2, gfull[p] - grow2[p], 0.0)), 0.0) for p in pairs}
    gram = {(j, h): _mm_nt(jnp.concatenate([home(h, kb2[j, h // 2]), home(h, qp[j, h // 2])], axis=0), kk[j, h // 2])
            for j, h in pairs}
    mat = {p: jnp.where(strict2, gram[p][:c] * decay[p], 0.0).astype(BF16) for p in pairs}
    qk = {p: jnp.where(causal2, gram[p][c:] * decay[p], 0.0) for p in pairs}
    sol = {(j, h): home(h, vb2[j, h // 2], kw2s[j, h // 2]) for j, h in pairs}
    yield
    levels = int(math.log2(c))
    zeros2 = jnp.zeros((c, 2 * c), BF16)
    for lvl in range(levels):
        hi = {p: sol[p].astype(BF16) for p in pairs}
        lo = {p: (sol[p] - hi[p].astype(F32)).astype(BF16) for p in pairs}
        if lvl < levels - 1:
            y = {p: jnp.dot(mat[p], jnp.concatenate([jnp.concatenate([hi[p], mat[p]], axis=1),
                                                     jnp.concatenate([lo[p], zeros2], axis=1)], axis=0),
                            preferred_element_type=F32) for p in pairs}
            mat = {p: y[p][:, 2 * dk:].astype(BF16) for p in pairs}
            upd = {p: y[p][:, :2 * dk] for p in pairs}
        else:
            upd = {p: jnp.dot(mat[p], jnp.concatenate([hi[p], lo[p]], axis=0), preferred_element_type=F32)
                   for p in pairs}
        sol = {p: (sol[p] - upd[p]) if lvl == 0 else (sol[p] + upd[p]) for p in pairs}
        yield
    for j, h in pairs:
        n = j * DN_HEADS + h
        sol_buf[n] = sol[j, h]
        wq_buf[n] = jnp.concatenate([home(h, 0.0, sol[j, h]), home(h, qd2[j, h // 2])], axis=0).astype(BF16)
        qk_buf[n] = qk[j, h].astype(BF16)
    for j, pr in units:
        n = j * (DN_HEADS // 2) + pr
        kdec_buf[n] = kdec_t2[j, pr].astype(BF16)
        dl_buf[n] = jnp.broadcast_to(dlast2[j, pr], (SUBLANES, LANES))


def _delta_apply(bufs, o_ref, s_ref, *, nsub):
    sol_buf, wq_buf, qk_buf, kdec_buf, dl_buf = bufs
    c = DN_CHUNK
    dk = DN_HEAD_DIM
    heads = range(DN_HEADS)
    low = lax.broadcasted_iota(I32, (c, 2 * c), 1) < dk
    s = [s_ref[h] for h in heads]
    for j in range(nsub):
        ws, v_new, o_h = [], [], []
        for h in heads:
            n = j * DN_HEADS + h
            ws.append(jnp.dot(wq_buf[n], jnp.concatenate([s[h], s[h]], axis=0).astype(BF16),
                              preferred_element_type=F32))
        yield
        for h in heads:
            v_new.append(sol_buf[j * DN_HEADS + h] - ws[h][:c])
        for h in heads:
            o_h.append(ws[h][c:] + jnp.dot(qk_buf[j * DN_HEADS + h][:, :c], v_new[h].astype(BF16),
                                           preferred_element_type=F32))
        for pr in range(DN_HEADS // 2):
            o_ref[j * c:(j + 1) * c, pr * LANES:(pr + 1) * LANES] = jnp.where(low, o_h[2 * pr], o_h[2 * pr + 1])
        nxt = []
        for h in heads:
            u = j * (DN_HEADS // 2) + h // 2
            kdt = kdec_buf[u][(h % 2) * dk:(h % 2 + 1) * dk, :]
            d = dl_buf[u][0:1, :]
            nxt.append(_delta_home(low, h, s[h] * d + jnp.dot(kdt, v_new[h].astype(BF16),
                                                               preferred_element_type=F32)))
        s = nxt
        yield
    for h in heads:
        s_ref[h] = s[h]


def _delta_chunk_kernel(q_ref, k_ref, v_ref, gate_ref, tril_ref, o_ref, sfin_ref, s_ref, *bufs, nsub, nc):
    i = pl.program_id(0)
    half = len(bufs) // 2
    sets = (bufs[:half], bufs[half:])
    local = lax.rem(jnp.maximum(i - 1, 0), nc)

    @pl.when(i == 0)
    def _():
        for b in sets[1]:
            b[...] = jnp.zeros_like(b)

    @pl.when(local == 0)
    def _():
        s_ref[...] = jnp.zeros_like(s_ref)

    for par in range(2):
        @pl.when(lax.rem(i, 2) == par)
        def _(par=par):
            parts = [_delta_prepare(q_ref, k_ref, v_ref, gate_ref, tril_ref, sets[par], nsub=nsub),
                     _delta_apply(sets[1 - par], o_ref, s_ref, nsub=nsub)]
            while parts:
                parts = [g for g in parts if next(g, StopIteration) is not StopIteration]

    @pl.when(jnp.logical_and(i >= 1, local == nc - 1))
    def _():
        dk = DN_HEAD_DIM
        for h in range(DN_HEADS):
            sfin_ref[0, h] = s_ref[h][:, (h % 2) * dk:(h % 2 + 1) * dk]


def _split3_dot_left(b01, a):
    a1 = a.astype(BF16)
    r1 = a - a1.astype(F32)
    a2 = r1.astype(BF16)
    a3 = (r1 - a2.astype(F32)).astype(BF16)
    out = jnp.dot(b01, a3, preferred_element_type=F32)
    out = out + jnp.dot(b01, a2, preferred_element_type=F32)
    return out + jnp.dot(b01, a1, preferred_element_type=F32)


def _delta_prompt(q, k, v, gate, nb):
    n = q.shape[0]
    t = n // nb
    c = DN_CHUNK
    nsub = DELTA_SUBCHUNKS
    rows = nsub * c
    nc = t // rows
    nblk = nb * nc
    row_in = lambda i: (jnp.minimum(i, nblk - 1), 0)
    row_out = lambda i: (jnp.maximum(i - 1, 0), 0)
    tril = jnp.tril(jnp.ones((c, c), F32)).astype(BF16)
    nh = nsub * DN_HEADS
    npair = nsub * DN_HEADS // 2
    buf_set = [pltpu.VMEM((nh, c, 2 * DN_HEAD_DIM), F32),
               pltpu.VMEM((nh, 2 * c, 2 * DN_HEAD_DIM), BF16),
               pltpu.VMEM((nh, c, 2 * c), BF16),
               pltpu.VMEM((npair, 2 * DN_HEAD_DIM, c), BF16),
               pltpu.VMEM((npair, SUBLANES, LANES), F32)]
    return pl.pallas_call(
        functools.partial(_delta_chunk_kernel, nsub=nsub, nc=nc),
        grid=(nblk + 1,),
        in_specs=[pl.BlockSpec((rows, DN_WIDTH), row_in),
                  pl.BlockSpec((rows, DN_WIDTH), row_in),
                  pl.BlockSpec((rows, DN_WIDTH), row_in),
                  pl.BlockSpec((rows, LANES), row_in),
                  pl.BlockSpec((c, c), lambda i: (0, 0))],
        out_specs=[pl.BlockSpec((rows, DN_WIDTH), row_out),
                   pl.BlockSpec((1, DN_HEADS, DN_HEAD_DIM, DN_HEAD_DIM),
                                lambda i: (jnp.maximum(i - 1, 0) // nc, 0, 0, 0))],
        out_shape=[jax.ShapeDtypeStruct((n, DN_WIDTH), F32),
                   jax.ShapeDtypeStruct((nb, DN_HEADS, DN_HEAD_DIM, DN_HEAD_DIM), F32)],
        scratch_shapes=[pltpu.VMEM((DN_HEADS, DN_HEAD_DIM, 2 * DN_HEAD_DIM), F32)] + buf_set + buf_set,
        compiler_params=_cparams(("arbitrary",)),
        name="delta_prompt",
    )(q, k, v, gate, tril)


def _delta_step_kernel(q_ref, k_ref, v_ref, gate_ref, s0_ref, o_ref, s_ref, kt_ref, qt_ref, gt_ref, *, nt, nb):
    dk = DN_HEAD_DIM
    p = pl.program_id(0)
    for t in range(nt):
        rs = slice(t * nb, (t + 1) * nb)
        gt_ref[...] = gate_ref[rs, :].T
        kt_ref[...] = k_ref[rs, :].T
        qt_ref[...] = (q_ref[rs, :] * (dk ** -0.5)).T
        vt = v_ref[rs, :].T
        src = s0_ref if t == 0 else s_ref
        o_heads = []
        for j in range(2):
            a = jnp.exp(gt_ref[pl.ds(2 * p + j, 1), :])
            beta = gt_ref[pl.ds(2 * p + j + DN_HEADS, 1), :]
            base = j * dk * dk

            def k_dot_s(d, acc, j=j, base=base, src=src):
                sd = src[pl.ds(pl.multiple_of(base + d * dk, dk), dk), :]
                return acc + kt_ref[pl.ds(j * dk + d, 1), :] * sd

            ks = lax.fori_loop(0, dk, k_dot_s, jnp.zeros((dk, nb), F32), unroll=4)
            delta = beta * (vt[j * dk:(j + 1) * dk, :] - a * ks)

            def update(d, acc, j=j, base=base, src=src, a=a, delta=delta):
                r0 = pl.multiple_of(base + d * dk, dk)
                sn = a * src[pl.ds(r0, dk), :] + kt_ref[pl.ds(j * dk + d, 1), :] * delta
                s_ref[pl.ds(r0, dk), :] = sn
                return acc + qt_ref[pl.ds(j * dk + d, 1), :] * sn

            o_heads.append(lax.fori_loop(0, dk, update, jnp.zeros((dk, nb), F32), unroll=4))
        o_ref[rs, :] = jnp.concatenate(o_heads, axis=0).T


def _delta_sample(q, k, v, gate, s0t, nb, nt):
    dk = DN_HEAD_DIM
    flat = dk * dk
    n = nt * nb
    kern = functools.partial(_delta_step_kernel, nt=nt, nb=nb)
    pair = lambda p: (0, p)
    return pl.pallas_call(
        kern,
        grid=(DN_HEADS // 2,),
        in_specs=[pl.BlockSpec((n, LANES), pair),
                  pl.BlockSpec((n, LANES), pair),
                  pl.BlockSpec((n, LANES), pair),
                  pl.BlockSpec((n, LANES), lambda p: (0, 0)),
                  pl.BlockSpec((2 * flat, nb), lambda p: (p, 0))],
        out_specs=[pl.BlockSpec((n, LANES), pair),
                   pl.BlockSpec((2 * flat, nb), lambda p: (p, 0))],
        out_shape=[jax.ShapeDtypeStruct((n, DN_WIDTH), F32),
                   jax.ShapeDtypeStruct((DN_HEADS * flat, nb), F32)],
        scratch_shapes=[pltpu.VMEM((LANES, nb), F32),
                        pltpu.VMEM((LANES, nb), F32),
                        pltpu.VMEM((LANES, nb), F32)],
        compiler_params=_cparams(("arbitrary",)),
        name="delta_sample",
    )(q, k, v, gate, s0t)


def _s5_kernel(u_ref, btre_ref, btim_ref, lam_ref, ctre_ref, ctim_ref, d_ref, h0_ref, y_ref, hfin_ref,
               bw_ref, c_ref, ab_ref, x_ref, h_ref, ru_ref, ry_ref, *, nb, tt, wide):
    p2 = S5_FLAT

    @pl.when(pl.program_id(0) == 0)
    def _():
        lr = lam_ref[0:1, :]
        li = lam_ref[1:2, :]
        dt = jnp.exp(lam_ref[2:3, :])
        mag = jnp.exp(lr * dt)
        ab_re = mag * jnp.cos(li * dt)
        ab_im = mag * jnp.sin(li * dt)
        den = lr * lr + li * li
        nr = ab_re - 1.0
        ni = ab_im
        f_re = (nr * lr + ni * li) / den
        f_im = (ni * lr - nr * li) / den
        ab_ref[0:1, :] = ab_re
        ab_ref[1:2, :] = ab_im
        gpl = LANES // S5_STATE
        ch_g = lax.broadcasted_iota(I32, (S5_WIDTH, LANES), 0) // S5_GROUP_CH
        lane_g = lax.broadcasted_iota(I32, (S5_WIDTH, LANES), 1) // S5_STATE
        bre2 = jnp.concatenate([btre_ref[...]] * gpl, axis=1)
        bim2 = jnp.concatenate([btim_ref[...]] * gpl, axis=1)
        for j in range(p2 // LANES):
            cols = slice(j * LANES, (j + 1) * LANES)
            own = ch_g == gpl * j + lane_g
            bre = jnp.where(own, bre2, 0.0)
            bim = jnp.where(own, bim2, 0.0)
            bw_ref[:, cols] = (bre * f_re[:, cols] - bim * f_im[:, cols]).astype(BF16)
            bw_ref[:, p2 + j * LANES:p2 + (j + 1) * LANES] = (bim * f_re[:, cols] + bre * f_im[:, cols]).astype(BF16)
        cpl = LANES // S5_GROUP_CH
        st_g = lax.broadcasted_iota(I32, (p2, LANES), 0) // S5_STATE
        lane_cg = lax.broadcasted_iota(I32, (p2, LANES), 1) // S5_GROUP_CH
        for j in range(S5_WIDTH // LANES):
            cols = slice(j * LANES, (j + 1) * LANES)
            own = st_g == cpl * j + lane_cg
            c_ref[0:p2, cols] = jnp.where(own, ctre_ref[...], 0.0).astype(BF16)
            c_ref[p2:2 * p2, cols] = jnp.where(own, -ctim_ref[...], 0.0).astype(BF16)
        h_ref[...] = h0_ref[...]

    nck = S5_WIDTH // LANES
    if wide:
        for b in range(nb):
            for ck in range(nck):
                lo = b * S5_WIDTH + ck * LANES
                ru_ref[ck, pl.ds(b, tt, stride=nb), :] = u_ref[:, lo:lo + LANES].astype(F32)
        u = jnp.concatenate([ru_ref[ck] for ck in range(nck)], axis=1)
    else:
        u = u_ref[...].astype(F32)
    ub = u.astype(BF16)
    cw = S5_WIDTH // S5_SUPER
    sw = S5_FLAT // S5_SUPER
    for part in (0, p2):
        for b in range(S5_SUPER):
            x_ref[:, part + b * sw:part + (b + 1) * sw] = jnp.dot(
                ub[:, b * cw:(b + 1) * cw], bw_ref[b * cw:(b + 1) * cw, part + b * sw:part + (b + 1) * sw],
                preferred_element_type=F32)
    a_re = ab_ref[0:1, :]
    a_im = ab_ref[1:2, :]

    if nb == SUBLANES:
        wsl = p2 // S5_SCAN_SPLIT
        for sp in range(S5_SCAN_SPLIT):
            c0 = sp * wsl
            are = jnp.broadcast_to(a_re[:, c0:c0 + wsl], (nb, wsl))
            aim = jnp.broadcast_to(a_im[:, c0:c0 + wsl], (nb, wsl))

            def step(t, carry, c0=c0, are=are, aim=aim):
                hr, hi = carry
                r0 = pl.multiple_of(t * nb, nb)
                nr = are * hr - aim * hi + x_ref[pl.ds(r0, nb), c0:c0 + wsl]
                ni = are * hi + aim * hr + x_ref[pl.ds(r0, nb), p2 + c0:p2 + c0 + wsl]
                x_ref[pl.ds(r0, nb), c0:c0 + wsl] = nr
                x_ref[pl.ds(r0, nb), p2 + c0:p2 + c0 + wsl] = ni
                return nr, ni

            hr, hi = lax.fori_loop(0, tt, step, (h_ref[:, c0:c0 + wsl], h_ref[:, p2 + c0:p2 + c0 + wsl]),
                                   unroll=4)
            h_ref[:, c0:c0 + wsl] = hr
            h_ref[:, p2 + c0:p2 + c0 + wsl] = hi
    else:
        for t in range(tt):
            rs = slice(t * nb, (t + 1) * nb)
            hr = h_ref[:, 0:p2]
            hi = h_ref[:, p2:2 * p2]
            nr = a_re * hr - a_im * hi + x_ref[rs, 0:p2]
            ni = a_re * hi + a_im * hr + x_ref[rs, p2:2 * p2]
            h_ref[:, 0:p2] = nr
            h_ref[:, p2:2 * p2] = ni
            x_ref[rs, 0:p2] = nr
            x_ref[rs, p2:2 * p2] = ni

    for b in range(S5_SUPER):
        cols = slice(b * cw, (b + 1) * cw)
        y = None
        for part in (0, p2):
            rws = slice(part + b * sw, part + (b + 1) * sw)
            term = jnp.dot(x_ref[:, rws].astype(BF16), c_ref[rws, cols], preferred_element_type=F32)
            y = term if y is None else y + term
        if wide:
            skip = jnp.concatenate([ru_ref[b * (cw // LANES) + ck] for ck in range(cw // LANES)], axis=1)
        else:
            skip = u[:, cols]
        y = y + d_ref[:, cols] * skip
        if wide:
            for ck in range(cw // LANES):
                ry_ref[b * (cw // LANES) + ck] = y[:, ck * LANES:(ck + 1) * LANES]
        else:
            y_ref[:, cols] = y
    if wide:
        for b in range(nb):
            for ck in range(nck):
                lo = b * S5_WIDTH + ck * LANES
                y_ref[:, lo:lo + LANES] = ry_ref[ck, pl.ds(b, tt, stride=nb), :]
    hfin_ref[...] = h_ref[...]


def _s5(u, params, h0, nb, wide):
    btre, btim, lam, ctre, ctim, dvec = params
    t = u.shape[0] if wide else u.shape[0] // nb
    tt = min(S5_TILE_ROWS // nb, t)
    rows = tt * nb
    const = lambda i: (0, 0)
    kern = functools.partial(_s5_kernel, nb=nb, tt=tt, wide=wide)
    io_block = (tt, nb * S5_WIDTH) if wide else (rows, S5_WIDTH)
    return pl.pallas_call(
        kern,
        grid=(t // tt,),
        in_specs=[pl.BlockSpec(io_block, lambda i: (i, 0)),
                  pl.BlockSpec((S5_WIDTH, S5_STATE), const),
                  pl.BlockSpec((S5_WIDTH, S5_STATE), const),
                  pl.BlockSpec((SUBLANES, S5_FLAT), const),
                  pl.BlockSpec((S5_FLAT, LANES), const),
                  pl.BlockSpec((S5_FLAT, LANES), const),
                  pl.BlockSpec((1, S5_WIDTH), const),
                  pl.BlockSpec((nb, 2 * S5_FLAT), const)],
        out_specs=[pl.BlockSpec(io_block, lambda i: (i, 0)),
                   pl.BlockSpec((nb, 2 * S5_FLAT), const)],
        out_shape=[jax.ShapeDtypeStruct(u.shape, F32),
                   jax.ShapeDtypeStruct((nb, 2 * S5_FLAT), F32)],
        scratch_shapes=[pltpu.VMEM((S5_WIDTH, 2 * S5_FLAT), BF16),
                        pltpu.VMEM((2 * S5_FLAT, S5_WIDTH), BF16),
                        pltpu.VMEM((SUBLANES, S5_FLAT), F32),
                        pltpu.VMEM((rows, 2 * S5_FLAT), F32),
                        pltpu.VMEM((nb, 2 * S5_FLAT), F32),
                        pltpu.VMEM((S5_WIDTH // LANES, rows, LANES), F32),
                        pltpu.VMEM((S5_WIDTH // LANES, rows, LANES), F32)],
        compiler_params=_cparams(("arbitrary",)),
        name="s5",
    )(u, btre, btim, lam, ctre, ctim, dvec, h0)


def _postmix_kernel(xp_ref, op_ref, zp_ref, ysp_ref, gap_ref, gbp_ref,
                    xs_ref, os_ref, zs_ref, yss_ref, gas_ref, gbs_ref, *rest, nblk_p, range_tok):
    carry_ref = rest[-1]

    @pl.when(pl.program_id(0) == 0)
    def _():
        carry_ref[...] = jnp.zeros_like(carry_ref)

    @pl.when(pl.program_id(0) < nblk_p)
    def _():
        _postmix_body(xp_ref, op_ref, zp_ref, ysp_ref, gap_ref, gbp_ref, *rest, range_tok=range_tok)

    @pl.when(pl.program_id(0) >= nblk_p)
    def _():
        _postmix_body(xs_ref, os_ref, zs_ref, yss_ref, gas_ref, gbs_ref, *rest, range_tok=range_tok)


def _postmix_body(x_ref, o_ref, z_ref, ys_ref, ga_ref, gb_ref, hw_ref, seg_ref, wa_ref, wglu_ref, wb_ref,
                  wo_ref, nf_ref, wr_ref, su_ref, x1_ref, hn_ref, bkt_ref, rank_ref, rw_ref, cnt_ref, carry_ref,
                  *, range_tok):
    rows = x_ref.shape[0]
    pr = rows // POSTMIX_PARTS
    parts = [_postmix_part(p, pr, x_ref, o_ref, z_ref, ys_ref, ga_ref, gb_ref, hw_ref, seg_ref, wa_ref, wglu_ref,
                           wb_ref, wo_ref, nf_ref, wr_ref, su_ref, x1_ref, hn_ref, bkt_ref, rank_ref, rw_ref,
                           carry_ref, range_tok) for p in range(POSTMIX_PARTS)]
    live = []
    while live or parts:
        if parts:
            live.append(parts.pop(0))
        live = [g for g in live if next(g, StopIteration) is not StopIteration]
    cnt_ref[...] = carry_ref[...]


def _postmix_part(part, pr, x_ref, o_ref, z_ref, ys_ref, ga_ref, gb_ref, hw_ref, seg_ref, wa_ref, wglu_ref, wb_ref,
                  wo_ref, nf_ref, wr_ref, su_ref, x1_ref, hn_ref, bkt_ref, rank_ref, rw_ref, carry_ref, range_tok):
    rs = slice(part * pr, (part + 1) * pr)
    o = o_ref[rs, :]
    ms = jnp.dot((o * o).astype(BF16), seg_ref[...], preferred_element_type=F32) * (1.0 / DN_HEAD_DIM)
    on = o * lax.rsqrt(ms + RMS_EPS) * hw_ref[...]
    z = z_ref[rs, :]
    oa = on * (z * _sigmoid(z)).astype(F32)
    yield
    y_a = _mm(oa, wa_ref[...])
    ys = jax.nn.gelu(ys_ref[rs, :])
    yield
    ys = ys * _sigmoid(_mm(ys, wglu_ref[...]))
    yield
    y_b = _mm(ys, wb_ref[...])
    mixed = _sigmoid(ga_ref[rs, :]).astype(F32) * y_a + _sigmoid(gb_ref[rs, :]).astype(F32) * y_b
    yield
    x1 = x_ref[rs, :] + _mm(mixed, wo_ref[...])
    x1_ref[rs, :] = x1
    hn = x1 * lax.rsqrt(jnp.mean(x1 * x1, axis=-1, keepdims=True) + RMS_EPS) * nf_ref[...]
    _slab_store(hn_ref, hn, part * pr)
    yield

    wr = wr_ref[...]
    w_hi = wr.astype(BF16)
    w_lo = (wr - w_hi.astype(F32)).astype(BF16)
    hn_hi = hn.astype(BF16)
    hn_lo = (hn - hn_hi.astype(F32)).astype(BF16)
    both = _mm_nt(jnp.concatenate([w_hi, w_lo], axis=0), hn_hi)
    logits = both[:ROUTER_ROWS] + both[ROUTER_ROWS:] + _mm_nt(w_hi, hn_lo)
    yield
    coarse = logits[N_EXPERTS:N_EXPERTS + MOE_GROUPS, :]
    cm = jnp.max(coarse, axis=0, keepdims=True)
    ce = jnp.exp(coarse - cm)
    pc = ce / jnp.sum(ce, axis=0, keepdims=True)
    p_sel = jnp.max(pc, axis=0, keepdims=True)
    gi = lax.broadcasted_iota(I32, pc.shape, 0)
    g_sel = jnp.min(jnp.where(pc == p_sel, gi, MOE_GROUPS), axis=0, keepdims=True)
    fine = jnp.zeros((EXPERTS_PER_GROUP, logits.shape[1]), F32)
    for g in range(MOE_GROUPS):
        fine = fine + jnp.where(g_sel == g, logits[g * EXPERTS_PER_GROUP:(g + 1) * EXPERTS_PER_GROUP, :], 0.0)
    fm = jnp.max(fine, axis=0, keepdims=True)
    fe = jnp.exp(fine - fm)
    pf = fe / jnp.sum(fe, axis=0, keepdims=True)
    ei = lax.broadcasted_iota(I32, pf.shape, 0)
    v1 = jnp.max(pf, axis=0, keepdims=True)
    i1 = jnp.min(jnp.where(pf == v1, ei, EXPERTS_PER_GROUP), axis=0, keepdims=True)
    rest = jnp.where(ei == i1, -1.0, pf)
    v2 = jnp.max(rest, axis=0, keepdims=True)
    i2 = jnp.min(jnp.where(rest == v2, ei, EXPERTS_PER_GROUP), axis=0, keepdims=True)
    tot = v1 + v2
    rw_ref[0:1, rs] = v1 / tot * p_sel
    rw_ref[1:2, rs] = v2 / tot * p_sel

    tok = pl.program_id(0) * (pr * POSTMIX_PARTS) + part * pr + lax.broadcasted_iota(I32, (1, pr), 1)
    ph = jnp.zeros((1, pr), I32)
    for r in range(1, MOE_PHASES):
        ph = ph + (tok >= r * range_tok).astype(I32)
    bsel = [ph * N_EXPERTS + g_sel * EXPERTS_PER_GROUP + ix for ix in (i1, i2)]
    bi = lax.broadcasted_iota(I32, (MOE_PHASES * N_EXPERTS, pr), 0)
    onehot = [(bi == b).astype(F32) for b in bsel]
    cnt = onehot[0] + onehot[1]
    before = carry_ref[:, 0:1] + jnp.dot(cnt.astype(BF16), su_ref[0:pr, 0:pr], preferred_element_type=F32)
    for s in range(TOP_K):
        bkt_ref[s:s + 1, rs] = bsel[s]
        rank_ref[s:s + 1, rs] = jnp.sum(onehot[s] * before, axis=0, keepdims=True).astype(I32)
    carry_ref[...] = carry_ref[...] + jnp.sum(cnt, axis=1, keepdims=True)


def _postmix(prompt, sample, weights, nb):
    n_p = prompt[0].shape[0]
    n_s = sample[0].shape[0]
    t = n_p // nb
    tt = min(ROW_TILE, t, n_s)
    nt = t // tt
    nblk_p = n_p // tt
    nblk = nblk_p + n_s // tt
    n_total = n_p + n_s
    prow = lambda i: (jnp.minimum(i, nblk_p - 1), 0)
    pys = lambda i: (jnp.minimum(i, nblk_p - 1) % nt, jnp.minimum(i, nblk_p - 1) // nt)
    srow = lambda i: (jnp.maximum(i - nblk_p, 0), 0)
    const = lambda i: (0, 0)

    def stream_specs(row, ysmap):
        return [pl.BlockSpec((tt, D_MODEL), row),
                pl.BlockSpec((tt, DN_WIDTH), row),
                pl.BlockSpec((tt, DN_WIDTH), row),
                pl.BlockSpec((tt, S5_WIDTH), ysmap),
                pl.BlockSpec((tt, D_MODEL), row),
                pl.BlockSpec((tt, D_MODEL), row)]

    weight_specs = [pl.BlockSpec((1, DN_WIDTH), const),
                    pl.BlockSpec((DN_WIDTH, DN_WIDTH), const),
                    pl.BlockSpec((DN_WIDTH, D_MODEL), const),
                    pl.BlockSpec((S5_WIDTH, S5_WIDTH), const),
                    pl.BlockSpec((S5_WIDTH, D_MODEL), const),
                    pl.BlockSpec((D_MODEL, D_MODEL), const),
                    pl.BlockSpec((1, D_MODEL), const),
                    pl.BlockSpec((ROUTER_ROWS, D_MODEL), const),
                    pl.BlockSpec((tt, tt), const)]
    xp, op, zp, ysp, gap, gbp = prompt
    nbk = MOE_PHASES * N_EXPERTS
    earlier = jnp.triu(jnp.ones((tt, tt), F32), k=1).astype(BF16)
    return pl.pallas_call(
        functools.partial(_postmix_kernel, nblk_p=nblk_p, range_tok=n_total // MOE_PHASES),
        grid=(nblk,),
        in_specs=stream_specs(prow, pys) + stream_specs(srow, srow) + weight_specs,
        out_specs=[pl.BlockSpec((tt, D_MODEL), lambda i: (i, 0)),
                   pl.BlockSpec((tt * ROW_SLAB, LANES), lambda i: (i, 0)),
                   pl.BlockSpec((TOP_K, tt), lambda i: (0, i)),
                   pl.BlockSpec((TOP_K, tt), lambda i: (0, i)),
                   pl.BlockSpec((TOP_K, tt), lambda i: (0, i)),
                   pl.BlockSpec((nbk, LANES), const)],
        out_shape=[jax.ShapeDtypeStruct((n_total, D_MODEL), F32),
                   jax.ShapeDtypeStruct((n_total * ROW_SLAB, LANES), F32),
                   jax.ShapeDtypeStruct((TOP_K, n_total), I32),
                   jax.ShapeDtypeStruct((TOP_K, n_total), I32),
                   jax.ShapeDtypeStruct((TOP_K, n_total), F32),
                   jax.ShapeDtypeStruct((nbk, LANES), F32)],
        scratch_shapes=[pltpu.VMEM((nbk, LANES), F32)],
        compiler_params=_cparams(("arbitrary",)),
        name="postmix",
    )(xp, op, zp, ysp, gap, gbp, *sample, *weights, earlier)


def _wait_slabs(buf, sem):
    pltpu.make_async_copy(buf, buf, sem).wait()


def _moe_kernel(texp_ref, tph_ref, tsrc_ref, tnv_ref, tfirst_ref, tslot_ref, tnext_ref, otok_ref,
                hn_hbm, wu_hbm, wd_hbm, y_ref, hnv, xbuf, wu_buf, wd_buf, wub, wdb, sem, wsem):
    i = pl.program_id(0)
    tm = MOE_TILE
    rs = ROW_SLAB
    nv = tnv_ref[i]
    ph = tph_ref[i]
    range_rows = hnv.shape[0]

    def weight_copies(e, sl):
        return (pltpu.make_async_copy(wu_hbm.at[e], wu_buf.at[sl], wsem.at[sl]),
                pltpu.make_async_copy(wd_hbm.at[e], wd_buf.at[sl], wsem.at[sl]))

    @pl.when(i == 0)
    def _():
        for p, c in enumerate(weight_copies(texp_ref[0], 0)):
            c.start(priority=p % DMA_QUEUES)

    @pl.when(jnp.logical_and(nv > 0, jnp.logical_or(i == 0, ph != tph_ref[jnp.maximum(i - 1, 0)])))
    def _():
        piece = range_rows // DMA_QUEUES
        loads = [pltpu.make_async_copy(hn_hbm.at[pl.ds(pl.multiple_of(ph * range_rows + p * piece, rs), piece), :],
                                       hnv.at[pl.ds(p * piece, piece), :], sem) for p in range(DMA_QUEUES)]
        for p, c in enumerate(loads):
            c.start(priority=p)
        for c in loads:
            c.wait()

    for sl in range(2):
        @pl.when(jnp.logical_and(jnp.logical_and(nv > 0, tfirst_ref[i] == 1), tslot_ref[i] == sl))
        def _():
            for c in weight_copies(texp_ref[i], sl):
                c.wait()

            @pl.when(tnext_ref[i] >= 0)
            def _():
                for p, c in enumerate(weight_copies(tnext_ref[i], 1 - sl)):
                    c.start(priority=p % DMA_QUEUES)

            wub[...] = wu_buf[sl].astype(BF16)
            wdb[...] = wd_buf[sl].astype(BF16)

    @pl.when(nv == 0)
    def _():
        y_ref[...] = jnp.zeros_like(y_ref)

    @pl.when(nv > 0)
    def _():
        src0 = tsrc_ref[i]
        for r in range(tm):
            tok8 = pl.multiple_of(otok_ref[src0 + r], rs)
            xbuf[pl.ds(r * rs, rs), :] = hnv[pl.ds(tok8, rs), :]
        x = _slab_load(xbuf, tm).astype(BF16)
        hu = jnp.dot(x, wub[...], preferred_element_type=F32)
        gate = hu[:, :EXPERT_FF]
        up = hu[:, EXPERT_FF:]
        act = gate * _sigmoid(gate) * up
        _slab_store(y_ref, jnp.dot(act.astype(BF16), wdb[...], preferred_element_type=F32))


def _moe(hn, w_up, w_down, plan):
    ntiles = plan[0].shape[0]
    grid_spec = pltpu.PrefetchScalarGridSpec(
        num_scalar_prefetch=len(plan),
        grid=(ntiles,),
        in_specs=[pl.BlockSpec(memory_space=pl.ANY),
                  pl.BlockSpec(memory_space=pl.ANY),
                  pl.BlockSpec(memory_space=pl.ANY)],
        out_specs=pl.BlockSpec((MOE_TILE * ROW_SLAB, LANES), lambda i, *_: (i, 0)),
        scratch_shapes=[pltpu.VMEM((hn.shape[0] // MOE_PHASES, LANES), F32),
                        pltpu.VMEM((MOE_TILE * ROW_SLAB, LANES), F32),
                        pltpu.VMEM((2, D_MODEL, 2 * EXPERT_FF), F32),
                        pltpu.VMEM((2, EXPERT_FF, D_MODEL), F32),
                        pltpu.VMEM((D_MODEL, 2 * EXPERT_FF), BF16),
                        pltpu.VMEM((EXPERT_FF, D_MODEL), BF16),
                        pltpu.SemaphoreType.DMA,
                        pltpu.SemaphoreType.DMA((2,))])
    return pl.pallas_call(
        _moe_kernel,
        grid_spec=grid_spec,
        out_shape=jax.ShapeDtypeStruct((ntiles * MOE_TILE * ROW_SLAB, LANES), F32),
        compiler_params=_cparams(("arbitrary",)),
        name="moe",
    )(*plan, hn, w_up, w_down)


def _combine_kernel(pos_ref, x1_ref, ys_hbm, w_ref, nw_ref, outp_ref, outs_ref,
                    ybuf0, ybuf1, sem, *, nblk_p, n_tok):
    i = pl.program_id(0)
    nsteps = pl.num_programs(0)
    tt = x1_ref.shape[0]
    rs = ROW_SLAB
    slot = lax.rem(i, 2)
    ybuf = (ybuf0, ybuf1)

    def start_gather(step, sl):
        base = step * tt
        for r in range(tt * TOP_K):
            j, s = divmod(r, TOP_K)
            p8 = pl.multiple_of(pos_ref[s * n_tok + base + j], rs)
            pltpu.make_async_copy(ys_hbm.at[pl.ds(p8, rs), :], ybuf[sl].at[pl.ds((s * tt + j) * rs, rs), :],
                                  sem.at[sl]).start(priority=r % DMA_QUEUES)

    @pl.when(i == 0)
    def _():
        start_gather(0, 0)

    for sl in range(2):
        @pl.when(slot == sl)
        def _():
            _wait_slabs(ybuf[sl], sem.at[sl])
            start_gather(jnp.minimum(i + 1, nsteps - 1), 1 - sl)
            w = w_ref[...]
            y0 = _slab_load(ybuf[sl], tt, 0)
            y1 = _slab_load(ybuf[sl], tt, tt * rs)
            x = x1_ref[...] + w[:, 0:1] * y0 + w[:, 1:2] * y1
            res = x * lax.rsqrt(jnp.mean(x * x, axis=-1, keepdims=True) + RMS_EPS) * nw_ref[...]

            @pl.when(i < nblk_p)
            def _():
                outp_ref[...] = res

            @pl.when(i >= nblk_p)
            def _():
                outs_ref[...] = res

        @pl.when(jnp.logical_and(slot == sl, i == nsteps - 1))
        def _():
            _wait_slabs(ybuf[1 - sl], sem.at[1 - sl])


def _combine(x1, ysorted, pos8, wtok, nw, n_p):
    n = x1.shape[0]
    tt = math.gcd(math.gcd(n_p, n - n_p), COMBINE_TILE)
    nblk_p = n_p // tt
    grid_spec = pltpu.PrefetchScalarGridSpec(
        num_scalar_prefetch=1,
        grid=(n // tt,),
        in_specs=[pl.BlockSpec((tt, D_MODEL), lambda i, *_: (i, 0)),
                  pl.BlockSpec(memory_space=pl.ANY),
                  pl.BlockSpec((tt, TOP_K), lambda i, *_: (i, 0)),
                  pl.BlockSpec((1, D_MODEL), lambda i, *_: (0, 0))],
        out_specs=[pl.BlockSpec((tt, D_MODEL), lambda i, *_: (jnp.minimum(i, nblk_p - 1), 0)),
                   pl.BlockSpec((tt, D_MODEL), lambda i, *_: (jnp.maximum(i - nblk_p, 0), 0))],
        scratch_shapes=[pltpu.VMEM((tt * TOP_K * ROW_SLAB, LANES), F32),
                        pltpu.VMEM((tt * TOP_K * ROW_SLAB, LANES), F32),
                        pltpu.SemaphoreType.DMA((2,))])
    return pl.pallas_call(
        functools.partial(_combine_kernel, nblk_p=nblk_p, n_tok=n),
        grid_spec=grid_spec,
        out_shape=[jax.ShapeDtypeStruct((n_p, D_MODEL), F32),
                   jax.ShapeDtypeStruct((n - n_p, D_MODEL), F32)],
        compiler_params=_cparams(("arbitrary",)),
        name="combine",
    )(pos8, x1, ysorted, wtok, nw)


def _route_plan(bkt, rank, cnt, n_tok):
    tm = MOE_TILE
    n_assign = n_tok * TOP_K
    nbk = MOE_PHASES * N_EXPERTS
    ntiles = n_assign // tm + nbk
    range_tok = n_tok // MOE_PHASES
    b_flat = bkt.T.reshape(n_assign)
    order = jnp.argsort(b_flat, stable=True).astype(I32)
    counts = cnt[:, 0].astype(I32)
    cstart = jnp.cumsum(counts) - counts
    tiles_b = (counts + tm - 1) // tm
    tend = jnp.cumsum(tiles_b)
    tstart = tend - tiles_b
    tile_id = jnp.arange(ntiles, dtype=I32)
    tbk = jnp.minimum(jnp.sum((tile_id[:, None] >= tend[None, :]).astype(I32), axis=1), nbk - 1)
    onehot = (tbk[:, None] == jnp.arange(nbk, dtype=I32)[None, :]).astype(I32)
    pick = lambda v: jnp.sum(onehot * v[None, :], axis=1)
    done = (tile_id - pick(tstart)) * tm
    tnv = jnp.where(tile_id < tend[-1], jnp.clip(pick(counts) - done, 0, tm), 0)
    tsrc = jnp.where(tnv > 0, pick(cstart) + done, 0)
    texp = tbk % N_EXPERTS
    tph = tbk // N_EXPERTS
    nonempty = counts > 0
    bslot = (jnp.cumsum(nonempty.astype(I32)) - 1) % 2
    bidx = jnp.where(nonempty, jnp.arange(nbk, dtype=I32), nbk)
    nxt = jnp.concatenate([lax.cummin(bidx[::-1])[::-1][1:], jnp.full((1,), nbk, I32)])
    bnext = jnp.where(nxt < nbk, nxt % N_EXPERTS, -1)
    tfirst = jnp.logical_and(tnv > 0, done == 0).astype(I32)
    tslot = pick(bslot)
    tnext = pick(bnext)
    otok8 = jnp.concatenate([((order // TOP_K) % range_tok) * ROW_SLAB, jnp.zeros((tm,), I32)])
    plan = tuple(a.astype(I32) for a in (texp, tph, tsrc, tnv, tfirst, tslot, tnext, otok8))
    first = jnp.sum((bkt[:, :, None] == jnp.arange(nbk, dtype=I32)[None, None, :]).astype(I32)
                    * (tstart * tm)[None, None, :], axis=2)
    pos8 = ((first + rank) * ROW_SLAB).reshape(n_assign)
    return plan, pos8.astype(I32)


def _block_diag(m):
    g, a, b = m.shape
    eye = jnp.eye(g, dtype=m.dtype)
    return (eye[:, None, :, None] * m[:, :, None, :]).reshape(g * a, g * b)


def kernel(x_prompt, x_sample, state_conv, state_delta, state_ssm_re, state_ssm_im, norm_mix_w, w_in, conv_w, a_log, dt_bias, head_norm_w, w_a_up, s5_lambda_re, s5_lambda_im, s5_log_step, s5_b_re, s5_b_im, s5_c_re, s5_c_im, s5_d, w_glu, w_b_up, w_o, norm_ffn_w, w_router_coarse, w_router_fine, w_expert_up, w_expert_down, norm_final_w):
    bp, tp, _ = x_prompt.shape
    bs, ts, _ = x_sample.shape
    n_p = bp * tp
    n_s = bs * ts
    n_tok = n_p + n_s
    l = 0

    w = w_in[l].astype(BF16)
    c_ab = W1_COLS + 2 * DN_HEADS
    w_parts = (w[:, :W1_COLS], w[:, c_ab:],
               jnp.concatenate([w[:, W1_COLS:c_ab], jnp.zeros((D_MODEL, LANES - 2 * DN_HEADS), BF16)], axis=1))
    nw_mix = norm_mix_w[l].reshape(1, D_MODEL)
    pad8 = lambda v: jnp.concatenate([v, jnp.zeros((LANES - DN_HEADS,), F32)]).reshape(1, LANES)
    gate_p = jnp.concatenate([pad8(a_log[l]), pad8(dt_bias[l])], axis=0)
    seg = _block_diag(jnp.ones((DN_HEADS, DN_HEAD_DIM, DN_HEAD_DIM), BF16))
    chan_rows = lambda b: jnp.swapaxes(b, 1, 2).reshape(S5_WIDTH, S5_STATE)
    state_rows = lambda c: jnp.tile(jnp.swapaxes(c, 1, 2).reshape(S5_FLAT, S5_GROUP_CH),
                                    (1, LANES // S5_GROUP_CH))
    lam = jnp.concatenate([s5_lambda_re[l].reshape(1, S5_FLAT), s5_lambda_im[l].reshape(1, S5_FLAT),
                           jnp.repeat(s5_log_step[l], S5_STATE).reshape(1, S5_FLAT),
                           jnp.zeros((SUBLANES - 3, S5_FLAT), F32)], axis=0)
    s5_params = (chan_rows(s5_b_re[l]), chan_rows(s5_b_im[l]), lam,
                 state_rows(s5_c_re[l]), state_rows(s5_c_im[l]), s5_d[l].reshape(1, S5_WIDTH))
    hw = jnp.tile(head_norm_w[l], DN_HEADS).reshape(1, DN_WIDTH)
    wr = jnp.concatenate([w_router_fine[l].T, w_router_coarse[l].T,
                          jnp.zeros((ROUTER_ROWS - N_EXPERTS - MOE_GROUPS, D_MODEL), F32)], axis=0)
    pm_weights = (hw, seg, w_a_up[l].astype(BF16), w_glu[l].astype(BF16), w_b_up[l].astype(BF16),
                  w_o[l].astype(BF16), norm_ffn_w[l].reshape(1, D_MODEL), wr)

    xp2 = x_prompt.reshape(n_p, D_MODEL)
    q_p, k_p, v_p, gates_p, conv_p, z_p, u_p, ga_p, gb_p = _inprep(
        xp2, nw_mix, w_parts, jnp.zeros((bp, SUBLANES, QKV_DIM), F32), conv_w[l], gate_p, seg, bp, 1)
    o_p, delta_p = _delta_prompt(q_p, k_p, v_p, gates_p, bp)
    ys_p, h_p = _s5(u_p, s5_params, jnp.zeros((bp, 2 * S5_FLAT), F32), bp, True)

    xs2 = jnp.swapaxes(x_sample, 0, 1).reshape(n_s, D_MODEL)
    cinit_s = jnp.swapaxes(state_conv[l], 0, 1).reshape(1, (CONV_W - 1) * bs, QKV_DIM)
    q_s, k_s, v_s, gate_s, conv_s, z_s, u_s, ga_s, gb_s = _inprep(
        xs2, nw_mix, w_parts, cinit_s, conv_w[l], gate_p, seg, 1, bs)
    s0t = jnp.transpose(state_delta[l], (1, 2, 3, 0)).reshape(DN_HEADS * DN_HEAD_DIM * DN_HEAD_DIM, bs)
    o_s, delta_st = _delta_sample(q_s, k_s, v_s, gate_s, s0t, bs, ts)
    delta_s = jnp.transpose(delta_st.reshape(DN_HEADS, DN_HEAD_DIM, DN_HEAD_DIM, bs), (3, 0, 1, 2))
    h0_s = jnp.concatenate([state_ssm_re[l].reshape(bs, S5_FLAT), state_ssm_im[l].reshape(bs, S5_FLAT)], axis=1)
    ys_s, h_s = _s5(u_s, s5_params, h0_s, bs, False)
    x1, hn, bkt, rank, rw, cnt = _postmix((xp2, o_p, z_p, ys_p, ga_p, gb_p), (xs2, o_s, z_s, ys_s, ga_s, gb_s),
                                          pm_weights, bp)

    plan, pos8 = _route_plan(bkt, rank, cnt, n_tok)
    ysorted = _moe(hn, w_expert_up[l], w_expert_down[l], plan)
    y_p, y_s = _combine(x1, ysorted, pos8, rw.T, norm_final_w.reshape(1, D_MODEL), n_p)

    y_prompt = y_p.reshape(bp, tp, D_MODEL)
    y_sample = jnp.swapaxes(y_s.reshape(ts, bs, D_MODEL), 0, 1)
    conv_sample = jnp.swapaxes(conv_s.reshape(CONV_W - 1, bs, QKV_DIM), 0, 1)
    return (y_prompt, y_sample,
            conv_p[None], delta_p[None],
            h_p[:, :S5_FLAT].reshape(1, bp, S5_GROUPS, S5_STATE), h_p[:, S5_FLAT:].reshape(1, bp, S5_GROUPS, S5_STATE),
            conv_sample[None], delta_s[None],
            h_s[:, :S5_FLAT].reshape(1, bs, S5_GROUPS, S5_STATE), h_s[:, S5_FLAT:].reshape(1, bs, S5_GROUPS, S5_STATE))
```

```python
import functools
import math

import jax
import jax.numpy as jnp
import numpy as np
from jax import lax
from jax.experimental import pallas as pl
from jax.experimental.pallas import tpu as pltpu

F32 = jnp.float32
BF16 = jnp.bfloat16
I32 = jnp.int32

D_MODEL = 1024
DN_HEADS = 8
DN_HEAD_DIM = 64
DN_WIDTH = DN_HEADS * DN_HEAD_DIM
QKV_DIM = 3 * DN_WIDTH
CONV_W = 4
DN_CHUNK = 64
S5_GROUP_CH = 16
S5_WIDTH = D_MODEL // 2
S5_GROUPS = S5_WIDTH // S5_GROUP_CH
S5_STATE = 64
S5_FLAT = S5_GROUPS * S5_STATE
MOE_GROUPS = 4
EXPERTS_PER_GROUP = 8
N_EXPERTS = MOE_GROUPS * EXPERTS_PER_GROUP
TOP_K = 2
EXPERT_FF = 256
RMS_EPS = 1e-6
L2_EPS = 1e-6

LANES = 128
SUBLANES = 8
VMEM_LIMIT = 56 * 1024 * 1024

W1_COLS = QKV_DIM + DN_WIDTH
W2_COLS = S5_WIDTH + 2 * D_MODEL

ROW_TILE = 512
INPREP_PARTS = 2
POSTMIX_PARTS = 2
MOE_TILE = 256
MOE_PHASES = 2
COMBINE_TILE = 512
DMA_QUEUES = 2
DELTA_SUBCHUNKS = 4
S5_SUPER = 2
S5_TILE_ROWS = 1024
S5_SCAN_SPLIT = 2
ROUTER_ROWS = 40


def _mm(a, b):
    return jnp.dot(a.astype(BF16), b.astype(BF16), preferred_element_type=F32)


def _mm_nt(a, b):
    return lax.dot_general(a.astype(BF16), b.astype(BF16), (((1,), (1,)), ((), ())),
                           preferred_element_type=F32)


def _sigmoid(x):
    return 0.5 * jnp.tanh(0.5 * x) + 0.5


def _cparams(sem):
    return pltpu.CompilerParams(dimension_semantics=sem, vmem_limit_bytes=VMEM_LIMIT)


ROW_SLAB = D_MODEL // LANES


def _slab_load(ref, rows, first=0, pitch=ROW_SLAB):
    return jnp.concatenate([ref[pl.ds(first + j, rows, stride=pitch), :] for j in range(ROW_SLAB)], axis=1)


def _slab_store(ref, x, first=0):
    for j in range(ROW_SLAB):
        ref[pl.ds(first * ROW_SLAB + j, x.shape[0], stride=ROW_SLAB), :] = x[:, j * LANES:(j + 1) * LANES]


def _softplus(x):
    return jnp.maximum(x, 0.0) + jnp.log1p(jnp.exp(-jnp.abs(x)))


def _inprep_kernel(x_ref, nw_ref, w1_ref, w2_ref, wab_ref, cinit_ref, cw_ref, gp_ref, seg_ref,
                   q_ref, k_ref, v_ref, gate_ref, cnew_ref, z_ref, u_ref, ga_ref, gb_ref, xp_ref,
                   *, shift, rc, rows):
    @pl.when(pl.program_id(1) == 0)
    def _():
        xp_ref[0:rc, :] = cinit_ref[0]

    seg = seg_ref[...]
    pr = rows // INPREP_PARTS

    def part_stages(part):
        rs = slice(part * pr, (part + 1) * pr)
        x = x_ref[rs, :]
        h = x * lax.rsqrt(jnp.mean(x * x, axis=-1, keepdims=True) + RMS_EPS) * nw_ref[...]
        hb = h.astype(BF16)

        def proj(w_ref, lo, hi):
            return jnp.dot(hb, w_ref[:, lo:hi], preferred_element_type=F32)

        xp_ref[rc + part * pr:rc + (part + 1) * pr, :] = proj(w1_ref, 0, QKV_DIM)
        ab = proj(wab_ref, 0, LANES)
        yield
        z_ref[rs, :] = proj(w1_ref, QKV_DIM, W1_COLS).astype(z_ref.dtype)
        u_ref[rs, :] = proj(w2_ref, 0, S5_WIDTH).astype(u_ref.dtype)
        acc = None
        for i in range(CONV_W):
            lo = rc + part * pr + (i - (CONV_W - 1)) * shift
            term = xp_ref[lo:lo + pr, :] * cw_ref[i:i + 1, :]
            acc = term if acc is None else acc + term
        y = acc * _sigmoid(acc)
        yield
        ga_ref[rs, :] = proj(w2_ref, S5_WIDTH, S5_WIDTH + D_MODEL).astype(ga_ref.dtype)
        q = y[:, 0:DN_WIDTH]
        k = y[:, DN_WIDTH:2 * DN_WIDTH]
        q_ref[rs, :] = q * lax.rsqrt(jnp.dot((q * q).astype(BF16), seg, preferred_element_type=F32) + L2_EPS)
        k_ref[rs, :] = k * lax.rsqrt(jnp.dot((k * k).astype(BF16), seg, preferred_element_type=F32) + L2_EPS)
        v_ref[rs, :] = y[:, 2 * DN_WIDTH:]
        yield
        gb_ref[rs, :] = proj(w2_ref, S5_WIDTH + D_MODEL, W2_COLS).astype(gb_ref.dtype)
        g = -jnp.exp(gp_ref[0:1, :]) * _softplus(ab + gp_ref[1:2, :])
        beta = _sigmoid(ab)
        lane = lax.broadcasted_iota(I32, ab.shape, 1)
        gate_ref[rs, :] = jnp.where(lane < DN_HEADS, g, beta)

    live = []
    pending = [part_stages(p) for p in range(INPREP_PARTS)]
    while live or pending:
        if pending:
            live.append(pending.pop(0))
        live = [g for g in live if next(g, StopIteration) is not StopIteration]

    keep = (CONV_W - 1) * shift
    cnew_ref[0] = xp_ref[rc + rows - keep:rc + rows, :]
    xp_ref[0:rc, :] = xp_ref[rows:rows + rc, :]


def _inprep(x2d, nw, w_parts, cinit, conv_w, gate_p, seg, nb, shift):
    n = x2d.shape[0]
    r = n // nb
    rows = min(ROW_TILE, r)
    nt = r // rows
    rc = cinit.shape[1]
    keep = (CONV_W - 1) * shift
    row = lambda b, i: (b * nt + i, 0)
    const = lambda b, i: (0, 0)
    kern = functools.partial(_inprep_kernel, shift=shift, rc=rc, rows=rows)
    outs = pl.pallas_call(
        kern,
        grid=(nb, nt),
        in_specs=[pl.BlockSpec((rows, D_MODEL), row),
                  pl.BlockSpec((1, D_MODEL), const),
                  pl.BlockSpec((D_MODEL, W1_COLS), const),
                  pl.BlockSpec((D_MODEL, W2_COLS), const),
                  pl.BlockSpec((D_MODEL, LANES), const),
                  pl.BlockSpec((1, rc, QKV_DIM), lambda b, i: (b, 0, 0)),
                  pl.BlockSpec((CONV_W, QKV_DIM), const),
                  pl.BlockSpec((2, LANES), const),
                  pl.BlockSpec((DN_WIDTH, DN_WIDTH), const)],
        out_specs=[pl.BlockSpec((rows, DN_WIDTH), row),
                   pl.BlockSpec((rows, DN_WIDTH), row),
                   pl.BlockSpec((rows, DN_WIDTH), row),
                   pl.BlockSpec((rows, LANES), row),
                   pl.BlockSpec((1, keep, QKV_DIM), lambda b, i: (b, 0, 0)),
                   pl.BlockSpec((rows, DN_WIDTH), row),
                   pl.BlockSpec((rows, S5_WIDTH), lambda b, i: (i, b)),
                   pl.BlockSpec((rows, D_MODEL), row),
                   pl.BlockSpec((rows, D_MODEL), row)],
        out_shape=[jax.ShapeDtypeStruct((n, DN_WIDTH), F32),
                   jax.ShapeDtypeStruct((n, DN_WIDTH), F32),
                   jax.ShapeDtypeStruct((n, DN_WIDTH), F32),
                   jax.ShapeDtypeStruct((n, LANES), F32),
                   jax.ShapeDtypeStruct((nb, keep, QKV_DIM), F32),
                   jax.ShapeDtypeStruct((n, DN_WIDTH), BF16),
                   jax.ShapeDtypeStruct((r, nb * S5_WIDTH), BF16),
                   jax.ShapeDtypeStruct((n, D_MODEL), BF16),
                   jax.ShapeDtypeStruct((n, D_MODEL), BF16)],
        scratch_shapes=[pltpu.VMEM((rc + rows, QKV_DIM), F32)],
        compiler_params=_cparams(("arbitrary", "arbitrary")),
        name="inprep",
    )(x2d, nw, *w_parts, cinit, conv_w, gate_p, seg)
    return outs


def _delta_home(low, h, x, other=0.0):
    return jnp.where(low, x, other) if h % 2 == 0 else jnp.where(low, other, x)


def _delta_prepare(q_ref, k_ref, v_ref, gate_ref, tril_ref, bufs, *, nsub):
    sol_buf, wq_buf, qk_buf, kdec_buf, dl_buf = bufs
    c = DN_CHUNK
    dk = DN_HEAD_DIM

    def home(h, x, other=0.0):
        return _delta_home(low, h, x, other)

    rowi2 = lax.broadcasted_iota(I32, (c, 2 * c), 0)
    lane2 = lax.broadcasted_iota(I32, (c, 2 * c), 1)
    coli2 = lane2 & (c - 1)
    causal2 = rowi2 >= coli2
    strict2 = rowi2 > coli2
    low = lane2 < dk
    tril = tril_ref[...]
    pairs = [(j, h) for j in range(nsub) for h in range(DN_HEADS)]
    units = [(j, pr) for j in range(nsub) for pr in range(DN_HEADS // 2)]
    rows = [slice(j * c, (j + 1) * c) for j in range(nsub)]
    gate = [gate_ref[rows[j], :] for j in range(nsub)]
    gc_all = [_split3_dot_left(tril, gate[j]) for j in range(nsub)]
    gc_t = [gc_all[j].T for j in range(nsub)]

    def block(ref, j, pr):
        return ref[rows[j], pr * LANES:(pr + 1) * LANES]

    gfull = {(j, h): jnp.broadcast_to(gc_all[j][:, h:h + 1], (c, 2 * c)) for j, h in pairs}
    g2 = {(j, pr): jnp.where(low, gfull[j, 2 * pr], gfull[j, 2 * pr + 1]) for j, pr in units}
    b2 = {(j, pr): jnp.where(low, gate[j][:, DN_HEADS + 2 * pr:DN_HEADS + 2 * pr + 1],
                             gate[j][:, DN_HEADS + 2 * pr + 1:DN_HEADS + 2 * pr + 2]) for j, pr in units}
    kp = {u: block(k_ref, *u) for u in units}
    qp = {u: block(q_ref, *u) * (dk ** -0.5) for u in units}
    egc2 = {u: jnp.exp(g2[u]) for u in units}
    kb2 = {u: kp[u] * b2[u] for u in units}
    vb2 = {u: block(v_ref, *u) * b2[u] for u in units}
    kw2s = {u: pltpu.roll(kb2[u] * egc2[u], dk, axis=1) for u in units}
    qd2 = {u: qp[u] * egc2[u] for u in units}
    glast2 = {u: g2[u][c - 1:c, :] for u in units}
    kdec_t2 = {u: (kp[u] * jnp.exp(glast2[u] - g2[u])).T for u in units}
    dlast2 = {u: jnp.exp(glast2[u]) for u in units}
    kk = {u: jnp.concatenate([kp[u], kp[u]], axis=0) for u in units}
    yield

    grow2 = {(j, h): jnp.concatenate([gc_t[j][h:h + 1, :], gc_t[j][h:h + 1, :]], axis=1) for j, h in pairs}
    decay = {p: jnp.where(causal2, jnp.exp(jnp.where(causal2, gfull[p] - grow2[p], 0.0)), 0.0) for p in pairs}
    gram = {(j, h): _mm_nt(jnp.concatenate([home(h, kb2[j, h // 2]), home(h, qp[j, h // 2])], axis=0), kk[j, h // 2])
            for j, h in pairs}
    mat = {p: jnp.where(strict2, gram[p][:c] * decay[p], 0.0).astype(BF16) for p in pairs}
    qk = {p: jnp.where(causal2, gram[p][c:] * decay[p], 0.0) for p in pairs}
    sol = {(j, h): home(h, vb2[j, h // 2], kw2s[j, h // 2]) for j, h in pairs}
    yield
    levels = int(math.log2(c))
    zeros2 = jnp.zeros((c, 2 * c), BF16)
    for lvl in range(levels):
        hi = {p: sol[p].astype(BF16) for p in pairs}
        lo = {p: (sol[p] - hi[p].astype(F32)).astype(BF16) for p in pairs}
        if lvl < levels - 1:
            y = {p: jnp.dot(mat[p], jnp.concatenate([jnp.concatenate([hi[p], mat[p]], axis=1),
                                                     jnp.concatenate([lo[p], zeros2], axis=1)], axis=0),
                            preferred_element_type=F32) for p in pairs}
            mat = {p: y[p][:, 2 * dk:].astype(BF16) for p in pairs}
            upd = {p: y[p][:, :2 * dk] for p in pairs}
        else:
            upd = {p: jnp.dot(mat[p], jnp.concatenate([hi[p], lo[p]], axis=0), preferred_element_type=F32)
                   for p in pairs}
        sol = {p: (sol[p] - upd[p]) if lvl == 0 else (sol[p] + upd[p]) for p in pairs}
        yield
    for j, h in pairs:
        n = j * DN_HEADS + h
        sol_buf[n] = sol[j, h]
        wq_buf[n] = jnp.concatenate([home(h, 0.0, sol[j, h]), home(h, qd2[j, h // 2])], axis=0).astype(BF16)
        qk_buf[n] = qk[j, h].astype(BF16)
    for j, pr in units:
        n = j * (DN_HEADS // 2) + pr
        kdec_buf[n] = kdec_t2[j, pr].astype(BF16)
        dl_buf[n] = jnp.broadcast_to(dlast2[j, pr], (SUBLANES, LANES))


def _delta_apply(bufs, o_ref, s_ref, *, nsub):
    sol_buf, wq_buf, qk_buf, kdec_buf, dl_buf = bufs
    c = DN_CHUNK
    dk = DN_HEAD_DIM
    heads = range(DN_HEADS)
    low = lax.broadcasted_iota(I32, (c, 2 * c), 1) < dk
    s = [s_ref[h] for h in heads]
    for j in range(nsub):
        ws, v_new, o_h = [], [], []
        for h in heads:
            n = j * DN_HEADS + h
            ws.append(jnp.dot(wq_buf[n], jnp.concatenate([s[h], s[h]], axis=0).astype(BF16),
                              preferred_element_type=F32))
        yield
        for h in heads:
            v_new.append(sol_buf[j * DN_HEADS + h] - ws[h][:c])
        for h in heads:
            o_h.append(ws[h][c:] + jnp.dot(qk_buf[j * DN_HEADS + h][:, :c], v_new[h].astype(BF16),
                                           preferred_element_type=F32))
        for pr in range(DN_HEADS // 2):
            o_ref[j * c:(j + 1) * c, pr * LANES:(pr + 1) * LANES] = jnp.where(low, o_h[2 * pr], o_h[2 * pr + 1])
        nxt = []
        for h in heads:
            u = j * (DN_HEADS // 2) + h // 2
            kdt = kdec_buf[u][(h % 2) * dk:(h % 2 + 1) * dk, :]
            d = dl_buf[u][0:1, :]
            nxt.append(_delta_home(low, h, s[h] * d + jnp.dot(kdt, v_new[h].astype(BF16),
                                                               preferred_element_type=F32)))
        s = nxt
        yield
    for h in heads:
        s_ref[h] = s[h]


def _delta_chunk_kernel(q_ref, k_ref, v_ref, gate_ref, tril_ref, o_ref, sfin_ref, s_ref, *bufs, nsub, nc):
    i = pl.program_id(0)
    half = len(bufs) // 2
    sets = (bufs[:half], bufs[half:])
    local = lax.rem(jnp.maximum(i - 1, 0), nc)

    @pl.when(i == 0)
    def _():
        for b in sets[1]:
            b[...] = jnp.zeros_like(b)

    @pl.when(local == 0)
    def _():
        s_ref[...] = jnp.zeros_like(s_ref)

    for par in range(2):
        @pl.when(lax.rem(i, 2) == par)
        def _(par=par):
            parts = [_delta_prepare(q_ref, k_ref, v_ref, gate_ref, tril_ref, sets[par], nsub=nsub),
                     _delta_apply(sets[1 - par], o_ref, s_ref, nsub=nsub)]
            while parts:
                parts = [g for g in parts if next(g, StopIteration) is not StopIteration]

    @pl.when(jnp.logical_and(i >= 1, local == nc - 1))
    def _():
        dk = DN_HEAD_DIM
        for h in range(DN_HEADS):
            sfin_ref[0, h] = s_ref[h][:, (h % 2) * dk:(h % 2 + 1) * dk]


def _split3_dot_left(b01, a):
    a1 = a.astype(BF16)
    r1 = a - a1.astype(F32)
    a2 = r1.astype(BF16)
    a3 = (r1 - a2.astype(F32)).astype(BF16)
    out = jnp.dot(b01, a3, preferred_element_type=F32)
    out = out + jnp.dot(b01, a2, preferred_element_type=F32)
    return out + jnp.dot(b01, a1, preferred_element_type=F32)


def _delta_prompt(q, k, v, gate, nb):
    n = q.shape[0]
    t = n // nb
    c = DN_CHUNK
    nsub = DELTA_SUBCHUNKS
    rows = nsub * c
    nc = t // rows
    nblk = nb * nc
    row_in = lambda i: (jnp.minimum(i, nblk - 1), 0)
    row_out = lambda i: (jnp.maximum(i - 1, 0), 0)
    tril = jnp.tril(jnp.ones((c, c), F32)).astype(BF16)
    nh = nsub * DN_HEADS
    npair = nsub * DN_HEADS // 2
    buf_set = [pltpu.VMEM((nh, c, 2 * DN_HEAD_DIM), F32),
               pltpu.VMEM((nh, 2 * c, 2 * DN_HEAD_DIM), BF16),
               pltpu.VMEM((nh, c, 2 * c), BF16),
               pltpu.VMEM((npair, 2 * DN_HEAD_DIM, c), BF16),
               pltpu.VMEM((npair, SUBLANES, LANES), F32)]
    return pl.pallas_call(
        functools.partial(_delta_chunk_kernel, nsub=nsub, nc=nc),
        grid=(nblk + 1,),
        in_specs=[pl.BlockSpec((rows, DN_WIDTH), row_in),
                  pl.BlockSpec((rows, DN_WIDTH), row_in),
                  pl.BlockSpec((rows, DN_WIDTH), row_in),
                  pl.BlockSpec((rows, LANES), row_in),
                  pl.BlockSpec((c, c), lambda i: (0, 0))],
        out_specs=[pl.BlockSpec((rows, DN_WIDTH), row_out),
                   pl.BlockSpec((1, DN_HEADS, DN_HEAD_DIM, DN_HEAD_DIM),
                                lambda i: (jnp.maximum(i - 1, 0) // nc, 0, 0, 0))],
        out_shape=[jax.ShapeDtypeStruct((n, DN_WIDTH), F32),
                   jax.ShapeDtypeStruct((nb, DN_HEADS, DN_HEAD_DIM, DN_HEAD_DIM), F32)],
        scratch_shapes=[pltpu.VMEM((DN_HEADS, DN_HEAD_DIM, 2 * DN_HEAD_DIM), F32)] + buf_set + buf_set,
        compiler_params=_cparams(("arbitrary",)),
        name="delta_prompt",
    )(q, k, v, gate, tril)


def _delta_step_kernel(q_ref, k_ref, v_ref, gate_ref, s0_ref, o_ref, s_ref, kt_ref, qt_ref, gt_ref, *, nt, nb):
    dk = DN_HEAD_DIM
    p = pl.program_id(0)
    for t in range(nt):
        rs = slice(t * nb, (t + 1) * nb)
        gt_ref[...] = gate_ref[rs, :].T
        kt_ref[...] = k_ref[rs, :].T
        qt_ref[...] = (q_ref[rs, :] * (dk ** -0.5)).T
        vt = v_ref[rs, :].T
        src = s0_ref if t == 0 else s_ref
        o_heads = []
        for j in range(2):
            a = jnp.exp(gt_ref[pl.ds(2 * p + j, 1), :])
            beta = gt_ref[pl.ds(2 * p + j + DN_HEADS, 1), :]
            base = j * dk * dk

            def k_dot_s(d, acc, j=j, base=base, src=src):
                sd = src[pl.ds(pl.multiple_of(base + d * dk, dk), dk), :]
                return acc + kt_ref[pl.ds(j * dk + d, 1), :] * sd

            ks = lax.fori_loop(0, dk, k_dot_s, jnp.zeros((dk, nb), F32), unroll=4)
            delta = beta * (vt[j * dk:(j + 1) * dk, :] - a * ks)

            def update(d, acc, j=j, base=base, src=src, a=a, delta=delta):
                r0 = pl.multiple_of(base + d * dk, dk)
                sn = a * src[pl.ds(r0, dk), :] + kt_ref[pl.ds(j * dk + d, 1), :] * delta
                s_ref[pl.ds(r0, dk), :] = sn
                return acc + qt_ref[pl.ds(j * dk + d, 1), :] * sn

            o_heads.append(lax.fori_loop(0, dk, update, jnp.zeros((dk, nb), F32), unroll=4))
        o_ref[rs, :] = jnp.concatenate(o_heads, axis=0).T


def _delta_sample(q, k, v, gate, s0t, nb, nt):
    dk = DN_HEAD_DIM
    flat = dk * dk
    n = nt * nb
    kern = functools.partial(_delta_step_kernel, nt=nt, nb=nb)
    pair = lambda p: (0, p)
    return pl.pallas_call(
        kern,
        grid=(DN_HEADS // 2,),
        in_specs=[pl.BlockSpec((n, LANES), pair),
                  pl.BlockSpec((n, LANES), pair),
                  pl.BlockSpec((n, LANES), pair),
                  pl.BlockSpec((n, LANES), lambda p: (0, 0)),
                  pl.BlockSpec((2 * flat, nb), lambda p: (p, 0))],
        out_specs=[pl.BlockSpec((n, LANES), pair),
                   pl.BlockSpec((2 * flat, nb), lambda p: (p, 0))],
        out_shape=[jax.ShapeDtypeStruct((n, DN_WIDTH), F32),
                   jax.ShapeDtypeStruct((DN_HEADS * flat, nb), F32)],
        scratch_shapes=[pltpu.VMEM((LANES, nb), F32),
                        pltpu.VMEM((LANES, nb), F32),
                        pltpu.VMEM((LANES, nb), F32)],
        compiler_params=_cparams(("arbitrary",)),
        name="delta_sample",
    )(q, k, v, gate, s0t)


def _s5_kernel(u_ref, btre_ref, btim_ref, lam_ref, ctre_ref, ctim_ref, d_ref, h0_ref, y_ref, hfin_ref,
               bw_ref, c_ref, ab_ref, x_ref, h_ref, ru_ref, ry_ref, *, nb, tt, wide):
    p2 = S5_FLAT

    @pl.when(pl.program_id(0) == 0)
    def _():
        lr = lam_ref[0:1, :]
        li = lam_ref[1:2, :]
        dt = jnp.exp(lam_ref[2:3, :])
        mag = jnp.exp(lr * dt)
        ab_re = mag * jnp.cos(li * dt)
        ab_im = mag * jnp.sin(li * dt)
        den = lr * lr + li * li
        nr = ab_re - 1.0
        ni = ab_im
        f_re = (nr * lr + ni * li) / den
        f_im = (ni * lr - nr * li) / den
        ab_ref[0:1, :] = ab_re
        ab_ref[1:2, :] = ab_im
        gpl = LANES // S5_STATE
        ch_g = lax.broadcasted_iota(I32, (S5_WIDTH, LANES), 0) // S5_GROUP_CH
        lane_g = lax.broadcasted_iota(I32, (S5_WIDTH, LANES), 1) // S5_STATE
        bre2 = jnp.concatenate([btre_ref[...]] * gpl, axis=1)
        bim2 = jnp.concatenate([btim_ref[...]] * gpl, axis=1)
        for j in range(p2 // LANES):
            cols = slice(j * LANES, (j + 1) * LANES)
            own = ch_g == gpl * j + lane_g
            bre = jnp.where(own, bre2, 0.0)
            bim = jnp.where(own, bim2, 0.0)
            bw_ref[:, cols] = (bre * f_re[:, cols] - bim * f_im[:, cols]).astype(BF16)
            bw_ref[:, p2 + j * LANES:p2 + (j + 1) * LANES] = (bim * f_re[:, cols] + bre * f_im[:, cols]).astype(BF16)
        cpl = LANES // S5_GROUP_CH
        st_g = lax.broadcasted_iota(I32, (p2, LANES), 0) // S5_STATE
        lane_cg = lax.broadcasted_iota(I32, (p2, LANES), 1) // S5_GROUP_CH
        for j in range(S5_WIDTH // LANES):
            cols = slice(j * LANES, (j + 1) * LANES)
            own = st_g == cpl * j + lane_cg
            c_ref[0:p2, cols] = jnp.where(own, ctre_ref[...], 0.0).astype(BF16)
            c_ref[p2:2 * p2, cols] = jnp.where(own, -ctim_ref[...], 0.0).astype(BF16)
        h_ref[...] = h0_ref[...]

    nck = S5_WIDTH // LANES
    if wide:
        for b in range(nb):
            for ck in range(nck):
                lo = b * S5_WIDTH + ck * LANES
                ru_ref[ck, pl.ds(b, tt, stride=nb), :] = u_ref[:, lo:lo + LANES].astype(F32)
        u = jnp.concatenate([ru_ref[ck] for ck in range(nck)], axis=1)
    else:
        u = u_ref[...].astype(F32)
    ub = u.astype(BF16)
    cw = S5_WIDTH // S5_SUPER
    sw = S5_FLAT // S5_SUPER
    for part in (0, p2):
        for b in range(S5_SUPER):
            x_ref[:, part + b * sw:part + (b + 1) * sw] = jnp.dot(
                ub[:, b * cw:(b + 1) * cw], bw_ref[b * cw:(b + 1) * cw, part + b * sw:part + (b + 1) * sw],
                preferred_element_type=F32)
    a_re = ab_ref[0:1, :]
    a_im = ab_ref[1:2, :]

    if nb == SUBLANES:
        wsl = p2 // S5_SCAN_SPLIT
        for sp in range(S5_SCAN_SPLIT):
            c0 = sp * wsl
            are = jnp.broadcast_to(a_re[:, c0:c0 + wsl], (nb, wsl))
            aim = jnp.broadcast_to(a_im[:, c0:c0 + wsl], (nb, wsl))

            def step(t, carry, c0=c0, are=are, aim=aim):
                hr, hi = carry
                r0 = pl.multiple_of(t * nb, nb)
                nr = are * hr - aim * hi + x_ref[pl.ds(r0, nb), c0:c0 + wsl]
                ni = are * hi + aim * hr + x_ref[pl.ds(r0, nb), p2 + c0:p2 + c0 + wsl]
                x_ref[pl.ds(r0, nb), c0:c0 + wsl] = nr
                x_ref[pl.ds(r0, nb), p2 + c0:p2 + c0 + wsl] = ni
                return nr, ni

            hr, hi = lax.fori_loop(0, tt, step, (h_ref[:, c0:c0 + wsl], h_ref[:, p2 + c0:p2 + c0 + wsl]),
                                   unroll=4)
            h_ref[:, c0:c0 + wsl] = hr
            h_ref[:, p2 + c0:p2 + c0 + wsl] = hi
    else:
        for t in range(tt):
            rs = slice(t * nb, (t + 1) * nb)
            hr = h_ref[:, 0:p2]
            hi = h_ref[:, p2:2 * p2]
            nr = a_re * hr - a_im * hi + x_ref[rs, 0:p2]
            ni = a_re * hi + a_im * hr + x_ref[rs, p2:2 * p2]
            h_ref[:, 0:p2] = nr
            h_ref[:, p2:2 * p2] = ni
            x_ref[rs, 0:p2] = nr
            x_ref[rs, p2:2 * p2] = ni

    for b in range(S5_SUPER):
        cols = slice(b * cw, (b + 1) * cw)
        y = None
        for part in (0, p2):
            rws = slice(part + b * sw, part + (b + 1) * sw)
            term = jnp.dot(x_ref[:, rws].astype(BF16), c_ref[rws, cols], preferred_element_type=F32)
            y = term if y is None else y + term
        if wide:
            skip = jnp.concatenate([ru_ref[b * (cw // LANES) + ck] for ck in range(cw // LANES)], axis=1)
        else:
            skip = u[:, cols]
        y = y + d_ref[:, cols] * skip
        if wide:
            for ck in range(cw // LANES):
                ry_ref[b * (cw // LANES) + ck] = y[:, ck * LANES:(ck + 1) * LANES]
        else:
            y_ref[:, cols] = y
    if wide:
        for b in range(nb):
            for ck in range(nck):
                lo = b * S5_WIDTH + ck * LANES
                y_ref[:, lo:lo + LANES] = ry_ref[ck, pl.ds(b, tt, stride=nb), :]
    hfin_ref[...] = h_ref[...]


def _s5(u, params, h0, nb, wide):
    btre, btim, lam, ctre, ctim, dvec = params
    t = u.shape[0] if wide else u.shape[0] // nb
    tt = min(S5_TILE_ROWS // nb, t)
    rows = tt * nb
    const = lambda i: (0, 0)
    kern = functools.partial(_s5_kernel, nb=nb, tt=tt, wide=wide)
    io_block = (tt, nb * S5_WIDTH) if wide else (rows, S5_WIDTH)
    return pl.pallas_call(
        kern,
        grid=(t // tt,),
        in_specs=[pl.BlockSpec(io_block, lambda i: (i, 0)),
                  pl.BlockSpec((S5_WIDTH, S5_STATE), const),
                  pl.BlockSpec((S5_WIDTH, S5_STATE), const),
                  pl.BlockSpec((SUBLANES, S5_FLAT), const),
                  pl.BlockSpec((S5_FLAT, LANES), const),
                  pl.BlockSpec((S5_FLAT, LANES), const),
                  pl.BlockSpec((1, S5_WIDTH), const),
                  pl.BlockSpec((nb, 2 * S5_FLAT), const)],
        out_specs=[pl.BlockSpec(io_block, lambda i: (i, 0)),
                   pl.BlockSpec((nb, 2 * S5_FLAT), const)],
        out_shape=[jax.ShapeDtypeStruct(u.shape, F32),
                   jax.ShapeDtypeStruct((nb, 2 * S5_FLAT), F32)],
        scratch_shapes=[pltpu.VMEM((S5_WIDTH, 2 * S5_FLAT), BF16),
                        pltpu.VMEM((2 * S5_FLAT, S5_WIDTH), BF16),
                        pltpu.VMEM((SUBLANES, S5_FLAT), F32),
                        pltpu.VMEM((rows, 2 * S5_FLAT), F32),
                        pltpu.VMEM((nb, 2 * S5_FLAT), F32),
                        pltpu.VMEM((S5_WIDTH // LANES, rows, LANES), F32),
                        pltpu.VMEM((S5_WIDTH // LANES, rows, LANES), F32)],
        compiler_params=_cparams(("arbitrary",)),
        name="s5",
    )(u, btre, btim, lam, ctre, ctim, dvec, h0)


def _postmix_kernel(xp_ref, op_ref, zp_ref, ysp_ref, gap_ref, gbp_ref,
                    xs_ref, os_ref, zs_ref, yss_ref, gas_ref, gbs_ref, *rest, nblk_p, range_tok):
    carry_ref = rest[-1]

    @pl.when(pl.program_id(0) == 0)
    def _():
        carry_ref[...] = jnp.zeros_like(carry_ref)

    @pl.when(pl.program_id(0) < nblk_p)
    def _():
        _postmix_body(xp_ref, op_ref, zp_ref, ysp_ref, gap_ref, gbp_ref, *rest, range_tok=range_tok)

    @pl.when(pl.program_id(0) >= nblk_p)
    def _():
        _postmix_body(xs_ref, os_ref, zs_ref, yss_ref, gas_ref, gbs_ref, *rest, range_tok=range_tok)


def _postmix_body(x_ref, o_ref, z_ref, ys_ref, ga_ref, gb_ref, hw_ref, seg_ref, wa_ref, wglu_ref, wb_ref,
                  wo_ref, nf_ref, wr_ref, su_ref, x1_ref, hn_ref, bkt_ref, rank_ref, rw_ref, cnt_ref, carry_ref,
                  *, range_tok):
    rows = x_ref.shape[0]
    pr = rows // POSTMIX_PARTS
    parts = [_postmix_part(p, pr, x_ref, o_ref, z_ref, ys_ref, ga_ref, gb_ref, hw_ref, seg_ref, wa_ref, wglu_ref,
                           wb_ref, wo_ref, nf_ref, wr_ref, su_ref, x1_ref, hn_ref, bkt_ref, rank_ref, rw_ref,
                           carry_ref, range_tok) for p in range(POSTMIX_PARTS)]
    live = []
    while live or parts:
        if parts:
            live.append(parts.pop(0))
        live = [g for g in live if next(g, StopIteration) is not StopIteration]
    cnt_ref[...] = carry_ref[...]


def _postmix_part(part, pr, x_ref, o_ref, z_ref, ys_ref, ga_ref, gb_ref, hw_ref, seg_ref, wa_ref, wglu_ref, wb_ref,
                  wo_ref, nf_ref, wr_ref, su_ref, x1_ref, hn_ref, bkt_ref, rank_ref, rw_ref, carry_ref, range_tok):
    rs = slice(part * pr, (part + 1) * pr)
    o = o_ref[rs, :]
    ms = jnp.dot((o * o).astype(BF16), seg_ref[...], preferred_element_type=F32) * (1.0 / DN_HEAD_DIM)
    on = o * lax.rsqrt(ms + RMS_EPS) * hw_ref[...]
    z = z_ref[rs, :]
    oa = on * (z * _sigmoid(z)).astype(F32)
    yield
    y_a = _mm(oa, wa_ref[...])
    ys = jax.nn.gelu(ys_ref[rs, :])
    yield
    ys = ys * _sigmoid(_mm(ys, wglu_ref[...]))
    yield
    y_b = _mm(ys, wb_ref[...])
    mixed = _sigmoid(ga_ref[rs, :]).astype(F32) * y_a + _sigmoid(gb_ref[rs, :]).astype(F32) * y_b
    yield
    x1 = x_ref[rs, :] + _mm(mixed, wo_ref[...])
    x1_ref[rs, :] = x1
    hn = x1 * lax.rsqrt(jnp.mean(x1 * x1, axis=-1, keepdims=True) + RMS_EPS) * nf_ref[...]
    _slab_store(hn_ref, hn, part * pr)
    yield

    wr = wr_ref[...]
    w_hi = wr.astype(BF16)
    w_lo = (wr - w_hi.astype(F32)).astype(BF16)
    hn_hi = hn.astype(BF16)
    hn_lo = (hn - hn_hi.astype(F32)).astype(BF16)
    both = _mm_nt(jnp.concatenate([w_hi, w_lo], axis=0), hn_hi)
    logits = both[:ROUTER_ROWS] + both[ROUTER_ROWS:] + _mm_nt(w_hi, hn_lo)
    yield
    coarse = logits[N_EXPERTS:N_EXPERTS + MOE_GROUPS, :]
    cm = jnp.max(coarse, axis=0, keepdims=True)
    ce = jnp.exp(coarse - cm)
    pc = ce / jnp.sum(ce, axis=0, keepdims=True)
    p_sel = jnp.max(pc, axis=0, keepdims=True)
    gi = lax.broadcasted_iota(I32, pc.shape, 0)
    g_sel = jnp.min(jnp.where(pc == p_sel, gi, MOE_GROUPS), axis=0, keepdims=True)
    fine = jnp.zeros((EXPERTS_PER_GROUP, logits.shape[1]), F32)
    for g in range(MOE_GROUPS):
        fine = fine + jnp.where(g_sel == g, logits[g * EXPERTS_PER_GROUP:(g + 1) * EXPERTS_PER_GROUP, :], 0.0)
    fm = jnp.max(fine, axis=0, keepdims=True)
    fe = jnp.exp(fine - fm)
    pf = fe / jnp.sum(fe, axis=0, keepdims=True)
    ei = lax.broadcasted_iota(I32, pf.shape, 0)
    v1 = jnp.max(pf, axis=0, keepdims=True)
    i1 = jnp.min(jnp.where(pf == v1, ei, EXPERTS_PER_GROUP), axis=0, keepdims=True)
    rest = jnp.where(ei == i1, -1.0, pf)
    v2 = jnp.max(rest, axis=0, keepdims=True)
    i2 = jnp.min(jnp.where(rest == v2, ei, EXPERTS_PER_GROUP), axis=0, keepdims=True)
    tot = v1 + v2
    rw_ref[0:1, rs] = v1 / tot * p_sel
    rw_ref[1:2, rs] = v2 / tot * p_sel

    tok = pl.program_id(0) * (pr * POSTMIX_PARTS) + part * pr + lax.broadcasted_iota(I32, (1, pr), 1)
    ph = jnp.zeros((1, pr), I32)
    for r in range(1, MOE_PHASES):
        ph = ph + (tok >= r * range_tok).astype(I32)
    bsel = [ph * N_EXPERTS + g_sel * EXPERTS_PER_GROUP + ix for ix in (i1, i2)]
    bi = lax.broadcasted_iota(I32, (MOE_PHASES * N_EXPERTS, pr), 0)
    onehot = [(bi == b).astype(F32) for b in bsel]
    cnt = onehot[0] + onehot[1]
    before = carry_ref[:, 0:1] + jnp.dot(cnt.astype(BF16), su_ref[0:pr, 0:pr], preferred_element_type=F32)
    for s in range(TOP_K):
        bkt_ref[s:s + 1, rs] = bsel[s]
        rank_ref[s:s + 1, rs] = jnp.sum(onehot[s] * before, axis=0, keepdims=True).astype(I32)
    carry_ref[...] = carry_ref[...] + jnp.sum(cnt, axis=1, keepdims=True)


def _postmix(prompt, sample, weights, nb):
    n_p = prompt[0].shape[0]
    n_s = sample[0].shape[0]
    t = n_p // nb
    tt = min(ROW_TILE, t, n_s)
    nt = t // tt
    nblk_p = n_p // tt
    nblk = nblk_p + n_s // tt
    n_total = n_p + n_s
    prow = lambda i: (jnp.minimum(i, nblk_p - 1), 0)
    pys = lambda i: (jnp.minimum(i, nblk_p - 1) % nt, jnp.minimum(i, nblk_p - 1) // nt)
    srow = lambda i: (jnp.maximum(i - nblk_p, 0), 0)
    const = lambda i: (0, 0)

    def stream_specs(row, ysmap):
        return [pl.BlockSpec((tt, D_MODEL), row),
                pl.BlockSpec((tt, DN_WIDTH), row),
                pl.BlockSpec((tt, DN_WIDTH), row),
                pl.BlockSpec((tt, S5_WIDTH), ysmap),
                pl.BlockSpec((tt, D_MODEL), row),
                pl.BlockSpec((tt, D_MODEL), row)]

    weight_specs = [pl.BlockSpec((1, DN_WIDTH), const),
                    pl.BlockSpec((DN_WIDTH, DN_WIDTH), const),
                    pl.BlockSpec((DN_WIDTH, D_MODEL), const),
                    pl.BlockSpec((S5_WIDTH, S5_WIDTH), const),
                    pl.BlockSpec((S5_WIDTH, D_MODEL), const),
                    pl.BlockSpec((D_MODEL, D_MODEL), const),
                    pl.BlockSpec((1, D_MODEL), const),
                    pl.BlockSpec((ROUTER_ROWS, D_MODEL), const),
                    pl.BlockSpec((tt, tt), const)]
    xp, op, zp, ysp, gap, gbp = prompt
    nbk = MOE_PHASES * N_EXPERTS
    earlier = jnp.triu(jnp.ones((tt, tt), F32), k=1).astype(BF16)
    return pl.pallas_call(
        functools.partial(_postmix_kernel, nblk_p=nblk_p, range_tok=n_total // MOE_PHASES),
        grid=(nblk,),
        in_specs=stream_specs(prow, pys) + stream_specs(srow, srow) + weight_specs,
        out_specs=[pl.BlockSpec((tt, D_MODEL), lambda i: (i, 0)),
                   pl.BlockSpec((tt * ROW_SLAB, LANES), lambda i: (i, 0)),
                   pl.BlockSpec((TOP_K, tt), lambda i: (0, i)),
                   pl.BlockSpec((TOP_K, tt), lambda i: (0, i)),
                   pl.BlockSpec((TOP_K, tt), lambda i: (0, i)),
                   pl.BlockSpec((nbk, LANES), const)],
        out_shape=[jax.ShapeDtypeStruct((n_total, D_MODEL), F32),
                   jax.ShapeDtypeStruct((n_total * ROW_SLAB, LANES), F32),
                   jax.ShapeDtypeStruct((TOP_K, n_total), I32),
                   jax.ShapeDtypeStruct((TOP_K, n_total), I32),
                   jax.ShapeDtypeStruct((TOP_K, n_total), F32),
                   jax.ShapeDtypeStruct((nbk, LANES), F32)],
        scratch_shapes=[pltpu.VMEM((nbk, LANES), F32)],
        compiler_params=_cparams(("arbitrary",)),
        name="postmix",
    )(xp, op, zp, ysp, gap, gbp, *sample, *weights, earlier)


def _wait_slabs(buf, sem):
    pltpu.make_async_copy(buf, buf, sem).wait()


def _moe_kernel(texp_ref, tph_ref, tsrc_ref, tnv_ref, tfirst_ref, tslot_ref, tnext_ref, otok_ref,
                hn_hbm, wu_hbm, wd_hbm, y_ref, hnv, xbuf, wu_buf, wd_buf, wub, wdb, sem, wsem):
    i = pl.program_id(0)
    tm = MOE_TILE
    rs = ROW_SLAB
    nv = tnv_ref[i]
    ph = tph_ref[i]
    range_rows = hnv.shape[0]

    def weight_copies(e, sl):
        return (pltpu.make_async_copy(wu_hbm.at[e], wu_buf.at[sl], wsem.at[sl]),
                pltpu.make_async_copy(wd_hbm.at[e], wd_buf.at[sl], wsem.at[sl]))

    @pl.when(i == 0)
    def _():
        for p, c in enumerate(weight_copies(texp_ref[0], 0)):
            c.start(priority=p % DMA_QUEUES)

    @pl.when(jnp.logical_and(nv > 0, jnp.logical_or(i == 0, ph != tph_ref[jnp.maximum(i - 1, 0)])))
    def _():
        piece = range_rows // DMA_QUEUES
        loads = [pltpu.make_async_copy(hn_hbm.at[pl.ds(pl.multiple_of(ph * range_rows + p * piece, rs), piece), :],
                                       hnv.at[pl.ds(p * piece, piece), :], sem) for p in range(DMA_QUEUES)]
        for p, c in enumerate(loads):
            c.start(priority=p)
        for c in loads:
            c.wait()

    for sl in range(2):
        @pl.when(jnp.logical_and(jnp.logical_and(nv > 0, tfirst_ref[i] == 1), tslot_ref[i] == sl))
        def _():
            for c in weight_copies(texp_ref[i], sl):
                c.wait()

            @pl.when(tnext_ref[i] >= 0)
            def _():
                for p, c in enumerate(weight_copies(tnext_ref[i], 1 - sl)):
                    c.start(priority=p % DMA_QUEUES)

            wub[...] = wu_buf[sl].astype(BF16)
            wdb[...] = wd_buf[sl].astype(BF16)

    @pl.when(nv == 0)
    def _():
        y_ref[...] = jnp.zeros_like(y_ref)

    @pl.when(nv > 0)
    def _():
        src0 = tsrc_ref[i]
        for r in range(tm):
            tok8 = pl.multiple_of(otok_ref[src0 + r], rs)
            xbuf[pl.ds(r * rs, rs), :] = hnv[pl.ds(tok8, rs), :]
        x = _slab_load(xbuf, tm).astype(BF16)
        hu = jnp.dot(x, wub[...], preferred_element_type=F32)
        gate = hu[:, :EXPERT_FF]
        up = hu[:, EXPERT_FF:]
        act = gate * _sigmoid(gate) * up
        _slab_store(y_ref, jnp.dot(act.astype(BF16), wdb[...], preferred_element_type=F32))


def _moe(hn, w_up, w_down, plan):
    ntiles = plan[0].shape[0]
    grid_spec = pltpu.PrefetchScalarGridSpec(
        num_scalar_prefetch=len(plan),
        grid=(ntiles,),
        in_specs=[pl.BlockSpec(memory_space=pl.ANY),
                  pl.BlockSpec(memory_space=pl.ANY),
                  pl.BlockSpec(memory_space=pl.ANY)],
        out_specs=pl.BlockSpec((MOE_TILE * ROW_SLAB, LANES), lambda i, *_: (i, 0)),
        scratch_shapes=[pltpu.VMEM((hn.shape[0] // MOE_PHASES, LANES), F32),
                        pltpu.VMEM((MOE_TILE * ROW_SLAB, LANES), F32),
                        pltpu.VMEM((2, D_MODEL, 2 * EXPERT_FF), F32),
                        pltpu.VMEM((2, EXPERT_FF, D_MODEL), F32),
                        pltpu.VMEM((D_MODEL, 2 * EXPERT_FF), BF16),
                        pltpu.VMEM((EXPERT_FF, D_MODEL), BF16),
                        pltpu.SemaphoreType.DMA,
                        pltpu.SemaphoreType.DMA((2,))])
    return pl.pallas_call(
        _moe_kernel,
        grid_spec=grid_spec,
        out_shape=jax.ShapeDtypeStruct((ntiles * MOE_TILE * ROW_SLAB, LANES), F32),
        compiler_params=_cparams(("arbitrary",)),
        name="moe",
    )(*plan, hn, w_up, w_down)


def _combine_kernel(pos_ref, x1_ref, ys_hbm, w_ref, nw_ref, outp_ref, outs_ref,
                    ybuf0, ybuf1, sem, *, nblk_p, n_tok):
    i = pl.program_id(0)
    nsteps = pl.num_programs(0)
    tt = x1_ref.shape[0]
    rs = ROW_SLAB
    slot = lax.rem(i, 2)
    ybuf = (ybuf0, ybuf1)

    def start_gather(step, sl):
        base = step * tt
        for r in range(tt * TOP_K):
            j, s = divmod(r, TOP_K)
            p8 = pl.multiple_of(pos_ref[s * n_tok + base + j], rs)
            pltpu.make_async_copy(ys_hbm.at[pl.ds(p8, rs), :], ybuf[sl].at[pl.ds((s * tt + j) * rs, rs), :],
                                  sem.at[sl]).start(priority=r % DMA_QUEUES)

    @pl.when(i == 0)
    def _():
        start_gather(0, 0)

    for sl in range(2):
        @pl.when(slot == sl)
        def _():
            _wait_slabs(ybuf[sl], sem.at[sl])
            start_gather(jnp.minimum(i + 1, nsteps - 1), 1 - sl)
            w = w_ref[...]
            y0 = _slab_load(ybuf[sl], tt, 0)
            y1 = _slab_load(ybuf[sl], tt, tt * rs)
            x = x1_ref[...] + w[:, 0:1] * y0 + w[:, 1:2] * y1
            res = x * lax.rsqrt(jnp.mean(x * x, axis=-1, keepdims=True) + RMS_EPS) * nw_ref[...]

            @pl.when(i < nblk_p)
            def _():
                outp_ref[...] = res

            @pl.when(i >= nblk_p)
            def _():
                outs_ref[...] = res

        @pl.when(jnp.logical_and(slot == sl, i == nsteps - 1))
        def _():
            _wait_slabs(ybuf[1 - sl], sem.at[1 - sl])


def _combine(x1, ysorted, pos8, wtok, nw, n_p):
    n = x1.shape[0]
    tt = math.gcd(math.gcd(n_p, n - n_p), COMBINE_TILE)
    nblk_p = n_p // tt
    grid_spec = pltpu.PrefetchScalarGridSpec(
        num_scalar_prefetch=1,
        grid=(n // tt,),
        in_specs=[pl.BlockSpec((tt, D_MODEL), lambda i, *_: (i, 0)),
                  pl.BlockSpec(memory_space=pl.ANY),
                  pl.BlockSpec((tt, TOP_K), lambda i, *_: (i, 0)),
                  pl.BlockSpec((1, D_MODEL), lambda i, *_: (0, 0))],
        out_specs=[pl.BlockSpec((tt, D_MODEL), lambda i, *_: (jnp.minimum(i, nblk_p - 1), 0)),
                   pl.BlockSpec((tt, D_MODEL), lambda i, *_: (jnp.maximum(i - nblk_p, 0), 0))],
        scratch_shapes=[pltpu.VMEM((tt * TOP_K * ROW_SLAB, LANES), F32),
                        pltpu.VMEM((tt * TOP_K * ROW_SLAB, LANES), F32),
                        pltpu.SemaphoreType.DMA((2,))])
    return pl.pallas_call(
        functools.partial(_combine_kernel, nblk_p=nblk_p, n_tok=n),
        grid_spec=grid_spec,
        out_shape=[jax.ShapeDtypeStruct((n_p, D_MODEL), F32),
                   jax.ShapeDtypeStruct((n - n_p, D_MODEL), F32)],
        compiler_params=_cparams(("arbitrary",)),
        name="combine",
    )(pos8, x1, ysorted, wtok, nw)


def _route_plan(bkt, rank, cnt, n_tok):
    tm = MOE_TILE
    n_assign = n_tok * TOP_K
    nbk = MOE_PHASES * N_EXPERTS
    ntiles = n_assign // tm + nbk
    range_tok = n_tok // MOE_PHASES
    b_flat = bkt.T.reshape(n_assign)
    id_bits = max(n_assign - 1, 1).bit_length()
    assert nbk << id_bits < 2 ** 31
    packed = jnp.sort((b_flat << id_bits) | jnp.arange(n_assign, dtype=I32))
    order = packed & ((1 << id_bits) - 1)
    counts = cnt[:, 0].astype(I32)
    cstart = jnp.cumsum(counts) - counts
    tiles_b = (counts + tm - 1) // tm
    tend = jnp.cumsum(tiles_b)
    tstart = tend - tiles_b
    tile_id = jnp.arange(ntiles, dtype=I32)
    tbk = jnp.minimum(jnp.sum((tile_id[:, None] >= tend[None, :]).astype(I32), axis=1), nbk - 1)
    onehot = (tbk[:, None] == jnp.arange(nbk, dtype=I32)[None, :]).astype(I32)
    pick = lambda v: jnp.sum(onehot * v[None, :], axis=1)
    done = (tile_id - pick(tstart)) * tm
    tnv = jnp.where(tile_id < tend[-1], jnp.clip(pick(counts) - done, 0, tm), 0)
    tsrc = jnp.where(tnv > 0, pick(cstart) + done, 0)
    texp = tbk % N_EXPERTS
    tph = tbk // N_EXPERTS
    nonempty = counts > 0
    bslot = (jnp.cumsum(nonempty.astype(I32)) - 1) % 2
    bidx = jnp.where(nonempty, jnp.arange(nbk, dtype=I32), nbk)
    nxt = jnp.concatenate([lax.cummin(bidx[::-1])[::-1][1:], jnp.full((1,), nbk, I32)])
    bnext = jnp.where(nxt < nbk, nxt % N_EXPERTS, -1)
    tfirst = jnp.logical_and(tnv > 0, done == 0).astype(I32)
    tslot = pick(bslot)
    tnext = pick(bnext)
    otok8 = jnp.concatenate([((order // TOP_K) % range_tok) * ROW_SLAB, jnp.zeros((tm,), I32)])
    plan = tuple(a.astype(I32) for a in (texp, tph, tsrc, tnv, tfirst, tslot, tnext, otok8))
    first = jnp.sum((bkt[:, :, None] == jnp.arange(nbk, dtype=I32)[None, None, :]).astype(I32)
                    * (tstart * tm)[None, None, :], axis=2)
    pos8 = ((first + rank) * ROW_SLAB).reshape(n_assign)
    return plan, pos8.astype(I32)


def _block_diag(m):
    g, a, b = m.shape
    eye = jnp.eye(g, dtype=m.dtype)
    return (eye[:, None, :, None] * m[:, :, None, :]).reshape(g * a, g * b)


def kernel(x_prompt, x_sample, state_conv, state_delta, state_ssm_re, state_ssm_im, norm_mix_w, w_in, conv_w, a_log, dt_bias, head_norm_w, w_a_up, s5_lambda_re, s5_lambda_im, s5_log_step, s5_b_re, s5_b_im, s5_c_re, s5_c_im, s5_d, w_glu, w_b_up, w_o, norm_ffn_w, w_router_coarse, w_router_fine, w_expert_up, w_expert_down, norm_final_w):
    bp, tp, _ = x_prompt.shape
    bs, ts, _ = x_sample.shape
    n_p = bp * tp
    n_s = bs * ts
    n_tok = n_p + n_s
    l = 0

    w = w_in[l].astype(BF16)
    c_ab = W1_COLS + 2 * DN_HEADS
    w_parts = (w[:, :W1_COLS], w[:, c_ab:],
               jnp.concatenate([w[:, W1_COLS:c_ab], jnp.zeros((D_MODEL, LANES - 2 * DN_HEADS), BF16)], axis=1))
    nw_mix = norm_mix_w[l].reshape(1, D_MODEL)
    pad8 = lambda v: jnp.concatenate([v, jnp.zeros((LANES - DN_HEADS,), F32)]).reshape(1, LANES)
    gate_p = jnp.concatenate([pad8(a_log[l]), pad8(dt_bias[l])], axis=0)
    seg = _block_diag(jnp.ones((DN_HEADS, DN_HEAD_DIM, DN_HEAD_DIM), BF16))
    chan_rows = lambda b: jnp.swapaxes(b, 1, 2).reshape(S5_WIDTH, S5_STATE)
    state_rows = lambda c: jnp.tile(jnp.swapaxes(c, 1, 2).reshape(S5_FLAT, S5_GROUP_CH),
                                    (1, LANES // S5_GROUP_CH))
    lam = jnp.concatenate([s5_lambda_re[l].reshape(1, S5_FLAT), s5_lambda_im[l].reshape(1, S5_FLAT),
                           jnp.repeat(s5_log_step[l], S5_STATE).reshape(1, S5_FLAT),
                           jnp.zeros((SUBLANES - 3, S5_FLAT), F32)], axis=0)
    s5_params = (chan_rows(s5_b_re[l]), chan_rows(s5_b_im[l]), lam,
                 state_rows(s5_c_re[l]), state_rows(s5_c_im[l]), s5_d[l].reshape(1, S5_WIDTH))
    hw = jnp.tile(head_norm_w[l], DN_HEADS).reshape(1, DN_WIDTH)
    wr = jnp.concatenate([w_router_fine[l].T, w_router_coarse[l].T,
                          jnp.zeros((ROUTER_ROWS - N_EXPERTS - MOE_GROUPS, D_MODEL), F32)], axis=0)
    pm_weights = (hw, seg, w_a_up[l].astype(BF16), w_glu[l].astype(BF16), w_b_up[l].astype(BF16),
                  w_o[l].astype(BF16), norm_ffn_w[l].reshape(1, D_MODEL), wr)

    xp2 = x_prompt.reshape(n_p, D_MODEL)
    q_p, k_p, v_p, gates_p, conv_p, z_p, u_p, ga_p, gb_p = _inprep(
        xp2, nw_mix, w_parts, jnp.zeros((bp, SUBLANES, QKV_DIM), F32), conv_w[l], gate_p, seg, bp, 1)
    o_p, delta_p = _delta_prompt(q_p, k_p, v_p, gates_p, bp)
    ys_p, h_p = _s5(u_p, s5_params, jnp.zeros((bp, 2 * S5_FLAT), F32), bp, True)

    xs2 = jnp.swapaxes(x_sample, 0, 1).reshape(n_s, D_MODEL)
    cinit_s = jnp.swapaxes(state_conv[l], 0, 1).reshape(1, (CONV_W - 1) * bs, QKV_DIM)
    q_s, k_s, v_s, gate_s, conv_s, z_s, u_s, ga_s, gb_s = _inprep(
        xs2, nw_mix, w_parts, cinit_s, conv_w[l], gate_p, seg, 1, bs)
    s0t = jnp.transpose(state_delta[l], (1, 2, 3, 0)).reshape(DN_HEADS * DN_HEAD_DIM * DN_HEAD_DIM, bs)
    o_s, delta_st = _delta_sample(q_s, k_s, v_s, gate_s, s0t, bs, ts)
    delta_s = jnp.transpose(delta_st.reshape(DN_HEADS, DN_HEAD_DIM, DN_HEAD_DIM, bs), (3, 0, 1, 2))
    h0_s = jnp.concatenate([state_ssm_re[l].reshape(bs, S5_FLAT), state_ssm_im[l].reshape(bs, S5_FLAT)], axis=1)
    ys_s, h_s = _s5(u_s, s5_params, h0_s, bs, False)
    x1, hn, bkt, rank, rw, cnt = _postmix((xp2, o_p, z_p, ys_p, ga_p, gb_p), (xs2, o_s, z_s, ys_s, ga_s, gb_s),
                                          pm_weights, bp)

    plan, pos8 = _route_plan(bkt, rank, cnt, n_tok)
    ysorted = _moe(hn, w_expert_up[l], w_expert_down[l], plan)
    y_p, y_s = _combine(x1, ysorted, pos8, rw.T, norm_final_w.reshape(1, D_MODEL), n_p)

    y_prompt = y_p.reshape(bp, tp, D_MODEL)
    y_sample = jnp.swapaxes(y_s.reshape(ts, bs, D_MODEL), 0, 1)
    conv_sample = jnp.swapaxes(conv_s.reshape(CONV_W - 1, bs, QKV_DIM), 0, 1)
    return (y_prompt, y_sample,
            conv_p[None], delta_p[None],
            h_p[:, :S5_FLAT].reshape(1, bp, S5_GROUPS, S5_STATE), h_p[:, S5_FLAT:].reshape(1, bp, S5_GROUPS, S5_STATE),
            conv_sample[None], delta_s[None],
            h_s[:, :S5_FLAT].reshape(1, bs, S5_GROUPS, S5_STATE), h_s[:, S5_FLAT:].reshape(1, bs, S5_GROUPS, S5_STATE))
```

```python
import functools
import math

import jax
import jax.numpy as jnp
import numpy as np
from jax import lax
from jax.experimental import pallas as pl
from jax.experimental.pallas import tpu as pltpu

F32 = jnp.float32
BF16 = jnp.bfloat16
I32 = jnp.int32

D_MODEL = 1024
DN_HEADS = 8
DN_HEAD_DIM = 64
DN_WIDTH = DN_HEADS * DN_HEAD_DIM
QKV_DIM = 3 * DN_WIDTH
CONV_W = 4
DN_CHUNK = 64
S5_GROUP_CH = 16
S5_WIDTH = D_MODEL // 2
S5_GROUPS = S5_WIDTH // S5_GROUP_CH
S5_STATE = 64
S5_FLAT = S5_GROUPS * S5_STATE
MOE_GROUPS = 4
EXPERTS_PER_GROUP = 8
N_EXPERTS = MOE_GROUPS * EXPERTS_PER_GROUP
TOP_K = 2
EXPERT_FF = 256
RMS_EPS = 1e-6
L2_EPS = 1e-6

LANES = 128
SUBLANES = 8
VMEM_LIMIT = 56 * 1024 * 1024

W1_COLS = QKV_DIM + DN_WIDTH
W2_COLS = S5_WIDTH + 2 * D_MODEL

ROW_TILE = 512
INPREP_PARTS = 2
POSTMIX_PARTS = 2
MOE_TILE = 256
MOE_PHASES = 2
COMBINE_TILE = 512
DMA_QUEUES = 2
DELTA_SUBCHUNKS = 4
S5_SUPER = 2
S5_TILE_ROWS = 1024
S5_SCAN_SPLIT = 2
ROUTER_ROWS = 40


def _mm(a, b):
    return jnp.dot(a.astype(BF16), b.astype(BF16), preferred_element_type=F32)


def _mm_nt(a, b):
    return lax.dot_general(a.astype(BF16), b.astype(BF16), (((1,), (1,)), ((), ())),
                           preferred_element_type=F32)


def _sigmoid(x):
    return 0.5 * jnp.tanh(0.5 * x) + 0.5


def _cparams(sem):
    return pltpu.CompilerParams(dimension_semantics=sem, vmem_limit_bytes=VMEM_LIMIT)


ROW_SLAB = D_MODEL // LANES


def _slab_load(ref, rows, first=0, pitch=ROW_SLAB):
    return jnp.concatenate([ref[pl.ds(first + j, rows, stride=pitch), :] for j in range(ROW_SLAB)], axis=1)


def _slab_store(ref, x, first=0):
    for j in range(ROW_SLAB):
        ref[pl.ds(first * ROW_SLAB + j, x.shape[0], stride=ROW_SLAB), :] = x[:, j * LANES:(j + 1) * LANES]


def _softplus(x):
    return jnp.maximum(x, 0.0) + jnp.log1p(jnp.exp(-jnp.abs(x)))


def _inprep_kernel(x_ref, nw_ref, w1_ref, w2_ref, wab_ref, cinit_ref, cw_ref, gp_ref, seg_ref,
                   q_ref, k_ref, v_ref, gate_ref, cnew_ref, z_ref, u_ref, ga_ref, gb_ref, xp_ref,
                   *, shift, rc, rows):
    @pl.when(pl.program_id(1) == 0)
    def _():
        xp_ref[0:rc, :] = cinit_ref[0]

    seg = seg_ref[...]
    pr = rows // INPREP_PARTS

    def part_stages(part):
        rs = slice(part * pr, (part + 1) * pr)
        x = x_ref[rs, :]
        h = x * lax.rsqrt(jnp.mean(x * x, axis=-1, keepdims=True) + RMS_EPS) * nw_ref[...]
        hb = h.astype(BF16)

        def proj(w_ref, lo, hi):
            return jnp.dot(hb, w_ref[:, lo:hi], preferred_element_type=F32)

        xp_ref[rc + part * pr:rc + (part + 1) * pr, :] = proj(w1_ref, 0, QKV_DIM)
        ab = proj(wab_ref, 0, LANES)
        yield
        z_ref[rs, :] = proj(w1_ref, QKV_DIM, W1_COLS).astype(z_ref.dtype)
        u_ref[rs, :] = proj(w2_ref, 0, S5_WIDTH).astype(u_ref.dtype)
        acc = None
        for i in range(CONV_W):
            lo = rc + part * pr + (i - (CONV_W - 1)) * shift
            term = xp_ref[lo:lo + pr, :] * cw_ref[i:i + 1, :]
            acc = term if acc is None else acc + term
        y = acc * _sigmoid(acc)
        yield
        ga_ref[rs, :] = proj(w2_ref, S5_WIDTH, S5_WIDTH + D_MODEL).astype(ga_ref.dtype)
        q = y[:, 0:DN_WIDTH]
        k = y[:, DN_WIDTH:2 * DN_WIDTH]
        q_ref[rs, :] = q * lax.rsqrt(jnp.dot((q * q).astype(BF16), seg, preferred_element_type=F32) + L2_EPS)
        k_ref[rs, :] = k * lax.rsqrt(jnp.dot((k * k).astype(BF16), seg, preferred_element_type=F32) + L2_EPS)
        v_ref[rs, :] = y[:, 2 * DN_WIDTH:]
        yield
        gb_ref[rs, :] = proj(w2_ref, S5_WIDTH + D_MODEL, W2_COLS).astype(gb_ref.dtype)
        g = -jnp.exp(gp_ref[0:1, :]) * _softplus(ab + gp_ref[1:2, :])
        beta = _sigmoid(ab)
        lane = lax.broadcasted_iota(I32, ab.shape, 1)
        gate_ref[rs, :] = jnp.where(lane < DN_HEADS, g, beta)

    live = []
    pending = [part_stages(p) for p in range(INPREP_PARTS)]
    while live or pending:
        if pending:
            live.append(pending.pop(0))
        live = [g for g in live if next(g, StopIteration) is not StopIteration]

    keep = (CONV_W - 1) * shift
    cnew_ref[0] = xp_ref[rc + rows - keep:rc + rows, :]
    xp_ref[0:rc, :] = xp_ref[rows:rows + rc, :]


def _inprep(x2d, nw, w_parts, cinit, conv_w, gate_p, seg, nb, shift):
    n = x2d.shape[0]
    r = n // nb
    rows = min(ROW_TILE, r)
    nt = r // rows
    rc = cinit.shape[1]
    keep = (CONV_W - 1) * shift
    row = lambda b, i: (b * nt + i, 0)
    const = lambda b, i: (0, 0)
    kern = functools.partial(_inprep_kernel, shift=shift, rc=rc, rows=rows)
    outs = pl.pallas_call(
        kern,
        grid=(nb, nt),
        in_specs=[pl.BlockSpec((rows, D_MODEL), row),
                  pl.BlockSpec((1, D_MODEL), const),
                  pl.BlockSpec((D_MODEL, W1_COLS), const),
                  pl.BlockSpec((D_MODEL, W2_COLS), const),
                  pl.BlockSpec((D_MODEL, LANES), const),
                  pl.BlockSpec((1, rc, QKV_DIM), lambda b, i: (b, 0, 0)),
                  pl.BlockSpec((CONV_W, QKV_DIM), const),
                  pl.BlockSpec((2, LANES), const),
                  pl.BlockSpec((DN_WIDTH, DN_WIDTH), const)],
        out_specs=[pl.BlockSpec((rows, DN_WIDTH), row),
                   pl.BlockSpec((rows, DN_WIDTH), row),
                   pl.BlockSpec((rows, DN_WIDTH), row),
                   pl.BlockSpec((rows, LANES), row),
                   pl.BlockSpec((1, keep, QKV_DIM), lambda b, i: (b, 0, 0)),
                   pl.BlockSpec((rows, DN_WIDTH), row),
                   pl.BlockSpec((rows, S5_WIDTH), lambda b, i: (i, b)),
                   pl.BlockSpec((rows, D_MODEL), row),
                   pl.BlockSpec((rows, D_MODEL), row)],
        out_shape=[jax.ShapeDtypeStruct((n, DN_WIDTH), F32),
                   jax.ShapeDtypeStruct((n, DN_WIDTH), F32),
                   jax.ShapeDtypeStruct((n, DN_WIDTH), F32),
                   jax.ShapeDtypeStruct((n, LANES), F32),
                   jax.ShapeDtypeStruct((nb, keep, QKV_DIM), F32),
                   jax.ShapeDtypeStruct((n, DN_WIDTH), BF16),
                   jax.ShapeDtypeStruct((r, nb * S5_WIDTH), BF16),
                   jax.ShapeDtypeStruct((n, D_MODEL), BF16),
                   jax.ShapeDtypeStruct((n, D_MODEL), BF16)],
        scratch_shapes=[pltpu.VMEM((rc + rows, QKV_DIM), F32)],
        compiler_params=_cparams(("arbitrary", "arbitrary")),
        name="inprep",
    )(x2d, nw, *w_parts, cinit, conv_w, gate_p, seg)
    return outs


def _delta_home(low, h, x, other=0.0):
    return jnp.where(low, x, other) if h % 2 == 0 else jnp.where(low, other, x)


def _delta_prepare(q_ref, k_ref, v_ref, gate_ref, tril_ref, bufs, *, nsub):
    sol_buf, wq_buf, qk_buf, kdec_buf, dl_buf = bufs
    c = DN_CHUNK
    dk = DN_HEAD_DIM

    def home(h, x, other=0.0):
        return _delta_home(low, h, x, other)

    rowi2 = lax.broadcasted_iota(I32, (c, 2 * c), 0)
    lane2 = lax.broadcasted_iota(I32, (c, 2 * c), 1)
    coli2 = lane2 & (c - 1)
    causal2 = rowi2 >= coli2
    strict2 = rowi2 > coli2
    low = lane2 < dk
    tril = tril_ref[...]
    pairs = [(j, h) for j in range(nsub) for h in range(DN_HEADS)]
    units = [(j, pr) for j in range(nsub) for pr in range(DN_HEADS // 2)]
    rows = [slice(j * c, (j + 1) * c) for j in range(nsub)]
    gate = [gate_ref[rows[j], :] for j in range(nsub)]
    gc_all = [_split3_dot_left(tril, gate[j]) for j in range(nsub)]
    gc_t = [gc_all[j].T for j in range(nsub)]

    def block(ref, j, pr):
        return ref[rows[j], pr * LANES:(pr + 1) * LANES]

    gfull = {(j, h): jnp.broadcast_to(gc_all[j][:, h:h + 1], (c, 2 * c)) for j, h in pairs}
    g2 = {(j, pr): jnp.where(low, gfull[j, 2 * pr], gfull[j, 2 * pr + 1]) for j, pr in units}
    b2 = {(j, pr): jnp.where(low, gate[j][:, DN_HEADS + 2 * pr:DN_HEADS + 2 * pr + 1],
                             gate[j][:, DN_HEADS + 2 * pr + 1:DN_HEADS + 2 * pr + 2]) for j, pr in units}
    kp = {u: block(k_ref, *u) for u in units}
    qp = {u: block(q_ref, *u) * (dk ** -0.5) for u in units}
    egc2 = {u: jnp.exp(g2[u]) for u in units}
    kb2 = {u: kp[u] * b2[u] for u in units}
    vb2 = {u: block(v_ref, *u) * b2[u] for u in units}
    kw2s = {u: pltpu.roll(kb2[u] * egc2[u], dk, axis=1) for u in units}
    qd2 = {u: qp[u] * egc2[u] for u in units}
    glast2 = {u: g2[u][c - 1:c, :] for u in units}
    kdec_t2 = {u: (kp[u] * jnp.exp(glast2[u] - g2[u])).T for u in units}
    dlast2 = {u: jnp.exp(glast2[u]) for u in units}
    kk = {u: jnp.concatenate([kp[u], kp[u]], axis=0) for u in units}
    yield

    grow2 = {(j, h): jnp.concatenate([gc_t[j][h:h + 1, :], gc_t[j][h:h + 1, :]], axis=1) for j, h in pairs}
    decay = {p: jnp.where(causal2, jnp.exp(jnp.where(causal2, gfull[p] - grow2[p], 0.0)), 0.0) for p in pairs}
    gram = {(j, h): _mm_nt(jnp.concatenate([home(h, kb2[j, h // 2]), home(h, qp[j, h // 2])], axis=0), kk[j, h // 2])
            for j, h in pairs}
    mat = {p: jnp.where(strict2, gram[p][:c] * decay[p], 0.0).astype(BF16) for p in pairs}
    qk = {p: jnp.where(causal2, gram[p][c:] * decay[p], 0.0) for p in pairs}
    sol = {(j, h): home(h, vb2[j, h // 2], kw2s[j, h // 2]) for j, h in pairs}
    yield
    levels = int(math.log2(c))
    zeros2 = jnp.zeros((c, 2 * c), BF16)
    for lvl in range(levels):
        hi = {p: sol[p].astype(BF16) for p in pairs}
        lo = {p: (sol[p] - hi[p].astype(F32)).astype(BF16) for p in pairs}
        if lvl < levels - 1:
            y = {p: jnp.dot(mat[p], jnp.concatenate([jnp.concatenate([hi[p], mat[p]], axis=1),
                                                     jnp.concatenate([lo[p], zeros2], axis=1)], axis=0),
                            preferred_element_type=F32) for p in pairs}
            mat = {p: y[p][:, 2 * dk:].astype(BF16) for p in pairs}
            upd = {p: y[p][:, :2 * dk] for p in pairs}
        else:
            upd = {p: jnp.dot(mat[p], jnp.concatenate([hi[p], lo[p]], axis=0), preferred_element_type=F32)
                   for p in pairs}
        sol = {p: (sol[p] - upd[p]) if lvl == 0 else (sol[p] + upd[p]) for p in pairs}
        yield
    for j, h in pairs:
        n = j * DN_HEADS + h
        sol_buf[n] = sol[j, h]
        wq_buf[n] = jnp.concatenate([home(h, 0.0, sol[j, h]), home(h, qd2[j, h // 2])], axis=0).astype(BF16)
        qk_buf[n] = qk[j, h].astype(BF16)
    for j, pr in units:
        n = j * (DN_HEADS // 2) + pr
        kdec_buf[n] = kdec_t2[j, pr].astype(BF16)
        dl_buf[n] = jnp.broadcast_to(dlast2[j, pr], (SUBLANES, LANES))


def _delta_apply(bufs, o_ref, s_ref, *, nsub):
    sol_buf, wq_buf, qk_buf, kdec_buf, dl_buf = bufs
    c = DN_CHUNK
    dk = DN_HEAD_DIM
    heads = range(DN_HEADS)
    low = lax.broadcasted_iota(I32, (c, 2 * c), 1) < dk
    s = [s_ref[h] for h in heads]
    for j in range(nsub):
        ws, v_new, o_h = [], [], []
        for h in heads:
            n = j * DN_HEADS + h
            ws.append(jnp.dot(wq_buf[n], jnp.concatenate([s[h], s[h]], axis=0).astype(BF16),
                              preferred_element_type=F32))
        yield
        for h in heads:
            v_new.append(sol_buf[j * DN_HEADS + h] - ws[h][:c])
        for h in heads:
            o_h.append(ws[h][c:] + jnp.dot(qk_buf[j * DN_HEADS + h][:, :c], v_new[h].astype(BF16),
                                           preferred_element_type=F32))
        for pr in range(DN_HEADS // 2):
            o_ref[j * c:(j + 1) * c, pr * LANES:(pr + 1) * LANES] = jnp.where(low, o_h[2 * pr], o_h[2 * pr + 1])
        nxt = []
        for h in heads:
            u = j * (DN_HEADS // 2) + h // 2
            kdt = kdec_buf[u][(h % 2) * dk:(h % 2 + 1) * dk, :]
            d = dl_buf[u][0:1, :]
            nxt.append(_delta_home(low, h, s[h] * d + jnp.dot(kdt, v_new[h].astype(BF16),
                                                               preferred_element_type=F32)))
        s = nxt
        yield
    for h in heads:
        s_ref[h] = s[h]


def _delta_chunk_kernel(q_ref, k_ref, v_ref, gate_ref, tril_ref, o_ref, sfin_ref, s_ref, *bufs, nsub, nc):
    i = pl.program_id(0)
    half = len(bufs) // 2
    sets = (bufs[:half], bufs[half:])
    local = lax.rem(jnp.maximum(i - 1, 0), nc)

    @pl.when(i == 0)
    def _():
        for b in sets[1]:
            b[...] = jnp.zeros_like(b)

    @pl.when(local == 0)
    def _():
        s_ref[...] = jnp.zeros_like(s_ref)

    for par in range(2):
        @pl.when(lax.rem(i, 2) == par)
        def _(par=par):
            parts = [_delta_prepare(q_ref, k_ref, v_ref, gate_ref, tril_ref, sets[par], nsub=nsub),
                     _delta_apply(sets[1 - par], o_ref, s_ref, nsub=nsub)]
            while parts:
                parts = [g for g in parts if next(g, StopIteration) is not StopIteration]

    @pl.when(jnp.logical_and(i >= 1, local == nc - 1))
    def _():
        dk = DN_HEAD_DIM
        for h in range(DN_HEADS):
            sfin_ref[0, h] = s_ref[h][:, (h % 2) * dk:(h % 2 + 1) * dk]


def _split3_dot_left(b01, a):
    a1 = a.astype(BF16)
    r1 = a - a1.astype(F32)
    a2 = r1.astype(BF16)
    a3 = (r1 - a2.astype(F32)).astype(BF16)
    out = jnp.dot(b01, a3, preferred_element_type=F32)
    out = out + jnp.dot(b01, a2, preferred_element_type=F32)
    return out + jnp.dot(b01, a1, preferred_element_type=F32)


def _delta_prompt(q, k, v, gate, nb):
    n = q.shape[0]
    t = n // nb
    c = DN_CHUNK
    nsub = DELTA_SUBCHUNKS
    rows = nsub * c
    nc = t // rows
    nblk = nb * nc
    row_in = lambda i: (jnp.minimum(i, nblk - 1), 0)
    row_out = lambda i: (jnp.maximum(i - 1, 0), 0)
    tril = jnp.tril(jnp.ones((c, c), F32)).astype(BF16)
    nh = nsub * DN_HEADS
    npair = nsub * DN_HEADS // 2
    buf_set = [pltpu.VMEM((nh, c, 2 * DN_HEAD_DIM), F32),
               pltpu.VMEM((nh, 2 * c, 2 * DN_HEAD_DIM), BF16),
               pltpu.VMEM((nh, c, 2 * c), BF16),
               pltpu.VMEM((npair, 2 * DN_HEAD_DIM, c), BF16),
               pltpu.VMEM((npair, SUBLANES, LANES), F32)]
    return pl.pallas_call(
        functools.partial(_delta_chunk_kernel, nsub=nsub, nc=nc),
        grid=(nblk + 1,),
        in_specs=[pl.BlockSpec((rows, DN_WIDTH), row_in),
                  pl.BlockSpec((rows, DN_WIDTH), row_in),
                  pl.BlockSpec((rows, DN_WIDTH), row_in),
                  pl.BlockSpec((rows, LANES), row_in),
                  pl.BlockSpec((c, c), lambda i: (0, 0))],
        out_specs=[pl.BlockSpec((rows, DN_WIDTH), row_out),
                   pl.BlockSpec((1, DN_HEADS, DN_HEAD_DIM, DN_HEAD_DIM),
                                lambda i: (jnp.maximum(i - 1, 0) // nc, 0, 0, 0))],
        out_shape=[jax.ShapeDtypeStruct((n, DN_WIDTH), F32),
                   jax.ShapeDtypeStruct((nb, DN_HEADS, DN_HEAD_DIM, DN_HEAD_DIM), F32)],
        scratch_shapes=[pltpu.VMEM((DN_HEADS, DN_HEAD_DIM, 2 * DN_HEAD_DIM), F32)] + buf_set + buf_set,
        compiler_params=_cparams(("arbitrary",)),
        name="delta_prompt",
    )(q, k, v, gate, tril)


def _delta_step_kernel(q_ref, k_ref, v_ref, gate_ref, s0_ref, o_ref, s_ref, kt_ref, qt_ref, gt_ref, *, nt, nb):
    dk = DN_HEAD_DIM
    p = pl.program_id(0)
    for t in range(nt):
        rs = slice(t * nb, (t + 1) * nb)
        gt_ref[...] = gate_ref[rs, :].T
        kt_ref[...] = k_ref[rs, :].T
        qt_ref[...] = (q_ref[rs, :] * (dk ** -0.5)).T
        vt = v_ref[rs, :].T
        src = s0_ref if t == 0 else s_ref
        o_heads = []
        for j in range(2):
            a = jnp.exp(gt_ref[pl.ds(2 * p + j, 1), :])
            beta = gt_ref[pl.ds(2 * p + j + DN_HEADS, 1), :]
            base = j * dk * dk

            def k_dot_s(d, acc, j=j, base=base, src=src):
                sd = src[pl.ds(pl.multiple_of(base + d * dk, dk), dk), :]
                return acc + kt_ref[pl.ds(j * dk + d, 1), :] * sd

            ks = lax.fori_loop(0, dk, k_dot_s, jnp.zeros((dk, nb), F32), unroll=4)
            delta = beta * (vt[j * dk:(j + 1) * dk, :] - a * ks)

            def update(d, acc, j=j, base=base, src=src, a=a, delta=delta):
                r0 = pl.multiple_of(base + d * dk, dk)
                sn = a * src[pl.ds(r0, dk), :] + kt_ref[pl.ds(j * dk + d, 1), :] * delta
                s_ref[pl.ds(r0, dk), :] = sn
                return acc + qt_ref[pl.ds(j * dk + d, 1), :] * sn

            o_heads.append(lax.fori_loop(0, dk, update, jnp.zeros((dk, nb), F32), unroll=4))
        o_ref[rs, :] = jnp.concatenate(o_heads, axis=0).T


def _delta_sample(q, k, v, gate, s0t, nb, nt):
    dk = DN_HEAD_DIM
    flat = dk * dk
    n = nt * nb
    kern = functools.partial(_delta_step_kernel, nt=nt, nb=nb)
    pair = lambda p: (0, p)
    return pl.pallas_call(
        kern,
        grid=(DN_HEADS // 2,),
        in_specs=[pl.BlockSpec((n, LANES), pair),
                  pl.BlockSpec((n, LANES), pair),
                  pl.BlockSpec((n, LANES), pair),
                  pl.BlockSpec((n, LANES), lambda p: (0, 0)),
                  pl.BlockSpec((2 * flat, nb), lambda p: (p, 0))],
        out_specs=[pl.BlockSpec((n, LANES), pair),
                   pl.BlockSpec((2 * flat, nb), lambda p: (p, 0))],
        out_shape=[jax.ShapeDtypeStruct((n, DN_WIDTH), F32),
                   jax.ShapeDtypeStruct((DN_HEADS * flat, nb), F32)],
        scratch_shapes=[pltpu.VMEM((LANES, nb), F32),
                        pltpu.VMEM((LANES, nb), F32),
                        pltpu.VMEM((LANES, nb), F32)],
        compiler_params=_cparams(("arbitrary",)),
        name="delta_sample",
    )(q, k, v, gate, s0t)


def _s5_kernel(u_ref, btre_ref, btim_ref, lam_ref, ctre_ref, ctim_ref, d_ref, h0_ref, y_ref, hfin_ref,
               bw_ref, c_ref, ab_ref, x_ref, h_ref, ru_ref, ry_ref, *, nb, tt, wide):
    p2 = S5_FLAT

    @pl.when(pl.program_id(0) == 0)
    def _():
        lr = lam_ref[0:1, :]
        li = lam_ref[1:2, :]
        dt = jnp.exp(lam_ref[2:3, :])
        mag = jnp.exp(lr * dt)
        ab_re = mag * jnp.cos(li * dt)
        ab_im = mag * jnp.sin(li * dt)
        den = lr * lr + li * li
        nr = ab_re - 1.0
        ni = ab_im
        f_re = (nr * lr + ni * li) / den
        f_im = (ni * lr - nr * li) / den
        ab_ref[0:1, :] = ab_re
        ab_ref[1:2, :] = ab_im
        gpl = LANES // S5_STATE
        ch_g = lax.broadcasted_iota(I32, (S5_WIDTH, LANES), 0) // S5_GROUP_CH
        lane_g = lax.broadcasted_iota(I32, (S5_WIDTH, LANES), 1) // S5_STATE
        bre2 = jnp.concatenate([btre_ref[...]] * gpl, axis=1)
        bim2 = jnp.concatenate([btim_ref[...]] * gpl, axis=1)
        for j in range(p2 // LANES):
            cols = slice(j * LANES, (j + 1) * LANES)
            own = ch_g == gpl * j + lane_g
            bre = jnp.where(own, bre2, 0.0)
            bim = jnp.where(own, bim2, 0.0)
            bw_ref[:, cols] = (bre * f_re[:, cols] - bim * f_im[:, cols]).astype(BF16)
            bw_ref[:, p2 + j * LANES:p2 + (j + 1) * LANES] = (bim * f_re[:, cols] + bre * f_im[:, cols]).astype(BF16)
        cpl = LANES // S5_GROUP_CH
        st_g = lax.broadcasted_iota(I32, (p2, LANES), 0) // S5_STATE
        lane_cg = lax.broadcasted_iota(I32, (p2, LANES), 1) // S5_GROUP_CH
        for j in range(S5_WIDTH // LANES):
            cols = slice(j * LANES, (j + 1) * LANES)
            own = st_g == cpl * j + lane_cg
            c_ref[0:p2, cols] = jnp.where(own, ctre_ref[...], 0.0).astype(BF16)
            c_ref[p2:2 * p2, cols] = jnp.where(own, -ctim_ref[...], 0.0).astype(BF16)
        h_ref[...] = h0_ref[...]

    nck = S5_WIDTH // LANES
    if wide:
        for b in range(nb):
            for ck in range(nck):
                lo = b * S5_WIDTH + ck * LANES
                ru_ref[ck, pl.ds(b, tt, stride=nb), :] = u_ref[:, lo:lo + LANES].astype(F32)
        u = jnp.concatenate([ru_ref[ck] for ck in range(nck)], axis=1)
    else:
        u = u_ref[...].astype(F32)
    ub = u.astype(BF16)
    cw = S5_WIDTH // S5_SUPER
    sw = S5_FLAT // S5_SUPER
    for part in (0, p2):
        for b in range(S5_SUPER):
            x_ref[:, part + b * sw:part + (b + 1) * sw] = jnp.dot(
                ub[:, b * cw:(b + 1) * cw], bw_ref[b * cw:(b + 1) * cw, part + b * sw:part + (b + 1) * sw],
                preferred_element_type=F32)
    a_re = ab_ref[0:1, :]
    a_im = ab_ref[1:2, :]

    if nb == SUBLANES:
        wsl = p2 // S5_SCAN_SPLIT
        for sp in range(S5_SCAN_SPLIT):
            c0 = sp * wsl
            are = jnp.broadcast_to(a_re[:, c0:c0 + wsl], (nb, wsl))
            aim = jnp.broadcast_to(a_im[:, c0:c0 + wsl], (nb, wsl))

            def step(t, carry, c0=c0, are=are, aim=aim):
                hr, hi = carry
                r0 = pl.multiple_of(t * nb, nb)
                nr = are * hr - aim * hi + x_ref[pl.ds(r0, nb), c0:c0 + wsl]
                ni = are * hi + aim * hr + x_ref[pl.ds(r0, nb), p2 + c0:p2 + c0 + wsl]
                x_ref[pl.ds(r0, nb), c0:c0 + wsl] = nr
                x_ref[pl.ds(r0, nb), p2 + c0:p2 + c0 + wsl] = ni
                return nr, ni

            hr, hi = lax.fori_loop(0, tt, step, (h_ref[:, c0:c0 + wsl], h_ref[:, p2 + c0:p2 + c0 + wsl]),
                                   unroll=4)
            h_ref[:, c0:c0 + wsl] = hr
            h_ref[:, p2 + c0:p2 + c0 + wsl] = hi
    else:
        for t in range(tt):
            rs = slice(t * nb, (t + 1) * nb)
            hr = h_ref[:, 0:p2]
            hi = h_ref[:, p2:2 * p2]
            nr = a_re * hr - a_im * hi + x_ref[rs, 0:p2]
            ni = a_re * hi + a_im * hr + x_ref[rs, p2:2 * p2]
            h_ref[:, 0:p2] = nr
            h_ref[:, p2:2 * p2] = ni
            x_ref[rs, 0:p2] = nr
            x_ref[rs, p2:2 * p2] = ni

    for b in range(S5_SUPER):
        cols = slice(b * cw, (b + 1) * cw)
        y = None
        for part in (0, p2):
            rws = slice(part + b * sw, part + (b + 1) * sw)
            term = jnp.dot(x_ref[:, rws].astype(BF16), c_ref[rws, cols], preferred_element_type=F32)
            y = term if y is None else y + term
        if wide:
            skip = jnp.concatenate([ru_ref[b * (cw // LANES) + ck] for ck in range(cw // LANES)], axis=1)
        else:
            skip = u[:, cols]
        y = y + d_ref[:, cols] * skip
        if wide:
            for ck in range(cw // LANES):
                ry_ref[b * (cw // LANES) + ck] = y[:, ck * LANES:(ck + 1) * LANES]
        else:
            y_ref[:, cols] = y
    if wide:
        for b in range(nb):
            for ck in range(nck):
                lo = b * S5_WIDTH + ck * LANES
                y_ref[:, lo:lo + LANES] = ry_ref[ck, pl.ds(b, tt, stride=nb), :]
    hfin_ref[...] = h_ref[...]


def _s5(u, params, h0, nb, wide):
    btre, btim, lam, ctre, ctim, dvec = params
    t = u.shape[0] if wide else u.shape[0] // nb
    tt = min(S5_TILE_ROWS // nb, t)
    rows = tt * nb
    const = lambda i: (0, 0)
    kern = functools.partial(_s5_kernel, nb=nb, tt=tt, wide=wide)
    io_block = (tt, nb * S5_WIDTH) if wide else (rows, S5_WIDTH)
    return pl.pallas_call(
        kern,
        grid=(t // tt,),
        in_specs=[pl.BlockSpec(io_block, lambda i: (i, 0)),
                  pl.BlockSpec((S5_WIDTH, S5_STATE), const),
                  pl.BlockSpec((S5_WIDTH, S5_STATE), const),
                  pl.BlockSpec((SUBLANES, S5_FLAT), const),
                  pl.BlockSpec((S5_FLAT, LANES), const),
                  pl.BlockSpec((S5_FLAT, LANES), const),
                  pl.BlockSpec((1, S5_WIDTH), const),
                  pl.BlockSpec((nb, 2 * S5_FLAT), const)],
        out_specs=[pl.BlockSpec(io_block, lambda i: (i, 0)),
                   pl.BlockSpec((nb, 2 * S5_FLAT), const)],
        out_shape=[jax.ShapeDtypeStruct(u.shape, F32),
                   jax.ShapeDtypeStruct((nb, 2 * S5_FLAT), F32)],
        scratch_shapes=[pltpu.VMEM((S5_WIDTH, 2 * S5_FLAT), BF16),
                        pltpu.VMEM((2 * S5_FLAT, S5_WIDTH), BF16),
                        pltpu.VMEM((SUBLANES, S5_FLAT), F32),
                        pltpu.VMEM((rows, 2 * S5_FLAT), F32),
                        pltpu.VMEM((nb, 2 * S5_FLAT), F32),
                        pltpu.VMEM((S5_WIDTH // LANES, rows, LANES), F32),
                        pltpu.VMEM((S5_WIDTH // LANES, rows, LANES), F32)],
        compiler_params=_cparams(("arbitrary",)),
        name="s5",
    )(u, btre, btim, lam, ctre, ctim, dvec, h0)


def _postmix_kernel(xp_ref, op_ref, zp_ref, ysp_ref, gap_ref, gbp_ref,
                    xs_ref, os_ref, zs_ref, yss_ref, gas_ref, gbs_ref, *rest, nblk_p, range_tok):
    carry_ref = rest[-1]

    @pl.when(pl.program_id(0) == 0)
    def _():
        carry_ref[...] = jnp.zeros_like(carry_ref)

    @pl.when(pl.program_id(0) < nblk_p)
    def _():
        _postmix_body(xp_ref, op_ref, zp_ref, ysp_ref, gap_ref, gbp_ref, *rest, range_tok=range_tok)

    @pl.when(pl.program_id(0) >= nblk_p)
    def _():
        _postmix_body(xs_ref, os_ref, zs_ref, yss_ref, gas_ref, gbs_ref, *rest, range_tok=range_tok)


def _postmix_body(x_ref, o_ref, z_ref, ys_ref, ga_ref, gb_ref, hw_ref, seg_ref, wa_ref, wglu_ref, wb_ref,
                  wo_ref, nf_ref, wr_ref, su_ref, x1_ref, hn_ref, bkt_ref, rank_ref, rw_ref, cnt_ref, carry_ref,
                  *, range_tok):
    rows = x_ref.shape[0]
    pr = rows // POSTMIX_PARTS
    parts = [_postmix_part(p, pr, x_ref, o_ref, z_ref, ys_ref, ga_ref, gb_ref, hw_ref, seg_ref, wa_ref, wglu_ref,
                           wb_ref, wo_ref, nf_ref, wr_ref, su_ref, x1_ref, hn_ref, bkt_ref, rank_ref, rw_ref,
                           carry_ref, range_tok) for p in range(POSTMIX_PARTS)]
    live = []
    while live or parts:
        if parts:
            live.append(parts.pop(0))
        live = [g for g in live if next(g, StopIteration) is not StopIteration]
    cnt_ref[...] = carry_ref[...]


def _postmix_part(part, pr, x_ref, o_ref, z_ref, ys_ref, ga_ref, gb_ref, hw_ref, seg_ref, wa_ref, wglu_ref, wb_ref,
                  wo_ref, nf_ref, wr_ref, su_ref, x1_ref, hn_ref, bkt_ref, rank_ref, rw_ref, carry_ref, range_tok):
    rs = slice(part * pr, (part + 1) * pr)
    o = o_ref[rs, :]
    ms = jnp.dot((o * o).astype(BF16), seg_ref[...], preferred_element_type=F32) * (1.0 / DN_HEAD_DIM)
    on = o * lax.rsqrt(ms + RMS_EPS) * hw_ref[...]
    z = z_ref[rs, :]
    oa = on * (z * _sigmoid(z)).astype(F32)
    yield
    y_a = _mm(oa, wa_ref[...])
    ys = jax.nn.gelu(ys_ref[rs, :])
    yield
    ys = ys * _sigmoid(_mm(ys, wglu_ref[...]))
    yield
    y_b = _mm(ys, wb_ref[...])
    mixed = _sigmoid(ga_ref[rs, :]).astype(F32) * y_a + _sigmoid(gb_ref[rs, :]).astype(F32) * y_b
    yield
    x1 = x_ref[rs, :] + _mm(mixed, wo_ref[...])
    x1_ref[rs, :] = x1
    hn = x1 * lax.rsqrt(jnp.mean(x1 * x1, axis=-1, keepdims=True) + RMS_EPS) * nf_ref[...]
    _slab_store(hn_ref, hn, part * pr)
    yield

    wr = wr_ref[...]
    w_hi = wr.astype(BF16)
    w_lo = (wr - w_hi.astype(F32)).astype(BF16)
    hn_hi = hn.astype(BF16)
    hn_lo = (hn - hn_hi.astype(F32)).astype(BF16)
    both = _mm_nt(jnp.concatenate([w_hi, w_lo], axis=0), hn_hi)
    logits = both[:ROUTER_ROWS] + both[ROUTER_ROWS:] + _mm_nt(w_hi, hn_lo)
    yield
    coarse = logits[N_EXPERTS:N_EXPERTS + MOE_GROUPS, :]
    cm = jnp.max(coarse, axis=0, keepdims=True)
    ce = jnp.exp(coarse - cm)
    pc = ce / jnp.sum(ce, axis=0, keepdims=True)
    p_sel = jnp.max(pc, axis=0, keepdims=True)
    gi = lax.broadcasted_iota(I32, pc.shape, 0)
    g_sel = jnp.min(jnp.where(pc == p_sel, gi, MOE_GROUPS), axis=0, keepdims=True)
    fine = jnp.zeros((EXPERTS_PER_GROUP, logits.shape[1]), F32)
    for g in range(MOE_GROUPS):
        fine = fine + jnp.where(g_sel == g, logits[g * EXPERTS_PER_GROUP:(g + 1) * EXPERTS_PER_GROUP, :], 0.0)
    fm = jnp.max(fine, axis=0, keepdims=True)
    fe = jnp.exp(fine - fm)
    pf = fe / jnp.sum(fe, axis=0, keepdims=True)
    ei = lax.broadcasted_iota(I32, pf.shape, 0)
    v1 = jnp.max(pf, axis=0, keepdims=True)
    i1 = jnp.min(jnp.where(pf == v1, ei, EXPERTS_PER_GROUP), axis=0, keepdims=True)
    rest = jnp.where(ei == i1, -1.0, pf)
    v2 = jnp.max(rest, axis=0, keepdims=True)
    i2 = jnp.min(jnp.where(rest == v2, ei, EXPERTS_PER_GROUP), axis=0, keepdims=True)
    tot = v1 + v2
    rw_ref[0:1, rs] = v1 / tot * p_sel
    rw_ref[1:2, rs] = v2 / tot * p_sel

    tok = pl.program_id(0) * (pr * POSTMIX_PARTS) + part * pr + lax.broadcasted_iota(I32, (1, pr), 1)
    ph = jnp.zeros((1, pr), I32)
    for r in range(1, MOE_PHASES):
        ph = ph + (tok >= r * range_tok).astype(I32)
    bsel = [ph * N_EXPERTS + g_sel * EXPERTS_PER_GROUP + ix for ix in (i1, i2)]
    bi = lax.broadcasted_iota(I32, (MOE_PHASES * N_EXPERTS, pr), 0)
    onehot = [(bi == b).astype(F32) for b in bsel]
    cnt = onehot[0] + onehot[1]
    before = carry_ref[:, 0:1] + jnp.dot(cnt.astype(BF16), su_ref[0:pr, 0:pr], preferred_element_type=F32)
    for s in range(TOP_K):
        bkt_ref[s:s + 1, rs] = bsel[s]
        rank_ref[s:s + 1, rs] = jnp.sum(onehot[s] * before, axis=0, keepdims=True).astype(I32)
    carry_ref[...] = carry_ref[...] + jnp.sum(cnt, axis=1, keepdims=True)


def _postmix(prompt, sample, weights, nb):
    n_p = prompt[0].shape[0]
    n_s = sample[0].shape[0]
    t = n_p // nb
    tt = min(ROW_TILE, t, n_s)
    nt = t // tt
    nblk_p = n_p // tt
    nblk = nblk_p + n_s // tt
    n_total = n_p + n_s
    prow = lambda i: (jnp.minimum(i, nblk_p - 1), 0)
    pys = lambda i: (jnp.minimum(i, nblk_p - 1) % nt, jnp.minimum(i, nblk_p - 1) // nt)
    srow = lambda i: (jnp.maximum(i - nblk_p, 0), 0)
    const = lambda i: (0, 0)

    def stream_specs(row, ysmap):
        return [pl.BlockSpec((tt, D_MODEL), row),
                pl.BlockSpec((tt, DN_WIDTH), row),
                pl.BlockSpec((tt, DN_WIDTH), row),
                pl.BlockSpec((tt, S5_WIDTH), ysmap),
                pl.BlockSpec((tt, D_MODEL), row),
                pl.BlockSpec((tt, D_MODEL), row)]

    weight_specs = [pl.BlockSpec((1, DN_WIDTH), const),
                    pl.BlockSpec((DN_WIDTH, DN_WIDTH), const),
                    pl.BlockSpec((DN_WIDTH, D_MODEL), const),
                    pl.BlockSpec((S5_WIDTH, S5_WIDTH), const),
                    pl.BlockSpec((S5_WIDTH, D_MODEL), const),
                    pl.BlockSpec((D_MODEL, D_MODEL), const),
                    pl.BlockSpec((1, D_MODEL), const),
                    pl.BlockSpec((ROUTER_ROWS, D_MODEL), const),
                    pl.BlockSpec((tt, tt), const)]
    xp, op, zp, ysp, gap, gbp = prompt
    nbk = MOE_PHASES * N_EXPERTS
    earlier = jnp.triu(jnp.ones((tt, tt), F32), k=1).astype(BF16)
    return pl.pallas_call(
        functools.partial(_postmix_kernel, nblk_p=nblk_p, range_tok=n_total // MOE_PHASES),
        grid=(nblk,),
        in_specs=stream_specs(prow, pys) + stream_specs(srow, srow) + weight_specs,
        out_specs=[pl.BlockSpec((tt, D_MODEL), lambda i: (i, 0)),
                   pl.BlockSpec((tt * ROW_SLAB, LANES), lambda i: (i, 0)),
                   pl.BlockSpec((TOP_K, tt), lambda i: (0, i)),
                   pl.BlockSpec((TOP_K, tt), lambda i: (0, i)),
                   pl.BlockSpec((TOP_K, tt), lambda i: (0, i)),
                   pl.BlockSpec((nbk, LANES), const)],
        out_shape=[jax.ShapeDtypeStruct((n_total, D_MODEL), F32),
                   jax.ShapeDtypeStruct((n_total * ROW_SLAB, LANES), F32),
                   jax.ShapeDtypeStruct((TOP_K, n_total), I32),
                   jax.ShapeDtypeStruct((TOP_K, n_total), I32),
                   jax.ShapeDtypeStruct((TOP_K, n_total), F32),
                   jax.ShapeDtypeStruct((nbk, LANES), F32)],
        scratch_shapes=[pltpu.VMEM((nbk, LANES), F32)],
        compiler_params=_cparams(("arbitrary",)),
        name="postmix",
    )(xp, op, zp, ysp, gap, gbp, *sample, *weights, earlier)


def _wait_slabs(buf, sem):
    pltpu.make_async_copy(buf, buf, sem).wait()


def _moe_kernel(texp_ref, tph_ref, tsrc_ref, tnv_ref, tfirst_ref, tslot_ref, tnext_ref, otok_ref,
                hn_hbm, wu_hbm, wd_hbm, y_ref, hnv, xbuf0, xbuf1, wu_buf, wd_buf, wub, wdb, sem, wsem):
    i = pl.program_id(0)
    tm = MOE_TILE
    rs = ROW_SLAB
    nv = tnv_ref[i]
    ph = tph_ref[i]
    range_rows = hnv.shape[0]

    def weight_copies(e, sl):
        return (pltpu.make_async_copy(wu_hbm.at[e], wu_buf.at[sl], wsem.at[sl]),
                pltpu.make_async_copy(wd_hbm.at[e], wd_buf.at[sl], wsem.at[sl]))

    @pl.when(i == 0)
    def _():
        for p, c in enumerate(weight_copies(texp_ref[0], 0)):
            c.start(priority=p % DMA_QUEUES)

    @pl.when(jnp.logical_and(nv > 0, jnp.logical_or(i == 0, ph != tph_ref[jnp.maximum(i - 1, 0)])))
    def _():
        piece = range_rows // DMA_QUEUES
        loads = [pltpu.make_async_copy(hn_hbm.at[pl.ds(pl.multiple_of(ph * range_rows + p * piece, rs), piece), :],
                                       hnv.at[pl.ds(p * piece, piece), :], sem) for p in range(DMA_QUEUES)]
        for p, c in enumerate(loads):
            c.start(priority=p)
        for c in loads:
            c.wait()

    xbuf = (xbuf0, xbuf1)
    parity = lax.rem(i, 2)

    def gather(step, buf):
        src0 = tsrc_ref[step]
        for r in range(tm):
            tok8 = pl.multiple_of(otok_ref[src0 + r], rs)
            buf[pl.ds(r * rs, rs), :] = hnv[pl.ds(tok8, rs), :]

    for par in range(2):
        @pl.when(jnp.logical_and(parity == par, jnp.logical_and(
            nv > 0, jnp.logical_or(i == 0, ph != tph_ref[jnp.maximum(i - 1, 0)]))))
        def _(par=par):
            gather(i, xbuf[par])

    for sl in range(2):
        @pl.when(jnp.logical_and(jnp.logical_and(nv > 0, tfirst_ref[i] == 1), tslot_ref[i] == sl))
        def _():
            for c in weight_copies(texp_ref[i], sl):
                c.wait()

            @pl.when(tnext_ref[i] >= 0)
            def _():
                for p, c in enumerate(weight_copies(tnext_ref[i], 1 - sl)):
                    c.start(priority=p % DMA_QUEUES)

            wub[...] = wu_buf[sl].astype(BF16)
            wdb[...] = wd_buf[sl].astype(BF16)

    @pl.when(nv == 0)
    def _():
        y_ref[...] = jnp.zeros_like(y_ref)

    for par in range(2):
        @pl.when(jnp.logical_and(parity == par, nv > 0))
        def _(par=par):
            gather(jnp.minimum(i + 1, pl.num_programs(0) - 1), xbuf[1 - par])
            x = _slab_load(xbuf[par], tm).astype(BF16)
            hu = jnp.dot(x, wub[...], preferred_element_type=F32)
            gate = hu[:, :EXPERT_FF]
            up = hu[:, EXPERT_FF:]
            act = gate * _sigmoid(gate) * up
            _slab_store(y_ref, jnp.dot(act.astype(BF16), wdb[...], preferred_element_type=F32))


def _moe(hn, w_up, w_down, plan):
    ntiles = plan[0].shape[0]
    grid_spec = pltpu.PrefetchScalarGridSpec(
        num_scalar_prefetch=len(plan),
        grid=(ntiles,),
        in_specs=[pl.BlockSpec(memory_space=pl.ANY),
                  pl.BlockSpec(memory_space=pl.ANY),
                  pl.BlockSpec(memory_space=pl.ANY)],
        out_specs=pl.BlockSpec((MOE_TILE * ROW_SLAB, LANES), lambda i, *_: (i, 0)),
        scratch_shapes=[pltpu.VMEM((hn.shape[0] // MOE_PHASES, LANES), F32),
                        pltpu.VMEM((MOE_TILE * ROW_SLAB, LANES), F32),
                        pltpu.VMEM((MOE_TILE * ROW_SLAB, LANES), F32),
                        pltpu.VMEM((2, D_MODEL, 2 * EXPERT_FF), F32),
                        pltpu.VMEM((2, EXPERT_FF, D_MODEL), F32),
                        pltpu.VMEM((D_MODEL, 2 * EXPERT_FF), BF16),
                        pltpu.VMEM((EXPERT_FF, D_MODEL), BF16),
                        pltpu.SemaphoreType.DMA,
                        pltpu.SemaphoreType.DMA((2,))])
    return pl.pallas_call(
        _moe_kernel,
        grid_spec=grid_spec,
        out_shape=jax.ShapeDtypeStruct((ntiles * MOE_TILE * ROW_SLAB, LANES), F32),
        compiler_params=_cparams(("arbitrary",)),
        name="moe",
    )(*plan, hn, w_up, w_down)


def _combine_kernel(pos_ref, x1_ref, ys_hbm, w_ref, nw_ref, outp_ref, outs_ref,
                    ybuf0, ybuf1, sem, *, nblk_p, n_tok):
    i = pl.program_id(0)
    nsteps = pl.num_programs(0)
    tt = x1_ref.shape[0]
    rs = ROW_SLAB
    slot = lax.rem(i, 2)
    ybuf = (ybuf0, ybuf1)

    def start_gather(step, sl):
        base = step * tt
        for r in range(tt * TOP_K):
            j, s = divmod(r, TOP_K)
            p8 = pl.multiple_of(pos_ref[s * n_tok + base + j], rs)
            pltpu.make_async_copy(ys_hbm.at[pl.ds(p8, rs), :], ybuf[sl].at[pl.ds((s * tt + j) * rs, rs), :],
                                  sem.at[sl]).start(priority=r % DMA_QUEUES)

    @pl.when(i == 0)
    def _():
        start_gather(0, 0)

    for sl in range(2):
        @pl.when(slot == sl)
        def _():
            _wait_slabs(ybuf[sl], sem.at[sl])
            start_gather(jnp.minimum(i + 1, nsteps - 1), 1 - sl)
            w = w_ref[...]
            y0 = _slab_load(ybuf[sl], tt, 0)
            y1 = _slab_load(ybuf[sl], tt, tt * rs)
            x = x1_ref[...] + w[:, 0:1] * y0 + w[:, 1:2] * y1
            res = x * lax.rsqrt(jnp.mean(x * x, axis=-1, keepdims=True) + RMS_EPS) * nw_ref[...]

            @pl.when(i < nblk_p)
            def _():
                outp_ref[...] = res

            @pl.when(i >= nblk_p)
            def _():
                outs_ref[...] = res

        @pl.when(jnp.logical_and(slot == sl, i == nsteps - 1))
        def _():
            _wait_slabs(ybuf[1 - sl], sem.at[1 - sl])


def _combine(x1, ysorted, pos8, wtok, nw, n_p):
    n = x1.shape[0]
    tt = math.gcd(math.gcd(n_p, n - n_p), COMBINE_TILE)
    nblk_p = n_p // tt
    grid_spec = pltpu.PrefetchScalarGridSpec(
        num_scalar_prefetch=1,
        grid=(n // tt,),
        in_specs=[pl.BlockSpec((tt, D_MODEL), lambda i, *_: (i, 0)),
                  pl.BlockSpec(memory_space=pl.ANY),
                  pl.BlockSpec((tt, TOP_K), lambda i, *_: (i, 0)),
                  pl.BlockSpec((1, D_MODEL), lambda i, *_: (0, 0))],
        out_specs=[pl.BlockSpec((tt, D_MODEL), lambda i, *_: (jnp.minimum(i, nblk_p - 1), 0)),
                   pl.BlockSpec((tt, D_MODEL), lambda i, *_: (jnp.maximum(i - nblk_p, 0), 0))],
        scratch_shapes=[pltpu.VMEM((tt * TOP_K * ROW_SLAB, LANES), F32),
                        pltpu.VMEM((tt * TOP_K * ROW_SLAB, LANES), F32),
                        pltpu.SemaphoreType.DMA((2,))])
    return pl.pallas_call(
        functools.partial(_combine_kernel, nblk_p=nblk_p, n_tok=n),
        grid_spec=grid_spec,
        out_shape=[jax.ShapeDtypeStruct((n_p, D_MODEL), F32),
                   jax.ShapeDtypeStruct((n - n_p, D_MODEL), F32)],
        compiler_params=_cparams(("arbitrary",)),
        name="combine",
    )(pos8, x1, ysorted, wtok, nw)


def _route_plan(bkt, rank, cnt, n_tok):
    tm = MOE_TILE
    n_assign = n_tok * TOP_K
    nbk = MOE_PHASES * N_EXPERTS
    ntiles = n_assign // tm + nbk
    range_tok = n_tok // MOE_PHASES
    b_flat = bkt.T.reshape(n_assign)
    order = jnp.argsort(b_flat, stable=True).astype(I32)
    counts = cnt[:, 0].astype(I32)
    cstart = jnp.cumsum(counts) - counts
    tiles_b = (counts + tm - 1) // tm
    tend = jnp.cumsum(tiles_b)
    tstart = tend - tiles_b
    tile_id = jnp.arange(ntiles, dtype=I32)
    tbk = jnp.minimum(jnp.sum((tile_id[:, None] >= tend[None, :]).astype(I32), axis=1), nbk - 1)
    onehot = (tbk[:, None] == jnp.arange(nbk, dtype=I32)[None, :]).astype(I32)
    pick = lambda v: jnp.sum(onehot * v[None, :], axis=1)
    done = (tile_id - pick(tstart)) * tm
    tnv = jnp.where(tile_id < tend[-1], jnp.clip(pick(counts) - done, 0, tm), 0)
    tsrc = jnp.where(tnv > 0, pick(cstart) + done, 0)
    texp = tbk % N_EXPERTS
    tph = tbk // N_EXPERTS
    nonempty = counts > 0
    bslot = (jnp.cumsum(nonempty.astype(I32)) - 1) % 2
    bidx = jnp.where(nonempty, jnp.arange(nbk, dtype=I32), nbk)
    nxt = jnp.concatenate([lax.cummin(bidx[::-1])[::-1][1:], jnp.full((1,), nbk, I32)])
    bnext = jnp.where(nxt < nbk, nxt % N_EXPERTS, -1)
    tfirst = jnp.logical_and(tnv > 0, done == 0).astype(I32)
    tslot = pick(bslot)
    tnext = pick(bnext)
    otok8 = jnp.concatenate([((order // TOP_K) % range_tok) * ROW_SLAB, jnp.zeros((tm,), I32)])
    plan = tuple(a.astype(I32) for a in (texp, tph, tsrc, tnv, tfirst, tslot, tnext, otok8))
    first = jnp.sum((bkt[:, :, None] == jnp.arange(nbk, dtype=I32)[None, None, :]).astype(I32)
                    * (tstart * tm)[None, None, :], axis=2)
    pos8 = ((first + rank) * ROW_SLAB).reshape(n_assign)
    return plan, pos8.astype(I32)


def _block_diag(m):
    g, a, b = m.shape
    eye = jnp.eye(g, dtype=m.dtype)
    return (eye[:, None, :, None] * m[:, :, None, :]).reshape(g * a, g * b)


def kernel(x_prompt, x_sample, state_conv, state_delta, state_ssm_re, state_ssm_im, norm_mix_w, w_in, conv_w, a_log, dt_bias, head_norm_w, w_a_up, s5_lambda_re, s5_lambda_im, s5_log_step, s5_b_re, s5_b_im, s5_c_re, s5_c_im, s5_d, w_glu, w_b_up, w_o, norm_ffn_w, w_router_coarse, w_router_fine, w_expert_up, w_expert_down, norm_final_w):
    bp, tp, _ = x_prompt.shape
    bs, ts, _ = x_sample.shape
    n_p = bp * tp
    n_s = bs * ts
    n_tok = n_p + n_s
    l = 0

    w = w_in[l].astype(BF16)
    c_ab = W1_COLS + 2 * DN_HEADS
    w_parts = (w[:, :W1_COLS], w[:, c_ab:],
               jnp.concatenate([w[:, W1_COLS:c_ab], jnp.zeros((D_MODEL, LANES - 2 * DN_HEADS), BF16)], axis=1))
    nw_mix = norm_mix_w[l].reshape(1, D_MODEL)
    pad8 = lambda v: jnp.concatenate([v, jnp.zeros((LANES - DN_HEADS,), F32)]).reshape(1, LANES)
    gate_p = jnp.concatenate([pad8(a_log[l]), pad8(dt_bias[l])], axis=0)
    seg = _block_diag(jnp.ones((DN_HEADS, DN_HEAD_DIM, DN_HEAD_DIM), BF16))
    chan_rows = lambda b: jnp.swapaxes(b, 1, 2).reshape(S5_WIDTH, S5_STATE)
    state_rows = lambda c: jnp.tile(jnp.swapaxes(c, 1, 2).reshape(S5_FLAT, S5_GROUP_CH),
                                    (1, LANES // S5_GROUP_CH))
    lam = jnp.concatenate([s5_lambda_re[l].reshape(1, S5_FLAT), s5_lambda_im[l].reshape(1, S5_FLAT),
                           jnp.repeat(s5_log_step[l], S5_STATE).reshape(1, S5_FLAT),
                           jnp.zeros((SUBLANES - 3, S5_FLAT), F32)], axis=0)
    s5_params = (chan_rows(s5_b_re[l]), chan_rows(s5_b_im[l]), lam,
                 state_rows(s5_c_re[l]), state_rows(s5_c_im[l]), s5_d[l].reshape(1, S5_WIDTH))
    hw = jnp.tile(head_norm_w[l], DN_HEADS).reshape(1, DN_WIDTH)
    wr = jnp.concatenate([w_router_fine[l].T, w_router_coarse[l].T,
                          jnp.zeros((ROUTER_ROWS - N_EXPERTS - MOE_GROUPS, D_MODEL), F32)], axis=0)
    pm_weights = (hw, seg, w_a_up[l].astype(BF16), w_glu[l].astype(BF16), w_b_up[l].astype(BF16),
                  w_o[l].astype(BF16), norm_ffn_w[l].reshape(1, D_MODEL), wr)

    xp2 = x_prompt.reshape(n_p, D_MODEL)
    q_p, k_p, v_p, gates_p, conv_p, z_p, u_p, ga_p, gb_p = _inprep(
        xp2, nw_mix, w_parts, jnp.zeros((bp, SUBLANES, QKV_DIM), F32), conv_w[l], gate_p, seg, bp, 1)
    o_p, delta_p = _delta_prompt(q_p, k_p, v_p, gates_p, bp)
    ys_p, h_p = _s5(u_p, s5_params, jnp.zeros((bp, 2 * S5_FLAT), F32), bp, True)

    xs2 = jnp.swapaxes(x_sample, 0, 1).reshape(n_s, D_MODEL)
    cinit_s = jnp.swapaxes(state_conv[l], 0, 1).reshape(1, (CONV_W - 1) * bs, QKV_DIM)
    q_s, k_s, v_s, gate_s, conv_s, z_s, u_s, ga_s, gb_s = _inprep(
        xs2, nw_mix, w_parts, cinit_s, conv_w[l], gate_p, seg, 1, bs)
    s0t = jnp.transpose(state_delta[l], (1, 2, 3, 0)).reshape(DN_HEADS * DN_HEAD_DIM * DN_HEAD_DIM, bs)
    o_s, delta_st = _delta_sample(q_s, k_s, v_s, gate_s, s0t, bs, ts)
    delta_s = jnp.transpose(delta_st.reshape(DN_HEADS, DN_HEAD_DIM, DN_HEAD_DIM, bs), (3, 0, 1, 2))
    h0_s = jnp.concatenate([state_ssm_re[l].reshape(bs, S5_FLAT), state_ssm_im[l].reshape(bs, S5_FLAT)], axis=1)
    ys_s, h_s = _s5(u_s, s5_params, h0_s, bs, False)
    x1, hn, bkt, rank, rw, cnt = _postmix((xp2, o_p, z_p, ys_p, ga_p, gb_p), (xs2, o_s, z_s, ys_s, ga_s, gb_s),
                                          pm_weights, bp)

    plan, pos8 = _route_plan(bkt, rank, cnt, n_tok)
    ysorted = _moe(hn, w_expert_up[l], w_expert_down[l], plan)
    y_p, y_s = _combine(x1, ysorted, pos8, rw.T, norm_final_w.reshape(1, D_MODEL), n_p)

    y_prompt = y_p.reshape(bp, tp, D_MODEL)
    y_sample = jnp.swapaxes(y_s.reshape(ts, bs, D_MODEL), 0, 1)
    conv_sample = jnp.swapaxes(conv_s.reshape(CONV_W - 1, bs, QKV_DIM), 0, 1)
    return (y_prompt, y_sample,
            conv_p[None], delta_p[None],
            h_p[:, :S5_FLAT].reshape(1, bp, S5_GROUPS, S5_STATE), h_p[:, S5_FLAT:].reshape(1, bp, S5_GROUPS, S5_STATE),
            conv_sample[None], delta_s[None],
            h_s[:, :S5_FLAT].reshape(1, bs, S5_GROUPS, S5_STATE), h_s[:, S5_FLAT:].reshape(1, bs, S5_GROUPS, S5_STATE))
```

```python
import functools
import math

import jax
import jax.numpy as jnp
import numpy as np
from jax import lax
from jax.experimental import pallas as pl
from jax.experimental.pallas import tpu as pltpu

F32 = jnp.float32
BF16 = jnp.bfloat16
I32 = jnp.int32

D_MODEL = 1024
DN_HEADS = 8
DN_HEAD_DIM = 64
DN_WIDTH = DN_HEADS * DN_HEAD_DIM
QKV_DIM = 3 * DN_WIDTH
CONV_W = 4
DN_CHUNK = 64
S5_GROUP_CH = 16
S5_WIDTH = D_MODEL // 2
S5_GROUPS = S5_WIDTH // S5_GROUP_CH
S5_STATE = 64
S5_FLAT = S5_GROUPS * S5_STATE
MOE_GROUPS = 4
EXPERTS_PER_GROUP = 8
N_EXPERTS = MOE_GROUPS * EXPERTS_PER_GROUP
TOP_K = 2
EXPERT_FF = 256
RMS_EPS = 1e-6
L2_EPS = 1e-6

LANES = 128
SUBLANES = 8
VMEM_LIMIT = 56 * 1024 * 1024

W1_COLS = QKV_DIM + DN_WIDTH
W2_COLS = S5_WIDTH + 2 * D_MODEL

ROW_TILE = 512
INPREP_PARTS = 2
POSTMIX_PARTS = 2
MOE_TILE = 256
MOE_PHASES = 2
COMBINE_TILE = 512
DMA_QUEUES = 2
DELTA_SUBCHUNKS = 4
S5_SUPER = 2
S5_TILE_ROWS = 1024
S5_SCAN_SPLIT = 2
ROUTER_ROWS = 40


def _mm(a, b):
    return jnp.dot(a.astype(BF16), b.astype(BF16), preferred_element_type=F32)


def _mm_nt(a, b):
    return lax.dot_general(a.astype(BF16), b.astype(BF16), (((1,), (1,)), ((), ())),
                           preferred_element_type=F32)


def _sigmoid(x):
    return 0.5 * jnp.tanh(0.5 * x) + 0.5


def _cparams(sem):
    return pltpu.CompilerParams(dimension_semantics=sem, vmem_limit_bytes=VMEM_LIMIT)


ROW_SLAB = D_MODEL // LANES


def _slab_load(ref, rows, first=0, pitch=ROW_SLAB):
    return jnp.concatenate([ref[pl.ds(first + j, rows, stride=pitch), :] for j in range(ROW_SLAB)], axis=1)


def _slab_store(ref, x, first=0):
    for j in range(ROW_SLAB):
        ref[pl.ds(first * ROW_SLAB + j, x.shape[0], stride=ROW_SLAB), :] = x[:, j * LANES:(j + 1) * LANES]


def _softplus(x):
    return jnp.maximum(x, 0.0) + jnp.log1p(jnp.exp(-jnp.abs(x)))


def _inprep_kernel(x_ref, nw_ref, w1_ref, w2_ref, wab_ref, cinit_ref, cw_ref, gp_ref, seg_ref,
                   q_ref, k_ref, v_ref, gate_ref, cnew_ref, z_ref, u_ref, ga_ref, gb_ref, xp_ref,
                   *, shift, rc, rows):
    @pl.when(pl.program_id(1) == 0)
    def _():
        xp_ref[0:rc, :] = cinit_ref[0]

    seg = seg_ref[...]
    pr = rows // INPREP_PARTS

    def part_stages(part):
        rs = slice(part * pr, (part + 1) * pr)
        x = x_ref[rs, :]
        h = x * lax.rsqrt(jnp.mean(x * x, axis=-1, keepdims=True) + RMS_EPS) * nw_ref[...]
        hb = h.astype(BF16)

        def proj(w_ref, lo, hi):
            return jnp.dot(hb, w_ref[:, lo:hi], preferred_element_type=F32)

        xp_ref[rc + part * pr:rc + (part + 1) * pr, :] = proj(w1_ref, 0, QKV_DIM)
        ab = proj(wab_ref, 0, LANES)
        yield
        z_ref[rs, :] = proj(w1_ref, QKV_DIM, W1_COLS).astype(z_ref.dtype)
        u_ref[rs, :] = proj(w2_ref, 0, S5_WIDTH).astype(u_ref.dtype)
        acc = None
        for i in range(CONV_W):
            lo = rc + part * pr + (i - (CONV_W - 1)) * shift
            term = xp_ref[lo:lo + pr, :] * cw_ref[i:i + 1, :]
            acc = term if acc is None else acc + term
        y = acc * _sigmoid(acc)
        yield
        ga_ref[rs, :] = proj(w2_ref, S5_WIDTH, S5_WIDTH + D_MODEL).astype(ga_ref.dtype)
        q = y[:, 0:DN_WIDTH]
        k = y[:, DN_WIDTH:2 * DN_WIDTH]
        q_ref[rs, :] = q * lax.rsqrt(jnp.dot((q * q).astype(BF16), seg, preferred_element_type=F32) + L2_EPS)
        k_ref[rs, :] = k * lax.rsqrt(jnp.dot((k * k).astype(BF16), seg, preferred_element_type=F32) + L2_EPS)
        v_ref[rs, :] = y[:, 2 * DN_WIDTH:]
        yield
        gb_ref[rs, :] = proj(w2_ref, S5_WIDTH + D_MODEL, W2_COLS).astype(gb_ref.dtype)
        g = -jnp.exp(gp_ref[0:1, :]) * _softplus(ab + gp_ref[1:2, :])
        beta = _sigmoid(ab)
        lane = lax.broadcasted_iota(I32, ab.shape, 1)
        gate_ref[rs, :] = jnp.where(lane < DN_HEADS, g, beta)

    live = []
    pending = [part_stages(p) for p in range(INPREP_PARTS)]
    while live or pending:
        if pending:
            live.append(pending.pop(0))
        live = [g for g in live if next(g, StopIteration) is not StopIteration]

    keep = (CONV_W - 1) * shift
    cnew_ref[0] = xp_ref[rc + rows - keep:rc + rows, :]
    xp_ref[0:rc, :] = xp_ref[rows:rows + rc, :]


def _inprep(x2d, nw, w_parts, cinit, conv_w, gate_p, seg, nb, shift):
    n = x2d.shape[0]
    r = n // nb
    rows = min(ROW_TILE, r)
    nt = r // rows
    rc = cinit.shape[1]
    keep = (CONV_W - 1) * shift
    row = lambda b, i: (b * nt + i, 0)
    const = lambda b, i: (0, 0)
    kern = functools.partial(_inprep_kernel, shift=shift, rc=rc, rows=rows)
    outs = pl.pallas_call(
        kern,
        grid=(nb, nt),
        in_specs=[pl.BlockSpec((rows, D_MODEL), row),
                  pl.BlockSpec((1, D_MODEL), const),
                  pl.BlockSpec((D_MODEL, W1_COLS), const),
                  pl.BlockSpec((D_MODEL, W2_COLS), const),
                  pl.BlockSpec((D_MODEL, LANES), const),
                  pl.BlockSpec((1, rc, QKV_DIM), lambda b, i: (b, 0, 0)),
                  pl.BlockSpec((CONV_W, QKV_DIM), const),
                  pl.BlockSpec((2, LANES), const),
                  pl.BlockSpec((DN_WIDTH, DN_WIDTH), const)],
        out_specs=[pl.BlockSpec((rows, DN_WIDTH), row),
                   pl.BlockSpec((rows, DN_WIDTH), row),
                   pl.BlockSpec((rows, DN_WIDTH), row),
                   pl.BlockSpec((rows, LANES), row),
                   pl.BlockSpec((1, keep, QKV_DIM), lambda b, i: (b, 0, 0)),
                   pl.BlockSpec((rows, DN_WIDTH), row),
                   pl.BlockSpec((rows, S5_WIDTH), lambda b, i: (i, b)),
                   pl.BlockSpec((rows, D_MODEL), row),
                   pl.BlockSpec((rows, D_MODEL), row)],
        out_shape=[jax.ShapeDtypeStruct((n, DN_WIDTH), F32),
                   jax.ShapeDtypeStruct((n, DN_WIDTH), F32),
                   jax.ShapeDtypeStruct((n, DN_WIDTH), F32),
                   jax.ShapeDtypeStruct((n, LANES), F32),
                   jax.ShapeDtypeStruct((nb, keep, QKV_DIM), F32),
                   jax.ShapeDtypeStruct((n, DN_WIDTH), BF16),
                   jax.ShapeDtypeStruct((r, nb * S5_WIDTH), BF16),
                   jax.ShapeDtypeStruct((n, D_MODEL), BF16),
                   jax.ShapeDtypeStruct((n, D_MODEL), BF16)],
        scratch_shapes=[pltpu.VMEM((rc + rows, QKV_DIM), F32)],
        compiler_params=_cparams(("arbitrary", "arbitrary")),
        name="inprep",
    )(x2d, nw, *w_parts, cinit, conv_w, gate_p, seg)
    return outs


def _delta_home(low, h, x, other=0.0):
    return jnp.where(low, x, other) if h % 2 == 0 else jnp.where(low, other, x)


def _delta_prepare(q_ref, k_ref, v_ref, gate_ref, tril_ref, bufs, *, nsub):
    sol_buf, wq_buf, qk_buf, kdec_buf, dl_buf = bufs
    c = DN_CHUNK
    dk = DN_HEAD_DIM

    def home(h, x, other=0.0):
        return _delta_home(low, h, x, other)

    rowi2 = lax.broadcasted_iota(I32, (c, 2 * c), 0)
    lane2 = lax.broadcasted_iota(I32, (c, 2 * c), 1)
    coli2 = lane2 & (c - 1)
    causal2 = rowi2 >= coli2
    strict2 = rowi2 > coli2
    low = lane2 < dk
    tril = tril_ref[...]
    pairs = [(j, h) for j in range(nsub) for h in range(DN_HEADS)]
    units = [(j, pr) for j in range(nsub) for pr in range(DN_HEADS // 2)]
    rows = [slice(j * c, (j + 1) * c) for j in range(nsub)]
    gate = [gate_ref[rows[j], :] for j in range(nsub)]
    gc_all = [_split3_dot_left(tril, gate[j]) for j in range(nsub)]
    gc_t = [gc_all[j].T for j in range(nsub)]

    def block(ref, j, pr):
        return ref[rows[j], pr * LANES:(pr + 1) * LANES]

    gfull = {(j, h): jnp.broadcast_to(gc_all[j][:, h:h + 1], (c, 2 * c)) for j, h in pairs}
    g2 = {(j, pr): jnp.where(low, gfull[j, 2 * pr], gfull[j, 2 * pr + 1]) for j, pr in units}
    b2 = {(j, pr): jnp.where(low, gate[j][:, DN_HEADS + 2 * pr:DN_HEADS + 2 * pr + 1],
                             gate[j][:, DN_HEADS + 2 * pr + 1:DN_HEADS + 2 * pr + 2]) for j, pr in units}
    kp = {u: block(k_ref, *u) for u in units}
    qp = {u: block(q_ref, *u) * (dk ** -0.5) for u in units}
    egc2 = {u: jnp.exp(g2[u]) for u in units}
    kb2 = {u: kp[u] * b2[u] for u in units}
    vb2 = {u: block(v_ref, *u) * b2[u] for u in units}
    kw2s = {u: pltpu.roll(kb2[u] * egc2[u], dk, axis=1) for u in units}
    qd2 = {u: qp[u] * egc2[u] for u in units}
    glast2 = {u: g2[u][c - 1:c, :] for u in units}
    kdec_t2 = {u: (kp[u] * jnp.exp(glast2[u] - g2[u])).T for u in units}
    dlast2 = {u: jnp.exp(glast2[u]) for u in units}
    kk = {u: jnp.concatenate([kp[u], kp[u]], axis=0) for u in units}
    yield

    grow2 = {(j, h): jnp.concatenate([gc_t[j][h:h + 1, :], gc_t[j][h:h + 1, :]], axis=1) for j, h in pairs}
    decay = {p: jnp.where(causal2, jnp.exp(jnp.where(causal2, gfull[p] - grow2[p], 0.0)), 0.0) for p in pairs}
    gram = {(j, h): _mm_nt(jnp.concatenate([home(h, kb2[j, h // 2]), home(h, qp[j, h // 2])], axis=0), kk[j, h // 2])
            for j, h in pairs}
    mat = {p: jnp.where(strict2, gram[p][:c] * decay[p], 0.0).astype(BF16) for p in pairs}
    qk = {p: jnp.where(causal2, gram[p][c:] * decay[p], 0.0) for p in pairs}
    sol = {(j, h): home(h, vb2[j, h // 2], kw2s[j, h // 2]) for j, h in pairs}
    yield
    levels = int(math.log2(c))
    zeros2 = jnp.zeros((c, 2 * c), BF16)
    for lvl in range(levels):
        hi = {p: sol[p].astype(BF16) for p in pairs}
        lo = {p: (sol[p] - hi[p].astype(F32)).astype(BF16) for p in pairs}
        if lvl < levels - 1:
            y = {p: jnp.dot(mat[p], jnp.concatenate([jnp.concatenate([hi[p], mat[p]], axis=1),
                                                     jnp.concatenate([lo[p], zeros2], axis=1)], axis=0),
                            preferred_element_type=F32) for p in pairs}
            mat = {p: y[p][:, 2 * dk:].astype(BF16) for p in pairs}
            upd = {p: y[p][:, :2 * dk] for p in pairs}
        else:
            upd = {p: jnp.dot(mat[p], jnp.concatenate([hi[p], lo[p]], axis=0), preferred_element_type=F32)
                   for p in pairs}
        sol = {p: (sol[p] - upd[p]) if lvl == 0 else (sol[p] + upd[p]) for p in pairs}
        yield
    for j, h in pairs:
        n = j * DN_HEADS + h
        sol_buf[n] = sol[j, h]
        wq_buf[n] = jnp.concatenate([home(h, 0.0, sol[j, h]), home(h, qd2[j, h // 2])], axis=0).astype(BF16)
        qk_buf[n] = qk[j, h].astype(BF16)
    for j, pr in units:
        n = j * (DN_HEADS // 2) + pr
        kdec_buf[n] = kdec_t2[j, pr].astype(BF16)
        dl_buf[n] = jnp.broadcast_to(dlast2[j, pr], (SUBLANES, LANES))


def _delta_apply(bufs, o_ref, s_ref, *, nsub):
    sol_buf, wq_buf, qk_buf, kdec_buf, dl_buf = bufs
    c = DN_CHUNK
    dk = DN_HEAD_DIM
    heads = range(DN_HEADS)
    low = lax.broadcasted_iota(I32, (c, 2 * c), 1) < dk
    s = [s_ref[h] for h in heads]
    for j in range(nsub):
        ws, v_new, o_h = [], [], []
        for h in heads:
            n = j * DN_HEADS + h
            ws.append(jnp.dot(wq_buf[n], jnp.concatenate([s[h], s[h]], axis=0).astype(BF16),
                              preferred_element_type=F32))
        yield
        for h in heads:
            v_new.append(sol_buf[j * DN_HEADS + h] - ws[h][:c])
        for h in heads:
            o_h.append(ws[h][c:] + jnp.dot(qk_buf[j * DN_HEADS + h][:, :c], v_new[h].astype(BF16),
                                           preferred_element_type=F32))
        for pr in range(DN_HEADS // 2):
            o_ref[j * c:(j + 1) * c, pr * LANES:(pr + 1) * LANES] = jnp.where(low, o_h[2 * pr], o_h[2 * pr + 1])
        nxt = []
        for h in heads:
            u = j * (DN_HEADS // 2) + h // 2
            kdt = kdec_buf[u][(h % 2) * dk:(h % 2 + 1) * dk, :]
            d = dl_buf[u][0:1, :]
            nxt.append(_delta_home(low, h, s[h] * d + jnp.dot(kdt, v_new[h].astype(BF16),
                                                               preferred_element_type=F32)))
        s = nxt
        yield
    for h in heads:
        s_ref[h] = s[h]


def _delta_chunk_kernel(q_ref, k_ref, v_ref, gate_ref, tril_ref, o_ref, sfin_ref, s_ref, *bufs, nsub, nc):
    i = pl.program_id(0)
    half = len(bufs) // 2
    sets = (bufs[:half], bufs[half:])
    local = lax.rem(jnp.maximum(i - 1, 0), nc)

    @pl.when(i == 0)
    def _():
        for b in sets[1]:
            b[...] = jnp.zeros_like(b)

    @pl.when(local == 0)
    def _():
        s_ref[...] = jnp.zeros_like(s_ref)

    for par in range(2):
        @pl.when(lax.rem(i, 2) == par)
        def _(par=par):
            parts = [_delta_prepare(q_ref, k_ref, v_ref, gate_ref, tril_ref, sets[par], nsub=nsub),
                     _delta_apply(sets[1 - par], o_ref, s_ref, nsub=nsub)]
            while parts:
                parts = [g for g in parts if next(g, StopIteration) is not StopIteration]

    @pl.when(jnp.logical_and(i >= 1, local == nc - 1))
    def _():
        dk = DN_HEAD_DIM
        for h in range(DN_HEADS):
            sfin_ref[0, h] = s_ref[h][:, (h % 2) * dk:(h % 2 + 1) * dk]


def _split3_dot_left(b01, a):
    a1 = a.astype(BF16)
    r1 = a - a1.astype(F32)
    a2 = r1.astype(BF16)
    a3 = (r1 - a2.astype(F32)).astype(BF16)
    out = jnp.dot(b01, a3, preferred_element_type=F32)
    out = out + jnp.dot(b01, a2, preferred_element_type=F32)
    return out + jnp.dot(b01, a1, preferred_element_type=F32)


def _delta_prompt(q, k, v, gate, nb):
    n = q.shape[0]
    t = n // nb
    c = DN_CHUNK
    nsub = DELTA_SUBCHUNKS
    rows = nsub * c
    nc = t // rows
    nblk = nb * nc
    row_in = lambda i: (jnp.minimum(i, nblk - 1), 0)
    row_out = lambda i: (jnp.maximum(i - 1, 0), 0)
    tril = jnp.tril(jnp.ones((c, c), F32)).astype(BF16)
    nh = nsub * DN_HEADS
    npair = nsub * DN_HEADS // 2
    buf_set = [pltpu.VMEM((nh, c, 2 * DN_HEAD_DIM), F32),
               pltpu.VMEM((nh, 2 * c, 2 * DN_HEAD_DIM), BF16),
               pltpu.VMEM((nh, c, 2 * c), BF16),
               pltpu.VMEM((npair, 2 * DN_HEAD_DIM, c), BF16),
               pltpu.VMEM((npair, SUBLANES, LANES), F32)]
    return pl.pallas_call(
        functools.partial(_delta_chunk_kernel, nsub=nsub, nc=nc),
        grid=(nblk + 1,),
        in_specs=[pl.BlockSpec((rows, DN_WIDTH), row_in),
                  pl.BlockSpec((rows, DN_WIDTH), row_in),
                  pl.BlockSpec((rows, DN_WIDTH), row_in),
                  pl.BlockSpec((rows, LANES), row_in),
                  pl.BlockSpec((c, c), lambda i: (0, 0))],
        out_specs=[pl.BlockSpec((rows, DN_WIDTH), row_out),
                   pl.BlockSpec((1, DN_HEADS, DN_HEAD_DIM, DN_HEAD_DIM),
                                lambda i: (jnp.maximum(i - 1, 0) // nc, 0, 0, 0))],
        out_shape=[jax.ShapeDtypeStruct((n, DN_WIDTH), F32),
                   jax.ShapeDtypeStruct((nb, DN_HEADS, DN_HEAD_DIM, DN_HEAD_DIM), F32)],
        scratch_shapes=[pltpu.VMEM((DN_HEADS, DN_HEAD_DIM, 2 * DN_HEAD_DIM), F32)] + buf_set + buf_set,
        compiler_params=_cparams(("arbitrary",)),
        name="delta_prompt",
    )(q, k, v, gate, tril)


def _delta_step_kernel(q_ref, k_ref, v_ref, gate_ref, s0_ref, o_ref, s_ref, kt_ref, qt_ref, gt_ref, *, nt, nb):
    dk = DN_HEAD_DIM
    p = pl.program_id(0)
    for t in range(nt):
        rs = slice(t * nb, (t + 1) * nb)
        gt_ref[...] = gate_ref[rs, :].T
        kt_ref[...] = k_ref[rs, :].T
        qt_ref[...] = (q_ref[rs, :] * (dk ** -0.5)).T
        vt = v_ref[rs, :].T
        src = s0_ref if t == 0 else s_ref
        o_heads = []
        for j in range(2):
            a = jnp.exp(gt_ref[pl.ds(2 * p + j, 1), :])
            beta = gt_ref[pl.ds(2 * p + j + DN_HEADS, 1), :]
            base = j * dk * dk

            def k_dot_s(d, acc, j=j, base=base, src=src):
                sd = src[pl.ds(pl.multiple_of(base + d * dk, dk), dk), :]
                return acc + kt_ref[pl.ds(j * dk + d, 1), :] * sd

            ks = lax.fori_loop(0, dk, k_dot_s, jnp.zeros((dk, nb), F32), unroll=4)
            delta = beta * (vt[j * dk:(j + 1) * dk, :] - a * ks)

            def update(d, acc, j=j, base=base, src=src, a=a, delta=delta):
                r0 = pl.multiple_of(base + d * dk, dk)
                sn = a * src[pl.ds(r0, dk), :] + kt_ref[pl.ds(j * dk + d, 1), :] * delta
                s_ref[pl.ds(r0, dk), :] = sn
                return acc + qt_ref[pl.ds(j * dk + d, 1), :] * sn

            o_heads.append(lax.fori_loop(0, dk, update, jnp.zeros((dk, nb), F32), unroll=4))
        o_ref[rs, :] = jnp.concatenate(o_heads, axis=0).T


def _delta_sample(q, k, v, gate, s0t, nb, nt):
    dk = DN_HEAD_DIM
    flat = dk * dk
    n = nt * nb
    kern = functools.partial(_delta_step_kernel, nt=nt, nb=nb)
    pair = lambda p: (0, p)
    return pl.pallas_call(
        kern,
        grid=(DN_HEADS // 2,),
        in_specs=[pl.BlockSpec((n, LANES), pair),
                  pl.BlockSpec((n, LANES), pair),
                  pl.BlockSpec((n, LANES), pair),
                  pl.BlockSpec((n, LANES), lambda p: (0, 0)),
                  pl.BlockSpec((2 * flat, nb), lambda p: (p, 0))],
        out_specs=[pl.BlockSpec((n, LANES), pair),
                   pl.BlockSpec((2 * flat, nb), lambda p: (p, 0))],
        out_shape=[jax.ShapeDtypeStruct((n, DN_WIDTH), F32),
                   jax.ShapeDtypeStruct((DN_HEADS * flat, nb), F32)],
        scratch_shapes=[pltpu.VMEM((LANES, nb), F32),
                        pltpu.VMEM((LANES, nb), F32),
                        pltpu.VMEM((LANES, nb), F32)],
        compiler_params=_cparams(("arbitrary",)),
        name="delta_sample",
    )(q, k, v, gate, s0t)


def _s5_kernel(u_ref, btre_ref, btim_ref, lam_ref, ctre_ref, ctim_ref, d_ref, h0_ref, y_ref, hfin_ref,
               bw_ref, c_ref, ab_ref, x_ref, h_ref, ru_ref, ry_ref, *, nb, tt, wide):
    p2 = S5_FLAT

    @pl.when(pl.program_id(0) == 0)
    def _():
        lr = lam_ref[0:1, :]
        li = lam_ref[1:2, :]
        dt = jnp.exp(lam_ref[2:3, :])
        mag = jnp.exp(lr * dt)
        ab_re = mag * jnp.cos(li * dt)
        ab_im = mag * jnp.sin(li * dt)
        den = lr * lr + li * li
        nr = ab_re - 1.0
        ni = ab_im
        f_re = (nr * lr + ni * li) / den
        f_im = (ni * lr - nr * li) / den
        ab_ref[0:1, :] = ab_re
        ab_ref[1:2, :] = ab_im
        gpl = LANES // S5_STATE
        ch_g = lax.broadcasted_iota(I32, (S5_WIDTH, LANES), 0) // S5_GROUP_CH
        lane_g = lax.broadcasted_iota(I32, (S5_WIDTH, LANES), 1) // S5_STATE
        bre2 = jnp.concatenate([btre_ref[...]] * gpl, axis=1)
        bim2 = jnp.concatenate([btim_ref[...]] * gpl, axis=1)
        for j in range(p2 // LANES):
            cols = slice(j * LANES, (j + 1) * LANES)
            own = ch_g == gpl * j + lane_g
            bre = jnp.where(own, bre2, 0.0)
            bim = jnp.where(own, bim2, 0.0)
            bw_ref[:, cols] = (bre * f_re[:, cols] - bim * f_im[:, cols]).astype(BF16)
            bw_ref[:, p2 + j * LANES:p2 + (j + 1) * LANES] = (bim * f_re[:, cols] + bre * f_im[:, cols]).astype(BF16)
        cpl = LANES // S5_GROUP_CH
        st_g = lax.broadcasted_iota(I32, (p2, LANES), 0) // S5_STATE
        lane_cg = lax.broadcasted_iota(I32, (p2, LANES), 1) // S5_GROUP_CH
        for j in range(S5_WIDTH // LANES):
            cols = slice(j * LANES, (j + 1) * LANES)
            own = st_g == cpl * j + lane_cg
            c_ref[0:p2, cols] = jnp.where(own, ctre_ref[...], 0.0).astype(BF16)
            c_ref[p2:2 * p2, cols] = jnp.where(own, -ctim_ref[...], 0.0).astype(BF16)
        h_ref[...] = h0_ref[...]

    nck = S5_WIDTH // LANES
    if wide:
        for b in range(nb):
            for ck in range(nck):
                lo = b * S5_WIDTH + ck * LANES
                ru_ref[ck, pl.ds(b, tt, stride=nb), :] = u_ref[:, lo:lo + LANES].astype(F32)
        u = jnp.concatenate([ru_ref[ck] for ck in range(nck)], axis=1)
    else:
        u = u_ref[...].astype(F32)
    ub = u.astype(BF16)
    cw = S5_WIDTH // S5_SUPER
    sw = S5_FLAT // S5_SUPER
    for part in (0, p2):
        for b in range(S5_SUPER):
            x_ref[:, part + b * sw:part + (b + 1) * sw] = jnp.dot(
                ub[:, b * cw:(b + 1) * cw], bw_ref[b * cw:(b + 1) * cw, part + b * sw:part + (b + 1) * sw],
                preferred_element_type=F32)
    a_re = ab_ref[0:1, :]
    a_im = ab_ref[1:2, :]

    if nb == SUBLANES:
        wsl = p2 // S5_SCAN_SPLIT
        for sp in range(S5_SCAN_SPLIT):
            c0 = sp * wsl
            are = jnp.broadcast_to(a_re[:, c0:c0 + wsl], (nb, wsl))
            aim = jnp.broadcast_to(a_im[:, c0:c0 + wsl], (nb, wsl))

            def step(t, carry, c0=c0, are=are, aim=aim):
                hr, hi = carry
                r0 = pl.multiple_of(t * nb, nb)
                nr = are * hr - aim * hi + x_ref[pl.ds(r0, nb), c0:c0 + wsl]
                ni = are * hi + aim * hr + x_ref[pl.ds(r0, nb), p2 + c0:p2 + c0 + wsl]
                x_ref[pl.ds(r0, nb), c0:c0 + wsl] = nr
                x_ref[pl.ds(r0, nb), p2 + c0:p2 + c0 + wsl] = ni
                return nr, ni

            hr, hi = lax.fori_loop(0, tt, step, (h_ref[:, c0:c0 + wsl], h_ref[:, p2 + c0:p2 + c0 + wsl]),
                                   unroll=4)
            h_ref[:, c0:c0 + wsl] = hr
            h_ref[:, p2 + c0:p2 + c0 + wsl] = hi
    else:
        for t in range(tt):
            rs = slice(t * nb, (t + 1) * nb)
            hr = h_ref[:, 0:p2]
            hi = h_ref[:, p2:2 * p2]
            nr = a_re * hr - a_im * hi + x_ref[rs, 0:p2]
            ni = a_re * hi + a_im * hr + x_ref[rs, p2:2 * p2]
            h_ref[:, 0:p2] = nr
            h_ref[:, p2:2 * p2] = ni
            x_ref[rs, 0:p2] = nr
            x_ref[rs, p2:2 * p2] = ni

    for b in range(S5_SUPER):
        cols = slice(b * cw, (b + 1) * cw)
        y = None
        for part in (0, p2):
            rws = slice(part + b * sw, part + (b + 1) * sw)
            term = jnp.dot(x_ref[:, rws].astype(BF16), c_ref[rws, cols], preferred_element_type=F32)
            y = term if y is None else y + term
        if wide:
            skip = jnp.concatenate([ru_ref[b * (cw // LANES) + ck] for ck in range(cw // LANES)], axis=1)
        else:
            skip = u[:, cols]
        y = y + d_ref[:, cols] * skip
        if wide:
            for ck in range(cw // LANES):
                ry_ref[b * (cw // LANES) + ck] = y[:, ck * LANES:(ck + 1) * LANES]
        else:
            y_ref[:, cols] = y
    if wide:
        for b in range(nb):
            for ck in range(nck):
                lo = b * S5_WIDTH + ck * LANES
                y_ref[:, lo:lo + LANES] = ry_ref[ck, pl.ds(b, tt, stride=nb), :]
    hfin_ref[...] = h_ref[...]


def _s5(u, params, h0, nb, wide):
    btre, btim, lam, ctre, ctim, dvec = params
    t = u.shape[0] if wide else u.shape[0] // nb
    tt = min(S5_TILE_ROWS // nb, t)
    rows = tt * nb
    const = lambda i: (0, 0)
    kern = functools.partial(_s5_kernel, nb=nb, tt=tt, wide=wide)
    io_block = (tt, nb * S5_WIDTH) if wide else (rows, S5_WIDTH)
    return pl.pallas_call(
        kern,
        grid=(t // tt,),
        in_specs=[pl.BlockSpec(io_block, lambda i: (i, 0)),
                  pl.BlockSpec((S5_WIDTH, S5_STATE), const),
                  pl.BlockSpec((S5_WIDTH, S5_STATE), const),
                  pl.BlockSpec((SUBLANES, S5_FLAT), const),
                  pl.BlockSpec((S5_FLAT, LANES), const),
                  pl.BlockSpec((S5_FLAT, LANES), const),
                  pl.BlockSpec((1, S5_WIDTH), const),
                  pl.BlockSpec((nb, 2 * S5_FLAT), const)],
        out_specs=[pl.BlockSpec(io_block, lambda i: (i, 0)),
                   pl.BlockSpec((nb, 2 * S5_FLAT), const)],
        out_shape=[jax.ShapeDtypeStruct(u.shape, F32),
                   jax.ShapeDtypeStruct((nb, 2 * S5_FLAT), F32)],
        scratch_shapes=[pltpu.VMEM((S5_WIDTH, 2 * S5_FLAT), BF16),
                        pltpu.VMEM((2 * S5_FLAT, S5_WIDTH), BF16),
                        pltpu.VMEM((SUBLANES, S5_FLAT), F32),
                        pltpu.VMEM((rows, 2 * S5_FLAT), F32),
                        pltpu.VMEM((nb, 2 * S5_FLAT), F32),
                        pltpu.VMEM((S5_WIDTH // LANES, rows, LANES), F32),
                        pltpu.VMEM((S5_WIDTH // LANES, rows, LANES), F32)],
        compiler_params=_cparams(("arbitrary",)),
        name="s5",
    )(u, btre, btim, lam, ctre, ctim, dvec, h0)


def _postmix_kernel(xp_ref, op_ref, zp_ref, ysp_ref, gap_ref, gbp_ref,
                    xs_ref, os_ref, zs_ref, yss_ref, gas_ref, gbs_ref, *rest, nblk_p, range_tok):
    carry_ref = rest[-1]

    @pl.when(pl.program_id(0) == 0)
    def _():
        carry_ref[...] = jnp.zeros_like(carry_ref)

    @pl.when(pl.program_id(0) < nblk_p)
    def _():
        _postmix_body(xp_ref, op_ref, zp_ref, ysp_ref, gap_ref, gbp_ref, *rest, range_tok=range_tok)

    @pl.when(pl.program_id(0) >= nblk_p)
    def _():
        _postmix_body(xs_ref, os_ref, zs_ref, yss_ref, gas_ref, gbs_ref, *rest, range_tok=range_tok)


def _postmix_body(x_ref, o_ref, z_ref, ys_ref, ga_ref, gb_ref, hw_ref, seg_ref, wa_ref, wglu_ref, wb_ref,
                  wo_ref, nf_ref, wr_ref, su_ref, x1_ref, hn_ref, bkt_ref, rank_ref, rw_ref, cnt_ref, carry_ref,
                  *, range_tok):
    rows = x_ref.shape[0]
    pr = rows // POSTMIX_PARTS
    parts = [_postmix_part(p, pr, x_ref, o_ref, z_ref, ys_ref, ga_ref, gb_ref, hw_ref, seg_ref, wa_ref, wglu_ref,
                           wb_ref, wo_ref, nf_ref, wr_ref, su_ref, x1_ref, hn_ref, bkt_ref, rank_ref, rw_ref,
                           carry_ref, range_tok) for p in range(POSTMIX_PARTS)]
    live = []
    while live or parts:
        if parts:
            live.append(parts.pop(0))
        live = [g for g in live if next(g, StopIteration) is not StopIteration]
    cnt_ref[...] = carry_ref[...]


def _postmix_part(part, pr, x_ref, o_ref, z_ref, ys_ref, ga_ref, gb_ref, hw_ref, seg_ref, wa_ref, wglu_ref, wb_ref,
                  wo_ref, nf_ref, wr_ref, su_ref, x1_ref, hn_ref, bkt_ref, rank_ref, rw_ref, carry_ref, range_tok):
    rs = slice(part * pr, (part + 1) * pr)
    o = o_ref[rs, :]
    ms = jnp.dot((o * o).astype(BF16), seg_ref[...], preferred_element_type=F32) * (1.0 / DN_HEAD_DIM)
    on = o * lax.rsqrt(ms + RMS_EPS) * hw_ref[...]
    z = z_ref[rs, :]
    oa = on * (z * _sigmoid(z)).astype(F32)
    yield
    y_a = _mm(oa, wa_ref[...])
    ys = jax.nn.gelu(ys_ref[rs, :])
    yield
    ys = ys * _sigmoid(_mm(ys, wglu_ref[...]))
    yield
    y_b = _mm(ys, wb_ref[...])
    mixed = _sigmoid(ga_ref[rs, :]).astype(F32) * y_a + _sigmoid(gb_ref[rs, :]).astype(F32) * y_b
    yield
    x1 = x_ref[rs, :] + _mm(mixed, wo_ref[...])
    x1_ref[rs, :] = x1
    hn = x1 * lax.rsqrt(jnp.mean(x1 * x1, axis=-1, keepdims=True) + RMS_EPS) * nf_ref[...]
    _slab_store(hn_ref, hn, part * pr)
    yield

    wr = wr_ref[...]
    w_hi = wr.astype(BF16)
    w_lo = (wr - w_hi.astype(F32)).astype(BF16)
    hn_hi = hn.astype(BF16)
    hn_lo = (hn - hn_hi.astype(F32)).astype(BF16)
    both = _mm_nt(jnp.concatenate([w_hi, w_lo], axis=0), hn_hi)
    logits = both[:ROUTER_ROWS] + both[ROUTER_ROWS:] + _mm_nt(w_hi, hn_lo)
    yield
    coarse = logits[N_EXPERTS:N_EXPERTS + MOE_GROUPS, :]
    cm = jnp.max(coarse, axis=0, keepdims=True)
    ce = jnp.exp(coarse - cm)
    pc = ce / jnp.sum(ce, axis=0, keepdims=True)
    p_sel = jnp.max(pc, axis=0, keepdims=True)
    gi = lax.broadcasted_iota(I32, pc.shape, 0)
    g_sel = jnp.min(jnp.where(pc == p_sel, gi, MOE_GROUPS), axis=0, keepdims=True)
    fine = jnp.zeros((EXPERTS_PER_GROUP, logits.shape[1]), F32)
    for g in range(MOE_GROUPS):
        fine = fine + jnp.where(g_sel == g, logits[g * EXPERTS_PER_GROUP:(g + 1) * EXPERTS_PER_GROUP, :], 0.0)
    fm = jnp.max(fine, axis=0, keepdims=True)
    fe = jnp.exp(fine - fm)
    pf = fe / jnp.sum(fe, axis=0, keepdims=True)
    ei = lax.broadcasted_iota(I32, pf.shape, 0)
    v1 = jnp.max(pf, axis=0, keepdims=True)
    i1 = jnp.min(jnp.where(pf == v1, ei, EXPERTS_PER_GROUP), axis=0, keepdims=True)
    rest = jnp.where(ei == i1, -1.0, pf)
    v2 = jnp.max(rest, axis=0, keepdims=True)
    i2 = jnp.min(jnp.where(rest == v2, ei, EXPERTS_PER_GROUP), axis=0, keepdims=True)
    tot = v1 + v2
    rw_ref[0:1, rs] = v1 / tot * p_sel
    rw_ref[1:2, rs] = v2 / tot * p_sel

    tok = pl.program_id(0) * (pr * POSTMIX_PARTS) + part * pr + lax.broadcasted_iota(I32, (1, pr), 1)
    ph = jnp.zeros((1, pr), I32)
    for r in range(1, MOE_PHASES):
        ph = ph + (tok >= r * range_tok).astype(I32)
    bsel = [ph * N_EXPERTS + g_sel * EXPERTS_PER_GROUP + ix for ix in (i1, i2)]
    bi = lax.broadcasted_iota(I32, (MOE_PHASES * N_EXPERTS, pr), 0)
    onehot = [(bi == b).astype(F32) for b in bsel]
    cnt = onehot[0] + onehot[1]
    before = carry_ref[:, 0:1] + jnp.dot(cnt.astype(BF16), su_ref[0:pr, 0:pr], preferred_element_type=F32)
    for s in range(TOP_K):
        bkt_ref[s:s + 1, rs] = bsel[s]
        rank_ref[s:s + 1, rs] = jnp.sum(onehot[s] * before, axis=0, keepdims=True).astype(I32)
    carry_ref[...] = carry_ref[...] + jnp.sum(cnt, axis=1, keepdims=True)


def _postmix(prompt, sample, weights, nb):
    n_p = prompt[0].shape[0]
    n_s = sample[0].shape[0]
    t = n_p // nb
    tt = min(ROW_TILE, t, n_s)
    nt = t // tt
    nblk_p = n_p // tt
    nblk = nblk_p + n_s // tt
    n_total = n_p + n_s
    prow = lambda i: (jnp.minimum(i, nblk_p - 1), 0)
    pys = lambda i: (jnp.minimum(i, nblk_p - 1) % nt, jnp.minimum(i, nblk_p - 1) // nt)
    srow = lambda i: (jnp.maximum(i - nblk_p, 0), 0)
    const = lambda i: (0, 0)

    def stream_specs(row, ysmap):
        return [pl.BlockSpec((tt, D_MODEL), row),
                pl.BlockSpec((tt, DN_WIDTH), row),
                pl.BlockSpec((tt, DN_WIDTH), row),
                pl.BlockSpec((tt, S5_WIDTH), ysmap),
                pl.BlockSpec((tt, D_MODEL), row),
                pl.BlockSpec((tt, D_MODEL), row)]

    weight_specs = [pl.BlockSpec((1, DN_WIDTH), const),
                    pl.BlockSpec((DN_WIDTH, DN_WIDTH), const),
                    pl.BlockSpec((DN_WIDTH, D_MODEL), const),
                    pl.BlockSpec((S5_WIDTH, S5_WIDTH), const),
                    pl.BlockSpec((S5_WIDTH, D_MODEL), const),
                    pl.BlockSpec((D_MODEL, D_MODEL), const),
                    pl.BlockSpec((1, D_MODEL), const),
                    pl.BlockSpec((ROUTER_ROWS, D_MODEL), const),
                    pl.BlockSpec((tt, tt), const)]
    xp, op, zp, ysp, gap, gbp = prompt
    nbk = MOE_PHASES * N_EXPERTS
    earlier = jnp.triu(jnp.ones((tt, tt), F32), k=1).astype(BF16)
    return pl.pallas_call(
        functools.partial(_postmix_kernel, nblk_p=nblk_p, range_tok=n_total // MOE_PHASES),
        grid=(nblk,),
        in_specs=stream_specs(prow, pys) + stream_specs(srow, srow) + weight_specs,
        out_specs=[pl.BlockSpec((tt, D_MODEL), lambda i: (i, 0)),
                   pl.BlockSpec((tt * ROW_SLAB, LANES), lambda i: (i, 0)),
                   pl.BlockSpec((TOP_K, tt), lambda i: (0, i)),
                   pl.BlockSpec((TOP_K, tt), lambda i: (0, i)),
                   pl.BlockSpec((TOP_K, tt), lambda i: (0, i)),
                   pl.BlockSpec((nbk, LANES), const)],
        out_shape=[jax.ShapeDtypeStruct((n_total, D_MODEL), F32),
                   jax.ShapeDtypeStruct((n_total * ROW_SLAB, LANES), F32),
                   jax.ShapeDtypeStruct((TOP_K, n_total), I32),
                   jax.ShapeDtypeStruct((TOP_K, n_total), I32),
                   jax.ShapeDtypeStruct((TOP_K, n_total), F32),
                   jax.ShapeDtypeStruct((nbk, LANES), F32)],
        scratch_shapes=[pltpu.VMEM((nbk, LANES), F32)],
        compiler_params=_cparams(("arbitrary",)),
        name="postmix",
    )(xp, op, zp, ysp, gap, gbp, *sample, *weights, earlier)


def _wait_slabs(buf, sem):
    pltpu.make_async_copy(buf, buf, sem).wait()


def _moe_kernel(texp_ref, tph_ref, tsrc_ref, tnv_ref, tfirst_ref, tslot_ref, tnext_ref, otok_ref,
                hn_hbm, wu_hbm, wd_hbm, y_ref, hnv, xbuf0, xbuf1, wu_buf, wd_buf, wub, wdb, sem, wsem):
    i = pl.program_id(0)
    tm = MOE_TILE
    rs = ROW_SLAB
    nv = tnv_ref[i]
    ph = tph_ref[i]
    range_rows = hnv.shape[0]

    def weight_copies(e, sl):
        hu, hd = D_MODEL // 2, EXPERT_FF // 2
        return (pltpu.make_async_copy(wu_hbm.at[e, 0:hu], wu_buf.at[sl, 0:hu], wsem.at[sl]),
                pltpu.make_async_copy(wu_hbm.at[e, hu:], wu_buf.at[sl, hu:], wsem.at[sl]),
                pltpu.make_async_copy(wd_hbm.at[e, 0:hd], wd_buf.at[sl, 0:hd], wsem.at[sl]),
                pltpu.make_async_copy(wd_hbm.at[e, hd:], wd_buf.at[sl, hd:], wsem.at[sl]))

    @pl.when(i == 0)
    def _():
        for p, c in enumerate(weight_copies(texp_ref[0], 0)):
            c.start(priority=p % DMA_QUEUES)

    @pl.when(jnp.logical_and(nv > 0, jnp.logical_or(i == 0, ph != tph_ref[jnp.maximum(i - 1, 0)])))
    def _():
        piece = range_rows // DMA_QUEUES
        loads = [pltpu.make_async_copy(hn_hbm.at[pl.ds(pl.multiple_of(ph * range_rows + p * piece, rs), piece), :],
                                       hnv.at[pl.ds(p * piece, piece), :], sem) for p in range(DMA_QUEUES)]
        for p, c in enumerate(loads):
            c.start(priority=p)
        for c in loads:
            c.wait()

    xbuf = (xbuf0, xbuf1)
    parity = lax.rem(i, 2)

    def gather(step, buf):
        src0 = tsrc_ref[step]
        for r in range(tm):
            tok8 = pl.multiple_of(otok_ref[src0 + r], rs)
            buf[pl.ds(r * rs, rs), :] = hnv[pl.ds(tok8, rs), :]

    for par in range(2):
        @pl.when(jnp.logical_and(parity == par, jnp.logical_and(
            nv > 0, jnp.logical_or(i == 0, ph != tph_ref[jnp.maximum(i - 1, 0)]))))
        def _(par=par):
            gather(i, xbuf[par])

    for sl in range(2):
        @pl.when(jnp.logical_and(jnp.logical_and(nv > 0, tfirst_ref[i] == 1), tslot_ref[i] == sl))
        def _():
            for c in weight_copies(texp_ref[i], sl):
                c.wait()

            @pl.when(tnext_ref[i] >= 0)
            def _():
                for p, c in enumerate(weight_copies(tnext_ref[i], 1 - sl)):
                    c.start(priority=p % DMA_QUEUES)

            wub[...] = wu_buf[sl].astype(BF16)
            wdb[...] = wd_buf[sl].astype(BF16)

    @pl.when(nv == 0)
    def _():
        y_ref[...] = jnp.zeros_like(y_ref)

    for par in range(2):
        @pl.when(jnp.logical_and(parity == par, nv > 0))
        def _(par=par):
            gather(jnp.minimum(i + 1, pl.num_programs(0) - 1), xbuf[1 - par])
            x = _slab_load(xbuf[par], tm).astype(BF16)
            hu = jnp.dot(x, wub[...], preferred_element_type=F32)
            gate = hu[:, :EXPERT_FF]
            up = hu[:, EXPERT_FF:]
            act = gate * _sigmoid(gate) * up
            _slab_store(y_ref, jnp.dot(act.astype(BF16), wdb[...], preferred_element_type=F32))


def _moe(hn, w_up, w_down, plan):
    ntiles = plan[0].shape[0]
    grid_spec = pltpu.PrefetchScalarGridSpec(
        num_scalar_prefetch=len(plan),
        grid=(ntiles,),
        in_specs=[pl.BlockSpec(memory_space=pl.ANY),
                  pl.BlockSpec(memory_space=pl.ANY),
                  pl.BlockSpec(memory_space=pl.ANY)],
        out_specs=pl.BlockSpec((MOE_TILE * ROW_SLAB, LANES), lambda i, *_: (i, 0)),
        scratch_shapes=[pltpu.VMEM((hn.shape[0] // MOE_PHASES, LANES), F32),
                        pltpu.VMEM((MOE_TILE * ROW_SLAB, LANES), F32),
                        pltpu.VMEM((MOE_TILE * ROW_SLAB, LANES), F32),
                        pltpu.VMEM((2, D_MODEL, 2 * EXPERT_FF), F32),
                        pltpu.VMEM((2, EXPERT_FF, D_MODEL), F32),
                        pltpu.VMEM((D_MODEL, 2 * EXPERT_FF), BF16),
                        pltpu.VMEM((EXPERT_FF, D_MODEL), BF16),
                        pltpu.SemaphoreType.DMA,
                        pltpu.SemaphoreType.DMA((2,))])
    return pl.pallas_call(
        _moe_kernel,
        grid_spec=grid_spec,
        out_shape=jax.ShapeDtypeStruct((ntiles * MOE_TILE * ROW_SLAB, LANES), F32),
        compiler_params=_cparams(("arbitrary",)),
        name="moe",
    )(*plan, hn, w_up, w_down)


def _combine_kernel(pos_ref, x1_ref, ys_hbm, w_ref, nw_ref, outp_ref, outs_ref,
                    ybuf0, ybuf1, sem, *, nblk_p, n_tok):
    i = pl.program_id(0)
    nsteps = pl.num_programs(0)
    tt = x1_ref.shape[0]
    rs = ROW_SLAB
    slot = lax.rem(i, 2)
    ybuf = (ybuf0, ybuf1)

    def start_gather(step, sl):
        base = step * tt
        for r in range(tt * TOP_K):
            j, s = divmod(r, TOP_K)
            p8 = pl.multiple_of(pos_ref[s * n_tok + base + j], rs)
            pltpu.make_async_copy(ys_hbm.at[pl.ds(p8, rs), :], ybuf[sl].at[pl.ds((s * tt + j) * rs, rs), :],
                                  sem.at[sl]).start(priority=r % DMA_QUEUES)

    @pl.when(i == 0)
    def _():
        start_gather(0, 0)

    for sl in range(2):
        @pl.when(slot == sl)
        def _():
            _wait_slabs(ybuf[sl], sem.at[sl])
            start_gather(jnp.minimum(i + 1, nsteps - 1), 1 - sl)
            w = w_ref[...]
            y0 = _slab_load(ybuf[sl], tt, 0)
            y1 = _slab_load(ybuf[sl], tt, tt * rs)
            x = x1_ref[...] + w[:, 0:1] * y0 + w[:, 1:2] * y1
            res = x * lax.rsqrt(jnp.mean(x * x, axis=-1, keepdims=True) + RMS_EPS) * nw_ref[...]

            @pl.when(i < nblk_p)
            def _():
                outp_ref[...] = res

            @pl.when(i >= nblk_p)
            def _():
                outs_ref[...] = res

        @pl.when(jnp.logical_and(slot == sl, i == nsteps - 1))
        def _():
            _wait_slabs(ybuf[1 - sl], sem.at[1 - sl])


def _combine(x1, ysorted, pos8, wtok, nw, n_p):
    n = x1.shape[0]
    tt = math.gcd(math.gcd(n_p, n - n_p), COMBINE_TILE)
    nblk_p = n_p // tt
    grid_spec = pltpu.PrefetchScalarGridSpec(
        num_scalar_prefetch=1,
        grid=(n // tt,),
        in_specs=[pl.BlockSpec((tt, D_MODEL), lambda i, *_: (i, 0)),
                  pl.BlockSpec(memory_space=pl.ANY),
                  pl.BlockSpec((tt, TOP_K), lambda i, *_: (i, 0)),
                  pl.BlockSpec((1, D_MODEL), lambda i, *_: (0, 0))],
        out_specs=[pl.BlockSpec((tt, D_MODEL), lambda i, *_: (jnp.minimum(i, nblk_p - 1), 0)),
                   pl.BlockSpec((tt, D_MODEL), lambda i, *_: (jnp.maximum(i - nblk_p, 0), 0))],
        scratch_shapes=[pltpu.VMEM((tt * TOP_K * ROW_SLAB, LANES), F32),
                        pltpu.VMEM((tt * TOP_K * ROW_SLAB, LANES), F32),
                        pltpu.SemaphoreType.DMA((2,))])
    return pl.pallas_call(
        functools.partial(_combine_kernel, nblk_p=nblk_p, n_tok=n),
        grid_spec=grid_spec,
        out_shape=[jax.ShapeDtypeStruct((n_p, D_MODEL), F32),
                   jax.ShapeDtypeStruct((n - n_p, D_MODEL), F32)],
        compiler_params=_cparams(("arbitrary",)),
        name="combine",
    )(pos8, x1, ysorted, wtok, nw)


def _route_plan(bkt, rank, cnt, n_tok):
    tm = MOE_TILE
    n_assign = n_tok * TOP_K
    nbk = MOE_PHASES * N_EXPERTS
    ntiles = n_assign // tm + nbk
    range_tok = n_tok // MOE_PHASES
    b_flat = bkt.T.reshape(n_assign)
    order = jnp.argsort(b_flat, stable=True).astype(I32)
    counts = cnt[:, 0].astype(I32)
    cstart = jnp.cumsum(counts) - counts
    tiles_b = (counts + tm - 1) // tm
    tend = jnp.cumsum(tiles_b)
    tstart = tend - tiles_b
    tile_id = jnp.arange(ntiles, dtype=I32)
    tbk = jnp.minimum(jnp.sum((tile_id[:, None] >= tend[None, :]).astype(I32), axis=1), nbk - 1)
    onehot = (tbk[:, None] == jnp.arange(nbk, dtype=I32)[None, :]).astype(I32)
    pick = lambda v: jnp.sum(onehot * v[None, :], axis=1)
    done = (tile_id - pick(tstart)) * tm
    tnv = jnp.where(tile_id < tend[-1], jnp.clip(pick(counts) - done, 0, tm), 0)
    tsrc = jnp.where(tnv > 0, pick(cstart) + done, 0)
    texp = tbk % N_EXPERTS
    tph = tbk // N_EXPERTS
    nonempty = counts > 0
    bslot = (jnp.cumsum(nonempty.astype(I32)) - 1) % 2
    bidx = jnp.where(nonempty, jnp.arange(nbk, dtype=I32), nbk)
    nxt = jnp.concatenate([lax.cummin(bidx[::-1])[::-1][1:], jnp.full((1,), nbk, I32)])
    bnext = jnp.where(nxt < nbk, nxt % N_EXPERTS, -1)
    tfirst = jnp.logical_and(tnv > 0, done == 0).astype(I32)
    tslot = pick(bslot)
    tnext = pick(bnext)
    otok8 = jnp.concatenate([((order // TOP_K) % range_tok) * ROW_SLAB, jnp.zeros((tm,), I32)])
    plan = tuple(a.astype(I32) for a in (texp, tph, tsrc, tnv, tfirst, tslot, tnext, otok8))
    first = jnp.sum((bkt[:, :, None] == jnp.arange(nbk, dtype=I32)[None, None, :]).astype(I32)
                    * (tstart * tm)[None, None, :], axis=2)
    pos8 = ((first + rank) * ROW_SLAB).reshape(n_assign)
    return plan, pos8.astype(I32)


def _block_diag(m):
    g, a, b = m.shape
    eye = jnp.eye(g, dtype=m.dtype)
    return (eye[:, None, :, None] * m[:, :, None, :]).reshape(g * a, g * b)


def kernel(x_prompt, x_sample, state_conv, state_delta, state_ssm_re, state_ssm_im, norm_mix_w, w_in, conv_w, a_log, dt_bias, head_norm_w, w_a_up, s5_lambda_re, s5_lambda_im, s5_log_step, s5_b_re, s5_b_im, s5_c_re, s5_c_im, s5_d, w_glu, w_b_up, w_o, norm_ffn_w, w_router_coarse, w_router_fine, w_expert_up, w_expert_down, norm_final_w):
    bp, tp, _ = x_prompt.shape
    bs, ts, _ = x_sample.shape
    n_p = bp * tp
    n_s = bs * ts
    n_tok = n_p + n_s
    l = 0

    w = w_in[l].astype(BF16)
    c_ab = W1_COLS + 2 * DN_HEADS
    w_parts = (w[:, :W1_COLS], w[:, c_ab:],
               jnp.concatenate([w[:, W1_COLS:c_ab], jnp.zeros((D_MODEL, LANES - 2 * DN_HEADS), BF16)], axis=1))
    nw_mix = norm_mix_w[l].reshape(1, D_MODEL)
    pad8 = lambda v: jnp.concatenate([v, jnp.zeros((LANES - DN_HEADS,), F32)]).reshape(1, LANES)
    gate_p = jnp.concatenate([pad8(a_log[l]), pad8(dt_bias[l])], axis=0)
    seg = _block_diag(jnp.ones((DN_HEADS, DN_HEAD_DIM, DN_HEAD_DIM), BF16))
    chan_rows = lambda b: jnp.swapaxes(b, 1, 2).reshape(S5_WIDTH, S5_STATE)
    state_rows = lambda c: jnp.tile(jnp.swapaxes(c, 1, 2).reshape(S5_FLAT, S5_GROUP_CH),
                                    (1, LANES // S5_GROUP_CH))
    lam = jnp.concatenate([s5_lambda_re[l].reshape(1, S5_FLAT), s5_lambda_im[l].reshape(1, S5_FLAT),
                           jnp.repeat(s5_log_step[l], S5_STATE).reshape(1, S5_FLAT),
                           jnp.zeros((SUBLANES - 3, S5_FLAT), F32)], axis=0)
    s5_params = (chan_rows(s5_b_re[l]), chan_rows(s5_b_im[l]), lam,
                 state_rows(s5_c_re[l]), state_rows(s5_c_im[l]), s5_d[l].reshape(1, S5_WIDTH))
    hw = jnp.tile(head_norm_w[l], DN_HEADS).reshape(1, DN_WIDTH)
    wr = jnp.concatenate([w_router_fine[l].T, w_router_coarse[l].T,
                          jnp.zeros((ROUTER_ROWS - N_EXPERTS - MOE_GROUPS, D_MODEL), F32)], axis=0)
    pm_weights = (hw, seg, w_a_up[l].astype(BF16), w_glu[l].astype(BF16), w_b_up[l].astype(BF16),
                  w_o[l].astype(BF16), norm_ffn_w[l].reshape(1, D_MODEL), wr)

    xp2 = x_prompt.reshape(n_p, D_MODEL)
    q_p, k_p, v_p, gates_p, conv_p, z_p, u_p, ga_p, gb_p = _inprep(
        xp2, nw_mix, w_parts, jnp.zeros((bp, SUBLANES, QKV_DIM), F32), conv_w[l], gate_p, seg, bp, 1)
    o_p, delta_p = _delta_prompt(q_p, k_p, v_p, gates_p, bp)
    ys_p, h_p = _s5(u_p, s5_params, jnp.zeros((bp, 2 * S5_FLAT), F32), bp, True)

    xs2 = jnp.swapaxes(x_sample, 0, 1).reshape(n_s, D_MODEL)
    cinit_s = jnp.swapaxes(state_conv[l], 0, 1).reshape(1, (CONV_W - 1) * bs, QKV_DIM)
    q_s, k_s, v_s, gate_s, conv_s, z_s, u_s, ga_s, gb_s = _inprep(
        xs2, nw_mix, w_parts, cinit_s, conv_w[l], gate_p, seg, 1, bs)
    s0t = jnp.transpose(state_delta[l], (1, 2, 3, 0)).reshape(DN_HEADS * DN_HEAD_DIM * DN_HEAD_DIM, bs)
    o_s, delta_st = _delta_sample(q_s, k_s, v_s, gate_s, s0t, bs, ts)
    delta_s = jnp.transpose(delta_st.reshape(DN_HEADS, DN_HEAD_DIM, DN_HEAD_DIM, bs), (3, 0, 1, 2))
    h0_s = jnp.concatenate([state_ssm_re[l].reshape(bs, S5_FLAT), state_ssm_im[l].reshape(bs, S5_FLAT)], axis=1)
    ys_s, h_s = _s5(u_s, s5_params, h0_s, bs, False)
    x1, hn, bkt, rank, rw, cnt = _postmix((xp2, o_p, z_p, ys_p, ga_p, gb_p), (xs2, o_s, z_s, ys_s, ga_s, gb_s),
                                          pm_weights, bp)

    plan, pos8 = _route_plan(bkt, rank, cnt, n_tok)
    ysorted = _moe(hn, w_expert_up[l], w_expert_down[l], plan)
    y_p, y_s = _combine(x1, ysorted, pos8, rw.T, norm_final_w.reshape(1, D_MODEL), n_p)

    y_prompt = y_p.reshape(bp, tp, D_MODEL)
    y_sample = jnp.swapaxes(y_s.reshape(ts, bs, D_MODEL), 0, 1)
    conv_sample = jnp.swapaxes(conv_s.reshape(CONV_W - 1, bs, QKV_DIM), 0, 1)
    return (y_prompt, y_sample,
            conv_p[None], delta_p[None],
            h_p[:, :S5_FLAT].reshape(1, bp, S5_GROUPS, S5_STATE), h_p[:, S5_FLAT:].reshape(1, bp, S5_GROUPS, S5_STATE),
            conv_sample[None], delta_s[None],
            h_s[:, :S5_FLAT].reshape(1, bs, S5_GROUPS, S5_STATE), h_s[:, S5_FLAT:].reshape(1, bs, S5_GROUPS, S5_STATE))
```

```python
import functools
import math

import jax
import jax.numpy as jnp
import numpy as np
from jax import lax
from jax.experimental import pallas as pl
from jax.experimental.pallas import tpu as pltpu

F32 = jnp.float32
BF16 = jnp.bfloat16
I32 = jnp.int32

D_MODEL = 1024
DN_HEADS = 8
DN_HEAD_DIM = 64
DN_WIDTH = DN_HEADS * DN_HEAD_DIM
QKV_DIM = 3 * DN_WIDTH
CONV_W = 4
DN_CHUNK = 64
S5_GROUP_CH = 16
S5_WIDTH = D_MODEL // 2
S5_GROUPS = S5_WIDTH // S5_GROUP_CH
S5_STATE = 64
S5_FLAT = S5_GROUPS * S5_STATE
MOE_GROUPS = 4
EXPERTS_PER_GROUP = 8
N_EXPERTS = MOE_GROUPS * EXPERTS_PER_GROUP
TOP_K = 2
EXPERT_FF = 256
RMS_EPS = 1e-6
L2_EPS = 1e-6

LANES = 128
SUBLANES = 8
VMEM_LIMIT = 56 * 1024 * 1024

W1_COLS = QKV_DIM + DN_WIDTH
W2_COLS = S5_WIDTH + 2 * D_MODEL

ROW_TILE = 512
INPREP_PARTS = 2
POSTMIX_PARTS = 2
MOE_TILE = 256
MOE_PHASES = 2
COMBINE_TILE = 512
DMA_QUEUES = 2
DELTA_SUBCHUNKS = 4
S5_SUPER = 2
S5_TILE_ROWS = 1024
S5_SCAN_SPLIT = 2
ROUTER_ROWS = 40


def _mm(a, b):
    return jnp.dot(a.astype(BF16), b.astype(BF16), preferred_element_type=F32)


def _mm_nt(a, b):
    return lax.dot_general(a.astype(BF16), b.astype(BF16), (((1,), (1,)), ((), ())),
                           preferred_element_type=F32)


def _sigmoid(x):
    return 0.5 * jnp.tanh(0.5 * x) + 0.5


def _cparams(sem):
    return pltpu.CompilerParams(dimension_semantics=sem, vmem_limit_bytes=VMEM_LIMIT)


ROW_SLAB = D_MODEL // LANES


def _slab_load(ref, rows, first=0, pitch=ROW_SLAB):
    return jnp.concatenate([ref[pl.ds(first + j, rows, stride=pitch), :] for j in range(ROW_SLAB)], axis=1)


def _slab_store(ref, x, first=0):
    for j in range(ROW_SLAB):
        ref[pl.ds(first * ROW_SLAB + j, x.shape[0], stride=ROW_SLAB), :] = x[:, j * LANES:(j + 1) * LANES]


def _softplus(x):
    return jnp.maximum(x, 0.0) + jnp.log1p(jnp.exp(-jnp.abs(x)))


def _inprep_kernel(x_ref, nw_ref, w1_ref, w2_ref, wab_ref, cinit_ref, cw_ref, gp_ref, seg_ref,
                   q_ref, k_ref, v_ref, gate_ref, cnew_ref, z_ref, u_ref, ga_ref, gb_ref, xp_ref,
                   *, shift, rc, rows):
    @pl.when(pl.program_id(1) == 0)
    def _():
        xp_ref[0:rc, :] = cinit_ref[0]

    seg = seg_ref[...]
    pr = rows // INPREP_PARTS

    def part_stages(part):
        rs = slice(part * pr, (part + 1) * pr)
        x = x_ref[rs, :]
        h = x * lax.rsqrt(jnp.mean(x * x, axis=-1, keepdims=True) + RMS_EPS) * nw_ref[...]
        hb = h.astype(BF16)

        def proj(w_ref, lo, hi):
            return jnp.dot(hb, w_ref[:, lo:hi], preferred_element_type=F32)

        xp_ref[rc + part * pr:rc + (part + 1) * pr, :] = proj(w1_ref, 0, QKV_DIM)
        ab = proj(wab_ref, 0, LANES)
        yield
        z_ref[rs, :] = proj(w1_ref, QKV_DIM, W1_COLS).astype(z_ref.dtype)
        u_ref[rs, :] = proj(w2_ref, 0, S5_WIDTH).astype(u_ref.dtype)
        acc = None
        for i in range(CONV_W):
            lo = rc + part * pr + (i - (CONV_W - 1)) * shift
            term = xp_ref[lo:lo + pr, :] * cw_ref[i:i + 1, :]
            acc = term if acc is None else acc + term
        y = acc * _sigmoid(acc)
        yield
        ga_ref[rs, :] = proj(w2_ref, S5_WIDTH, S5_WIDTH + D_MODEL).astype(ga_ref.dtype)
        q = y[:, 0:DN_WIDTH]
        k = y[:, DN_WIDTH:2 * DN_WIDTH]
        q_ref[rs, :] = q * lax.rsqrt(jnp.dot((q * q).astype(BF16), seg, preferred_element_type=F32) + L2_EPS)
        k_ref[rs, :] = k * lax.rsqrt(jnp.dot((k * k).astype(BF16), seg, preferred_element_type=F32) + L2_EPS)
        v_ref[rs, :] = y[:, 2 * DN_WIDTH:]
        yield
        gb_ref[rs, :] = proj(w2_ref, S5_WIDTH + D_MODEL, W2_COLS).astype(gb_ref.dtype)
        g = -jnp.exp(gp_ref[0:1, :]) * _softplus(ab + gp_ref[1:2, :])
        beta = _sigmoid(ab)
        lane = lax.broadcasted_iota(I32, ab.shape, 1)
        gate_ref[rs, :] = jnp.where(lane < DN_HEADS, g, beta)

    live = []
    pending = [part_stages(p) for p in range(INPREP_PARTS)]
    while live or pending:
        if pending:
            live.append(pending.pop(0))
        live = [g for g in live if next(g, StopIteration) is not StopIteration]

    keep = (CONV_W - 1) * shift
    cnew_ref[0] = xp_ref[rc + rows - keep:rc + rows, :]
    xp_ref[0:rc, :] = xp_ref[rows:rows + rc, :]


def _inprep(x2d, nw, w_parts, cinit, conv_w, gate_p, seg, nb, shift):
    n = x2d.shape[0]
    r = n // nb
    rows = min(ROW_TILE, r)
    nt = r // rows
    rc = cinit.shape[1]
    keep = (CONV_W - 1) * shift
    row = lambda b, i: (b * nt + i, 0)
    const = lambda b, i: (0, 0)
    kern = functools.partial(_inprep_kernel, shift=shift, rc=rc, rows=rows)
    outs = pl.pallas_call(
        kern,
        grid=(nb, nt),
        in_specs=[pl.BlockSpec((rows, D_MODEL), row),
                  pl.BlockSpec((1, D_MODEL), const),
                  pl.BlockSpec((D_MODEL, W1_COLS), const),
                  pl.BlockSpec((D_MODEL, W2_COLS), const),
                  pl.BlockSpec((D_MODEL, LANES), const),
                  pl.BlockSpec((1, rc, QKV_DIM), lambda b, i: (b, 0, 0)),
                  pl.BlockSpec((CONV_W, QKV_DIM), const),
                  pl.BlockSpec((2, LANES), const),
                  pl.BlockSpec((DN_WIDTH, DN_WIDTH), const)],
        out_specs=[pl.BlockSpec((rows, DN_WIDTH), row),
                   pl.BlockSpec((rows, DN_WIDTH), row),
                   pl.BlockSpec((rows, DN_WIDTH), row),
                   pl.BlockSpec((rows, LANES), row),
                   pl.BlockSpec((1, keep, QKV_DIM), lambda b, i: (b, 0, 0)),
                   pl.BlockSpec((rows, DN_WIDTH), row),
                   pl.BlockSpec((rows, S5_WIDTH), lambda b, i: (i, b)),
                   pl.BlockSpec((rows, D_MODEL), row),
                   pl.BlockSpec((rows, D_MODEL), row)],
        out_shape=[jax.ShapeDtypeStruct((n, DN_WIDTH), F32),
                   jax.ShapeDtypeStruct((n, DN_WIDTH), F32),
                   jax.ShapeDtypeStruct((n, DN_WIDTH), F32),
                   jax.ShapeDtypeStruct((n, LANES), F32),
                   jax.ShapeDtypeStruct((nb, keep, QKV_DIM), F32),
                   jax.ShapeDtypeStruct((n, DN_WIDTH), BF16),
                   jax.ShapeDtypeStruct((r, nb * S5_WIDTH), BF16),
                   jax.ShapeDtypeStruct((n, D_MODEL), BF16),
                   jax.ShapeDtypeStruct((n, D_MODEL), BF16)],
        scratch_shapes=[pltpu.VMEM((rc + rows, QKV_DIM), F32)],
        compiler_params=_cparams(("arbitrary", "arbitrary")),
        name="inprep",
    )(x2d, nw, *w_parts, cinit, conv_w, gate_p, seg)
    return outs


def _delta_home(low, h, x, other=0.0):
    return jnp.where(low, x, other) if h % 2 == 0 else jnp.where(low, other, x)


def _delta_prepare(q_ref, k_ref, v_ref, gate_ref, tril_ref, bufs, *, nsub):
    sol_buf, wq_buf, qk_buf, kdec_buf, dl_buf = bufs
    c = DN_CHUNK
    dk = DN_HEAD_DIM

    def home(h, x, other=0.0):
        return _delta_home(low, h, x, other)

    rowi2 = lax.broadcasted_iota(I32, (c, 2 * c), 0)
    lane2 = lax.broadcasted_iota(I32, (c, 2 * c), 1)
    coli2 = lane2 & (c - 1)
    causal2 = rowi2 >= coli2
    strict2 = rowi2 > coli2
    low = lane2 < dk
    tril = tril_ref[...]
    pairs = [(j, h) for j in range(nsub) for h in range(DN_HEADS)]
    units = [(j, pr) for j in range(nsub) for pr in range(DN_HEADS // 2)]
    rows = [slice(j * c, (j + 1) * c) for j in range(nsub)]
    gate = [gate_ref[rows[j], :] for j in range(nsub)]
    gc_all = [_split3_dot_left(tril, gate[j]) for j in range(nsub)]
    gc_t = [gc_all[j].T for j in range(nsub)]

    def block(ref, j, pr):
        return ref[rows[j], pr * LANES:(pr + 1) * LANES]

    gfull = {(j, h): jnp.broadcast_to(gc_all[j][:, h:h + 1], (c, 2 * c)) for j, h in pairs}
    g2 = {(j, pr): jnp.where(low, gfull[j, 2 * pr], gfull[j, 2 * pr + 1]) for j, pr in units}
    b2 = {(j, pr): jnp.where(low, gate[j][:, DN_HEADS + 2 * pr:DN_HEADS + 2 * pr + 1],
                             gate[j][:, DN_HEADS + 2 * pr + 1:DN_HEADS + 2 * pr + 2]) for j, pr in units}
    kp = {u: block(k_ref, *u) for u in units}
    qp = {u: block(q_ref, *u) * (dk ** -0.5) for u in units}
    egc2 = {u: jnp.exp(g2[u]) for u in units}
    kb2 = {u: kp[u] * b2[u] for u in units}
    vb2 = {u: block(v_ref, *u) * b2[u] for u in units}
    kw2s = {u: pltpu.roll(kb2[u] * egc2[u], dk, axis=1) for u in units}
    qd2 = {u: qp[u] * egc2[u] for u in units}
    glast2 = {u: g2[u][c - 1:c, :] for u in units}
    kdec_t2 = {u: (kp[u] * jnp.exp(glast2[u] - g2[u])).T for u in units}
    dlast2 = {u: jnp.exp(glast2[u]) for u in units}
    kk = {u: jnp.concatenate([kp[u], kp[u]], axis=0) for u in units}
    yield

    grow2 = {(j, h): jnp.concatenate([gc_t[j][h:h + 1, :], gc_t[j][h:h + 1, :]], axis=1) for j, h in pairs}
    decay = {p: jnp.where(causal2, jnp.exp(jnp.where(causal2, gfull[p] - grow2[p], 0.0)), 0.0) for p in pairs}
    gram = {(j, h): _mm_nt(jnp.concatenate([home(h, kb2[j, h // 2]), home(h, qp[j, h // 2])], axis=0), kk[j, h // 2])
            for j, h in pairs}
    mat = {p: jnp.where(strict2, gram[p][:c] * decay[p], 0.0).astype(BF16) for p in pairs}
    qk = {p: jnp.where(causal2, gram[p][c:] * decay[p], 0.0) for p in pairs}
    sol = {(j, h): home(h, vb2[j, h // 2], kw2s[j, h // 2]) for j, h in pairs}
    yield
    levels = int(math.log2(c))
    zeros2 = jnp.zeros((c, 2 * c), BF16)
    for lvl in range(levels):
        hi = {p: sol[p].astype(BF16) for p in pairs}
        lo = {p: (sol[p] - hi[p].astype(F32)).astype(BF16) for p in pairs}
        if lvl < levels - 1:
            y = {p: jnp.dot(mat[p], jnp.concatenate([jnp.concatenate([hi[p], mat[p]], axis=1),
                                                     jnp.concatenate([lo[p], zeros2], axis=1)], axis=0),
                            preferred_element_type=F32) for p in pairs}
            mat = {p: y[p][:, 2 * dk:].astype(BF16) for p in pairs}
            upd = {p: y[p][:, :2 * dk] for p in pairs}
        else:
            upd = {p: jnp.dot(mat[p], jnp.concatenate([hi[p], lo[p]], axis=0), preferred_element_type=F32)
                   for p in pairs}
        sol = {p: (sol[p] - upd[p]) if lvl == 0 else (sol[p] + upd[p]) for p in pairs}
        yield
    for j, h in pairs:
        n = j * DN_HEADS + h
        sol_buf[n] = sol[j, h]
        wq_buf[n] = jnp.concatenate([home(h, 0.0, sol[j, h]), home(h, qd2[j, h // 2])], axis=0).astype(BF16)
        qk_buf[n] = qk[j, h].astype(BF16)
    for j, pr in units:
        n = j * (DN_HEADS // 2) + pr
        kdec_buf[n] = kdec_t2[j, pr].astype(BF16)
        dl_buf[n] = jnp.broadcast_to(dlast2[j, pr], (SUBLANES, LANES))


def _delta_apply(bufs, o_ref, s_ref, *, nsub):
    sol_buf, wq_buf, qk_buf, kdec_buf, dl_buf = bufs
    c = DN_CHUNK
    dk = DN_HEAD_DIM
    heads = range(DN_HEADS)
    low = lax.broadcasted_iota(I32, (c, 2 * c), 1) < dk
    s = [s_ref[h] for h in heads]
    for j in range(nsub):
        ws, v_new, o_h = [], [], []
        for h in heads:
            n = j * DN_HEADS + h
            ws.append(jnp.dot(wq_buf[n], jnp.concatenate([s[h], s[h]], axis=0).astype(BF16),
                              preferred_element_type=F32))
        yield
        for h in heads:
            v_new.append(sol_buf[j * DN_HEADS + h] - ws[h][:c])
        for h in heads:
            o_h.append(ws[h][c:] + jnp.dot(qk_buf[j * DN_HEADS + h][:, :c], v_new[h].astype(BF16),
                                           preferred_element_type=F32))
        for pr in range(DN_HEADS // 2):
            o_ref[j * c:(j + 1) * c, pr * LANES:(pr + 1) * LANES] = jnp.where(low, o_h[2 * pr], o_h[2 * pr + 1])
        nxt = []
        for h in heads:
            u = j * (DN_HEADS // 2) + h // 2
            kdt = kdec_buf[u][(h % 2) * dk:(h % 2 + 1) * dk, :]
            d = dl_buf[u][0:1, :]
            nxt.append(_delta_home(low, h, s[h] * d + jnp.dot(kdt, v_new[h].astype(BF16),
                                                               preferred_element_type=F32)))
        s = nxt
        yield
    for h in heads:
        s_ref[h] = s[h]


def _delta_chunk_kernel(q_ref, k_ref, v_ref, gate_ref, tril_ref, o_ref, sfin_ref, s_ref, *bufs, nsub, nc):
    i = pl.program_id(0)
    half = len(bufs) // 2
    sets = (bufs[:half], bufs[half:])
    local = lax.rem(jnp.maximum(i - 1, 0), nc)

    @pl.when(i == 0)
    def _():
        for b in sets[1]:
            b[...] = jnp.zeros_like(b)

    @pl.when(local == 0)
    def _():
        s_ref[...] = jnp.zeros_like(s_ref)

    for par in range(2):
        @pl.when(lax.rem(i, 2) == par)
        def _(par=par):
            parts = [_delta_prepare(q_ref, k_ref, v_ref, gate_ref, tril_ref, sets[par], nsub=nsub),
                     _delta_apply(sets[1 - par], o_ref, s_ref, nsub=nsub)]
            while parts:
                parts = [g for g in parts if next(g, StopIteration) is not StopIteration]

    @pl.when(jnp.logical_and(i >= 1, local == nc - 1))
    def _():
        dk = DN_HEAD_DIM
        for h in range(DN_HEADS):
            sfin_ref[0, h] = s_ref[h][:, (h % 2) * dk:(h % 2 + 1) * dk]


def _split3_dot_left(b01, a):
    a1 = a.astype(BF16)
    r1 = a - a1.astype(F32)
    a2 = r1.astype(BF16)
    a3 = (r1 - a2.astype(F32)).astype(BF16)
    out = jnp.dot(b01, a3, preferred_element_type=F32)
    out = out + jnp.dot(b01, a2, preferred_element_type=F32)
    return out + jnp.dot(b01, a1, preferred_element_type=F32)


def _delta_prompt(q, k, v, gate, nb):
    n = q.shape[0]
    t = n // nb
    c = DN_CHUNK
    nsub = DELTA_SUBCHUNKS
    rows = nsub * c
    nc = t // rows
    nblk = nb * nc
    row_in = lambda i: (jnp.minimum(i, nblk - 1), 0)
    row_out = lambda i: (jnp.maximum(i - 1, 0), 0)
    tril = jnp.tril(jnp.ones((c, c), F32)).astype(BF16)
    nh = nsub * DN_HEADS
    npair = nsub * DN_HEADS // 2
    buf_set = [pltpu.VMEM((nh, c, 2 * DN_HEAD_DIM), F32),
               pltpu.VMEM((nh, 2 * c, 2 * DN_HEAD_DIM), BF16),
               pltpu.VMEM((nh, c, 2 * c), BF16),
               pltpu.VMEM((npair, 2 * DN_HEAD_DIM, c), BF16),
               pltpu.VMEM((npair, SUBLANES, LANES), F32)]
    return pl.pallas_call(
        functools.partial(_delta_chunk_kernel, nsub=nsub, nc=nc),
        grid=(nblk + 1,),
        in_specs=[pl.BlockSpec((rows, DN_WIDTH), row_in),
                  pl.BlockSpec((rows, DN_WIDTH), row_in),
                  pl.BlockSpec((rows, DN_WIDTH), row_in),
                  pl.BlockSpec((rows, LANES), row_in),
                  pl.BlockSpec((c, c), lambda i: (0, 0))],
        out_specs=[pl.BlockSpec((rows, DN_WIDTH), row_out),
                   pl.BlockSpec((1, DN_HEADS, DN_HEAD_DIM, DN_HEAD_DIM),
                                lambda i: (jnp.maximum(i - 1, 0) // nc, 0, 0, 0))],
        out_shape=[jax.ShapeDtypeStruct((n, DN_WIDTH), F32),
                   jax.ShapeDtypeStruct((nb, DN_HEADS, DN_HEAD_DIM, DN_HEAD_DIM), F32)],
        scratch_shapes=[pltpu.VMEM((DN_HEADS, DN_HEAD_DIM, 2 * DN_HEAD_DIM), F32)] + buf_set + buf_set,
        compiler_params=_cparams(("arbitrary",)),
        name="delta_prompt",
    )(q, k, v, gate, tril)


def _delta_step_kernel(q_ref, k_ref, v_ref, gate_ref, s0_ref, o_ref, s_ref, kt_ref, qt_ref, gt_ref, *, nt, nb):
    dk = DN_HEAD_DIM
    p = pl.program_id(0)
    for t in range(nt):
        rs = slice(t * nb, (t + 1) * nb)
        gt_ref[...] = gate_ref[rs, :].T
        kt_ref[...] = k_ref[rs, :].T
        qt_ref[...] = (q_ref[rs, :] * (dk ** -0.5)).T
        vt = v_ref[rs, :].T
        src = s0_ref if t == 0 else s_ref
        o_heads = []
        for j in range(2):
            a = jnp.exp(gt_ref[pl.ds(2 * p + j, 1), :])
            beta = gt_ref[pl.ds(2 * p + j + DN_HEADS, 1), :]
            base = j * dk * dk

            def k_dot_s(d, acc, j=j, base=base, src=src):
                sd = src[pl.ds(pl.multiple_of(base + d * dk, dk), dk), :]
                return acc + kt_ref[pl.ds(j * dk + d, 1), :] * sd

            ks = lax.fori_loop(0, dk, k_dot_s, jnp.zeros((dk, nb), F32), unroll=4)
            delta = beta * (vt[j * dk:(j + 1) * dk, :] - a * ks)

            def update(d, acc, j=j, base=base, src=src, a=a, delta=delta):
                r0 = pl.multiple_of(base + d * dk, dk)
                sn = a * src[pl.ds(r0, dk), :] + kt_ref[pl.ds(j * dk + d, 1), :] * delta
                s_ref[pl.ds(r0, dk), :] = sn
                return acc + qt_ref[pl.ds(j * dk + d, 1), :] * sn

            o_heads.append(lax.fori_loop(0, dk, update, jnp.zeros((dk, nb), F32), unroll=4))
        o_ref[rs, :] = jnp.concatenate(o_heads, axis=0).T


def _delta_sample(q, k, v, gate, s0t, nb, nt):
    dk = DN_HEAD_DIM
    flat = dk * dk
    n = nt * nb
    kern = functools.partial(_delta_step_kernel, nt=nt, nb=nb)
    pair = lambda p: (0, p)
    return pl.pallas_call(
        kern,
        grid=(DN_HEADS // 2,),
        in_specs=[pl.BlockSpec((n, LANES), pair),
                  pl.BlockSpec((n, LANES), pair),
                  pl.BlockSpec((n, LANES), pair),
                  pl.BlockSpec((n, LANES), lambda p: (0, 0)),
                  pl.BlockSpec((2 * flat, nb), lambda p: (p, 0))],
        out_specs=[pl.BlockSpec((n, LANES), pair),
                   pl.BlockSpec((2 * flat, nb), lambda p: (p, 0))],
        out_shape=[jax.ShapeDtypeStruct((n, DN_WIDTH), F32),
                   jax.ShapeDtypeStruct((DN_HEADS * flat, nb), F32)],
        scratch_shapes=[pltpu.VMEM((LANES, nb), F32),
                        pltpu.VMEM((LANES, nb), F32),
                        pltpu.VMEM((LANES, nb), F32)],
        compiler_params=_cparams(("arbitrary",)),
        name="delta_sample",
    )(q, k, v, gate, s0t)


def _s5_kernel(u_ref, btre_ref, btim_ref, lam_ref, ctre_ref, ctim_ref, d_ref, h0_ref, y_ref, hfin_ref,
               bw_ref, c_ref, ab_ref, x_ref, h_ref, ru_ref, ry_ref, *, nb, tt, wide):
    p2 = S5_FLAT

    @pl.when(pl.program_id(0) == 0)
    def _():
        lr = lam_ref[0:1, :]
        li = lam_ref[1:2, :]
        dt = jnp.exp(lam_ref[2:3, :])
        mag = jnp.exp(lr * dt)
        ab_re = mag * jnp.cos(li * dt)
        ab_im = mag * jnp.sin(li * dt)
        den = lr * lr + li * li
        nr = ab_re - 1.0
        ni = ab_im
        f_re = (nr * lr + ni * li) / den
        f_im = (ni * lr - nr * li) / den
        ab_ref[0:1, :] = ab_re
        ab_ref[1:2, :] = ab_im
        gpl = LANES // S5_STATE
        ch_g = lax.broadcasted_iota(I32, (S5_WIDTH, LANES), 0) // S5_GROUP_CH
        lane_g = lax.broadcasted_iota(I32, (S5_WIDTH, LANES), 1) // S5_STATE
        bre2 = jnp.concatenate([btre_ref[...]] * gpl, axis=1)
        bim2 = jnp.concatenate([btim_ref[...]] * gpl, axis=1)
        for j in range(p2 // LANES):
            cols = slice(j * LANES, (j + 1) * LANES)
            own = ch_g == gpl * j + lane_g
            bre = jnp.where(own, bre2, 0.0)
            bim = jnp.where(own, bim2, 0.0)
            bw_ref[:, cols] = (bre * f_re[:, cols] - bim * f_im[:, cols]).astype(BF16)
            bw_ref[:, p2 + j * LANES:p2 + (j + 1) * LANES] = (bim * f_re[:, cols] + bre * f_im[:, cols]).astype(BF16)
        cpl = LANES // S5_GROUP_CH
        st_g = lax.broadcasted_iota(I32, (p2, LANES), 0) // S5_STATE
        lane_cg = lax.broadcasted_iota(I32, (p2, LANES), 1) // S5_GROUP_CH
        for j in range(S5_WIDTH // LANES):
            cols = slice(j * LANES, (j + 1) * LANES)
            own = st_g == cpl * j + lane_cg
            c_ref[0:p2, cols] = jnp.where(own, ctre_ref[...], 0.0).astype(BF16)
            c_ref[p2:2 * p2, cols] = jnp.where(own, -ctim_ref[...], 0.0).astype(BF16)
        h_ref[...] = h0_ref[...]

    nck = S5_WIDTH // LANES
    if wide:
        for b in range(nb):
            for ck in range(nck):
                lo = b * S5_WIDTH + ck * LANES
                ru_ref[ck, pl.ds(b, tt, stride=nb), :] = u_ref[:, lo:lo + LANES].astype(F32)
        u = jnp.concatenate([ru_ref[ck] for ck in range(nck)], axis=1)
    else:
        u = u_ref[...].astype(F32)
    ub = u.astype(BF16)
    cw = S5_WIDTH // S5_SUPER
    sw = S5_FLAT // S5_SUPER
    for part in (0, p2):
        for b in range(S5_SUPER):
            x_ref[:, part + b * sw:part + (b + 1) * sw] = jnp.dot(
                ub[:, b * cw:(b + 1) * cw], bw_ref[b * cw:(b + 1) * cw, part + b * sw:part + (b + 1) * sw],
                preferred_element_type=F32)
    a_re = ab_ref[0:1, :]
    a_im = ab_ref[1:2, :]

    if nb == SUBLANES:
        wsl = p2 // S5_SCAN_SPLIT
        for sp in range(S5_SCAN_SPLIT):
            c0 = sp * wsl
            are = jnp.broadcast_to(a_re[:, c0:c0 + wsl], (nb, wsl))
            aim = jnp.broadcast_to(a_im[:, c0:c0 + wsl], (nb, wsl))

            def step(t, carry, c0=c0, are=are, aim=aim):
                hr, hi = carry
                r0 = pl.multiple_of(t * nb, nb)
                nr = are * hr - aim * hi + x_ref[pl.ds(r0, nb), c0:c0 + wsl]
                ni = are * hi + aim * hr + x_ref[pl.ds(r0, nb), p2 + c0:p2 + c0 + wsl]
                x_ref[pl.ds(r0, nb), c0:c0 + wsl] = nr
                x_ref[pl.ds(r0, nb), p2 + c0:p2 + c0 + wsl] = ni
                return nr, ni

            hr, hi = lax.fori_loop(0, tt, step, (h_ref[:, c0:c0 + wsl], h_ref[:, p2 + c0:p2 + c0 + wsl]),
                                   unroll=4)
            h_ref[:, c0:c0 + wsl] = hr
            h_ref[:, p2 + c0:p2 + c0 + wsl] = hi
    else:
        for t in range(tt):
            rs = slice(t * nb, (t + 1) * nb)
            hr = h_ref[:, 0:p2]
            hi = h_ref[:, p2:2 * p2]
            nr = a_re * hr - a_im * hi + x_ref[rs, 0:p2]
            ni = a_re * hi + a_im * hr + x_ref[rs, p2:2 * p2]
            h_ref[:, 0:p2] = nr
            h_ref[:, p2:2 * p2] = ni
            x_ref[rs, 0:p2] = nr
            x_ref[rs, p2:2 * p2] = ni

    for b in range(S5_SUPER):
        cols = slice(b * cw, (b + 1) * cw)
        y = None
        for part in (0, p2):
            rws = slice(part + b * sw, part + (b + 1) * sw)
            term = jnp.dot(x_ref[:, rws].astype(BF16), c_ref[rws, cols], preferred_element_type=F32)
            y = term if y is None else y + term
        if wide:
            skip = jnp.concatenate([ru_ref[b * (cw // LANES) + ck] for ck in range(cw // LANES)], axis=1)
        else:
            skip = u[:, cols]
        y = y + d_ref[:, cols] * skip
        if wide:
            for ck in range(cw // LANES):
                ry_ref[b * (cw // LANES) + ck] = y[:, ck * LANES:(ck + 1) * LANES]
        else:
            y_ref[:, cols] = y
    if wide:
        for b in range(nb):
            for ck in range(nck):
                lo = b * S5_WIDTH + ck * LANES
                y_ref[:, lo:lo + LANES] = ry_ref[ck, pl.ds(b, tt, stride=nb), :]
    hfin_ref[...] = h_ref[...]


def _s5(u, params, h0, nb, wide):
    btre, btim, lam, ctre, ctim, dvec = params
    t = u.shape[0] if wide else u.shape[0] // nb
    tt = min(S5_TILE_ROWS // nb, t)
    rows = tt * nb
    const = lambda i: (0, 0)
    kern = functools.partial(_s5_kernel, nb=nb, tt=tt, wide=wide)
    io_block = (tt, nb * S5_WIDTH) if wide else (rows, S5_WIDTH)
    return pl.pallas_call(
        kern,
        grid=(t // tt,),
        in_specs=[pl.BlockSpec(io_block, lambda i: (i, 0)),
                  pl.BlockSpec((S5_WIDTH, S5_STATE), const),
                  pl.BlockSpec((S5_WIDTH, S5_STATE), const),
                  pl.BlockSpec((SUBLANES, S5_FLAT), const),
                  pl.BlockSpec((S5_FLAT, LANES), const),
                  pl.BlockSpec((S5_FLAT, LANES), const),
                  pl.BlockSpec((1, S5_WIDTH), const),
                  pl.BlockSpec((nb, 2 * S5_FLAT), const)],
        out_specs=[pl.BlockSpec(io_block, lambda i: (i, 0)),
                   pl.BlockSpec((nb, 2 * S5_FLAT), const)],
        out_shape=[jax.ShapeDtypeStruct(u.shape, F32),
                   jax.ShapeDtypeStruct((nb, 2 * S5_FLAT), F32)],
        scratch_shapes=[pltpu.VMEM((S5_WIDTH, 2 * S5_FLAT), BF16),
                        pltpu.VMEM((2 * S5_FLAT, S5_WIDTH), BF16),
                        pltpu.VMEM((SUBLANES, S5_FLAT), F32),
                        pltpu.VMEM((rows, 2 * S5_FLAT), F32),
                        pltpu.VMEM((nb, 2 * S5_FLAT), F32),
                        pltpu.VMEM((S5_WIDTH // LANES, rows, LANES), F32),
                        pltpu.VMEM((S5_WIDTH // LANES, rows, LANES), F32)],
        compiler_params=_cparams(("arbitrary",)),
        name="s5",
    )(u, btre, btim, lam, ctre, ctim, dvec, h0)


def _postmix_kernel(xp_ref, op_ref, zp_ref, ysp_ref, gap_ref, gbp_ref,
                    xs_ref, os_ref, zs_ref, yss_ref, gas_ref, gbs_ref, *rest, nblk_p, range_tok):
    carry_ref = rest[-1]

    @pl.when(pl.program_id(0) == 0)
    def _():
        carry_ref[...] = jnp.zeros_like(carry_ref)

    @pl.when(pl.program_id(0) < nblk_p)
    def _():
        _postmix_body(xp_ref, op_ref, zp_ref, ysp_ref, gap_ref, gbp_ref, *rest, range_tok=range_tok)

    @pl.when(pl.program_id(0) >= nblk_p)
    def _():
        _postmix_body(xs_ref, os_ref, zs_ref, yss_ref, gas_ref, gbs_ref, *rest, range_tok=range_tok)


def _postmix_body(x_ref, o_ref, z_ref, ys_ref, ga_ref, gb_ref, hw_ref, seg_ref, wa_ref, wglu_ref, wb_ref,
                  wo_ref, nf_ref, wr_ref, su_ref, x1_ref, hn_ref, bkt_ref, rank_ref, rw_ref, cnt_ref, carry_ref,
                  *, range_tok):
    rows = x_ref.shape[0]
    pr = rows // POSTMIX_PARTS
    parts = [_postmix_part(p, pr, x_ref, o_ref, z_ref, ys_ref, ga_ref, gb_ref, hw_ref, seg_ref, wa_ref, wglu_ref,
                           wb_ref, wo_ref, nf_ref, wr_ref, su_ref, x1_ref, hn_ref, bkt_ref, rank_ref, rw_ref,
                           carry_ref, range_tok) for p in range(POSTMIX_PARTS)]
    live = []
    while live or parts:
        if parts:
            live.append(parts.pop(0))
        live = [g for g in live if next(g, StopIteration) is not StopIteration]
    cnt_ref[...] = carry_ref[...]


def _postmix_part(part, pr, x_ref, o_ref, z_ref, ys_ref, ga_ref, gb_ref, hw_ref, seg_ref, wa_ref, wglu_ref, wb_ref,
                  wo_ref, nf_ref, wr_ref, su_ref, x1_ref, hn_ref, bkt_ref, rank_ref, rw_ref, carry_ref, range_tok):
    rs = slice(part * pr, (part + 1) * pr)
    o = o_ref[rs, :]
    ms = jnp.dot((o * o).astype(BF16), seg_ref[...], preferred_element_type=F32) * (1.0 / DN_HEAD_DIM)
    on = o * lax.rsqrt(ms + RMS_EPS) * hw_ref[...]
    z = z_ref[rs, :]
    oa = on * (z * _sigmoid(z)).astype(F32)
    yield
    y_a = _mm(oa, wa_ref[...])
    ys = jax.nn.gelu(ys_ref[rs, :])
    yield
    ys = ys * _sigmoid(_mm(ys, wglu_ref[...]))
    yield
    y_b = _mm(ys, wb_ref[...])
    mixed = _sigmoid(ga_ref[rs, :]).astype(F32) * y_a + _sigmoid(gb_ref[rs, :]).astype(F32) * y_b
    yield
    x1 = x_ref[rs, :] + _mm(mixed, wo_ref[...])
    x1_ref[rs, :] = x1
    hn = x1 * lax.rsqrt(jnp.mean(x1 * x1, axis=-1, keepdims=True) + RMS_EPS) * nf_ref[...]
    _slab_store(hn_ref, hn, part * pr)
    yield

    wr = wr_ref[...]
    w_hi = wr.astype(BF16)
    w_lo = (wr - w_hi.astype(F32)).astype(BF16)
    hn_hi = hn.astype(BF16)
    hn_lo = (hn - hn_hi.astype(F32)).astype(BF16)
    both = _mm_nt(jnp.concatenate([w_hi, w_lo], axis=0), hn_hi)
    logits = both[:ROUTER_ROWS] + both[ROUTER_ROWS:] + _mm_nt(w_hi, hn_lo)
    yield
    coarse = logits[N_EXPERTS:N_EXPERTS + MOE_GROUPS, :]
    cm = jnp.max(coarse, axis=0, keepdims=True)
    ce = jnp.exp(coarse - cm)
    pc = ce / jnp.sum(ce, axis=0, keepdims=True)
    p_sel = jnp.max(pc, axis=0, keepdims=True)
    gi = lax.broadcasted_iota(I32, pc.shape, 0)
    g_sel = jnp.min(jnp.where(pc == p_sel, gi, MOE_GROUPS), axis=0, keepdims=True)
    fine = jnp.zeros((EXPERTS_PER_GROUP, logits.shape[1]), F32)
    for g in range(MOE_GROUPS):
        fine = fine + jnp.where(g_sel == g, logits[g * EXPERTS_PER_GROUP:(g + 1) * EXPERTS_PER_GROUP, :], 0.0)
    fm = jnp.max(fine, axis=0, keepdims=True)
    fe = jnp.exp(fine - fm)
    pf = fe / jnp.sum(fe, axis=0, keepdims=True)
    ei = lax.broadcasted_iota(I32, pf.shape, 0)
    v1 = jnp.max(pf, axis=0, keepdims=True)
    i1 = jnp.min(jnp.where(pf == v1, ei, EXPERTS_PER_GROUP), axis=0, keepdims=True)
    rest = jnp.where(ei == i1, -1.0, pf)
    v2 = jnp.max(rest, axis=0, keepdims=True)
    i2 = jnp.min(jnp.where(rest == v2, ei, EXPERTS_PER_GROUP), axis=0, keepdims=True)
    tot = v1 + v2
    rw_ref[0:1, rs] = v1 / tot * p_sel
    rw_ref[1:2, rs] = v2 / tot * p_sel

    tok = pl.program_id(0) * (pr * POSTMIX_PARTS) + part * pr + lax.broadcasted_iota(I32, (1, pr), 1)
    ph = jnp.zeros((1, pr), I32)
    for r in range(1, MOE_PHASES):
        ph = ph + (tok >= r * range_tok).astype(I32)
    bsel = [ph * N_EXPERTS + g_sel * EXPERTS_PER_GROUP + ix for ix in (i1, i2)]
    bi = lax.broadcasted_iota(I32, (MOE_PHASES * N_EXPERTS, pr), 0)
    onehot = [(bi == b).astype(F32) for b in bsel]
    cnt = onehot[0] + onehot[1]
    before = carry_ref[:, 0:1] + jnp.dot(cnt.astype(BF16), su_ref[0:pr, 0:pr], preferred_element_type=F32)
    for s in range(TOP_K):
        bkt_ref[s:s + 1, rs] = bsel[s]
        rank_ref[s:s + 1, rs] = jnp.sum(onehot[s] * before, axis=0, keepdims=True).astype(I32)
    carry_ref[...] = carry_ref[...] + jnp.sum(cnt, axis=1, keepdims=True)


def _postmix(prompt, sample, weights, nb):
    n_p = prompt[0].shape[0]
    n_s = sample[0].shape[0]
    t = n_p // nb
    tt = min(ROW_TILE, t, n_s)
    nt = t // tt
    nblk_p = n_p // tt
    nblk = nblk_p + n_s // tt
    n_total = n_p + n_s
    prow = lambda i: (jnp.minimum(i, nblk_p - 1), 0)
    pys = lambda i: (jnp.minimum(i, nblk_p - 1) % nt, jnp.minimum(i, nblk_p - 1) // nt)
    srow = lambda i: (jnp.maximum(i - nblk_p, 0), 0)
    const = lambda i: (0, 0)

    def stream_specs(row, ysmap):
        return [pl.BlockSpec((tt, D_MODEL), row),
                pl.BlockSpec((tt, DN_WIDTH), row),
                pl.BlockSpec((tt, DN_WIDTH), row),
                pl.BlockSpec((tt, S5_WIDTH), ysmap),
                pl.BlockSpec((tt, D_MODEL), row),
                pl.BlockSpec((tt, D_MODEL), row)]

    weight_specs = [pl.BlockSpec((1, DN_WIDTH), const),
                    pl.BlockSpec((DN_WIDTH, DN_WIDTH), const),
                    pl.BlockSpec((DN_WIDTH, D_MODEL), const),
                    pl.BlockSpec((S5_WIDTH, S5_WIDTH), const),
                    pl.BlockSpec((S5_WIDTH, D_MODEL), const),
                    pl.BlockSpec((D_MODEL, D_MODEL), const),
                    pl.BlockSpec((1, D_MODEL), const),
                    pl.BlockSpec((ROUTER_ROWS, D_MODEL), const),
                    pl.BlockSpec((tt, tt), const)]
    xp, op, zp, ysp, gap, gbp = prompt
    nbk = MOE_PHASES * N_EXPERTS
    earlier = jnp.triu(jnp.ones((tt, tt), F32), k=1).astype(BF16)
    return pl.pallas_call(
        functools.partial(_postmix_kernel, nblk_p=nblk_p, range_tok=n_total // MOE_PHASES),
        grid=(nblk,),
        in_specs=stream_specs(prow, pys) + stream_specs(srow, srow) + weight_specs,
        out_specs=[pl.BlockSpec((tt, D_MODEL), lambda i: (i, 0)),
                   pl.BlockSpec((tt * ROW_SLAB, LANES), lambda i: (i, 0)),
                   pl.BlockSpec((TOP_K, tt), lambda i: (0, i)),
                   pl.BlockSpec((TOP_K, tt), lambda i: (0, i)),
                   pl.BlockSpec((TOP_K, tt), lambda i: (0, i)),
                   pl.BlockSpec((nbk, LANES), const)],
        out_shape=[jax.ShapeDtypeStruct((n_total, D_MODEL), F32),
                   jax.ShapeDtypeStruct((n_total * ROW_SLAB, LANES), F32),
                   jax.ShapeDtypeStruct((TOP_K, n_total), I32),
                   jax.ShapeDtypeStruct((TOP_K, n_total), I32),
                   jax.ShapeDtypeStruct((TOP_K, n_total), F32),
                   jax.ShapeDtypeStruct((nbk, LANES), F32)],
        scratch_shapes=[pltpu.VMEM((nbk, LANES), F32)],
        compiler_params=_cparams(("arbitrary",)),
        name="postmix",
    )(xp, op, zp, ysp, gap, gbp, *sample, *weights, earlier)


def _wait_slabs(buf, sem):
    pltpu.make_async_copy(buf, buf, sem).wait()


def _moe_kernel(texp_ref, tph_ref, tsrc_ref, tnv_ref, tfirst_ref, tslot_ref, tnext_ref, otok_ref,
                hn_hbm, wu_hbm, wd_hbm, y_ref, hnv, xbuf0, xbuf1, wu_buf, wd_buf, wub, wdb, sem, wsem):
    i = pl.program_id(0)
    tm = MOE_TILE
    rs = ROW_SLAB
    nv = tnv_ref[i]
    ph = tph_ref[i]
    range_rows = hnv.shape[0]

    def weight_copies(e, sl):
        return (pltpu.make_async_copy(wu_hbm.at[e], wu_buf.at[sl], wsem.at[sl]),
                pltpu.make_async_copy(wd_hbm.at[e], wd_buf.at[sl], wsem.at[sl]))

    @pl.when(i == 0)
    def _():
        for p, c in enumerate(weight_copies(texp_ref[0], 0)):
            c.start(priority=p % DMA_QUEUES)

    @pl.when(jnp.logical_and(nv > 0, jnp.logical_or(i == 0, ph != tph_ref[jnp.maximum(i - 1, 0)])))
    def _():
        piece = range_rows // DMA_QUEUES
        loads = [pltpu.make_async_copy(hn_hbm.at[pl.ds(pl.multiple_of(ph * range_rows + p * piece, rs), piece), :],
                                       hnv.at[pl.ds(p * piece, piece), :], sem) for p in range(DMA_QUEUES)]
        for p, c in enumerate(loads):
            c.start(priority=p)
        for c in loads:
            c.wait()

    xbuf = (xbuf0, xbuf1)
    parity = lax.rem(i, 2)

    def gather(step, buf):
        src0 = tsrc_ref[step]
        for r in range(tm):
            tok8 = pl.multiple_of(otok_ref[src0 + r], rs)
            buf[pl.ds(r * rs, rs), :] = hnv[pl.ds(tok8, rs), :]

    for par in range(2):
        @pl.when(jnp.logical_and(parity == par, jnp.logical_and(
            nv > 0, jnp.logical_or(i == 0, ph != tph_ref[jnp.maximum(i - 1, 0)]))))
        def _(par=par):
            gather(i, xbuf[par])

    for sl in range(2):
        @pl.when(jnp.logical_and(jnp.logical_and(nv > 0, tfirst_ref[i] == 1), tslot_ref[i] == sl))
        def _():
            for c in weight_copies(texp_ref[i], sl):
                c.wait()

            @pl.when(tnext_ref[i] >= 0)
            def _():
                for p, c in enumerate(weight_copies(tnext_ref[i], 1 - sl)):
                    c.start(priority=p % DMA_QUEUES)

            wub[...] = wu_buf[sl].astype(BF16)
            wdb[...] = wd_buf[sl].astype(BF16)

    @pl.when(nv == 0)
    def _():
        y_ref[...] = jnp.zeros_like(y_ref)

    for par in range(2):
        @pl.when(jnp.logical_and(parity == par, nv > 0))
        def _(par=par):
            gather(jnp.minimum(i + 1, pl.num_programs(0) - 1), xbuf[1 - par])
            x = _slab_load(xbuf[par], tm).astype(BF16)
            hu = jnp.dot(x, wub[...], preferred_element_type=F32)
            gate = hu[:, :EXPERT_FF]
            up = hu[:, EXPERT_FF:]
            act = gate * _sigmoid(gate) * up
            _slab_store(y_ref, jnp.dot(act.astype(BF16), wdb[...], preferred_element_type=F32))


def _moe(hn, w_up, w_down, plan):
    ntiles = plan[0].shape[0]
    grid_spec = pltpu.PrefetchScalarGridSpec(
        num_scalar_prefetch=len(plan),
        grid=(ntiles,),
        in_specs=[pl.BlockSpec(memory_space=pl.ANY),
                  pl.BlockSpec(memory_space=pl.ANY),
                  pl.BlockSpec(memory_space=pl.ANY)],
        out_specs=pl.BlockSpec((MOE_TILE * ROW_SLAB, LANES), lambda i, *_: (i, 0)),
        scratch_shapes=[pltpu.VMEM((hn.shape[0] // MOE_PHASES, LANES), F32),
                        pltpu.VMEM((MOE_TILE * ROW_SLAB, LANES), F32),
                        pltpu.VMEM((MOE_TILE * ROW_SLAB, LANES), F32),
                        pltpu.VMEM((2, D_MODEL, 2 * EXPERT_FF), w_up.dtype),
                        pltpu.VMEM((2, EXPERT_FF, D_MODEL), w_down.dtype),
                        pltpu.VMEM((D_MODEL, 2 * EXPERT_FF), BF16),
                        pltpu.VMEM((EXPERT_FF, D_MODEL), BF16),
                        pltpu.SemaphoreType.DMA,
                        pltpu.SemaphoreType.DMA((2,))])
    return pl.pallas_call(
        _moe_kernel,
        grid_spec=grid_spec,
        out_shape=jax.ShapeDtypeStruct((ntiles * MOE_TILE * ROW_SLAB, LANES), F32),
        compiler_params=_cparams(("arbitrary",)),
        name="moe",
    )(*plan, hn, w_up, w_down)


def _combine_kernel(pos_ref, x1_ref, ys_hbm, w_ref, nw_ref, outp_ref, outs_ref,
                    ybuf0, ybuf1, sem, *, nblk_p, n_tok):
    i = pl.program_id(0)
    nsteps = pl.num_programs(0)
    tt = x1_ref.shape[0]
    rs = ROW_SLAB
    slot = lax.rem(i, 2)
    ybuf = (ybuf0, ybuf1)

    def start_gather(step, sl):
        base = step * tt
        for r in range(tt * TOP_K):
            j, s = divmod(r, TOP_K)
            p8 = pl.multiple_of(pos_ref[s * n_tok + base + j], rs)
            pltpu.make_async_copy(ys_hbm.at[pl.ds(p8, rs), :], ybuf[sl].at[pl.ds((s * tt + j) * rs, rs), :],
                                  sem.at[sl]).start(priority=r % DMA_QUEUES)

    @pl.when(i == 0)
    def _():
        start_gather(0, 0)

    for sl in range(2):
        @pl.when(slot == sl)
        def _():
            _wait_slabs(ybuf[sl], sem.at[sl])
            start_gather(jnp.minimum(i + 1, nsteps - 1), 1 - sl)
            w = w_ref[...]
            y0 = _slab_load(ybuf[sl], tt, 0)
            y1 = _slab_load(ybuf[sl], tt, tt * rs)
            x = x1_ref[...] + w[:, 0:1] * y0 + w[:, 1:2] * y1
            res = x * lax.rsqrt(jnp.mean(x * x, axis=-1, keepdims=True) + RMS_EPS) * nw_ref[...]

            @pl.when(i < nblk_p)
            def _():
                outp_ref[...] = res

            @pl.when(i >= nblk_p)
            def _():
                outs_ref[...] = res

        @pl.when(jnp.logical_and(slot == sl, i == nsteps - 1))
        def _():
            _wait_slabs(ybuf[1 - sl], sem.at[1 - sl])


def _combine(x1, ysorted, pos8, wtok, nw, n_p):
    n = x1.shape[0]
    tt = math.gcd(math.gcd(n_p, n - n_p), COMBINE_TILE)
    nblk_p = n_p // tt
    grid_spec = pltpu.PrefetchScalarGridSpec(
        num_scalar_prefetch=1,
        grid=(n // tt,),
        in_specs=[pl.BlockSpec((tt, D_MODEL), lambda i, *_: (i, 0)),
                  pl.BlockSpec(memory_space=pl.ANY),
                  pl.BlockSpec((tt, TOP_K), lambda i, *_: (i, 0)),
                  pl.BlockSpec((1, D_MODEL), lambda i, *_: (0, 0))],
        out_specs=[pl.BlockSpec((tt, D_MODEL), lambda i, *_: (jnp.minimum(i, nblk_p - 1), 0)),
                   pl.BlockSpec((tt, D_MODEL), lambda i, *_: (jnp.maximum(i - nblk_p, 0), 0))],
        scratch_shapes=[pltpu.VMEM((tt * TOP_K * ROW_SLAB, LANES), F32),
                        pltpu.VMEM((tt * TOP_K * ROW_SLAB, LANES), F32),
                        pltpu.SemaphoreType.DMA((2,))])
    return pl.pallas_call(
        functools.partial(_combine_kernel, nblk_p=nblk_p, n_tok=n),
        grid_spec=grid_spec,
        out_shape=[jax.ShapeDtypeStruct((n_p, D_MODEL), F32),
                   jax.ShapeDtypeStruct((n - n_p, D_MODEL), F32)],
        compiler_params=_cparams(("arbitrary",)),
        name="combine",
    )(pos8, x1, ysorted, wtok, nw)


def _route_plan(bkt, rank, cnt, n_tok):
    tm = MOE_TILE
    n_assign = n_tok * TOP_K
    nbk = MOE_PHASES * N_EXPERTS
    ntiles = n_assign // tm + nbk
    range_tok = n_tok // MOE_PHASES
    b_flat = bkt.T.reshape(n_assign)
    order = jnp.argsort(b_flat, stable=True).astype(I32)
    counts = cnt[:, 0].astype(I32)
    cstart = jnp.cumsum(counts) - counts
    tiles_b = (counts + tm - 1) // tm
    tend = jnp.cumsum(tiles_b)
    tstart = tend - tiles_b
    tile_id = jnp.arange(ntiles, dtype=I32)
    tbk = jnp.minimum(jnp.sum((tile_id[:, None] >= tend[None, :]).astype(I32), axis=1), nbk - 1)
    onehot = (tbk[:, None] == jnp.arange(nbk, dtype=I32)[None, :]).astype(I32)
    pick = lambda v: jnp.sum(onehot * v[None, :], axis=1)
    done = (tile_id - pick(tstart)) * tm
    tnv = jnp.where(tile_id < tend[-1], jnp.clip(pick(counts) - done, 0, tm), 0)
    tsrc = jnp.where(tnv > 0, pick(cstart) + done, 0)
    texp = tbk % N_EXPERTS
    tph = tbk // N_EXPERTS
    nonempty = counts > 0
    bslot = (jnp.cumsum(nonempty.astype(I32)) - 1) % 2
    bidx = jnp.where(nonempty, jnp.arange(nbk, dtype=I32), nbk)
    nxt = jnp.concatenate([lax.cummin(bidx[::-1])[::-1][1:], jnp.full((1,), nbk, I32)])
    bnext = jnp.where(nxt < nbk, nxt % N_EXPERTS, -1)
    tfirst = jnp.logical_and(tnv > 0, done == 0).astype(I32)
    tslot = pick(bslot)
    tnext = pick(bnext)
    otok8 = jnp.concatenate([((order // TOP_K) % range_tok) * ROW_SLAB, jnp.zeros((tm,), I32)])
    plan = tuple(a.astype(I32) for a in (texp, tph, tsrc, tnv, tfirst, tslot, tnext, otok8))
    first = jnp.sum((bkt[:, :, None] == jnp.arange(nbk, dtype=I32)[None, None, :]).astype(I32)
                    * (tstart * tm)[None, None, :], axis=2)
    pos8 = ((first + rank) * ROW_SLAB).reshape(n_assign)
    return plan, pos8.astype(I32)


def _block_diag(m):
    g, a, b = m.shape
    eye = jnp.eye(g, dtype=m.dtype)
    return (eye[:, None, :, None] * m[:, :, None, :]).reshape(g * a, g * b)


def kernel(x_prompt, x_sample, state_conv, state_delta, state_ssm_re, state_ssm_im, norm_mix_w, w_in, conv_w, a_log, dt_bias, head_norm_w, w_a_up, s5_lambda_re, s5_lambda_im, s5_log_step, s5_b_re, s5_b_im, s5_c_re, s5_c_im, s5_d, w_glu, w_b_up, w_o, norm_ffn_w, w_router_coarse, w_router_fine, w_expert_up, w_expert_down, norm_final_w):
    bp, tp, _ = x_prompt.shape
    bs, ts, _ = x_sample.shape
    n_p = bp * tp
    n_s = bs * ts
    n_tok = n_p + n_s
    l = 0

    w = w_in[l].astype(BF16)
    c_ab = W1_COLS + 2 * DN_HEADS
    w_parts = (w[:, :W1_COLS], w[:, c_ab:],
               jnp.concatenate([w[:, W1_COLS:c_ab], jnp.zeros((D_MODEL, LANES - 2 * DN_HEADS), BF16)], axis=1))
    nw_mix = norm_mix_w[l].reshape(1, D_MODEL)
    pad8 = lambda v: jnp.concatenate([v, jnp.zeros((LANES - DN_HEADS,), F32)]).reshape(1, LANES)
    gate_p = jnp.concatenate([pad8(a_log[l]), pad8(dt_bias[l])], axis=0)
    seg = _block_diag(jnp.ones((DN_HEADS, DN_HEAD_DIM, DN_HEAD_DIM), BF16))
    chan_rows = lambda b: jnp.swapaxes(b, 1, 2).reshape(S5_WIDTH, S5_STATE)
    state_rows = lambda c: jnp.tile(jnp.swapaxes(c, 1, 2).reshape(S5_FLAT, S5_GROUP_CH),
                                    (1, LANES // S5_GROUP_CH))
    lam = jnp.concatenate([s5_lambda_re[l].reshape(1, S5_FLAT), s5_lambda_im[l].reshape(1, S5_FLAT),
                           jnp.repeat(s5_log_step[l], S5_STATE).reshape(1, S5_FLAT),
                           jnp.zeros((SUBLANES - 3, S5_FLAT), F32)], axis=0)
    s5_params = (chan_rows(s5_b_re[l]), chan_rows(s5_b_im[l]), lam,
                 state_rows(s5_c_re[l]), state_rows(s5_c_im[l]), s5_d[l].reshape(1, S5_WIDTH))
    hw = jnp.tile(head_norm_w[l], DN_HEADS).reshape(1, DN_WIDTH)
    wr = jnp.concatenate([w_router_fine[l].T, w_router_coarse[l].T,
                          jnp.zeros((ROUTER_ROWS - N_EXPERTS - MOE_GROUPS, D_MODEL), F32)], axis=0)
    pm_weights = (hw, seg, w_a_up[l].astype(BF16), w_glu[l].astype(BF16), w_b_up[l].astype(BF16),
                  w_o[l].astype(BF16), norm_ffn_w[l].reshape(1, D_MODEL), wr)

    xp2 = x_prompt.reshape(n_p, D_MODEL)
    q_p, k_p, v_p, gates_p, conv_p, z_p, u_p, ga_p, gb_p = _inprep(
        xp2, nw_mix, w_parts, jnp.zeros((bp, SUBLANES, QKV_DIM), F32), conv_w[l], gate_p, seg, bp, 1)
    o_p, delta_p = _delta_prompt(q_p, k_p, v_p, gates_p, bp)
    ys_p, h_p = _s5(u_p, s5_params, jnp.zeros((bp, 2 * S5_FLAT), F32), bp, True)

    xs2 = jnp.swapaxes(x_sample, 0, 1).reshape(n_s, D_MODEL)
    cinit_s = jnp.swapaxes(state_conv[l], 0, 1).reshape(1, (CONV_W - 1) * bs, QKV_DIM)
    q_s, k_s, v_s, gate_s, conv_s, z_s, u_s, ga_s, gb_s = _inprep(
        xs2, nw_mix, w_parts, cinit_s, conv_w[l], gate_p, seg, 1, bs)
    s0t = jnp.transpose(state_delta[l], (1, 2, 3, 0)).reshape(DN_HEADS * DN_HEAD_DIM * DN_HEAD_DIM, bs)
    o_s, delta_st = _delta_sample(q_s, k_s, v_s, gate_s, s0t, bs, ts)
    delta_s = jnp.transpose(delta_st.reshape(DN_HEADS, DN_HEAD_DIM, DN_HEAD_DIM, bs), (3, 0, 1, 2))
    h0_s = jnp.concatenate([state_ssm_re[l].reshape(bs, S5_FLAT), state_ssm_im[l].reshape(bs, S5_FLAT)], axis=1)
    ys_s, h_s = _s5(u_s, s5_params, h0_s, bs, False)
    x1, hn, bkt, rank, rw, cnt = _postmix((xp2, o_p, z_p, ys_p, ga_p, gb_p), (xs2, o_s, z_s, ys_s, ga_s, gb_s),
                                          pm_weights, bp)

    plan, pos8 = _route_plan(bkt, rank, cnt, n_tok)
    ysorted = _moe(hn, w_expert_up[l].astype(BF16), w_expert_down[l].astype(BF16), plan)
    y_p, y_s = _combine(x1, ysorted, pos8, rw.T, norm_final_w.reshape(1, D_MODEL), n_p)

    y_prompt = y_p.reshape(bp, tp, D_MODEL)
    y_sample = jnp.swapaxes(y_s.reshape(ts, bs, D_MODEL), 0, 1)
    conv_sample = jnp.swapaxes(conv_s.reshape(CONV_W - 1, bs, QKV_DIM), 0, 1)
    return (y_prompt, y_sample,
            conv_p[None], delta_p[None],
            h_p[:, :S5_FLAT].reshape(1, bp, S5_GROUPS, S5_STATE), h_p[:, S5_FLAT:].reshape(1, bp, S5_GROUPS, S5_STATE),
            conv_sample[None], delta_s[None],
            h_s[:, :S5_FLAT].reshape(1, bs, S5_GROUPS, S5_STATE), h_s[:, S5_FLAT:].reshape(1, bs, S5_GROUPS, S5_STATE))
```
